```python
import math
import jax
import jax.numpy as jnp
from jax import lax
import numpy as np

D_MODEL = 1024
BATCH = 8
SEQ = 4096
DEPTH = 2

GRID_W = 64
CTX_LEN = 256
EPS = 1e-6
NEG_INF = -1e30
ROPE_BASE = 10000.0
BLOCK = 128

N_BRANCHES = 4
BRANCH_WIDTH = D_MODEL // N_BRANCHES

S5_WIDTH = BRANCH_WIDTH
S5_GROUP = 16
S5_GROUPS = S5_WIDTH // S5_GROUP
S5_STATE = 64

HEAD_DIM = 64
NA_HEADS = BRANCH_WIDTH // HEAD_DIM
NA_WIN_ROWS = 8
NA_WIN_COLS = 16

SWA_Q_HEADS = BRANCH_WIDTH // HEAD_DIM
SWA_KV_HEADS = 2
SWA_WINDOW = 128

DIFF_HEADS = 4
DIFF_QK_DIM = BRANCH_WIDTH // (2 * DIFF_HEADS)
DIFF_V_DIM = BRANCH_WIDTH // DIFF_HEADS

N_EXPERTS = 64
N_EXPERT_GROUPS = 8
TOPK_GROUPS = 4
TOP_K = 6
EXPERT_HIDDEN = 256
SHARED_HIDDEN = 256
ROUTED_SCALE = 2.5
MOE_BLOCK = 128

IN_SIZES = (S5_WIDTH,
            NA_HEADS * HEAD_DIM, NA_HEADS * HEAD_DIM, NA_HEADS * HEAD_DIM,
            SWA_Q_HEADS * HEAD_DIM, SWA_KV_HEADS * HEAD_DIM, SWA_KV_HEADS * HEAD_DIM,
            DIFF_HEADS * 2 * DIFF_QK_DIM, DIFF_HEADS * 2 * DIFF_QK_DIM, DIFF_HEADS * DIFF_V_DIM,
            N_BRANCHES * D_MODEL)
IN_WIDTH = sum(IN_SIZES)

kernel_name = 'hybrid_parallel_diffusion_block'


def rms_norm(x, g):
    xf = x.astype(jnp.float32)
    y = xf * lax.rsqrt(jnp.mean(xf * xf, axis=-1, keepdims=True) + EPS)
    return y * g.astype(jnp.float32)


def softmax_f32(s):
    return jax.nn.softmax(s.astype(jnp.float32), axis=-1)


def axial_rope(x):
    L, dim = x.shape[1], x.shape[-1]
    quarter = dim // 4
    t = jnp.arange(L)
    rows = (t // GRID_W).astype(jnp.float32)
    cols = (t % GRID_W).astype(jnp.float32)
    inv_freq = ROPE_BASE ** (-jnp.arange(quarter, dtype=jnp.float32) / quarter)
    bshape = (1, L) + (1,) * (x.ndim - 3) + (quarter,)
    xr, xc = jnp.split(x.astype(jnp.float32), 2, axis=-1)

    def rot(hpart, pos):
        ang = (pos[:, None] * inv_freq[None, :]).reshape(bshape)
        cos, sin = jnp.cos(ang), jnp.sin(ang)
        h1, h2 = jnp.split(hpart, 2, axis=-1)
        return jnp.concatenate([h1 * cos - h2 * sin, h2 * cos + h1 * sin], axis=-1)

    return jnp.concatenate([rot(xr, rows), rot(xc, cols)], axis=-1).astype(x.dtype)


def ctx_attention(q, k, v):
    scale = q.shape[-1] ** -0.5
    s = jnp.einsum('bqhd,bkhd->bhqk', q, k).astype(jnp.float32) * scale
    return jnp.einsum('bhqk,bkhd->bqhd', softmax_f32(s), v)


def s5_discretize(lam_re, lam_im, log_step, b_re, b_im):
    lr = lam_re.astype(jnp.float32)
    li = lam_im.astype(jnp.float32)
    dt = jnp.exp(log_step.astype(jnp.float32))[:, None]
    mag = jnp.exp(lr * dt)
    a_re = mag * jnp.cos(li * dt)
    a_im = mag * jnp.sin(li * dt)
    nr, ni, den = a_re - 1.0, a_im, lr * lr + li * li
    k_re = ((nr * lr + ni * li) / den)[..., None]
    k_im = ((ni * lr - nr * li) / den)[..., None]
    br = b_re.astype(jnp.float32)
    bi = b_im.astype(jnp.float32)
    return a_re, a_im, k_re * br - k_im * bi, k_re * bi + k_im * br


def ssm_combine(e1, e2):
    a1r, a1i, b1r, b1i = e1
    a2r, a2i, b2r, b2i = e2
    return (a2r * a1r - a2i * a1i, a2r * a1i + a2i * a1r,
            a2r * b1r - a2i * b1i + b2r, a2r * b1i + a2i * b1r + b2i)


def s5_scan(u, a_re, a_im, bb_re, bb_im, h0, reverse):
    bu_re = jnp.einsum('bngc,gpc->bngp', u, bb_re)
    bu_im = jnp.einsum('bngc,gpc->bngp', u, bb_im)
    if h0 is not None:
        h0_re, h0_im = h0
        first = -1 if reverse else 0
        bu_re = bu_re.at[:, first].add(a_re * h0_re - a_im * h0_im)
        bu_im = bu_im.at[:, first].add(a_re * h0_im + a_im * h0_re)
    a_re_full = jnp.broadcast_to(a_re, bu_re.shape)
    a_im_full = jnp.broadcast_to(a_im, bu_im.shape)
    _, _, x_re, x_im = lax.associative_scan(ssm_combine, (a_re_full, a_im_full, bu_re, bu_im),
                                            reverse=reverse, axis=1)
    return x_re, x_im


def s5_readout(x_re, x_im, c_re, c_im):
    return jnp.einsum('bngp,gcp->bngc', x_re, c_re) - jnp.einsum('bngp,gcp->bngc', x_im, c_im)


def s5_branch(u_lat, u_ctx, lam_re, lam_im, log_step, b_re, b_im, c_re, c_im, d_skip, glu_w, glu_b,
              with_ctx_out):
    B, L, _ = u_lat.shape
    N = u_ctx.shape[1]
    ul = u_lat.astype(jnp.float32).reshape(B, L, S5_GROUPS, S5_GROUP)
    uc = u_ctx.astype(jnp.float32).reshape(B, N, S5_GROUPS, S5_GROUP)
    dsk = d_skip.astype(jnp.float32).reshape(S5_GROUPS, S5_GROUP)
    y_lat = ul * dsk
    y_ctx = uc * dsk if with_ctx_out else None
    for direction, reverse in ((0, False), (1, True)):
        a_re, a_im, bb_re, bb_im = s5_discretize(lam_re[direction], lam_im[direction], log_step[direction],
                                                 b_re[direction], b_im[direction])
        cr = c_re[direction].astype(jnp.float32)
        ci = c_im[direction].astype(jnp.float32)
        xc_re, xc_im = s5_scan(uc, a_re, a_im, bb_re, bb_im, None, reverse)
        last = 0 if reverse else -1
        xl_re, xl_im = s5_scan(ul, a_re, a_im, bb_re, bb_im, (xc_re[:, last], xc_im[:, last]), reverse)
        y_lat = y_lat + s5_readout(xl_re, xl_im, cr, ci)
        if with_ctx_out:
            y_ctx = y_ctx + s5_readout(xc_re, xc_im, cr, ci)

    def glu(y):
        a = jax.nn.gelu(y.reshape(y.shape[0], y.shape[1], S5_WIDTH))
        return a * jax.nn.sigmoid(a @ glu_w + glu_b)

    return glu(y_lat), (glu(y_ctx) if with_ctx_out else None)


def na_branch(q, k, v, q_c, k_c, v_c, rpb, with_ctx_out):
    B, L, H, dh = q.shape
    rows = L // GRID_W
    wr = min(NA_WIN_ROWS, rows)
    scale = dh ** -0.5
    qg = q.reshape(B, rows, GRID_W, H, dh)
    kg = k.reshape(B, rows, GRID_W, H, dh)
    vg = v.reshape(B, rows, GRID_W, H, dh)
    r = jnp.arange(rows)
    row_idx = jnp.clip(r - wr // 2, 0, rows - wr)[:, None] + jnp.arange(wr)[None, :]
    k_rows = kg[:, row_idx]
    v_rows = vg[:, row_idx]
    col = jnp.arange(GRID_W)
    col_start = jnp.clip(col - NA_WIN_COLS // 2, 0, GRID_W - NA_WIN_COLS)
    col_in = (col[None, :] >= col_start[:, None]) & (col[None, :] < col_start[:, None] + NA_WIN_COLS)
    row_bias_idx = (row_idx - r[:, None]) + (NA_WIN_ROWS - 1)
    col_bias_idx = jnp.clip(col[None, :] - col[:, None] + (NA_WIN_COLS - 1), 0, 2 * NA_WIN_COLS - 2)
    bias = rpb[:, row_bias_idx[:, None, :, None], col_bias_idx[None, :, None, :]].astype(jnp.float32)
    s_loc = jnp.einsum('brqhd,brjkhd->bhrqjk', qg, k_rows).astype(jnp.float32) * scale + bias
    s_loc = jnp.where(col_in[:, None, :], s_loc, NEG_INF)
    s_ctx = jnp.einsum('brqhd,bnhd->bhrqn', qg, k_c).astype(jnp.float32) * scale
    n_loc = wr * GRID_W
    p = softmax_f32(jnp.concatenate([s_loc.reshape(B, H, rows, GRID_W, n_loc), s_ctx], axis=-1))
    p_loc = p[..., :n_loc].reshape(B, H, rows, GRID_W, wr, GRID_W)
    o = (jnp.einsum('bhrqjk,brjkhd->brqhd', p_loc, v_rows)
         + jnp.einsum('bhrqn,bnhd->brqhd', p[..., n_loc:], v_c))
    o_ctx = ctx_attention(q_c, k_c, v_c).reshape(B, -1, H * dh) if with_ctx_out else None
    return o.reshape(B, L, H * dh), o_ctx


def swa_branch(q, k, v, q_c, k_c, v_c, sink, with_ctx_out):
    B, L, Hq, dh = q.shape
    Hkv = k.shape[2]
    G = Hq // Hkv
    N = k_c.shape[1]
    nb = L // BLOCK
    scale = dh ** -0.5
    sink_f = sink.astype(jnp.float32).reshape(Hkv, G)
    qb = q.reshape(B, nb, BLOCK, Hkv, G, dh)

    def band(t):
        tp = jnp.pad(t, ((0, 0), (BLOCK, BLOCK), (0, 0), (0, 0))).reshape(B, nb + 2, BLOCK, Hkv, dh)
        return jnp.concatenate([tp[:, :nb], tp[:, 1:nb + 1], tp[:, 2:]], axis=2)

    kb, vb = band(k), band(v)
    qpos = jnp.arange(nb)[:, None] * BLOCK + jnp.arange(BLOCK)[None, :]
    kpos = (jnp.arange(nb)[:, None] - 1) * BLOCK + jnp.arange(3 * BLOCK)[None, :]
    mask = ((jnp.abs(qpos[:, :, None] - kpos[:, None, :]) <= SWA_WINDOW)
            & (kpos[:, None, :] >= 0) & (kpos[:, None, :] < L))
    s_loc = jnp.einsum('bnqkgd,bnjkd->bkgnqj', qb, kb).astype(jnp.float32) * scale
    s_loc = jnp.where(mask, s_loc, NEG_INF)
    s_ctx = jnp.einsum('bnqkgd,bmkd->bkgnqm', qb, k_c).astype(jnp.float32) * scale
    sink_col = jnp.broadcast_to(sink_f.reshape(1, Hkv, G, 1, 1, 1), s_ctx.shape[:-1] + (1,))
    p = softmax_f32(jnp.concatenate([s_loc, s_ctx, sink_col], axis=-1))
    n_loc = 3 * BLOCK
    o = (jnp.einsum('bkgnqj,bnjkd->bnqkgd', p[..., :n_loc], vb)
         + jnp.einsum('bkgnqm,bmkd->bnqkgd', p[..., n_loc:n_loc + N], v_c))
    o_lat = o.reshape(B, L, Hq * dh)
    o_ctx = None
    if with_ctx_out:
        qcg = q_c.reshape(B, N, Hkv, G, dh)
        sc = jnp.einsum('bqkgd,bmkd->bkgqm', qcg, k_c).astype(jnp.float32) * scale
        sc_sink = jnp.broadcast_to(sink_f.reshape(1, Hkv, G, 1, 1), sc.shape[:-1] + (1,))
        pc = softmax_f32(jnp.concatenate([sc, sc_sink], axis=-1))
        o_ctx = jnp.einsum('bkgqm,bmkd->bqkgd', pc[..., :N], v_c).reshape(B, N, Hq * dh)
    return o_lat, o_ctx


def diff_branch(q, k, v, q_c, k_c, v_c, lq1, lk1, lq2, lk2, subln_g, lambda_init, with_ctx_out):
    B, L, H, _, dq = q.shape
    dv = v.shape[-1]
    scale = dq ** -0.5
    lam = (jnp.exp(jnp.sum(lq1.astype(jnp.float32) * lk1.astype(jnp.float32)))
           - jnp.exp(jnp.sum(lq2.astype(jnp.float32) * lk2.astype(jnp.float32))) + lambda_init)
    nb = L // BLOCK

    def block_fn(qb):
        s = jnp.concatenate([jnp.einsum('bqhmd,bkhmd->bhmqk', qb, k),
                             jnp.einsum('bqhmd,bkhmd->bhmqk', qb, k_c)], axis=-1).astype(jnp.float32) * scale
        p = softmax_f32(s)
        wgt = p[:, :, 0] - lam * p[:, :, 1]
        return (jnp.einsum('bhqk,bkhd->bqhd', wgt[..., :L], v)
                + jnp.einsum('bhqk,bkhd->bqhd', wgt[..., L:], v_c))

    qblocks = jnp.moveaxis(q.reshape(B, nb, BLOCK, H, 2, dq), 1, 0)
    o = jnp.moveaxis(lax.map(block_fn, qblocks), 0, 1).reshape(B, L, H, dv)
    o_lat = (rms_norm(o, subln_g) * (1.0 - lambda_init)).reshape(B, L, H * dv)
    o_ctx = None
    if with_ctx_out:
        sc = jnp.einsum('bqhmd,bkhmd->bhmqk', q_c, k_c).astype(jnp.float32) * scale
        pc = softmax_f32(sc)
        oc = jnp.einsum('bhqk,bkhd->bqhd', pc[:, :, 0] - lam * pc[:, :, 1], v_c)
        o_ctx = (rms_norm(oc, subln_g) * (1.0 - lambda_init)).reshape(B, -1, H * dv)
    return o_lat, o_ctx


def token_mixer(h, hc, lp, layer_idx, with_ctx_out):
    offs = np.cumsum(IN_SIZES)[:-1].tolist()
    (u, na_q, na_k, na_v, sw_q, sw_k, sw_v, df_q, df_k, df_v, gate) = jnp.split(h @ lp['w_in'], offs, axis=-1)
    (u_c, na_q_c, na_k_c, na_v_c, sw_q_c, sw_k_c, sw_v_c, df_q_c, df_k_c, df_v_c, gate_c) = jnp.split(
        hc @ lp['w_in'], offs, axis=-1)

    def hd(t, *shape):
        return t.reshape(t.shape[:2] + shape)

    y_a, y_a_c = s5_branch(u, u_c, lp['s5_lambda_re'], lp['s5_lambda_im'], lp['s5_log_step'],
                           lp['s5_b_re'], lp['s5_b_im'], lp['s5_c_re'], lp['s5_c_im'], lp['s5_d'],
                           lp['s5_glu_w'], lp['s5_glu_b'], with_ctx_out)
    y_b, y_b_c = na_branch(hd(na_q, NA_HEADS, HEAD_DIM), hd(na_k, NA_HEADS, HEAD_DIM), hd(na_v, NA_HEADS, HEAD_DIM),
                           hd(na_q_c, NA_HEADS, HEAD_DIM), hd(na_k_c, NA_HEADS, HEAD_DIM),
                           hd(na_v_c, NA_HEADS, HEAD_DIM), lp['na_rpb'], with_ctx_out)
    y_c, y_c_c = swa_branch(axial_rope(hd(sw_q, SWA_Q_HEADS, HEAD_DIM)), axial_rope(hd(sw_k, SWA_KV_HEADS, HEAD_DIM)),
                            hd(sw_v, SWA_KV_HEADS, HEAD_DIM), hd(sw_q_c, SWA_Q_HEADS, HEAD_DIM),
                            hd(sw_k_c, SWA_KV_HEADS, HEAD_DIM), hd(sw_v_c, SWA_KV_HEADS, HEAD_DIM),
                            lp['swa_sink'], with_ctx_out)
    y_d, y_d_c = diff_branch(axial_rope(hd(df_q, DIFF_HEADS, 2, DIFF_QK_DIM)),
                             axial_rope(hd(df_k, DIFF_HEADS, 2, DIFF_QK_DIM)), hd(df_v, DIFF_HEADS, DIFF_V_DIM),
                             hd(df_q_c, DIFF_HEADS, 2, DIFF_QK_DIM), hd(df_k_c, DIFF_HEADS, 2, DIFF_QK_DIM),
                             hd(df_v_c, DIFF_HEADS, DIFF_V_DIM), lp['diff_lq1'], lp['diff_lk1'], lp['diff_lq2'],
                             lp['diff_lk2'], lp['diff_subln_g'], 0.8 - 0.6 * math.exp(-0.3 * layer_idx),
                             with_ctx_out)

    def merge(branches, gate_pre):
        acc = 0.0
        for i, y in enumerate(branches):
            g = jax.nn.sigmoid(gate_pre[..., i * D_MODEL:(i + 1) * D_MODEL].astype(jnp.float32))
            acc = acc + g * (y @ lp['w_branch'][i])
        return acc @ lp['w_out']

    out = merge((y_a, y_b, y_c, y_d), gate)
    out_c = merge((y_a_c, y_b_c, y_c_c, y_d_c), gate_c) if with_ctx_out else None
    return out, out_c


def swiglu(x, wg, wu, wd):
    return (jax.nn.silu(x @ wg) * (x @ wu)) @ wd


def moe(h, router_w, router_b, w_gate, w_up, w_down, sh_gate, sh_up, sh_down):
    shape = h.shape
    D = shape[-1]
    x = h.reshape(-1, D)
    T = x.shape[0]
    scores = jax.nn.sigmoid((x @ router_w).astype(jnp.float32))
    biased = (scores + router_b.astype(jnp.float32)).reshape(T, N_EXPERT_GROUPS, -1)
    grp_score = jnp.sum(lax.top_k(biased, 2)[0], axis=-1)
    _, grp_idx = lax.top_k(grp_score, TOPK_GROUPS)
    grp_keep = jnp.sum(jax.nn.one_hot(grp_idx, N_EXPERT_GROUPS, dtype=jnp.float32), axis=1) > 0
    masked = jnp.where(grp_keep[:, :, None], biased, NEG_INF).reshape(T, N_EXPERTS)
    _, idx = lax.top_k(masked, TOP_K)
    sel = jnp.take_along_axis(scores, idx, axis=-1)
    wts = sel / jnp.sum(sel, axis=-1, keepdims=True) * ROUTED_SCALE
    n_assign = T * TOP_K
    flat_e = idx.reshape(-1)
    flat_tok = jnp.repeat(jnp.arange(T, dtype=jnp.int32), TOP_K)
    flat_w = wts.reshape(-1)
    order = jnp.argsort(flat_e)
    e_s, tok_s, w_s = flat_e[order], flat_tok[order], flat_w[order]
    counts = jnp.bincount(flat_e, length=N_EXPERTS)
    starts = jnp.cumsum(counts) - counts
    padded = (counts + MOE_BLOCK - 1) // MOE_BLOCK * MOE_BLOCK
    pends = jnp.cumsum(padded)
    pstarts = pends - padded
    slot = pstarts[e_s] + (jnp.arange(n_assign) - starts[e_s])
    n_blocks = -(-n_assign // MOE_BLOCK) + N_EXPERTS
    slot_tok = jnp.full((n_blocks * MOE_BLOCK,), T, jnp.int32).at[slot].set(tok_s)
    slot_w = jnp.zeros((n_blocks * MOE_BLOCK,), jnp.float32).at[slot].set(w_s)
    block_e = jnp.clip(jnp.searchsorted(pends, jnp.arange(n_blocks) * MOE_BLOCK, side='right'), 0, N_EXPERTS - 1)
    x_pad = jnp.concatenate([x, jnp.zeros((1, D), x.dtype)], axis=0)

    def body(acc, blk):
        tok_b, e_b, w_b = blk
        y = swiglu(x_pad[tok_b], w_gate[e_b], w_up[e_b], w_down[e_b])
        return acc.at[tok_b].add((y * w_b[:, None]).astype(acc.dtype)), None

    routed, _ = lax.scan(body, jnp.zeros((T + 1, D), jnp.float32),
                         (slot_tok.reshape(n_blocks, MOE_BLOCK), block_e, slot_w.reshape(n_blocks, MOE_BLOCK)))
    out = routed[:T] + swiglu(x, sh_gate, sh_up, sh_down)
    return out.reshape(shape)


def trunk_layer(x, xc, c, c_ctx, lp, layer_idx, with_ctx_out):
    mod = jax.nn.silu(c) @ lp['ada_w'] + lp['ada_b']
    mod_c = jax.nn.silu(c_ctx) @ lp['ada_w'] + lp['ada_b']
    sh1, sc1, g1, sh2, sc2, g2 = jnp.split(mod[:, None, :], 6, axis=-1)
    csh1, csc1, cg1, csh2, csc2, cg2 = jnp.split(mod_c, 6, axis=-1)
    h = rms_norm(x, lp['norm1_g']) * (1.0 + sc1) + sh1
    hc = rms_norm(xc, lp['norm1_g']) * (1.0 + csc1) + csh1
    mixed, mixed_c = token_mixer(h, hc, lp, layer_idx, with_ctx_out)
    x = x + g1 * mixed
    h2 = rms_norm(x, lp['norm2_g']) * (1.0 + sc2) + sh2
    x = x + g2 * moe(h2, lp['router_w'], lp['router_b'], lp['exp_w_gate'], lp['exp_w_up'], lp['exp_w_down'],
                     lp['sh_w_gate'], lp['sh_w_up'], lp['sh_w_down'])
    if with_ctx_out:
        xc = xc + cg1 * mixed_c
        h2c = rms_norm(xc, lp['norm2_g']) * (1.0 + csc2) + csh2
        xc = xc + cg2 * moe(h2c, lp['router_w'], lp['router_b'], lp['exp_w_gate'], lp['exp_w_up'],
                            lp['exp_w_down'], lp['sh_w_gate'], lp['sh_w_up'], lp['sh_w_down'])
    return x, xc


def setup_inputs(seed: int = 0) -> dict:
    key = jax.random.key(seed)
    ks = iter(jax.random.split(key, 48))
    f32 = jnp.float32

    def nrm(shape, scale):
        return jax.random.normal(next(ks), shape, f32) * scale

    G, P = S5_GROUPS, S5_STATE
    n_idx = jnp.arange(P, dtype=f32)
    return {
        'x': nrm((BATCH, SEQ, D_MODEL), 1.0),
        'c': nrm((BATCH, D_MODEL), 1.0),
        'ctx': nrm((BATCH, CTX_LEN, D_MODEL), 1.0),
        'c_ctx': nrm((D_MODEL,), 1.0),
        'ada_w': nrm((DEPTH, D_MODEL, 6 * D_MODEL), 0.5 * D_MODEL ** -0.5),
        'ada_b': nrm((DEPTH, 6 * D_MODEL), 0.02),
        'norm1_g': 1.0 + nrm((DEPTH, D_MODEL), 0.02),
        'norm2_g': 1.0 + nrm((DEPTH, D_MODEL), 0.02),
        'w_in': nrm((DEPTH, D_MODEL, IN_WIDTH), D_MODEL ** -0.5),
        's5_lambda_re': -0.5 + nrm((DEPTH, 2, G, P), 0.01),
        's5_lambda_im': jnp.pi * n_idx + nrm((DEPTH, 2, G, P), 0.01),
        's5_log_step': jax.random.uniform(next(ks), (DEPTH, 2, G), f32, math.log(1e-3), math.log(1e-1)),
        's5_b_re': nrm((DEPTH, 2, G, P, S5_GROUP), (2 * S5_GROUP) ** -0.5),
        's5_b_im': nrm((DEPTH, 2, G, P, S5_GROUP), (2 * S5_GROUP) ** -0.5),
        's5_c_re': nrm((DEPTH, 2, G, S5_GROUP, P), (2 * P) ** -0.5),
        's5_c_im': nrm((DEPTH, 2, G, S5_GROUP, P), (2 * P) ** -0.5),
        's5_d': nrm((DEPTH, S5_WIDTH), 0.5),
        's5_glu_w': nrm((DEPTH, S5_WIDTH, S5_WIDTH), S5_WIDTH ** -0.5),
        's5_glu_b': nrm((DEPTH, S5_WIDTH), 0.01),
        'na_rpb': nrm((DEPTH, NA_HEADS, 2 * NA_WIN_ROWS - 1, 2 * NA_WIN_COLS - 1), 0.02),
        'swa_sink': nrm((DEPTH, SWA_Q_HEADS), 0.5),
        'diff_lq1': nrm((DEPTH, DIFF_QK_DIM), 0.1),
        'diff_lk1': nrm((DEPTH, DIFF_QK_DIM), 0.1),
        'diff_lq2': nrm((DEPTH, DIFF_QK_DIM), 0.1),
        'diff_lk2': nrm((DEPTH, DIFF_QK_DIM), 0.1),
        'diff_subln_g': 1.0 + nrm((DEPTH, DIFF_V_DIM), 0.02),
        'w_branch': nrm((DEPTH, N_BRANCHES, BRANCH_WIDTH, D_MODEL), BRANCH_WIDTH ** -0.5),
        'w_out': nrm((DEPTH, D_MODEL, D_MODEL), D_MODEL ** -0.5),
        'router_w': nrm((DEPTH, D_MODEL, N_EXPERTS), D_MODEL ** -0.5),
        'router_b': nrm((DEPTH, N_EXPERTS), 0.01),
        'exp_w_gate': nrm((DEPTH, N_EXPERTS, D_MODEL, EXPERT_HIDDEN), D_MODEL ** -0.5),
        'exp_w_up': nrm((DEPTH, N_EXPERTS, D_MODEL, EXPERT_HIDDEN), D_MODEL ** -0.5),
        'exp_w_down': nrm((DEPTH, N_EXPERTS, EXPERT_HIDDEN, D_MODEL), EXPERT_HIDDEN ** -0.5),
        'sh_w_gate': nrm((DEPTH, D_MODEL, SHARED_HIDDEN), D_MODEL ** -0.5),
        'sh_w_up': nrm((DEPTH, D_MODEL, SHARED_HIDDEN), D_MODEL ** -0.5),
        'sh_w_down': nrm((DEPTH, SHARED_HIDDEN, D_MODEL), SHARED_HIDDEN ** -0.5),
        'final_g': 1.0 + nrm((D_MODEL,), 0.02),
    }


def reference(x, c, ctx, c_ctx, ada_w, ada_b, norm1_g, norm2_g, w_in, s5_lambda_re, s5_lambda_im, s5_log_step,
              s5_b_re, s5_b_im, s5_c_re, s5_c_im, s5_d, s5_glu_w, s5_glu_b, na_rpb, swa_sink, diff_lq1, diff_lk1,
              diff_lq2, diff_lk2, diff_subln_g, w_branch, w_out, router_w, router_b, exp_w_gate, exp_w_up,
              exp_w_down, sh_w_gate, sh_w_up, sh_w_down, final_g):
    xc = ctx
    for l in range(DEPTH):
        lp = {
            'ada_w': ada_w[l], 'ada_b': ada_b[l], 'norm1_g': norm1_g[l], 'norm2_g': norm2_g[l], 'w_in': w_in[l],
            's5_lambda_re': s5_lambda_re[l], 's5_lambda_im': s5_lambda_im[l], 's5_log_step': s5_log_step[l],
            's5_b_re': s5_b_re[l], 's5_b_im': s5_b_im[l], 's5_c_re': s5_c_re[l], 's5_c_im': s5_c_im[l],
            's5_d': s5_d[l], 's5_glu_w': s5_glu_w[l], 's5_glu_b': s5_glu_b[l], 'na_rpb': na_rpb[l],
            'swa_sink': swa_sink[l], 'diff_lq1': diff_lq1[l], 'diff_lk1': diff_lk1[l], 'diff_lq2': diff_lq2[l],
            'diff_lk2': diff_lk2[l], 'diff_subln_g': diff_subln_g[l], 'w_branch': w_branch[l], 'w_out': w_out[l],
            'router_w': router_w[l], 'router_b': router_b[l], 'exp_w_gate': exp_w_gate[l], 'exp_w_up': exp_w_up[l],
            'exp_w_down': exp_w_down[l], 'sh_w_gate': sh_w_gate[l], 'sh_w_up': sh_w_up[l],
            'sh_w_down': sh_w_down[l],
        }
        x, xc = trunk_layer(x, xc, c, c_ctx, lp, l, l < DEPTH - 1)
    return rms_norm(x, final_g)
```

```python
import functools
import math

import numpy as np
import jax
import jax.numpy as jnp
from jax import lax
from jax.experimental import pallas as pl
from jax.experimental.pallas import tpu as pltpu

F32 = jnp.float32
BF16 = jnp.bfloat16
HIGHEST = lax.Precision.HIGHEST

GRID_W = 64
EPS = 1e-6
NEG_INF = -1e30
ROPE_BASE = 10000.0
D_MODEL = 1024
BRANCH_WIDTH = 256
HEAD_DIM = 64
S5_GROUP = 16
S5_GROUPS = 16
S5_STATE = 64
S5_FLAT = S5_GROUPS * S5_STATE
NA_HEADS = 4
NA_WIN_ROWS = 8
NA_WIN_COLS = 16
SWA_KV_HEADS = 2
SWA_WINDOW = 128
DIFF_HEADS = 4
DIFF_QK_DIM = 32
N_EXPERTS = 64
N_EXPERT_GROUPS = 8
TOPK_GROUPS = 4
TOP_K = 6
EXPERT_HIDDEN = 256
ROUTED_SCALE = 2.5
GATE_WIDTH = 4 * D_MODEL

VMEM_LIMIT = 56 * 1024 * 1024


def _params(*sem):
    return pltpu.CompilerParams(dimension_semantics=sem, vmem_limit_bytes=VMEM_LIMIT)


def _nt_dot(a, b):
    return lax.dot_general(a, b, (((1,), (1,)), ((), ())), preferred_element_type=F32)


def _dot(a, b):
    return jnp.dot(a, b, preferred_element_type=F32)


def _rms(x):
    return x * lax.rsqrt(jnp.mean(x * x, axis=-1, keepdims=True) + EPS)


def _ada_kernel(c_ref, w_ref, b_ref, o_ref):
    c = c_ref[...]
    s = c * jax.nn.sigmoid(c)
    o_ref[...] = jnp.dot(s, w_ref[...], preferred_element_type=F32, precision=HIGHEST) + b_ref[...]


def _ada_mod(cc, w, b):
    rows, d = cc.shape
    width = w.shape[1]
    tn = 1536
    return pl.pallas_call(
        _ada_kernel,
        grid=(width // tn,),
        in_specs=[pl.BlockSpec((rows, d), lambda j: (0, 0)),
                  pl.BlockSpec((d, tn), lambda j: (0, j)),
                  pl.BlockSpec((1, tn), lambda j: (0, j))],
        out_specs=pl.BlockSpec((rows, tn), lambda j: (0, j)),
        out_shape=jax.ShapeDtypeStruct((rows, width), F32),
        compiler_params=_params("arbitrary"),
        name="ada_mod",
    )(cc, w, b.reshape(1, width))


_C_GATE = 0
_C_U = GATE_WIDTH
_C_NA = _C_U + 256
_C_SW = _C_NA + 768
_C_DF = _C_SW + 512
_C_END = _C_DF + 768


def _rope_apply(x, cos, sins, half):
    outs = []
    for j in range(x.shape[1] // 128):
        xs = x[:, j * 128:(j + 1) * 128]
        lane = lax.broadcasted_iota(jnp.int32, xs.shape, 1)
        lo = (lane % (2 * half)) < half
        partner = jnp.where(lo, pltpu.roll(xs, 128 - half, 1), pltpu.roll(xs, half, 1))
        outs.append(xs * cos + partner * sins)
    return outs[0] if len(outs) == 1 else jnp.concatenate(outs, axis=1)


def _inproj_kernel(x_ref, g_ref, sc_ref, sh_ref, w_ref, c64_ref, s64_ref, c32_ref, s32_ref,
                   gate_o, u_o, na_o, sw_o, df_o, *, rope):
    h = _rms(x_ref[...]) * g_ref[...]
    h = h * (1.0 + sc_ref[0]) + sh_ref[0]
    hb = h.astype(BF16)

    def mm(c0, c1):
        return _dot(hb, w_ref[:, c0:c1])

    for k in range(GATE_WIDTH // 512):
        gate_o[:, k * 512:(k + 1) * 512] = jax.nn.sigmoid(mm(k * 512, (k + 1) * 512)).astype(BF16)
    u_o[...] = mm(_C_U, _C_U + 256)

    na = mm(_C_NA, _C_NA + 768)
    na_o[:, 0:256] = (na[:, 0:256] * (HEAD_DIM ** -0.5)).astype(BF16)
    na_o[:, 256:768] = na[:, 256:768].astype(BF16)

    sw = mm(_C_SW, _C_SW + 512)
    swq, swk = sw[:, 0:256], sw[:, 256:384]
    if rope:
        swq = _rope_apply(swq, c64_ref[...], s64_ref[...], 16)
        swk = _rope_apply(swk, c64_ref[...], s64_ref[...], 16)
    sw_o[:, 0:256] = (swq * (HEAD_DIM ** -0.5)).astype(BF16)
    sw_o[:, 256:384] = swk.astype(BF16)
    sw_o[:, 384:512] = sw[:, 384:512].astype(BF16)

    df = mm(_C_DF, _C_DF + 768)
    dfq, dfk = df[:, 0:256], df[:, 256:512]
    if rope:
        dfq = _rope_apply(dfq, c32_ref[...], s32_ref[...], 8)
        dfk = _rope_apply(dfk, c32_ref[...], s32_ref[...], 8)
    df_o[:, 0:256] = (dfq * (DIFF_QK_DIM ** -0.5)).astype(BF16)
    df_o[:, 256:512] = dfk.astype(BF16)
    df_o[:, 512:768] = df[:, 512:768].astype(BF16)


def _inproj(x2d, norm_g, sc, sh, w_bf16, tables, *, rows_per_mod, rope, seq):
    rows = x2d.shape[0]
    tm = 512
    assert rows % tm == 0 and rows_per_mod % tm == 0 and seq % tm == 0
    tiles_per_seq = seq // tm

    def mod_map(i):
        return ((i * tm) // rows_per_mod, 0, 0)

    def tab_map(i):
        return (i % tiles_per_seq, 0)

    tab_spec = pl.BlockSpec((tm, 128), tab_map)
    row = lambda w: pl.BlockSpec((tm, w), lambda i: (i, 0))
    return pl.pallas_call(
        functools.partial(_inproj_kernel, rope=rope),
        grid=(rows // tm,),
        in_specs=[row(D_MODEL),
                  pl.BlockSpec((1, D_MODEL), lambda i: (0, 0)),
                  pl.BlockSpec((1, 1, D_MODEL), mod_map),
                  pl.BlockSpec((1, 1, D_MODEL), mod_map),
                  pl.BlockSpec((D_MODEL, _C_END), lambda i: (0, 0)),
                  tab_spec, tab_spec, tab_spec, tab_spec],
        out_specs=[row(GATE_WIDTH), row(256), row(768), row(512), row(768)],
        out_shape=[jax.ShapeDtypeStruct((rows, GATE_WIDTH), BF16),
                   jax.ShapeDtypeStruct((rows, 256), F32),
                   jax.ShapeDtypeStruct((rows, 768), BF16),
                   jax.ShapeDtypeStruct((rows, 512), BF16),
                   jax.ShapeDtypeStruct((rows, 768), BF16)],
        compiler_params=_params("arbitrary"),
        name="inproj",
    )(x2d, norm_g.reshape(1, D_MODEL), sc, sh, w_bf16, *tables)


def _rope_tables(seq):
    t = jnp.arange(seq)
    rows = (t // GRID_W).astype(F32)
    cols = (t % GRID_W).astype(F32)
    lane = np.arange(128)
    out = []
    for dim in (64, 32):
        quarter = dim // 4
        inv_freq = ROPE_BASE ** (-jnp.arange(quarter, dtype=F32) / quarter)
        l = lane % dim
        use_col = l >= dim // 2
        fidx = l % quarter
        hi = (l % (dim // 2)) >= quarter
        ang_r = rows[:, None] * inv_freq[None, :]
        ang_c = cols[:, None] * inv_freq[None, :]
        ang = jnp.where(use_col[None, :], ang_c[:, fidx], ang_r[:, fidx])
        out.append(jnp.cos(ang))
        out.append(jnp.where(hi[None, :], jnp.sin(ang), -jnp.sin(ang)))
    return tuple(out)


S5_CHUNK = 128


def _s5_kernel(u_ref, win_ref, wout_ref, are_ref, aim_ref, y_ref, bu_ref, st_ref, *, tc, nb):
    d = pl.program_id(0)
    i = pl.program_id(1)

    @pl.when(i == 0)
    def _():
        st_ref[...] = jnp.zeros_like(st_ref)

    u = u_ref[...].reshape(tc * nb, BRANCH_WIDTH).astype(BF16)
    bu_ref[...] = _dot(u, win_ref[0])
    ar = jnp.broadcast_to(are_ref[0], (nb, S5_FLAT))
    ai = jnp.broadcast_to(aim_ref[0], (nb, S5_FLAT))

    def body(j, carry):
        xr, xi = carry
        t = j + d * (tc - 1 - 2 * j)
        row = pl.multiple_of(t * nb, nb)
        br = bu_ref[pl.ds(row, nb), 0:S5_FLAT]
        bi = bu_ref[pl.ds(row, nb), S5_FLAT:2 * S5_FLAT]
        nr = ar * xr - ai * xi + br
        ni = ar * xi + ai * xr + bi
        bu_ref[pl.ds(row, nb), 0:S5_FLAT] = nr
        bu_ref[pl.ds(row, nb), S5_FLAT:2 * S5_FLAT] = ni
        return nr, ni

    xr, xi = lax.fori_loop(0, tc, body, (st_ref[:, 0:S5_FLAT], st_ref[:, S5_FLAT:2 * S5_FLAT]), unroll=4)
    st_ref[:, 0:S5_FLAT] = xr
    st_ref[:, S5_FLAT:2 * S5_FLAT] = xi
    y = _dot(bu_ref[...].astype(BF16), wout_ref[0])
    y_ref[0] = y.reshape(tc, nb, BRANCH_WIDTH)


def _s5_scan(u_tm, win, wout, a_re, a_im, n_ctx):
    s_len, nb, _ = u_tm.shape
    tc = S5_CHUNK
    assert nb == 8 and s_len % tc == 0 and n_ctx % tc == 0
    nct = n_ctx // tc
    nlt = (s_len - n_ctx) // tc

    def chunk(d, i):
        rev = jnp.where(i < nct, nct - 1 - i, 2 * nct + nlt - 1 - i)
        return jnp.where(d == 0, i, rev)

    return pl.pallas_call(
        functools.partial(_s5_kernel, tc=tc, nb=nb),
        grid=(2, nct + nlt),
        in_specs=[pl.BlockSpec((tc, nb, BRANCH_WIDTH), lambda d, i: (chunk(d, i), 0, 0)),
                  pl.BlockSpec((1, BRANCH_WIDTH, 2 * S5_FLAT), lambda d, i: (d, 0, 0)),
                  pl.BlockSpec((1, 2 * S5_FLAT, BRANCH_WIDTH), lambda d, i: (d, 0, 0)),
                  pl.BlockSpec((1, 1, S5_FLAT), lambda d, i: (d, 0, 0)),
                  pl.BlockSpec((1, 1, S5_FLAT), lambda d, i: (d, 0, 0))],
        out_specs=pl.BlockSpec((1, tc, nb, BRANCH_WIDTH), lambda d, i: (d, chunk(d, i), 0, 0)),
        out_shape=jax.ShapeDtypeStruct((2, s_len, nb, BRANCH_WIDTH), F32),
        scratch_shapes=[pltpu.VMEM((tc * nb, 2 * S5_FLAT), F32),
                        pltpu.VMEM((nb, 2 * S5_FLAT), F32)],
        compiler_params=_params("arbitrary", "arbitrary"),
        name="s5_scan",
    )(u_tm, win, wout, a_re, a_im)


def _s5_params(lam_re, lam_im, log_step, b_re, b_im, c_re, c_im):
    lr = lam_re.astype(F32)
    li = lam_im.astype(F32)
    dt = jnp.exp(log_step.astype(F32))[..., None]
    mag = jnp.exp(lr * dt)
    a_re = mag * jnp.cos(li * dt)
    a_im = mag * jnp.sin(li * dt)
    nr, ni, den = a_re - 1.0, a_im, lr * lr + li * li
    k_re = ((nr * lr + ni * li) / den)[..., None]
    k_im = ((ni * lr - nr * li) / den)[..., None]
    br = b_re.astype(F32)
    bi = b_im.astype(F32)
    bb_re = k_re * br - k_im * bi
    bb_im = k_re * bi + k_im * br
    eye = jnp.eye(S5_GROUPS, dtype=F32)

    def blockdiag_in(bb):
        m = jnp.einsum('dgpc,gh->dgchp', bb, eye)
        return m.reshape(2, S5_GROUPS * S5_GROUP, S5_GROUPS * S5_STATE)

    def blockdiag_out(cc):
        m = jnp.einsum('dgcp,gh->dgphc', cc, eye)
        return m.reshape(2, S5_GROUPS * S5_STATE, S5_GROUPS * S5_GROUP)

    win = jnp.concatenate([blockdiag_in(bb_re), blockdiag_in(bb_im)], axis=2).astype(BF16)
    wout = jnp.concatenate([blockdiag_out(c_re.astype(F32)), -blockdiag_out(c_im.astype(F32))], axis=1).astype(BF16)
    return win, wout, a_re.reshape(2, 1, S5_FLAT), a_im.reshape(2, 1, S5_FLAT)


def _softmax_parts(scores, extra=None):
    m = scores[0].max(axis=-1, keepdims=True)
    for s in scores[1:]:
        m = jnp.maximum(m, s.max(axis=-1, keepdims=True))
    if extra is not None:
        m = jnp.maximum(m, extra)
    ps = [jnp.exp(s - m) for s in scores]
    l = ps[0].sum(axis=-1, keepdims=True)
    for p in ps[1:]:
        l = l + p.sum(axis=-1, keepdims=True)
    if extra is not None:
        l = l + jnp.exp(extra - m)
    return ps, l


def _na_kernel(q_ref, k_ref, v_ref, kc_ref, vc_ref, bias_ref, o_ref, *, grid_rows):
    r = pl.program_id(1)
    start = pl.multiple_of(jnp.clip(r - NA_WIN_ROWS // 2, 0, grid_rows - NA_WIN_ROWS) * GRID_W, GRID_W)
    nloc = NA_WIN_ROWS * GRID_W
    kw = k_ref[pl.ds(start, nloc), :]
    vw = v_ref[pl.ds(start, nloc), :]
    q = q_ref[...]
    kc = kc_ref[...]
    vc = vc_ref[...]
    outs = []
    for h in range(NA_HEADS):
        sl = slice(h * HEAD_DIM, (h + 1) * HEAD_DIM)
        qh = q[:, sl]
        s_loc = _nt_dot(qh, kw[:, sl]) + bias_ref[0, h]
        s_ctx = _nt_dot(qh, kc[:, sl])
        (p_loc, p_ctx), l = _softmax_parts([s_loc, s_ctx])
        o = _dot(p_loc.astype(BF16), vw[:, sl]) + _dot(p_ctx.astype(BF16), vc[:, sl])
        outs.append(o / l)
    o_ref[...] = jnp.concatenate(outs, axis=1).astype(BF16)


def _na_bias_table(rpb, grid_rows):
    cls_r = np.array([0, 1, 2, 3, 4, grid_rows - 3, grid_rows - 2, grid_rows - 1])
    start = np.clip(cls_r - NA_WIN_ROWS // 2, 0, grid_rows - NA_WIN_ROWS)
    delta = start[:, None] + np.arange(NA_WIN_ROWS)[None, :] - cls_r[:, None] + (NA_WIN_ROWS - 1)
    col = np.arange(GRID_W)
    cstart = np.clip(col - NA_WIN_COLS // 2, 0, GRID_W - NA_WIN_COLS)
    col_in = (col[None, :] >= cstart[:, None]) & (col[None, :] < cstart[:, None] + NA_WIN_COLS)
    cb = np.clip(col[None, :] - col[:, None] + (NA_WIN_COLS - 1), 0, 2 * NA_WIN_COLS - 2)
    t = rpb.astype(F32)[:, delta[:, :, None, None], cb[None, None, :, :]]
    t = jnp.where(col_in[None, None, None], t, NEG_INF)
    return t.transpose(1, 0, 3, 2, 4).reshape(8, NA_HEADS, GRID_W, NA_WIN_ROWS * GRID_W)


def _na_attention(na, na_c, bias, nbatch, seq, n_ctx):
    grid_rows = seq // GRID_W
    assert grid_rows >= NA_WIN_ROWS

    def cls(r):
        return jnp.where(r < 4, r, jnp.where(r > grid_rows - 4, r - (grid_rows - 8), 4))

    return pl.pallas_call(
        functools.partial(_na_kernel, grid_rows=grid_rows),
        grid=(nbatch, grid_rows),
        in_specs=[pl.BlockSpec((GRID_W, 256), lambda b, r: (b * grid_rows + r, 0)),
                  pl.BlockSpec((seq, 256), lambda b, r: (b, 1)),
                  pl.BlockSpec((seq, 256), lambda b, r: (b, 2)),
                  pl.BlockSpec((n_ctx, 256), lambda b, r: (b, 1)),
                  pl.BlockSpec((n_ctx, 256), lambda b, r: (b, 2)),
                  pl.BlockSpec((1, NA_HEADS, GRID_W, NA_WIN_ROWS * GRID_W), lambda b, r: (cls(r), 0, 0, 0))],
        out_specs=pl.BlockSpec((GRID_W, 256), lambda b, r: (b * grid_rows + r, 0)),
        out_shape=jax.ShapeDtypeStruct((nbatch * seq, 256), BF16),
        compiler_params=_params("arbitrary", "arbitrary"),
        name="na_attention",
    )(na, na, na, na_c, na_c, bias)


SWA_BLOCK = 128


def _swa_kernel(q_ref, k_ref, v_ref, kc_ref, vc_ref, sink_ref, o_ref, *, seq):
    n = pl.program_id(1)
    band = 3 * SWA_BLOCK
    bstart = pl.multiple_of(jnp.clip((n - 1) * SWA_BLOCK, 0, seq - band), SWA_BLOCK)
    kb = k_ref[pl.ds(bstart, band), :]
    vb = v_ref[pl.ds(bstart, band), :]
    q = q_ref[...]
    kc = kc_ref[...]
    vc = vc_ref[...]
    rows = 2 * SWA_BLOCK
    qpos = n * SWA_BLOCK + lax.broadcasted_iota(jnp.int32, (rows, band), 0) % SWA_BLOCK
    kpos = bstart + lax.broadcasted_iota(jnp.int32, (rows, band), 1)
    mask = jnp.abs(qpos - kpos) <= SWA_WINDOW
    outs = [None] * 4
    for kv in range(SWA_KV_HEADS):
        sl = slice(kv * HEAD_DIM, (kv + 1) * HEAD_DIM)
        h0, h1 = 2 * kv, 2 * kv + 1
        qh = jnp.concatenate([q[:, h0 * HEAD_DIM:(h0 + 1) * HEAD_DIM],
                              q[:, h1 * HEAD_DIM:(h1 + 1) * HEAD_DIM]], axis=0)
        sk = jnp.concatenate([jnp.broadcast_to(sink_ref[0:1, h0:h0 + 1], (SWA_BLOCK, 1)),
                              jnp.broadcast_to(sink_ref[0:1, h1:h1 + 1], (SWA_BLOCK, 1))], axis=0)
        s_loc = jnp.where(mask, _nt_dot(qh, kb[:, sl]), NEG_INF)
        s_ctx = _nt_dot(qh, kc[:, sl])
        (p_loc, p_ctx), l = _softmax_parts([s_loc, s_ctx], extra=sk)
        o = (_dot(p_loc.astype(BF16), vb[:, sl]) + _dot(p_ctx.astype(BF16), vc[:, sl])) / l
        outs[h0] = o[0:SWA_BLOCK]
        outs[h1] = o[SWA_BLOCK:rows]
    o_ref[...] = jnp.concatenate(outs, axis=1).astype(BF16)


def _swa_attention(sw, sw_c, sink, nbatch, seq, n_ctx):
    nblk = seq // SWA_BLOCK
    assert seq >= 3 * SWA_BLOCK
    sink_pad = jnp.zeros((1, 128), F32).at[0, 0:4].set(sink.astype(F32))
    return pl.pallas_call(
        functools.partial(_swa_kernel, seq=seq),
        grid=(nbatch, nblk),
        in_specs=[pl.BlockSpec((SWA_BLOCK, 256), lambda b, n: (b * nblk + n, 0)),
                  pl.BlockSpec((seq, 128), lambda b, n: (b, 2)),
                  pl.BlockSpec((seq, 128), lambda b, n: (b, 3)),
                  pl.BlockSpec((n_ctx, 128), lambda b, n: (b, 2)),
                  pl.BlockSpec((n_ctx, 128), lambda b, n: (b, 3)),
                  pl.BlockSpec((1, 128), lambda b, n: (0, 0))],
        out_specs=pl.BlockSpec((SWA_BLOCK, 256), lambda b, n: (b * nblk + n, 0)),
        out_shape=jax.ShapeDtypeStruct((nbatch * seq, 256), BF16),
        compiler_params=_params("arbitrary", "arbitrary"),
        name="swa_attention",
    )(sw, sw, sw, sw_c, sw_c, sink_pad)


DIFF_TQ = 128
DIFF_CK = 512


def _diff_lambda(lq1_ref, lk1_ref, lq2_ref, lk2_ref, lambda_init):
    s1 = jnp.sum(lq1_ref[...] * lk1_ref[...], axis=-1, keepdims=True)
    s2 = jnp.sum(lq2_ref[...] * lk2_ref[...], axis=-1, keepdims=True)
    return jnp.exp(s1) - jnp.exp(s2) + lambda_init


def _stack_maps(qh):
    lane = lax.broadcasted_iota(jnp.int32, qh.shape, 1)
    zero = jnp.zeros_like(qh)
    return jnp.concatenate([jnp.where(lane < DIFF_QK_DIM, qh, zero),
                            jnp.where(lane >= DIFF_QK_DIM, qh, zero)], axis=0)


def _subln(o0, o1, lam, g, lambda_init):
    o = o0 - lam * o1
    return _rms(o) * g * (1.0 - lambda_init)


def _diff_kernel(q_ref, k_ref, v_ref, kc_ref, vc_ref, lq1_ref, lk1_ref, lq2_ref, lk2_ref, g_ref, o_ref,
                 *, seq, lambda_init):
    lam = _diff_lambda(lq1_ref, lk1_ref, lq2_ref, lk2_ref, lambda_init)
    q = q_ref[...]
    tq = q.shape[0]
    nchunk = seq // DIFF_CK
    outs = []
    for h in range(DIFF_HEADS):
        sl = slice(h * HEAD_DIM, (h + 1) * HEAD_DIM)
        q2 = _stack_maps(q[:, sl])

        def update(carry, kblk, vblk):
            m, l, acc = carry
            s = _nt_dot(q2, kblk)
            m_new = jnp.maximum(m, s.max(axis=-1, keepdims=True))
            alpha = jnp.exp(m - m_new)
            p = jnp.exp(s - m_new)
            l = alpha * l + p.sum(axis=-1, keepdims=True)
            acc = alpha * acc + _dot(p.astype(BF16), vblk)
            return m_new, l, acc

        def body(c, carry):
            off = pl.multiple_of(c * DIFF_CK, DIFF_CK)
            return update(carry, k_ref[pl.ds(off, DIFF_CK), sl], v_ref[pl.ds(off, DIFF_CK), sl])

        init = (jnp.full((2 * tq, 1), NEG_INF, F32), jnp.zeros((2 * tq, 1), F32), jnp.zeros((2 * tq, HEAD_DIM), F32))
        carry = lax.fori_loop(0, nchunk, body, init)
        m, l, acc = update(carry, kc_ref[:, sl], vc_ref[:, sl])
        o = acc / l
        outs.append(_subln(o[0:tq], o[tq:2 * tq], lam, g_ref[...], lambda_init))
    o_ref[...] = jnp.concatenate(outs, axis=1).astype(BF16)


def _diff_attention(df, df_c, lqk, subln_g, lambda_init, nbatch, seq, n_ctx):
    nq = seq // DIFF_TQ
    assert seq % DIFF_CK == 0
    vec = pl.BlockSpec((1, DIFF_QK_DIM), lambda b, n: (0, 0))
    return pl.pallas_call(
        functools.partial(_diff_kernel, seq=seq, lambda_init=lambda_init),
        grid=(nbatch, nq),
        in_specs=[pl.BlockSpec((DIFF_TQ, 256), lambda b, n: (b * nq + n, 0)),
                  pl.BlockSpec((seq, 256), lambda b, n: (b, 1)),
                  pl.BlockSpec((seq, 256), lambda b, n: (b, 2)),
                  pl.BlockSpec((n_ctx, 256), lambda b, n: (b, 1)),
                  pl.BlockSpec((n_ctx, 256), lambda b, n: (b, 2)),
                  vec, vec, vec, vec,
                  pl.BlockSpec((1, HEAD_DIM), lambda b, n: (0, 0))],
        out_specs=pl.BlockSpec((DIFF_TQ, 256), lambda b, n: (b * nq + n, 0)),
        out_shape=jax.ShapeDtypeStruct((nbatch * seq, 256), BF16),
        compiler_params=_params("arbitrary", "arbitrary"),
        name="diff_attention",
    )(df, df, df, df_c, df_c, *lqk, subln_g.reshape(1, HEAD_DIM).astype(F32))


def _ctx_attn_kernel(na_ref, sw_ref, df_ref, sink_ref, lq1_ref, lk1_ref, lq2_ref, lk2_ref, g_ref,
                     nb_o, sw_o, df_o, *, lambda_init):
    n = na_ref.shape[0]
    na = na_ref[...]
    outs = []
    for h in range(NA_HEADS):
        sl = slice(h * HEAD_DIM, (h + 1) * HEAD_DIM)
        (p,), l = _softmax_parts([_nt_dot(na[:, sl], na[:, 256 + h * HEAD_DIM:256 + (h + 1) * HEAD_DIM])])
        outs.append(_dot(p.astype(BF16), na[:, 512 + h * HEAD_DIM:512 + (h + 1) * HEAD_DIM]) / l)
    nb_o[...] = jnp.concatenate(outs, axis=1).astype(BF16)
    sw = sw_ref[...]
    outs = []
    for hq in range(4):
        kv = hq // 2
        k = sw[:, 256 + kv * HEAD_DIM:256 + (kv + 1) * HEAD_DIM]
        v = sw[:, 384 + kv * HEAD_DIM:384 + (kv + 1) * HEAD_DIM]
        sk = jnp.broadcast_to(sink_ref[0:1, hq:hq + 1], (n, 1))
        (p,), l = _softmax_parts([_nt_dot(sw[:, hq * HEAD_DIM:(hq + 1) * HEAD_DIM], k)], extra=sk)
        outs.append(_dot(p.astype(BF16), v) / l)
    sw_o[...] = jnp.concatenate(outs, axis=1).astype(BF16)
    lam = _diff_lambda(lq1_ref, lk1_ref, lq2_ref, lk2_ref, lambda_init)
    df = df_ref[...]
    outs = []
    for h in range(DIFF_HEADS):
        sl = slice(h * HEAD_DIM, (h + 1) * HEAD_DIM)
        q2 = _stack_maps(df[:, sl])
        (p,), l = _softmax_parts([_nt_dot(q2, df[:, 256 + h * HEAD_DIM:256 + (h + 1) * HEAD_DIM])])
        o = _dot(p.astype(BF16), df[:, 512 + h * HEAD_DIM:512 + (h + 1) * HEAD_DIM]) / l
        outs.append(_subln(o[0:n], o[n:2 * n], lam, g_ref[...], lambda_init))
    df_o[...] = jnp.concatenate(outs, axis=1).astype(BF16)


def _ctx_attention(na_c, sw_c, df_c, sink, lqk, subln_g, lambda_init, nbatch, n_ctx):
    sink_pad = jnp.zeros((1, 128), F32).at[0, 0:4].set(sink.astype(F32))
    vec = pl.BlockSpec((1, DIFF_QK_DIM), lambda b: (0, 0))
    out = jax.ShapeDtypeStruct((nbatch * n_ctx, 256), BF16)
    return pl.pallas_call(
        functools.partial(_ctx_attn_kernel, lambda_init=lambda_init),
        grid=(nbatch,),
        in_specs=[pl.BlockSpec((n_ctx, 768), lambda b: (b, 0)),
                  pl.BlockSpec((n_ctx, 512), lambda b: (b, 0)),
                  pl.BlockSpec((n_ctx, 768), lambda b: (b, 0)),
                  pl.BlockSpec((1, 128), lambda b: (0, 0)),
                  vec, vec, vec, vec,
                  pl.BlockSpec((1, HEAD_DIM), lambda b: (0, 0))],
        out_specs=[pl.BlockSpec((n_ctx, 256), lambda b: (b, 0))] * 3,
        out_shape=[out, out, out],
        compiler_params=_params("arbitrary"),
        name="ctx_attention",
    )(na_c, sw_c, df_c, sink_pad, *lqk, subln_g.reshape(1, HEAD_DIM).astype(F32))


def _merge_kernel(u_ref, yf_ref, yr_ref, d_ref, gw_ref, gb_ref, yb_ref, yc_ref, yd_ref, gate_ref,
                  wb_ref, wo_ref, x_ref, g1_ref, n2_ref, sc2_ref, sh2_ref, rw_ref,
                  x1_o, h2_o, lg_o):
    y = u_ref[...] * d_ref[...] + yf_ref[...] + yr_ref[...]
    a = jax.nn.gelu(y, approximate=True)
    ya = a * jax.nn.sigmoid(_dot(a.astype(BF16), gw_ref[...]) + gb_ref[...])
    branches = (ya.astype(BF16), yb_ref[...], yc_ref[...], yd_ref[...])
    acc = None
    for i in range(4):
        t = gate_ref[:, i * D_MODEL:(i + 1) * D_MODEL].astype(F32) * _dot(branches[i], wb_ref[i])
        acc = t if acc is None else acc + t
    mixed = _dot(acc.astype(BF16), wo_ref[...])
    x1 = x_ref[...] + g1_ref[0] * mixed
    x1_o[...] = x1
    h2 = _rms(x1) * n2_ref[...]
    h2 = h2 * (1.0 + sc2_ref[0]) + sh2_ref[0]
    h2_o[...] = h2.astype(BF16)
    lg_o[...] = jnp.dot(h2, rw_ref[...], preferred_element_type=F32, precision=HIGHEST)


def _merge(u, yf, yr, s5_d, glu_w, glu_b, yb, yc, yd, gate, wb, wo, x2d, g1, norm2_g, sc2, sh2, router_w,
           *, rows_per_mod):
    rows = x2d.shape[0]
    tm = 512
    assert rows % tm == 0 and rows_per_mod % tm == 0

    def mod_map(i):
        return ((i * tm) // rows_per_mod, 0, 0)

    row = lambda w: pl.BlockSpec((tm, w), lambda i: (i, 0))
    full = lambda *shape: pl.BlockSpec(shape, lambda i: (0,) * len(shape))
    mod = pl.BlockSpec((1, 1, D_MODEL), mod_map)
    return pl.pallas_call(
        _merge_kernel,
        grid=(rows // tm,),
        in_specs=[row(256), row(256), row(256), full(1, 256), full(256, 256), full(1, 256),
                  row(256), row(256), row(256), row(GATE_WIDTH),
                  full(4, 256, D_MODEL), full(D_MODEL, D_MODEL), row(D_MODEL),
                  mod, full(1, D_MODEL), mod, mod, full(D_MODEL, N_EXPERTS)],
        out_specs=[row(D_MODEL), row(D_MODEL), row(N_EXPERTS)],
        out_shape=[jax.ShapeDtypeStruct((rows, D_MODEL), F32),
                   jax.ShapeDtypeStruct((rows, D_MODEL), BF16),
                   jax.ShapeDtypeStruct((rows, N_EXPERTS), F32)],
        compiler_params=_params("arbitrary"),
        name="merge",
    )(u, yf, yr, s5_d.reshape(1, 256).astype(F32), glu_w.astype(BF16), glu_b.reshape(1, 256).astype(F32),
      yb, yc, yd, gate, wb.astype(BF16), wo.astype(BF16), x2d, g1, norm2_g.reshape(1, D_MODEL), sc2, sh2,
      router_w.astype(F32))


def _router_kernel(lg_ref, b_ref, cw_ref):
    tr = lg_ref.shape[1]
    gsz = N_EXPERTS // N_EXPERT_GROUPS
    sc = jax.nn.sigmoid(lg_ref[...])
    bi = sc + b_ref[...]
    e_iota = lax.broadcasted_iota(jnp.int32, (gsz, tr), 0).astype(F32)
    groups = [bi[g * gsz:(g + 1) * gsz] for g in range(N_EXPERT_GROUPS)]
    gs = []
    for bg in groups:
        m1 = bg.max(axis=0, keepdims=True)
        i1 = jnp.where(bg == m1, e_iota, float(gsz)).min(axis=0, keepdims=True)
        m2 = jnp.where(e_iota == i1, -jnp.inf, bg).max(axis=0, keepdims=True)
        gs.append(m1 + m2)
    v = []
    for g in range(N_EXPERT_GROUPS):
        rank = jnp.zeros((1, tr), F32)
        for g2 in range(N_EXPERT_GROUPS):
            if g2 == g:
                continue
            beats = (gs[g2] >= gs[g]) if g2 < g else (gs[g2] > gs[g])
            rank = rank + jnp.where(beats, 1.0, 0.0)
        v.append(jnp.where(rank < TOPK_GROUPS, groups[g], NEG_INF))
    flat = [e_iota + float(g * gsz) for g in range(N_EXPERT_GROUPS)]
    sel = [jnp.zeros((gsz, tr), F32) for _ in range(N_EXPERT_GROUPS)]
    for _ in range(TOP_K):
        m = v[0].max(axis=0, keepdims=True)
        for g in range(1, N_EXPERT_GROUPS):
            m = jnp.maximum(m, v[g].max(axis=0, keepdims=True))
        am = jnp.where(v[0] == m, flat[0], float(N_EXPERTS)).min(axis=0, keepdims=True)
        for g in range(1, N_EXPERT_GROUPS):
            am = jnp.minimum(am, jnp.where(v[g] == m, flat[g], float(N_EXPERTS)).min(axis=0, keepdims=True))
        for g in range(N_EXPERT_GROUPS):
            hit = flat[g] == am
            sel[g] = jnp.where(hit, 1.0, sel[g])
            v[g] = jnp.where(hit, -jnp.inf, v[g])
    ssel = [sel[g] * sc[g * gsz:(g + 1) * gsz] for g in range(N_EXPERT_GROUPS)]
    den = ssel[0].sum(axis=0, keepdims=True)
    for g in range(1, N_EXPERT_GROUPS):
        den = den + ssel[g].sum(axis=0, keepdims=True)
    for g in range(N_EXPERT_GROUPS):
        cw_ref[g * gsz:(g + 1) * gsz, :] = ssel[g] / den * ROUTED_SCALE


def _router(logits_t, router_b):
    ne, rows = logits_t.shape
    tr = 512
    assert rows % tr == 0
    return pl.pallas_call(
        _router_kernel,
        grid=(rows // tr,),
        in_specs=[pl.BlockSpec((ne, tr), lambda i: (0, i)),
                  pl.BlockSpec((ne, 1), lambda i: (0, 0))],
        out_specs=pl.BlockSpec((ne, tr), lambda i: (0, i)),
        out_shape=jax.ShapeDtypeStruct((ne, rows), F32),
        compiler_params=_params("arbitrary"),
        name="router",
    )(logits_t, router_b.reshape(ne, 1).astype(F32))


def _moe_kernel(h_ref, cw_ref, wgu_ref, wd_ref, x1_ref, g2_ref, fg_ref, o_ref, acc_ref, *, n_exp, final):
    e = pl.program_id(1)

    @pl.when(e == 0)
    def _():
        acc_ref[...] = jnp.zeros_like(acc_ref)

    hgu = _dot(h_ref[...], wgu_ref[0])
    g = hgu[:, 0:EXPERT_HIDDEN]
    a = g * jax.nn.sigmoid(g) * hgu[:, EXPERT_HIDDEN:2 * EXPERT_HIDDEN]
    cw = cw_ref[...]
    lane = lax.broadcasted_iota(jnp.int32, cw.shape, 1)
    w = jnp.sum(jnp.where(lane == e, cw, 0.0), axis=1, keepdims=True)
    acc_ref[...] += _dot((a * w).astype(BF16), wd_ref[0])

    @pl.when(e == n_exp - 1)
    def _():
        x2 = x1_ref[...] + g2_ref[0] * acc_ref[...]
        if final:
            x2 = _rms(x2) * fg_ref[...]
        o_ref[...] = x2


def _moe(h2, cw_ext, wgu, wd, x1, g2, final_g, *, rows_per_mod, final):
    rows = h2.shape[0]
    n_exp = wgu.shape[0]
    tm = min(1024, rows_per_mod)
    assert rows % tm == 0 and rows_per_mod % tm == 0

    def mod_map(i, e):
        return ((i * tm) // rows_per_mod, 0, 0)

    return pl.pallas_call(
        functools.partial(_moe_kernel, n_exp=n_exp, final=final),
        grid=(rows // tm, n_exp),
        in_specs=[pl.BlockSpec((tm, D_MODEL), lambda i, e: (i, 0)),
                  pl.BlockSpec((tm, 128), lambda i, e: (i, 0)),
                  pl.BlockSpec((1, D_MODEL, 2 * EXPERT_HIDDEN), lambda i, e: (e, 0, 0)),
                  pl.BlockSpec((1, EXPERT_HIDDEN, D_MODEL), lambda i, e: (e, 0, 0)),
                  pl.BlockSpec((tm, D_MODEL), lambda i, e: (i, 0)),
                  pl.BlockSpec((1, 1, D_MODEL), mod_map),
                  pl.BlockSpec((1, D_MODEL), lambda i, e: (0, 0))],
        out_specs=pl.BlockSpec((tm, D_MODEL), lambda i, e: (i, 0)),
        out_shape=jax.ShapeDtypeStruct((rows, D_MODEL), F32),
        scratch_shapes=[pltpu.VMEM((tm, D_MODEL), F32)],
        compiler_params=_params("arbitrary", "arbitrary"),
        name="moe_experts",
    )(h2, cw_ext, wgu, wd, x1, g2, final_g.reshape(1, D_MODEL).astype(F32))


def _reorder_w_in(w_in):
    split = 256 + 768 + 512 + 768
    return jnp.concatenate([w_in[:, split:], w_in[:, :split]], axis=1).astype(BF16)


def _mods(mod_row_block):
    return [mod_row_block[:, None, k * D_MODEL:(k + 1) * D_MODEL] for k in range(6)]


def _moe_block(h2, logits, lp, x1, g2, final_g, *, rows_per_mod, final):
    cw_t = _router(logits.T, lp['router_b'])
    rows = h2.shape[0]
    cw_ext = jnp.concatenate([cw_t.T, jnp.ones((rows, 1), F32), jnp.zeros((rows, 127 - N_EXPERTS), F32)], axis=1)
    wgu = jnp.concatenate([jnp.concatenate([lp['exp_w_gate'], lp['exp_w_up']], axis=2),
                           jnp.concatenate([lp['sh_w_gate'], lp['sh_w_up']], axis=1)[None]], axis=0).astype(BF16)
    wd = jnp.concatenate([lp['exp_w_down'], lp['sh_w_down'][None]], axis=0).astype(BF16)
    return _moe(h2, cw_ext, wgu, wd, x1, g2, final_g, rows_per_mod=rows_per_mod, final=final)


def _layer(x2d, xc2d, c16, lp, layer_idx, tables, final_g, *, nbatch, seq, n_ctx, with_ctx_out, final):
    lambda_init = 0.8 - 0.6 * math.exp(-0.3 * layer_idx)
    mod = _ada_mod(c16, lp['ada_w'].astype(F32), lp['ada_b'].astype(F32))
    sh1, sc1, g1, sh2, sc2, g2 = _mods(mod[0:nbatch])
    csh1, csc1, cg1, csh2, csc2, cg2 = _mods(mod[nbatch:nbatch + 1])
    w_in = _reorder_w_in(lp['w_in'])
    rows_lat = nbatch * seq
    rows_ctx = nbatch * n_ctx

    gate, u, na, sw, df = _inproj(x2d, lp['norm1_g'], sc1, sh1, w_in, tables,
                                  rows_per_mod=seq, rope=True, seq=seq)
    gate_c, u_c, na_c, sw_c, df_c = _inproj(xc2d, lp['norm1_g'], csc1, csh1, w_in, tables,
                                            rows_per_mod=rows_ctx, rope=False, seq=seq)

    win, wout, a_re, a_im = _s5_params(lp['s5_lambda_re'], lp['s5_lambda_im'], lp['s5_log_step'],
                                       lp['s5_b_re'], lp['s5_b_im'], lp['s5_c_re'], lp['s5_c_im'])
    u_tm = jnp.concatenate([u_c.reshape(nbatch, n_ctx, 256).transpose(1, 0, 2),
                            u.reshape(nbatch, seq, 256).transpose(1, 0, 2)], axis=0)
    y_tm = _s5_scan(u_tm, win, wout, a_re, a_im, n_ctx)
    y_bm = y_tm.transpose(0, 2, 1, 3)
    yf = y_bm[0, :, n_ctx:].reshape(rows_lat, 256)
    yr = y_bm[1, :, n_ctx:].reshape(rows_lat, 256)

    lqk = [lp[k].reshape(1, DIFF_QK_DIM).astype(F32) for k in ('diff_lq1', 'diff_lk1', 'diff_lq2', 'diff_lk2')]
    bias = _na_bias_table(lp['na_rpb'], seq // GRID_W)
    yb = _na_attention(na, na_c, bias, nbatch, seq, n_ctx)
    yc = _swa_attention(sw, sw_c, lp['swa_sink'], nbatch, seq, n_ctx)
    yd = _diff_attention(df, df_c, lqk, lp['diff_subln_g'], lambda_init, nbatch, seq, n_ctx)

    merge_w = (lp['s5_d'], lp['s5_glu_w'], lp['s5_glu_b'])
    x1, h2, logits = _merge(u, yf, yr, *merge_w, yb, yc, yd, gate, lp['w_branch'], lp['w_out'], x2d, g1,
                            lp['norm2_g'], sc2, sh2, lp['router_w'], rows_per_mod=seq)
    x_out = _moe_block(h2, logits, lp, x1, g2, final_g, rows_per_mod=seq, final=final)

    xc_out = None
    if with_ctx_out:
        yf_c = y_bm[0, :, :n_ctx].reshape(rows_ctx, 256)
        yr_c = y_bm[1, :, :n_ctx].reshape(rows_ctx, 256)
        yb_c, yc_c, yd_c = _ctx_attention(na_c, sw_c, df_c, lp['swa_sink'], lqk, lp['diff_subln_g'],
                                          lambda_init, nbatch, n_ctx)
        x1c, h2c, logits_c = _merge(u_c, yf_c, yr_c, *merge_w, yb_c, yc_c, yd_c, gate_c, lp['w_branch'],
                                    lp['w_out'], xc2d, cg1, lp['norm2_g'], csc2, csh2, lp['router_w'],
                                    rows_per_mod=rows_ctx)
        xc_out = _moe_block(h2c, logits_c, lp, x1c, cg2, final_g, rows_per_mod=rows_ctx, final=False)
    return x_out, xc_out


def kernel(x, c, ctx, c_ctx, ada_w, ada_b, norm1_g, norm2_g, w_in, s5_lambda_re, s5_lambda_im, s5_log_step,
           s5_b_re, s5_b_im, s5_c_re, s5_c_im, s5_d, s5_glu_w, s5_glu_b, na_rpb, swa_sink, diff_lq1, diff_lk1,
           diff_lq2, diff_lk2, diff_subln_g, w_branch, w_out, router_w, router_b, exp_w_gate, exp_w_up,
           exp_w_down, sh_w_gate, sh_w_up, sh_w_down, final_g):
    nbatch, seq, d = x.shape
    n_ctx = ctx.shape[1]
    depth = ada_w.shape[0]
    assert d == D_MODEL and nbatch == 8
    stacked = dict(ada_w=ada_w, ada_b=ada_b, norm1_g=norm1_g, norm2_g=norm2_g, w_in=w_in,
                   s5_lambda_re=s5_lambda_re, s5_lambda_im=s5_lambda_im, s5_log_step=s5_log_step,
                   s5_b_re=s5_b_re, s5_b_im=s5_b_im, s5_c_re=s5_c_re, s5_c_im=s5_c_im, s5_d=s5_d,
                   s5_glu_w=s5_glu_w, s5_glu_b=s5_glu_b, na_rpb=na_rpb, swa_sink=swa_sink,
                   diff_lq1=diff_lq1, diff_lk1=diff_lk1, diff_lq2=diff_lq2, diff_lk2=diff_lk2,
                   diff_subln_g=diff_subln_g, w_branch=w_branch, w_out=w_out, router_w=router_w,
                   router_b=router_b, exp_w_gate=exp_w_gate, exp_w_up=exp_w_up, exp_w_down=exp_w_down,
                   sh_w_gate=sh_w_gate, sh_w_up=sh_w_up, sh_w_down=sh_w_down)
    tables = _rope_tables(seq)
    c16 = jnp.concatenate([c.astype(F32), c_ctx.reshape(1, d).astype(F32),
                           jnp.zeros((16 - nbatch - 1, d), F32)], axis=0)
    x2d = x.reshape(nbatch * seq, d).astype(F32)
    xc2d = ctx.reshape(nbatch * n_ctx, d).astype(F32)
    for l in range(depth):
        lp = {k: v[l] for k, v in stacked.items()}
        last = l == depth - 1
        x2d, xc2d = _layer(x2d, xc2d, c16, lp, l, tables, final_g, nbatch=nbatch, seq=seq, n_ctx=n_ctx,
                           with_ctx_out=not last, final=last)
    return x2d.reshape(nbatch, seq, d)
```

```python
import functools
import math

import numpy as np
import jax
import jax.numpy as jnp
from jax import lax
from jax.experimental import pallas as pl
from jax.experimental.pallas import tpu as pltpu

F32 = jnp.float32
BF16 = jnp.bfloat16
HIGHEST = lax.Precision.HIGHEST

GRID_W = 64
EPS = 1e-6
NEG_INF = -1e30
ROPE_BASE = 10000.0
D_MODEL = 1024
BRANCH_WIDTH = 256
HEAD_DIM = 64
S5_GROUP = 16
S5_GROUPS = 16
S5_STATE = 64
S5_FLAT = S5_GROUPS * S5_STATE
NA_HEADS = 4
NA_WIN_ROWS = 8
NA_WIN_COLS = 16
SWA_KV_HEADS = 2
SWA_WINDOW = 128
DIFF_HEADS = 4
DIFF_QK_DIM = 32
N_EXPERTS = 64
N_EXPERT_GROUPS = 8
TOPK_GROUPS = 4
TOP_K = 6
EXPERT_HIDDEN = 256
ROUTED_SCALE = 2.5
GATE_WIDTH = 4 * D_MODEL

VMEM_LIMIT = 56 * 1024 * 1024


def _params(*sem):
    return pltpu.CompilerParams(dimension_semantics=sem, vmem_limit_bytes=VMEM_LIMIT)


def _nt_dot(a, b):
    return lax.dot_general(a, b, (((1,), (1,)), ((), ())), preferred_element_type=F32)


def _dot(a, b):
    return jnp.dot(a, b, preferred_element_type=F32)


def _rms(x):
    return x * lax.rsqrt(jnp.mean(x * x, axis=-1, keepdims=True) + EPS)


def _ada_kernel(c_ref, w_ref, b_ref, o_ref):
    c = c_ref[...]
    s = c * jax.nn.sigmoid(c)
    o_ref[...] = jnp.dot(s, w_ref[...], preferred_element_type=F32, precision=HIGHEST) + b_ref[...]


def _ada_mod(cc, w, b):
    rows, d = cc.shape
    width = w.shape[1]
    tn = 1536
    return pl.pallas_call(
        _ada_kernel,
        grid=(width // tn,),
        in_specs=[pl.BlockSpec((rows, d), lambda j: (0, 0)),
                  pl.BlockSpec((d, tn), lambda j: (0, j)),
                  pl.BlockSpec((1, tn), lambda j: (0, j))],
        out_specs=pl.BlockSpec((rows, tn), lambda j: (0, j)),
        out_shape=jax.ShapeDtypeStruct((rows, width), F32),
        compiler_params=_params("arbitrary"),
        name="ada_mod",
    )(cc, w, b.reshape(1, width))


_C_GATE = 0
_C_U = GATE_WIDTH
_C_NA = _C_U + 256
_C_SW = _C_NA + 768
_C_DF = _C_SW + 512
_C_END = _C_DF + 768


def _rope_apply(x, cos, sins, half):
    outs = []
    for j in range(x.shape[1] // 128):
        xs = x[:, j * 128:(j + 1) * 128]
        lane = lax.broadcasted_iota(jnp.int32, xs.shape, 1)
        lo = (lane % (2 * half)) < half
        partner = jnp.where(lo, pltpu.roll(xs, 128 - half, 1), pltpu.roll(xs, half, 1))
        outs.append(xs * cos + partner * sins)
    return outs[0] if len(outs) == 1 else jnp.concatenate(outs, axis=1)


def _inproj_kernel(x_ref, g_ref, sc_ref, sh_ref, w_ref, c64_ref, s64_ref, c32_ref, s32_ref,
                   gate_o, u_o, na_o, sw_o, df_o, *, rope):
    h = _rms(x_ref[...]) * g_ref[...]
    h = h * (1.0 + sc_ref[0]) + sh_ref[0]
    hb = h.astype(BF16)

    def mm(c0, c1):
        return _dot(hb, w_ref[:, c0:c1])

    for k in range(GATE_WIDTH // 512):
        gate_o[:, k * 512:(k + 1) * 512] = jax.nn.sigmoid(mm(k * 512, (k + 1) * 512)).astype(BF16)
    u_o[...] = mm(_C_U, _C_U + 256)

    na = mm(_C_NA, _C_NA + 768)
    na_o[:, 0:256] = (na[:, 0:256] * (HEAD_DIM ** -0.5)).astype(BF16)
    na_o[:, 256:768] = na[:, 256:768].astype(BF16)

    sw = mm(_C_SW, _C_SW + 512)
    swq, swk = sw[:, 0:256], sw[:, 256:384]
    if rope:
        swq = _rope_apply(swq, c64_ref[...], s64_ref[...], 16)
        swk = _rope_apply(swk, c64_ref[...], s64_ref[...], 16)
    sw_o[:, 0:256] = (swq * (HEAD_DIM ** -0.5)).astype(BF16)
    sw_o[:, 256:384] = swk.astype(BF16)
    sw_o[:, 384:512] = sw[:, 384:512].astype(BF16)

    df = mm(_C_DF, _C_DF + 768)
    dfq, dfk = df[:, 0:256], df[:, 256:512]
    if rope:
        dfq = _rope_apply(dfq, c32_ref[...], s32_ref[...], 8)
        dfk = _rope_apply(dfk, c32_ref[...], s32_ref[...], 8)
    df_o[:, 0:256] = (dfq * (DIFF_QK_DIM ** -0.5 * (LOG2E if rope else 1.0))).astype(BF16)
    df_o[:, 256:512] = dfk.astype(BF16)
    df_o[:, 512:768] = df[:, 512:768].astype(BF16)


def _inproj(x2d, norm_g, sc, sh, w_bf16, tables, *, rows_per_mod, rope, seq):
    rows = x2d.shape[0]
    tm = 512
    assert rows % tm == 0 and rows_per_mod % tm == 0 and seq % tm == 0
    tiles_per_seq = seq // tm

    def mod_map(i):
        return ((i * tm) // rows_per_mod, 0, 0)

    def tab_map(i):
        return (i % tiles_per_seq, 0)

    tab_spec = pl.BlockSpec((tm, 128), tab_map)
    row = lambda w: pl.BlockSpec((tm, w), lambda i: (i, 0))
    return pl.pallas_call(
        functools.partial(_inproj_kernel, rope=rope),
        grid=(rows // tm,),
        in_specs=[row(D_MODEL),
                  pl.BlockSpec((1, D_MODEL), lambda i: (0, 0)),
                  pl.BlockSpec((1, 1, D_MODEL), mod_map),
                  pl.BlockSpec((1, 1, D_MODEL), mod_map),
                  pl.BlockSpec((D_MODEL, _C_END), lambda i: (0, 0)),
                  tab_spec, tab_spec, tab_spec, tab_spec],
        out_specs=[row(GATE_WIDTH), row(256), row(768), row(512), row(768)],
        out_shape=[jax.ShapeDtypeStruct((rows, GATE_WIDTH), BF16),
                   jax.ShapeDtypeStruct((rows, 256), F32),
                   jax.ShapeDtypeStruct((rows, 768), BF16),
                   jax.ShapeDtypeStruct((rows, 512), BF16),
                   jax.ShapeDtypeStruct((rows, 768), BF16)],
        compiler_params=_params("arbitrary"),
        name="inproj",
    )(x2d, norm_g.reshape(1, D_MODEL), sc, sh, w_bf16, *tables)


def _rope_tables(seq):
    t = jnp.arange(seq)
    rows = (t // GRID_W).astype(F32)
    cols = (t % GRID_W).astype(F32)
    lane = np.arange(128)
    out = []
    for dim in (64, 32):
        quarter = dim // 4
        inv_freq = ROPE_BASE ** (-jnp.arange(quarter, dtype=F32) / quarter)
        l = lane % dim
        use_col = l >= dim // 2
        fidx = l % quarter
        hi = (l % (dim // 2)) >= quarter
        ang_r = rows[:, None] * inv_freq[None, :]
        ang_c = cols[:, None] * inv_freq[None, :]
        ang = jnp.where(use_col[None, :], ang_c[:, fidx], ang_r[:, fidx])
        out.append(jnp.cos(ang))
        out.append(jnp.where(hi[None, :], jnp.sin(ang), -jnp.sin(ang)))
    return tuple(out)


S5_CHUNK = 128


def _s5_kernel(u_ref, win_ref, wout_ref, are_ref, aim_ref, y_ref, bu_ref, st_ref, *, tc, nb):
    d = pl.program_id(0)
    i = pl.program_id(1)

    @pl.when(i == 0)
    def _():
        st_ref[...] = jnp.zeros_like(st_ref)

    u = u_ref[...].reshape(tc * nb, BRANCH_WIDTH).astype(BF16)
    bu_ref[...] = _dot(u, win_ref[0])
    ar = jnp.broadcast_to(are_ref[0], (nb, S5_FLAT))
    ai = jnp.broadcast_to(aim_ref[0], (nb, S5_FLAT))

    def body(j, carry):
        xr, xi = carry
        t = j + d * (tc - 1 - 2 * j)
        row = pl.multiple_of(t * nb, nb)
        br = bu_ref[pl.ds(row, nb), 0:S5_FLAT]
        bi = bu_ref[pl.ds(row, nb), S5_FLAT:2 * S5_FLAT]
        nr = ar * xr - ai * xi + br
        ni = ar * xi + ai * xr + bi
        bu_ref[pl.ds(row, nb), 0:S5_FLAT] = nr
        bu_ref[pl.ds(row, nb), S5_FLAT:2 * S5_FLAT] = ni
        return nr, ni

    xr, xi = lax.fori_loop(0, tc, body, (st_ref[:, 0:S5_FLAT], st_ref[:, S5_FLAT:2 * S5_FLAT]), unroll=4)
    st_ref[:, 0:S5_FLAT] = xr
    st_ref[:, S5_FLAT:2 * S5_FLAT] = xi
    y = _dot(bu_ref[...].astype(BF16), wout_ref[0])
    y_ref[0] = y.reshape(tc, nb, BRANCH_WIDTH)


def _s5_scan(u_tm, win, wout, a_re, a_im, n_ctx):
    s_len, nb, _ = u_tm.shape
    tc = S5_CHUNK
    assert nb == 8 and s_len % tc == 0 and n_ctx % tc == 0
    nct = n_ctx // tc
    nlt = (s_len - n_ctx) // tc

    def chunk(d, i):
        rev = jnp.where(i < nct, nct - 1 - i, 2 * nct + nlt - 1 - i)
        return jnp.where(d == 0, i, rev)

    return pl.pallas_call(
        functools.partial(_s5_kernel, tc=tc, nb=nb),
        grid=(2, nct + nlt),
        in_specs=[pl.BlockSpec((tc, nb, BRANCH_WIDTH), lambda d, i: (chunk(d, i), 0, 0)),
                  pl.BlockSpec((1, BRANCH_WIDTH, 2 * S5_FLAT), lambda d, i: (d, 0, 0)),
                  pl.BlockSpec((1, 2 * S5_FLAT, BRANCH_WIDTH), lambda d, i: (d, 0, 0)),
                  pl.BlockSpec((1, 1, S5_FLAT), lambda d, i: (d, 0, 0)),
                  pl.BlockSpec((1, 1, S5_FLAT), lambda d, i: (d, 0, 0))],
        out_specs=pl.BlockSpec((1, tc, nb, BRANCH_WIDTH), lambda d, i: (d, chunk(d, i), 0, 0)),
        out_shape=jax.ShapeDtypeStruct((2, s_len, nb, BRANCH_WIDTH), F32),
        scratch_shapes=[pltpu.VMEM((tc * nb, 2 * S5_FLAT), F32),
                        pltpu.VMEM((nb, 2 * S5_FLAT), F32)],
        compiler_params=_params("arbitrary", "arbitrary"),
        name="s5_scan",
    )(u_tm, win, wout, a_re, a_im)


def _s5_params(lam_re, lam_im, log_step, b_re, b_im, c_re, c_im):
    lr = lam_re.astype(F32)
    li = lam_im.astype(F32)
    dt = jnp.exp(log_step.astype(F32))[..., None]
    mag = jnp.exp(lr * dt)
    a_re = mag * jnp.cos(li * dt)
    a_im = mag * jnp.sin(li * dt)
    nr, ni, den = a_re - 1.0, a_im, lr * lr + li * li
    k_re = ((nr * lr + ni * li) / den)[..., None]
    k_im = ((ni * lr - nr * li) / den)[..., None]
    br = b_re.astype(F32)
    bi = b_im.astype(F32)
    bb_re = k_re * br - k_im * bi
    bb_im = k_re * bi + k_im * br
    eye = jnp.eye(S5_GROUPS, dtype=F32)

    def blockdiag_in(bb):
        m = jnp.einsum('dgpc,gh->dgchp', bb, eye)
        return m.reshape(2, S5_GROUPS * S5_GROUP, S5_GROUPS * S5_STATE)

    def blockdiag_out(cc):
        m = jnp.einsum('dgcp,gh->dgphc', cc, eye)
        return m.reshape(2, S5_GROUPS * S5_STATE, S5_GROUPS * S5_GROUP)

    win = jnp.concatenate([blockdiag_in(bb_re), blockdiag_in(bb_im)], axis=2).astype(BF16)
    wout = jnp.concatenate([blockdiag_out(c_re.astype(F32)), -blockdiag_out(c_im.astype(F32))], axis=1).astype(BF16)
    return win, wout, a_re.reshape(2, 1, S5_FLAT), a_im.reshape(2, 1, S5_FLAT)


def _softmax_parts(scores, extra=None):
    m = scores[0].max(axis=-1, keepdims=True)
    for s in scores[1:]:
        m = jnp.maximum(m, s.max(axis=-1, keepdims=True))
    if extra is not None:
        m = jnp.maximum(m, extra)
    ps = [jnp.exp(s - m) for s in scores]
    l = ps[0].sum(axis=-1, keepdims=True)
    for p in ps[1:]:
        l = l + p.sum(axis=-1, keepdims=True)
    if extra is not None:
        l = l + jnp.exp(extra - m)
    return ps, l


def _na_kernel(q_ref, k_ref, v_ref, kc_ref, vc_ref, bias_ref, o_ref, *, grid_rows):
    r = pl.program_id(1)
    start = pl.multiple_of(jnp.clip(r - NA_WIN_ROWS // 2, 0, grid_rows - NA_WIN_ROWS) * GRID_W, GRID_W)
    nloc = NA_WIN_ROWS * GRID_W
    kw = k_ref[pl.ds(start, nloc), :]
    vw = v_ref[pl.ds(start, nloc), :]
    q = q_ref[...]
    kc = kc_ref[...]
    vc = vc_ref[...]
    outs = []
    for h in range(NA_HEADS):
        sl = slice(h * HEAD_DIM, (h + 1) * HEAD_DIM)
        qh = q[:, sl]
        s_loc = _nt_dot(qh, kw[:, sl]) + bias_ref[0, h]
        s_ctx = _nt_dot(qh, kc[:, sl])
        (p_loc, p_ctx), l = _softmax_parts([s_loc, s_ctx])
        o = _dot(p_loc.astype(BF16), vw[:, sl]) + _dot(p_ctx.astype(BF16), vc[:, sl])
        outs.append(o / l)
    o_ref[...] = jnp.concatenate(outs, axis=1).astype(BF16)


def _na_bias_table(rpb, grid_rows):
    cls_r = np.array([0, 1, 2, 3, 4, grid_rows - 3, grid_rows - 2, grid_rows - 1])
    start = np.clip(cls_r - NA_WIN_ROWS // 2, 0, grid_rows - NA_WIN_ROWS)
    delta = start[:, None] + np.arange(NA_WIN_ROWS)[None, :] - cls_r[:, None] + (NA_WIN_ROWS - 1)
    col = np.arange(GRID_W)
    cstart = np.clip(col - NA_WIN_COLS // 2, 0, GRID_W - NA_WIN_COLS)
    col_in = (col[None, :] >= cstart[:, None]) & (col[None, :] < cstart[:, None] + NA_WIN_COLS)
    cb = np.clip(col[None, :] - col[:, None] + (NA_WIN_COLS - 1), 0, 2 * NA_WIN_COLS - 2)
    oh_row = jnp.asarray(delta[:, :, None] == np.arange(2 * NA_WIN_ROWS - 1), F32)
    oh_col = jnp.asarray(cb[:, :, None] == np.arange(2 * NA_WIN_COLS - 1), F32)
    t = jnp.einsum('hab,cja,qkb->hcjqk', rpb.astype(F32), oh_row, oh_col, precision=HIGHEST)
    t = jnp.where(col_in[None, None, None], t, NEG_INF)
    return t.transpose(1, 0, 3, 2, 4).reshape(8, NA_HEADS, GRID_W, NA_WIN_ROWS * GRID_W)


def _na_attention(na, na_c, bias, nbatch, seq, n_ctx):
    grid_rows = seq // GRID_W
    assert grid_rows >= NA_WIN_ROWS

    def cls(r):
        return jnp.where(r < 4, r, jnp.where(r > grid_rows - 4, r - (grid_rows - 8), 4))

    return pl.pallas_call(
        functools.partial(_na_kernel, grid_rows=grid_rows),
        grid=(nbatch, grid_rows),
        in_specs=[pl.BlockSpec((GRID_W, 256), lambda b, r: (b * grid_rows + r, 0)),
                  pl.BlockSpec((seq, 256), lambda b, r: (b, 1)),
                  pl.BlockSpec((seq, 256), lambda b, r: (b, 2)),
                  pl.BlockSpec((n_ctx, 256), lambda b, r: (b, 1)),
                  pl.BlockSpec((n_ctx, 256), lambda b, r: (b, 2)),
                  pl.BlockSpec((1, NA_HEADS, GRID_W, NA_WIN_ROWS * GRID_W), lambda b, r: (cls(r), 0, 0, 0))],
        out_specs=pl.BlockSpec((GRID_W, 256), lambda b, r: (b * grid_rows + r, 0)),
        out_shape=jax.ShapeDtypeStruct((nbatch * seq, 256), BF16),
        compiler_params=_params("arbitrary", "arbitrary"),
        name="na_attention",
    )(na, na, na, na_c, na_c, bias)


SWA_BLOCK = 128


def _swa_kernel(q_ref, k_ref, v_ref, kc_ref, vc_ref, sink_ref, o_ref, *, seq):
    n = pl.program_id(1)
    band = 3 * SWA_BLOCK
    bstart = pl.multiple_of(jnp.clip((n - 1) * SWA_BLOCK, 0, seq - band), SWA_BLOCK)
    kb = k_ref[pl.ds(bstart, band), :]
    vb = v_ref[pl.ds(bstart, band), :]
    q = q_ref[...]
    kc = kc_ref[...]
    vc = vc_ref[...]
    rows = 2 * SWA_BLOCK
    qpos = n * SWA_BLOCK + lax.broadcasted_iota(jnp.int32, (rows, band), 0) % SWA_BLOCK
    kpos = bstart + lax.broadcasted_iota(jnp.int32, (rows, band), 1)
    mask = jnp.abs(qpos - kpos) <= SWA_WINDOW
    outs = [None] * 4
    for kv in range(SWA_KV_HEADS):
        sl = slice(kv * HEAD_DIM, (kv + 1) * HEAD_DIM)
        h0, h1 = 2 * kv, 2 * kv + 1
        qh = jnp.concatenate([q[:, h0 * HEAD_DIM:(h0 + 1) * HEAD_DIM],
                              q[:, h1 * HEAD_DIM:(h1 + 1) * HEAD_DIM]], axis=0)
        sk = jnp.concatenate([jnp.broadcast_to(sink_ref[0:1, h0:h0 + 1], (SWA_BLOCK, 1)),
                              jnp.broadcast_to(sink_ref[0:1, h1:h1 + 1], (SWA_BLOCK, 1))], axis=0)
        s_loc = jnp.where(mask, _nt_dot(qh, kb[:, sl]), NEG_INF)
        s_ctx = _nt_dot(qh, kc[:, sl])
        (p_loc, p_ctx), l = _softmax_parts([s_loc, s_ctx], extra=sk)
        o = (_dot(p_loc.astype(BF16), vb[:, sl]) + _dot(p_ctx.astype(BF16), vc[:, sl])) / l
        outs[h0] = o[0:SWA_BLOCK]
        outs[h1] = o[SWA_BLOCK:rows]
    o_ref[...] = jnp.concatenate(outs, axis=1).astype(BF16)


def _swa_attention(sw, sw_c, sink, nbatch, seq, n_ctx):
    nblk = seq // SWA_BLOCK
    assert seq >= 3 * SWA_BLOCK
    sink_pad = jnp.zeros((1, 128), F32).at[0, 0:4].set(sink.astype(F32))
    return pl.pallas_call(
        functools.partial(_swa_kernel, seq=seq),
        grid=(nbatch, nblk),
        in_specs=[pl.BlockSpec((SWA_BLOCK, 256), lambda b, n: (b * nblk + n, 0)),
                  pl.BlockSpec((seq, 128), lambda b, n: (b, 2)),
                  pl.BlockSpec((seq, 128), lambda b, n: (b, 3)),
                  pl.BlockSpec((n_ctx, 128), lambda b, n: (b, 2)),
                  pl.BlockSpec((n_ctx, 128), lambda b, n: (b, 3)),
                  pl.BlockSpec((1, 128), lambda b, n: (0, 0))],
        out_specs=pl.BlockSpec((SWA_BLOCK, 256), lambda b, n: (b * nblk + n, 0)),
        out_shape=jax.ShapeDtypeStruct((nbatch * seq, 256), BF16),
        compiler_params=_params("arbitrary", "arbitrary"),
        name="swa_attention",
    )(sw, sw, sw, sw_c, sw_c, sink_pad)


DIFF_TQ = 256
DIFF_CK = 256
LOG2E = math.log2(math.e)
DIFF_VROWS = HEAD_DIM + 16


def _diff_lambda(lq1_ref, lk1_ref, lq2_ref, lk2_ref, lambda_init):
    s1 = jnp.sum(lq1_ref[...] * lk1_ref[...], axis=-1, keepdims=True)
    s2 = jnp.sum(lq2_ref[...] * lk2_ref[...], axis=-1, keepdims=True)
    return jnp.exp(s1) - jnp.exp(s2) + lambda_init


def _stack_maps(qh):
    lane = lax.broadcasted_iota(jnp.int32, qh.shape, 1)
    zero = jnp.zeros_like(qh)
    return jnp.concatenate([jnp.where(lane < DIFF_QK_DIM, qh, zero),
                            jnp.where(lane >= DIFF_QK_DIM, qh, zero)], axis=0)


def _subln(o0, o1, lam, g, lambda_init):
    o = o0 - lam * o1
    return _rms(o) * g * (1.0 - lambda_init)


def _diff_kernel(qt_ref, k_ref, vt_ref, lq1_ref, lk1_ref, lq2_ref, lk2_ref, g_ref, o_ref, qbd_ref, acc_ref, s_ref,
                 *, nchunk, lambda_init):
    lam = _diff_lambda(lq1_ref, lk1_ref, lq2_ref, lk2_ref, lambda_init)
    qt = qt_ref[0]
    tq = qt.shape[1]
    w = 2 * tq
    row = lax.broadcasted_iota(jnp.int32, qt.shape, 0) // DIFF_QK_DIM
    zero = jnp.zeros_like(qt)
    for j in range(2 * DIFF_HEADS):
        qbd_ref[:, j * tq:(j + 1) * tq] = jnp.where(row == j, qt, zero)
    acc_ref[...] = jnp.zeros_like(acc_ref)

    def scores_into(slot, c):
        kblk = k_ref[0, c]
        for h in range(DIFF_HEADS):
            s_ref[slot, h] = _dot(kblk, qbd_ref[:, h * w:(h + 1) * w])

    def softmax_pv(slot, c, ms):
        vblk = vt_ref[0, c]
        new = []
        for h in range(DIFF_HEADS):
            s = s_ref[slot, h]
            m_new = jnp.maximum(ms[h], s.max(axis=0, keepdims=True))
            alpha = jnp.exp2(ms[h] - m_new)
            p = jnp.exp2((s - m_new).astype(BF16))
            acc_ref[h] = alpha * acc_ref[h] + _dot(vblk[h * DIFF_VROWS:(h + 1) * DIFF_VROWS, :], p)
            new.append(m_new)
        return tuple(new)

    def body(i, ms):
        c = 2 * i
        scores_into(1, c + 1)
        ms = softmax_pv(0, c, ms)
        scores_into(0, c + 2)
        return softmax_pv(1, c + 1, ms)

    ms = tuple(jnp.full((1, w), NEG_INF, F32) for _ in range(DIFF_HEADS))
    scores_into(0, 0)
    npair = (nchunk - 1) // 2
    ms = lax.fori_loop(0, npair, body, ms)
    if nchunk % 2 == 0:
        scores_into(1, nchunk - 1)
        ms = softmax_pv(0, nchunk - 2, ms)
        softmax_pv(1, nchunk - 1, ms)
    else:
        softmax_pv(0, nchunk - 1, ms)
    outs = []
    for h in range(DIFF_HEADS):
        o = acc_ref[h, 0:HEAD_DIM, :] / acc_ref[h, HEAD_DIM:HEAD_DIM + 1, :]
        d = o[:, 0:tq] - lam * o[:, tq:w]
        d = d * lax.rsqrt(jnp.mean(d * d, axis=0, keepdims=True) + EPS)
        outs.append(d * g_ref[...] * (1.0 - lambda_init))
    o_ref[...] = jnp.concatenate(outs, axis=0).T.astype(BF16)


def _diff_attention(df, df_c, lqk, subln_g, lambda_init, nbatch, seq, n_ctx):
    tq, ck = min(DIFF_TQ, seq), DIFF_CK
    s_all = seq + n_ctx
    assert s_all % ck == 0 and seq % tq == 0
    nchunk = s_all // ck
    nq = seq // tq
    dfl = df.reshape(nbatch, seq, 768)
    dfc = df_c.reshape(nbatch, n_ctx, 768)
    qt = dfl[:, :, 0:256].transpose(0, 2, 1)
    k_all = jnp.concatenate([dfl[:, :, 256:512], dfc[:, :, 256:512]], axis=1).reshape(nbatch, nchunk, ck, 256)
    v_all = jnp.concatenate([dfl[:, :, 512:768], dfc[:, :, 512:768]], axis=1)
    v_all = v_all.reshape(nbatch, nchunk, ck, DIFF_HEADS, HEAD_DIM)
    pad = jnp.zeros((nbatch, nchunk, ck, DIFF_HEADS, DIFF_VROWS - HEAD_DIM), BF16).at[..., 0].set(1.0)
    vt_all = jnp.concatenate([v_all, pad], axis=-1).reshape(nbatch, nchunk, ck, DIFF_HEADS * DIFF_VROWS)
    vt_all = vt_all.transpose(0, 1, 3, 2)
    vec = pl.BlockSpec((1, DIFF_QK_DIM), lambda b, n: (0, 0))
    return pl.pallas_call(
        functools.partial(_diff_kernel, nchunk=nchunk, lambda_init=lambda_init),
        grid=(nbatch, nq),
        in_specs=[pl.BlockSpec((1, 256, tq), lambda b, n: (b, 0, n)),
                  pl.BlockSpec((1, nchunk, ck, 256), lambda b, n: (b, 0, 0, 0)),
                  pl.BlockSpec((1, nchunk, DIFF_HEADS * DIFF_VROWS, ck), lambda b, n: (b, 0, 0, 0)),
                  vec, vec, vec, vec,
                  pl.BlockSpec((HEAD_DIM, 1), lambda b, n: (0, 0))],
        out_specs=pl.BlockSpec((tq, 256), lambda b, n: (b * nq + n, 0)),
        out_shape=jax.ShapeDtypeStruct((nbatch * seq, 256), BF16),
        scratch_shapes=[pltpu.VMEM((256, 2 * DIFF_HEADS * tq), BF16),
                        pltpu.VMEM((DIFF_HEADS, DIFF_VROWS, 2 * tq), F32),
                        pltpu.VMEM((2, DIFF_HEADS, ck, 2 * tq), F32)],
        compiler_params=_params("arbitrary", "arbitrary"),
        name="diff_attention",
    )(qt, k_all, vt_all, *lqk, subln_g.reshape(HEAD_DIM, 1).astype(F32))


def _ctx_attn_kernel(na_ref, sw_ref, df_ref, sink_ref, lq1_ref, lk1_ref, lq2_ref, lk2_ref, g_ref,
                     nb_o, sw_o, df_o, *, lambda_init):
    n = na_ref.shape[0]
    na = na_ref[...]
    outs = []
    for h in range(NA_HEADS):
        sl = slice(h * HEAD_DIM, (h + 1) * HEAD_DIM)
        (p,), l = _softmax_parts([_nt_dot(na[:, sl], na[:, 256 + h * HEAD_DIM:256 + (h + 1) * HEAD_DIM])])
        outs.append(_dot(p.astype(BF16), na[:, 512 + h * HEAD_DIM:512 + (h + 1) * HEAD_DIM]) / l)
    nb_o[...] = jnp.concatenate(outs, axis=1).astype(BF16)
    sw = sw_ref[...]
    outs = []
    for hq in range(4):
        kv = hq // 2
        k = sw[:, 256 + kv * HEAD_DIM:256 + (kv + 1) * HEAD_DIM]
        v = sw[:, 384 + kv * HEAD_DIM:384 + (kv + 1) * HEAD_DIM]
        sk = jnp.broadcast_to(sink_ref[0:1, hq:hq + 1], (n, 1))
        (p,), l = _softmax_parts([_nt_dot(sw[:, hq * HEAD_DIM:(hq + 1) * HEAD_DIM], k)], extra=sk)
        outs.append(_dot(p.astype(BF16), v) / l)
    sw_o[...] = jnp.concatenate(outs, axis=1).astype(BF16)
    lam = _diff_lambda(lq1_ref, lk1_ref, lq2_ref, lk2_ref, lambda_init)
    df = df_ref[...]
    outs = []
    for h in range(DIFF_HEADS):
        sl = slice(h * HEAD_DIM, (h + 1) * HEAD_DIM)
        q2 = _stack_maps(df[:, sl])
        (p,), l = _softmax_parts([_nt_dot(q2, df[:, 256 + h * HEAD_DIM:256 + (h + 1) * HEAD_DIM])])
        o = _dot(p.astype(BF16), df[:, 512 + h * HEAD_DIM:512 + (h + 1) * HEAD_DIM]) / l
        outs.append(_subln(o[0:n], o[n:2 * n], lam, g_ref[...], lambda_init))
    df_o[...] = jnp.concatenate(outs, axis=1).astype(BF16)


def _ctx_attention(na_c, sw_c, df_c, sink, lqk, subln_g, lambda_init, nbatch, n_ctx):
    sink_pad = jnp.zeros((1, 128), F32).at[0, 0:4].set(sink.astype(F32))
    vec = pl.BlockSpec((1, DIFF_QK_DIM), lambda b: (0, 0))
    out = jax.ShapeDtypeStruct((nbatch * n_ctx, 256), BF16)
    return pl.pallas_call(
        functools.partial(_ctx_attn_kernel, lambda_init=lambda_init),
        grid=(nbatch,),
        in_specs=[pl.BlockSpec((n_ctx, 768), lambda b: (b, 0)),
                  pl.BlockSpec((n_ctx, 512), lambda b: (b, 0)),
                  pl.BlockSpec((n_ctx, 768), lambda b: (b, 0)),
                  pl.BlockSpec((1, 128), lambda b: (0, 0)),
                  vec, vec, vec, vec,
                  pl.BlockSpec((1, HEAD_DIM), lambda b: (0, 0))],
        out_specs=[pl.BlockSpec((n_ctx, 256), lambda b: (b, 0))] * 3,
        out_shape=[out, out, out],
        compiler_params=_params("arbitrary"),
        name="ctx_attention",
    )(na_c, sw_c, df_c, sink_pad, *lqk, subln_g.reshape(1, HEAD_DIM).astype(F32))


def _merge_kernel(u_ref, yf_ref, yr_ref, d_ref, gw_ref, gb_ref, yb_ref, yc_ref, yd_ref, gate_ref,
                  wb_ref, wo_ref, x_ref, g1_ref, n2_ref, sc2_ref, sh2_ref, rw_ref,
                  x1_o, h2_o, lg_o):
    y = u_ref[...] * d_ref[...] + yf_ref[...] + yr_ref[...]
    a = jax.nn.gelu(y, approximate=True)
    ya = a * jax.nn.sigmoid(_dot(a.astype(BF16), gw_ref[...]) + gb_ref[...])
    branches = (ya.astype(BF16), yb_ref[...], yc_ref[...], yd_ref[...])
    acc = None
    for i in range(4):
        t = gate_ref[:, i * D_MODEL:(i + 1) * D_MODEL].astype(F32) * _dot(branches[i], wb_ref[i])
        acc = t if acc is None else acc + t
    mixed = _dot(acc.astype(BF16), wo_ref[...])
    x1 = x_ref[...] + g1_ref[0] * mixed
    x1_o[...] = x1
    h2 = _rms(x1) * n2_ref[...]
    h2 = h2 * (1.0 + sc2_ref[0]) + sh2_ref[0]
    h2_o[...] = h2.astype(BF16)
    lg_o[...] = jnp.dot(h2, rw_ref[...], preferred_element_type=F32, precision=HIGHEST)


def _merge(u, yf, yr, s5_d, glu_w, glu_b, yb, yc, yd, gate, wb, wo, x2d, g1, norm2_g, sc2, sh2, router_w,
           *, rows_per_mod):
    rows = x2d.shape[0]
    tm = 512
    assert rows % tm == 0 and rows_per_mod % tm == 0

    def mod_map(i):
        return ((i * tm) // rows_per_mod, 0, 0)

    row = lambda w: pl.BlockSpec((tm, w), lambda i: (i, 0))
    full = lambda *shape: pl.BlockSpec(shape, lambda i: (0,) * len(shape))
    mod = pl.BlockSpec((1, 1, D_MODEL), mod_map)
    return pl.pallas_call(
        _merge_kernel,
        grid=(rows // tm,),
        in_specs=[row(256), row(256), row(256), full(1, 256), full(256, 256), full(1, 256),
                  row(256), row(256), row(256), row(GATE_WIDTH),
                  full(4, 256, D_MODEL), full(D_MODEL, D_MODEL), row(D_MODEL),
                  mod, full(1, D_MODEL), mod, mod, full(D_MODEL, N_EXPERTS)],
        out_specs=[row(D_MODEL), row(D_MODEL), row(N_EXPERTS)],
        out_shape=[jax.ShapeDtypeStruct((rows, D_MODEL), F32),
                   jax.ShapeDtypeStruct((rows, D_MODEL), BF16),
                   jax.ShapeDtypeStruct((rows, N_EXPERTS), F32)],
        compiler_params=_params("arbitrary"),
        name="merge",
    )(u, yf, yr, s5_d.reshape(1, 256).astype(F32), glu_w.astype(BF16), glu_b.reshape(1, 256).astype(F32),
      yb, yc, yd, gate, wb.astype(BF16), wo.astype(BF16), x2d, g1, norm2_g.reshape(1, D_MODEL), sc2, sh2,
      router_w.astype(F32))


def _router_kernel(lg_ref, b_ref, cw_ref):
    tr = lg_ref.shape[1]
    gsz = N_EXPERTS // N_EXPERT_GROUPS
    sc = jax.nn.sigmoid(lg_ref[...])
    bi = sc + b_ref[...]
    e_iota = lax.broadcasted_iota(jnp.int32, (gsz, tr), 0).astype(F32)
    groups = [bi[g * gsz:(g + 1) * gsz] for g in range(N_EXPERT_GROUPS)]
    gs = []
    for bg in groups:
        m1 = bg.max(axis=0, keepdims=True)
        i1 = jnp.where(bg == m1, e_iota, float(gsz)).min(axis=0, keepdims=True)
        m2 = jnp.where(e_iota == i1, -jnp.inf, bg).max(axis=0, keepdims=True)
        gs.append(m1 + m2)
    v = []
    for g in range(N_EXPERT_GROUPS):
        rank = jnp.zeros((1, tr), F32)
        for g2 in range(N_EXPERT_GROUPS):
            if g2 == g:
                continue
            beats = (gs[g2] >= gs[g]) if g2 < g else (gs[g2] > gs[g])
            rank = rank + jnp.where(beats, 1.0, 0.0)
        v.append(jnp.where(rank < TOPK_GROUPS, groups[g], NEG_INF))
    flat = [e_iota + float(g * gsz) for g in range(N_EXPERT_GROUPS)]
    sel = [jnp.zeros((gsz, tr), F32) for _ in range(N_EXPERT_GROUPS)]
    for _ in range(TOP_K):
        m = v[0].max(axis=0, keepdims=True)
        for g in range(1, N_EXPERT_GROUPS):
            m = jnp.maximum(m, v[g].max(axis=0, keepdims=True))
        am = jnp.where(v[0] == m, flat[0], float(N_EXPERTS)).min(axis=0, keepdims=True)
        for g in range(1, N_EXPERT_GROUPS):
            am = jnp.minimum(am, jnp.where(v[g] == m, flat[g], float(N_EXPERTS)).min(axis=0, keepdims=True))
        for g in range(N_EXPERT_GROUPS):
            hit = flat[g] == am
            sel[g] = jnp.where(hit, 1.0, sel[g])
            v[g] = jnp.where(hit, -jnp.inf, v[g])
    ssel = [sel[g] * sc[g * gsz:(g + 1) * gsz] for g in range(N_EXPERT_GROUPS)]
    den = ssel[0].sum(axis=0, keepdims=True)
    for g in range(1, N_EXPERT_GROUPS):
        den = den + ssel[g].sum(axis=0, keepdims=True)
    for g in range(N_EXPERT_GROUPS):
        cw_ref[g * gsz:(g + 1) * gsz, :] = ssel[g] / den * ROUTED_SCALE


def _router(logits_t, router_b):
    ne, rows = logits_t.shape
    tr = 512
    assert rows % tr == 0
    return pl.pallas_call(
        _router_kernel,
        grid=(rows // tr,),
        in_specs=[pl.BlockSpec((ne, tr), lambda i: (0, i)),
                  pl.BlockSpec((ne, 1), lambda i: (0, 0))],
        out_specs=pl.BlockSpec((ne, tr), lambda i: (0, i)),
        out_shape=jax.ShapeDtypeStruct((ne, rows), F32),
        compiler_params=_params("arbitrary"),
        name="router",
    )(logits_t, router_b.reshape(ne, 1).astype(F32))


def _moe_kernel(h_ref, cw_ref, wgu_ref, wd_ref, x1_ref, g2_ref, fg_ref, o_ref, acc_ref, *, n_exp, final):
    e = pl.program_id(1)

    @pl.when(e == 0)
    def _():
        acc_ref[...] = jnp.zeros_like(acc_ref)

    hgu = _dot(h_ref[...], wgu_ref[0])
    g = hgu[:, 0:EXPERT_HIDDEN]
    a = g * jax.nn.sigmoid(g) * hgu[:, EXPERT_HIDDEN:2 * EXPERT_HIDDEN]
    cw = cw_ref[...]
    lane = lax.broadcasted_iota(jnp.int32, cw.shape, 1)
    w = jnp.sum(jnp.where(lane == e, cw, 0.0), axis=1, keepdims=True)
    acc_ref[...] += _dot((a * w).astype(BF16), wd_ref[0])

    @pl.when(e == n_exp - 1)
    def _():
        x2 = x1_ref[...] + g2_ref[0] * acc_ref[...]
        if final:
            x2 = _rms(x2) * fg_ref[...]
        o_ref[...] = x2


def _moe(h2, cw_ext, wgu, wd, x1, g2, final_g, *, rows_per_mod, final):
    rows = h2.shape[0]
    n_exp = wgu.shape[0]
    tm = min(1024, rows_per_mod)
    assert rows % tm == 0 and rows_per_mod % tm == 0

    def mod_map(i, e):
        return ((i * tm) // rows_per_mod, 0, 0)

    return pl.pallas_call(
        functools.partial(_moe_kernel, n_exp=n_exp, final=final),
        grid=(rows // tm, n_exp),
        in_specs=[pl.BlockSpec((tm, D_MODEL), lambda i, e: (i, 0)),
                  pl.BlockSpec((tm, 128), lambda i, e: (i, 0)),
                  pl.BlockSpec((1, D_MODEL, 2 * EXPERT_HIDDEN), lambda i, e: (e, 0, 0)),
                  pl.BlockSpec((1, EXPERT_HIDDEN, D_MODEL), lambda i, e: (e, 0, 0)),
                  pl.BlockSpec((tm, D_MODEL), lambda i, e: (i, 0)),
                  pl.BlockSpec((1, 1, D_MODEL), mod_map),
                  pl.BlockSpec((1, D_MODEL), lambda i, e: (0, 0))],
        out_specs=pl.BlockSpec((tm, D_MODEL), lambda i, e: (i, 0)),
        out_shape=jax.ShapeDtypeStruct((rows, D_MODEL), F32),
        scratch_shapes=[pltpu.VMEM((tm, D_MODEL), F32)],
        compiler_params=_params("arbitrary", "arbitrary"),
        name="moe_experts",
    )(h2, cw_ext, wgu, wd, x1, g2, final_g.reshape(1, D_MODEL).astype(F32))


def _reorder_w_in(w_in):
    split = 256 + 768 + 512 + 768
    return jnp.concatenate([w_in[:, split:], w_in[:, :split]], axis=1).astype(BF16)


def _mods(mod_row_block):
    return [mod_row_block[:, None, k * D_MODEL:(k + 1) * D_MODEL] for k in range(6)]


def _moe_block(h2, logits, lp, x1, g2, final_g, *, rows_per_mod, final):
    cw_t = _router(logits.T, lp['router_b'])
    rows = h2.shape[0]
    cw_ext = jnp.concatenate([cw_t.T, jnp.ones((rows, 1), F32), jnp.zeros((rows, 127 - N_EXPERTS), F32)], axis=1)
    wgu = jnp.concatenate([jnp.concatenate([lp['exp_w_gate'], lp['exp_w_up']], axis=2),
                           jnp.concatenate([lp['sh_w_gate'], lp['sh_w_up']], axis=1)[None]], axis=0).astype(BF16)
    wd = jnp.concatenate([lp['exp_w_down'], lp['sh_w_down'][None]], axis=0).astype(BF16)
    return _moe(h2, cw_ext, wgu, wd, x1, g2, final_g, rows_per_mod=rows_per_mod, final=final)


def _layer(x2d, xc2d, c16, lp, layer_idx, tables, final_g, *, nbatch, seq, n_ctx, with_ctx_out, final):
    lambda_init = 0.8 - 0.6 * math.exp(-0.3 * layer_idx)
    mod = _ada_mod(c16, lp['ada_w'].astype(F32), lp['ada_b'].astype(F32))
    sh1, sc1, g1, sh2, sc2, g2 = _mods(mod[0:nbatch])
    csh1, csc1, cg1, csh2, csc2, cg2 = _mods(mod[nbatch:nbatch + 1])
    w_in = _reorder_w_in(lp['w_in'])
    rows_lat = nbatch * seq
    rows_ctx = nbatch * n_ctx

    gate, u, na, sw, df = _inproj(x2d, lp['norm1_g'], sc1, sh1, w_in, tables,
                                  rows_per_mod=seq, rope=True, seq=seq)
    gate_c, u_c, na_c, sw_c, df_c = _inproj(xc2d, lp['norm1_g'], csc1, csh1, w_in, tables,
                                            rows_per_mod=rows_ctx, rope=False, seq=seq)

    win, wout, a_re, a_im = _s5_params(lp['s5_lambda_re'], lp['s5_lambda_im'], lp['s5_log_step'],
                                       lp['s5_b_re'], lp['s5_b_im'], lp['s5_c_re'], lp['s5_c_im'])
    u_tm = jnp.concatenate([u_c.reshape(nbatch, n_ctx, 256).transpose(1, 0, 2),
                            u.reshape(nbatch, seq, 256).transpose(1, 0, 2)], axis=0)
    y_tm = _s5_scan(u_tm, win, wout, a_re, a_im, n_ctx)
    y_bm = y_tm.transpose(0, 2, 1, 3)
    yf = y_bm[0, :, n_ctx:].reshape(rows_lat, 256)
    yr = y_bm[1, :, n_ctx:].reshape(rows_lat, 256)

    lqk = [lp[k].reshape(1, DIFF_QK_DIM).astype(F32) for k in ('diff_lq1', 'diff_lk1', 'diff_lq2', 'diff_lk2')]
    bias = _na_bias_table(lp['na_rpb'], seq // GRID_W)
    yb = _na_attention(na, na_c, bias, nbatch, seq, n_ctx)
    yc = _swa_attention(sw, sw_c, lp['swa_sink'], nbatch, seq, n_ctx)
    yd = _diff_attention(df, df_c, lqk, lp['diff_subln_g'], lambda_init, nbatch, seq, n_ctx)

    merge_w = (lp['s5_d'], lp['s5_glu_w'], lp['s5_glu_b'])
    x1, h2, logits = _merge(u, yf, yr, *merge_w, yb, yc, yd, gate, lp['w_branch'], lp['w_out'], x2d, g1,
                            lp['norm2_g'], sc2, sh2, lp['router_w'], rows_per_mod=seq)
    x_out = _moe_block(h2, logits, lp, x1, g2, final_g, rows_per_mod=seq, final=final)

    xc_out = None
    if with_ctx_out:
        yf_c = y_bm[0, :, :n_ctx].reshape(rows_ctx, 256)
        yr_c = y_bm[1, :, :n_ctx].reshape(rows_ctx, 256)
        yb_c, yc_c, yd_c = _ctx_attention(na_c, sw_c, df_c, lp['swa_sink'], lqk, lp['diff_subln_g'],
                                          lambda_init, nbatch, n_ctx)
        x1c, h2c, logits_c = _merge(u_c, yf_c, yr_c, *merge_w, yb_c, yc_c, yd_c, gate_c, lp['w_branch'],
                                    lp['w_out'], xc2d, cg1, lp['norm2_g'], csc2, csh2, lp['router_w'],
                                    rows_per_mod=rows_ctx)
        xc_out = _moe_block(h2c, logits_c, lp, x1c, cg2, final_g, rows_per_mod=rows_ctx, final=False)
    return x_out, xc_out


def kernel(x, c, ctx, c_ctx, ada_w, ada_b, norm1_g, norm2_g, w_in, s5_lambda_re, s5_lambda_im, s5_log_step,
           s5_b_re, s5_b_im, s5_c_re, s5_c_im, s5_d, s5_glu_w, s5_glu_b, na_rpb, swa_sink, diff_lq1, diff_lk1,
           diff_lq2, diff_lk2, diff_subln_g, w_branch, w_out, router_w, router_b, exp_w_gate, exp_w_up,
           exp_w_down, sh_w_gate, sh_w_up, sh_w_down, final_g):
    nbatch, seq, d = x.shape
    n_ctx = ctx.shape[1]
    depth = ada_w.shape[0]
    assert d == D_MODEL and nbatch == 8
    stacked = dict(ada_w=ada_w, ada_b=ada_b, norm1_g=norm1_g, norm2_g=norm2_g, w_in=w_in,
                   s5_lambda_re=s5_lambda_re, s5_lambda_im=s5_lambda_im, s5_log_step=s5_log_step,
                   s5_b_re=s5_b_re, s5_b_im=s5_b_im, s5_c_re=s5_c_re, s5_c_im=s5_c_im, s5_d=s5_d,
                   s5_glu_w=s5_glu_w, s5_glu_b=s5_glu_b, na_rpb=na_rpb, swa_sink=swa_sink,
                   diff_lq1=diff_lq1, diff_lk1=diff_lk1, diff_lq2=diff_lq2, diff_lk2=diff_lk2,
                   diff_subln_g=diff_subln_g, w_branch=w_branch, w_out=w_out, router_w=router_w,
                   router_b=router_b, exp_w_gate=exp_w_gate, exp_w_up=exp_w_up, exp_w_down=exp_w_down,
                   sh_w_gate=sh_w_gate, sh_w_up=sh_w_up, sh_w_down=sh_w_down)
    tables = _rope_tables(seq)
    c16 = jnp.concatenate([c.astype(F32), c_ctx.reshape(1, d).astype(F32),
                           jnp.zeros((16 - nbatch - 1, d), F32)], axis=0)
    x2d = x.reshape(nbatch * seq, d).astype(F32)
    xc2d = ctx.reshape(nbatch * n_ctx, d).astype(F32)
    for l in range(depth):
        lp = {k: v[l] for k, v in stacked.items()}
        last = l == depth - 1
        x2d, xc2d = _layer(x2d, xc2d, c16, lp, l, tables, final_g, nbatch=nbatch, seq=seq, n_ctx=n_ctx,
                           with_ctx_out=not last, final=last)
    return x2d.reshape(nbatch, seq, d)
```

```python
import functools
import math

import numpy as np
import jax
import jax.numpy as jnp
from jax import lax
from jax.experimental import pallas as pl
from jax.experimental.pallas import tpu as pltpu

F32 = jnp.float32
BF16 = jnp.bfloat16
HIGHEST = lax.Precision.HIGHEST

GRID_W = 64
EPS = 1e-6
NEG_INF = -1e30
ROPE_BASE = 10000.0
D_MODEL = 1024
BRANCH_WIDTH = 256
HEAD_DIM = 64
S5_GROUP = 16
S5_GROUPS = 16
S5_STATE = 64
S5_FLAT = S5_GROUPS * S5_STATE
NA_HEADS = 4
NA_WIN_ROWS = 8
NA_WIN_COLS = 16
SWA_KV_HEADS = 2
SWA_WINDOW = 128
DIFF_HEADS = 4
DIFF_QK_DIM = 32
N_EXPERTS = 64
N_EXPERT_GROUPS = 8
TOPK_GROUPS = 4
TOP_K = 6
EXPERT_HIDDEN = 256
ROUTED_SCALE = 2.5
GATE_WIDTH = 4 * D_MODEL

VMEM_LIMIT = 56 * 1024 * 1024


def _params(*sem):
    return pltpu.CompilerParams(dimension_semantics=sem, vmem_limit_bytes=VMEM_LIMIT)


def _nt_dot(a, b):
    return lax.dot_general(a, b, (((1,), (1,)), ((), ())), preferred_element_type=F32)


def _dot(a, b):
    return jnp.dot(a, b, preferred_element_type=F32)


def _rms(x):
    return x * lax.rsqrt(jnp.mean(x * x, axis=-1, keepdims=True) + EPS)


def _ada_kernel(c_ref, w_ref, b_ref, o_ref):
    c = c_ref[...]
    s = c * jax.nn.sigmoid(c)
    o_ref[...] = jnp.dot(s, w_ref[...], preferred_element_type=F32, precision=HIGHEST) + b_ref[...]


def _ada_mod(cc, w, b):
    rows, d = cc.shape
    width = w.shape[1]
    tn = 1536
    return pl.pallas_call(
        _ada_kernel,
        grid=(width // tn,),
        in_specs=[pl.BlockSpec((rows, d), lambda j: (0, 0)),
                  pl.BlockSpec((d, tn), lambda j: (0, j)),
                  pl.BlockSpec((1, tn), lambda j: (0, j))],
        out_specs=pl.BlockSpec((rows, tn), lambda j: (0, j)),
        out_shape=jax.ShapeDtypeStruct((rows, width), F32),
        compiler_params=_params("arbitrary"),
        name="ada_mod",
    )(cc, w, b.reshape(1, width))


_C_GATE = 0
_C_U = GATE_WIDTH
_C_NA = _C_U + 256
_C_SW = _C_NA + 768
_C_DF = _C_SW + 512
_C_END = _C_DF + 768


def _rope_apply(x, cos, sins, half):
    outs = []
    for j in range(x.shape[1] // 128):
        xs = x[:, j * 128:(j + 1) * 128]
        lane = lax.broadcasted_iota(jnp.int32, xs.shape, 1)
        lo = (lane % (2 * half)) < half
        partner = jnp.where(lo, pltpu.roll(xs, 128 - half, 1), pltpu.roll(xs, half, 1))
        outs.append(xs * cos + partner * sins)
    return outs[0] if len(outs) == 1 else jnp.concatenate(outs, axis=1)


def _inproj_kernel(x_ref, g_ref, sc_ref, sh_ref, w_ref, c64_ref, s64_ref, c32_ref, s32_ref,
                   gate_o, u_o, na_o, sw_o, df_o, *, rope):
    h = _rms(x_ref[...]) * g_ref[...]
    h = h * (1.0 + sc_ref[0]) + sh_ref[0]
    hb = h.astype(BF16)

    def mm(c0, c1):
        return _dot(hb, w_ref[:, c0:c1])

    for k in range(GATE_WIDTH // 512):
        gate_o[:, k * 512:(k + 1) * 512] = jax.nn.sigmoid(mm(k * 512, (k + 1) * 512)).astype(BF16)
    u_o[...] = mm(_C_U, _C_U + 256)

    na = mm(_C_NA, _C_NA + 768)
    na_o[:, 0:256] = (na[:, 0:256] * (HEAD_DIM ** -0.5)).astype(BF16)
    na_o[:, 256:768] = na[:, 256:768].astype(BF16)

    sw = mm(_C_SW, _C_SW + 512)
    swq, swk = sw[:, 0:256], sw[:, 256:384]
    if rope:
        swq = _rope_apply(swq, c64_ref[...], s64_ref[...], 16)
        swk = _rope_apply(swk, c64_ref[...], s64_ref[...], 16)
    sw_o[:, 0:256] = (swq * (HEAD_DIM ** -0.5)).astype(BF16)
    sw_o[:, 256:384] = swk.astype(BF16)
    sw_o[:, 384:512] = sw[:, 384:512].astype(BF16)

    df = mm(_C_DF, _C_DF + 768)
    dfq, dfk = df[:, 0:256], df[:, 256:512]
    if rope:
        dfq = _rope_apply(dfq, c32_ref[...], s32_ref[...], 8)
        dfk = _rope_apply(dfk, c32_ref[...], s32_ref[...], 8)
    df_o[:, 0:256] = (dfq * (DIFF_QK_DIM ** -0.5 * (LOG2E if rope else 1.0))).astype(BF16)
    df_o[:, 256:512] = dfk.astype(BF16)
    df_o[:, 512:768] = df[:, 512:768].astype(BF16)


def _inproj(x2d, norm_g, sc, sh, w_bf16, tables, *, rows_per_mod, rope, seq):
    rows = x2d.shape[0]
    tm = 512
    assert rows % tm == 0 and rows_per_mod % tm == 0 and seq % tm == 0
    tiles_per_seq = seq // tm

    def mod_map(i):
        return ((i * tm) // rows_per_mod, 0, 0)

    def tab_map(i):
        return (i % tiles_per_seq, 0)

    tab_spec = pl.BlockSpec((tm, 128), tab_map)
    row = lambda w: pl.BlockSpec((tm, w), lambda i: (i, 0))
    return pl.pallas_call(
        functools.partial(_inproj_kernel, rope=rope),
        grid=(rows // tm,),
        in_specs=[row(D_MODEL),
                  pl.BlockSpec((1, D_MODEL), lambda i: (0, 0)),
                  pl.BlockSpec((1, 1, D_MODEL), mod_map),
                  pl.BlockSpec((1, 1, D_MODEL), mod_map),
                  pl.BlockSpec((D_MODEL, _C_END), lambda i: (0, 0)),
                  tab_spec, tab_spec, tab_spec, tab_spec],
        out_specs=[row(GATE_WIDTH), row(256), row(768), row(512), row(768)],
        out_shape=[jax.ShapeDtypeStruct((rows, GATE_WIDTH), BF16),
                   jax.ShapeDtypeStruct((rows, 256), F32),
                   jax.ShapeDtypeStruct((rows, 768), BF16),
                   jax.ShapeDtypeStruct((rows, 512), BF16),
                   jax.ShapeDtypeStruct((rows, 768), BF16)],
        compiler_params=_params("arbitrary"),
        name="inproj",
    )(x2d, norm_g.reshape(1, D_MODEL), sc, sh, w_bf16, *tables)


def _rope_tables(seq):
    t = jnp.arange(seq)
    rows = (t // GRID_W).astype(F32)
    cols = (t % GRID_W).astype(F32)
    lane = np.arange(128)
    out = []
    for dim in (64, 32):
        quarter = dim // 4
        inv_freq = ROPE_BASE ** (-jnp.arange(quarter, dtype=F32) / quarter)
        l = lane % dim
        use_col = l >= dim // 2
        fidx = l % quarter
        hi = (l % (dim // 2)) >= quarter
        ang_r = rows[:, None] * inv_freq[None, :]
        ang_c = cols[:, None] * inv_freq[None, :]
        ang = jnp.where(use_col[None, :], ang_c[:, fidx], ang_r[:, fidx])
        out.append(jnp.cos(ang))
        out.append(jnp.where(hi[None, :], jnp.sin(ang), -jnp.sin(ang)))
    return tuple(out)


S5_CHUNK = 128


def _s5_kernel(u_ref, win_ref, wout_ref, are_ref, aim_ref, y_ref, bu_ref, st_ref, *, tc, nb):
    d = pl.program_id(0)
    i = pl.program_id(1)

    @pl.when(i == 0)
    def _():
        st_ref[...] = jnp.zeros_like(st_ref)

    u = u_ref[...].reshape(tc * nb, BRANCH_WIDTH).astype(BF16)
    bu_ref[...] = _dot(u, win_ref[0])
    ar = jnp.broadcast_to(are_ref[0], (nb, S5_FLAT))
    ai = jnp.broadcast_to(aim_ref[0], (nb, S5_FLAT))

    def body(j, carry):
        xr, xi = carry
        t = j + d * (tc - 1 - 2 * j)
        row = pl.multiple_of(t * nb, nb)
        br = bu_ref[pl.ds(row, nb), 0:S5_FLAT]
        bi = bu_ref[pl.ds(row, nb), S5_FLAT:2 * S5_FLAT]
        nr = ar * xr - ai * xi + br
        ni = ar * xi + ai * xr + bi
        bu_ref[pl.ds(row, nb), 0:S5_FLAT] = nr
        bu_ref[pl.ds(row, nb), S5_FLAT:2 * S5_FLAT] = ni
        return nr, ni

    xr, xi = lax.fori_loop(0, tc, body, (st_ref[:, 0:S5_FLAT], st_ref[:, S5_FLAT:2 * S5_FLAT]), unroll=4)
    st_ref[:, 0:S5_FLAT] = xr
    st_ref[:, S5_FLAT:2 * S5_FLAT] = xi
    y = _dot(bu_ref[...].astype(BF16), wout_ref[0])
    y_ref[0] = y.reshape(tc, nb, BRANCH_WIDTH)


def _s5_scan(u_tm, win, wout, a_re, a_im, n_ctx):
    s_len, nb, _ = u_tm.shape
    tc = S5_CHUNK
    assert nb == 8 and s_len % tc == 0 and n_ctx % tc == 0
    nct = n_ctx // tc
    nlt = (s_len - n_ctx) // tc

    def chunk(d, i):
        rev = jnp.where(i < nct, nct - 1 - i, 2 * nct + nlt - 1 - i)
        return jnp.where(d == 0, i, rev)

    return pl.pallas_call(
        functools.partial(_s5_kernel, tc=tc, nb=nb),
        grid=(2, nct + nlt),
        in_specs=[pl.BlockSpec((tc, nb, BRANCH_WIDTH), lambda d, i: (chunk(d, i), 0, 0)),
                  pl.BlockSpec((1, BRANCH_WIDTH, 2 * S5_FLAT), lambda d, i: (d, 0, 0)),
                  pl.BlockSpec((1, 2 * S5_FLAT, BRANCH_WIDTH), lambda d, i: (d, 0, 0)),
                  pl.BlockSpec((1, 1, S5_FLAT), lambda d, i: (d, 0, 0)),
                  pl.BlockSpec((1, 1, S5_FLAT), lambda d, i: (d, 0, 0))],
        out_specs=pl.BlockSpec((1, tc, nb, BRANCH_WIDTH), lambda d, i: (d, chunk(d, i), 0, 0)),
        out_shape=jax.ShapeDtypeStruct((2, s_len, nb, BRANCH_WIDTH), F32),
        scratch_shapes=[pltpu.VMEM((tc * nb, 2 * S5_FLAT), F32),
                        pltpu.VMEM((nb, 2 * S5_FLAT), F32)],
        compiler_params=_params("arbitrary", "arbitrary"),
        name="s5_scan",
    )(u_tm, win, wout, a_re, a_im)


def _s5_params(lam_re, lam_im, log_step, b_re, b_im, c_re, c_im):
    lr = lam_re.astype(F32)
    li = lam_im.astype(F32)
    dt = jnp.exp(log_step.astype(F32))[..., None]
    mag = jnp.exp(lr * dt)
    a_re = mag * jnp.cos(li * dt)
    a_im = mag * jnp.sin(li * dt)
    nr, ni, den = a_re - 1.0, a_im, lr * lr + li * li
    k_re = ((nr * lr + ni * li) / den)[..., None]
    k_im = ((ni * lr - nr * li) / den)[..., None]
    br = b_re.astype(F32)
    bi = b_im.astype(F32)
    bb_re = k_re * br - k_im * bi
    bb_im = k_re * bi + k_im * br
    eye = jnp.eye(S5_GROUPS, dtype=F32)

    def blockdiag_in(bb):
        m = jnp.einsum('dgpc,gh->dgchp', bb, eye)
        return m.reshape(2, S5_GROUPS * S5_GROUP, S5_GROUPS * S5_STATE)

    def blockdiag_out(cc):
        m = jnp.einsum('dgcp,gh->dgphc', cc, eye)
        return m.reshape(2, S5_GROUPS * S5_STATE, S5_GROUPS * S5_GROUP)

    win = jnp.concatenate([blockdiag_in(bb_re), blockdiag_in(bb_im)], axis=2).astype(BF16)
    wout = jnp.concatenate([blockdiag_out(c_re.astype(F32)), -blockdiag_out(c_im.astype(F32))], axis=1).astype(BF16)
    return win, wout, a_re.reshape(2, 1, S5_FLAT), a_im.reshape(2, 1, S5_FLAT)


def _softmax_parts(scores, extra=None):
    m = scores[0].max(axis=-1, keepdims=True)
    for s in scores[1:]:
        m = jnp.maximum(m, s.max(axis=-1, keepdims=True))
    if extra is not None:
        m = jnp.maximum(m, extra)
    ps = [jnp.exp(s - m) for s in scores]
    l = ps[0].sum(axis=-1, keepdims=True)
    for p in ps[1:]:
        l = l + p.sum(axis=-1, keepdims=True)
    if extra is not None:
        l = l + jnp.exp(extra - m)
    return ps, l


def _na_kernel(q_ref, k_ref, v_ref, kc_ref, vc_ref, bias_ref, o_ref, *, grid_rows):
    r = pl.program_id(1)
    start = pl.multiple_of(jnp.clip(r - NA_WIN_ROWS // 2, 0, grid_rows - NA_WIN_ROWS) * GRID_W, GRID_W)
    nloc = NA_WIN_ROWS * GRID_W
    kw = k_ref[pl.ds(start, nloc), :]
    vw = v_ref[pl.ds(start, nloc), :]
    q = q_ref[...]
    kc = kc_ref[...]
    vc = vc_ref[...]
    outs = []
    for h in range(NA_HEADS):
        sl = slice(h * HEAD_DIM, (h + 1) * HEAD_DIM)
        qh = q[:, sl]
        s_loc = _nt_dot(qh, kw[:, sl]) + bias_ref[0, h]
        s_ctx = _nt_dot(qh, kc[:, sl])
        (p_loc, p_ctx), l = _softmax_parts([s_loc, s_ctx])
        o = _dot(p_loc.astype(BF16), vw[:, sl]) + _dot(p_ctx.astype(BF16), vc[:, sl])
        outs.append(o / l)
    o_ref[...] = jnp.concatenate(outs, axis=1).astype(BF16)


def _na_bias_table(rpb, grid_rows):
    cls_r = np.array([0, 1, 2, 3, 4, grid_rows - 3, grid_rows - 2, grid_rows - 1])
    start = np.clip(cls_r - NA_WIN_ROWS // 2, 0, grid_rows - NA_WIN_ROWS)
    delta = start[:, None] + np.arange(NA_WIN_ROWS)[None, :] - cls_r[:, None] + (NA_WIN_ROWS - 1)
    col = np.arange(GRID_W)
    cstart = np.clip(col - NA_WIN_COLS // 2, 0, GRID_W - NA_WIN_COLS)
    col_in = (col[None, :] >= cstart[:, None]) & (col[None, :] < cstart[:, None] + NA_WIN_COLS)
    cb = np.clip(col[None, :] - col[:, None] + (NA_WIN_COLS - 1), 0, 2 * NA_WIN_COLS - 2)
    oh_row = jnp.asarray(delta[:, :, None] == np.arange(2 * NA_WIN_ROWS - 1), F32)
    oh_col = jnp.asarray(cb[:, :, None] == np.arange(2 * NA_WIN_COLS - 1), F32)
    t = jnp.einsum('hab,cja,qkb->hcjqk', rpb.astype(F32), oh_row, oh_col, precision=HIGHEST)
    t = jnp.where(col_in[None, None, None], t, NEG_INF)
    return t.transpose(1, 0, 3, 2, 4).reshape(8, NA_HEADS, GRID_W, NA_WIN_ROWS * GRID_W)


def _na_attention(na, na_c, bias, nbatch, seq, n_ctx):
    grid_rows = seq // GRID_W
    assert grid_rows >= NA_WIN_ROWS

    def cls(r):
        return jnp.where(r < 4, r, jnp.where(r > grid_rows - 4, r - (grid_rows - 8), 4))

    return pl.pallas_call(
        functools.partial(_na_kernel, grid_rows=grid_rows),
        grid=(nbatch, grid_rows),
        in_specs=[pl.BlockSpec((GRID_W, 256), lambda b, r: (b * grid_rows + r, 0)),
                  pl.BlockSpec((seq, 256), lambda b, r: (b, 1)),
                  pl.BlockSpec((seq, 256), lambda b, r: (b, 2)),
                  pl.BlockSpec((n_ctx, 256), lambda b, r: (b, 1)),
                  pl.BlockSpec((n_ctx, 256), lambda b, r: (b, 2)),
                  pl.BlockSpec((1, NA_HEADS, GRID_W, NA_WIN_ROWS * GRID_W), lambda b, r: (cls(r), 0, 0, 0))],
        out_specs=pl.BlockSpec((GRID_W, 256), lambda b, r: (b * grid_rows + r, 0)),
        out_shape=jax.ShapeDtypeStruct((nbatch * seq, 256), BF16),
        compiler_params=_params("arbitrary", "arbitrary"),
        name="na_attention",
    )(na, na, na, na_c, na_c, bias)


SWA_BLOCK = 128


def _swa_kernel(q_ref, k_ref, v_ref, kc_ref, vc_ref, sink_ref, o_ref, *, seq):
    n = pl.program_id(1)
    band = 3 * SWA_BLOCK
    bstart = pl.multiple_of(jnp.clip((n - 1) * SWA_BLOCK, 0, seq - band), SWA_BLOCK)
    kb = k_ref[pl.ds(bstart, band), :]
    vb = v_ref[pl.ds(bstart, band), :]
    q = q_ref[...]
    kc = kc_ref[...]
    vc = vc_ref[...]
    rows = 2 * SWA_BLOCK
    qpos = n * SWA_BLOCK + lax.broadcasted_iota(jnp.int32, (rows, band), 0) % SWA_BLOCK
    kpos = bstart + lax.broadcasted_iota(jnp.int32, (rows, band), 1)
    mask = jnp.abs(qpos - kpos) <= SWA_WINDOW
    outs = [None] * 4
    for kv in range(SWA_KV_HEADS):
        sl = slice(kv * HEAD_DIM, (kv + 1) * HEAD_DIM)
        h0, h1 = 2 * kv, 2 * kv + 1
        qh = jnp.concatenate([q[:, h0 * HEAD_DIM:(h0 + 1) * HEAD_DIM],
                              q[:, h1 * HEAD_DIM:(h1 + 1) * HEAD_DIM]], axis=0)
        sk = jnp.concatenate([jnp.broadcast_to(sink_ref[0:1, h0:h0 + 1], (SWA_BLOCK, 1)),
                              jnp.broadcast_to(sink_ref[0:1, h1:h1 + 1], (SWA_BLOCK, 1))], axis=0)
        s_loc = jnp.where(mask, _nt_dot(qh, kb[:, sl]), NEG_INF)
        s_ctx = _nt_dot(qh, kc[:, sl])
        (p_loc, p_ctx), l = _softmax_parts([s_loc, s_ctx], extra=sk)
        o = (_dot(p_loc.astype(BF16), vb[:, sl]) + _dot(p_ctx.astype(BF16), vc[:, sl])) / l
        outs[h0] = o[0:SWA_BLOCK]
        outs[h1] = o[SWA_BLOCK:rows]
    o_ref[...] = jnp.concatenate(outs, axis=1).astype(BF16)


def _swa_attention(sw, sw_c, sink, nbatch, seq, n_ctx):
    nblk = seq // SWA_BLOCK
    assert seq >= 3 * SWA_BLOCK
    sink_pad = jnp.zeros((1, 128), F32).at[0, 0:4].set(sink.astype(F32))
    return pl.pallas_call(
        functools.partial(_swa_kernel, seq=seq),
        grid=(nbatch, nblk),
        in_specs=[pl.BlockSpec((SWA_BLOCK, 256), lambda b, n: (b * nblk + n, 0)),
                  pl.BlockSpec((seq, 128), lambda b, n: (b, 2)),
                  pl.BlockSpec((seq, 128), lambda b, n: (b, 3)),
                  pl.BlockSpec((n_ctx, 128), lambda b, n: (b, 2)),
                  pl.BlockSpec((n_ctx, 128), lambda b, n: (b, 3)),
                  pl.BlockSpec((1, 128), lambda b, n: (0, 0))],
        out_specs=pl.BlockSpec((SWA_BLOCK, 256), lambda b, n: (b * nblk + n, 0)),
        out_shape=jax.ShapeDtypeStruct((nbatch * seq, 256), BF16),
        compiler_params=_params("arbitrary", "arbitrary"),
        name="swa_attention",
    )(sw, sw, sw, sw_c, sw_c, sink_pad)


DIFF_TQ = 256
DIFF_CK = 256
LOG2E = math.log2(math.e)
DIFF_VROWS = HEAD_DIM + 16


def _diff_lambda(lq1_ref, lk1_ref, lq2_ref, lk2_ref, lambda_init):
    s1 = jnp.sum(lq1_ref[...] * lk1_ref[...], axis=-1, keepdims=True)
    s2 = jnp.sum(lq2_ref[...] * lk2_ref[...], axis=-1, keepdims=True)
    return jnp.exp(s1) - jnp.exp(s2) + lambda_init


def _stack_maps(qh):
    lane = lax.broadcasted_iota(jnp.int32, qh.shape, 1)
    zero = jnp.zeros_like(qh)
    return jnp.concatenate([jnp.where(lane < DIFF_QK_DIM, qh, zero),
                            jnp.where(lane >= DIFF_QK_DIM, qh, zero)], axis=0)


def _subln(o0, o1, lam, g, lambda_init):
    o = o0 - lam * o1
    return _rms(o) * g * (1.0 - lambda_init)


def _diff_kernel(qt_ref, k_ref, vt_ref, lq1_ref, lk1_ref, lq2_ref, lk2_ref, g_ref, o_ref, qbd_ref, acc_ref, s_ref,
                 *, nchunk, lambda_init):
    lam = _diff_lambda(lq1_ref, lk1_ref, lq2_ref, lk2_ref, lambda_init)
    qt = qt_ref[0]
    tq = qt.shape[1]
    w = 2 * tq
    row = lax.broadcasted_iota(jnp.int32, qt.shape, 0) // DIFF_QK_DIM
    zero = jnp.zeros_like(qt)
    for j in range(2 * DIFF_HEADS):
        qbd_ref[:, j * tq:(j + 1) * tq] = jnp.where(row == j, qt, zero)
    acc_ref[...] = jnp.zeros_like(acc_ref)

    def scores_into(slot, c):
        kblk = k_ref[0, c]
        for h in range(DIFF_HEADS):
            s_ref[slot, h] = _dot(kblk, qbd_ref[:, h * w:(h + 1) * w])

    def softmax_pv(slot, c, ms):
        vblk = vt_ref[0, c]
        new = []
        for h in range(DIFF_HEADS):
            s = s_ref[slot, h]
            m_new = jnp.maximum(ms[h], s.max(axis=0, keepdims=True))
            alpha = jnp.exp2(ms[h] - m_new)
            p = jnp.exp2((s - m_new).astype(BF16))
            acc_ref[h] = alpha * acc_ref[h] + _dot(vblk[h * DIFF_VROWS:(h + 1) * DIFF_VROWS, :], p)
            new.append(m_new)
        return tuple(new)

    def body(i, ms):
        c = 2 * i
        scores_into(1, c + 1)
        ms = softmax_pv(0, c, ms)
        scores_into(0, c + 2)
        return softmax_pv(1, c + 1, ms)

    ms = tuple(jnp.full((1, w), NEG_INF, F32) for _ in range(DIFF_HEADS))
    scores_into(0, 0)
    npair = (nchunk - 1) // 2
    ms = lax.fori_loop(0, npair, body, ms)
    if nchunk % 2 == 0:
        scores_into(1, nchunk - 1)
        ms = softmax_pv(0, nchunk - 2, ms)
        softmax_pv(1, nchunk - 1, ms)
    else:
        softmax_pv(0, nchunk - 1, ms)
    outs = []
    for h in range(DIFF_HEADS):
        o = acc_ref[h, 0:HEAD_DIM, :] / acc_ref[h, HEAD_DIM:HEAD_DIM + 1, :]
        d = o[:, 0:tq] - lam * o[:, tq:w]
        d = d * lax.rsqrt(jnp.mean(d * d, axis=0, keepdims=True) + EPS)
        outs.append(d * g_ref[...] * (1.0 - lambda_init))
    o_ref[...] = jnp.concatenate(outs, axis=0).T.astype(BF16)


def _diff_attention(df, df_c, lqk, subln_g, lambda_init, nbatch, seq, n_ctx):
    tq, ck = min(DIFF_TQ, seq), DIFF_CK
    s_all = seq + n_ctx
    assert s_all % ck == 0 and seq % tq == 0
    nchunk = s_all // ck
    nq = seq // tq
    dfl = df.reshape(nbatch, seq, 768)
    dfc = df_c.reshape(nbatch, n_ctx, 768)
    qt = dfl[:, :, 0:256].transpose(0, 2, 1)
    k_all = jnp.concatenate([dfl[:, :, 256:512], dfc[:, :, 256:512]], axis=1).reshape(nbatch, nchunk, ck, 256)
    v_all = jnp.concatenate([dfl[:, :, 512:768], dfc[:, :, 512:768]], axis=1)
    v_all = v_all.reshape(nbatch, nchunk, ck, DIFF_HEADS, HEAD_DIM)
    pad = jnp.zeros((nbatch, nchunk, ck, DIFF_HEADS, DIFF_VROWS - HEAD_DIM), BF16).at[..., 0].set(1.0)
    vt_all = jnp.concatenate([v_all, pad], axis=-1).reshape(nbatch, nchunk, ck, DIFF_HEADS * DIFF_VROWS)
    vt_all = vt_all.transpose(0, 1, 3, 2)
    vec = pl.BlockSpec((1, DIFF_QK_DIM), lambda b, n: (0, 0))
    return pl.pallas_call(
        functools.partial(_diff_kernel, nchunk=nchunk, lambda_init=lambda_init),
        grid=(nbatch, nq),
        in_specs=[pl.BlockSpec((1, 256, tq), lambda b, n: (b, 0, n)),
                  pl.BlockSpec((1, nchunk, ck, 256), lambda b, n: (b, 0, 0, 0)),
                  pl.BlockSpec((1, nchunk, DIFF_HEADS * DIFF_VROWS, ck), lambda b, n: (b, 0, 0, 0)),
                  vec, vec, vec, vec,
                  pl.BlockSpec((HEAD_DIM, 1), lambda b, n: (0, 0))],
        out_specs=pl.BlockSpec((tq, 256), lambda b, n: (b * nq + n, 0)),
        out_shape=jax.ShapeDtypeStruct((nbatch * seq, 256), BF16),
        scratch_shapes=[pltpu.VMEM((256, 2 * DIFF_HEADS * tq), BF16),
                        pltpu.VMEM((DIFF_HEADS, DIFF_VROWS, 2 * tq), F32),
                        pltpu.VMEM((2, DIFF_HEADS, ck, 2 * tq), F32)],
        compiler_params=_params("arbitrary", "arbitrary"),
        name="diff_attention",
    )(qt, k_all, vt_all, *lqk, subln_g.reshape(HEAD_DIM, 1).astype(F32))


def _ctx_attn_kernel(na_ref, sw_ref, df_ref, sink_ref, lq1_ref, lk1_ref, lq2_ref, lk2_ref, g_ref,
                     nb_o, sw_o, df_o, *, lambda_init):
    n = na_ref.shape[0]
    na = na_ref[...]
    outs = []
    for h in range(NA_HEADS):
        sl = slice(h * HEAD_DIM, (h + 1) * HEAD_DIM)
        (p,), l = _softmax_parts([_nt_dot(na[:, sl], na[:, 256 + h * HEAD_DIM:256 + (h + 1) * HEAD_DIM])])
        outs.append(_dot(p.astype(BF16), na[:, 512 + h * HEAD_DIM:512 + (h + 1) * HEAD_DIM]) / l)
    nb_o[...] = jnp.concatenate(outs, axis=1).astype(BF16)
    sw = sw_ref[...]
    outs = []
    for hq in range(4):
        kv = hq // 2
        k = sw[:, 256 + kv * HEAD_DIM:256 + (kv + 1) * HEAD_DIM]
        v = sw[:, 384 + kv * HEAD_DIM:384 + (kv + 1) * HEAD_DIM]
        sk = jnp.broadcast_to(sink_ref[0:1, hq:hq + 1], (n, 1))
        (p,), l = _softmax_parts([_nt_dot(sw[:, hq * HEAD_DIM:(hq + 1) * HEAD_DIM], k)], extra=sk)
        outs.append(_dot(p.astype(BF16), v) / l)
    sw_o[...] = jnp.concatenate(outs, axis=1).astype(BF16)
    lam = _diff_lambda(lq1_ref, lk1_ref, lq2_ref, lk2_ref, lambda_init)
    df = df_ref[...]
    outs = []
    for h in range(DIFF_HEADS):
        sl = slice(h * HEAD_DIM, (h + 1) * HEAD_DIM)
        q2 = _stack_maps(df[:, sl])
        (p,), l = _softmax_parts([_nt_dot(q2, df[:, 256 + h * HEAD_DIM:256 + (h + 1) * HEAD_DIM])])
        o = _dot(p.astype(BF16), df[:, 512 + h * HEAD_DIM:512 + (h + 1) * HEAD_DIM]) / l
        outs.append(_subln(o[0:n], o[n:2 * n], lam, g_ref[...], lambda_init))
    df_o[...] = jnp.concatenate(outs, axis=1).astype(BF16)


def _ctx_attention(na_c, sw_c, df_c, sink, lqk, subln_g, lambda_init, nbatch, n_ctx):
    sink_pad = jnp.zeros((1, 128), F32).at[0, 0:4].set(sink.astype(F32))
    vec = pl.BlockSpec((1, DIFF_QK_DIM), lambda b: (0, 0))
    out = jax.ShapeDtypeStruct((nbatch * n_ctx, 256), BF16)
    return pl.pallas_call(
        functools.partial(_ctx_attn_kernel, lambda_init=lambda_init),
        grid=(nbatch,),
        in_specs=[pl.BlockSpec((n_ctx, 768), lambda b: (b, 0)),
                  pl.BlockSpec((n_ctx, 512), lambda b: (b, 0)),
                  pl.BlockSpec((n_ctx, 768), lambda b: (b, 0)),
                  pl.BlockSpec((1, 128), lambda b: (0, 0)),
                  vec, vec, vec, vec,
                  pl.BlockSpec((1, HEAD_DIM), lambda b: (0, 0))],
        out_specs=[pl.BlockSpec((n_ctx, 256), lambda b: (b, 0))] * 3,
        out_shape=[out, out, out],
        compiler_params=_params("arbitrary"),
        name="ctx_attention",
    )(na_c, sw_c, df_c, sink_pad, *lqk, subln_g.reshape(1, HEAD_DIM).astype(F32))


def _merge_kernel(u_ref, yf_ref, yr_ref, d_ref, gw_ref, gb_ref, yb_ref, yc_ref, yd_ref, gate_ref,
                  wb_ref, wo_ref, x_ref, g1_ref, n2_ref, sc2_ref, sh2_ref, rw_ref,
                  x1_o, h2_o, h2p_o, lg_o):
    y = u_ref[...] * d_ref[...] + yf_ref[...] + yr_ref[...]
    a = jax.nn.gelu(y, approximate=True)
    ya = a * jax.nn.sigmoid(_dot(a.astype(BF16), gw_ref[...]) + gb_ref[...])
    branches = (ya.astype(BF16), yb_ref[...], yc_ref[...], yd_ref[...])
    acc = None
    for i in range(4):
        t = gate_ref[:, i * D_MODEL:(i + 1) * D_MODEL].astype(F32) * _dot(branches[i], wb_ref[i])
        acc = t if acc is None else acc + t
    mixed = _dot(acc.astype(BF16), wo_ref[...])
    x1 = x_ref[...] + g1_ref[0] * mixed
    x1_o[...] = x1
    h2 = _rms(x1) * n2_ref[...]
    h2 = h2 * (1.0 + sc2_ref[0]) + sh2_ref[0]
    h2_o[...] = h2.astype(BF16)
    h2p_o[...] = _pack_rows(h2[:, 0:HALF_D], h2[:, HALF_D:D_MODEL])
    lg_o[...] = jnp.dot(h2, rw_ref[...], preferred_element_type=F32, precision=HIGHEST)


def _merge(u, yf, yr, s5_d, glu_w, glu_b, yb, yc, yd, gate, wb, wo, x2d, g1, norm2_g, sc2, sh2, router_w,
           *, rows_per_mod):
    rows = x2d.shape[0]
    tm = 512
    assert rows % tm == 0 and rows_per_mod % tm == 0

    def mod_map(i):
        return ((i * tm) // rows_per_mod, 0, 0)

    row = lambda w: pl.BlockSpec((tm, w), lambda i: (i, 0))
    full = lambda *shape: pl.BlockSpec(shape, lambda i: (0,) * len(shape))
    mod = pl.BlockSpec((1, 1, D_MODEL), mod_map)
    return pl.pallas_call(
        _merge_kernel,
        grid=(rows // tm,),
        in_specs=[row(256), row(256), row(256), full(1, 256), full(256, 256), full(1, 256),
                  row(256), row(256), row(256), row(GATE_WIDTH),
                  full(4, 256, D_MODEL), full(D_MODEL, D_MODEL), row(D_MODEL),
                  mod, full(1, D_MODEL), mod, mod, full(D_MODEL, N_EXPERTS)],
        out_specs=[row(D_MODEL), row(D_MODEL), row(HALF_D), row(N_EXPERTS)],
        out_shape=[jax.ShapeDtypeStruct((rows, D_MODEL), F32),
                   jax.ShapeDtypeStruct((rows, D_MODEL), BF16),
                   jax.ShapeDtypeStruct((rows, HALF_D), jnp.uint32),
                   jax.ShapeDtypeStruct((rows, N_EXPERTS), F32)],
        compiler_params=_params("arbitrary"),
        name="merge",
    )(u, yf, yr, s5_d.reshape(1, 256).astype(F32), glu_w.astype(BF16), glu_b.reshape(1, 256).astype(F32),
      yb, yc, yd, gate, wb.astype(BF16), wo.astype(BF16), x2d, g1, norm2_g.reshape(1, D_MODEL), sc2, sh2,
      router_w.astype(F32))


def _router_kernel(lg_ref, b_ref, tri_ref, idx_ref, rank_ref, w_ref, cnt_ref, base_ref):
    tr = lg_ref.shape[1]
    gsz = N_EXPERTS // N_EXPERT_GROUPS
    sc = jax.nn.sigmoid(lg_ref[...])
    bi = sc + b_ref[...]
    e_iota = lax.broadcasted_iota(jnp.int32, (gsz, tr), 0).astype(F32)
    groups = [bi[g * gsz:(g + 1) * gsz] for g in range(N_EXPERT_GROUPS)]
    gs = []
    for bg in groups:
        m1 = bg.max(axis=0, keepdims=True)
        i1 = jnp.where(bg == m1, e_iota, float(gsz)).min(axis=0, keepdims=True)
        m2 = jnp.where(e_iota == i1, -jnp.inf, bg).max(axis=0, keepdims=True)
        gs.append(m1 + m2)
    v = []
    for g in range(N_EXPERT_GROUPS):
        rank = jnp.zeros((1, tr), F32)
        for g2 in range(N_EXPERT_GROUPS):
            if g2 == g:
                continue
            beats = (gs[g2] >= gs[g]) if g2 < g else (gs[g2] > gs[g])
            rank = rank + jnp.where(beats, 1.0, 0.0)
        v.append(jnp.where(rank < TOPK_GROUPS, groups[g], NEG_INF))
    flat = [e_iota + float(g * gsz) for g in range(N_EXPERT_GROUPS)]
    sel = [jnp.zeros((gsz, tr), F32) for _ in range(N_EXPERT_GROUPS)]
    picks = []
    for _ in range(TOP_K):
        m = v[0].max(axis=0, keepdims=True)
        for g in range(1, N_EXPERT_GROUPS):
            m = jnp.maximum(m, v[g].max(axis=0, keepdims=True))
        am = jnp.where(v[0] == m, flat[0], float(N_EXPERTS)).min(axis=0, keepdims=True)
        for g in range(1, N_EXPERT_GROUPS):
            am = jnp.minimum(am, jnp.where(v[g] == m, flat[g], float(N_EXPERTS)).min(axis=0, keepdims=True))
        hits = []
        for g in range(N_EXPERT_GROUPS):
            hit = flat[g] == am
            hits.append(hit)
            sel[g] = jnp.where(hit, 1.0, sel[g])
            v[g] = jnp.where(hit, -jnp.inf, v[g])
        picks.append((am, hits))
    scg = [sc[g * gsz:(g + 1) * gsz] for g in range(N_EXPERT_GROUPS)]
    den = (sel[0] * scg[0]).sum(axis=0, keepdims=True)
    for g in range(1, N_EXPERT_GROUPS):
        den = den + (sel[g] * scg[g]).sum(axis=0, keepdims=True)

    @pl.when(pl.program_id(0) == 0)
    def _():
        base_ref[...] = jnp.zeros_like(base_ref)

    sel_all = jnp.concatenate(sel, axis=0)
    before = _dot(sel_all.astype(jnp.bfloat16), tri_ref[...]) + base_ref[...]
    for k, (am, hits) in enumerate(picks):
        wk = jnp.zeros((1, tr), F32)
        rk = jnp.zeros((1, tr), F32)
        for g in range(N_EXPERT_GROUPS):
            wk = wk + jnp.where(hits[g], scg[g], 0.0).sum(axis=0, keepdims=True)
            rk = rk + jnp.where(hits[g], before[g * gsz:(g + 1) * gsz], 0.0).sum(axis=0, keepdims=True)
        idx_ref[k:k + 1, :] = am.astype(jnp.int32)
        rank_ref[k:k + 1, :] = rk.astype(jnp.int32)
        w_ref[k:k + 1, :] = wk / den * ROUTED_SCALE
    idx_ref[TOP_K:8, :] = jnp.zeros((8 - TOP_K, tr), jnp.int32)
    rank_ref[TOP_K:8, :] = jnp.zeros((8 - TOP_K, tr), jnp.int32)
    w_ref[TOP_K:8, :] = jnp.zeros((8 - TOP_K, tr), F32)
    base_ref[...] += sel_all.sum(axis=1, keepdims=True)
    cnt_ref[...] = base_ref[...].astype(jnp.int32)


ROUTER_TILE = 512


def _router(logits_t, router_b):
    ne, rows = logits_t.shape
    tr = ROUTER_TILE
    assert rows % tr == 0
    tri = jnp.asarray(np.triu(np.ones((tr, tr), np.float32), k=1), jnp.bfloat16)
    pick = pl.BlockSpec((8, tr), lambda i: (0, i))
    return pl.pallas_call(
        _router_kernel,
        grid=(rows // tr,),
        in_specs=[pl.BlockSpec((ne, tr), lambda i: (0, i)),
                  pl.BlockSpec((ne, 1), lambda i: (0, 0)),
                  pl.BlockSpec((tr, tr), lambda i: (0, 0))],
        out_specs=[pick, pick, pick, pl.BlockSpec((ne, 1), lambda i: (0, 0))],
        out_shape=[jax.ShapeDtypeStruct((8, rows), jnp.int32),
                   jax.ShapeDtypeStruct((8, rows), jnp.int32),
                   jax.ShapeDtypeStruct((8, rows), F32),
                   jax.ShapeDtypeStruct((ne, 1), jnp.int32)],
        scratch_shapes=[pltpu.VMEM((ne, 1), F32)],
        compiler_params=_params("arbitrary"),
        name="router",
    )(logits_t, router_b.reshape(ne, 1).astype(F32), tri)


MOE_BLOCK = 256
MOE_TOKENS = 256
HALF_D = D_MODEL // 2


def _pack_rows(lo, hi):
    lo_b = pltpu.bitcast(lo.astype(jnp.bfloat16).astype(F32), jnp.uint32)
    hi_b = pltpu.bitcast(hi.astype(jnp.bfloat16).astype(F32), jnp.uint32)
    return (hi_b & jnp.uint32(0xFFFF0000)) | (lo_b >> 16)


def _unpack_rows(u):
    lo = pltpu.bitcast(u << 16, F32)
    hi = pltpu.bitcast(u & jnp.uint32(0xFFFF0000), F32)
    return lo, hi


def _swiglu(x_bf16, wgu, wd):
    hgu = _dot(x_bf16, wgu)
    g = hgu[:, 0:EXPERT_HIDDEN]
    a = g * jax.nn.sigmoid(g) * hgu[:, EXPERT_HIDDEN:2 * EXPERT_HIDDEN]
    return _dot(a.astype(BF16), wd)


def _row_copy(src_ref, src_row, dst_ref, dst_row, sem):
    return pltpu.make_async_copy(src_ref.at[pl.ds(src_row, 1)], dst_ref.at[pl.ds(dst_row, 1)], sem)


def _dispatch_kernel(pstart_ref, idx_ref, rank_ref, h_ref, xs_zero_ref, xs_ref, sem, *, tt):
    del xs_zero_ref

    def issue(t, carry):
        for k in range(TOP_K):
            slot = pstart_ref[idx_ref[0, 0, k * tt + t]] + rank_ref[0, 0, k * tt + t]
            _row_copy(h_ref, t, xs_ref, slot, sem).start()
        return carry

    def drain(t, carry):
        for k in range(TOP_K):
            _row_copy(h_ref, 0, xs_ref, 0, sem).wait()
        return carry

    lax.fori_loop(0, tt, issue, 0, unroll=4)
    lax.fori_loop(0, tt, drain, 0)


def _experts_kernel(be_ref, nb_ref, xs_ref, wgu_ref, wd_ref, ys_ref):
    @pl.when(pl.program_id(0) < nb_ref[0])
    def _():
        lo, hi = _unpack_rows(xs_ref[...])
        x = jnp.concatenate([lo, hi], axis=1).astype(BF16)
        y = _swiglu(x, wgu_ref[0], wd_ref[0])
        ys_ref[...] = _pack_rows(y[:, 0:HALF_D], y[:, HALF_D:D_MODEL])


def _combine_kernel(pstart_ref, idx_ref, rank_ref, w_ref, h_ref, wsgu_ref, wsd_ref, x1_ref, g2_ref, fg_ref, ys_ref,
                    o_ref, gbuf, sem, *, tt, final):
    def issue(t, carry):
        for k in range(TOP_K):
            slot = pstart_ref[idx_ref[0, 0, k * tt + t]] + rank_ref[0, 0, k * tt + t]
            _row_copy(ys_ref, slot, gbuf.at[k], t, sem).start()
        return carry

    def drain(t, carry):
        for k in range(TOP_K):
            _row_copy(ys_ref, 0, gbuf.at[k], 0, sem).wait()
        return carry

    lax.fori_loop(0, tt, issue, 0, unroll=4)
    shared = _swiglu(h_ref[...], wsgu_ref[...], wsd_ref[...])
    lax.fori_loop(0, tt, drain, 0)
    acc_lo = shared[:, 0:HALF_D]
    acc_hi = shared[:, HALF_D:D_MODEL]
    w = w_ref[...]
    for k in range(TOP_K):
        lo, hi = _unpack_rows(gbuf[k])
        acc_lo = acc_lo + w[:, k:k + 1] * lo
        acc_hi = acc_hi + w[:, k:k + 1] * hi
    x2 = x1_ref[...] + g2_ref[0] * jnp.concatenate([acc_lo, acc_hi], axis=1)
    if final:
        x2 = _rms(x2) * fg_ref[...]
    o_ref[...] = x2


def _moe(h2, h2p, picks, lp, x1, g2, final_g, *, rows_per_mod, final):
    idx, rank, wsel, counts = picks
    rows = h2.shape[0]
    tt = MOE_TOKENS
    blk = MOE_BLOCK
    assert rows % tt == 0 and rows_per_mod % tt == 0 and (rows * TOP_K) % blk == 0
    ntile = rows // tt
    nblock = rows * TOP_K // blk + N_EXPERTS

    cnt = counts.reshape(N_EXPERTS)
    padded = (cnt + blk - 1) // blk * blk
    pends = jnp.cumsum(padded)
    pstart = (pends - padded).astype(jnp.int32)
    nb_used = (pends[-1] // blk).astype(jnp.int32).reshape(1)
    block_e = jnp.clip(jnp.searchsorted(pends, jnp.arange(nblock, dtype=jnp.int32) * blk, side='right'),
                       0, N_EXPERTS - 1).astype(jnp.int32)
    tile = lambda a: a.reshape(8, ntile, tt).transpose(1, 0, 2).reshape(ntile, 1, 8 * tt)
    idx_t, rank_t = tile(idx), tile(rank)
    smem_tile = pl.BlockSpec((1, 1, 8 * tt), lambda i, *_: (i, 0, 0), memory_space=pltpu.SMEM)

    xs = pl.pallas_call(
        functools.partial(_dispatch_kernel, tt=tt),
        grid_spec=pltpu.PrefetchScalarGridSpec(
            num_scalar_prefetch=1,
            grid=(ntile,),
            in_specs=[smem_tile, smem_tile,
                      pl.BlockSpec((tt, HALF_D), lambda i, *_: (i, 0)),
                      pl.BlockSpec(memory_space=pl.ANY)],
            out_specs=pl.BlockSpec(memory_space=pl.ANY),
            scratch_shapes=[pltpu.SemaphoreType.DMA(())]),
        out_shape=jax.ShapeDtypeStruct((nblock * blk, HALF_D), jnp.uint32),
        input_output_aliases={4: 0},
        compiler_params=_params("arbitrary"),
        name="moe_dispatch",
    )(pstart, idx_t, rank_t, h2p, jnp.zeros((nblock * blk, HALF_D), jnp.uint32))

    wgu = jnp.concatenate([lp['exp_w_gate'], lp['exp_w_up']], axis=2).astype(BF16)
    wd = lp['exp_w_down'].astype(BF16)

    def blk_map(b, be, nb):
        return (jnp.minimum(b, nb[0] - 1), 0)

    def w_map(b, be, nb):
        return (be[jnp.minimum(b, nb[0] - 1)], 0, 0)

    ys = pl.pallas_call(
        _experts_kernel,
        grid_spec=pltpu.PrefetchScalarGridSpec(
            num_scalar_prefetch=2,
            grid=(nblock,),
            in_specs=[pl.BlockSpec((blk, HALF_D), blk_map),
                      pl.BlockSpec((1, D_MODEL, 2 * EXPERT_HIDDEN), w_map),
                      pl.BlockSpec((1, EXPERT_HIDDEN, D_MODEL), w_map)],
            out_specs=pl.BlockSpec((blk, HALF_D), blk_map)),
        out_shape=jax.ShapeDtypeStruct((nblock * blk, HALF_D), jnp.uint32),
        compiler_params=_params("arbitrary"),
        name="moe_experts",
    )(block_e, nb_used, xs, wgu, wd)

    wsgu = jnp.concatenate([lp['sh_w_gate'], lp['sh_w_up']], axis=1).astype(BF16)
    wsd = lp['sh_w_down'].astype(BF16)
    full = lambda *shape: pl.BlockSpec(shape, lambda i, *_: (0,) * len(shape))
    row = lambda width: pl.BlockSpec((tt, width), lambda i, *_: (i, 0))
    return pl.pallas_call(
        functools.partial(_combine_kernel, tt=tt, final=final),
        grid_spec=pltpu.PrefetchScalarGridSpec(
            num_scalar_prefetch=1,
            grid=(ntile,),
            in_specs=[smem_tile, smem_tile, row(8), row(D_MODEL),
                      full(D_MODEL, 2 * EXPERT_HIDDEN), full(EXPERT_HIDDEN, D_MODEL), row(D_MODEL),
                      pl.BlockSpec((1, 1, D_MODEL), lambda i, *_: ((i * tt) // rows_per_mod, 0, 0)),
                      full(1, D_MODEL),
                      pl.BlockSpec(memory_space=pl.ANY)],
            out_specs=row(D_MODEL),
            scratch_shapes=[pltpu.VMEM((TOP_K, tt, HALF_D), jnp.uint32),
                            pltpu.SemaphoreType.DMA(())]),
        out_shape=jax.ShapeDtypeStruct((rows, D_MODEL), F32),
        compiler_params=_params("arbitrary"),
        name="moe_combine",
    )(pstart, idx_t, rank_t, wsel.T, h2, wsgu, wsd, x1, g2, final_g.reshape(1, D_MODEL).astype(F32), ys)


def _reorder_w_in(w_in):
    split = 256 + 768 + 512 + 768
    return jnp.concatenate([w_in[:, split:], w_in[:, :split]], axis=1).astype(BF16)


def _mods(mod_row_block):
    return [mod_row_block[:, None, k * D_MODEL:(k + 1) * D_MODEL] for k in range(6)]


def _moe_block(h2, h2p, logits, lp, x1, g2, final_g, *, rows_per_mod, final):
    picks = _router(logits.T, lp['router_b'])
    return _moe(h2, h2p, picks, lp, x1, g2, final_g, rows_per_mod=rows_per_mod, final=final)


def _layer(x2d, xc2d, c16, lp, layer_idx, tables, final_g, *, nbatch, seq, n_ctx, with_ctx_out, final):
    lambda_init = 0.8 - 0.6 * math.exp(-0.3 * layer_idx)
    mod = _ada_mod(c16, lp['ada_w'].astype(F32), lp['ada_b'].astype(F32))
    sh1, sc1, g1, sh2, sc2, g2 = _mods(mod[0:nbatch])
    csh1, csc1, cg1, csh2, csc2, cg2 = _mods(mod[nbatch:nbatch + 1])
    w_in = _reorder_w_in(lp['w_in'])
    rows_lat = nbatch * seq
    rows_ctx = nbatch * n_ctx

    gate, u, na, sw, df = _inproj(x2d, lp['norm1_g'], sc1, sh1, w_in, tables,
                                  rows_per_mod=seq, rope=True, seq=seq)
    gate_c, u_c, na_c, sw_c, df_c = _inproj(xc2d, lp['norm1_g'], csc1, csh1, w_in, tables,
                                            rows_per_mod=rows_ctx, rope=False, seq=seq)

    win, wout, a_re, a_im = _s5_params(lp['s5_lambda_re'], lp['s5_lambda_im'], lp['s5_log_step'],
                                       lp['s5_b_re'], lp['s5_b_im'], lp['s5_c_re'], lp['s5_c_im'])
    u_tm = jnp.concatenate([u_c.reshape(nbatch, n_ctx, 256).transpose(1, 0, 2),
                            u.reshape(nbatch, seq, 256).transpose(1, 0, 2)], axis=0)
    y_tm = _s5_scan(u_tm, win, wout, a_re, a_im, n_ctx)
    y_bm = y_tm.transpose(0, 2, 1, 3)
    yf = y_bm[0, :, n_ctx:].reshape(rows_lat, 256)
    yr = y_bm[1, :, n_ctx:].reshape(rows_lat, 256)

    lqk = [lp[k].reshape(1, DIFF_QK_DIM).astype(F32) for k in ('diff_lq1', 'diff_lk1', 'diff_lq2', 'diff_lk2')]
    bias = _na_bias_table(lp['na_rpb'], seq // GRID_W)
    yb = _na_attention(na, na_c, bias, nbatch, seq, n_ctx)
    yc = _swa_attention(sw, sw_c, lp['swa_sink'], nbatch, seq, n_ctx)
    yd = _diff_attention(df, df_c, lqk, lp['diff_subln_g'], lambda_init, nbatch, seq, n_ctx)

    merge_w = (lp['s5_d'], lp['s5_glu_w'], lp['s5_glu_b'])
    x1, h2, h2p, logits = _merge(u, yf, yr, *merge_w, yb, yc, yd, gate, lp['w_branch'], lp['w_out'], x2d, g1,
                                 lp['norm2_g'], sc2, sh2, lp['router_w'], rows_per_mod=seq)
    x_out = _moe_block(h2, h2p, logits, lp, x1, g2, final_g, rows_per_mod=seq, final=final)

    xc_out = None
    if with_ctx_out:
        yf_c = y_bm[0, :, :n_ctx].reshape(rows_ctx, 256)
        yr_c = y_bm[1, :, :n_ctx].reshape(rows_ctx, 256)
        yb_c, yc_c, yd_c = _ctx_attention(na_c, sw_c, df_c, lp['swa_sink'], lqk, lp['diff_subln_g'],
                                          lambda_init, nbatch, n_ctx)
        x1c, h2c, h2pc, logits_c = _merge(u_c, yf_c, yr_c, *merge_w, yb_c, yc_c, yd_c, gate_c, lp['w_branch'],
                                          lp['w_out'], xc2d, cg1, lp['norm2_g'], csc2, csh2, lp['router_w'],
                                          rows_per_mod=rows_ctx)
        xc_out = _moe_block(h2c, h2pc, logits_c, lp, x1c, cg2, final_g, rows_per_mod=rows_ctx, final=False)
    return x_out, xc_out


def kernel(x, c, ctx, c_ctx, ada_w, ada_b, norm1_g, norm2_g, w_in, s5_lambda_re, s5_lambda_im, s5_log_step,
           s5_b_re, s5_b_im, s5_c_re, s5_c_im, s5_d, s5_glu_w, s5_glu_b, na_rpb, swa_sink, diff_lq1, diff_lk1,
           diff_lq2, diff_lk2, diff_subln_g, w_branch, w_out, router_w, router_b, exp_w_gate, exp_w_up,
           exp_w_down, sh_w_gate, sh_w_up, sh_w_down, final_g):
    nbatch, seq, d = x.shape
    n_ctx = ctx.shape[1]
    depth = ada_w.shape[0]
    assert d == D_MODEL and nbatch == 8
    stacked = dict(ada_w=ada_w, ada_b=ada_b, norm1_g=norm1_g, norm2_g=norm2_g, w_in=w_in,
                   s5_lambda_re=s5_lambda_re, s5_lambda_im=s5_lambda_im, s5_log_step=s5_log_step,
                   s5_b_re=s5_b_re, s5_b_im=s5_b_im, s5_c_re=s5_c_re, s5_c_im=s5_c_im, s5_d=s5_d,
                   s5_glu_w=s5_glu_w, s5_glu_b=s5_glu_b, na_rpb=na_rpb, swa_sink=swa_sink,
                   diff_lq1=diff_lq1, diff_lk1=diff_lk1, diff_lq2=diff_lq2, diff_lk2=diff_lk2,
                   diff_subln_g=diff_subln_g, w_branch=w_branch, w_out=w_out, router_w=router_w,
                   router_b=router_b, exp_w_gate=exp_w_gate, exp_w_up=exp_w_up, exp_w_down=exp_w_down,
                   sh_w_gate=sh_w_gate, sh_w_up=sh_w_up, sh_w_down=sh_w_down)
    tables = _rope_tables(seq)
    c16 = jnp.concatenate([c.astype(F32), c_ctx.reshape(1, d).astype(F32),
                           jnp.zeros((16 - nbatch - 1, d), F32)], axis=0)
    x2d = x.reshape(nbatch * seq, d).astype(F32)
    xc2d = ctx.reshape(nbatch * n_ctx, d).astype(F32)
    for l in range(depth):
        lp = {k: v[l] for k, v in stacked.items()}
        last = l == depth - 1
        x2d, xc2d = _layer(x2d, xc2d, c16, lp, l, tables, final_g, nbatch=nbatch, seq=seq, n_ctx=n_ctx,
                           with_ctx_out=not last, final=last)
    return x2d.reshape(nbatch, seq, d)
```

```python
import functools
import math

import numpy as np
import jax
import jax.numpy as jnp
from jax import lax
from jax.experimental import pallas as pl
from jax.experimental.pallas import tpu as pltpu

F32 = jnp.float32
BF16 = jnp.bfloat16
HIGHEST = lax.Precision.HIGHEST

GRID_W = 64
EPS = 1e-6
NEG_INF = -1e30
ROPE_BASE = 10000.0
D_MODEL = 1024
BRANCH_WIDTH = 256
HEAD_DIM = 64
S5_GROUP = 16
S5_GROUPS = 16
S5_STATE = 64
S5_FLAT = S5_GROUPS * S5_STATE
NA_HEADS = 4
NA_WIN_ROWS = 8
NA_WIN_COLS = 16
SWA_KV_HEADS = 2
SWA_WINDOW = 128
DIFF_HEADS = 4
DIFF_QK_DIM = 32
N_EXPERTS = 64
N_EXPERT_GROUPS = 8
TOPK_GROUPS = 4
TOP_K = 6
EXPERT_HIDDEN = 256
ROUTED_SCALE = 2.5
GATE_WIDTH = 4 * D_MODEL

VMEM_LIMIT = 56 * 1024 * 1024


def _params(*sem):
    return pltpu.CompilerParams(dimension_semantics=sem, vmem_limit_bytes=VMEM_LIMIT)


def _nt_dot(a, b):
    return lax.dot_general(a, b, (((1,), (1,)), ((), ())), preferred_element_type=F32)


def _dot(a, b):
    return jnp.dot(a, b, preferred_element_type=F32)


def _rms(x):
    return x * lax.rsqrt(jnp.mean(x * x, axis=-1, keepdims=True) + EPS)


def _ada_kernel(c_ref, w_ref, b_ref, o_ref):
    c = c_ref[...]
    s = c * jax.nn.sigmoid(c)
    o_ref[...] = jnp.dot(s, w_ref[...], preferred_element_type=F32, precision=HIGHEST) + b_ref[...]


def _ada_mod(cc, w, b):
    rows, d = cc.shape
    width = w.shape[1]
    tn = 1536
    return pl.pallas_call(
        _ada_kernel,
        grid=(width // tn,),
        in_specs=[pl.BlockSpec((rows, d), lambda j: (0, 0)),
                  pl.BlockSpec((d, tn), lambda j: (0, j)),
                  pl.BlockSpec((1, tn), lambda j: (0, j))],
        out_specs=pl.BlockSpec((rows, tn), lambda j: (0, j)),
        out_shape=jax.ShapeDtypeStruct((rows, width), F32),
        compiler_params=_params("arbitrary"),
        name="ada_mod",
    )(cc, w, b.reshape(1, width))


_C_GATE = 0
_C_U = GATE_WIDTH
_C_NA = _C_U + 256
_C_SW = _C_NA + 768
_C_DF = _C_SW + 512
_C_END = _C_DF + 768


def _rope_apply(x, cos, sins, half):
    outs = []
    for j in range(x.shape[1] // 128):
        xs = x[:, j * 128:(j + 1) * 128]
        lane = lax.broadcasted_iota(jnp.int32, xs.shape, 1)
        lo = (lane % (2 * half)) < half
        partner = jnp.where(lo, pltpu.roll(xs, 128 - half, 1), pltpu.roll(xs, half, 1))
        outs.append(xs * cos + partner * sins)
    return outs[0] if len(outs) == 1 else jnp.concatenate(outs, axis=1)


def _inproj_kernel(x_ref, g_ref, sc_ref, sh_ref, w_ref, c64_ref, s64_ref, c32_ref, s32_ref,
                   gate_o, u_o, na_o, sw_o, df_o, *, rope):
    h = _rms(x_ref[...]) * g_ref[...]
    h = h * (1.0 + sc_ref[0]) + sh_ref[0]
    hb = h.astype(BF16)

    def mm(c0, c1):
        return _dot(hb, w_ref[:, c0:c1])

    for k in range(GATE_WIDTH // 512):
        gate_o[:, k * 512:(k + 1) * 512] = jax.nn.sigmoid(mm(k * 512, (k + 1) * 512)).astype(BF16)
    u_o[...] = mm(_C_U, _C_U + 256)

    na = mm(_C_NA, _C_NA + 768)
    na_o[:, 0:256] = (na[:, 0:256] * (HEAD_DIM ** -0.5 * (LOG2E if rope else 1.0))).astype(BF16)
    na_o[:, 256:768] = na[:, 256:768].astype(BF16)

    sw = mm(_C_SW, _C_SW + 512)
    swq, swk = sw[:, 0:256], sw[:, 256:384]
    if rope:
        swq = _rope_apply(swq, c64_ref[...], s64_ref[...], 16)
        swk = _rope_apply(swk, c64_ref[...], s64_ref[...], 16)
    sw_o[:, 0:256] = (swq * (HEAD_DIM ** -0.5 * (LOG2E if rope else 1.0))).astype(BF16)
    sw_o[:, 256:384] = swk.astype(BF16)
    sw_o[:, 384:512] = sw[:, 384:512].astype(BF16)

    df = mm(_C_DF, _C_DF + 768)
    dfq, dfk = df[:, 0:256], df[:, 256:512]
    if rope:
        dfq = _rope_apply(dfq, c32_ref[...], s32_ref[...], 8)
        dfk = _rope_apply(dfk, c32_ref[...], s32_ref[...], 8)
    df_o[:, 0:256] = (dfq * (DIFF_QK_DIM ** -0.5 * (LOG2E if rope else 1.0))).astype(BF16)
    df_o[:, 256:512] = dfk.astype(BF16)
    df_o[:, 512:768] = df[:, 512:768].astype(BF16)


def _inproj(x2d, norm_g, sc, sh, w_bf16, tables, *, rows_per_mod, rope, seq):
    rows = x2d.shape[0]
    tm = 512
    assert rows % tm == 0 and rows_per_mod % tm == 0 and seq % tm == 0
    tiles_per_seq = seq // tm

    def mod_map(i):
        return ((i * tm) // rows_per_mod, 0, 0)

    def tab_map(i):
        return (i % tiles_per_seq, 0)

    tab_spec = pl.BlockSpec((tm, 128), tab_map)
    row = lambda w: pl.BlockSpec((tm, w), lambda i: (i, 0))
    return pl.pallas_call(
        functools.partial(_inproj_kernel, rope=rope),
        grid=(rows // tm,),
        in_specs=[row(D_MODEL),
                  pl.BlockSpec((1, D_MODEL), lambda i: (0, 0)),
                  pl.BlockSpec((1, 1, D_MODEL), mod_map),
                  pl.BlockSpec((1, 1, D_MODEL), mod_map),
                  pl.BlockSpec((D_MODEL, _C_END), lambda i: (0, 0)),
                  tab_spec, tab_spec, tab_spec, tab_spec],
        out_specs=[row(GATE_WIDTH), row(256), row(768), row(512), row(768)],
        out_shape=[jax.ShapeDtypeStruct((rows, GATE_WIDTH), BF16),
                   jax.ShapeDtypeStruct((rows, 256), F32),
                   jax.ShapeDtypeStruct((rows, 768), BF16),
                   jax.ShapeDtypeStruct((rows, 512), BF16),
                   jax.ShapeDtypeStruct((rows, 768), BF16)],
        compiler_params=_params("arbitrary"),
        name="inproj",
    )(x2d, norm_g.reshape(1, D_MODEL), sc, sh, w_bf16, *tables)


def _rope_tables(seq):
    t = jnp.arange(seq)
    rows = (t // GRID_W).astype(F32)
    cols = (t % GRID_W).astype(F32)
    lane = np.arange(128)
    out = []
    for dim in (64, 32):
        quarter = dim // 4
        inv_freq = ROPE_BASE ** (-jnp.arange(quarter, dtype=F32) / quarter)
        l = lane % dim
        use_col = l >= dim // 2
        fidx = l % quarter
        hi = (l % (dim // 2)) >= quarter
        ang_r = rows[:, None] * inv_freq[None, :]
        ang_c = cols[:, None] * inv_freq[None, :]
        ang = jnp.where(use_col[None, :], ang_c[:, fidx], ang_r[:, fidx])
        out.append(jnp.cos(ang))
        out.append(jnp.where(hi[None, :], jnp.sin(ang), -jnp.sin(ang)))
    return tuple(out)


S5_CHUNK = 128


def _s5_kernel(u_ref, win_ref, wout_ref, are_ref, aim_ref, y_ref, bu_ref, st_ref, *, tc, nb):
    d = pl.program_id(0)
    i = pl.program_id(1)

    @pl.when(i == 0)
    def _():
        st_ref[...] = jnp.zeros_like(st_ref)

    u = u_ref[...].reshape(tc * nb, BRANCH_WIDTH).astype(BF16)
    bu_ref[...] = _dot(u, win_ref[0])
    ar = jnp.broadcast_to(are_ref[0], (nb, S5_FLAT))
    ai = jnp.broadcast_to(aim_ref[0], (nb, S5_FLAT))

    def body(j, carry):
        xr, xi = carry
        t = j + d * (tc - 1 - 2 * j)
        row = pl.multiple_of(t * nb, nb)
        br = bu_ref[pl.ds(row, nb), 0:S5_FLAT]
        bi = bu_ref[pl.ds(row, nb), S5_FLAT:2 * S5_FLAT]
        nr = ar * xr - ai * xi + br
        ni = ar * xi + ai * xr + bi
        bu_ref[pl.ds(row, nb), 0:S5_FLAT] = nr
        bu_ref[pl.ds(row, nb), S5_FLAT:2 * S5_FLAT] = ni
        return nr, ni

    xr, xi = lax.fori_loop(0, tc, body, (st_ref[:, 0:S5_FLAT], st_ref[:, S5_FLAT:2 * S5_FLAT]), unroll=4)
    st_ref[:, 0:S5_FLAT] = xr
    st_ref[:, S5_FLAT:2 * S5_FLAT] = xi
    y = _dot(bu_ref[...].astype(BF16), wout_ref[0])
    y_ref[0] = y.reshape(tc, nb, BRANCH_WIDTH)


def _s5_scan(u_tm, win, wout, a_re, a_im, n_ctx):
    s_len, nb, _ = u_tm.shape
    tc = S5_CHUNK
    assert nb == 8 and s_len % tc == 0 and n_ctx % tc == 0
    nct = n_ctx // tc
    nlt = (s_len - n_ctx) // tc

    def chunk(d, i):
        rev = jnp.where(i < nct, nct - 1 - i, 2 * nct + nlt - 1 - i)
        return jnp.where(d == 0, i, rev)

    return pl.pallas_call(
        functools.partial(_s5_kernel, tc=tc, nb=nb),
        grid=(2, nct + nlt),
        in_specs=[pl.BlockSpec((tc, nb, BRANCH_WIDTH), lambda d, i: (chunk(d, i), 0, 0)),
                  pl.BlockSpec((1, BRANCH_WIDTH, 2 * S5_FLAT), lambda d, i: (d, 0, 0)),
                  pl.BlockSpec((1, 2 * S5_FLAT, BRANCH_WIDTH), lambda d, i: (d, 0, 0)),
                  pl.BlockSpec((1, 1, S5_FLAT), lambda d, i: (d, 0, 0)),
                  pl.BlockSpec((1, 1, S5_FLAT), lambda d, i: (d, 0, 0))],
        out_specs=pl.BlockSpec((1, tc, nb, BRANCH_WIDTH), lambda d, i: (d, chunk(d, i), 0, 0)),
        out_shape=jax.ShapeDtypeStruct((2, s_len, nb, BRANCH_WIDTH), F32),
        scratch_shapes=[pltpu.VMEM((tc * nb, 2 * S5_FLAT), F32),
                        pltpu.VMEM((nb, 2 * S5_FLAT), F32)],
        compiler_params=_params("arbitrary", "arbitrary"),
        name="s5_scan",
    )(u_tm, win, wout, a_re, a_im)


def _s5_params(lam_re, lam_im, log_step, b_re, b_im, c_re, c_im):
    lr = lam_re.astype(F32)
    li = lam_im.astype(F32)
    dt = jnp.exp(log_step.astype(F32))[..., None]
    mag = jnp.exp(lr * dt)
    a_re = mag * jnp.cos(li * dt)
    a_im = mag * jnp.sin(li * dt)
    nr, ni, den = a_re - 1.0, a_im, lr * lr + li * li
    k_re = ((nr * lr + ni * li) / den)[..., None]
    k_im = ((ni * lr - nr * li) / den)[..., None]
    br = b_re.astype(F32)
    bi = b_im.astype(F32)
    bb_re = k_re * br - k_im * bi
    bb_im = k_re * bi + k_im * br
    eye = jnp.eye(S5_GROUPS, dtype=F32)

    def blockdiag_in(bb):
        m = jnp.einsum('dgpc,gh->dgchp', bb, eye)
        return m.reshape(2, S5_GROUPS * S5_GROUP, S5_GROUPS * S5_STATE)

    def blockdiag_out(cc):
        m = jnp.einsum('dgcp,gh->dgphc', cc, eye)
        return m.reshape(2, S5_GROUPS * S5_STATE, S5_GROUPS * S5_GROUP)

    win = jnp.concatenate([blockdiag_in(bb_re), blockdiag_in(bb_im)], axis=2).astype(BF16)
    wout = jnp.concatenate([blockdiag_out(c_re.astype(F32)), -blockdiag_out(c_im.astype(F32))], axis=1).astype(BF16)
    return win, wout, a_re.reshape(2, 1, S5_FLAT), a_im.reshape(2, 1, S5_FLAT)


def _softmax_parts(scores, extra=None):
    m = scores[0].max(axis=-1, keepdims=True)
    for s in scores[1:]:
        m = jnp.maximum(m, s.max(axis=-1, keepdims=True))
    if extra is not None:
        m = jnp.maximum(m, extra)
    ps = [jnp.exp(s - m) for s in scores]
    l = ps[0].sum(axis=-1, keepdims=True)
    for p in ps[1:]:
        l = l + p.sum(axis=-1, keepdims=True)
    if extra is not None:
        l = l + jnp.exp(extra - m)
    return ps, l


NA_QROWS = 2
NA_KROWS = 10
VROWS = HEAD_DIM + 16


def _na_window_start(r, grid_rows):
    start = jnp.clip(r - NA_WIN_ROWS // 2, 0, grid_rows - NA_WIN_ROWS)
    return (jnp.minimum(start, grid_rows - NA_KROWS) // 2) * 2


def _head_blockdiag(qt, qbd_ref, nheads, rows_per_head):
    n = qt.shape[1]
    row_h = lax.broadcasted_iota(jnp.int32, qt.shape, 0) // rows_per_head
    zero = jnp.zeros_like(qt)
    for h in range(nheads):
        qbd_ref[:, h * n:(h + 1) * n] = jnp.where(row_h == h, qt, zero)


def _na_kernel(qt_ref, k_ref, vt_ref, kc_ref, vct_ref, bias_ref, o_ref, qbd_ref, *, grid_rows):
    nq = NA_QROWS * GRID_W
    nk = NA_KROWS * GRID_W
    off = pl.multiple_of(_na_window_start(NA_QROWS * pl.program_id(1), grid_rows) * GRID_W, 128)
    _head_blockdiag(qt_ref[0], qbd_ref, NA_HEADS, HEAD_DIM)
    qbd = qbd_ref[...]
    s_loc = _dot(k_ref[0, pl.ds(off, nk), :], qbd) + bias_ref[0]
    s_ctx = _dot(kc_ref[0], qbd)
    m = jnp.maximum(s_loc.max(axis=0, keepdims=True), s_ctx.max(axis=0, keepdims=True))
    p_loc = jnp.exp2((s_loc - m).astype(BF16))
    p_ctx = jnp.exp2((s_ctx - m).astype(BF16))
    vw = vt_ref[0, :, pl.ds(off, nk)]
    outs = []
    for h in range(NA_HEADS):
        rows = slice(h * VROWS, (h + 1) * VROWS)
        cols = slice(h * nq, (h + 1) * nq)
        o = _dot(vw[rows], p_loc[:, cols]) + _dot(vct_ref[0, rows, :], p_ctx[:, cols])
        outs.append(o[0:HEAD_DIM] / o[HEAD_DIM:HEAD_DIM + 1])
    o_ref[...] = jnp.concatenate(outs, axis=0).T.astype(BF16)


def _na_classes(grid_rows):
    return [0, 2, 4, grid_rows - 4, grid_rows - 2]


def _na_bias_table(rpb, grid_rows):
    col = np.arange(GRID_W)
    cstart = np.clip(col - NA_WIN_COLS // 2, 0, GRID_W - NA_WIN_COLS)
    col_in = (col[None, :] >= cstart[:, None]) & (col[None, :] < cstart[:, None] + NA_WIN_COLS)
    cb = np.clip(col[None, :] - col[:, None] + (NA_WIN_COLS - 1), 0, 2 * NA_WIN_COLS - 2)
    classes = _na_classes(grid_rows)
    rbi = np.zeros((len(classes), NA_QROWS, NA_KROWS), np.int64)
    row_in = np.zeros((len(classes), NA_QROWS, NA_KROWS), bool)
    for c, r in enumerate(classes):
        a_row = (min(int(np.clip(r - NA_WIN_ROWS // 2, 0, grid_rows - NA_WIN_ROWS)), grid_rows - NA_KROWS) // 2) * 2
        for qi in range(NA_QROWS):
            start_q = int(np.clip(r + qi - NA_WIN_ROWS // 2, 0, grid_rows - NA_WIN_ROWS))
            for j in range(NA_KROWS):
                row_in[c, qi, j] = start_q <= a_row + j < start_q + NA_WIN_ROWS
                rbi[c, qi, j] = np.clip(a_row + j - (r + qi) + NA_WIN_ROWS - 1, 0, 2 * NA_WIN_ROWS - 2)
    oh_row = jnp.asarray(rbi[..., None] == np.arange(2 * NA_WIN_ROWS - 1), F32)
    oh_col = jnp.asarray(cb[:, :, None] == np.arange(2 * NA_WIN_COLS - 1), F32)
    t = jnp.einsum('hab,cija,qkb->cjkhiq', rpb.astype(F32), oh_row, oh_col, precision=HIGHEST) * LOG2E
    valid = row_in.transpose(0, 2, 1)[:, :, None, None, :, None] & col_in.T[None, None, :, None, None, :]
    t = jnp.where(valid, t, NEG_INF)
    return t.reshape(len(classes), NA_KROWS * GRID_W, NA_HEADS * NA_QROWS * GRID_W)


def _augment_vt(v, nheads):
    nb, s, _ = v.shape
    v = v.reshape(nb, s, nheads, HEAD_DIM)
    pad = jnp.zeros((nb, s, nheads, VROWS - HEAD_DIM), v.dtype).at[..., 0].set(1.0)
    return jnp.concatenate([v, pad], axis=-1).reshape(nb, s, nheads * VROWS).transpose(0, 2, 1)


def _na_attention(na, na_c, bias, nbatch, seq, n_ctx):
    grid_rows = seq // GRID_W
    assert grid_rows >= NA_KROWS and grid_rows % NA_QROWS == 0
    nq = NA_QROWS * GRID_W
    npair = grid_rows // NA_QROWS
    nal = na.reshape(nbatch, seq, 768)
    nac = na_c.reshape(nbatch, n_ctx, 768)
    qt = nal[:, :, 0:256].transpose(0, 2, 1)
    vt = _augment_vt(nal[:, :, 512:768], NA_HEADS)
    vct = _augment_vt(nac[:, :, 512:768], NA_HEADS)

    def cls(b, p):
        r = NA_QROWS * p
        c = jnp.where(r < 4, r // 2, jnp.where(r >= grid_rows - 4, (r - (grid_rows - 4)) // 2 + 3, 2))
        return (c, 0, 0)

    return pl.pallas_call(
        functools.partial(_na_kernel, grid_rows=grid_rows),
        grid=(nbatch, npair),
        in_specs=[pl.BlockSpec((1, 256, nq), lambda b, p: (b, 0, p)),
                  pl.BlockSpec((1, seq, 256), lambda b, p: (b, 0, 1)),
                  pl.BlockSpec((1, NA_HEADS * VROWS, seq), lambda b, p: (b, 0, 0)),
                  pl.BlockSpec((1, n_ctx, 256), lambda b, p: (b, 0, 1)),
                  pl.BlockSpec((1, NA_HEADS * VROWS, n_ctx), lambda b, p: (b, 0, 0)),
                  pl.BlockSpec((1, NA_KROWS * GRID_W, NA_HEADS * nq), cls)],
        out_specs=pl.BlockSpec((nq, 256), lambda b, p: (b * npair + p, 0)),
        out_shape=jax.ShapeDtypeStruct((nbatch * seq, 256), BF16),
        scratch_shapes=[pltpu.VMEM((256, NA_HEADS * nq), BF16)],
        compiler_params=_params("arbitrary", "arbitrary"),
        name="na_attention",
    )(qt, nal, vt, nac, vct, bias)


SWA_BLOCK = 128


def _swa_kernel(qt_ref, k_ref, vt_ref, kc_ref, vct_ref, sink_ref, o_ref, qbd_ref, *, seq):
    n = pl.program_id(1)
    band = 3 * SWA_BLOCK
    nq = SWA_BLOCK
    nqh = 2 * SWA_KV_HEADS
    bstart = pl.multiple_of(jnp.clip((n - 1) * SWA_BLOCK, 0, seq - band), SWA_BLOCK)
    qt = qt_ref[0]
    zero = jnp.zeros((HEAD_DIM, nq), qt.dtype)
    for hq in range(nqh):
        qh = qt[hq * HEAD_DIM:(hq + 1) * HEAD_DIM]
        qbd_ref[:, hq * nq:(hq + 1) * nq] = jnp.concatenate([qh, zero] if hq // 2 == 0 else [zero, qh], axis=0)
    qbd = qbd_ref[...]
    kpos = bstart + lax.broadcasted_iota(jnp.int32, (band, nqh * nq), 0)
    qpos = n * SWA_BLOCK + lax.broadcasted_iota(jnp.int32, (band, nqh * nq), 1) % nq
    s_loc = jnp.where(jnp.abs(qpos - kpos) <= SWA_WINDOW, _dot(k_ref[0, pl.ds(bstart, band), :], qbd), NEG_INF)
    s_ctx = _dot(kc_ref[0], qbd)
    sink = jnp.concatenate([jnp.broadcast_to(sink_ref[0:1, hq:hq + 1] * LOG2E, (1, nq)) for hq in range(nqh)], axis=1)
    m = jnp.maximum(jnp.maximum(s_loc.max(axis=0, keepdims=True), s_ctx.max(axis=0, keepdims=True)), sink)
    p_loc = jnp.exp2((s_loc - m).astype(BF16))
    p_ctx = jnp.exp2((s_ctx - m).astype(BF16))
    p_sink = jnp.exp2(sink - m)
    vw = vt_ref[0, :, pl.ds(bstart, band)]
    outs = []
    for hq in range(nqh):
        rows = slice((hq // 2) * VROWS, (hq // 2 + 1) * VROWS)
        cols = slice(hq * nq, (hq + 1) * nq)
        o = _dot(vw[rows], p_loc[:, cols]) + _dot(vct_ref[0, rows, :], p_ctx[:, cols])
        outs.append(o[0:HEAD_DIM] / (o[HEAD_DIM:HEAD_DIM + 1] + p_sink[:, cols]))
    o_ref[...] = jnp.concatenate(outs, axis=0).T.astype(BF16)


def _swa_attention(sw, sw_c, sink, nbatch, seq, n_ctx):
    nblk = seq // SWA_BLOCK
    nqh = 2 * SWA_KV_HEADS
    assert seq >= 3 * SWA_BLOCK
    sink_pad = jnp.zeros((1, 128), F32).at[0, 0:nqh].set(sink.astype(F32))
    swl = sw.reshape(nbatch, seq, 512)
    swc = sw_c.reshape(nbatch, n_ctx, 512)
    qt = swl[:, :, 0:256].transpose(0, 2, 1)
    vt = _augment_vt(swl[:, :, 384:512], SWA_KV_HEADS)
    vct = _augment_vt(swc[:, :, 384:512], SWA_KV_HEADS)
    return pl.pallas_call(
        functools.partial(_swa_kernel, seq=seq),
        grid=(nbatch, nblk),
        in_specs=[pl.BlockSpec((1, 256, SWA_BLOCK), lambda b, n: (b, 0, n)),
                  pl.BlockSpec((1, seq, 128), lambda b, n: (b, 0, 2)),
                  pl.BlockSpec((1, SWA_KV_HEADS * VROWS, seq), lambda b, n: (b, 0, 0)),
                  pl.BlockSpec((1, n_ctx, 128), lambda b, n: (b, 0, 2)),
                  pl.BlockSpec((1, SWA_KV_HEADS * VROWS, n_ctx), lambda b, n: (b, 0, 0)),
                  pl.BlockSpec((1, 128), lambda b, n: (0, 0))],
        out_specs=pl.BlockSpec((SWA_BLOCK, 256), lambda b, n: (b * nblk + n, 0)),
        out_shape=jax.ShapeDtypeStruct((nbatch * seq, 256), BF16),
        scratch_shapes=[pltpu.VMEM((SWA_KV_HEADS * HEAD_DIM, nqh * SWA_BLOCK), BF16)],
        compiler_params=_params("arbitrary", "arbitrary"),
        name="swa_attention",
    )(qt, swl, vt, swc, vct, sink_pad)


DIFF_TQ = 512
DIFF_CK = 256
LOG2E = math.log2(math.e)
DIFF_VROWS = HEAD_DIM + 16


def _diff_lambda(lq1_ref, lk1_ref, lq2_ref, lk2_ref, lambda_init):
    s1 = jnp.sum(lq1_ref[...] * lk1_ref[...], axis=-1, keepdims=True)
    s2 = jnp.sum(lq2_ref[...] * lk2_ref[...], axis=-1, keepdims=True)
    return jnp.exp(s1) - jnp.exp(s2) + lambda_init


def _stack_maps(qh):
    lane = lax.broadcasted_iota(jnp.int32, qh.shape, 1)
    zero = jnp.zeros_like(qh)
    return jnp.concatenate([jnp.where(lane < DIFF_QK_DIM, qh, zero),
                            jnp.where(lane >= DIFF_QK_DIM, qh, zero)], axis=0)


def _subln(o0, o1, lam, g, lambda_init):
    o = o0 - lam * o1
    return _rms(o) * g * (1.0 - lambda_init)


def _diff_kernel(qt_ref, k_ref, vt_ref, lq1_ref, lk1_ref, lq2_ref, lk2_ref, g_ref, o_ref, qbd_ref, acc_ref, s_ref,
                 *, nchunk, lambda_init):
    lam = _diff_lambda(lq1_ref, lk1_ref, lq2_ref, lk2_ref, lambda_init)
    qt = qt_ref[0]
    tq = qt.shape[1]
    w = 2 * tq
    row = lax.broadcasted_iota(jnp.int32, qt.shape, 0) // DIFF_QK_DIM
    zero = jnp.zeros_like(qt)
    for j in range(2 * DIFF_HEADS):
        qbd_ref[:, j * tq:(j + 1) * tq] = jnp.where(row == j, qt, zero)
    acc_ref[...] = jnp.zeros_like(acc_ref)

    def scores(slot, c, h):
        s = _dot(k_ref[0, c], qbd_ref[:, h * w:(h + 1) * w])
        s_ref[slot, h] = s
        return s.max(axis=0, keepdims=True)

    def softmax_pv(slot, c, h, m_run, m_chunk):
        m_new = jnp.maximum(m_run, m_chunk)
        alpha = jnp.exp2(m_run - m_new)
        p = jnp.exp2((s_ref[slot, h] - m_new).astype(BF16))
        acc_ref[h] = alpha * acc_ref[h] + _dot(vt_ref[0, c, h * DIFF_VROWS:(h + 1) * DIFF_VROWS, :], p)
        return m_new

    def step(slot, c, carry):
        m_run, m_chunk = carry
        new_run, new_chunk = [], []
        for h in range(DIFF_HEADS):
            new_chunk.append(scores(1 - slot, c + 1, h))
            new_run.append(softmax_pv(slot, c, h, m_run[h], m_chunk[h]))
        return tuple(new_run), tuple(new_chunk)

    def body(i, carry):
        c = 2 * i
        return step(1, c + 1, step(0, c, carry))

    m_run = tuple(jnp.full((1, w), NEG_INF, F32) for _ in range(DIFF_HEADS))
    m_chunk = tuple(scores(0, 0, h) for h in range(DIFF_HEADS))
    npair = (nchunk - 1) // 2
    m_run, m_chunk = lax.fori_loop(0, npair, body, (m_run, m_chunk))
    if nchunk % 2 == 0:
        m_run, m_chunk = step(0, nchunk - 2, (m_run, m_chunk))
        for h in range(DIFF_HEADS):
            softmax_pv(1, nchunk - 1, h, m_run[h], m_chunk[h])
    else:
        for h in range(DIFF_HEADS):
            softmax_pv(0, nchunk - 1, h, m_run[h], m_chunk[h])
    outs = []
    for h in range(DIFF_HEADS):
        o = acc_ref[h, 0:HEAD_DIM, :] / acc_ref[h, HEAD_DIM:HEAD_DIM + 1, :]
        d = o[:, 0:tq] - lam * o[:, tq:w]
        d = d * lax.rsqrt(jnp.mean(d * d, axis=0, keepdims=True) + EPS)
        outs.append(d * g_ref[...] * (1.0 - lambda_init))
    o_ref[...] = jnp.concatenate(outs, axis=0).T.astype(BF16)


def _diff_attention(df, df_c, lqk, subln_g, lambda_init, nbatch, seq, n_ctx):
    tq, ck = min(DIFF_TQ, seq), DIFF_CK
    s_all = seq + n_ctx
    assert s_all % ck == 0 and seq % tq == 0
    nchunk = s_all // ck
    nq = seq // tq
    dfl = df.reshape(nbatch, seq, 768)
    dfc = df_c.reshape(nbatch, n_ctx, 768)
    qt = dfl[:, :, 0:256].transpose(0, 2, 1)
    k_all = jnp.concatenate([dfl[:, :, 256:512], dfc[:, :, 256:512]], axis=1).reshape(nbatch, nchunk, ck, 256)
    v_all = jnp.concatenate([dfl[:, :, 512:768], dfc[:, :, 512:768]], axis=1)
    v_all = v_all.reshape(nbatch, nchunk, ck, DIFF_HEADS, HEAD_DIM)
    pad = jnp.zeros((nbatch, nchunk, ck, DIFF_HEADS, DIFF_VROWS - HEAD_DIM), BF16).at[..., 0].set(1.0)
    vt_all = jnp.concatenate([v_all, pad], axis=-1).reshape(nbatch, nchunk, ck, DIFF_HEADS * DIFF_VROWS)
    vt_all = vt_all.transpose(0, 1, 3, 2)
    vec = pl.BlockSpec((1, DIFF_QK_DIM), lambda b, n: (0, 0))
    return pl.pallas_call(
        functools.partial(_diff_kernel, nchunk=nchunk, lambda_init=lambda_init),
        grid=(nbatch, nq),
        in_specs=[pl.BlockSpec((1, 256, tq), lambda b, n: (b, 0, n)),
                  pl.BlockSpec((1, nchunk, ck, 256), lambda b, n: (b, 0, 0, 0)),
                  pl.BlockSpec((1, nchunk, DIFF_HEADS * DIFF_VROWS, ck), lambda b, n: (b, 0, 0, 0)),
                  vec, vec, vec, vec,
                  pl.BlockSpec((HEAD_DIM, 1), lambda b, n: (0, 0))],
        out_specs=pl.BlockSpec((tq, 256), lambda b, n: (b * nq + n, 0)),
        out_shape=jax.ShapeDtypeStruct((nbatch * seq, 256), BF16),
        scratch_shapes=[pltpu.VMEM((256, 2 * DIFF_HEADS * tq), BF16),
                        pltpu.VMEM((DIFF_HEADS, DIFF_VROWS, 2 * tq), F32),
                        pltpu.VMEM((2, DIFF_HEADS, ck, 2 * tq), F32)],
        compiler_params=_params("arbitrary", "arbitrary"),
        name="diff_attention",
    )(qt, k_all, vt_all, *lqk, subln_g.reshape(HEAD_DIM, 1).astype(F32))


def _ctx_attn_kernel(na_ref, sw_ref, df_ref, sink_ref, lq1_ref, lk1_ref, lq2_ref, lk2_ref, g_ref,
                     nb_o, sw_o, df_o, *, lambda_init):
    n = na_ref.shape[0]
    na = na_ref[...]
    outs = []
    for h in range(NA_HEADS):
        sl = slice(h * HEAD_DIM, (h + 1) * HEAD_DIM)
        (p,), l = _softmax_parts([_nt_dot(na[:, sl], na[:, 256 + h * HEAD_DIM:256 + (h + 1) * HEAD_DIM])])
        outs.append(_dot(p.astype(BF16), na[:, 512 + h * HEAD_DIM:512 + (h + 1) * HEAD_DIM]) / l)
    nb_o[...] = jnp.concatenate(outs, axis=1).astype(BF16)
    sw = sw_ref[...]
    outs = []
    for hq in range(4):
        kv = hq // 2
        k = sw[:, 256 + kv * HEAD_DIM:256 + (kv + 1) * HEAD_DIM]
        v = sw[:, 384 + kv * HEAD_DIM:384 + (kv + 1) * HEAD_DIM]
        sk = jnp.broadcast_to(sink_ref[0:1, hq:hq + 1], (n, 1))
        (p,), l = _softmax_parts([_nt_dot(sw[:, hq * HEAD_DIM:(hq + 1) * HEAD_DIM], k)], extra=sk)
        outs.append(_dot(p.astype(BF16), v) / l)
    sw_o[...] = jnp.concatenate(outs, axis=1).astype(BF16)
    lam = _diff_lambda(lq1_ref, lk1_ref, lq2_ref, lk2_ref, lambda_init)
    df = df_ref[...]
    outs = []
    for h in range(DIFF_HEADS):
        sl = slice(h * HEAD_DIM, (h + 1) * HEAD_DIM)
        q2 = _stack_maps(df[:, sl])
        (p,), l = _softmax_parts([_nt_dot(q2, df[:, 256 + h * HEAD_DIM:256 + (h + 1) * HEAD_DIM])])
        o = _dot(p.astype(BF16), df[:, 512 + h * HEAD_DIM:512 + (h + 1) * HEAD_DIM]) / l
        outs.append(_subln(o[0:n], o[n:2 * n], lam, g_ref[...], lambda_init))
    df_o[...] = jnp.concatenate(outs, axis=1).astype(BF16)


def _ctx_attention(na_c, sw_c, df_c, sink, lqk, subln_g, lambda_init, nbatch, n_ctx):
    sink_pad = jnp.zeros((1, 128), F32).at[0, 0:4].set(sink.astype(F32))
    vec = pl.BlockSpec((1, DIFF_QK_DIM), lambda b: (0, 0))
    out = jax.ShapeDtypeStruct((nbatch * n_ctx, 256), BF16)
    return pl.pallas_call(
        functools.partial(_ctx_attn_kernel, lambda_init=lambda_init),
        grid=(nbatch,),
        in_specs=[pl.BlockSpec((n_ctx, 768), lambda b: (b, 0)),
                  pl.BlockSpec((n_ctx, 512), lambda b: (b, 0)),
                  pl.BlockSpec((n_ctx, 768), lambda b: (b, 0)),
                  pl.BlockSpec((1, 128), lambda b: (0, 0)),
                  vec, vec, vec, vec,
                  pl.BlockSpec((1, HEAD_DIM), lambda b: (0, 0))],
        out_specs=[pl.BlockSpec((n_ctx, 256), lambda b: (b, 0))] * 3,
        out_shape=[out, out, out],
        compiler_params=_params("arbitrary"),
        name="ctx_attention",
    )(na_c, sw_c, df_c, sink_pad, *lqk, subln_g.reshape(1, HEAD_DIM).astype(F32))


def _merge_kernel(u_ref, yf_ref, yr_ref, d_ref, gw_ref, gb_ref, yb_ref, yc_ref, yd_ref, gate_ref,
                  wb_ref, wo_ref, x_ref, g1_ref, n2_ref, sc2_ref, sh2_ref, rw_ref,
                  x1_o, h2_o, h2p_o, lg_o):
    y = u_ref[...] * d_ref[...] + yf_ref[...] + yr_ref[...]
    a = jax.nn.gelu(y, approximate=True)
    ya = a * jax.nn.sigmoid(_dot(a.astype(BF16), gw_ref[...]) + gb_ref[...])
    branches = (ya.astype(BF16), yb_ref[...], yc_ref[...], yd_ref[...])
    acc = None
    for i in range(4):
        t = gate_ref[:, i * D_MODEL:(i + 1) * D_MODEL].astype(F32) * _dot(branches[i], wb_ref[i])
        acc = t if acc is None else acc + t
    mixed = _dot(acc.astype(BF16), wo_ref[...])
    x1 = x_ref[...] + g1_ref[0] * mixed
    x1_o[...] = x1
    h2 = _rms(x1) * n2_ref[...]
    h2 = h2 * (1.0 + sc2_ref[0]) + sh2_ref[0]
    h2_o[...] = h2.astype(BF16)
    h2p_o[...] = _pack_rows(h2[:, 0:HALF_D], h2[:, HALF_D:D_MODEL])
    lg_o[...] = jnp.dot(h2, rw_ref[...], preferred_element_type=F32, precision=HIGHEST)


def _merge(u, yf, yr, s5_d, glu_w, glu_b, yb, yc, yd, gate, wb, wo, x2d, g1, norm2_g, sc2, sh2, router_w,
           *, rows_per_mod):
    rows = x2d.shape[0]
    tm = 512
    assert rows % tm == 0 and rows_per_mod % tm == 0

    def mod_map(i):
        return ((i * tm) // rows_per_mod, 0, 0)

    row = lambda w: pl.BlockSpec((tm, w), lambda i: (i, 0))
    full = lambda *shape: pl.BlockSpec(shape, lambda i: (0,) * len(shape))
    mod = pl.BlockSpec((1, 1, D_MODEL), mod_map)
    return pl.pallas_call(
        _merge_kernel,
        grid=(rows // tm,),
        in_specs=[row(256), row(256), row(256), full(1, 256), full(256, 256), full(1, 256),
                  row(256), row(256), row(256), row(GATE_WIDTH),
                  full(4, 256, D_MODEL), full(D_MODEL, D_MODEL), row(D_MODEL),
                  mod, full(1, D_MODEL), mod, mod, full(D_MODEL, N_EXPERTS)],
        out_specs=[row(D_MODEL), row(D_MODEL), row(HALF_D), row(N_EXPERTS)],
        out_shape=[jax.ShapeDtypeStruct((rows, D_MODEL), F32),
                   jax.ShapeDtypeStruct((rows, D_MODEL), BF16),
                   jax.ShapeDtypeStruct((rows, HALF_D), jnp.uint32),
                   jax.ShapeDtypeStruct((rows, N_EXPERTS), F32)],
        compiler_params=_params("arbitrary"),
        name="merge",
    )(u, yf, yr, s5_d.reshape(1, 256).astype(F32), glu_w.astype(BF16), glu_b.reshape(1, 256).astype(F32),
      yb, yc, yd, gate, wb.astype(BF16), wo.astype(BF16), x2d, g1, norm2_g.reshape(1, D_MODEL), sc2, sh2,
      router_w.astype(F32))


def _router_kernel(lg_ref, b_ref, tri_ref, idx_ref, rank_ref, w_ref, cnt_ref, base_ref):
    tr = lg_ref.shape[1]
    gsz = N_EXPERTS // N_EXPERT_GROUPS
    sc = jax.nn.sigmoid(lg_ref[...])
    bi = sc + b_ref[...]
    e_iota = lax.broadcasted_iota(jnp.int32, (gsz, tr), 0).astype(F32)
    groups = [bi[g * gsz:(g + 1) * gsz] for g in range(N_EXPERT_GROUPS)]
    gs = []
    for bg in groups:
        m1 = bg.max(axis=0, keepdims=True)
        i1 = jnp.where(bg == m1, e_iota, float(gsz)).min(axis=0, keepdims=True)
        m2 = jnp.where(e_iota == i1, -jnp.inf, bg).max(axis=0, keepdims=True)
        gs.append(m1 + m2)
    v = []
    for g in range(N_EXPERT_GROUPS):
        rank = jnp.zeros((1, tr), F32)
        for g2 in range(N_EXPERT_GROUPS):
            if g2 == g:
                continue
            beats = (gs[g2] >= gs[g]) if g2 < g else (gs[g2] > gs[g])
            rank = rank + jnp.where(beats, 1.0, 0.0)
        v.append(jnp.where(rank < TOPK_GROUPS, groups[g], NEG_INF))
    flat = [e_iota + float(g * gsz) for g in range(N_EXPERT_GROUPS)]
    sel = [jnp.zeros((gsz, tr), F32) for _ in range(N_EXPERT_GROUPS)]
    picks = []
    for _ in range(TOP_K):
        m = v[0].max(axis=0, keepdims=True)
        for g in range(1, N_EXPERT_GROUPS):
            m = jnp.maximum(m, v[g].max(axis=0, keepdims=True))
        am = jnp.where(v[0] == m, flat[0], float(N_EXPERTS)).min(axis=0, keepdims=True)
        for g in range(1, N_EXPERT_GROUPS):
            am = jnp.minimum(am, jnp.where(v[g] == m, flat[g], float(N_EXPERTS)).min(axis=0, keepdims=True))
        hits = []
        for g in range(N_EXPERT_GROUPS):
            hit = flat[g] == am
            hits.append(hit)
            sel[g] = jnp.where(hit, 1.0, sel[g])
            v[g] = jnp.where(hit, -jnp.inf, v[g])
        picks.append((am, hits))
    scg = [sc[g * gsz:(g + 1) * gsz] for g in range(N_EXPERT_GROUPS)]
    den = (sel[0] * scg[0]).sum(axis=0, keepdims=True)
    for g in range(1, N_EXPERT_GROUPS):
        den = den + (sel[g] * scg[g]).sum(axis=0, keepdims=True)

    @pl.when(pl.program_id(0) == 0)
    def _():
        base_ref[...] = jnp.zeros_like(base_ref)

    sel_all = jnp.concatenate(sel, axis=0)
    before = _dot(sel_all.astype(jnp.bfloat16), tri_ref[...]) + base_ref[...]
    for k, (am, hits) in enumerate(picks):
        wk = jnp.zeros((1, tr), F32)
        rk = jnp.zeros((1, tr), F32)
        for g in range(N_EXPERT_GROUPS):
            wk = wk + jnp.where(hits[g], scg[g], 0.0).sum(axis=0, keepdims=True)
            rk = rk + jnp.where(hits[g], before[g * gsz:(g + 1) * gsz], 0.0).sum(axis=0, keepdims=True)
        idx_ref[k:k + 1, :] = am.astype(jnp.int32)
        rank_ref[k:k + 1, :] = rk.astype(jnp.int32)
        w_ref[k:k + 1, :] = wk / den * ROUTED_SCALE
    idx_ref[TOP_K:8, :] = jnp.zeros((8 - TOP_K, tr), jnp.int32)
    rank_ref[TOP_K:8, :] = jnp.zeros((8 - TOP_K, tr), jnp.int32)
    w_ref[TOP_K:8, :] = jnp.zeros((8 - TOP_K, tr), F32)
    base_ref[...] += sel_all.sum(axis=1, keepdims=True)
    cnt_ref[...] = base_ref[...].astype(jnp.int32)


ROUTER_TILE = 512


def _router(logits_t, router_b):
    ne, rows = logits_t.shape
    tr = ROUTER_TILE
    assert rows % tr == 0
    tri = jnp.asarray(np.triu(np.ones((tr, tr), np.float32), k=1), jnp.bfloat16)
    pick = pl.BlockSpec((8, tr), lambda i: (0, i))
    return pl.pallas_call(
        _router_kernel,
        grid=(rows // tr,),
        in_specs=[pl.BlockSpec((ne, tr), lambda i: (0, i)),
                  pl.BlockSpec((ne, 1), lambda i: (0, 0)),
                  pl.BlockSpec((tr, tr), lambda i: (0, 0))],
        out_specs=[pick, pick, pick, pl.BlockSpec((ne, 1), lambda i: (0, 0))],
        out_shape=[jax.ShapeDtypeStruct((8, rows), jnp.int32),
                   jax.ShapeDtypeStruct((8, rows), jnp.int32),
                   jax.ShapeDtypeStruct((8, rows), F32),
                   jax.ShapeDtypeStruct((ne, 1), jnp.int32)],
        scratch_shapes=[pltpu.VMEM((ne, 1), F32)],
        compiler_params=_params("arbitrary"),
        name="router",
    )(logits_t, router_b.reshape(ne, 1).astype(F32), tri)


MOE_BLOCK = 256
MOE_TOKENS = 256
HALF_D = D_MODEL // 2


def _pack_rows(lo, hi):
    lo_b = pltpu.bitcast(lo.astype(jnp.bfloat16).astype(F32), jnp.uint32)
    hi_b = pltpu.bitcast(hi.astype(jnp.bfloat16).astype(F32), jnp.uint32)
    return (hi_b & jnp.uint32(0xFFFF0000)) | (lo_b >> 16)


def _unpack_rows(u):
    lo = pltpu.bitcast(u << 16, F32)
    hi = pltpu.bitcast(u & jnp.uint32(0xFFFF0000), F32)
    return lo, hi


def _swiglu(x_bf16, wgu, wd):
    hgu = _dot(x_bf16, wgu)
    g = hgu[:, 0:EXPERT_HIDDEN]
    a = g * jax.nn.sigmoid(g) * hgu[:, EXPERT_HIDDEN:2 * EXPERT_HIDDEN]
    return _dot(a.astype(BF16), wd)


def _row_copy(src_ref, src_row, dst_ref, dst_row, sem):
    return pltpu.make_async_copy(src_ref.at[pl.ds(src_row, 1)], dst_ref.at[pl.ds(dst_row, 1)], sem)


def _dispatch_kernel(pstart_ref, idx_ref, rank_ref, h_ref, xs_zero_ref, xs_ref, sem, *, tt):
    del xs_zero_ref

    def issue(t, carry):
        for k in range(TOP_K):
            slot = pstart_ref[idx_ref[0, 0, k * tt + t]] + rank_ref[0, 0, k * tt + t]
            _row_copy(h_ref, t, xs_ref, slot, sem).start()
        return carry

    def drain(t, carry):
        for k in range(TOP_K):
            _row_copy(h_ref, 0, xs_ref, 0, sem).wait()
        return carry

    lax.fori_loop(0, tt, issue, 0, unroll=4)
    lax.fori_loop(0, tt, drain, 0)


def _experts_kernel(be_ref, nb_ref, xs_ref, wgu_ref, wd_ref, ys_ref):
    @pl.when(pl.program_id(0) < nb_ref[0])
    def _():
        lo, hi = _unpack_rows(xs_ref[...])
        x = jnp.concatenate([lo, hi], axis=1).astype(BF16)
        y = _swiglu(x, wgu_ref[0], wd_ref[0])
        ys_ref[...] = _pack_rows(y[:, 0:HALF_D], y[:, HALF_D:D_MODEL])


def _combine_kernel(pstart_ref, idx_ref, rank_ref, w_ref, h_ref, wsgu_ref, wsd_ref, x1_ref, g2_ref, fg_ref, ys_ref,
                    o_ref, gbuf, sem, *, tt, final):
    def issue(t, carry):
        for k in range(TOP_K):
            slot = pstart_ref[idx_ref[0, 0, k * tt + t]] + rank_ref[0, 0, k * tt + t]
            _row_copy(ys_ref, slot, gbuf.at[k], t, sem).start()
        return carry

    def drain(t, carry):
        for k in range(TOP_K):
            _row_copy(ys_ref, 0, gbuf.at[k], 0, sem).wait()
        return carry

    lax.fori_loop(0, tt, issue, 0, unroll=4)
    shared = _swiglu(h_ref[...], wsgu_ref[...], wsd_ref[...])
    lax.fori_loop(0, tt, drain, 0)
    acc_lo = shared[:, 0:HALF_D]
    acc_hi = shared[:, HALF_D:D_MODEL]
    w = w_ref[...]
    for k in range(TOP_K):
        lo, hi = _unpack_rows(gbuf[k])
        acc_lo = acc_lo + w[:, k:k + 1] * lo
        acc_hi = acc_hi + w[:, k:k + 1] * hi
    x2 = x1_ref[...] + g2_ref[0] * jnp.concatenate([acc_lo, acc_hi], axis=1)
    if final:
        x2 = _rms(x2) * fg_ref[...]
    o_ref[...] = x2


def _moe(h2, h2p, picks, lp, x1, g2, final_g, *, rows_per_mod, final):
    idx, rank, wsel, counts = picks
    rows = h2.shape[0]
    tt = MOE_TOKENS
    blk = MOE_BLOCK
    assert rows % tt == 0 and rows_per_mod % tt == 0 and (rows * TOP_K) % blk == 0
    ntile = rows // tt
    nblock = rows * TOP_K // blk + N_EXPERTS

    cnt = counts.reshape(N_EXPERTS)
    padded = (cnt + blk - 1) // blk * blk
    pends = jnp.cumsum(padded)
    pstart = (pends - padded).astype(jnp.int32)
    nb_used = (pends[-1] // blk).astype(jnp.int32).reshape(1)
    first_row = jnp.arange(nblock, dtype=jnp.int32) * blk
    block_e = jnp.minimum(jnp.sum(pends[None, :] <= first_row[:, None], axis=1), N_EXPERTS - 1).astype(jnp.int32)
    tile = lambda a: a.reshape(8, ntile, tt).transpose(1, 0, 2).reshape(ntile, 1, 8 * tt)
    idx_t, rank_t = tile(idx), tile(rank)
    smem_tile = pl.BlockSpec((1, 1, 8 * tt), lambda i, *_: (i, 0, 0), memory_space=pltpu.SMEM)

    xs = pl.pallas_call(
        functools.partial(_dispatch_kernel, tt=tt),
        grid_spec=pltpu.PrefetchScalarGridSpec(
            num_scalar_prefetch=1,
            grid=(ntile,),
            in_specs=[smem_tile, smem_tile,
                      pl.BlockSpec((tt, HALF_D), lambda i, *_: (i, 0)),
                      pl.BlockSpec(memory_space=pl.ANY)],
            out_specs=pl.BlockSpec(memory_space=pl.ANY),
            scratch_shapes=[pltpu.SemaphoreType.DMA(())]),
        out_shape=jax.ShapeDtypeStruct((nblock * blk, HALF_D), jnp.uint32),
        input_output_aliases={4: 0},
        compiler_params=_params("arbitrary"),
        name="moe_dispatch",
    )(pstart, idx_t, rank_t, h2p, jnp.zeros((nblock * blk, HALF_D), jnp.uint32))

    wgu = jnp.concatenate([lp['exp_w_gate'], lp['exp_w_up']], axis=2).astype(BF16)
    wd = lp['exp_w_down'].astype(BF16)

    def blk_map(b, be, nb):
        return (jnp.minimum(b, nb[0] - 1), 0)

    def w_map(b, be, nb):
        return (be[jnp.minimum(b, nb[0] - 1)], 0, 0)

    ys = pl.pallas_call(
        _experts_kernel,
        grid_spec=pltpu.PrefetchScalarGridSpec(
            num_scalar_prefetch=2,
            grid=(nblock,),
            in_specs=[pl.BlockSpec((blk, HALF_D), blk_map),
                      pl.BlockSpec((1, D_MODEL, 2 * EXPERT_HIDDEN), w_map),
                      pl.BlockSpec((1, EXPERT_HIDDEN, D_MODEL), w_map)],
            out_specs=pl.BlockSpec((blk, HALF_D), blk_map)),
        out_shape=jax.ShapeDtypeStruct((nblock * blk, HALF_D), jnp.uint32),
        compiler_params=_params("arbitrary"),
        name="moe_experts",
    )(block_e, nb_used, xs, wgu, wd)

    wsgu = jnp.concatenate([lp['sh_w_gate'], lp['sh_w_up']], axis=1).astype(BF16)
    wsd = lp['sh_w_down'].astype(BF16)
    full = lambda *shape: pl.BlockSpec(shape, lambda i, *_: (0,) * len(shape))
    row = lambda width: pl.BlockSpec((tt, width), lambda i, *_: (i, 0))
    return pl.pallas_call(
        functools.partial(_combine_kernel, tt=tt, final=final),
        grid_spec=pltpu.PrefetchScalarGridSpec(
            num_scalar_prefetch=1,
            grid=(ntile,),
            in_specs=[smem_tile, smem_tile, row(8), row(D_MODEL),
                      full(D_MODEL, 2 * EXPERT_HIDDEN), full(EXPERT_HIDDEN, D_MODEL), row(D_MODEL),
                      pl.BlockSpec((1, 1, D_MODEL), lambda i, *_: ((i * tt) // rows_per_mod, 0, 0)),
                      full(1, D_MODEL),
                      pl.BlockSpec(memory_space=pl.ANY)],
            out_specs=row(D_MODEL),
            scratch_shapes=[pltpu.VMEM((TOP_K, tt, HALF_D), jnp.uint32),
                            pltpu.SemaphoreType.DMA(())]),
        out_shape=jax.ShapeDtypeStruct((rows, D_MODEL), F32),
        compiler_params=_params("arbitrary"),
        name="moe_combine",
    )(pstart, idx_t, rank_t, wsel.T, h2, wsgu, wsd, x1, g2, final_g.reshape(1, D_MODEL).astype(F32), ys)


def _reorder_w_in(w_in):
    split = 256 + 768 + 512 + 768
    return jnp.concatenate([w_in[:, split:], w_in[:, :split]], axis=1).astype(BF16)


def _mods(mod_row_block):
    return [mod_row_block[:, None, k * D_MODEL:(k + 1) * D_MODEL] for k in range(6)]


def _moe_block(h2, h2p, logits, lp, x1, g2, final_g, *, rows_per_mod, final):
    picks = _router(logits.T, lp['router_b'])
    return _moe(h2, h2p, picks, lp, x1, g2, final_g, rows_per_mod=rows_per_mod, final=final)


def _layer(x2d, xc2d, c16, lp, layer_idx, tables, final_g, *, nbatch, seq, n_ctx, with_ctx_out, final):
    lambda_init = 0.8 - 0.6 * math.exp(-0.3 * layer_idx)
    mod = _ada_mod(c16, lp['ada_w'].astype(F32), lp['ada_b'].astype(F32))
    sh1, sc1, g1, sh2, sc2, g2 = _mods(mod[0:nbatch])
    csh1, csc1, cg1, csh2, csc2, cg2 = _mods(mod[nbatch:nbatch + 1])
    w_in = _reorder_w_in(lp['w_in'])
    rows_lat = nbatch * seq
    rows_ctx = nbatch * n_ctx

    gate, u, na, sw, df = _inproj(x2d, lp['norm1_g'], sc1, sh1, w_in, tables,
                                  rows_per_mod=seq, rope=True, seq=seq)
    gate_c, u_c, na_c, sw_c, df_c = _inproj(xc2d, lp['norm1_g'], csc1, csh1, w_in, tables,
                                            rows_per_mod=rows_ctx, rope=False, seq=seq)

    win, wout, a_re, a_im = _s5_params(lp['s5_lambda_re'], lp['s5_lambda_im'], lp['s5_log_step'],
                                       lp['s5_b_re'], lp['s5_b_im'], lp['s5_c_re'], lp['s5_c_im'])
    u_tm = jnp.concatenate([u_c.reshape(nbatch, n_ctx, 256).transpose(1, 0, 2),
                            u.reshape(nbatch, seq, 256).transpose(1, 0, 2)], axis=0)
    y_tm = _s5_scan(u_tm, win, wout, a_re, a_im, n_ctx)
    y_bm = y_tm.transpose(0, 2, 1, 3)
    yf = y_bm[0, :, n_ctx:].reshape(rows_lat, 256)
    yr = y_bm[1, :, n_ctx:].reshape(rows_lat, 256)

    lqk = [lp[k].reshape(1, DIFF_QK_DIM).astype(F32) for k in ('diff_lq1', 'diff_lk1', 'diff_lq2', 'diff_lk2')]
    bias = _na_bias_table(lp['na_rpb'], seq // GRID_W)
    yb = _na_attention(na, na_c, bias, nbatch, seq, n_ctx)
    yc = _swa_attention(sw, sw_c, lp['swa_sink'], nbatch, seq, n_ctx)
    yd = _diff_attention(df, df_c, lqk, lp['diff_subln_g'], lambda_init, nbatch, seq, n_ctx)

    merge_w = (lp['s5_d'], lp['s5_glu_w'], lp['s5_glu_b'])
    x1, h2, h2p, logits = _merge(u, yf, yr, *merge_w, yb, yc, yd, gate, lp['w_branch'], lp['w_out'], x2d, g1,
                                 lp['norm2_g'], sc2, sh2, lp['router_w'], rows_per_mod=seq)
    x_out = _moe_block(h2, h2p, logits, lp, x1, g2, final_g, rows_per_mod=seq, final=final)

    xc_out = None
    if with_ctx_out:
        yf_c = y_bm[0, :, :n_ctx].reshape(rows_ctx, 256)
        yr_c = y_bm[1, :, :n_ctx].reshape(rows_ctx, 256)
        yb_c, yc_c, yd_c = _ctx_attention(na_c, sw_c, df_c, lp['swa_sink'], lqk, lp['diff_subln_g'],
                                          lambda_init, nbatch, n_ctx)
        x1c, h2c, h2pc, logits_c = _merge(u_c, yf_c, yr_c, *merge_w, yb_c, yc_c, yd_c, gate_c, lp['w_branch'],
                                          lp['w_out'], xc2d, cg1, lp['norm2_g'], csc2, csh2, lp['router_w'],
                                          rows_per_mod=rows_ctx)
        xc_out = _moe_block(h2c, h2pc, logits_c, lp, x1c, cg2, final_g, rows_per_mod=rows_ctx, final=False)
    return x_out, xc_out


def kernel(x, c, ctx, c_ctx, ada_w, ada_b, norm1_g, norm2_g, w_in, s5_lambda_re, s5_lambda_im, s5_log_step,
           s5_b_re, s5_b_im, s5_c_re, s5_c_im, s5_d, s5_glu_w, s5_glu_b, na_rpb, swa_sink, diff_lq1, diff_lk1,
           diff_lq2, diff_lk2, diff_subln_g, w_branch, w_out, router_w, router_b, exp_w_gate, exp_w_up,
           exp_w_down, sh_w_gate, sh_w_up, sh_w_down, final_g):
    nbatch, seq, d = x.shape
    n_ctx = ctx.shape[1]
    depth = ada_w.shape[0]
    assert d == D_MODEL and nbatch == 8
    stacked = dict(ada_w=ada_w, ada_b=ada_b, norm1_g=norm1_g, norm2_g=norm2_g, w_in=w_in,
                   s5_lambda_re=s5_lambda_re, s5_lambda_im=s5_lambda_im, s5_log_step=s5_log_step,
                   s5_b_re=s5_b_re, s5_b_im=s5_b_im, s5_c_re=s5_c_re, s5_c_im=s5_c_im, s5_d=s5_d,
                   s5_glu_w=s5_glu_w, s5_glu_b=s5_glu_b, na_rpb=na_rpb, swa_sink=swa_sink,
                   diff_lq1=diff_lq1, diff_lk1=diff_lk1, diff_lq2=diff_lq2, diff_lk2=diff_lk2,
                   diff_subln_g=diff_subln_g, w_branch=w_branch, w_out=w_out, router_w=router_w,
                   router_b=router_b, exp_w_gate=exp_w_gate, exp_w_up=exp_w_up, exp_w_down=exp_w_down,
                   sh_w_gate=sh_w_gate, sh_w_up=sh_w_up, sh_w_down=sh_w_down)
    tables = _rope_tables(seq)
    c16 = jnp.concatenate([c.astype(F32), c_ctx.reshape(1, d).astype(F32),
                           jnp.zeros((16 - nbatch - 1, d), F32)], axis=0)
    x2d = x.reshape(nbatch * seq, d).astype(F32)
    xc2d = ctx.reshape(nbatch * n_ctx, d).astype(F32)
    for l in range(depth):
        lp = {k: v[l] for k, v in stacked.items()}
        last = l == depth - 1
        x2d, xc2d = _layer(x2d, xc2d, c16, lp, l, tables, final_g, nbatch=nbatch, seq=seq, n_ctx=n_ctx,
                           with_ctx_out=not last, final=last)
    return x2d.reshape(nbatch, seq, d)
```

```python
import functools
import math

import numpy as np
import jax
import jax.numpy as jnp
from jax import lax
from jax.experimental import pallas as pl
from jax.experimental.pallas import tpu as pltpu
from jax.experimental.pallas import tpu_sc as plsc

F32 = jnp.float32
BF16 = jnp.bfloat16
HIGHEST = lax.Precision.HIGHEST

GRID_W = 64
EPS = 1e-6
NEG_INF = -1e30
ROPE_BASE = 10000.0
D_MODEL = 1024
BRANCH_WIDTH = 256
HEAD_DIM = 64
S5_GROUP = 16
S5_GROUPS = 16
S5_STATE = 64
S5_FLAT = S5_GROUPS * S5_STATE
NA_HEADS = 4
NA_WIN_ROWS = 8
NA_WIN_COLS = 16
SWA_KV_HEADS = 2
SWA_WINDOW = 128
DIFF_HEADS = 4
DIFF_QK_DIM = 32
N_EXPERTS = 64
N_EXPERT_GROUPS = 8
TOPK_GROUPS = 4
TOP_K = 6
EXPERT_HIDDEN = 256
ROUTED_SCALE = 2.5
GATE_WIDTH = 4 * D_MODEL

VMEM_LIMIT = 56 * 1024 * 1024


def _params(*sem):
    return pltpu.CompilerParams(dimension_semantics=sem, vmem_limit_bytes=VMEM_LIMIT)


def _nt_dot(a, b):
    return lax.dot_general(a, b, (((1,), (1,)), ((), ())), preferred_element_type=F32)


def _dot(a, b):
    return jnp.dot(a, b, preferred_element_type=F32)


def _rms(x):
    return x * lax.rsqrt(jnp.mean(x * x, axis=-1, keepdims=True) + EPS)


def _ada_kernel(c_ref, w_ref, b_ref, o_ref):
    c = c_ref[...]
    s = c * jax.nn.sigmoid(c)
    o_ref[...] = jnp.dot(s, w_ref[...], preferred_element_type=F32, precision=HIGHEST) + b_ref[...]


def _ada_mod(cc, w, b):
    rows, d = cc.shape
    width = w.shape[1]
    tn = 1536
    return pl.pallas_call(
        _ada_kernel,
        grid=(width // tn,),
        in_specs=[pl.BlockSpec((rows, d), lambda j: (0, 0)),
                  pl.BlockSpec((d, tn), lambda j: (0, j)),
                  pl.BlockSpec((1, tn), lambda j: (0, j))],
        out_specs=pl.BlockSpec((rows, tn), lambda j: (0, j)),
        out_shape=jax.ShapeDtypeStruct((rows, width), F32),
        compiler_params=_params("arbitrary"),
        name="ada_mod",
    )(cc, w, b.reshape(1, width))


_C_GATE = 0
_C_U = GATE_WIDTH
_C_NA = _C_U + 256
_C_SW = _C_NA + 768
_C_DF = _C_SW + 512
_C_END = _C_DF + 768


def _rope_apply(x, cos, sins, half):
    outs = []
    for j in range(x.shape[1] // 128):
        xs = x[:, j * 128:(j + 1) * 128]
        lane = lax.broadcasted_iota(jnp.int32, xs.shape, 1)
        lo = (lane % (2 * half)) < half
        partner = jnp.where(lo, pltpu.roll(xs, 128 - half, 1), pltpu.roll(xs, half, 1))
        outs.append(xs * cos + partner * sins)
    return outs[0] if len(outs) == 1 else jnp.concatenate(outs, axis=1)


def _inproj_kernel(x_ref, g_ref, sc_ref, sh_ref, w_ref, c64_ref, s64_ref, c32_ref, s32_ref,
                   gate_o, u_o, na_o, sw_o, df_o, *, rope):
    h = _rms(x_ref[...]) * g_ref[...]
    h = h * (1.0 + sc_ref[0]) + sh_ref[0]
    hb = h.astype(BF16)

    def mm(c0, c1):
        return _dot(hb, w_ref[:, c0:c1])

    for k in range(GATE_WIDTH // 512):
        gate_o[:, k * 512:(k + 1) * 512] = jax.nn.sigmoid(mm(k * 512, (k + 1) * 512)).astype(BF16)
    u_o[...] = mm(_C_U, _C_U + 256)

    na = mm(_C_NA, _C_NA + 768)
    na_o[:, 0:256] = (na[:, 0:256] * (HEAD_DIM ** -0.5 * (LOG2E if rope else 1.0))).astype(BF16)
    na_o[:, 256:768] = na[:, 256:768].astype(BF16)

    sw = mm(_C_SW, _C_SW + 512)
    swq, swk = sw[:, 0:256], sw[:, 256:384]
    if rope:
        swq = _rope_apply(swq, c64_ref[...], s64_ref[...], 16)
        swk = _rope_apply(swk, c64_ref[...], s64_ref[...], 16)
    sw_o[:, 0:256] = (swq * (HEAD_DIM ** -0.5 * (LOG2E if rope else 1.0))).astype(BF16)
    sw_o[:, 256:384] = swk.astype(BF16)
    sw_o[:, 384:512] = sw[:, 384:512].astype(BF16)

    df = mm(_C_DF, _C_DF + 768)
    dfq, dfk = df[:, 0:256], df[:, 256:512]
    if rope:
        dfq = _rope_apply(dfq, c32_ref[...], s32_ref[...], 8)
        dfk = _rope_apply(dfk, c32_ref[...], s32_ref[...], 8)
    df_o[:, 0:256] = (dfq * (DIFF_QK_DIM ** -0.5 * (LOG2E if rope else 1.0))).astype(BF16)
    df_o[:, 256:512] = dfk.astype(BF16)
    df_o[:, 512:768] = df[:, 512:768].astype(BF16)


def _inproj(x2d, norm_g, sc, sh, w_bf16, tables, *, rows_per_mod, rope, seq):
    rows = x2d.shape[0]
    tm = 512
    assert rows % tm == 0 and rows_per_mod % tm == 0 and seq % tm == 0
    tiles_per_seq = seq // tm

    def mod_map(i):
        return ((i * tm) // rows_per_mod, 0, 0)

    def tab_map(i):
        return (i % tiles_per_seq, 0)

    tab_spec = pl.BlockSpec((tm, 128), tab_map)
    row = lambda w: pl.BlockSpec((tm, w), lambda i: (i, 0))
    return pl.pallas_call(
        functools.partial(_inproj_kernel, rope=rope),
        grid=(rows // tm,),
        in_specs=[row(D_MODEL),
                  pl.BlockSpec((1, D_MODEL), lambda i: (0, 0)),
                  pl.BlockSpec((1, 1, D_MODEL), mod_map),
                  pl.BlockSpec((1, 1, D_MODEL), mod_map),
                  pl.BlockSpec((D_MODEL, _C_END), lambda i: (0, 0)),
                  tab_spec, tab_spec, tab_spec, tab_spec],
        out_specs=[row(GATE_WIDTH), row(256), row(768), row(512), row(768)],
        out_shape=[jax.ShapeDtypeStruct((rows, GATE_WIDTH), BF16),
                   jax.ShapeDtypeStruct((rows, 256), F32),
                   jax.ShapeDtypeStruct((rows, 768), BF16),
                   jax.ShapeDtypeStruct((rows, 512), BF16),
                   jax.ShapeDtypeStruct((rows, 768), BF16)],
        compiler_params=_params("arbitrary"),
        name="inproj",
    )(x2d, norm_g.reshape(1, D_MODEL), sc, sh, w_bf16, *tables)


def _rope_tables(seq):
    t = jnp.arange(seq)
    rows = (t // GRID_W).astype(F32)
    cols = (t % GRID_W).astype(F32)
    lane = np.arange(128)
    out = []
    for dim in (64, 32):
        quarter = dim // 4
        inv_freq = ROPE_BASE ** (-jnp.arange(quarter, dtype=F32) / quarter)
        l = lane % dim
        use_col = l >= dim // 2
        fidx = l % quarter
        hi = (l % (dim // 2)) >= quarter
        ang_r = rows[:, None] * inv_freq[None, :]
        ang_c = cols[:, None] * inv_freq[None, :]
        ang = jnp.where(use_col[None, :], ang_c[:, fidx], ang_r[:, fidx])
        out.append(jnp.cos(ang))
        out.append(jnp.where(hi[None, :], jnp.sin(ang), -jnp.sin(ang)))
    return tuple(out)


S5_CHUNK = 128


def _s5_kernel(u_ref, win_ref, wout_ref, are_ref, aim_ref, y_ref, bu_ref, st_ref, *, tc, nb):
    d = pl.program_id(0)
    i = pl.program_id(1)

    @pl.when(i == 0)
    def _():
        st_ref[...] = jnp.zeros_like(st_ref)

    u = u_ref[...].reshape(tc * nb, BRANCH_WIDTH).astype(BF16)
    bu_ref[...] = _dot(u, win_ref[0])
    ar = jnp.broadcast_to(are_ref[0], (nb, S5_FLAT))
    ai = jnp.broadcast_to(aim_ref[0], (nb, S5_FLAT))

    def body(j, carry):
        xr, xi = carry
        t = j + d * (tc - 1 - 2 * j)
        row = pl.multiple_of(t * nb, nb)
        br = bu_ref[pl.ds(row, nb), 0:S5_FLAT]
        bi = bu_ref[pl.ds(row, nb), S5_FLAT:2 * S5_FLAT]
        nr = ar * xr - ai * xi + br
        ni = ar * xi + ai * xr + bi
        bu_ref[pl.ds(row, nb), 0:S5_FLAT] = nr
        bu_ref[pl.ds(row, nb), S5_FLAT:2 * S5_FLAT] = ni
        return nr, ni

    xr, xi = lax.fori_loop(0, tc, body, (st_ref[:, 0:S5_FLAT], st_ref[:, S5_FLAT:2 * S5_FLAT]), unroll=4)
    st_ref[:, 0:S5_FLAT] = xr
    st_ref[:, S5_FLAT:2 * S5_FLAT] = xi
    y = _dot(bu_ref[...].astype(BF16), wout_ref[0])
    y_ref[0] = y.reshape(tc, nb, BRANCH_WIDTH)


def _s5_scan(u_tm, win, wout, a_re, a_im, n_ctx):
    s_len, nb, _ = u_tm.shape
    tc = S5_CHUNK
    assert nb == 8 and s_len % tc == 0 and n_ctx % tc == 0
    nct = n_ctx // tc
    nlt = (s_len - n_ctx) // tc

    def chunk(d, i):
        rev = jnp.where(i < nct, nct - 1 - i, 2 * nct + nlt - 1 - i)
        return jnp.where(d == 0, i, rev)

    return pl.pallas_call(
        functools.partial(_s5_kernel, tc=tc, nb=nb),
        grid=(2, nct + nlt),
        in_specs=[pl.BlockSpec((tc, nb, BRANCH_WIDTH), lambda d, i: (chunk(d, i), 0, 0)),
                  pl.BlockSpec((1, BRANCH_WIDTH, 2 * S5_FLAT), lambda d, i: (d, 0, 0)),
                  pl.BlockSpec((1, 2 * S5_FLAT, BRANCH_WIDTH), lambda d, i: (d, 0, 0)),
                  pl.BlockSpec((1, 1, S5_FLAT), lambda d, i: (d, 0, 0)),
                  pl.BlockSpec((1, 1, S5_FLAT), lambda d, i: (d, 0, 0))],
        out_specs=pl.BlockSpec((1, tc, nb, BRANCH_WIDTH), lambda d, i: (d, chunk(d, i), 0, 0)),
        out_shape=jax.ShapeDtypeStruct((2, s_len, nb, BRANCH_WIDTH), F32),
        scratch_shapes=[pltpu.VMEM((tc * nb, 2 * S5_FLAT), F32),
                        pltpu.VMEM((nb, 2 * S5_FLAT), F32)],
        compiler_params=_params("arbitrary", "arbitrary"),
        name="s5_scan",
    )(u_tm, win, wout, a_re, a_im)


def _s5_params(lam_re, lam_im, log_step, b_re, b_im, c_re, c_im):
    lr = lam_re.astype(F32)
    li = lam_im.astype(F32)
    dt = jnp.exp(log_step.astype(F32))[..., None]
    mag = jnp.exp(lr * dt)
    a_re = mag * jnp.cos(li * dt)
    a_im = mag * jnp.sin(li * dt)
    nr, ni, den = a_re - 1.0, a_im, lr * lr + li * li
    k_re = ((nr * lr + ni * li) / den)[..., None]
    k_im = ((ni * lr - nr * li) / den)[..., None]
    br = b_re.astype(F32)
    bi = b_im.astype(F32)
    bb_re = k_re * br - k_im * bi
    bb_im = k_re * bi + k_im * br
    eye = jnp.eye(S5_GROUPS, dtype=F32)

    def blockdiag_in(bb):
        m = jnp.einsum('dgpc,gh->dgchp', bb, eye)
        return m.reshape(2, S5_GROUPS * S5_GROUP, S5_GROUPS * S5_STATE)

    def blockdiag_out(cc):
        m = jnp.einsum('dgcp,gh->dgphc', cc, eye)
        return m.reshape(2, S5_GROUPS * S5_STATE, S5_GROUPS * S5_GROUP)

    win = jnp.concatenate([blockdiag_in(bb_re), blockdiag_in(bb_im)], axis=2).astype(BF16)
    wout = jnp.concatenate([blockdiag_out(c_re.astype(F32)), -blockdiag_out(c_im.astype(F32))], axis=1).astype(BF16)
    return win, wout, a_re.reshape(2, 1, S5_FLAT), a_im.reshape(2, 1, S5_FLAT)


def _softmax_parts(scores, extra=None):
    m = scores[0].max(axis=-1, keepdims=True)
    for s in scores[1:]:
        m = jnp.maximum(m, s.max(axis=-1, keepdims=True))
    if extra is not None:
        m = jnp.maximum(m, extra)
    ps = [jnp.exp(s - m) for s in scores]
    l = ps[0].sum(axis=-1, keepdims=True)
    for p in ps[1:]:
        l = l + p.sum(axis=-1, keepdims=True)
    if extra is not None:
        l = l + jnp.exp(extra - m)
    return ps, l


NA_QROWS = 2
NA_KROWS = 10
VROWS = HEAD_DIM + 16


def _na_window_start(r, grid_rows):
    start = jnp.clip(r - NA_WIN_ROWS // 2, 0, grid_rows - NA_WIN_ROWS)
    return (jnp.minimum(start, grid_rows - NA_KROWS) // 2) * 2


def _head_blockdiag(qt, qbd_ref, nheads, rows_per_head):
    n = qt.shape[1]
    row_h = lax.broadcasted_iota(jnp.int32, qt.shape, 0) // rows_per_head
    zero = jnp.zeros_like(qt)
    for h in range(nheads):
        qbd_ref[:, h * n:(h + 1) * n] = jnp.where(row_h == h, qt, zero)


def _na_kernel(qt_ref, k_ref, vt_ref, kc_ref, vct_ref, bias_ref, o_ref, qbd_ref, *, grid_rows):
    nq = NA_QROWS * GRID_W
    nk = NA_KROWS * GRID_W
    off = pl.multiple_of(_na_window_start(NA_QROWS * pl.program_id(1), grid_rows) * GRID_W, 128)
    _head_blockdiag(qt_ref[0], qbd_ref, NA_HEADS, HEAD_DIM)
    qbd = qbd_ref[...]
    s_loc = _dot(k_ref[0, pl.ds(off, nk), :], qbd) + bias_ref[0]
    s_ctx = _dot(kc_ref[0], qbd)
    m = jnp.maximum(s_loc.max(axis=0, keepdims=True), s_ctx.max(axis=0, keepdims=True))
    p_loc = jnp.exp2((s_loc - m).astype(BF16))
    p_ctx = jnp.exp2((s_ctx - m).astype(BF16))
    vw = vt_ref[0, :, pl.ds(off, nk)]
    outs = []
    for h in range(NA_HEADS):
        rows = slice(h * VROWS, (h + 1) * VROWS)
        cols = slice(h * nq, (h + 1) * nq)
        o = _dot(vw[rows], p_loc[:, cols]) + _dot(vct_ref[0, rows, :], p_ctx[:, cols])
        outs.append(o[0:HEAD_DIM] / o[HEAD_DIM:HEAD_DIM + 1])
    o_ref[...] = jnp.concatenate(outs, axis=0).T.astype(BF16)


def _na_classes(grid_rows):
    return [0, 2, 4, grid_rows - 4, grid_rows - 2]


def _na_bias_table(rpb, grid_rows):
    col = np.arange(GRID_W)
    cstart = np.clip(col - NA_WIN_COLS // 2, 0, GRID_W - NA_WIN_COLS)
    col_in = (col[None, :] >= cstart[:, None]) & (col[None, :] < cstart[:, None] + NA_WIN_COLS)
    cb = np.clip(col[None, :] - col[:, None] + (NA_WIN_COLS - 1), 0, 2 * NA_WIN_COLS - 2)
    classes = _na_classes(grid_rows)
    rbi = np.zeros((len(classes), NA_QROWS, NA_KROWS), np.int64)
    row_in = np.zeros((len(classes), NA_QROWS, NA_KROWS), bool)
    for c, r in enumerate(classes):
        a_row = (min(int(np.clip(r - NA_WIN_ROWS // 2, 0, grid_rows - NA_WIN_ROWS)), grid_rows - NA_KROWS) // 2) * 2
        for qi in range(NA_QROWS):
            start_q = int(np.clip(r + qi - NA_WIN_ROWS // 2, 0, grid_rows - NA_WIN_ROWS))
            for j in range(NA_KROWS):
                row_in[c, qi, j] = start_q <= a_row + j < start_q + NA_WIN_ROWS
                rbi[c, qi, j] = np.clip(a_row + j - (r + qi) + NA_WIN_ROWS - 1, 0, 2 * NA_WIN_ROWS - 2)
    oh_row = jnp.asarray(rbi[..., None] == np.arange(2 * NA_WIN_ROWS - 1), F32)
    oh_col = jnp.asarray(cb[:, :, None] == np.arange(2 * NA_WIN_COLS - 1), F32)
    t = jnp.einsum('hab,cija,qkb->cjkhiq', rpb.astype(F32), oh_row, oh_col, precision=HIGHEST) * LOG2E
    valid = row_in.transpose(0, 2, 1)[:, :, None, None, :, None] & col_in.T[None, None, :, None, None, :]
    t = jnp.where(valid, t, NEG_INF)
    return t.reshape(len(classes), NA_KROWS * GRID_W, NA_HEADS * NA_QROWS * GRID_W)


def _augment_vt(v, nheads):
    nb, s, _ = v.shape
    v = v.reshape(nb, s, nheads, HEAD_DIM)
    pad = jnp.zeros((nb, s, nheads, VROWS - HEAD_DIM), v.dtype).at[..., 0].set(1.0)
    return jnp.concatenate([v, pad], axis=-1).reshape(nb, s, nheads * VROWS).transpose(0, 2, 1)


def _na_attention(na, na_c, bias, nbatch, seq, n_ctx):
    grid_rows = seq // GRID_W
    assert grid_rows >= NA_KROWS and grid_rows % NA_QROWS == 0
    nq = NA_QROWS * GRID_W
    npair = grid_rows // NA_QROWS
    nal = na.reshape(nbatch, seq, 768)
    nac = na_c.reshape(nbatch, n_ctx, 768)
    qt = nal[:, :, 0:256].transpose(0, 2, 1)
    vt = _augment_vt(nal[:, :, 512:768], NA_HEADS)
    vct = _augment_vt(nac[:, :, 512:768], NA_HEADS)

    def cls(b, p):
        r = NA_QROWS * p
        c = jnp.where(r < 4, r // 2, jnp.where(r >= grid_rows - 4, (r - (grid_rows - 4)) // 2 + 3, 2))
        return (c, 0, 0)

    return pl.pallas_call(
        functools.partial(_na_kernel, grid_rows=grid_rows),
        grid=(nbatch, npair),
        in_specs=[pl.BlockSpec((1, 256, nq), lambda b, p: (b, 0, p)),
                  pl.BlockSpec((1, seq, 256), lambda b, p: (b, 0, 1)),
                  pl.BlockSpec((1, NA_HEADS * VROWS, seq), lambda b, p: (b, 0, 0)),
                  pl.BlockSpec((1, n_ctx, 256), lambda b, p: (b, 0, 1)),
                  pl.BlockSpec((1, NA_HEADS * VROWS, n_ctx), lambda b, p: (b, 0, 0)),
                  pl.BlockSpec((1, NA_KROWS * GRID_W, NA_HEADS * nq), cls)],
        out_specs=pl.BlockSpec((nq, 256), lambda b, p: (b * npair + p, 0)),
        out_shape=jax.ShapeDtypeStruct((nbatch * seq, 256), BF16),
        scratch_shapes=[pltpu.VMEM((256, NA_HEADS * nq), BF16)],
        compiler_params=_params("arbitrary", "arbitrary"),
        name="na_attention",
    )(qt, nal, vt, nac, vct, bias)


SWA_BLOCK = 128


def _swa_kernel(qt_ref, k_ref, vt_ref, kc_ref, vct_ref, sink_ref, o_ref, qbd_ref, *, seq):
    n = pl.program_id(1)
    band = 3 * SWA_BLOCK
    nq = SWA_BLOCK
    nqh = 2 * SWA_KV_HEADS
    bstart = pl.multiple_of(jnp.clip((n - 1) * SWA_BLOCK, 0, seq - band), SWA_BLOCK)
    qt = qt_ref[0]
    zero = jnp.zeros((HEAD_DIM, nq), qt.dtype)
    for hq in range(nqh):
        qh = qt[hq * HEAD_DIM:(hq + 1) * HEAD_DIM]
        qbd_ref[:, hq * nq:(hq + 1) * nq] = jnp.concatenate([qh, zero] if hq // 2 == 0 else [zero, qh], axis=0)
    qbd = qbd_ref[...]
    kpos = bstart + lax.broadcasted_iota(jnp.int32, (band, nqh * nq), 0)
    qpos = n * SWA_BLOCK + lax.broadcasted_iota(jnp.int32, (band, nqh * nq), 1) % nq
    s_loc = jnp.where(jnp.abs(qpos - kpos) <= SWA_WINDOW, _dot(k_ref[0, pl.ds(bstart, band), :], qbd), NEG_INF)
    s_ctx = _dot(kc_ref[0], qbd)
    sink = jnp.concatenate([jnp.broadcast_to(sink_ref[0:1, hq:hq + 1] * LOG2E, (1, nq)) for hq in range(nqh)], axis=1)
    m = jnp.maximum(jnp.maximum(s_loc.max(axis=0, keepdims=True), s_ctx.max(axis=0, keepdims=True)), sink)
    p_loc = jnp.exp2((s_loc - m).astype(BF16))
    p_ctx = jnp.exp2((s_ctx - m).astype(BF16))
    p_sink = jnp.exp2(sink - m)
    vw = vt_ref[0, :, pl.ds(bstart, band)]
    outs = []
    for hq in range(nqh):
        rows = slice((hq // 2) * VROWS, (hq // 2 + 1) * VROWS)
        cols = slice(hq * nq, (hq + 1) * nq)
        o = _dot(vw[rows], p_loc[:, cols]) + _dot(vct_ref[0, rows, :], p_ctx[:, cols])
        outs.append(o[0:HEAD_DIM] / (o[HEAD_DIM:HEAD_DIM + 1] + p_sink[:, cols]))
    o_ref[...] = jnp.concatenate(outs, axis=0).T.astype(BF16)


def _swa_attention(sw, sw_c, sink, nbatch, seq, n_ctx):
    nblk = seq // SWA_BLOCK
    nqh = 2 * SWA_KV_HEADS
    assert seq >= 3 * SWA_BLOCK
    sink_pad = jnp.zeros((1, 128), F32).at[0, 0:nqh].set(sink.astype(F32))
    swl = sw.reshape(nbatch, seq, 512)
    swc = sw_c.reshape(nbatch, n_ctx, 512)
    qt = swl[:, :, 0:256].transpose(0, 2, 1)
    vt = _augment_vt(swl[:, :, 384:512], SWA_KV_HEADS)
    vct = _augment_vt(swc[:, :, 384:512], SWA_KV_HEADS)
    return pl.pallas_call(
        functools.partial(_swa_kernel, seq=seq),
        grid=(nbatch, nblk),
        in_specs=[pl.BlockSpec((1, 256, SWA_BLOCK), lambda b, n: (b, 0, n)),
                  pl.BlockSpec((1, seq, 128), lambda b, n: (b, 0, 2)),
                  pl.BlockSpec((1, SWA_KV_HEADS * VROWS, seq), lambda b, n: (b, 0, 0)),
                  pl.BlockSpec((1, n_ctx, 128), lambda b, n: (b, 0, 2)),
                  pl.BlockSpec((1, SWA_KV_HEADS * VROWS, n_ctx), lambda b, n: (b, 0, 0)),
                  pl.BlockSpec((1, 128), lambda b, n: (0, 0))],
        out_specs=pl.BlockSpec((SWA_BLOCK, 256), lambda b, n: (b * nblk + n, 0)),
        out_shape=jax.ShapeDtypeStruct((nbatch * seq, 256), BF16),
        scratch_shapes=[pltpu.VMEM((SWA_KV_HEADS * HEAD_DIM, nqh * SWA_BLOCK), BF16)],
        compiler_params=_params("arbitrary", "arbitrary"),
        name="swa_attention",
    )(qt, swl, vt, swc, vct, sink_pad)


DIFF_TQ = 512
DIFF_CK = 256
LOG2E = math.log2(math.e)
DIFF_VROWS = HEAD_DIM + 16


def _diff_lambda(lq1_ref, lk1_ref, lq2_ref, lk2_ref, lambda_init):
    s1 = jnp.sum(lq1_ref[...] * lk1_ref[...], axis=-1, keepdims=True)
    s2 = jnp.sum(lq2_ref[...] * lk2_ref[...], axis=-1, keepdims=True)
    return jnp.exp(s1) - jnp.exp(s2) + lambda_init


def _stack_maps(qh):
    lane = lax.broadcasted_iota(jnp.int32, qh.shape, 1)
    zero = jnp.zeros_like(qh)
    return jnp.concatenate([jnp.where(lane < DIFF_QK_DIM, qh, zero),
                            jnp.where(lane >= DIFF_QK_DIM, qh, zero)], axis=0)


def _subln(o0, o1, lam, g, lambda_init):
    o = o0 - lam * o1
    return _rms(o) * g * (1.0 - lambda_init)


def _diff_kernel(qt_ref, k_ref, vt_ref, lq1_ref, lk1_ref, lq2_ref, lk2_ref, g_ref, o_ref, qbd_ref, acc_ref, s_ref,
                 *, nchunk, lambda_init):
    lam = _diff_lambda(lq1_ref, lk1_ref, lq2_ref, lk2_ref, lambda_init)
    qt = qt_ref[0]
    tq = qt.shape[1]
    w = 2 * tq
    row = lax.broadcasted_iota(jnp.int32, qt.shape, 0) // DIFF_QK_DIM
    zero = jnp.zeros_like(qt)
    for j in range(2 * DIFF_HEADS):
        qbd_ref[:, j * tq:(j + 1) * tq] = jnp.where(row == j, qt, zero)
    acc_ref[...] = jnp.zeros_like(acc_ref)

    def scores(slot, c, h):
        s = _dot(k_ref[0, c], qbd_ref[:, h * w:(h + 1) * w])
        s_ref[slot, h] = s
        return s.max(axis=0, keepdims=True)

    def softmax_pv(slot, c, h, m_run, m_chunk):
        m_new = jnp.maximum(m_run, m_chunk)
        alpha = jnp.exp2(m_run - m_new)
        p = jnp.exp2((s_ref[slot, h] - m_new).astype(BF16))
        acc_ref[h] = alpha * acc_ref[h] + _dot(vt_ref[0, c, h * DIFF_VROWS:(h + 1) * DIFF_VROWS, :], p)
        return m_new

    def step(slot, c, carry):
        m_run, m_chunk = carry
        new_run, new_chunk = [], []
        for h in range(DIFF_HEADS):
            new_chunk.append(scores(1 - slot, c + 1, h))
            new_run.append(softmax_pv(slot, c, h, m_run[h], m_chunk[h]))
        return tuple(new_run), tuple(new_chunk)

    def body(i, carry):
        c = 2 * i
        return step(1, c + 1, step(0, c, carry))

    m_run = tuple(jnp.full((1, w), NEG_INF, F32) for _ in range(DIFF_HEADS))
    m_chunk = tuple(scores(0, 0, h) for h in range(DIFF_HEADS))
    npair = (nchunk - 1) // 2
    m_run, m_chunk = lax.fori_loop(0, npair, body, (m_run, m_chunk))
    if nchunk % 2 == 0:
        m_run, m_chunk = step(0, nchunk - 2, (m_run, m_chunk))
        for h in range(DIFF_HEADS):
            softmax_pv(1, nchunk - 1, h, m_run[h], m_chunk[h])
    else:
        for h in range(DIFF_HEADS):
            softmax_pv(0, nchunk - 1, h, m_run[h], m_chunk[h])
    outs = []
    for h in range(DIFF_HEADS):
        o = acc_ref[h, 0:HEAD_DIM, :] / acc_ref[h, HEAD_DIM:HEAD_DIM + 1, :]
        d = o[:, 0:tq] - lam * o[:, tq:w]
        d = d * lax.rsqrt(jnp.mean(d * d, axis=0, keepdims=True) + EPS)
        outs.append(d * g_ref[...] * (1.0 - lambda_init))
    o_ref[...] = jnp.concatenate(outs, axis=0).T.astype(BF16)


def _diff_attention(df, df_c, lqk, subln_g, lambda_init, nbatch, seq, n_ctx):
    tq, ck = min(DIFF_TQ, seq), DIFF_CK
    s_all = seq + n_ctx
    assert s_all % ck == 0 and seq % tq == 0
    nchunk = s_all // ck
    nq = seq // tq
    dfl = df.reshape(nbatch, seq, 768)
    dfc = df_c.reshape(nbatch, n_ctx, 768)
    qt = dfl[:, :, 0:256].transpose(0, 2, 1)
    k_all = jnp.concatenate([dfl[:, :, 256:512], dfc[:, :, 256:512]], axis=1).reshape(nbatch, nchunk, ck, 256)
    v_all = jnp.concatenate([dfl[:, :, 512:768], dfc[:, :, 512:768]], axis=1)
    v_all = v_all.reshape(nbatch, nchunk, ck, DIFF_HEADS, HEAD_DIM)
    pad = jnp.zeros((nbatch, nchunk, ck, DIFF_HEADS, DIFF_VROWS - HEAD_DIM), BF16).at[..., 0].set(1.0)
    vt_all = jnp.concatenate([v_all, pad], axis=-1).reshape(nbatch, nchunk, ck, DIFF_HEADS * DIFF_VROWS)
    vt_all = vt_all.transpose(0, 1, 3, 2)
    vec = pl.BlockSpec((1, DIFF_QK_DIM), lambda b, n: (0, 0))
    return pl.pallas_call(
        functools.partial(_diff_kernel, nchunk=nchunk, lambda_init=lambda_init),
        grid=(nbatch, nq),
        in_specs=[pl.BlockSpec((1, 256, tq), lambda b, n: (b, 0, n)),
                  pl.BlockSpec((1, nchunk, ck, 256), lambda b, n: (b, 0, 0, 0)),
                  pl.BlockSpec((1, nchunk, DIFF_HEADS * DIFF_VROWS, ck), lambda b, n: (b, 0, 0, 0)),
                  vec, vec, vec, vec,
                  pl.BlockSpec((HEAD_DIM, 1), lambda b, n: (0, 0))],
        out_specs=pl.BlockSpec((tq, 256), lambda b, n: (b * nq + n, 0)),
        out_shape=jax.ShapeDtypeStruct((nbatch * seq, 256), BF16),
        scratch_shapes=[pltpu.VMEM((256, 2 * DIFF_HEADS * tq), BF16),
                        pltpu.VMEM((DIFF_HEADS, DIFF_VROWS, 2 * tq), F32),
                        pltpu.VMEM((2, DIFF_HEADS, ck, 2 * tq), F32)],
        compiler_params=_params("arbitrary", "arbitrary"),
        name="diff_attention",
    )(qt, k_all, vt_all, *lqk, subln_g.reshape(HEAD_DIM, 1).astype(F32))


def _ctx_attn_kernel(na_ref, sw_ref, df_ref, sink_ref, lq1_ref, lk1_ref, lq2_ref, lk2_ref, g_ref,
                     nb_o, sw_o, df_o, *, lambda_init):
    n = na_ref.shape[0]
    na = na_ref[...]
    outs = []
    for h in range(NA_HEADS):
        sl = slice(h * HEAD_DIM, (h + 1) * HEAD_DIM)
        (p,), l = _softmax_parts([_nt_dot(na[:, sl], na[:, 256 + h * HEAD_DIM:256 + (h + 1) * HEAD_DIM])])
        outs.append(_dot(p.astype(BF16), na[:, 512 + h * HEAD_DIM:512 + (h + 1) * HEAD_DIM]) / l)
    nb_o[...] = jnp.concatenate(outs, axis=1).astype(BF16)
    sw = sw_ref[...]
    outs = []
    for hq in range(4):
        kv = hq // 2
        k = sw[:, 256 + kv * HEAD_DIM:256 + (kv + 1) * HEAD_DIM]
        v = sw[:, 384 + kv * HEAD_DIM:384 + (kv + 1) * HEAD_DIM]
        sk = jnp.broadcast_to(sink_ref[0:1, hq:hq + 1], (n, 1))
        (p,), l = _softmax_parts([_nt_dot(sw[:, hq * HEAD_DIM:(hq + 1) * HEAD_DIM], k)], extra=sk)
        outs.append(_dot(p.astype(BF16), v) / l)
    sw_o[...] = jnp.concatenate(outs, axis=1).astype(BF16)
    lam = _diff_lambda(lq1_ref, lk1_ref, lq2_ref, lk2_ref, lambda_init)
    df = df_ref[...]
    outs = []
    for h in range(DIFF_HEADS):
        sl = slice(h * HEAD_DIM, (h + 1) * HEAD_DIM)
        q2 = _stack_maps(df[:, sl])
        (p,), l = _softmax_parts([_nt_dot(q2, df[:, 256 + h * HEAD_DIM:256 + (h + 1) * HEAD_DIM])])
        o = _dot(p.astype(BF16), df[:, 512 + h * HEAD_DIM:512 + (h + 1) * HEAD_DIM]) / l
        outs.append(_subln(o[0:n], o[n:2 * n], lam, g_ref[...], lambda_init))
    df_o[...] = jnp.concatenate(outs, axis=1).astype(BF16)


def _ctx_attention(na_c, sw_c, df_c, sink, lqk, subln_g, lambda_init, nbatch, n_ctx):
    sink_pad = jnp.zeros((1, 128), F32).at[0, 0:4].set(sink.astype(F32))
    vec = pl.BlockSpec((1, DIFF_QK_DIM), lambda b: (0, 0))
    out = jax.ShapeDtypeStruct((nbatch * n_ctx, 256), BF16)
    return pl.pallas_call(
        functools.partial(_ctx_attn_kernel, lambda_init=lambda_init),
        grid=(nbatch,),
        in_specs=[pl.BlockSpec((n_ctx, 768), lambda b: (b, 0)),
                  pl.BlockSpec((n_ctx, 512), lambda b: (b, 0)),
                  pl.BlockSpec((n_ctx, 768), lambda b: (b, 0)),
                  pl.BlockSpec((1, 128), lambda b: (0, 0)),
                  vec, vec, vec, vec,
                  pl.BlockSpec((1, HEAD_DIM), lambda b: (0, 0))],
        out_specs=[pl.BlockSpec((n_ctx, 256), lambda b: (b, 0))] * 3,
        out_shape=[out, out, out],
        compiler_params=_params("arbitrary"),
        name="ctx_attention",
    )(na_c, sw_c, df_c, sink_pad, *lqk, subln_g.reshape(1, HEAD_DIM).astype(F32))


def _merge_kernel(u_ref, yf_ref, yr_ref, d_ref, gw_ref, gb_ref, yb_ref, yc_ref, yd_ref, gate_ref,
                  wb_ref, wo_ref, x_ref, g1_ref, n2_ref, sc2_ref, sh2_ref, rw_ref,
                  x1_o, h2_o, h2p_o, lg_o):
    y = u_ref[...] * d_ref[...] + yf_ref[...] + yr_ref[...]
    a = jax.nn.gelu(y, approximate=True)
    ya = a * jax.nn.sigmoid(_dot(a.astype(BF16), gw_ref[...]) + gb_ref[...])
    branches = (ya.astype(BF16), yb_ref[...], yc_ref[...], yd_ref[...])
    acc = None
    for i in range(4):
        t = gate_ref[:, i * D_MODEL:(i + 1) * D_MODEL].astype(F32) * _dot(branches[i], wb_ref[i])
        acc = t if acc is None else acc + t
    mixed = _dot(acc.astype(BF16), wo_ref[...])
    x1 = x_ref[...] + g1_ref[0] * mixed
    x1_o[...] = x1
    h2 = _rms(x1) * n2_ref[...]
    h2 = h2 * (1.0 + sc2_ref[0]) + sh2_ref[0]
    h2_o[...] = h2.astype(BF16)
    h2p_o[...] = _pack_rows(h2[:, 0:HALF_D], h2[:, HALF_D:D_MODEL])
    lg_o[...] = jnp.dot(h2, rw_ref[...], preferred_element_type=F32, precision=HIGHEST)


def _merge(u, yf, yr, s5_d, glu_w, glu_b, yb, yc, yd, gate, wb, wo, x2d, g1, norm2_g, sc2, sh2, router_w,
           *, rows_per_mod):
    rows = x2d.shape[0]
    tm = 512
    assert rows % tm == 0 and rows_per_mod % tm == 0

    def mod_map(i):
        return ((i * tm) // rows_per_mod, 0, 0)

    row = lambda w: pl.BlockSpec((tm, w), lambda i: (i, 0))
    full = lambda *shape: pl.BlockSpec(shape, lambda i: (0,) * len(shape))
    mod = pl.BlockSpec((1, 1, D_MODEL), mod_map)
    return pl.pallas_call(
        _merge_kernel,
        grid=(rows // tm,),
        in_specs=[row(256), row(256), row(256), full(1, 256), full(256, 256), full(1, 256),
                  row(256), row(256), row(256), row(GATE_WIDTH),
                  full(4, 256, D_MODEL), full(D_MODEL, D_MODEL), row(D_MODEL),
                  mod, full(1, D_MODEL), mod, mod, full(D_MODEL, N_EXPERTS)],
        out_specs=[row(D_MODEL), row(D_MODEL), row(HALF_D), row(N_EXPERTS)],
        out_shape=[jax.ShapeDtypeStruct((rows, D_MODEL), F32),
                   jax.ShapeDtypeStruct((rows, D_MODEL), BF16),
                   jax.ShapeDtypeStruct((rows, HALF_D), jnp.uint32),
                   jax.ShapeDtypeStruct((rows, N_EXPERTS), F32)],
        compiler_params=_params("arbitrary"),
        name="merge",
    )(u, yf, yr, s5_d.reshape(1, 256).astype(F32), glu_w.astype(BF16), glu_b.reshape(1, 256).astype(F32),
      yb, yc, yd, gate, wb.astype(BF16), wo.astype(BF16), x2d, g1, norm2_g.reshape(1, D_MODEL), sc2, sh2,
      router_w.astype(F32))


def _router_kernel(lg_ref, b_ref, tri_ref, idx_ref, rank_ref, w_ref, cnt_ref, base_ref):
    tr = lg_ref.shape[1]
    gsz = N_EXPERTS // N_EXPERT_GROUPS
    sc = jax.nn.sigmoid(lg_ref[...])
    bi = sc + b_ref[...]
    e_iota = lax.broadcasted_iota(jnp.int32, (gsz, tr), 0).astype(F32)
    groups = [bi[g * gsz:(g + 1) * gsz] for g in range(N_EXPERT_GROUPS)]
    gs = []
    for bg in groups:
        m1 = bg.max(axis=0, keepdims=True)
        i1 = jnp.where(bg == m1, e_iota, float(gsz)).min(axis=0, keepdims=True)
        m2 = jnp.where(e_iota == i1, -jnp.inf, bg).max(axis=0, keepdims=True)
        gs.append(m1 + m2)
    v = []
    for g in range(N_EXPERT_GROUPS):
        rank = jnp.zeros((1, tr), F32)
        for g2 in range(N_EXPERT_GROUPS):
            if g2 == g:
                continue
            beats = (gs[g2] >= gs[g]) if g2 < g else (gs[g2] > gs[g])
            rank = rank + jnp.where(beats, 1.0, 0.0)
        v.append(jnp.where(rank < TOPK_GROUPS, groups[g], NEG_INF))
    flat = [e_iota + float(g * gsz) for g in range(N_EXPERT_GROUPS)]
    sel = [jnp.zeros((gsz, tr), F32) for _ in range(N_EXPERT_GROUPS)]
    picks = []
    for _ in range(TOP_K):
        m = v[0].max(axis=0, keepdims=True)
        for g in range(1, N_EXPERT_GROUPS):
            m = jnp.maximum(m, v[g].max(axis=0, keepdims=True))
        am = jnp.where(v[0] == m, flat[0], float(N_EXPERTS)).min(axis=0, keepdims=True)
        for g in range(1, N_EXPERT_GROUPS):
            am = jnp.minimum(am, jnp.where(v[g] == m, flat[g], float(N_EXPERTS)).min(axis=0, keepdims=True))
        hits = []
        for g in range(N_EXPERT_GROUPS):
            hit = flat[g] == am
            hits.append(hit)
            sel[g] = jnp.where(hit, 1.0, sel[g])
            v[g] = jnp.where(hit, -jnp.inf, v[g])
        picks.append((am, hits))
    scg = [sc[g * gsz:(g + 1) * gsz] for g in range(N_EXPERT_GROUPS)]
    den = (sel[0] * scg[0]).sum(axis=0, keepdims=True)
    for g in range(1, N_EXPERT_GROUPS):
        den = den + (sel[g] * scg[g]).sum(axis=0, keepdims=True)

    @pl.when(pl.program_id(0) == 0)
    def _():
        base_ref[...] = jnp.zeros_like(base_ref)

    sel_all = jnp.concatenate(sel, axis=0)
    before = _dot(sel_all.astype(jnp.bfloat16), tri_ref[...]) + base_ref[...]
    for k, (am, hits) in enumerate(picks):
        wk = jnp.zeros((1, tr), F32)
        rk = jnp.zeros((1, tr), F32)
        for g in range(N_EXPERT_GROUPS):
            wk = wk + jnp.where(hits[g], scg[g], 0.0).sum(axis=0, keepdims=True)
            rk = rk + jnp.where(hits[g], before[g * gsz:(g + 1) * gsz], 0.0).sum(axis=0, keepdims=True)
        idx_ref[k:k + 1, :] = am.astype(jnp.int32)
        rank_ref[k:k + 1, :] = rk.astype(jnp.int32)
        w_ref[k:k + 1, :] = wk / den * ROUTED_SCALE
    idx_ref[TOP_K:8, :] = jnp.zeros((8 - TOP_K, tr), jnp.int32)
    rank_ref[TOP_K:8, :] = jnp.zeros((8 - TOP_K, tr), jnp.int32)
    w_ref[TOP_K:8, :] = jnp.zeros((8 - TOP_K, tr), F32)
    base_ref[...] += sel_all.sum(axis=1, keepdims=True)
    cnt_ref[...] = base_ref[...].astype(jnp.int32)


ROUTER_TILE = 512


def _router(logits_t, router_b):
    ne, rows = logits_t.shape
    tr = ROUTER_TILE
    assert rows % tr == 0
    tri = jnp.asarray(np.triu(np.ones((tr, tr), np.float32), k=1), jnp.bfloat16)
    pick = pl.BlockSpec((8, tr), lambda i: (0, i))
    return pl.pallas_call(
        _router_kernel,
        grid=(rows // tr,),
        in_specs=[pl.BlockSpec((ne, tr), lambda i: (0, i)),
                  pl.BlockSpec((ne, 1), lambda i: (0, 0)),
                  pl.BlockSpec((tr, tr), lambda i: (0, 0))],
        out_specs=[pick, pick, pick, pl.BlockSpec((ne, 1), lambda i: (0, 0))],
        out_shape=[jax.ShapeDtypeStruct((8, rows), jnp.int32),
                   jax.ShapeDtypeStruct((8, rows), jnp.int32),
                   jax.ShapeDtypeStruct((8, rows), F32),
                   jax.ShapeDtypeStruct((ne, 1), jnp.int32)],
        scratch_shapes=[pltpu.VMEM((ne, 1), F32)],
        compiler_params=_params("arbitrary"),
        name="router",
    )(logits_t, router_b.reshape(ne, 1).astype(F32), tri)


MOE_BLOCK = 512
MOE_TOKENS = 256
HALF_D = D_MODEL // 2


def _pack_rows(lo, hi):
    lo_b = pltpu.bitcast(lo.astype(jnp.bfloat16).astype(F32), jnp.uint32)
    hi_b = pltpu.bitcast(hi.astype(jnp.bfloat16).astype(F32), jnp.uint32)
    return (hi_b & jnp.uint32(0xFFFF0000)) | (lo_b >> 16)


def _unpack_rows(u):
    lo = pltpu.bitcast(u << 16, F32)
    hi = pltpu.bitcast(u & jnp.uint32(0xFFFF0000), F32)
    return lo, hi


def _swiglu(x_bf16, wgu, wd):
    hgu = _dot(x_bf16, wgu)
    g = hgu[:, 0:EXPERT_HIDDEN]
    a = g * jax.nn.sigmoid(g) * hgu[:, EXPERT_HIDDEN:2 * EXPERT_HIDDEN]
    return _dot(a.astype(BF16), wd)


def _row_copy(src_ref, src_row, dst_ref, dst_row, sem):
    return pltpu.make_async_copy(src_ref.at[pl.ds(src_row, 1)], dst_ref.at[pl.ds(dst_row, 1)], sem)


def _dispatch_kernel(slot_ref, h_ref, xs_zero_ref, xs_ref, sem, *, tt):
    del xs_zero_ref

    def issue(t, carry):
        for k in range(TOP_K):
            _row_copy(h_ref, t, xs_ref, slot_ref[0, 0, k * tt + t], sem).start()
        return carry

    def drain(t, carry):
        for k in range(TOP_K):
            _row_copy(h_ref, 0, xs_ref, 0, sem).wait()
        return carry

    lax.fori_loop(0, tt, issue, 0, unroll=4)
    lax.fori_loop(0, tt, drain, 0)


def _experts_kernel(be_ref, nb_ref, xs_ref, wg_ref, wu_ref, wd_ref, ys_ref, wgu_bf, wd_bf):
    b = pl.program_id(0)

    @pl.when(b < nb_ref[0])
    def _():
        @pl.when((b == 0) | (be_ref[b] != be_ref[jnp.maximum(b - 1, 0)]))
        def _():
            wgu_bf[:, 0:EXPERT_HIDDEN] = wg_ref[0].astype(BF16)
            wgu_bf[:, EXPERT_HIDDEN:2 * EXPERT_HIDDEN] = wu_ref[0].astype(BF16)
            wd_bf[...] = wd_ref[0].astype(BF16)

        lo, hi = _unpack_rows(xs_ref[...])
        x = jnp.concatenate([lo, hi], axis=1).astype(BF16)
        y = _swiglu(x, wgu_bf[...], wd_bf[...])
        ys_ref[...] = _pack_rows(y[:, 0:HALF_D], y[:, HALF_D:D_MODEL])


SC_CORES = 2
SC_SUBCORES = 16
SC_GATHER_ROWS = 128


def _sc_gather_rows(table, indices):
    m, n = indices.shape[0], table.shape[1]
    nworker = SC_CORES * SC_SUBCORES
    chunk = SC_GATHER_ROWS
    assert m % (nworker * chunk) == 0
    per_worker = m // nworker
    mesh = plsc.VectorSubcoreMesh(core_axis_name="c", subcore_axis_name="s")

    @functools.partial(
        pl.kernel, mesh=mesh,
        out_type=jax.ShapeDtypeStruct((m, n), jnp.int32),
        scratch_types=[pltpu.VMEM((chunk,), jnp.int32),
                       pltpu.VMEM((chunk, n), jnp.int32),
                       pltpu.SemaphoreType.DMA],
        name="moe_gather_sc",
    )
    def gather(table_hbm, idx_hbm, out_hbm, idx_v, rows_v, sem):
        base = (lax.axis_index("s") * SC_CORES + lax.axis_index("c")) * per_worker

        @pl.loop(0, per_worker // chunk)
        def _(j):
            off = base + j * chunk
            pltpu.sync_copy(idx_hbm.at[pl.ds(off, chunk)], idx_v)
            pltpu.async_copy(table_hbm.at[idx_v], rows_v, sem).wait()
            pltpu.sync_copy(rows_v, out_hbm.at[pl.ds(off, chunk)])

    return gather(table, indices)


def _combine_kernel(w_ref, rows_ref, h_ref, wsgu_ref, wsd_ref, x1_ref, g2_ref, fg_ref, o_ref, *, final):
    shared = _swiglu(h_ref[...], wsgu_ref[...], wsd_ref[...])
    acc_lo = shared[:, 0:HALF_D]
    acc_hi = shared[:, HALF_D:D_MODEL]
    w = w_ref[...]
    for k in range(TOP_K):
        lo, hi = _unpack_rows(pltpu.bitcast(rows_ref[k], jnp.uint32))
        acc_lo = acc_lo + w[:, k:k + 1] * lo
        acc_hi = acc_hi + w[:, k:k + 1] * hi
    x2 = x1_ref[...] + g2_ref[0] * jnp.concatenate([acc_lo, acc_hi], axis=1)
    if final:
        x2 = _rms(x2) * fg_ref[...]
    o_ref[...] = x2


def _moe(h2, h2p, picks, lp, x1, g2, final_g, *, rows_per_mod, final):
    idx, rank, wsel, counts = picks
    rows = h2.shape[0]
    tt = MOE_TOKENS
    blk = MOE_BLOCK
    assert rows % tt == 0 and rows_per_mod % tt == 0 and (rows * TOP_K) % blk == 0
    ntile = rows // tt
    nblock = rows * TOP_K // blk + N_EXPERTS

    cnt = counts.reshape(N_EXPERTS)
    padded = (cnt + blk - 1) // blk * blk
    pends = jnp.cumsum(padded)
    pstart = (pends - padded).astype(jnp.int32)
    nb_used = (pends[-1] // blk).astype(jnp.int32).reshape(1)
    first_row = jnp.arange(nblock, dtype=jnp.int32) * blk
    block_e = jnp.minimum(jnp.sum(pends[None, :] <= first_row[:, None], axis=1), N_EXPERTS - 1).astype(jnp.int32)
    slot = rank + jnp.sum(jnp.where(idx[:, :, None] == jnp.arange(N_EXPERTS, dtype=jnp.int32), pstart, 0), axis=-1)
    slot_t = slot.reshape(8, ntile, tt).transpose(1, 0, 2).reshape(ntile, 1, 8 * tt)

    xs = pl.pallas_call(
        functools.partial(_dispatch_kernel, tt=tt),
        grid=(ntile,),
        in_specs=[pl.BlockSpec((1, 1, 8 * tt), lambda i: (i, 0, 0), memory_space=pltpu.SMEM),
                  pl.BlockSpec((tt, HALF_D), lambda i: (i, 0)),
                  pl.BlockSpec(memory_space=pl.ANY)],
        out_specs=pl.BlockSpec(memory_space=pl.ANY),
        scratch_shapes=[pltpu.SemaphoreType.DMA(())],
        out_shape=jax.ShapeDtypeStruct((nblock * blk, HALF_D), jnp.uint32),
        input_output_aliases={2: 0},
        compiler_params=_params("arbitrary"),
        name="moe_dispatch",
    )(slot_t, h2p, jnp.zeros((nblock * blk, HALF_D), jnp.uint32))

    def blk_map(b, be, nb):
        return (jnp.minimum(b, nb[0] - 1), 0)

    def w_map(b, be, nb):
        return (be[jnp.minimum(b, nb[0] - 1)], 0, 0)

    ys = pl.pallas_call(
        _experts_kernel,
        grid_spec=pltpu.PrefetchScalarGridSpec(
            num_scalar_prefetch=2,
            grid=(nblock,),
            in_specs=[pl.BlockSpec((blk, HALF_D), blk_map),
                      pl.BlockSpec((1, D_MODEL, EXPERT_HIDDEN), w_map),
                      pl.BlockSpec((1, D_MODEL, EXPERT_HIDDEN), w_map),
                      pl.BlockSpec((1, EXPERT_HIDDEN, D_MODEL), w_map)],
            out_specs=pl.BlockSpec((blk, HALF_D), blk_map),
            scratch_shapes=[pltpu.VMEM((D_MODEL, 2 * EXPERT_HIDDEN), BF16),
                            pltpu.VMEM((EXPERT_HIDDEN, D_MODEL), BF16)]),
        out_shape=jax.ShapeDtypeStruct((nblock * blk, HALF_D), jnp.uint32),
        compiler_params=_params("arbitrary"),
        name="moe_experts",
    )(block_e, nb_used, xs, lp['exp_w_gate'].astype(F32), lp['exp_w_up'].astype(F32), lp['exp_w_down'].astype(F32))

    wsgu = jnp.concatenate([lp['sh_w_gate'], lp['sh_w_up']], axis=1).astype(BF16)
    wsd = lp['sh_w_down'].astype(BF16)
    gathered = _sc_gather_rows(lax.bitcast_convert_type(ys, jnp.int32), slot[0:TOP_K].reshape(TOP_K * rows))
    gathered = gathered.reshape(TOP_K, rows, HALF_D)
    full = lambda *shape: pl.BlockSpec(shape, lambda i: (0,) * len(shape))
    row = lambda width: pl.BlockSpec((tt, width), lambda i: (i, 0))
    return pl.pallas_call(
        functools.partial(_combine_kernel, final=final),
        grid=(ntile,),
        in_specs=[row(8), pl.BlockSpec((TOP_K, tt, HALF_D), lambda i: (0, i, 0)), row(D_MODEL),
                  full(D_MODEL, 2 * EXPERT_HIDDEN), full(EXPERT_HIDDEN, D_MODEL), row(D_MODEL),
                  pl.BlockSpec((1, 1, D_MODEL), lambda i: ((i * tt) // rows_per_mod, 0, 0)),
                  full(1, D_MODEL)],
        out_specs=row(D_MODEL),
        out_shape=jax.ShapeDtypeStruct((rows, D_MODEL), F32),
        compiler_params=_params("arbitrary"),
        name="moe_combine",
    )(wsel.T, gathered, h2, wsgu, wsd, x1, g2, final_g.reshape(1, D_MODEL).astype(F32))


def _reorder_w_in(w_in):
    split = 256 + 768 + 512 + 768
    return jnp.concatenate([w_in[:, split:], w_in[:, :split]], axis=1).astype(BF16)


def _mods(mod_row_block):
    return [mod_row_block[:, None, k * D_MODEL:(k + 1) * D_MODEL] for k in range(6)]


def _moe_block(h2, h2p, logits, lp, x1, g2, final_g, *, rows_per_mod, final):
    picks = _router(logits.T, lp['router_b'])
    return _moe(h2, h2p, picks, lp, x1, g2, final_g, rows_per_mod=rows_per_mod, final=final)


def _layer(x2d, xc2d, c16, lp, layer_idx, tables, final_g, *, nbatch, seq, n_ctx, with_ctx_out, final):
    lambda_init = 0.8 - 0.6 * math.exp(-0.3 * layer_idx)
    mod = _ada_mod(c16, lp['ada_w'].astype(F32), lp['ada_b'].astype(F32))
    sh1, sc1, g1, sh2, sc2, g2 = _mods(mod[0:nbatch])
    csh1, csc1, cg1, csh2, csc2, cg2 = _mods(mod[nbatch:nbatch + 1])
    w_in = _reorder_w_in(lp['w_in'])
    rows_lat = nbatch * seq
    rows_ctx = nbatch * n_ctx

    gate, u, na, sw, df = _inproj(x2d, lp['norm1_g'], sc1, sh1, w_in, tables,
                                  rows_per_mod=seq, rope=True, seq=seq)
    gate_c, u_c, na_c, sw_c, df_c = _inproj(xc2d, lp['norm1_g'], csc1, csh1, w_in, tables,
                                            rows_per_mod=rows_ctx, rope=False, seq=seq)

    win, wout, a_re, a_im = _s5_params(lp['s5_lambda_re'], lp['s5_lambda_im'], lp['s5_log_step'],
                                       lp['s5_b_re'], lp['s5_b_im'], lp['s5_c_re'], lp['s5_c_im'])
    u_tm = jnp.concatenate([u_c.reshape(nbatch, n_ctx, 256).transpose(1, 0, 2),
                            u.reshape(nbatch, seq, 256).transpose(1, 0, 2)], axis=0)
    y_tm = _s5_scan(u_tm, win, wout, a_re, a_im, n_ctx)
    y_bm = y_tm.transpose(0, 2, 1, 3)
    yf = y_bm[0, :, n_ctx:].reshape(rows_lat, 256)
    yr = y_bm[1, :, n_ctx:].reshape(rows_lat, 256)

    lqk = [lp[k].reshape(1, DIFF_QK_DIM).astype(F32) for k in ('diff_lq1', 'diff_lk1', 'diff_lq2', 'diff_lk2')]
    bias = _na_bias_table(lp['na_rpb'], seq // GRID_W)
    yb = _na_attention(na, na_c, bias, nbatch, seq, n_ctx)
    yc = _swa_attention(sw, sw_c, lp['swa_sink'], nbatch, seq, n_ctx)
    yd = _diff_attention(df, df_c, lqk, lp['diff_subln_g'], lambda_init, nbatch, seq, n_ctx)

    merge_w = (lp['s5_d'], lp['s5_glu_w'], lp['s5_glu_b'])
    x1, h2, h2p, logits = _merge(u, yf, yr, *merge_w, yb, yc, yd, gate, lp['w_branch'], lp['w_out'], x2d, g1,
                                 lp['norm2_g'], sc2, sh2, lp['router_w'], rows_per_mod=seq)
    x_out = _moe_block(h2, h2p, logits, lp, x1, g2, final_g, rows_per_mod=seq, final=final)

    xc_out = None
    if with_ctx_out:
        yf_c = y_bm[0, :, :n_ctx].reshape(rows_ctx, 256)
        yr_c = y_bm[1, :, :n_ctx].reshape(rows_ctx, 256)
        yb_c, yc_c, yd_c = _ctx_attention(na_c, sw_c, df_c, lp['swa_sink'], lqk, lp['diff_subln_g'],
                                          lambda_init, nbatch, n_ctx)
        x1c, h2c, h2pc, logits_c = _merge(u_c, yf_c, yr_c, *merge_w, yb_c, yc_c, yd_c, gate_c, lp['w_branch'],
                                          lp['w_out'], xc2d, cg1, lp['norm2_g'], csc2, csh2, lp['router_w'],
                                          rows_per_mod=rows_ctx)
        xc_out = _moe_block(h2c, h2pc, logits_c, lp, x1c, cg2, final_g, rows_per_mod=rows_ctx, final=False)
    return x_out, xc_out


def kernel(x, c, ctx, c_ctx, ada_w, ada_b, norm1_g, norm2_g, w_in, s5_lambda_re, s5_lambda_im, s5_log_step,
           s5_b_re, s5_b_im, s5_c_re, s5_c_im, s5_d, s5_glu_w, s5_glu_b, na_rpb, swa_sink, diff_lq1, diff_lk1,
           diff_lq2, diff_lk2, diff_subln_g, w_branch, w_out, router_w, router_b, exp_w_gate, exp_w_up,
           exp_w_down, sh_w_gate, sh_w_up, sh_w_down, final_g):
    nbatch, seq, d = x.shape
    n_ctx = ctx.shape[1]
    depth = ada_w.shape[0]
    assert d == D_MODEL and nbatch == 8
    stacked = dict(ada_w=ada_w, ada_b=ada_b, norm1_g=norm1_g, norm2_g=norm2_g, w_in=w_in,
                   s5_lambda_re=s5_lambda_re, s5_lambda_im=s5_lambda_im, s5_log_step=s5_log_step,
                   s5_b_re=s5_b_re, s5_b_im=s5_b_im, s5_c_re=s5_c_re, s5_c_im=s5_c_im, s5_d=s5_d,
                   s5_glu_w=s5_glu_w, s5_glu_b=s5_glu_b, na_rpb=na_rpb, swa_sink=swa_sink,
                   diff_lq1=diff_lq1, diff_lk1=diff_lk1, diff_lq2=diff_lq2, diff_lk2=diff_lk2,
                   diff_subln_g=diff_subln_g, w_branch=w_branch, w_out=w_out, router_w=router_w,
                   router_b=router_b, exp_w_gate=exp_w_gate, exp_w_up=exp_w_up, exp_w_down=exp_w_down,
                   sh_w_gate=sh_w_gate, sh_w_up=sh_w_up, sh_w_down=sh_w_down)
    tables = _rope_tables(seq)
    c16 = jnp.concatenate([c.astype(F32), c_ctx.reshape(1, d).astype(F32),
                           jnp.zeros((16 - nbatch - 1, d), F32)], axis=0)
    x2d = x.reshape(nbatch * seq, d).astype(F32)
    xc2d = ctx.reshape(nbatch * n_ctx, d).astype(F32)
    for l in range(depth):
        lp = {k: v[l] for k, v in stacked.items()}
        last = l == depth - 1
        x2d, xc2d = _layer(x2d, xc2d, c16, lp, l, tables, final_g, nbatch=nbatch, seq=seq, n_ctx=n_ctx,
                           with_ctx_out=not last, final=last)
    return x2d.reshape(nbatch, seq, d)
```

```python
import functools
import math

import numpy as np
import jax
import jax.numpy as jnp
from jax import lax
from jax.experimental import pallas as pl
from jax.experimental.pallas import tpu as pltpu
from jax.experimental.pallas import tpu_sc as plsc

F32 = jnp.float32
BF16 = jnp.bfloat16
HIGHEST = lax.Precision.HIGHEST

GRID_W = 64
EPS = 1e-6
NEG_INF = -1e30
ROPE_BASE = 10000.0
D_MODEL = 1024
BRANCH_WIDTH = 256
HEAD_DIM = 64
S5_GROUP = 16
S5_GROUPS = 16
S5_STATE = 64
S5_FLAT = S5_GROUPS * S5_STATE
NA_HEADS = 4
NA_WIN_ROWS = 8
NA_WIN_COLS = 16
SWA_KV_HEADS = 2
SWA_WINDOW = 128
DIFF_HEADS = 4
DIFF_QK_DIM = 32
N_EXPERTS = 64
N_EXPERT_GROUPS = 8
TOPK_GROUPS = 4
TOP_K = 6
EXPERT_HIDDEN = 256
ROUTED_SCALE = 2.5
GATE_WIDTH = 4 * D_MODEL

VMEM_LIMIT = 56 * 1024 * 1024


def _params(*sem):
    return pltpu.CompilerParams(dimension_semantics=sem, vmem_limit_bytes=VMEM_LIMIT)


def _nt_dot(a, b):
    return lax.dot_general(a, b, (((1,), (1,)), ((), ())), preferred_element_type=F32)


def _dot(a, b):
    return jnp.dot(a, b, preferred_element_type=F32)


def _rms(x):
    return x * lax.rsqrt(jnp.mean(x * x, axis=-1, keepdims=True) + EPS)


def _ada_kernel(c_ref, w_ref, b_ref, o_ref):
    c = c_ref[...]
    s = c * jax.nn.sigmoid(c)
    o_ref[...] = jnp.dot(s, w_ref[...], preferred_element_type=F32, precision=HIGHEST) + b_ref[...]


def _ada_mod(cc, w, b):
    rows, d = cc.shape
    width = w.shape[1]
    tn = 1536
    return pl.pallas_call(
        _ada_kernel,
        grid=(width // tn,),
        in_specs=[pl.BlockSpec((rows, d), lambda j: (0, 0)),
                  pl.BlockSpec((d, tn), lambda j: (0, j)),
                  pl.BlockSpec((1, tn), lambda j: (0, j))],
        out_specs=pl.BlockSpec((rows, tn), lambda j: (0, j)),
        out_shape=jax.ShapeDtypeStruct((rows, width), F32),
        compiler_params=_params("arbitrary"),
        name="ada_mod",
    )(cc, w, b.reshape(1, width))


_C_GATE = 0
_C_U = GATE_WIDTH
_C_NA = _C_U + 256
_C_SW = _C_NA + 768
_C_DF = _C_SW + 512
_C_END = _C_DF + 768


def _rope_apply(x, cos, sins, half):
    outs = []
    for j in range(x.shape[1] // 128):
        xs = x[:, j * 128:(j + 1) * 128]
        lane = lax.broadcasted_iota(jnp.int32, xs.shape, 1)
        lo = (lane % (2 * half)) < half
        partner = jnp.where(lo, pltpu.roll(xs, 128 - half, 1), pltpu.roll(xs, half, 1))
        outs.append(xs * cos + partner * sins)
    return outs[0] if len(outs) == 1 else jnp.concatenate(outs, axis=1)


def _inproj_kernel(x_ref, g_ref, sc_ref, sh_ref, w_ref, c64_ref, s64_ref, c32_ref, s32_ref,
                   gate_o, u_o, na_o, sw_o, df_o, *, rope):
    h = _rms(x_ref[...]) * g_ref[...]
    h = h * (1.0 + sc_ref[0]) + sh_ref[0]
    hb = h.astype(BF16)

    def mm(c0, c1):
        return _dot(hb, w_ref[:, c0:c1])

    for k in range(GATE_WIDTH // 512):
        gate_o[:, k * 512:(k + 1) * 512] = jax.nn.sigmoid(mm(k * 512, (k + 1) * 512)).astype(BF16)
    u_o[...] = mm(_C_U, _C_U + 256)

    na = mm(_C_NA, _C_NA + 768)
    na_o[:, 0:256] = (na[:, 0:256] * (HEAD_DIM ** -0.5 * (LOG2E if rope else 1.0))).astype(BF16)
    na_o[:, 256:768] = na[:, 256:768].astype(BF16)

    sw = mm(_C_SW, _C_SW + 512)
    swq, swk = sw[:, 0:256], sw[:, 256:384]
    if rope:
        swq = _rope_apply(swq, c64_ref[...], s64_ref[...], 16)
        swk = _rope_apply(swk, c64_ref[...], s64_ref[...], 16)
    sw_o[:, 0:256] = (swq * (HEAD_DIM ** -0.5 * (LOG2E if rope else 1.0))).astype(BF16)
    sw_o[:, 256:384] = swk.astype(BF16)
    sw_o[:, 384:512] = sw[:, 384:512].astype(BF16)

    df = mm(_C_DF, _C_DF + 768)
    dfq, dfk = df[:, 0:256], df[:, 256:512]
    if rope:
        dfq = _rope_apply(dfq, c32_ref[...], s32_ref[...], 8)
        dfk = _rope_apply(dfk, c32_ref[...], s32_ref[...], 8)
    df_o[:, 0:256] = (dfq * (DIFF_QK_DIM ** -0.5 * (LOG2E if rope else 1.0))).astype(BF16)
    df_o[:, 256:512] = dfk.astype(BF16)
    df_o[:, 512:768] = df[:, 512:768].astype(BF16)


def _inproj(x2d, norm_g, sc, sh, w_bf16, tables, *, rows_per_mod, rope, seq):
    rows = x2d.shape[0]
    tm = 512
    assert rows % tm == 0 and rows_per_mod % tm == 0 and seq % tm == 0
    tiles_per_seq = seq // tm

    def mod_map(i):
        return ((i * tm) // rows_per_mod, 0, 0)

    def tab_map(i):
        return (i % tiles_per_seq, 0)

    tab_spec = pl.BlockSpec((tm, 128), tab_map)
    row = lambda w: pl.BlockSpec((tm, w), lambda i: (i, 0))
    return pl.pallas_call(
        functools.partial(_inproj_kernel, rope=rope),
        grid=(rows // tm,),
        in_specs=[row(D_MODEL),
                  pl.BlockSpec((1, D_MODEL), lambda i: (0, 0)),
                  pl.BlockSpec((1, 1, D_MODEL), mod_map),
                  pl.BlockSpec((1, 1, D_MODEL), mod_map),
                  pl.BlockSpec((D_MODEL, _C_END), lambda i: (0, 0)),
                  tab_spec, tab_spec, tab_spec, tab_spec],
        out_specs=[row(GATE_WIDTH), row(256), row(768), row(512), row(768)],
        out_shape=[jax.ShapeDtypeStruct((rows, GATE_WIDTH), BF16),
                   jax.ShapeDtypeStruct((rows, 256), F32),
                   jax.ShapeDtypeStruct((rows, 768), BF16),
                   jax.ShapeDtypeStruct((rows, 512), BF16),
                   jax.ShapeDtypeStruct((rows, 768), BF16)],
        compiler_params=_params("arbitrary"),
        name="inproj",
    )(x2d, norm_g.reshape(1, D_MODEL), sc, sh, w_bf16, *tables)


def _rope_tables(seq):
    t = jnp.arange(seq)
    rows = (t // GRID_W).astype(F32)
    cols = (t % GRID_W).astype(F32)
    lane = np.arange(128)
    out = []
    for dim in (64, 32):
        quarter = dim // 4
        inv_freq = ROPE_BASE ** (-jnp.arange(quarter, dtype=F32) / quarter)
        l = lane % dim
        use_col = l >= dim // 2
        fidx = l % quarter
        hi = (l % (dim // 2)) >= quarter
        ang_r = rows[:, None] * inv_freq[None, :]
        ang_c = cols[:, None] * inv_freq[None, :]
        ang = jnp.where(use_col[None, :], ang_c[:, fidx], ang_r[:, fidx])
        out.append(jnp.cos(ang))
        out.append(jnp.where(hi[None, :], jnp.sin(ang), -jnp.sin(ang)))
    return tuple(out)


S5_CHUNK = 128


def _s5_kernel(u_ref, win_ref, wout_ref, are_ref, aim_ref, y_ref, bu_ref, st_ref, *, tc, nb):
    d = pl.program_id(0)
    i = pl.program_id(1)

    @pl.when(i == 0)
    def _():
        st_ref[...] = jnp.zeros_like(st_ref)

    u = u_ref[...].reshape(tc * nb, BRANCH_WIDTH).astype(BF16)
    bu_ref[...] = _dot(u, win_ref[0])
    ar = jnp.broadcast_to(are_ref[0], (nb, S5_FLAT))
    ai = jnp.broadcast_to(aim_ref[0], (nb, S5_FLAT))

    def body(j, carry):
        xr, xi = carry
        t = j + d * (tc - 1 - 2 * j)
        row = pl.multiple_of(t * nb, nb)
        br = bu_ref[pl.ds(row, nb), 0:S5_FLAT]
        bi = bu_ref[pl.ds(row, nb), S5_FLAT:2 * S5_FLAT]
        nr = ar * xr - ai * xi + br
        ni = ar * xi + ai * xr + bi
        bu_ref[pl.ds(row, nb), 0:S5_FLAT] = nr
        bu_ref[pl.ds(row, nb), S5_FLAT:2 * S5_FLAT] = ni
        return nr, ni

    xr, xi = lax.fori_loop(0, tc, body, (st_ref[:, 0:S5_FLAT], st_ref[:, S5_FLAT:2 * S5_FLAT]), unroll=4)
    st_ref[:, 0:S5_FLAT] = xr
    st_ref[:, S5_FLAT:2 * S5_FLAT] = xi
    y = _dot(bu_ref[...].astype(BF16), wout_ref[0])
    y_ref[0] = y.reshape(tc, nb, BRANCH_WIDTH)


def _s5_scan(u_tm, win, wout, a_re, a_im, n_ctx):
    s_len, nb, _ = u_tm.shape
    tc = S5_CHUNK
    assert nb == 8 and s_len % tc == 0 and n_ctx % tc == 0
    nct = n_ctx // tc
    nlt = (s_len - n_ctx) // tc

    def chunk(d, i):
        rev = jnp.where(i < nct, nct - 1 - i, 2 * nct + nlt - 1 - i)
        return jnp.where(d == 0, i, rev)

    return pl.pallas_call(
        functools.partial(_s5_kernel, tc=tc, nb=nb),
        grid=(2, nct + nlt),
        in_specs=[pl.BlockSpec((tc, nb, BRANCH_WIDTH), lambda d, i: (chunk(d, i), 0, 0)),
                  pl.BlockSpec((1, BRANCH_WIDTH, 2 * S5_FLAT), lambda d, i: (d, 0, 0)),
                  pl.BlockSpec((1, 2 * S5_FLAT, BRANCH_WIDTH), lambda d, i: (d, 0, 0)),
                  pl.BlockSpec((1, 1, S5_FLAT), lambda d, i: (d, 0, 0)),
                  pl.BlockSpec((1, 1, S5_FLAT), lambda d, i: (d, 0, 0))],
        out_specs=pl.BlockSpec((1, tc, nb, BRANCH_WIDTH), lambda d, i: (d, chunk(d, i), 0, 0)),
        out_shape=jax.ShapeDtypeStruct((2, s_len, nb, BRANCH_WIDTH), F32),
        scratch_shapes=[pltpu.VMEM((tc * nb, 2 * S5_FLAT), F32),
                        pltpu.VMEM((nb, 2 * S5_FLAT), F32)],
        compiler_params=_params("arbitrary", "arbitrary"),
        name="s5_scan",
    )(u_tm, win, wout, a_re, a_im)


def _s5_params(lam_re, lam_im, log_step, b_re, b_im, c_re, c_im):
    lr = lam_re.astype(F32)
    li = lam_im.astype(F32)
    dt = jnp.exp(log_step.astype(F32))[..., None]
    mag = jnp.exp(lr * dt)
    a_re = mag * jnp.cos(li * dt)
    a_im = mag * jnp.sin(li * dt)
    nr, ni, den = a_re - 1.0, a_im, lr * lr + li * li
    k_re = ((nr * lr + ni * li) / den)[..., None]
    k_im = ((ni * lr - nr * li) / den)[..., None]
    br = b_re.astype(F32)
    bi = b_im.astype(F32)
    bb_re = k_re * br - k_im * bi
    bb_im = k_re * bi + k_im * br
    eye = jnp.eye(S5_GROUPS, dtype=F32)

    def blockdiag_in(bb):
        m = jnp.einsum('dgpc,gh->dgchp', bb, eye)
        return m.reshape(2, S5_GROUPS * S5_GROUP, S5_GROUPS * S5_STATE)

    def blockdiag_out(cc):
        m = jnp.einsum('dgcp,gh->dgphc', cc, eye)
        return m.reshape(2, S5_GROUPS * S5_STATE, S5_GROUPS * S5_GROUP)

    win = jnp.concatenate([blockdiag_in(bb_re), blockdiag_in(bb_im)], axis=2).astype(BF16)
    wout = jnp.concatenate([blockdiag_out(c_re.astype(F32)), -blockdiag_out(c_im.astype(F32))], axis=1).astype(BF16)
    return win, wout, a_re.reshape(2, 1, S5_FLAT), a_im.reshape(2, 1, S5_FLAT)


def _softmax_parts(scores, extra=None):
    m = scores[0].max(axis=-1, keepdims=True)
    for s in scores[1:]:
        m = jnp.maximum(m, s.max(axis=-1, keepdims=True))
    if extra is not None:
        m = jnp.maximum(m, extra)
    ps = [jnp.exp(s - m) for s in scores]
    l = ps[0].sum(axis=-1, keepdims=True)
    for p in ps[1:]:
        l = l + p.sum(axis=-1, keepdims=True)
    if extra is not None:
        l = l + jnp.exp(extra - m)
    return ps, l


NA_QROWS = 2
NA_KROWS = 10
VROWS = HEAD_DIM + 16


def _na_window_start(r, grid_rows):
    start = jnp.clip(r - NA_WIN_ROWS // 2, 0, grid_rows - NA_WIN_ROWS)
    return (jnp.minimum(start, grid_rows - NA_KROWS) // 2) * 2


def _head_blockdiag(qt, qbd_ref, nheads, rows_per_head):
    n = qt.shape[1]
    row_h = lax.broadcasted_iota(jnp.int32, qt.shape, 0) // rows_per_head
    zero = jnp.zeros_like(qt)
    for h in range(nheads):
        qbd_ref[:, h * n:(h + 1) * n] = jnp.where(row_h == h, qt, zero)


def _na_kernel(qt_ref, k_ref, vt_ref, kc_ref, vct_ref, bias_ref, o_ref, qbd_ref, *, grid_rows):
    nq = NA_QROWS * GRID_W
    nk = NA_KROWS * GRID_W
    off = pl.multiple_of(_na_window_start(NA_QROWS * pl.program_id(1), grid_rows) * GRID_W, 128)
    _head_blockdiag(qt_ref[0], qbd_ref, NA_HEADS, HEAD_DIM)
    qbd = qbd_ref[...]
    s_loc = _dot(k_ref[0, pl.ds(off, nk), :], qbd) + bias_ref[0]
    s_ctx = _dot(kc_ref[0], qbd)
    m = jnp.maximum(s_loc.max(axis=0, keepdims=True), s_ctx.max(axis=0, keepdims=True))
    p_loc = jnp.exp2((s_loc - m).astype(BF16))
    p_ctx = jnp.exp2((s_ctx - m).astype(BF16))
    vw = vt_ref[0, :, pl.ds(off, nk)]
    outs = []
    for h in range(NA_HEADS):
        rows = slice(h * VROWS, (h + 1) * VROWS)
        cols = slice(h * nq, (h + 1) * nq)
        o = _dot(vw[rows], p_loc[:, cols]) + _dot(vct_ref[0, rows, :], p_ctx[:, cols])
        outs.append(o[0:HEAD_DIM] / o[HEAD_DIM:HEAD_DIM + 1])
    o_ref[...] = jnp.concatenate(outs, axis=0).T.astype(BF16)


def _na_classes(grid_rows):
    return [0, 2, 4, grid_rows - 4, grid_rows - 2]


def _na_bias_table(rpb, grid_rows):
    col = np.arange(GRID_W)
    cstart = np.clip(col - NA_WIN_COLS // 2, 0, GRID_W - NA_WIN_COLS)
    col_in = (col[None, :] >= cstart[:, None]) & (col[None, :] < cstart[:, None] + NA_WIN_COLS)
    cb = np.clip(col[None, :] - col[:, None] + (NA_WIN_COLS - 1), 0, 2 * NA_WIN_COLS - 2)
    classes = _na_classes(grid_rows)
    rbi = np.zeros((len(classes), NA_QROWS, NA_KROWS), np.int64)
    row_in = np.zeros((len(classes), NA_QROWS, NA_KROWS), bool)
    for c, r in enumerate(classes):
        a_row = (min(int(np.clip(r - NA_WIN_ROWS // 2, 0, grid_rows - NA_WIN_ROWS)), grid_rows - NA_KROWS) // 2) * 2
        for qi in range(NA_QROWS):
            start_q = int(np.clip(r + qi - NA_WIN_ROWS // 2, 0, grid_rows - NA_WIN_ROWS))
            for j in range(NA_KROWS):
                row_in[c, qi, j] = start_q <= a_row + j < start_q + NA_WIN_ROWS
                rbi[c, qi, j] = np.clip(a_row + j - (r + qi) + NA_WIN_ROWS - 1, 0, 2 * NA_WIN_ROWS - 2)
    oh_row = jnp.asarray(rbi[..., None] == np.arange(2 * NA_WIN_ROWS - 1), F32)
    oh_col = jnp.asarray(cb[:, :, None] == np.arange(2 * NA_WIN_COLS - 1), F32)
    t = jnp.einsum('hab,cija,qkb->cjkhiq', rpb.astype(F32), oh_row, oh_col, precision=HIGHEST) * LOG2E
    valid = row_in.transpose(0, 2, 1)[:, :, None, None, :, None] & col_in.T[None, None, :, None, None, :]
    t = jnp.where(valid, t, NEG_INF)
    return t.reshape(len(classes), NA_KROWS * GRID_W, NA_HEADS * NA_QROWS * GRID_W)


def _augment_vt(v, nheads):
    nb, s, _ = v.shape
    v = v.reshape(nb, s, nheads, HEAD_DIM)
    pad = jnp.zeros((nb, s, nheads, VROWS - HEAD_DIM), v.dtype).at[..., 0].set(1.0)
    return jnp.concatenate([v, pad], axis=-1).reshape(nb, s, nheads * VROWS).transpose(0, 2, 1)


def _na_attention(na, na_c, bias, nbatch, seq, n_ctx):
    grid_rows = seq // GRID_W
    assert grid_rows >= NA_KROWS and grid_rows % NA_QROWS == 0
    nq = NA_QROWS * GRID_W
    npair = grid_rows // NA_QROWS
    nal = na.reshape(nbatch, seq, 768)
    nac = na_c.reshape(nbatch, n_ctx, 768)
    qt = nal[:, :, 0:256].transpose(0, 2, 1)
    vt = _augment_vt(nal[:, :, 512:768], NA_HEADS)
    vct = _augment_vt(nac[:, :, 512:768], NA_HEADS)

    def cls(b, p):
        r = NA_QROWS * p
        c = jnp.where(r < 4, r // 2, jnp.where(r >= grid_rows - 4, (r - (grid_rows - 4)) // 2 + 3, 2))
        return (c, 0, 0)

    return pl.pallas_call(
        functools.partial(_na_kernel, grid_rows=grid_rows),
        grid=(nbatch, npair),
        in_specs=[pl.BlockSpec((1, 256, nq), lambda b, p: (b, 0, p)),
                  pl.BlockSpec((1, seq, 256), lambda b, p: (b, 0, 1)),
                  pl.BlockSpec((1, NA_HEADS * VROWS, seq), lambda b, p: (b, 0, 0)),
                  pl.BlockSpec((1, n_ctx, 256), lambda b, p: (b, 0, 1)),
                  pl.BlockSpec((1, NA_HEADS * VROWS, n_ctx), lambda b, p: (b, 0, 0)),
                  pl.BlockSpec((1, NA_KROWS * GRID_W, NA_HEADS * nq), cls)],
        out_specs=pl.BlockSpec((nq, 256), lambda b, p: (b * npair + p, 0)),
        out_shape=jax.ShapeDtypeStruct((nbatch * seq, 256), BF16),
        scratch_shapes=[pltpu.VMEM((256, NA_HEADS * nq), BF16)],
        compiler_params=_params("arbitrary", "arbitrary"),
        name="na_attention",
    )(qt, nal, vt, nac, vct, bias)


SWA_BLOCK = 128


def _swa_kernel(qt_ref, k_ref, vt_ref, kc_ref, vct_ref, sink_ref, o_ref, qbd_ref, *, seq):
    n = pl.program_id(1)
    band = 3 * SWA_BLOCK
    nq = SWA_BLOCK
    nqh = 2 * SWA_KV_HEADS
    bstart = pl.multiple_of(jnp.clip((n - 1) * SWA_BLOCK, 0, seq - band), SWA_BLOCK)
    qt = qt_ref[0]
    zero = jnp.zeros((HEAD_DIM, nq), qt.dtype)
    for hq in range(nqh):
        qh = qt[hq * HEAD_DIM:(hq + 1) * HEAD_DIM]
        qbd_ref[:, hq * nq:(hq + 1) * nq] = jnp.concatenate([qh, zero] if hq // 2 == 0 else [zero, qh], axis=0)
    qbd = qbd_ref[...]
    kpos = bstart + lax.broadcasted_iota(jnp.int32, (band, nqh * nq), 0)
    qpos = n * SWA_BLOCK + lax.broadcasted_iota(jnp.int32, (band, nqh * nq), 1) % nq
    s_loc = jnp.where(jnp.abs(qpos - kpos) <= SWA_WINDOW, _dot(k_ref[0, pl.ds(bstart, band), :], qbd), NEG_INF)
    s_ctx = _dot(kc_ref[0], qbd)
    sink = jnp.concatenate([jnp.broadcast_to(sink_ref[0:1, hq:hq + 1] * LOG2E, (1, nq)) for hq in range(nqh)], axis=1)
    m = jnp.maximum(jnp.maximum(s_loc.max(axis=0, keepdims=True), s_ctx.max(axis=0, keepdims=True)), sink)
    p_loc = jnp.exp2((s_loc - m).astype(BF16))
    p_ctx = jnp.exp2((s_ctx - m).astype(BF16))
    p_sink = jnp.exp2(sink - m)
    vw = vt_ref[0, :, pl.ds(bstart, band)]
    outs = []
    for hq in range(nqh):
        rows = slice((hq // 2) * VROWS, (hq // 2 + 1) * VROWS)
        cols = slice(hq * nq, (hq + 1) * nq)
        o = _dot(vw[rows], p_loc[:, cols]) + _dot(vct_ref[0, rows, :], p_ctx[:, cols])
        outs.append(o[0:HEAD_DIM] / (o[HEAD_DIM:HEAD_DIM + 1] + p_sink[:, cols]))
    o_ref[...] = jnp.concatenate(outs, axis=0).T.astype(BF16)


def _swa_attention(sw, sw_c, sink, nbatch, seq, n_ctx):
    nblk = seq // SWA_BLOCK
    nqh = 2 * SWA_KV_HEADS
    assert seq >= 3 * SWA_BLOCK
    sink_pad = jnp.zeros((1, 128), F32).at[0, 0:nqh].set(sink.astype(F32))
    swl = sw.reshape(nbatch, seq, 512)
    swc = sw_c.reshape(nbatch, n_ctx, 512)
    qt = swl[:, :, 0:256].transpose(0, 2, 1)
    vt = _augment_vt(swl[:, :, 384:512], SWA_KV_HEADS)
    vct = _augment_vt(swc[:, :, 384:512], SWA_KV_HEADS)
    return pl.pallas_call(
        functools.partial(_swa_kernel, seq=seq),
        grid=(nbatch, nblk),
        in_specs=[pl.BlockSpec((1, 256, SWA_BLOCK), lambda b, n: (b, 0, n)),
                  pl.BlockSpec((1, seq, 128), lambda b, n: (b, 0, 2)),
                  pl.BlockSpec((1, SWA_KV_HEADS * VROWS, seq), lambda b, n: (b, 0, 0)),
                  pl.BlockSpec((1, n_ctx, 128), lambda b, n: (b, 0, 2)),
                  pl.BlockSpec((1, SWA_KV_HEADS * VROWS, n_ctx), lambda b, n: (b, 0, 0)),
                  pl.BlockSpec((1, 128), lambda b, n: (0, 0))],
        out_specs=pl.BlockSpec((SWA_BLOCK, 256), lambda b, n: (b * nblk + n, 0)),
        out_shape=jax.ShapeDtypeStruct((nbatch * seq, 256), BF16),
        scratch_shapes=[pltpu.VMEM((SWA_KV_HEADS * HEAD_DIM, nqh * SWA_BLOCK), BF16)],
        compiler_params=_params("arbitrary", "arbitrary"),
        name="swa_attention",
    )(qt, swl, vt, swc, vct, sink_pad)


DIFF_TQ = 512
DIFF_CK = 256
LOG2E = math.log2(math.e)
DIFF_VROWS = HEAD_DIM + 16


def _diff_lambda(lq1_ref, lk1_ref, lq2_ref, lk2_ref, lambda_init):
    s1 = jnp.sum(lq1_ref[...] * lk1_ref[...], axis=-1, keepdims=True)
    s2 = jnp.sum(lq2_ref[...] * lk2_ref[...], axis=-1, keepdims=True)
    return jnp.exp(s1) - jnp.exp(s2) + lambda_init


def _stack_maps(qh):
    lane = lax.broadcasted_iota(jnp.int32, qh.shape, 1)
    zero = jnp.zeros_like(qh)
    return jnp.concatenate([jnp.where(lane < DIFF_QK_DIM, qh, zero),
                            jnp.where(lane >= DIFF_QK_DIM, qh, zero)], axis=0)


def _subln(o0, o1, lam, g, lambda_init):
    o = o0 - lam * o1
    return _rms(o) * g * (1.0 - lambda_init)


def _diff_kernel(qt_ref, k_ref, vt_ref, lq1_ref, lk1_ref, lq2_ref, lk2_ref, g_ref, o_ref, qbd_ref, acc_ref, s_ref,
                 *, nchunk, lambda_init):
    lam = _diff_lambda(lq1_ref, lk1_ref, lq2_ref, lk2_ref, lambda_init)
    qt = qt_ref[0]
    tq = qt.shape[1]
    w = 2 * tq
    row = lax.broadcasted_iota(jnp.int32, qt.shape, 0) // DIFF_QK_DIM
    zero = jnp.zeros_like(qt)
    for j in range(2 * DIFF_HEADS):
        qbd_ref[:, j * tq:(j + 1) * tq] = jnp.where(row == j, qt, zero)
    acc_ref[...] = jnp.zeros_like(acc_ref)

    def scores(slot, c, h):
        s = _dot(k_ref[0, c], qbd_ref[:, h * w:(h + 1) * w])
        s_ref[slot, h] = s
        return s.max(axis=0, keepdims=True)

    def softmax_pv(slot, c, h, m_run, m_chunk):
        m_new = jnp.maximum(m_run, m_chunk)
        alpha = jnp.exp2(m_run - m_new)
        p = jnp.exp2((s_ref[slot, h] - m_new).astype(BF16))
        acc_ref[h] = alpha * acc_ref[h] + _dot(vt_ref[0, c, h * DIFF_VROWS:(h + 1) * DIFF_VROWS, :], p)
        return m_new

    def step(slot, c, carry):
        m_run, m_chunk = carry
        new_run, new_chunk = [], []
        for h in range(DIFF_HEADS):
            new_chunk.append(scores(1 - slot, c + 1, h))
            new_run.append(softmax_pv(slot, c, h, m_run[h], m_chunk[h]))
        return tuple(new_run), tuple(new_chunk)

    def body(i, carry):
        c = 2 * i
        return step(1, c + 1, step(0, c, carry))

    m_run = tuple(jnp.full((1, w), NEG_INF, F32) for _ in range(DIFF_HEADS))
    m_chunk = tuple(scores(0, 0, h) for h in range(DIFF_HEADS))
    npair = (nchunk - 1) // 2
    m_run, m_chunk = lax.fori_loop(0, npair, body, (m_run, m_chunk))
    if nchunk % 2 == 0:
        m_run, m_chunk = step(0, nchunk - 2, (m_run, m_chunk))
        for h in range(DIFF_HEADS):
            softmax_pv(1, nchunk - 1, h, m_run[h], m_chunk[h])
    else:
        for h in range(DIFF_HEADS):
            softmax_pv(0, nchunk - 1, h, m_run[h], m_chunk[h])
    outs = []
    for h in range(DIFF_HEADS):
        o = acc_ref[h, 0:HEAD_DIM, :] / acc_ref[h, HEAD_DIM:HEAD_DIM + 1, :]
        d = o[:, 0:tq] - lam * o[:, tq:w]
        d = d * lax.rsqrt(jnp.mean(d * d, axis=0, keepdims=True) + EPS)
        outs.append(d * g_ref[...] * (1.0 - lambda_init))
    o_ref[...] = jnp.concatenate(outs, axis=0).T.astype(BF16)


def _diff_attention(df, df_c, lqk, subln_g, lambda_init, nbatch, seq, n_ctx):
    tq, ck = min(DIFF_TQ, seq), DIFF_CK
    s_all = seq + n_ctx
    assert s_all % ck == 0 and seq % tq == 0
    nchunk = s_all // ck
    nq = seq // tq
    dfl = df.reshape(nbatch, seq, 768)
    dfc = df_c.reshape(nbatch, n_ctx, 768)
    qt = dfl[:, :, 0:256].transpose(0, 2, 1)
    k_all = jnp.concatenate([dfl[:, :, 256:512], dfc[:, :, 256:512]], axis=1).reshape(nbatch, nchunk, ck, 256)
    v_all = jnp.concatenate([dfl[:, :, 512:768], dfc[:, :, 512:768]], axis=1)
    v_all = v_all.reshape(nbatch, nchunk, ck, DIFF_HEADS, HEAD_DIM)
    pad = jnp.zeros((nbatch, nchunk, ck, DIFF_HEADS, DIFF_VROWS - HEAD_DIM), BF16).at[..., 0].set(1.0)
    vt_all = jnp.concatenate([v_all, pad], axis=-1).reshape(nbatch, nchunk, ck, DIFF_HEADS * DIFF_VROWS)
    vt_all = vt_all.transpose(0, 1, 3, 2)
    vec = pl.BlockSpec((1, DIFF_QK_DIM), lambda b, n: (0, 0))
    return pl.pallas_call(
        functools.partial(_diff_kernel, nchunk=nchunk, lambda_init=lambda_init),
        grid=(nbatch, nq),
        in_specs=[pl.BlockSpec((1, 256, tq), lambda b, n: (b, 0, n)),
                  pl.BlockSpec((1, nchunk, ck, 256), lambda b, n: (b, 0, 0, 0)),
                  pl.BlockSpec((1, nchunk, DIFF_HEADS * DIFF_VROWS, ck), lambda b, n: (b, 0, 0, 0)),
                  vec, vec, vec, vec,
                  pl.BlockSpec((HEAD_DIM, 1), lambda b, n: (0, 0))],
        out_specs=pl.BlockSpec((tq, 256), lambda b, n: (b * nq + n, 0)),
        out_shape=jax.ShapeDtypeStruct((nbatch * seq, 256), BF16),
        scratch_shapes=[pltpu.VMEM((256, 2 * DIFF_HEADS * tq), BF16),
                        pltpu.VMEM((DIFF_HEADS, DIFF_VROWS, 2 * tq), F32),
                        pltpu.VMEM((2, DIFF_HEADS, ck, 2 * tq), F32)],
        compiler_params=_params("arbitrary", "arbitrary"),
        name="diff_attention",
    )(qt, k_all, vt_all, *lqk, subln_g.reshape(HEAD_DIM, 1).astype(F32))


def _ctx_attn_kernel(na_ref, sw_ref, df_ref, sink_ref, lq1_ref, lk1_ref, lq2_ref, lk2_ref, g_ref,
                     nb_o, sw_o, df_o, *, lambda_init):
    n = na_ref.shape[0]
    na = na_ref[...]
    outs = []
    for h in range(NA_HEADS):
        sl = slice(h * HEAD_DIM, (h + 1) * HEAD_DIM)
        (p,), l = _softmax_parts([_nt_dot(na[:, sl], na[:, 256 + h * HEAD_DIM:256 + (h + 1) * HEAD_DIM])])
        outs.append(_dot(p.astype(BF16), na[:, 512 + h * HEAD_DIM:512 + (h + 1) * HEAD_DIM]) / l)
    nb_o[...] = jnp.concatenate(outs, axis=1).astype(BF16)
    sw = sw_ref[...]
    outs = []
    for hq in range(4):
        kv = hq // 2
        k = sw[:, 256 + kv * HEAD_DIM:256 + (kv + 1) * HEAD_DIM]
        v = sw[:, 384 + kv * HEAD_DIM:384 + (kv + 1) * HEAD_DIM]
        sk = jnp.broadcast_to(sink_ref[0:1, hq:hq + 1], (n, 1))
        (p,), l = _softmax_parts([_nt_dot(sw[:, hq * HEAD_DIM:(hq + 1) * HEAD_DIM], k)], extra=sk)
        outs.append(_dot(p.astype(BF16), v) / l)
    sw_o[...] = jnp.concatenate(outs, axis=1).astype(BF16)
    lam = _diff_lambda(lq1_ref, lk1_ref, lq2_ref, lk2_ref, lambda_init)
    df = df_ref[...]
    outs = []
    for h in range(DIFF_HEADS):
        sl = slice(h * HEAD_DIM, (h + 1) * HEAD_DIM)
        q2 = _stack_maps(df[:, sl])
        (p,), l = _softmax_parts([_nt_dot(q2, df[:, 256 + h * HEAD_DIM:256 + (h + 1) * HEAD_DIM])])
        o = _dot(p.astype(BF16), df[:, 512 + h * HEAD_DIM:512 + (h + 1) * HEAD_DIM]) / l
        outs.append(_subln(o[0:n], o[n:2 * n], lam, g_ref[...], lambda_init))
    df_o[...] = jnp.concatenate(outs, axis=1).astype(BF16)


def _ctx_attention(na_c, sw_c, df_c, sink, lqk, subln_g, lambda_init, nbatch, n_ctx):
    sink_pad = jnp.zeros((1, 128), F32).at[0, 0:4].set(sink.astype(F32))
    vec = pl.BlockSpec((1, DIFF_QK_DIM), lambda b: (0, 0))
    out = jax.ShapeDtypeStruct((nbatch * n_ctx, 256), BF16)
    return pl.pallas_call(
        functools.partial(_ctx_attn_kernel, lambda_init=lambda_init),
        grid=(nbatch,),
        in_specs=[pl.BlockSpec((n_ctx, 768), lambda b: (b, 0)),
                  pl.BlockSpec((n_ctx, 512), lambda b: (b, 0)),
                  pl.BlockSpec((n_ctx, 768), lambda b: (b, 0)),
                  pl.BlockSpec((1, 128), lambda b: (0, 0)),
                  vec, vec, vec, vec,
                  pl.BlockSpec((1, HEAD_DIM), lambda b: (0, 0))],
        out_specs=[pl.BlockSpec((n_ctx, 256), lambda b: (b, 0))] * 3,
        out_shape=[out, out, out],
        compiler_params=_params("arbitrary"),
        name="ctx_attention",
    )(na_c, sw_c, df_c, sink_pad, *lqk, subln_g.reshape(1, HEAD_DIM).astype(F32))


def _merge_kernel(u_ref, yf_ref, yr_ref, d_ref, gw_ref, gb_ref, yb_ref, yc_ref, yd_ref, gate_ref,
                  wb_ref, wo_ref, x_ref, g1_ref, n2_ref, sc2_ref, sh2_ref, rw_ref,
                  x1_o, h2_o, h2p_o, lg_o):
    y = u_ref[...] * d_ref[...] + yf_ref[...] + yr_ref[...]
    a = jax.nn.gelu(y, approximate=True)
    ya = a * jax.nn.sigmoid(_dot(a.astype(BF16), gw_ref[...]) + gb_ref[...])
    branches = (ya.astype(BF16), yb_ref[...], yc_ref[...], yd_ref[...])
    acc = None
    for i in range(4):
        t = gate_ref[:, i * D_MODEL:(i + 1) * D_MODEL].astype(F32) * _dot(branches[i], wb_ref[i])
        acc = t if acc is None else acc + t
    mixed = _dot(acc.astype(BF16), wo_ref[...])
    x1 = x_ref[...] + g1_ref[0] * mixed
    x1_o[...] = x1
    h2 = _rms(x1) * n2_ref[...]
    h2 = h2 * (1.0 + sc2_ref[0]) + sh2_ref[0]
    h2_o[...] = h2.astype(BF16)
    h2p_o[...] = _pack_rows(h2[:, 0:HALF_D], h2[:, HALF_D:D_MODEL])
    lg_o[...] = jnp.dot(h2, rw_ref[...], preferred_element_type=F32, precision=HIGHEST)


def _merge(u, yf, yr, s5_d, glu_w, glu_b, yb, yc, yd, gate, wb, wo, x2d, g1, norm2_g, sc2, sh2, router_w,
           *, rows_per_mod):
    rows = x2d.shape[0]
    tm = 512
    assert rows % tm == 0 and rows_per_mod % tm == 0

    def mod_map(i):
        return ((i * tm) // rows_per_mod, 0, 0)

    row = lambda w: pl.BlockSpec((tm, w), lambda i: (i, 0))
    full = lambda *shape: pl.BlockSpec(shape, lambda i: (0,) * len(shape))
    mod = pl.BlockSpec((1, 1, D_MODEL), mod_map)
    return pl.pallas_call(
        _merge_kernel,
        grid=(rows // tm,),
        in_specs=[row(256), row(256), row(256), full(1, 256), full(256, 256), full(1, 256),
                  row(256), row(256), row(256), row(GATE_WIDTH),
                  full(4, 256, D_MODEL), full(D_MODEL, D_MODEL), row(D_MODEL),
                  mod, full(1, D_MODEL), mod, mod, full(D_MODEL, N_EXPERTS)],
        out_specs=[row(D_MODEL), row(D_MODEL), row(HALF_D), row(N_EXPERTS)],
        out_shape=[jax.ShapeDtypeStruct((rows, D_MODEL), F32),
                   jax.ShapeDtypeStruct((rows, D_MODEL), BF16),
                   jax.ShapeDtypeStruct((rows, HALF_D), jnp.int32),
                   jax.ShapeDtypeStruct((rows, N_EXPERTS), F32)],
        compiler_params=_params("arbitrary"),
        name="merge",
    )(u, yf, yr, s5_d.reshape(1, 256).astype(F32), glu_w.astype(BF16), glu_b.reshape(1, 256).astype(F32),
      yb, yc, yd, gate, wb.astype(BF16), wo.astype(BF16), x2d, g1, norm2_g.reshape(1, D_MODEL), sc2, sh2,
      router_w.astype(F32))


def _router_kernel(lg_ref, b_ref, tri_ref, idx_ref, rank_ref, w_ref, cnt_ref, base_ref):
    tr = lg_ref.shape[1]
    gsz = N_EXPERTS // N_EXPERT_GROUPS
    sc = jax.nn.sigmoid(lg_ref[...])
    bi = sc + b_ref[...]
    e_iota = lax.broadcasted_iota(jnp.int32, (gsz, tr), 0).astype(F32)
    groups = [bi[g * gsz:(g + 1) * gsz] for g in range(N_EXPERT_GROUPS)]
    gs = []
    for bg in groups:
        m1 = bg.max(axis=0, keepdims=True)
        i1 = jnp.where(bg == m1, e_iota, float(gsz)).min(axis=0, keepdims=True)
        m2 = jnp.where(e_iota == i1, -jnp.inf, bg).max(axis=0, keepdims=True)
        gs.append(m1 + m2)
    v = []
    for g in range(N_EXPERT_GROUPS):
        rank = jnp.zeros((1, tr), F32)
        for g2 in range(N_EXPERT_GROUPS):
            if g2 == g:
                continue
            beats = (gs[g2] >= gs[g]) if g2 < g else (gs[g2] > gs[g])
            rank = rank + jnp.where(beats, 1.0, 0.0)
        v.append(jnp.where(rank < TOPK_GROUPS, groups[g], NEG_INF))
    flat = [e_iota + float(g * gsz) for g in range(N_EXPERT_GROUPS)]
    sel = [jnp.zeros((gsz, tr), F32) for _ in range(N_EXPERT_GROUPS)]
    picks = []
    for _ in range(TOP_K):
        m = v[0].max(axis=0, keepdims=True)
        for g in range(1, N_EXPERT_GROUPS):
            m = jnp.maximum(m, v[g].max(axis=0, keepdims=True))
        am = jnp.where(v[0] == m, flat[0], float(N_EXPERTS)).min(axis=0, keepdims=True)
        for g in range(1, N_EXPERT_GROUPS):
            am = jnp.minimum(am, jnp.where(v[g] == m, flat[g], float(N_EXPERTS)).min(axis=0, keepdims=True))
        hits = []
        for g in range(N_EXPERT_GROUPS):
            hit = flat[g] == am
            hits.append(hit)
            sel[g] = jnp.where(hit, 1.0, sel[g])
            v[g] = jnp.where(hit, -jnp.inf, v[g])
        picks.append((am, hits))
    scg = [sc[g * gsz:(g + 1) * gsz] for g in range(N_EXPERT_GROUPS)]
    den = (sel[0] * scg[0]).sum(axis=0, keepdims=True)
    for g in range(1, N_EXPERT_GROUPS):
        den = den + (sel[g] * scg[g]).sum(axis=0, keepdims=True)

    @pl.when(pl.program_id(0) == 0)
    def _():
        base_ref[...] = jnp.zeros_like(base_ref)

    sel_all = jnp.concatenate(sel, axis=0)
    before = _dot(sel_all.astype(jnp.bfloat16), tri_ref[...]) + base_ref[...]
    for k, (am, hits) in enumerate(picks):
        wk = jnp.zeros((1, tr), F32)
        rk = jnp.zeros((1, tr), F32)
        for g in range(N_EXPERT_GROUPS):
            wk = wk + jnp.where(hits[g], scg[g], 0.0).sum(axis=0, keepdims=True)
            rk = rk + jnp.where(hits[g], before[g * gsz:(g + 1) * gsz], 0.0).sum(axis=0, keepdims=True)
        idx_ref[k:k + 1, :] = am.astype(jnp.int32)
        rank_ref[k:k + 1, :] = rk.astype(jnp.int32)
        w_ref[k:k + 1, :] = wk / den * ROUTED_SCALE
    idx_ref[TOP_K:8, :] = jnp.zeros((8 - TOP_K, tr), jnp.int32)
    rank_ref[TOP_K:8, :] = jnp.zeros((8 - TOP_K, tr), jnp.int32)
    w_ref[TOP_K:8, :] = jnp.zeros((8 - TOP_K, tr), F32)
    base_ref[...] += sel_all.sum(axis=1, keepdims=True)
    cnt_ref[...] = base_ref[...].astype(jnp.int32)


ROUTER_TILE = 512


def _router(logits_t, router_b):
    ne, rows = logits_t.shape
    tr = ROUTER_TILE
    assert rows % tr == 0
    tri = jnp.asarray(np.triu(np.ones((tr, tr), np.float32), k=1), jnp.bfloat16)
    pick = pl.BlockSpec((8, tr), lambda i: (0, i))
    return pl.pallas_call(
        _router_kernel,
        grid=(rows // tr,),
        in_specs=[pl.BlockSpec((ne, tr), lambda i: (0, i)),
                  pl.BlockSpec((ne, 1), lambda i: (0, 0)),
                  pl.BlockSpec((tr, tr), lambda i: (0, 0))],
        out_specs=[pick, pick, pick, pl.BlockSpec((ne, 1), lambda i: (0, 0))],
        out_shape=[jax.ShapeDtypeStruct((8, rows), jnp.int32),
                   jax.ShapeDtypeStruct((8, rows), jnp.int32),
                   jax.ShapeDtypeStruct((8, rows), F32),
                   jax.ShapeDtypeStruct((ne, 1), jnp.int32)],
        scratch_shapes=[pltpu.VMEM((ne, 1), F32)],
        compiler_params=_params("arbitrary"),
        name="router",
    )(logits_t, router_b.reshape(ne, 1).astype(F32), tri)


MOE_BLOCK = 512
MOE_TOKENS = 256
HALF_D = D_MODEL // 2


def _pack_rows(lo, hi):
    lo_b = pltpu.bitcast(lo.astype(jnp.bfloat16).astype(F32), jnp.uint32)
    hi_b = pltpu.bitcast(hi.astype(jnp.bfloat16).astype(F32), jnp.uint32)
    return pltpu.bitcast((hi_b & jnp.uint32(0xFFFF0000)) | (lo_b >> 16), jnp.int32)


def _unpack_rows(words):
    u = pltpu.bitcast(words, jnp.uint32)
    lo = pltpu.bitcast(u << 16, F32)
    hi = pltpu.bitcast(u & jnp.uint32(0xFFFF0000), F32)
    return lo, hi


def _swiglu(x_bf16, wgu, wd):
    hgu = _dot(x_bf16, wgu)
    g = hgu[:, 0:EXPERT_HIDDEN]
    a = g * jax.nn.sigmoid(g) * hgu[:, EXPERT_HIDDEN:2 * EXPERT_HIDDEN]
    return _dot(a.astype(BF16), wd)


SC_CORES = 2
SC_SUBCORES = 16
SC_STREAM_ROWS = 128


def _sc_for_each_chunk(total, body):
    chunk = SC_STREAM_ROWS
    assert total % chunk == 0
    nchunk = total // chunk
    per_worker = pl.cdiv(nchunk, SC_CORES * SC_SUBCORES)
    first = (lax.axis_index("s") * SC_CORES + lax.axis_index("c")) * per_worker

    @pl.loop(0, per_worker)
    def _(j):
        @pl.when(first + j < nchunk)
        def _():
            body((first + j) * chunk)


def _sc_scatter_rows(rows, slots, n_out):
    total, n = rows.shape
    chunk = SC_STREAM_ROWS
    mesh = plsc.VectorSubcoreMesh(core_axis_name="c", subcore_axis_name="s")

    @functools.partial(
        pl.kernel, mesh=mesh,
        out_type=jax.ShapeDtypeStruct((n_out, n), jnp.int32),
        scratch_types=[pltpu.VMEM((8, chunk), jnp.int32),
                       pltpu.VMEM((chunk, n), jnp.int32),
                       pltpu.SemaphoreType.DMA],
        name="moe_dispatch_sc",
    )
    def scatter(rows_hbm, slot_hbm, out_hbm, idx_v, rows_v, sem):
        def body(off):
            pltpu.sync_copy(rows_hbm.at[pl.ds(off, chunk)], rows_v)
            pltpu.sync_copy(slot_hbm.at[:, pl.ds(off, chunk)], idx_v)
            for k in range(TOP_K):
                pltpu.async_copy(rows_v, out_hbm.at[idx_v.at[k]], sem).wait()

        _sc_for_each_chunk(total, body)

    return scatter(rows, slots)


def _experts_kernel(be_ref, nv_ref, nb_ref, xs_ref, wg_ref, wu_ref, wd_ref, ys_ref, wgu_bf, wd_bf):
    b = pl.program_id(0)

    @pl.when(b < nb_ref[0])
    def _():
        @pl.when((b == 0) | (be_ref[b] != be_ref[jnp.maximum(b - 1, 0)]))
        def _():
            wgu_bf[:, 0:EXPERT_HIDDEN] = wg_ref[0].astype(BF16)
            wgu_bf[:, EXPERT_HIDDEN:2 * EXPERT_HIDDEN] = wu_ref[0].astype(BF16)
            wd_bf[...] = wd_ref[0].astype(BF16)

        lo, hi = _unpack_rows(xs_ref[...])
        live = lax.broadcasted_iota(jnp.int32, lo.shape, 0) < nv_ref[b]
        lo = jnp.where(live, lo, 0.0)
        hi = jnp.where(live, hi, 0.0)
        x = jnp.concatenate([lo, hi], axis=1).astype(BF16)
        y = _swiglu(x, wgu_bf[...], wd_bf[...])
        ys_ref[...] = _pack_rows(y[:, 0:HALF_D], y[:, HALF_D:D_MODEL])


def _sc_gather_rows(table, indices):
    m, n = indices.shape[0], table.shape[1]
    chunk = SC_STREAM_ROWS
    mesh = plsc.VectorSubcoreMesh(core_axis_name="c", subcore_axis_name="s")

    @functools.partial(
        pl.kernel, mesh=mesh,
        out_type=jax.ShapeDtypeStruct((m, n), jnp.int32),
        scratch_types=[pltpu.VMEM((chunk,), jnp.int32),
                       pltpu.VMEM((chunk, n), jnp.int32),
                       pltpu.SemaphoreType.DMA],
        name="moe_gather_sc",
    )
    def gather(table_hbm, idx_hbm, out_hbm, idx_v, rows_v, sem):
        def body(off):
            pltpu.sync_copy(idx_hbm.at[pl.ds(off, chunk)], idx_v)
            pltpu.async_copy(table_hbm.at[idx_v], rows_v, sem).wait()
            pltpu.sync_copy(rows_v, out_hbm.at[pl.ds(off, chunk)])

        _sc_for_each_chunk(m, body)

    return gather(table, indices)


def _combine_kernel(w_ref, rows_ref, h_ref, wsgu_ref, wsd_ref, x1_ref, g2_ref, fg_ref, o_ref, *, final):
    shared = _swiglu(h_ref[...], wsgu_ref[...], wsd_ref[...])
    acc_lo = shared[:, 0:HALF_D]
    acc_hi = shared[:, HALF_D:D_MODEL]
    w = w_ref[...]
    for k in range(TOP_K):
        lo, hi = _unpack_rows(rows_ref[k])
        acc_lo = acc_lo + w[:, k:k + 1] * lo
        acc_hi = acc_hi + w[:, k:k + 1] * hi
    x2 = x1_ref[...] + g2_ref[0] * jnp.concatenate([acc_lo, acc_hi], axis=1)
    if final:
        x2 = _rms(x2) * fg_ref[...]
    o_ref[...] = x2


def _moe(h2, h2p, picks, lp, x1, g2, final_g, *, rows_per_mod, final):
    idx, rank, wsel, counts = picks
    rows = h2.shape[0]
    tt = MOE_TOKENS
    blk = MOE_BLOCK
    assert rows % tt == 0 and rows_per_mod % tt == 0 and (rows * TOP_K) % blk == 0
    ntile = rows // tt
    nblock = rows * TOP_K // blk + N_EXPERTS

    cnt = counts.reshape(N_EXPERTS)
    padded = (cnt + blk - 1) // blk * blk
    pends = jnp.cumsum(padded)
    pstart = (pends - padded).astype(jnp.int32)
    nb_used = (pends[-1] // blk).astype(jnp.int32).reshape(1)
    first_row = jnp.arange(nblock, dtype=jnp.int32) * blk
    block_e = jnp.minimum(jnp.sum(pends[None, :] <= first_row[:, None], axis=1), N_EXPERTS - 1).astype(jnp.int32)
    slot = rank + jnp.sum(jnp.where(idx[:, :, None] == jnp.arange(N_EXPERTS, dtype=jnp.int32), pstart, 0), axis=-1)
    n_valid = jnp.clip(cnt[block_e] + pstart[block_e] - first_row, 0, blk).astype(jnp.int32)

    xs = _sc_scatter_rows(h2p, slot, nblock * blk)

    def blk_map(b, be, nv, nb):
        return (jnp.minimum(b, nb[0] - 1), 0)

    def w_map(b, be, nv, nb):
        return (be[jnp.minimum(b, nb[0] - 1)], 0, 0)

    ys = pl.pallas_call(
        _experts_kernel,
        grid_spec=pltpu.PrefetchScalarGridSpec(
            num_scalar_prefetch=3,
            grid=(nblock,),
            in_specs=[pl.BlockSpec((blk, HALF_D), blk_map),
                      pl.BlockSpec((1, D_MODEL, EXPERT_HIDDEN), w_map),
                      pl.BlockSpec((1, D_MODEL, EXPERT_HIDDEN), w_map),
                      pl.BlockSpec((1, EXPERT_HIDDEN, D_MODEL), w_map)],
            out_specs=pl.BlockSpec((blk, HALF_D), blk_map),
            scratch_shapes=[pltpu.VMEM((D_MODEL, 2 * EXPERT_HIDDEN), BF16),
                            pltpu.VMEM((EXPERT_HIDDEN, D_MODEL), BF16)]),
        out_shape=jax.ShapeDtypeStruct((nblock * blk, HALF_D), jnp.int32),
        compiler_params=_params("arbitrary"),
        name="moe_experts",
    )(block_e, n_valid, nb_used, xs,
      lp['exp_w_gate'].astype(F32), lp['exp_w_up'].astype(F32), lp['exp_w_down'].astype(F32))

    wsgu = jnp.concatenate([lp['sh_w_gate'], lp['sh_w_up']], axis=1).astype(BF16)
    wsd = lp['sh_w_down'].astype(BF16)
    gathered = _sc_gather_rows(ys, slot[0:TOP_K].reshape(TOP_K * rows))
    gathered = gathered.reshape(TOP_K, rows, HALF_D)
    full = lambda *shape: pl.BlockSpec(shape, lambda i: (0,) * len(shape))
    row = lambda width: pl.BlockSpec((tt, width), lambda i: (i, 0))
    return pl.pallas_call(
        functools.partial(_combine_kernel, final=final),
        grid=(ntile,),
        in_specs=[row(8), pl.BlockSpec((TOP_K, tt, HALF_D), lambda i: (0, i, 0)), row(D_MODEL),
                  full(D_MODEL, 2 * EXPERT_HIDDEN), full(EXPERT_HIDDEN, D_MODEL), row(D_MODEL),
                  pl.BlockSpec((1, 1, D_MODEL), lambda i: ((i * tt) // rows_per_mod, 0, 0)),
                  full(1, D_MODEL)],
        out_specs=row(D_MODEL),
        out_shape=jax.ShapeDtypeStruct((rows, D_MODEL), F32),
        compiler_params=_params("arbitrary"),
        name="moe_combine",
    )(wsel.T, gathered, h2, wsgu, wsd, x1, g2, final_g.reshape(1, D_MODEL).astype(F32))


def _reorder_w_in(w_in):
    split = 256 + 768 + 512 + 768
    return jnp.concatenate([w_in[:, split:], w_in[:, :split]], axis=1).astype(BF16)


def _mods(mod_row_block):
    return [mod_row_block[:, None, k * D_MODEL:(k + 1) * D_MODEL] for k in range(6)]


def _moe_block(h2, h2p, logits, lp, x1, g2, final_g, *, rows_per_mod, final):
    picks = _router(logits.T, lp['router_b'])
    return _moe(h2, h2p, picks, lp, x1, g2, final_g, rows_per_mod=rows_per_mod, final=final)


def _layer(x2d, xc2d, c16, lp, layer_idx, tables, final_g, *, nbatch, seq, n_ctx, with_ctx_out, final):
    lambda_init = 0.8 - 0.6 * math.exp(-0.3 * layer_idx)
    mod = _ada_mod(c16, lp['ada_w'].astype(F32), lp['ada_b'].astype(F32))
    sh1, sc1, g1, sh2, sc2, g2 = _mods(mod[0:nbatch])
    csh1, csc1, cg1, csh2, csc2, cg2 = _mods(mod[nbatch:nbatch + 1])
    w_in = _reorder_w_in(lp['w_in'])
    rows_lat = nbatch * seq
    rows_ctx = nbatch * n_ctx

    gate, u, na, sw, df = _inproj(x2d, lp['norm1_g'], sc1, sh1, w_in, tables,
                                  rows_per_mod=seq, rope=True, seq=seq)
    gate_c, u_c, na_c, sw_c, df_c = _inproj(xc2d, lp['norm1_g'], csc1, csh1, w_in, tables,
                                            rows_per_mod=rows_ctx, rope=False, seq=seq)

    win, wout, a_re, a_im = _s5_params(lp['s5_lambda_re'], lp['s5_lambda_im'], lp['s5_log_step'],
                                       lp['s5_b_re'], lp['s5_b_im'], lp['s5_c_re'], lp['s5_c_im'])
    u_tm = jnp.concatenate([u_c.reshape(nbatch, n_ctx, 256).transpose(1, 0, 2),
                            u.reshape(nbatch, seq, 256).transpose(1, 0, 2)], axis=0)
    y_tm = _s5_scan(u_tm, win, wout, a_re, a_im, n_ctx)
    y_bm = y_tm.transpose(0, 2, 1, 3)
    yf = y_bm[0, :, n_ctx:].reshape(rows_lat, 256)
    yr = y_bm[1, :, n_ctx:].reshape(rows_lat, 256)

    lqk = [lp[k].reshape(1, DIFF_QK_DIM).astype(F32) for k in ('diff_lq1', 'diff_lk1', 'diff_lq2', 'diff_lk2')]
    bias = _na_bias_table(lp['na_rpb'], seq // GRID_W)
    yb = _na_attention(na, na_c, bias, nbatch, seq, n_ctx)
    yc = _swa_attention(sw, sw_c, lp['swa_sink'], nbatch, seq, n_ctx)
    yd = _diff_attention(df, df_c, lqk, lp['diff_subln_g'], lambda_init, nbatch, seq, n_ctx)

    merge_w = (lp['s5_d'], lp['s5_glu_w'], lp['s5_glu_b'])
    x1, h2, h2p, logits = _merge(u, yf, yr, *merge_w, yb, yc, yd, gate, lp['w_branch'], lp['w_out'], x2d, g1,
                                 lp['norm2_g'], sc2, sh2, lp['router_w'], rows_per_mod=seq)
    x_out = _moe_block(h2, h2p, logits, lp, x1, g2, final_g, rows_per_mod=seq, final=final)

    xc_out = None
    if with_ctx_out:
        yf_c = y_bm[0, :, :n_ctx].reshape(rows_ctx, 256)
        yr_c = y_bm[1, :, :n_ctx].reshape(rows_ctx, 256)
        yb_c, yc_c, yd_c = _ctx_attention(na_c, sw_c, df_c, lp['swa_sink'], lqk, lp['diff_subln_g'],
                                          lambda_init, nbatch, n_ctx)
        x1c, h2c, h2pc, logits_c = _merge(u_c, yf_c, yr_c, *merge_w, yb_c, yc_c, yd_c, gate_c, lp['w_branch'],
                                          lp['w_out'], xc2d, cg1, lp['norm2_g'], csc2, csh2, lp['router_w'],
                                          rows_per_mod=rows_ctx)
        xc_out = _moe_block(h2c, h2pc, logits_c, lp, x1c, cg2, final_g, rows_per_mod=rows_ctx, final=False)
    return x_out, xc_out


def kernel(x, c, ctx, c_ctx, ada_w, ada_b, norm1_g, norm2_g, w_in, s5_lambda_re, s5_lambda_im, s5_log_step,
           s5_b_re, s5_b_im, s5_c_re, s5_c_im, s5_d, s5_glu_w, s5_glu_b, na_rpb, swa_sink, diff_lq1, diff_lk1,
           diff_lq2, diff_lk2, diff_subln_g, w_branch, w_out, router_w, router_b, exp_w_gate, exp_w_up,
           exp_w_down, sh_w_gate, sh_w_up, sh_w_down, final_g):
    nbatch, seq, d = x.shape
    n_ctx = ctx.shape[1]
    depth = ada_w.shape[0]
    assert d == D_MODEL and nbatch == 8
    stacked = dict(ada_w=ada_w, ada_b=ada_b, norm1_g=norm1_g, norm2_g=norm2_g, w_in=w_in,
                   s5_lambda_re=s5_lambda_re, s5_lambda_im=s5_lambda_im, s5_log_step=s5_log_step,
                   s5_b_re=s5_b_re, s5_b_im=s5_b_im, s5_c_re=s5_c_re, s5_c_im=s5_c_im, s5_d=s5_d,
                   s5_glu_w=s5_glu_w, s5_glu_b=s5_glu_b, na_rpb=na_rpb, swa_sink=swa_sink,
                   diff_lq1=diff_lq1, diff_lk1=diff_lk1, diff_lq2=diff_lq2, diff_lk2=diff_lk2,
                   diff_subln_g=diff_subln_g, w_branch=w_branch, w_out=w_out, router_w=router_w,
                   router_b=router_b, exp_w_gate=exp_w_gate, exp_w_up=exp_w_up, exp_w_down=exp_w_down,
                   sh_w_gate=sh_w_gate, sh_w_up=sh_w_up, sh_w_down=sh_w_down)
    tables = _rope_tables(seq)
    c16 = jnp.concatenate([c.astype(F32), c_ctx.reshape(1, d).astype(F32),
                           jnp.zeros((16 - nbatch - 1, d), F32)], axis=0)
    x2d = x.reshape(nbatch * seq, d).astype(F32)
    xc2d = ctx.reshape(nbatch * n_ctx, d).astype(F32)
    for l in range(depth):
        lp = {k: v[l] for k, v in stacked.items()}
        last = l == depth - 1
        x2d, xc2d = _layer(x2d, xc2d, c16, lp, l, tables, final_g, nbatch=nbatch, seq=seq, n_ctx=n_ctx,
                           with_ctx_out=not last, final=last)
    return x2d.reshape(nbatch, seq, d)
```

```python
import functools
import math

import numpy as np
import jax
import jax.numpy as jnp
from jax import lax
from jax.experimental import pallas as pl
from jax.experimental.pallas import tpu as pltpu
from jax.experimental.pallas import tpu_sc as plsc

F32 = jnp.float32
BF16 = jnp.bfloat16
HIGHEST = lax.Precision.HIGHEST

GRID_W = 64
EPS = 1e-6
NEG_INF = -1e30
ROPE_BASE = 10000.0
D_MODEL = 1024
BRANCH_WIDTH = 256
HEAD_DIM = 64
S5_GROUP = 16
S5_GROUPS = 16
S5_STATE = 64
S5_FLAT = S5_GROUPS * S5_STATE
NA_HEADS = 4
NA_WIN_ROWS = 8
NA_WIN_COLS = 16
SWA_KV_HEADS = 2
SWA_WINDOW = 128
DIFF_HEADS = 4
DIFF_QK_DIM = 32
N_EXPERTS = 64
N_EXPERT_GROUPS = 8
TOPK_GROUPS = 4
TOP_K = 6
EXPERT_HIDDEN = 256
ROUTED_SCALE = 2.5
GATE_WIDTH = 4 * D_MODEL

VMEM_LIMIT = 56 * 1024 * 1024


def _params(*sem):
    return pltpu.CompilerParams(dimension_semantics=sem, vmem_limit_bytes=VMEM_LIMIT)


def _nt_dot(a, b):
    return lax.dot_general(a, b, (((1,), (1,)), ((), ())), preferred_element_type=F32)


def _dot(a, b):
    return jnp.dot(a, b, preferred_element_type=F32)


def _rms(x):
    return x * lax.rsqrt(jnp.mean(x * x, axis=-1, keepdims=True) + EPS)


def _ada_kernel(c_ref, w_ref, b_ref, o_ref):
    c = c_ref[...]
    s = c * jax.nn.sigmoid(c)
    o_ref[...] = jnp.dot(s, w_ref[...], preferred_element_type=F32, precision=HIGHEST) + b_ref[...]


def _ada_mod(cc, w, b):
    rows, d = cc.shape
    width = w.shape[1]
    tn = 1536
    return pl.pallas_call(
        _ada_kernel,
        grid=(width // tn,),
        in_specs=[pl.BlockSpec((rows, d), lambda j: (0, 0)),
                  pl.BlockSpec((d, tn), lambda j: (0, j)),
                  pl.BlockSpec((1, tn), lambda j: (0, j))],
        out_specs=pl.BlockSpec((rows, tn), lambda j: (0, j)),
        out_shape=jax.ShapeDtypeStruct((rows, width), F32),
        compiler_params=_params("arbitrary"),
        name="ada_mod",
    )(cc, w, b.reshape(1, width))


_C_GATE = 0
_C_U = GATE_WIDTH
_C_NA = _C_U + 256
_C_SW = _C_NA + 768
_C_DF = _C_SW + 512
_C_END = _C_DF + 768


def _rope_apply(x, cos, sins, half):
    outs = []
    for j in range(x.shape[1] // 128):
        xs = x[:, j * 128:(j + 1) * 128]
        lane = lax.broadcasted_iota(jnp.int32, xs.shape, 1)
        lo = (lane % (2 * half)) < half
        partner = jnp.where(lo, pltpu.roll(xs, 128 - half, 1), pltpu.roll(xs, half, 1))
        outs.append(xs * cos + partner * sins)
    return outs[0] if len(outs) == 1 else jnp.concatenate(outs, axis=1)


def _inproj_kernel(x_ref, g_ref, sc_ref, sh_ref, w_ref, c64_ref, s64_ref, c32_ref, s32_ref,
                   gate_o, u_o, na_o, sw_o, df_o, *, rope):
    h = _rms(x_ref[...]) * g_ref[...]
    h = h * (1.0 + sc_ref[0]) + sh_ref[0]
    hb = h.astype(BF16)

    def mm(c0, c1):
        return _dot(hb, w_ref[:, c0:c1])

    for k in range(GATE_WIDTH // 512):
        gate_o[:, k * 512:(k + 1) * 512] = jax.nn.sigmoid(mm(k * 512, (k + 1) * 512)).astype(BF16)
    u_o[...] = mm(_C_U, _C_U + 256)

    na = mm(_C_NA, _C_NA + 768)
    na_o[:, 0:256] = (na[:, 0:256] * (HEAD_DIM ** -0.5 * (LOG2E if rope else 1.0))).astype(BF16)
    na_o[:, 256:768] = na[:, 256:768].astype(BF16)

    sw = mm(_C_SW, _C_SW + 512)
    swq, swk = sw[:, 0:256], sw[:, 256:384]
    if rope:
        swq = _rope_apply(swq, c64_ref[...], s64_ref[...], 16)
        swk = _rope_apply(swk, c64_ref[...], s64_ref[...], 16)
    sw_o[:, 0:256] = (swq * (HEAD_DIM ** -0.5 * (LOG2E if rope else 1.0))).astype(BF16)
    sw_o[:, 256:384] = swk.astype(BF16)
    sw_o[:, 384:512] = sw[:, 384:512].astype(BF16)

    df = mm(_C_DF, _C_DF + 768)
    dfq, dfk = df[:, 0:256], df[:, 256:512]
    if rope:
        dfq = _rope_apply(dfq, c32_ref[...], s32_ref[...], 8)
        dfk = _rope_apply(dfk, c32_ref[...], s32_ref[...], 8)
    df_o[:, 0:256] = (dfq * (DIFF_QK_DIM ** -0.5 * (LOG2E if rope else 1.0))).astype(BF16)
    df_o[:, 256:512] = dfk.astype(BF16)
    df_o[:, 512:768] = df[:, 512:768].astype(BF16)


def _inproj(x2d, norm_g, sc, sh, w_bf16, tables, *, rows_per_mod, rope, seq):
    rows = x2d.shape[0]
    tm = 512
    assert rows % tm == 0 and rows_per_mod % tm == 0 and seq % tm == 0
    tiles_per_seq = seq // tm

    def mod_map(i):
        return ((i * tm) // rows_per_mod, 0, 0)

    def tab_map(i):
        return (i % tiles_per_seq, 0)

    tab_spec = pl.BlockSpec((tm, 128), tab_map)
    row = lambda w: pl.BlockSpec((tm, w), lambda i: (i, 0))
    return pl.pallas_call(
        functools.partial(_inproj_kernel, rope=rope),
        grid=(rows // tm,),
        in_specs=[row(D_MODEL),
                  pl.BlockSpec((1, D_MODEL), lambda i: (0, 0)),
                  pl.BlockSpec((1, 1, D_MODEL), mod_map),
                  pl.BlockSpec((1, 1, D_MODEL), mod_map),
                  pl.BlockSpec((D_MODEL, _C_END), lambda i: (0, 0)),
                  tab_spec, tab_spec, tab_spec, tab_spec],
        out_specs=[row(GATE_WIDTH), row(256), row(768), row(512), row(768)],
        out_shape=[jax.ShapeDtypeStruct((rows, GATE_WIDTH), BF16),
                   jax.ShapeDtypeStruct((rows, 256), F32),
                   jax.ShapeDtypeStruct((rows, 768), BF16),
                   jax.ShapeDtypeStruct((rows, 512), BF16),
                   jax.ShapeDtypeStruct((rows, 768), BF16)],
        compiler_params=_params("arbitrary"),
        name="inproj",
    )(x2d, norm_g.reshape(1, D_MODEL), sc, sh, w_bf16, *tables)


def _rope_tables(seq):
    t = jnp.arange(seq)
    rows = (t // GRID_W).astype(F32)
    cols = (t % GRID_W).astype(F32)
    lane = np.arange(128)
    out = []
    for dim in (64, 32):
        quarter = dim // 4
        inv_freq = ROPE_BASE ** (-jnp.arange(quarter, dtype=F32) / quarter)
        l = lane % dim
        use_col = l >= dim // 2
        fidx = l % quarter
        hi = (l % (dim // 2)) >= quarter
        ang_r = rows[:, None] * inv_freq[None, :]
        ang_c = cols[:, None] * inv_freq[None, :]
        ang = jnp.where(use_col[None, :], ang_c[:, fidx], ang_r[:, fidx])
        out.append(jnp.cos(ang))
        out.append(jnp.where(hi[None, :], jnp.sin(ang), -jnp.sin(ang)))
    return tuple(out)


S5_CHUNK = 128


def _s5_kernel(u_ref, win_ref, wout_ref, are_ref, aim_ref, y_ref, bu_ref, st_ref, *, tc, nb):
    d = pl.program_id(0)
    i = pl.program_id(1)

    @pl.when(i == 0)
    def _():
        st_ref[...] = jnp.zeros_like(st_ref)

    u = u_ref[...].reshape(tc * nb, BRANCH_WIDTH).astype(BF16)
    bu_ref[...] = _dot(u, win_ref[0])
    ar = jnp.broadcast_to(are_ref[0], (nb, S5_FLAT))
    ai = jnp.broadcast_to(aim_ref[0], (nb, S5_FLAT))

    def body(j, carry):
        xr, xi = carry
        t = j + d * (tc - 1 - 2 * j)
        row = pl.multiple_of(t * nb, nb)
        br = bu_ref[pl.ds(row, nb), 0:S5_FLAT]
        bi = bu_ref[pl.ds(row, nb), S5_FLAT:2 * S5_FLAT]
        nr = ar * xr - ai * xi + br
        ni = ar * xi + ai * xr + bi
        bu_ref[pl.ds(row, nb), 0:S5_FLAT] = nr
        bu_ref[pl.ds(row, nb), S5_FLAT:2 * S5_FLAT] = ni
        return nr, ni

    xr, xi = lax.fori_loop(0, tc, body, (st_ref[:, 0:S5_FLAT], st_ref[:, S5_FLAT:2 * S5_FLAT]), unroll=4)
    st_ref[:, 0:S5_FLAT] = xr
    st_ref[:, S5_FLAT:2 * S5_FLAT] = xi
    y = _dot(bu_ref[...].astype(BF16), wout_ref[0])
    y_ref[0] = y.reshape(tc, nb, BRANCH_WIDTH)


def _s5_scan(u_tm, win, wout, a_re, a_im, n_ctx):
    s_len, nb, _ = u_tm.shape
    tc = S5_CHUNK
    assert nb == 8 and s_len % tc == 0 and n_ctx % tc == 0
    nct = n_ctx // tc
    nlt = (s_len - n_ctx) // tc

    def chunk(d, i):
        rev = jnp.where(i < nct, nct - 1 - i, 2 * nct + nlt - 1 - i)
        return jnp.where(d == 0, i, rev)

    return pl.pallas_call(
        functools.partial(_s5_kernel, tc=tc, nb=nb),
        grid=(2, nct + nlt),
        in_specs=[pl.BlockSpec((tc, nb, BRANCH_WIDTH), lambda d, i: (chunk(d, i), 0, 0)),
                  pl.BlockSpec((1, BRANCH_WIDTH, 2 * S5_FLAT), lambda d, i: (d, 0, 0)),
                  pl.BlockSpec((1, 2 * S5_FLAT, BRANCH_WIDTH), lambda d, i: (d, 0, 0)),
                  pl.BlockSpec((1, 1, S5_FLAT), lambda d, i: (d, 0, 0)),
                  pl.BlockSpec((1, 1, S5_FLAT), lambda d, i: (d, 0, 0))],
        out_specs=pl.BlockSpec((1, tc, nb, BRANCH_WIDTH), lambda d, i: (d, chunk(d, i), 0, 0)),
        out_shape=jax.ShapeDtypeStruct((2, s_len, nb, BRANCH_WIDTH), F32),
        scratch_shapes=[pltpu.VMEM((tc * nb, 2 * S5_FLAT), F32),
                        pltpu.VMEM((nb, 2 * S5_FLAT), F32)],
        compiler_params=_params("arbitrary", "arbitrary"),
        name="s5_scan",
    )(u_tm, win, wout, a_re, a_im)


def _s5_params(lam_re, lam_im, log_step, b_re, b_im, c_re, c_im):
    lr = lam_re.astype(F32)
    li = lam_im.astype(F32)
    dt = jnp.exp(log_step.astype(F32))[..., None]
    mag = jnp.exp(lr * dt)
    a_re = mag * jnp.cos(li * dt)
    a_im = mag * jnp.sin(li * dt)
    nr, ni, den = a_re - 1.0, a_im, lr * lr + li * li
    k_re = ((nr * lr + ni * li) / den)[..., None]
    k_im = ((ni * lr - nr * li) / den)[..., None]
    br = b_re.astype(F32)
    bi = b_im.astype(F32)
    bb_re = k_re * br - k_im * bi
    bb_im = k_re * bi + k_im * br
    eye = jnp.eye(S5_GROUPS, dtype=F32)

    def blockdiag_in(bb):
        m = jnp.einsum('dgpc,gh->dgchp', bb, eye)
        return m.reshape(2, S5_GROUPS * S5_GROUP, S5_GROUPS * S5_STATE)

    def blockdiag_out(cc):
        m = jnp.einsum('dgcp,gh->dgphc', cc, eye)
        return m.reshape(2, S5_GROUPS * S5_STATE, S5_GROUPS * S5_GROUP)

    win = jnp.concatenate([blockdiag_in(bb_re), blockdiag_in(bb_im)], axis=2).astype(BF16)
    wout = jnp.concatenate([blockdiag_out(c_re.astype(F32)), -blockdiag_out(c_im.astype(F32))], axis=1).astype(BF16)
    return win, wout, a_re.reshape(2, 1, S5_FLAT), a_im.reshape(2, 1, S5_FLAT)


def _softmax_parts(scores, extra=None):
    m = scores[0].max(axis=-1, keepdims=True)
    for s in scores[1:]:
        m = jnp.maximum(m, s.max(axis=-1, keepdims=True))
    if extra is not None:
        m = jnp.maximum(m, extra)
    ps = [jnp.exp(s - m) for s in scores]
    l = ps[0].sum(axis=-1, keepdims=True)
    for p in ps[1:]:
        l = l + p.sum(axis=-1, keepdims=True)
    if extra is not None:
        l = l + jnp.exp(extra - m)
    return ps, l


NA_QROWS = 2
NA_KROWS = 10
VROWS = HEAD_DIM + 16


def _na_window_start(r, grid_rows):
    start = jnp.clip(r - NA_WIN_ROWS // 2, 0, grid_rows - NA_WIN_ROWS)
    return (jnp.minimum(start, grid_rows - NA_KROWS) // 2) * 2


def _head_blockdiag(qt, qbd_ref, nheads, rows_per_head):
    n = qt.shape[1]
    row_h = lax.broadcasted_iota(jnp.int32, qt.shape, 0) // rows_per_head
    zero = jnp.zeros_like(qt)
    for h in range(nheads):
        qbd_ref[:, h * n:(h + 1) * n] = jnp.where(row_h == h, qt, zero)


def _na_kernel(qt_ref, k_ref, vt_ref, kc_ref, vct_ref, bias_ref, o_ref, qbd_ref, *, grid_rows):
    nq = NA_QROWS * GRID_W
    nk = NA_KROWS * GRID_W
    off = pl.multiple_of(_na_window_start(NA_QROWS * pl.program_id(1), grid_rows) * GRID_W, 128)
    _head_blockdiag(qt_ref[0], qbd_ref, NA_HEADS, HEAD_DIM)
    qbd = qbd_ref[...]
    s_loc = _dot(k_ref[0, pl.ds(off, nk), :], qbd) + bias_ref[0]
    s_ctx = _dot(kc_ref[0], qbd)
    m = jnp.maximum(s_loc.max(axis=0, keepdims=True), s_ctx.max(axis=0, keepdims=True))
    p_loc = jnp.exp2((s_loc - m).astype(BF16))
    p_ctx = jnp.exp2((s_ctx - m).astype(BF16))
    vw = vt_ref[0, :, pl.ds(off, nk)]
    outs = []
    for h in range(NA_HEADS):
        rows = slice(h * VROWS, (h + 1) * VROWS)
        cols = slice(h * nq, (h + 1) * nq)
        o = _dot(vw[rows], p_loc[:, cols]) + _dot(vct_ref[0, rows, :], p_ctx[:, cols])
        outs.append(o[0:HEAD_DIM] / o[HEAD_DIM:HEAD_DIM + 1])
    o_ref[...] = jnp.concatenate(outs, axis=0).T.astype(BF16)


def _na_classes(grid_rows):
    return [0, 2, 4, grid_rows - 4, grid_rows - 2]


def _na_bias_table(rpb, grid_rows):
    col = np.arange(GRID_W)
    cstart = np.clip(col - NA_WIN_COLS // 2, 0, GRID_W - NA_WIN_COLS)
    col_in = (col[None, :] >= cstart[:, None]) & (col[None, :] < cstart[:, None] + NA_WIN_COLS)
    cb = np.clip(col[None, :] - col[:, None] + (NA_WIN_COLS - 1), 0, 2 * NA_WIN_COLS - 2)
    classes = _na_classes(grid_rows)
    rbi = np.zeros((len(classes), NA_QROWS, NA_KROWS), np.int64)
    row_in = np.zeros((len(classes), NA_QROWS, NA_KROWS), bool)
    for c, r in enumerate(classes):
        a_row = (min(int(np.clip(r - NA_WIN_ROWS // 2, 0, grid_rows - NA_WIN_ROWS)), grid_rows - NA_KROWS) // 2) * 2
        for qi in range(NA_QROWS):
            start_q = int(np.clip(r + qi - NA_WIN_ROWS // 2, 0, grid_rows - NA_WIN_ROWS))
            for j in range(NA_KROWS):
                row_in[c, qi, j] = start_q <= a_row + j < start_q + NA_WIN_ROWS
                rbi[c, qi, j] = np.clip(a_row + j - (r + qi) + NA_WIN_ROWS - 1, 0, 2 * NA_WIN_ROWS - 2)
    oh_row = jnp.asarray(rbi[..., None] == np.arange(2 * NA_WIN_ROWS - 1), F32)
    oh_col = jnp.asarray(cb[:, :, None] == np.arange(2 * NA_WIN_COLS - 1), F32)
    t = jnp.einsum('hab,cija,qkb->cjkhiq', rpb.astype(F32), oh_row, oh_col, precision=HIGHEST) * LOG2E
    valid = row_in.transpose(0, 2, 1)[:, :, None, None, :, None] & col_in.T[None, None, :, None, None, :]
    t = jnp.where(valid, t, NEG_INF)
    return t.reshape(len(classes), NA_KROWS * GRID_W, NA_HEADS * NA_QROWS * GRID_W)


def _augment_vt(v, nheads):
    nb, s, _ = v.shape
    v = v.reshape(nb, s, nheads, HEAD_DIM)
    pad = jnp.zeros((nb, s, nheads, VROWS - HEAD_DIM), v.dtype).at[..., 0].set(1.0)
    return jnp.concatenate([v, pad], axis=-1).reshape(nb, s, nheads * VROWS).transpose(0, 2, 1)


def _na_attention(na, na_c, bias, nbatch, seq, n_ctx):
    grid_rows = seq // GRID_W
    assert grid_rows >= NA_KROWS and grid_rows % NA_QROWS == 0
    nq = NA_QROWS * GRID_W
    npair = grid_rows // NA_QROWS
    nal = na.reshape(nbatch, seq, 768)
    nac = na_c.reshape(nbatch, n_ctx, 768)
    qt = nal[:, :, 0:256].transpose(0, 2, 1)
    vt = _augment_vt(nal[:, :, 512:768], NA_HEADS)
    vct = _augment_vt(nac[:, :, 512:768], NA_HEADS)

    def cls(b, p):
        r = NA_QROWS * p
        c = jnp.where(r < 4, r // 2, jnp.where(r >= grid_rows - 4, (r - (grid_rows - 4)) // 2 + 3, 2))
        return (c, 0, 0)

    return pl.pallas_call(
        functools.partial(_na_kernel, grid_rows=grid_rows),
        grid=(nbatch, npair),
        in_specs=[pl.BlockSpec((1, 256, nq), lambda b, p: (b, 0, p)),
                  pl.BlockSpec((1, seq, 256), lambda b, p: (b, 0, 1)),
                  pl.BlockSpec((1, NA_HEADS * VROWS, seq), lambda b, p: (b, 0, 0)),
                  pl.BlockSpec((1, n_ctx, 256), lambda b, p: (b, 0, 1)),
                  pl.BlockSpec((1, NA_HEADS * VROWS, n_ctx), lambda b, p: (b, 0, 0)),
                  pl.BlockSpec((1, NA_KROWS * GRID_W, NA_HEADS * nq), cls)],
        out_specs=pl.BlockSpec((nq, 256), lambda b, p: (b * npair + p, 0)),
        out_shape=jax.ShapeDtypeStruct((nbatch * seq, 256), BF16),
        scratch_shapes=[pltpu.VMEM((256, NA_HEADS * nq), BF16)],
        compiler_params=_params("arbitrary", "arbitrary"),
        name="na_attention",
    )(qt, nal, vt, nac, vct, bias)


SWA_BLOCK = 128


def _swa_kernel(qt_ref, k_ref, vt_ref, kc_ref, vct_ref, sink_ref, o_ref, qbd_ref, *, seq):
    n = pl.program_id(1)
    band = 3 * SWA_BLOCK
    nq = SWA_BLOCK
    nqh = 2 * SWA_KV_HEADS
    bstart = pl.multiple_of(jnp.clip((n - 1) * SWA_BLOCK, 0, seq - band), SWA_BLOCK)
    qt = qt_ref[0]
    zero = jnp.zeros((HEAD_DIM, nq), qt.dtype)
    for hq in range(nqh):
        qh = qt[hq * HEAD_DIM:(hq + 1) * HEAD_DIM]
        qbd_ref[:, hq * nq:(hq + 1) * nq] = jnp.concatenate([qh, zero] if hq // 2 == 0 else [zero, qh], axis=0)
    qbd = qbd_ref[...]
    kpos = bstart + lax.broadcasted_iota(jnp.int32, (band, nqh * nq), 0)
    qpos = n * SWA_BLOCK + lax.broadcasted_iota(jnp.int32, (band, nqh * nq), 1) % nq
    s_loc = jnp.where(jnp.abs(qpos - kpos) <= SWA_WINDOW, _dot(k_ref[0, pl.ds(bstart, band), :], qbd), NEG_INF)
    s_ctx = _dot(kc_ref[0], qbd)
    sink = jnp.concatenate([jnp.broadcast_to(sink_ref[0:1, hq:hq + 1] * LOG2E, (1, nq)) for hq in range(nqh)], axis=1)
    m = jnp.maximum(jnp.maximum(s_loc.max(axis=0, keepdims=True), s_ctx.max(axis=0, keepdims=True)), sink)
    p_loc = jnp.exp2((s_loc - m).astype(BF16))
    p_ctx = jnp.exp2((s_ctx - m).astype(BF16))
    p_sink = jnp.exp2(sink - m)
    vw = vt_ref[0, :, pl.ds(bstart, band)]
    outs = []
    for hq in range(nqh):
        rows = slice((hq // 2) * VROWS, (hq // 2 + 1) * VROWS)
        cols = slice(hq * nq, (hq + 1) * nq)
        o = _dot(vw[rows], p_loc[:, cols]) + _dot(vct_ref[0, rows, :], p_ctx[:, cols])
        outs.append(o[0:HEAD_DIM] / (o[HEAD_DIM:HEAD_DIM + 1] + p_sink[:, cols]))
    o_ref[...] = jnp.concatenate(outs, axis=0).T.astype(BF16)


def _swa_attention(sw, sw_c, sink, nbatch, seq, n_ctx):
    nblk = seq // SWA_BLOCK
    nqh = 2 * SWA_KV_HEADS
    assert seq >= 3 * SWA_BLOCK
    sink_pad = jnp.zeros((1, 128), F32).at[0, 0:nqh].set(sink.astype(F32))
    swl = sw.reshape(nbatch, seq, 512)
    swc = sw_c.reshape(nbatch, n_ctx, 512)
    qt = swl[:, :, 0:256].transpose(0, 2, 1)
    vt = _augment_vt(swl[:, :, 384:512], SWA_KV_HEADS)
    vct = _augment_vt(swc[:, :, 384:512], SWA_KV_HEADS)
    return pl.pallas_call(
        functools.partial(_swa_kernel, seq=seq),
        grid=(nbatch, nblk),
        in_specs=[pl.BlockSpec((1, 256, SWA_BLOCK), lambda b, n: (b, 0, n)),
                  pl.BlockSpec((1, seq, 128), lambda b, n: (b, 0, 2)),
                  pl.BlockSpec((1, SWA_KV_HEADS * VROWS, seq), lambda b, n: (b, 0, 0)),
                  pl.BlockSpec((1, n_ctx, 128), lambda b, n: (b, 0, 2)),
                  pl.BlockSpec((1, SWA_KV_HEADS * VROWS, n_ctx), lambda b, n: (b, 0, 0)),
                  pl.BlockSpec((1, 128), lambda b, n: (0, 0))],
        out_specs=pl.BlockSpec((SWA_BLOCK, 256), lambda b, n: (b * nblk + n, 0)),
        out_shape=jax.ShapeDtypeStruct((nbatch * seq, 256), BF16),
        scratch_shapes=[pltpu.VMEM((SWA_KV_HEADS * HEAD_DIM, nqh * SWA_BLOCK), BF16)],
        compiler_params=_params("arbitrary", "arbitrary"),
        name="swa_attention",
    )(qt, swl, vt, swc, vct, sink_pad)


DIFF_TQ = 512
DIFF_CK = 256
LOG2E = math.log2(math.e)
DIFF_VROWS = HEAD_DIM + 16


def _diff_lambda(lq1_ref, lk1_ref, lq2_ref, lk2_ref, lambda_init):
    s1 = jnp.sum(lq1_ref[...] * lk1_ref[...], axis=-1, keepdims=True)
    s2 = jnp.sum(lq2_ref[...] * lk2_ref[...], axis=-1, keepdims=True)
    return jnp.exp(s1) - jnp.exp(s2) + lambda_init


def _stack_maps(qh):
    lane = lax.broadcasted_iota(jnp.int32, qh.shape, 1)
    zero = jnp.zeros_like(qh)
    return jnp.concatenate([jnp.where(lane < DIFF_QK_DIM, qh, zero),
                            jnp.where(lane >= DIFF_QK_DIM, qh, zero)], axis=0)


def _subln(o0, o1, lam, g, lambda_init):
    o = o0 - lam * o1
    return _rms(o) * g * (1.0 - lambda_init)


def _diff_kernel(qt_ref, k_ref, vt_ref, lq1_ref, lk1_ref, lq2_ref, lk2_ref, g_ref, o_ref, qbd_ref, acc_ref, s_ref,
                 *, nchunk, lambda_init):
    lam = _diff_lambda(lq1_ref, lk1_ref, lq2_ref, lk2_ref, lambda_init)
    qt = qt_ref[0]
    tq = qt.shape[1]
    w = 2 * tq
    row = lax.broadcasted_iota(jnp.int32, qt.shape, 0) // DIFF_QK_DIM
    zero = jnp.zeros_like(qt)
    for j in range(2 * DIFF_HEADS):
        qbd_ref[:, j * tq:(j + 1) * tq] = jnp.where(row == j, qt, zero)
    acc_ref[...] = jnp.zeros_like(acc_ref)

    def scores(slot, c, h):
        s = _dot(k_ref[0, c], qbd_ref[:, h * w:(h + 1) * w])
        s_ref[slot, h] = s
        return s.max(axis=0, keepdims=True)

    def softmax_pv(slot, c, h, m_run, m_chunk):
        m_new = jnp.maximum(m_run, m_chunk)
        alpha = jnp.exp2(m_run - m_new)
        p = jnp.exp2((s_ref[slot, h] - m_new).astype(BF16))
        acc_ref[h] = alpha * acc_ref[h] + _dot(vt_ref[0, c, h * DIFF_VROWS:(h + 1) * DIFF_VROWS, :], p)
        return m_new

    def step(slot, c, carry):
        m_run, m_chunk = carry
        new_run, new_chunk = [], []
        for h in range(DIFF_HEADS):
            new_chunk.append(scores(1 - slot, c + 1, h))
            new_run.append(softmax_pv(slot, c, h, m_run[h], m_chunk[h]))
        return tuple(new_run), tuple(new_chunk)

    def body(i, carry):
        c = 2 * i
        return step(1, c + 1, step(0, c, carry))

    m_run = tuple(jnp.full((1, w), NEG_INF, F32) for _ in range(DIFF_HEADS))
    m_chunk = tuple(scores(0, 0, h) for h in range(DIFF_HEADS))
    npair = (nchunk - 1) // 2
    m_run, m_chunk = lax.fori_loop(0, npair, body, (m_run, m_chunk))
    if nchunk % 2 == 0:
        m_run, m_chunk = step(0, nchunk - 2, (m_run, m_chunk))
        for h in range(DIFF_HEADS):
            softmax_pv(1, nchunk - 1, h, m_run[h], m_chunk[h])
    else:
        for h in range(DIFF_HEADS):
            softmax_pv(0, nchunk - 1, h, m_run[h], m_chunk[h])
    outs = []
    for h in range(DIFF_HEADS):
        o = acc_ref[h, 0:HEAD_DIM, :] / acc_ref[h, HEAD_DIM:HEAD_DIM + 1, :]
        d = o[:, 0:tq] - lam * o[:, tq:w]
        d = d * lax.rsqrt(jnp.mean(d * d, axis=0, keepdims=True) + EPS)
        outs.append(d * g_ref[...] * (1.0 - lambda_init))
    o_ref[...] = jnp.concatenate(outs, axis=0).T.astype(BF16)


def _diff_attention(df, df_c, lqk, subln_g, lambda_init, nbatch, seq, n_ctx):
    tq, ck = min(DIFF_TQ, seq), DIFF_CK
    s_all = seq + n_ctx
    assert s_all % ck == 0 and seq % tq == 0
    nchunk = s_all // ck
    nq = seq // tq
    dfl = df.reshape(nbatch, seq, 768)
    dfc = df_c.reshape(nbatch, n_ctx, 768)
    qt = dfl[:, :, 0:256].transpose(0, 2, 1)
    k_all = jnp.concatenate([dfl[:, :, 256:512], dfc[:, :, 256:512]], axis=1).reshape(nbatch, nchunk, ck, 256)
    v_all = jnp.concatenate([dfl[:, :, 512:768], dfc[:, :, 512:768]], axis=1)
    v_all = v_all.reshape(nbatch, nchunk, ck, DIFF_HEADS, HEAD_DIM)
    pad = jnp.zeros((nbatch, nchunk, ck, DIFF_HEADS, DIFF_VROWS - HEAD_DIM), BF16).at[..., 0].set(1.0)
    vt_all = jnp.concatenate([v_all, pad], axis=-1).reshape(nbatch, nchunk, ck, DIFF_HEADS * DIFF_VROWS)
    vt_all = vt_all.transpose(0, 1, 3, 2)
    vec = pl.BlockSpec((1, DIFF_QK_DIM), lambda b, n: (0, 0))
    return pl.pallas_call(
        functools.partial(_diff_kernel, nchunk=nchunk, lambda_init=lambda_init),
        grid=(nbatch, nq),
        in_specs=[pl.BlockSpec((1, 256, tq), lambda b, n: (b, 0, n)),
                  pl.BlockSpec((1, nchunk, ck, 256), lambda b, n: (b, 0, 0, 0)),
                  pl.BlockSpec((1, nchunk, DIFF_HEADS * DIFF_VROWS, ck), lambda b, n: (b, 0, 0, 0)),
                  vec, vec, vec, vec,
                  pl.BlockSpec((HEAD_DIM, 1), lambda b, n: (0, 0))],
        out_specs=pl.BlockSpec((tq, 256), lambda b, n: (b * nq + n, 0)),
        out_shape=jax.ShapeDtypeStruct((nbatch * seq, 256), BF16),
        scratch_shapes=[pltpu.VMEM((256, 2 * DIFF_HEADS * tq), BF16),
                        pltpu.VMEM((DIFF_HEADS, DIFF_VROWS, 2 * tq), F32),
                        pltpu.VMEM((2, DIFF_HEADS, ck, 2 * tq), F32)],
        compiler_params=_params("arbitrary", "arbitrary"),
        name="diff_attention",
    )(qt, k_all, vt_all, *lqk, subln_g.reshape(HEAD_DIM, 1).astype(F32))


def _ctx_attn_kernel(na_ref, sw_ref, df_ref, sink_ref, lq1_ref, lk1_ref, lq2_ref, lk2_ref, g_ref,
                     nb_o, sw_o, df_o, *, lambda_init):
    n = na_ref.shape[0]
    na = na_ref[...]
    outs = []
    for h in range(NA_HEADS):
        sl = slice(h * HEAD_DIM, (h + 1) * HEAD_DIM)
        (p,), l = _softmax_parts([_nt_dot(na[:, sl], na[:, 256 + h * HEAD_DIM:256 + (h + 1) * HEAD_DIM])])
        outs.append(_dot(p.astype(BF16), na[:, 512 + h * HEAD_DIM:512 + (h + 1) * HEAD_DIM]) / l)
    nb_o[...] = jnp.concatenate(outs, axis=1).astype(BF16)
    sw = sw_ref[...]
    outs = []
    for hq in range(4):
        kv = hq // 2
        k = sw[:, 256 + kv * HEAD_DIM:256 + (kv + 1) * HEAD_DIM]
        v = sw[:, 384 + kv * HEAD_DIM:384 + (kv + 1) * HEAD_DIM]
        sk = jnp.broadcast_to(sink_ref[0:1, hq:hq + 1], (n, 1))
        (p,), l = _softmax_parts([_nt_dot(sw[:, hq * HEAD_DIM:(hq + 1) * HEAD_DIM], k)], extra=sk)
        outs.append(_dot(p.astype(BF16), v) / l)
    sw_o[...] = jnp.concatenate(outs, axis=1).astype(BF16)
    lam = _diff_lambda(lq1_ref, lk1_ref, lq2_ref, lk2_ref, lambda_init)
    df = df_ref[...]
    outs = []
    for h in range(DIFF_HEADS):
        sl = slice(h * HEAD_DIM, (h + 1) * HEAD_DIM)
        q2 = _stack_maps(df[:, sl])
        (p,), l = _softmax_parts([_nt_dot(q2, df[:, 256 + h * HEAD_DIM:256 + (h + 1) * HEAD_DIM])])
        o = _dot(p.astype(BF16), df[:, 512 + h * HEAD_DIM:512 + (h + 1) * HEAD_DIM]) / l
        outs.append(_subln(o[0:n], o[n:2 * n], lam, g_ref[...], lambda_init))
    df_o[...] = jnp.concatenate(outs, axis=1).astype(BF16)


def _ctx_attention(na_c, sw_c, df_c, sink, lqk, subln_g, lambda_init, nbatch, n_ctx):
    sink_pad = jnp.zeros((1, 128), F32).at[0, 0:4].set(sink.astype(F32))
    vec = pl.BlockSpec((1, DIFF_QK_DIM), lambda b: (0, 0))
    out = jax.ShapeDtypeStruct((nbatch * n_ctx, 256), BF16)
    return pl.pallas_call(
        functools.partial(_ctx_attn_kernel, lambda_init=lambda_init),
        grid=(nbatch,),
        in_specs=[pl.BlockSpec((n_ctx, 768), lambda b: (b, 0)),
                  pl.BlockSpec((n_ctx, 512), lambda b: (b, 0)),
                  pl.BlockSpec((n_ctx, 768), lambda b: (b, 0)),
                  pl.BlockSpec((1, 128), lambda b: (0, 0)),
                  vec, vec, vec, vec,
                  pl.BlockSpec((1, HEAD_DIM), lambda b: (0, 0))],
        out_specs=[pl.BlockSpec((n_ctx, 256), lambda b: (b, 0))] * 3,
        out_shape=[out, out, out],
        compiler_params=_params("arbitrary"),
        name="ctx_attention",
    )(na_c, sw_c, df_c, sink_pad, *lqk, subln_g.reshape(1, HEAD_DIM).astype(F32))


def _merge_kernel(u_ref, yf_ref, yr_ref, d_ref, gw_ref, gb_ref, yb_ref, yc_ref, yd_ref, gate_ref,
                  wb_ref, wo_ref, x_ref, g1_ref, n2_ref, sc2_ref, sh2_ref, rw_ref,
                  x1_o, h2_o, h2p_o, lg_o):
    y = u_ref[...] * d_ref[...] + yf_ref[...] + yr_ref[...]
    a = jax.nn.gelu(y, approximate=True)
    ya = a * jax.nn.sigmoid(_dot(a.astype(BF16), gw_ref[...]) + gb_ref[...])
    branches = (ya.astype(BF16), yb_ref[...], yc_ref[...], yd_ref[...])
    acc = None
    for i in range(4):
        t = gate_ref[:, i * D_MODEL:(i + 1) * D_MODEL].astype(F32) * _dot(branches[i], wb_ref[i])
        acc = t if acc is None else acc + t
    mixed = _dot(acc.astype(BF16), wo_ref[...])
    x1 = x_ref[...] + g1_ref[0] * mixed
    x1_o[...] = x1
    h2 = _rms(x1) * n2_ref[...]
    h2 = h2 * (1.0 + sc2_ref[0]) + sh2_ref[0]
    h2_o[...] = h2.astype(BF16)
    h2p_o[...] = _pack_rows(h2[:, 0:HALF_D], h2[:, HALF_D:D_MODEL])
    lg_o[...] = lax.dot_general(rw_ref[...], h2, (((1,), (1,)), ((), ())), preferred_element_type=F32,
                                precision=HIGHEST)


def _merge(u, yf, yr, s5_d, glu_w, glu_b, yb, yc, yd, gate, wb, wo, x2d, g1, norm2_g, sc2, sh2, router_w,
           *, rows_per_mod):
    rows = x2d.shape[0]
    tm = 512
    assert rows % tm == 0 and rows_per_mod % tm == 0

    def mod_map(i):
        return ((i * tm) // rows_per_mod, 0, 0)

    row = lambda w: pl.BlockSpec((tm, w), lambda i: (i, 0))
    full = lambda *shape: pl.BlockSpec(shape, lambda i: (0,) * len(shape))
    mod = pl.BlockSpec((1, 1, D_MODEL), mod_map)
    return pl.pallas_call(
        _merge_kernel,
        grid=(rows // tm,),
        in_specs=[row(256), row(256), row(256), full(1, 256), full(256, 256), full(1, 256),
                  row(256), row(256), row(256), row(GATE_WIDTH),
                  full(4, 256, D_MODEL), full(D_MODEL, D_MODEL), row(D_MODEL),
                  mod, full(1, D_MODEL), mod, mod, full(N_EXPERTS, D_MODEL)],
        out_specs=[row(D_MODEL), row(D_MODEL), row(HALF_D), pl.BlockSpec((N_EXPERTS, tm), lambda i: (0, i))],
        out_shape=[jax.ShapeDtypeStruct((rows, D_MODEL), F32),
                   jax.ShapeDtypeStruct((rows, D_MODEL), BF16),
                   jax.ShapeDtypeStruct((rows, HALF_D), jnp.int32),
                   jax.ShapeDtypeStruct((N_EXPERTS, rows), F32)],
        compiler_params=_params("arbitrary"),
        name="merge",
    )(u, yf, yr, s5_d.reshape(1, 256).astype(F32), glu_w.astype(BF16), glu_b.reshape(1, 256).astype(F32),
      yb, yc, yd, gate, wb.astype(BF16), wo.astype(BF16), x2d, g1, norm2_g.reshape(1, D_MODEL), sc2, sh2,
      router_w.astype(F32).T)


def _router_kernel(lg_ref, b_ref, tri_ref, idx_ref, rank_ref, w_ref, cnt_ref, base_ref):
    tr = lg_ref.shape[1]
    gsz = N_EXPERTS // N_EXPERT_GROUPS
    sc = jax.nn.sigmoid(lg_ref[...])
    bi = sc + b_ref[...]
    e_iota = lax.broadcasted_iota(jnp.int32, (gsz, tr), 0).astype(F32)
    groups = [bi[g * gsz:(g + 1) * gsz] for g in range(N_EXPERT_GROUPS)]
    gs = []
    for bg in groups:
        m1 = bg.max(axis=0, keepdims=True)
        i1 = jnp.where(bg == m1, e_iota, float(gsz)).min(axis=0, keepdims=True)
        m2 = jnp.where(e_iota == i1, -jnp.inf, bg).max(axis=0, keepdims=True)
        gs.append(m1 + m2)
    v = []
    for g in range(N_EXPERT_GROUPS):
        rank = jnp.zeros((1, tr), F32)
        for g2 in range(N_EXPERT_GROUPS):
            if g2 == g:
                continue
            beats = (gs[g2] >= gs[g]) if g2 < g else (gs[g2] > gs[g])
            rank = rank + jnp.where(beats, 1.0, 0.0)
        v.append(jnp.where(rank < TOPK_GROUPS, groups[g], NEG_INF))
    flat = [e_iota + float(g * gsz) for g in range(N_EXPERT_GROUPS)]
    sel = [jnp.zeros((gsz, tr), F32) for _ in range(N_EXPERT_GROUPS)]
    picks = []
    for _ in range(TOP_K):
        m = v[0].max(axis=0, keepdims=True)
        for g in range(1, N_EXPERT_GROUPS):
            m = jnp.maximum(m, v[g].max(axis=0, keepdims=True))
        am = jnp.where(v[0] == m, flat[0], float(N_EXPERTS)).min(axis=0, keepdims=True)
        for g in range(1, N_EXPERT_GROUPS):
            am = jnp.minimum(am, jnp.where(v[g] == m, flat[g], float(N_EXPERTS)).min(axis=0, keepdims=True))
        hits = []
        for g in range(N_EXPERT_GROUPS):
            hit = flat[g] == am
            hits.append(hit)
            sel[g] = jnp.where(hit, 1.0, sel[g])
            v[g] = jnp.where(hit, -jnp.inf, v[g])
        picks.append((am, hits))
    scg = [sc[g * gsz:(g + 1) * gsz] for g in range(N_EXPERT_GROUPS)]
    den = (sel[0] * scg[0]).sum(axis=0, keepdims=True)
    for g in range(1, N_EXPERT_GROUPS):
        den = den + (sel[g] * scg[g]).sum(axis=0, keepdims=True)

    @pl.when(pl.program_id(0) == 0)
    def _():
        base_ref[...] = jnp.zeros_like(base_ref)

    sel_all = jnp.concatenate(sel, axis=0)
    before = _dot(sel_all.astype(jnp.bfloat16), tri_ref[...]) + base_ref[...]
    for k, (am, hits) in enumerate(picks):
        wk = jnp.zeros((1, tr), F32)
        rk = jnp.zeros((1, tr), F32)
        for g in range(N_EXPERT_GROUPS):
            wk = wk + jnp.where(hits[g], scg[g], 0.0).sum(axis=0, keepdims=True)
            rk = rk + jnp.where(hits[g], before[g * gsz:(g + 1) * gsz], 0.0).sum(axis=0, keepdims=True)
        idx_ref[k:k + 1, :] = am.astype(jnp.int32)
        rank_ref[k:k + 1, :] = rk.astype(jnp.int32)
        w_ref[k:k + 1, :] = wk / den * ROUTED_SCALE
    idx_ref[TOP_K:8, :] = jnp.zeros((8 - TOP_K, tr), jnp.int32)
    rank_ref[TOP_K:8, :] = jnp.zeros((8 - TOP_K, tr), jnp.int32)
    w_ref[TOP_K:8, :] = jnp.zeros((8 - TOP_K, tr), F32)
    base_ref[...] += sel_all.sum(axis=1, keepdims=True)
    cnt_ref[...] = base_ref[...].astype(jnp.int32)


ROUTER_TILE = 512


def _router(logits_t, router_b):
    ne, rows = logits_t.shape
    tr = ROUTER_TILE
    assert rows % tr == 0
    tri = jnp.asarray(np.triu(np.ones((tr, tr), np.float32), k=1), jnp.bfloat16)
    pick = pl.BlockSpec((8, tr), lambda i: (0, i))
    return pl.pallas_call(
        _router_kernel,
        grid=(rows // tr,),
        in_specs=[pl.BlockSpec((ne, tr), lambda i: (0, i)),
                  pl.BlockSpec((ne, 1), lambda i: (0, 0)),
                  pl.BlockSpec((tr, tr), lambda i: (0, 0))],
        out_specs=[pick, pick, pick, pl.BlockSpec((ne, 1), lambda i: (0, 0))],
        out_shape=[jax.ShapeDtypeStruct((8, rows), jnp.int32),
                   jax.ShapeDtypeStruct((8, rows), jnp.int32),
                   jax.ShapeDtypeStruct((8, rows), F32),
                   jax.ShapeDtypeStruct((ne, 1), jnp.int32)],
        scratch_shapes=[pltpu.VMEM((ne, 1), F32)],
        compiler_params=_params("arbitrary"),
        name="router",
    )(logits_t, router_b.reshape(ne, 1).astype(F32), tri)


MOE_BLOCK = 512
MOE_TOKENS = 256
HALF_D = D_MODEL // 2


def _pack_rows(lo, hi):
    lo_b = pltpu.bitcast(lo.astype(jnp.bfloat16).astype(F32), jnp.uint32)
    hi_b = pltpu.bitcast(hi.astype(jnp.bfloat16).astype(F32), jnp.uint32)
    return pltpu.bitcast((hi_b & jnp.uint32(0xFFFF0000)) | (lo_b >> 16), jnp.int32)


def _unpack_rows(words):
    u = pltpu.bitcast(words, jnp.uint32)
    lo = pltpu.bitcast(u << 16, F32)
    hi = pltpu.bitcast(u & jnp.uint32(0xFFFF0000), F32)
    return lo, hi


def _swiglu(x_bf16, wgu, wd):
    hgu = _dot(x_bf16, wgu)
    g = hgu[:, 0:EXPERT_HIDDEN]
    a = g * jax.nn.sigmoid(g) * hgu[:, EXPERT_HIDDEN:2 * EXPERT_HIDDEN]
    return _dot(a.astype(BF16), wd)


SC_CORES = 2
SC_SUBCORES = 16
SC_STREAM_ROWS = 128


def _sc_for_each_chunk(total, body):
    chunk = SC_STREAM_ROWS
    assert total % chunk == 0
    nchunk = total // chunk
    per_worker = pl.cdiv(nchunk, SC_CORES * SC_SUBCORES)
    first = (lax.axis_index("s") * SC_CORES + lax.axis_index("c")) * per_worker

    @pl.loop(0, per_worker)
    def _(j):
        @pl.when(first + j < nchunk)
        def _():
            body((first + j) * chunk)


def _sc_scatter_rows(rows, slots, n_out):
    total, n = rows.shape
    chunk = SC_STREAM_ROWS
    mesh = plsc.VectorSubcoreMesh(core_axis_name="c", subcore_axis_name="s")

    @functools.partial(
        pl.kernel, mesh=mesh,
        out_type=jax.ShapeDtypeStruct((n_out, n), jnp.int32),
        scratch_types=[pltpu.VMEM((8, chunk), jnp.int32),
                       pltpu.VMEM((chunk, n), jnp.int32),
                       pltpu.SemaphoreType.DMA],
        name="moe_dispatch_sc",
    )
    def scatter(rows_hbm, slot_hbm, out_hbm, idx_v, rows_v, sem):
        def body(off):
            pltpu.sync_copy(rows_hbm.at[pl.ds(off, chunk)], rows_v)
            pltpu.sync_copy(slot_hbm.at[:, pl.ds(off, chunk)], idx_v)
            for k in range(TOP_K):
                pltpu.async_copy(rows_v, out_hbm.at[idx_v.at[k]], sem).wait()

        _sc_for_each_chunk(total, body)

    return scatter(rows, slots)


def _experts_kernel(be_ref, nv_ref, nb_ref, xs_ref, wg_ref, wu_ref, wd_ref, ys_ref, wgu_bf, wd_bf):
    b = pl.program_id(0)

    @pl.when(b < nb_ref[0])
    def _():
        @pl.when((b == 0) | (be_ref[b] != be_ref[jnp.maximum(b - 1, 0)]))
        def _():
            wgu_bf[:, 0:EXPERT_HIDDEN] = wg_ref[0].astype(BF16)
            wgu_bf[:, EXPERT_HIDDEN:2 * EXPERT_HIDDEN] = wu_ref[0].astype(BF16)
            wd_bf[...] = wd_ref[0].astype(BF16)

        lo, hi = _unpack_rows(xs_ref[...])
        live = lax.broadcasted_iota(jnp.int32, lo.shape, 0) < nv_ref[b]
        lo = jnp.where(live, lo, 0.0)
        hi = jnp.where(live, hi, 0.0)
        x = jnp.concatenate([lo, hi], axis=1).astype(BF16)
        y = _swiglu(x, wgu_bf[...], wd_bf[...])
        ys_ref[...] = _pack_rows(y[:, 0:HALF_D], y[:, HALF_D:D_MODEL])


def _sc_gather_rows(table, indices):
    m, n = indices.shape[0], table.shape[1]
    chunk = SC_STREAM_ROWS
    mesh = plsc.VectorSubcoreMesh(core_axis_name="c", subcore_axis_name="s")

    @functools.partial(
        pl.kernel, mesh=mesh,
        out_type=jax.ShapeDtypeStruct((m, n), jnp.int32),
        scratch_types=[pltpu.VMEM((chunk,), jnp.int32),
                       pltpu.VMEM((chunk, n), jnp.int32),
                       pltpu.SemaphoreType.DMA],
        name="moe_gather_sc",
    )
    def gather(table_hbm, idx_hbm, out_hbm, idx_v, rows_v, sem):
        def body(off):
            pltpu.sync_copy(idx_hbm.at[pl.ds(off, chunk)], idx_v)
            pltpu.async_copy(table_hbm.at[idx_v], rows_v, sem).wait()
            pltpu.sync_copy(rows_v, out_hbm.at[pl.ds(off, chunk)])

        _sc_for_each_chunk(m, body)

    return gather(table, indices)


def _combine_kernel(w_ref, rows_ref, h_ref, wsgu_ref, wsd_ref, x1_ref, g2_ref, fg_ref, o_ref, *, final):
    shared = _swiglu(h_ref[...], wsgu_ref[...], wsd_ref[...])
    acc_lo = shared[:, 0:HALF_D]
    acc_hi = shared[:, HALF_D:D_MODEL]
    w = w_ref[...]
    for k in range(TOP_K):
        lo, hi = _unpack_rows(rows_ref[k])
        acc_lo = acc_lo + w[:, k:k + 1] * lo
        acc_hi = acc_hi + w[:, k:k + 1] * hi
    x2 = x1_ref[...] + g2_ref[0] * jnp.concatenate([acc_lo, acc_hi], axis=1)
    if final:
        x2 = _rms(x2) * fg_ref[...]
    o_ref[...] = x2


def _moe(h2, h2p, picks, lp, x1, g2, final_g, *, rows_per_mod, final):
    idx, rank, wsel, counts = picks
    rows = h2.shape[0]
    tt = MOE_TOKENS
    blk = MOE_BLOCK
    assert rows % tt == 0 and rows_per_mod % tt == 0 and (rows * TOP_K) % blk == 0
    ntile = rows // tt
    nblock = rows * TOP_K // blk + N_EXPERTS

    cnt = counts.reshape(N_EXPERTS)
    padded = (cnt + blk - 1) // blk * blk
    pends = jnp.cumsum(padded)
    pstart = (pends - padded).astype(jnp.int32)
    nb_used = (pends[-1] // blk).astype(jnp.int32).reshape(1)
    first_row = jnp.arange(nblock, dtype=jnp.int32) * blk
    block_e = jnp.minimum(jnp.sum(pends[None, :] <= first_row[:, None], axis=1), N_EXPERTS - 1).astype(jnp.int32)
    slot = rank + jnp.sum(jnp.where(idx[:, :, None] == jnp.arange(N_EXPERTS, dtype=jnp.int32), pstart, 0), axis=-1)
    n_valid = jnp.clip(cnt[block_e] + pstart[block_e] - first_row, 0, blk).astype(jnp.int32)

    xs = _sc_scatter_rows(h2p, slot, nblock * blk)

    def blk_map(b, be, nv, nb):
        return (jnp.minimum(b, nb[0] - 1), 0)

    def w_map(b, be, nv, nb):
        return (be[jnp.minimum(b, nb[0] - 1)], 0, 0)

    ys = pl.pallas_call(
        _experts_kernel,
        grid_spec=pltpu.PrefetchScalarGridSpec(
            num_scalar_prefetch=3,
            grid=(nblock,),
            in_specs=[pl.BlockSpec((blk, HALF_D), blk_map),
                      pl.BlockSpec((1, D_MODEL, EXPERT_HIDDEN), w_map),
                      pl.BlockSpec((1, D_MODEL, EXPERT_HIDDEN), w_map),
                      pl.BlockSpec((1, EXPERT_HIDDEN, D_MODEL), w_map)],
            out_specs=pl.BlockSpec((blk, HALF_D), blk_map),
            scratch_shapes=[pltpu.VMEM((D_MODEL, 2 * EXPERT_HIDDEN), BF16),
                            pltpu.VMEM((EXPERT_HIDDEN, D_MODEL), BF16)]),
        out_shape=jax.ShapeDtypeStruct((nblock * blk, HALF_D), jnp.int32),
        compiler_params=_params("arbitrary"),
        name="moe_experts",
    )(block_e, n_valid, nb_used, xs,
      lp['exp_w_gate'].astype(F32), lp['exp_w_up'].astype(F32), lp['exp_w_down'].astype(F32))

    wsgu = jnp.concatenate([lp['sh_w_gate'], lp['sh_w_up']], axis=1).astype(BF16)
    wsd = lp['sh_w_down'].astype(BF16)
    gathered = _sc_gather_rows(ys, slot[0:TOP_K].reshape(TOP_K * rows))
    gathered = gathered.reshape(TOP_K, rows, HALF_D)
    full = lambda *shape: pl.BlockSpec(shape, lambda i: (0,) * len(shape))
    row = lambda width: pl.BlockSpec((tt, width), lambda i: (i, 0))
    return pl.pallas_call(
        functools.partial(_combine_kernel, final=final),
        grid=(ntile,),
        in_specs=[row(8), pl.BlockSpec((TOP_K, tt, HALF_D), lambda i: (0, i, 0)), row(D_MODEL),
                  full(D_MODEL, 2 * EXPERT_HIDDEN), full(EXPERT_HIDDEN, D_MODEL), row(D_MODEL),
                  pl.BlockSpec((1, 1, D_MODEL), lambda i: ((i * tt) // rows_per_mod, 0, 0)),
                  full(1, D_MODEL)],
        out_specs=row(D_MODEL),
        out_shape=jax.ShapeDtypeStruct((rows, D_MODEL), F32),
        compiler_params=_params("arbitrary"),
        name="moe_combine",
    )(wsel.T, gathered, h2, wsgu, wsd, x1, g2, final_g.reshape(1, D_MODEL).astype(F32))


def _reorder_w_in(w_in):
    split = 256 + 768 + 512 + 768
    return jnp.concatenate([w_in[:, split:], w_in[:, :split]], axis=1).astype(BF16)


def _mods(mod_row_block):
    return [mod_row_block[:, None, k * D_MODEL:(k + 1) * D_MODEL] for k in range(6)]


def _moe_block(h2, h2p, logits, lp, x1, g2, final_g, *, rows_per_mod, final):
    picks = _router(logits, lp['router_b'])
    return _moe(h2, h2p, picks, lp, x1, g2, final_g, rows_per_mod=rows_per_mod, final=final)


def _layer(x2d, xc2d, c16, lp, layer_idx, tables, final_g, *, nbatch, seq, n_ctx, with_ctx_out, final):
    lambda_init = 0.8 - 0.6 * math.exp(-0.3 * layer_idx)
    mod = _ada_mod(c16, lp['ada_w'].astype(F32), lp['ada_b'].astype(F32))
    sh1, sc1, g1, sh2, sc2, g2 = _mods(mod[0:nbatch])
    csh1, csc1, cg1, csh2, csc2, cg2 = _mods(mod[nbatch:nbatch + 1])
    w_in = _reorder_w_in(lp['w_in'])
    rows_lat = nbatch * seq
    rows_ctx = nbatch * n_ctx

    gate, u, na, sw, df = _inproj(x2d, lp['norm1_g'], sc1, sh1, w_in, tables,
                                  rows_per_mod=seq, rope=True, seq=seq)
    gate_c, u_c, na_c, sw_c, df_c = _inproj(xc2d, lp['norm1_g'], csc1, csh1, w_in, tables,
                                            rows_per_mod=rows_ctx, rope=False, seq=seq)

    win, wout, a_re, a_im = _s5_params(lp['s5_lambda_re'], lp['s5_lambda_im'], lp['s5_log_step'],
                                       lp['s5_b_re'], lp['s5_b_im'], lp['s5_c_re'], lp['s5_c_im'])
    u_tm = jnp.concatenate([u_c.reshape(nbatch, n_ctx, 256).transpose(1, 0, 2),
                            u.reshape(nbatch, seq, 256).transpose(1, 0, 2)], axis=0)
    y_tm = _s5_scan(u_tm, win, wout, a_re, a_im, n_ctx)
    y_bm = y_tm.transpose(0, 2, 1, 3)
    yf = y_bm[0, :, n_ctx:].reshape(rows_lat, 256)
    yr = y_bm[1, :, n_ctx:].reshape(rows_lat, 256)

    lqk = [lp[k].reshape(1, DIFF_QK_DIM).astype(F32) for k in ('diff_lq1', 'diff_lk1', 'diff_lq2', 'diff_lk2')]
    bias = _na_bias_table(lp['na_rpb'], seq // GRID_W)
    yb = _na_attention(na, na_c, bias, nbatch, seq, n_ctx)
    yc = _swa_attention(sw, sw_c, lp['swa_sink'], nbatch, seq, n_ctx)
    yd = _diff_attention(df, df_c, lqk, lp['diff_subln_g'], lambda_init, nbatch, seq, n_ctx)

    merge_w = (lp['s5_d'], lp['s5_glu_w'], lp['s5_glu_b'])
    x1, h2, h2p, logits = _merge(u, yf, yr, *merge_w, yb, yc, yd, gate, lp['w_branch'], lp['w_out'], x2d, g1,
                                 lp['norm2_g'], sc2, sh2, lp['router_w'], rows_per_mod=seq)
    x_out = _moe_block(h2, h2p, logits, lp, x1, g2, final_g, rows_per_mod=seq, final=final)

    xc_out = None
    if with_ctx_out:
        yf_c = y_bm[0, :, :n_ctx].reshape(rows_ctx, 256)
        yr_c = y_bm[1, :, :n_ctx].reshape(rows_ctx, 256)
        yb_c, yc_c, yd_c = _ctx_attention(na_c, sw_c, df_c, lp['swa_sink'], lqk, lp['diff_subln_g'],
                                          lambda_init, nbatch, n_ctx)
        x1c, h2c, h2pc, logits_c = _merge(u_c, yf_c, yr_c, *merge_w, yb_c, yc_c, yd_c, gate_c, lp['w_branch'],
                                          lp['w_out'], xc2d, cg1, lp['norm2_g'], csc2, csh2, lp['router_w'],
                                          rows_per_mod=rows_ctx)
        xc_out = _moe_block(h2c, h2pc, logits_c, lp, x1c, cg2, final_g, rows_per_mod=rows_ctx, final=False)
    return x_out, xc_out


def kernel(x, c, ctx, c_ctx, ada_w, ada_b, norm1_g, norm2_g, w_in, s5_lambda_re, s5_lambda_im, s5_log_step,
           s5_b_re, s5_b_im, s5_c_re, s5_c_im, s5_d, s5_glu_w, s5_glu_b, na_rpb, swa_sink, diff_lq1, diff_lk1,
           diff_lq2, diff_lk2, diff_subln_g, w_branch, w_out, router_w, router_b, exp_w_gate, exp_w_up,
           exp_w_down, sh_w_gate, sh_w_up, sh_w_down, final_g):
    nbatch, seq, d = x.shape
    n_ctx = ctx.shape[1]
    depth = ada_w.shape[0]
    assert d == D_MODEL and nbatch == 8
    stacked = dict(ada_w=ada_w, ada_b=ada_b, norm1_g=norm1_g, norm2_g=norm2_g, w_in=w_in,
                   s5_lambda_re=s5_lambda_re, s5_lambda_im=s5_lambda_im, s5_log_step=s5_log_step,
                   s5_b_re=s5_b_re, s5_b_im=s5_b_im, s5_c_re=s5_c_re, s5_c_im=s5_c_im, s5_d=s5_d,
                   s5_glu_w=s5_glu_w, s5_glu_b=s5_glu_b, na_rpb=na_rpb, swa_sink=swa_sink,
                   diff_lq1=diff_lq1, diff_lk1=diff_lk1, diff_lq2=diff_lq2, diff_lk2=diff_lk2,
                   diff_subln_g=diff_subln_g, w_branch=w_branch, w_out=w_out, router_w=router_w,
                   router_b=router_b, exp_w_gate=exp_w_gate, exp_w_up=exp_w_up, exp_w_down=exp_w_down,
                   sh_w_gate=sh_w_gate, sh_w_up=sh_w_up, sh_w_down=sh_w_down)
    tables = _rope_tables(seq)
    c16 = jnp.concatenate([c.astype(F32), c_ctx.reshape(1, d).astype(F32),
                           jnp.zeros((16 - nbatch - 1, d), F32)], axis=0)
    x2d = x.reshape(nbatch * seq, d).astype(F32)
    xc2d = ctx.reshape(nbatch * n_ctx, d).astype(F32)
    for l in range(depth):
        lp = {k: v[l] for k, v in stacked.items()}
        last = l == depth - 1
        x2d, xc2d = _layer(x2d, xc2d, c16, lp, l, tables, final_g, nbatch=nbatch, seq=seq, n_ctx=n_ctx,
                           with_ctx_out=not last, final=last)
    return x2d.reshape(nbatch, seq, d)
```

```python
import functools
import math

import numpy as np
import jax
import jax.numpy as jnp
from jax import lax
from jax.experimental import pallas as pl
from jax.experimental.pallas import tpu as pltpu
from jax.experimental.pallas import tpu_sc as plsc

F32 = jnp.float32
BF16 = jnp.bfloat16
HIGHEST = lax.Precision.HIGHEST

GRID_W = 64
EPS = 1e-6
NEG_INF = -1e30
ROPE_BASE = 10000.0
D_MODEL = 1024
BRANCH_WIDTH = 256
HEAD_DIM = 64
S5_GROUP = 16
S5_GROUPS = 16
S5_STATE = 64
S5_FLAT = S5_GROUPS * S5_STATE
NA_HEADS = 4
NA_WIN_ROWS = 8
NA_WIN_COLS = 16
SWA_KV_HEADS = 2
SWA_WINDOW = 128
DIFF_HEADS = 4
DIFF_QK_DIM = 32
N_EXPERTS = 64
N_EXPERT_GROUPS = 8
TOPK_GROUPS = 4
TOP_K = 6
EXPERT_HIDDEN = 256
ROUTED_SCALE = 2.5
GATE_WIDTH = 4 * D_MODEL

VMEM_LIMIT = 56 * 1024 * 1024


def _params(*sem):
    return pltpu.CompilerParams(dimension_semantics=sem, vmem_limit_bytes=VMEM_LIMIT)


def _nt_dot(a, b):
    return lax.dot_general(a, b, (((1,), (1,)), ((), ())), preferred_element_type=F32)


def _dot(a, b):
    return jnp.dot(a, b, preferred_element_type=F32)


def _rms(x):
    return x * lax.rsqrt(jnp.mean(x * x, axis=-1, keepdims=True) + EPS)


def _ada_kernel(c_ref, w_ref, b_ref, o_ref):
    c = c_ref[...]
    s = c * jax.nn.sigmoid(c)
    o_ref[...] = jnp.dot(s, w_ref[...], preferred_element_type=F32, precision=HIGHEST) + b_ref[...]


def _ada_mod(cc, w, b):
    rows, d = cc.shape
    width = w.shape[1]
    tn = 1536
    return pl.pallas_call(
        _ada_kernel,
        grid=(width // tn,),
        in_specs=[pl.BlockSpec((rows, d), lambda j: (0, 0)),
                  pl.BlockSpec((d, tn), lambda j: (0, j)),
                  pl.BlockSpec((1, tn), lambda j: (0, j))],
        out_specs=pl.BlockSpec((rows, tn), lambda j: (0, j)),
        out_shape=jax.ShapeDtypeStruct((rows, width), F32),
        compiler_params=_params("arbitrary"),
        name="ada_mod",
    )(cc, w, b.reshape(1, width))


_C_GATE = 0
_C_U = GATE_WIDTH
_C_NA = _C_U + 256
_C_SW = _C_NA + 768
_C_DF = _C_SW + 512
_C_END = _C_DF + 768


def _rope_apply(x, cos, sins, half):
    outs = []
    for j in range(x.shape[1] // 128):
        xs = x[:, j * 128:(j + 1) * 128]
        lane = lax.broadcasted_iota(jnp.int32, xs.shape, 1)
        lo = (lane % (2 * half)) < half
        partner = jnp.where(lo, pltpu.roll(xs, 128 - half, 1), pltpu.roll(xs, half, 1))
        outs.append(xs * cos + partner * sins)
    return outs[0] if len(outs) == 1 else jnp.concatenate(outs, axis=1)


def _inproj_kernel(x_ref, g_ref, sc_ref, sh_ref, w_ref, c64_ref, s64_ref, c32_ref, s32_ref,
                   gate_o, u_o, na_o, sw_o, df_o, *, rope):
    h = _rms(x_ref[...]) * g_ref[...]
    h = h * (1.0 + sc_ref[0]) + sh_ref[0]
    hb = h.astype(BF16)

    def mm(c0, c1):
        return _dot(hb, w_ref[:, c0:c1])

    for k in range(GATE_WIDTH // 512):
        gate_o[:, k * 512:(k + 1) * 512] = jax.nn.sigmoid(mm(k * 512, (k + 1) * 512)).astype(BF16)
    u_o[...] = mm(_C_U, _C_U + 256)

    na = mm(_C_NA, _C_NA + 768)
    na_o[:, 0:256] = (na[:, 0:256] * (HEAD_DIM ** -0.5 * (LOG2E if rope else 1.0))).astype(BF16)
    na_o[:, 256:768] = na[:, 256:768].astype(BF16)

    sw = mm(_C_SW, _C_SW + 512)
    swq, swk = sw[:, 0:256], sw[:, 256:384]
    if rope:
        swq = _rope_apply(swq, c64_ref[...], s64_ref[...], 16)
        swk = _rope_apply(swk, c64_ref[...], s64_ref[...], 16)
    sw_o[:, 0:256] = (swq * (HEAD_DIM ** -0.5 * (LOG2E if rope else 1.0))).astype(BF16)
    sw_o[:, 256:384] = swk.astype(BF16)
    sw_o[:, 384:512] = sw[:, 384:512].astype(BF16)

    df = mm(_C_DF, _C_DF + 768)
    dfq, dfk = df[:, 0:256], df[:, 256:512]
    if rope:
        dfq = _rope_apply(dfq, c32_ref[...], s32_ref[...], 8)
        dfk = _rope_apply(dfk, c32_ref[...], s32_ref[...], 8)
    df_o[:, 0:256] = (dfq * (DIFF_QK_DIM ** -0.5 * (LOG2E if rope else 1.0))).astype(BF16)
    df_o[:, 256:512] = dfk.astype(BF16)
    df_o[:, 512:768] = df[:, 512:768].astype(BF16)


def _inproj(x2d, norm_g, sc, sh, w_bf16, tables, *, rows_per_mod, rope, seq):
    rows = x2d.shape[0]
    tm = 512
    assert rows % tm == 0 and rows_per_mod % tm == 0 and seq % tm == 0
    tiles_per_seq = seq // tm

    def mod_map(i):
        return ((i * tm) // rows_per_mod, 0, 0)

    def tab_map(i):
        return (i % tiles_per_seq, 0)

    tab_spec = pl.BlockSpec((tm, 128), tab_map)
    row = lambda w: pl.BlockSpec((tm, w), lambda i: (i, 0))
    return pl.pallas_call(
        functools.partial(_inproj_kernel, rope=rope),
        grid=(rows // tm,),
        in_specs=[row(D_MODEL),
                  pl.BlockSpec((1, D_MODEL), lambda i: (0, 0)),
                  pl.BlockSpec((1, 1, D_MODEL), mod_map),
                  pl.BlockSpec((1, 1, D_MODEL), mod_map),
                  pl.BlockSpec((D_MODEL, _C_END), lambda i: (0, 0)),
                  tab_spec, tab_spec, tab_spec, tab_spec],
        out_specs=[row(GATE_WIDTH), row(256), row(768), row(512), row(768)],
        out_shape=[jax.ShapeDtypeStruct((rows, GATE_WIDTH), BF16),
                   jax.ShapeDtypeStruct((rows, 256), F32),
                   jax.ShapeDtypeStruct((rows, 768), BF16),
                   jax.ShapeDtypeStruct((rows, 512), BF16),
                   jax.ShapeDtypeStruct((rows, 768), BF16)],
        compiler_params=_params("arbitrary"),
        name="inproj",
    )(x2d, norm_g.reshape(1, D_MODEL), sc, sh, w_bf16, *tables)


def _rope_tables(seq):
    t = jnp.arange(seq)
    rows = (t // GRID_W).astype(F32)
    cols = (t % GRID_W).astype(F32)
    lane = np.arange(128)
    out = []
    for dim in (64, 32):
        quarter = dim // 4
        inv_freq = ROPE_BASE ** (-jnp.arange(quarter, dtype=F32) / quarter)
        l = lane % dim
        use_col = l >= dim // 2
        fidx = l % quarter
        hi = (l % (dim // 2)) >= quarter
        ang_r = rows[:, None] * inv_freq[None, :]
        ang_c = cols[:, None] * inv_freq[None, :]
        ang = jnp.where(use_col[None, :], ang_c[:, fidx], ang_r[:, fidx])
        out.append(jnp.cos(ang))
        out.append(jnp.where(hi[None, :], jnp.sin(ang), -jnp.sin(ang)))
    return tuple(out)


S5_CHUNK = 128


def _s5_kernel(u_ref, win_ref, wout_ref, are_ref, aim_ref, y_ref, bu_ref, st_ref, *, tc, nb):
    d = pl.program_id(0)
    i = pl.program_id(1)

    @pl.when(i == 0)
    def _():
        st_ref[...] = jnp.zeros_like(st_ref)

    u = u_ref[...].reshape(tc * nb, BRANCH_WIDTH).astype(BF16)
    bu_ref[...] = _dot(u, win_ref[0])
    ar = jnp.broadcast_to(are_ref[0], (nb, S5_FLAT))
    ai = jnp.broadcast_to(aim_ref[0], (nb, S5_FLAT))

    def body(j, carry):
        xr, xi = carry
        t = j + d * (tc - 1 - 2 * j)
        row = pl.multiple_of(t * nb, nb)
        br = bu_ref[pl.ds(row, nb), 0:S5_FLAT]
        bi = bu_ref[pl.ds(row, nb), S5_FLAT:2 * S5_FLAT]
        nr = ar * xr - ai * xi + br
        ni = ar * xi + ai * xr + bi
        bu_ref[pl.ds(row, nb), 0:S5_FLAT] = nr
        bu_ref[pl.ds(row, nb), S5_FLAT:2 * S5_FLAT] = ni
        return nr, ni

    xr, xi = lax.fori_loop(0, tc, body, (st_ref[:, 0:S5_FLAT], st_ref[:, S5_FLAT:2 * S5_FLAT]), unroll=4)
    st_ref[:, 0:S5_FLAT] = xr
    st_ref[:, S5_FLAT:2 * S5_FLAT] = xi
    y = _dot(bu_ref[...].astype(BF16), wout_ref[0])
    y_ref[0] = y.reshape(tc, nb, BRANCH_WIDTH)


def _s5_scan(u_tm, win, wout, a_re, a_im, n_ctx):
    s_len, nb, _ = u_tm.shape
    tc = S5_CHUNK
    assert nb == 8 and s_len % tc == 0 and n_ctx % tc == 0
    nct = n_ctx // tc
    nlt = (s_len - n_ctx) // tc

    def chunk(d, i):
        rev = jnp.where(i < nct, nct - 1 - i, 2 * nct + nlt - 1 - i)
        return jnp.where(d == 0, i, rev)

    def out_chunk(d, i):
        c = chunk(d, i)
        return jnp.where(c < nct, nlt + c, c - nct)

    return pl.pallas_call(
        functools.partial(_s5_kernel, tc=tc, nb=nb),
        grid=(2, nct + nlt),
        in_specs=[pl.BlockSpec((tc, nb, BRANCH_WIDTH), lambda d, i: (chunk(d, i), 0, 0)),
                  pl.BlockSpec((1, BRANCH_WIDTH, 2 * S5_FLAT), lambda d, i: (d, 0, 0)),
                  pl.BlockSpec((1, 2 * S5_FLAT, BRANCH_WIDTH), lambda d, i: (d, 0, 0)),
                  pl.BlockSpec((1, 1, S5_FLAT), lambda d, i: (d, 0, 0)),
                  pl.BlockSpec((1, 1, S5_FLAT), lambda d, i: (d, 0, 0))],
        out_specs=pl.BlockSpec((1, tc, nb, BRANCH_WIDTH), lambda d, i: (d, out_chunk(d, i), 0, 0)),
        out_shape=jax.ShapeDtypeStruct((2, s_len, nb, BRANCH_WIDTH), F32),
        scratch_shapes=[pltpu.VMEM((tc * nb, 2 * S5_FLAT), F32),
                        pltpu.VMEM((nb, 2 * S5_FLAT), F32)],
        compiler_params=_params("arbitrary", "arbitrary"),
        name="s5_scan",
    )(u_tm, win, wout, a_re, a_im)


def _s5_params(lam_re, lam_im, log_step, b_re, b_im, c_re, c_im):
    lr = lam_re.astype(F32)
    li = lam_im.astype(F32)
    dt = jnp.exp(log_step.astype(F32))[..., None]
    mag = jnp.exp(lr * dt)
    a_re = mag * jnp.cos(li * dt)
    a_im = mag * jnp.sin(li * dt)
    nr, ni, den = a_re - 1.0, a_im, lr * lr + li * li
    k_re = ((nr * lr + ni * li) / den)[..., None]
    k_im = ((ni * lr - nr * li) / den)[..., None]
    br = b_re.astype(F32)
    bi = b_im.astype(F32)
    bb_re = k_re * br - k_im * bi
    bb_im = k_re * bi + k_im * br
    eye = jnp.eye(S5_GROUPS, dtype=F32)

    def blockdiag_in(bb):
        m = jnp.einsum('dgpc,gh->dgchp', bb, eye)
        return m.reshape(2, S5_GROUPS * S5_GROUP, S5_GROUPS * S5_STATE)

    def blockdiag_out(cc):
        m = jnp.einsum('dgcp,gh->dgphc', cc, eye)
        return m.reshape(2, S5_GROUPS * S5_STATE, S5_GROUPS * S5_GROUP)

    win = jnp.concatenate([blockdiag_in(bb_re), blockdiag_in(bb_im)], axis=2).astype(BF16)
    wout = jnp.concatenate([blockdiag_out(c_re.astype(F32)), -blockdiag_out(c_im.astype(F32))], axis=1).astype(BF16)
    return win, wout, a_re.reshape(2, 1, S5_FLAT), a_im.reshape(2, 1, S5_FLAT)


def _softmax_parts(scores, extra=None):
    m = scores[0].max(axis=-1, keepdims=True)
    for s in scores[1:]:
        m = jnp.maximum(m, s.max(axis=-1, keepdims=True))
    if extra is not None:
        m = jnp.maximum(m, extra)
    ps = [jnp.exp(s - m) for s in scores]
    l = ps[0].sum(axis=-1, keepdims=True)
    for p in ps[1:]:
        l = l + p.sum(axis=-1, keepdims=True)
    if extra is not None:
        l = l + jnp.exp(extra - m)
    return ps, l


NA_QROWS = 2
NA_KROWS = 10
VROWS = HEAD_DIM + 16


def _na_window_start(r, grid_rows):
    start = jnp.clip(r - NA_WIN_ROWS // 2, 0, grid_rows - NA_WIN_ROWS)
    return (jnp.minimum(start, grid_rows - NA_KROWS) // 2) * 2


def _head_blockdiag(qt, qbd_ref, nheads, rows_per_head):
    n = qt.shape[1]
    row_h = lax.broadcasted_iota(jnp.int32, qt.shape, 0) // rows_per_head
    zero = jnp.zeros_like(qt)
    for h in range(nheads):
        qbd_ref[:, h * n:(h + 1) * n] = jnp.where(row_h == h, qt, zero)


def _na_kernel(qt_ref, k_ref, vt_ref, kc_ref, vct_ref, bias_ref, o_ref, qbd_ref, *, grid_rows):
    nq = NA_QROWS * GRID_W
    nk = NA_KROWS * GRID_W
    off = pl.multiple_of(_na_window_start(NA_QROWS * pl.program_id(1), grid_rows) * GRID_W, 128)
    _head_blockdiag(qt_ref[0], qbd_ref, NA_HEADS, HEAD_DIM)
    qbd = qbd_ref[...]
    s_loc = _dot(k_ref[0, pl.ds(off, nk), :], qbd) + bias_ref[0]
    s_ctx = _dot(kc_ref[0], qbd)
    m = jnp.maximum(s_loc.max(axis=0, keepdims=True), s_ctx.max(axis=0, keepdims=True))
    p_loc = jnp.exp2((s_loc - m).astype(BF16))
    p_ctx = jnp.exp2((s_ctx - m).astype(BF16))
    vw = vt_ref[0, :, pl.ds(off, nk)]
    outs = []
    for h in range(NA_HEADS):
        rows = slice(h * VROWS, (h + 1) * VROWS)
        cols = slice(h * nq, (h + 1) * nq)
        o = _dot(vw[rows], p_loc[:, cols]) + _dot(vct_ref[0, rows, :], p_ctx[:, cols])
        outs.append(o[0:HEAD_DIM] / o[HEAD_DIM:HEAD_DIM + 1])
    o_ref[...] = jnp.concatenate(outs, axis=0).T.astype(BF16)


def _na_classes(grid_rows):
    return [0, 2, 4, grid_rows - 4, grid_rows - 2]


def _na_bias_table(rpb, grid_rows):
    col = np.arange(GRID_W)
    cstart = np.clip(col - NA_WIN_COLS // 2, 0, GRID_W - NA_WIN_COLS)
    col_in = (col[None, :] >= cstart[:, None]) & (col[None, :] < cstart[:, None] + NA_WIN_COLS)
    cb = np.clip(col[None, :] - col[:, None] + (NA_WIN_COLS - 1), 0, 2 * NA_WIN_COLS - 2)
    classes = _na_classes(grid_rows)
    rbi = np.zeros((len(classes), NA_QROWS, NA_KROWS), np.int64)
    row_in = np.zeros((len(classes), NA_QROWS, NA_KROWS), bool)
    for c, r in enumerate(classes):
        a_row = (min(int(np.clip(r - NA_WIN_ROWS // 2, 0, grid_rows - NA_WIN_ROWS)), grid_rows - NA_KROWS) // 2) * 2
        for qi in range(NA_QROWS):
            start_q = int(np.clip(r + qi - NA_WIN_ROWS // 2, 0, grid_rows - NA_WIN_ROWS))
            for j in range(NA_KROWS):
                row_in[c, qi, j] = start_q <= a_row + j < start_q + NA_WIN_ROWS
                rbi[c, qi, j] = np.clip(a_row + j - (r + qi) + NA_WIN_ROWS - 1, 0, 2 * NA_WIN_ROWS - 2)
    oh_row = jnp.asarray(rbi[..., None] == np.arange(2 * NA_WIN_ROWS - 1), F32)
    oh_col = jnp.asarray(cb[:, :, None] == np.arange(2 * NA_WIN_COLS - 1), F32)
    t = jnp.einsum('hab,cija,qkb->cjkhiq', rpb.astype(F32), oh_row, oh_col, precision=HIGHEST) * LOG2E
    valid = row_in.transpose(0, 2, 1)[:, :, None, None, :, None] & col_in.T[None, None, :, None, None, :]
    t = jnp.where(valid, t, NEG_INF)
    return t.reshape(len(classes), NA_KROWS * GRID_W, NA_HEADS * NA_QROWS * GRID_W)


def _augment_vt(v, nheads):
    nb, s, _ = v.shape
    v = v.reshape(nb, s, nheads, HEAD_DIM)
    pad = jnp.zeros((nb, s, nheads, VROWS - HEAD_DIM), v.dtype).at[..., 0].set(1.0)
    return jnp.concatenate([v, pad], axis=-1).reshape(nb, s, nheads * VROWS).transpose(0, 2, 1)


def _na_attention(na, na_c, bias, nbatch, seq, n_ctx):
    grid_rows = seq // GRID_W
    assert grid_rows >= NA_KROWS and grid_rows % NA_QROWS == 0
    nq = NA_QROWS * GRID_W
    npair = grid_rows // NA_QROWS
    nal = na.reshape(nbatch, seq, 768)
    nac = na_c.reshape(nbatch, n_ctx, 768)
    qt = nal[:, :, 0:256].transpose(0, 2, 1)
    vt = _augment_vt(nal[:, :, 512:768], NA_HEADS)
    vct = _augment_vt(nac[:, :, 512:768], NA_HEADS)

    def cls(b, p):
        r = NA_QROWS * p
        c = jnp.where(r < 4, r // 2, jnp.where(r >= grid_rows - 4, (r - (grid_rows - 4)) // 2 + 3, 2))
        return (c, 0, 0)

    return pl.pallas_call(
        functools.partial(_na_kernel, grid_rows=grid_rows),
        grid=(nbatch, npair),
        in_specs=[pl.BlockSpec((1, 256, nq), lambda b, p: (b, 0, p)),
                  pl.BlockSpec((1, seq, 256), lambda b, p: (b, 0, 1)),
                  pl.BlockSpec((1, NA_HEADS * VROWS, seq), lambda b, p: (b, 0, 0)),
                  pl.BlockSpec((1, n_ctx, 256), lambda b, p: (b, 0, 1)),
                  pl.BlockSpec((1, NA_HEADS * VROWS, n_ctx), lambda b, p: (b, 0, 0)),
                  pl.BlockSpec((1, NA_KROWS * GRID_W, NA_HEADS * nq), cls)],
        out_specs=pl.BlockSpec((nq, 256), lambda b, p: (b * npair + p, 0)),
        out_shape=jax.ShapeDtypeStruct((nbatch * seq, 256), BF16),
        scratch_shapes=[pltpu.VMEM((256, NA_HEADS * nq), BF16)],
        compiler_params=_params("arbitrary", "arbitrary"),
        name="na_attention",
    )(qt, nal, vt, nac, vct, bias)


SWA_BLOCK = 128


def _swa_kernel(qt_ref, k_ref, vt_ref, kc_ref, vct_ref, sink_ref, o_ref, qbd_ref, *, seq):
    n = pl.program_id(1)
    band = 3 * SWA_BLOCK
    nq = SWA_BLOCK
    nqh = 2 * SWA_KV_HEADS
    bstart = pl.multiple_of(jnp.clip((n - 1) * SWA_BLOCK, 0, seq - band), SWA_BLOCK)
    qt = qt_ref[0]
    zero = jnp.zeros((HEAD_DIM, nq), qt.dtype)
    for hq in range(nqh):
        qh = qt[hq * HEAD_DIM:(hq + 1) * HEAD_DIM]
        qbd_ref[:, hq * nq:(hq + 1) * nq] = jnp.concatenate([qh, zero] if hq // 2 == 0 else [zero, qh], axis=0)
    qbd = qbd_ref[...]
    kpos = bstart + lax.broadcasted_iota(jnp.int32, (band, nqh * nq), 0)
    qpos = n * SWA_BLOCK + lax.broadcasted_iota(jnp.int32, (band, nqh * nq), 1) % nq
    s_loc = jnp.where(jnp.abs(qpos - kpos) <= SWA_WINDOW, _dot(k_ref[0, pl.ds(bstart, band), :], qbd), NEG_INF)
    s_ctx = _dot(kc_ref[0], qbd)
    sink = jnp.concatenate([jnp.broadcast_to(sink_ref[0:1, hq:hq + 1] * LOG2E, (1, nq)) for hq in range(nqh)], axis=1)
    m = jnp.maximum(jnp.maximum(s_loc.max(axis=0, keepdims=True), s_ctx.max(axis=0, keepdims=True)), sink)
    p_loc = jnp.exp2((s_loc - m).astype(BF16))
    p_ctx = jnp.exp2((s_ctx - m).astype(BF16))
    p_sink = jnp.exp2(sink - m)
    vw = vt_ref[0, :, pl.ds(bstart, band)]
    outs = []
    for hq in range(nqh):
        rows = slice((hq // 2) * VROWS, (hq // 2 + 1) * VROWS)
        cols = slice(hq * nq, (hq + 1) * nq)
        o = _dot(vw[rows], p_loc[:, cols]) + _dot(vct_ref[0, rows, :], p_ctx[:, cols])
        outs.append(o[0:HEAD_DIM] / (o[HEAD_DIM:HEAD_DIM + 1] + p_sink[:, cols]))
    o_ref[...] = jnp.concatenate(outs, axis=0).T.astype(BF16)


def _swa_attention(sw, sw_c, sink, nbatch, seq, n_ctx):
    nblk = seq // SWA_BLOCK
    nqh = 2 * SWA_KV_HEADS
    assert seq >= 3 * SWA_BLOCK
    sink_pad = jnp.zeros((1, 128), F32).at[0, 0:nqh].set(sink.astype(F32))
    swl = sw.reshape(nbatch, seq, 512)
    swc = sw_c.reshape(nbatch, n_ctx, 512)
    qt = swl[:, :, 0:256].transpose(0, 2, 1)
    vt = _augment_vt(swl[:, :, 384:512], SWA_KV_HEADS)
    vct = _augment_vt(swc[:, :, 384:512], SWA_KV_HEADS)
    return pl.pallas_call(
        functools.partial(_swa_kernel, seq=seq),
        grid=(nbatch, nblk),
        in_specs=[pl.BlockSpec((1, 256, SWA_BLOCK), lambda b, n: (b, 0, n)),
                  pl.BlockSpec((1, seq, 128), lambda b, n: (b, 0, 2)),
                  pl.BlockSpec((1, SWA_KV_HEADS * VROWS, seq), lambda b, n: (b, 0, 0)),
                  pl.BlockSpec((1, n_ctx, 128), lambda b, n: (b, 0, 2)),
                  pl.BlockSpec((1, SWA_KV_HEADS * VROWS, n_ctx), lambda b, n: (b, 0, 0)),
                  pl.BlockSpec((1, 128), lambda b, n: (0, 0))],
        out_specs=pl.BlockSpec((SWA_BLOCK, 256), lambda b, n: (b * nblk + n, 0)),
        out_shape=jax.ShapeDtypeStruct((nbatch * seq, 256), BF16),
        scratch_shapes=[pltpu.VMEM((SWA_KV_HEADS * HEAD_DIM, nqh * SWA_BLOCK), BF16)],
        compiler_params=_params("arbitrary", "arbitrary"),
        name="swa_attention",
    )(qt, swl, vt, swc, vct, sink_pad)


DIFF_TQ = 512
DIFF_CK = 256
LOG2E = math.log2(math.e)
DIFF_VROWS = HEAD_DIM + 16


def _diff_lambda(lq1_ref, lk1_ref, lq2_ref, lk2_ref, lambda_init):
    s1 = jnp.sum(lq1_ref[...] * lk1_ref[...], axis=-1, keepdims=True)
    s2 = jnp.sum(lq2_ref[...] * lk2_ref[...], axis=-1, keepdims=True)
    return jnp.exp(s1) - jnp.exp(s2) + lambda_init


def _stack_maps(qh):
    lane = lax.broadcasted_iota(jnp.int32, qh.shape, 1)
    zero = jnp.zeros_like(qh)
    return jnp.concatenate([jnp.where(lane < DIFF_QK_DIM, qh, zero),
                            jnp.where(lane >= DIFF_QK_DIM, qh, zero)], axis=0)


def _subln(o0, o1, lam, g, lambda_init):
    o = o0 - lam * o1
    return _rms(o) * g * (1.0 - lambda_init)


def _diff_kernel(qt_ref, k_ref, vt_ref, lq1_ref, lk1_ref, lq2_ref, lk2_ref, g_ref, o_ref, qbd_ref, acc_ref, s_ref,
                 *, nchunk, lambda_init):
    lam = _diff_lambda(lq1_ref, lk1_ref, lq2_ref, lk2_ref, lambda_init)
    qt = qt_ref[0]
    tq = qt.shape[1]
    w = 2 * tq
    row = lax.broadcasted_iota(jnp.int32, qt.shape, 0) // DIFF_QK_DIM
    zero = jnp.zeros_like(qt)
    for j in range(2 * DIFF_HEADS):
        qbd_ref[:, j * tq:(j + 1) * tq] = jnp.where(row == j, qt, zero)
    acc_ref[...] = jnp.zeros_like(acc_ref)

    def scores(slot, c, h):
        s = _dot(k_ref[0, c], qbd_ref[:, h * w:(h + 1) * w])
        s_ref[slot, h] = s
        return s.max(axis=0, keepdims=True)

    def softmax_pv(slot, c, h, m_run, m_chunk):
        m_new = jnp.maximum(m_run, m_chunk)
        alpha = jnp.exp2(m_run - m_new)
        p = jnp.exp2((s_ref[slot, h] - m_new).astype(BF16))
        acc_ref[h] = alpha * acc_ref[h] + _dot(vt_ref[0, c, h * DIFF_VROWS:(h + 1) * DIFF_VROWS, :], p)
        return m_new

    def step(slot, c, carry):
        m_run, m_chunk = carry
        new_run, new_chunk = [], []
        for h in range(DIFF_HEADS):
            new_chunk.append(scores(1 - slot, c + 1, h))
            new_run.append(softmax_pv(slot, c, h, m_run[h], m_chunk[h]))
        return tuple(new_run), tuple(new_chunk)

    def body(i, carry):
        c = 2 * i
        return step(1, c + 1, step(0, c, carry))

    m_run = tuple(jnp.full((1, w), NEG_INF, F32) for _ in range(DIFF_HEADS))
    m_chunk = tuple(scores(0, 0, h) for h in range(DIFF_HEADS))
    npair = (nchunk - 1) // 2
    m_run, m_chunk = lax.fori_loop(0, npair, body, (m_run, m_chunk))
    if nchunk % 2 == 0:
        m_run, m_chunk = step(0, nchunk - 2, (m_run, m_chunk))
        for h in range(DIFF_HEADS):
            softmax_pv(1, nchunk - 1, h, m_run[h], m_chunk[h])
    else:
        for h in range(DIFF_HEADS):
            softmax_pv(0, nchunk - 1, h, m_run[h], m_chunk[h])
    outs = []
    for h in range(DIFF_HEADS):
        o = acc_ref[h, 0:HEAD_DIM, :] / acc_ref[h, HEAD_DIM:HEAD_DIM + 1, :]
        d = o[:, 0:tq] - lam * o[:, tq:w]
        d = d * lax.rsqrt(jnp.mean(d * d, axis=0, keepdims=True) + EPS)
        outs.append(d * g_ref[...] * (1.0 - lambda_init))
    o_ref[...] = jnp.concatenate(outs, axis=0).T.astype(BF16)


def _diff_attention(df, df_c, lqk, subln_g, lambda_init, nbatch, seq, n_ctx):
    tq, ck = min(DIFF_TQ, seq), DIFF_CK
    s_all = seq + n_ctx
    assert s_all % ck == 0 and seq % tq == 0
    nchunk = s_all // ck
    nq = seq // tq
    dfl = df.reshape(nbatch, seq, 768)
    dfc = df_c.reshape(nbatch, n_ctx, 768)
    qt = dfl[:, :, 0:256].transpose(0, 2, 1)
    k_all = jnp.concatenate([dfl[:, :, 256:512], dfc[:, :, 256:512]], axis=1).reshape(nbatch, nchunk, ck, 256)
    v_all = jnp.concatenate([dfl[:, :, 512:768], dfc[:, :, 512:768]], axis=1)
    v_all = v_all.reshape(nbatch, nchunk, ck, DIFF_HEADS, HEAD_DIM)
    pad = jnp.zeros((nbatch, nchunk, ck, DIFF_HEADS, DIFF_VROWS - HEAD_DIM), BF16).at[..., 0].set(1.0)
    vt_all = jnp.concatenate([v_all, pad], axis=-1).reshape(nbatch, nchunk, ck, DIFF_HEADS * DIFF_VROWS)
    vt_all = vt_all.transpose(0, 1, 3, 2)
    vec = pl.BlockSpec((1, DIFF_QK_DIM), lambda b, n: (0, 0))
    return pl.pallas_call(
        functools.partial(_diff_kernel, nchunk=nchunk, lambda_init=lambda_init),
        grid=(nbatch, nq),
        in_specs=[pl.BlockSpec((1, 256, tq), lambda b, n: (b, 0, n)),
                  pl.BlockSpec((1, nchunk, ck, 256), lambda b, n: (b, 0, 0, 0)),
                  pl.BlockSpec((1, nchunk, DIFF_HEADS * DIFF_VROWS, ck), lambda b, n: (b, 0, 0, 0)),
                  vec, vec, vec, vec,
                  pl.BlockSpec((HEAD_DIM, 1), lambda b, n: (0, 0))],
        out_specs=pl.BlockSpec((tq, 256), lambda b, n: (b * nq + n, 0)),
        out_shape=jax.ShapeDtypeStruct((nbatch * seq, 256), BF16),
        scratch_shapes=[pltpu.VMEM((256, 2 * DIFF_HEADS * tq), BF16),
                        pltpu.VMEM((DIFF_HEADS, DIFF_VROWS, 2 * tq), F32),
                        pltpu.VMEM((2, DIFF_HEADS, ck, 2 * tq), F32)],
        compiler_params=_params("arbitrary", "arbitrary"),
        name="diff_attention",
    )(qt, k_all, vt_all, *lqk, subln_g.reshape(HEAD_DIM, 1).astype(F32))


def _ctx_attn_kernel(na_ref, sw_ref, df_ref, sink_ref, lq1_ref, lk1_ref, lq2_ref, lk2_ref, g_ref,
                     nb_o, sw_o, df_o, *, lambda_init):
    n = na_ref.shape[0]
    na = na_ref[...]
    outs = []
    for h in range(NA_HEADS):
        sl = slice(h * HEAD_DIM, (h + 1) * HEAD_DIM)
        (p,), l = _softmax_parts([_nt_dot(na[:, sl], na[:, 256 + h * HEAD_DIM:256 + (h + 1) * HEAD_DIM])])
        outs.append(_dot(p.astype(BF16), na[:, 512 + h * HEAD_DIM:512 + (h + 1) * HEAD_DIM]) / l)
    nb_o[...] = jnp.concatenate(outs, axis=1).astype(BF16)
    sw = sw_ref[...]
    outs = []
    for hq in range(4):
        kv = hq // 2
        k = sw[:, 256 + kv * HEAD_DIM:256 + (kv + 1) * HEAD_DIM]
        v = sw[:, 384 + kv * HEAD_DIM:384 + (kv + 1) * HEAD_DIM]
        sk = jnp.broadcast_to(sink_ref[0:1, hq:hq + 1], (n, 1))
        (p,), l = _softmax_parts([_nt_dot(sw[:, hq * HEAD_DIM:(hq + 1) * HEAD_DIM], k)], extra=sk)
        outs.append(_dot(p.astype(BF16), v) / l)
    sw_o[...] = jnp.concatenate(outs, axis=1).astype(BF16)
    lam = _diff_lambda(lq1_ref, lk1_ref, lq2_ref, lk2_ref, lambda_init)
    df = df_ref[...]
    outs = []
    for h in range(DIFF_HEADS):
        sl = slice(h * HEAD_DIM, (h + 1) * HEAD_DIM)
        q2 = _stack_maps(df[:, sl])
        (p,), l = _softmax_parts([_nt_dot(q2, df[:, 256 + h * HEAD_DIM:256 + (h + 1) * HEAD_DIM])])
        o = _dot(p.astype(BF16), df[:, 512 + h * HEAD_DIM:512 + (h + 1) * HEAD_DIM]) / l
        outs.append(_subln(o[0:n], o[n:2 * n], lam, g_ref[...], lambda_init))
    df_o[...] = jnp.concatenate(outs, axis=1).astype(BF16)


def _ctx_attention(na_c, sw_c, df_c, sink, lqk, subln_g, lambda_init, nbatch, n_ctx):
    sink_pad = jnp.zeros((1, 128), F32).at[0, 0:4].set(sink.astype(F32))
    vec = pl.BlockSpec((1, DIFF_QK_DIM), lambda b: (0, 0))
    out = jax.ShapeDtypeStruct((nbatch * n_ctx, 256), BF16)
    return pl.pallas_call(
        functools.partial(_ctx_attn_kernel, lambda_init=lambda_init),
        grid=(nbatch,),
        in_specs=[pl.BlockSpec((n_ctx, 768), lambda b: (b, 0)),
                  pl.BlockSpec((n_ctx, 512), lambda b: (b, 0)),
                  pl.BlockSpec((n_ctx, 768), lambda b: (b, 0)),
                  pl.BlockSpec((1, 128), lambda b: (0, 0)),
                  vec, vec, vec, vec,
                  pl.BlockSpec((1, HEAD_DIM), lambda b: (0, 0))],
        out_specs=[pl.BlockSpec((n_ctx, 256), lambda b: (b, 0))] * 3,
        out_shape=[out, out, out],
        compiler_params=_params("arbitrary"),
        name="ctx_attention",
    )(na_c, sw_c, df_c, sink_pad, *lqk, subln_g.reshape(1, HEAD_DIM).astype(F32))


def _merge_kernel(u_ref, yf_ref, yr_ref, d_ref, gw_ref, gb_ref, yb_ref, yc_ref, yd_ref, gate_ref,
                  wb_ref, wo_ref, x_ref, g1_ref, n2_ref, sc2_ref, sh2_ref, rw_ref,
                  x1_o, h2_o, h2p_o, lg_o):
    y = u_ref[...] * d_ref[...] + yf_ref[0, 0] + yr_ref[0, 0]
    a = jax.nn.gelu(y, approximate=True)
    ya = a * jax.nn.sigmoid(_dot(a.astype(BF16), gw_ref[...]) + gb_ref[...])
    branches = (ya.astype(BF16), yb_ref[...], yc_ref[...], yd_ref[...])
    acc = None
    for i in range(4):
        t = gate_ref[:, i * D_MODEL:(i + 1) * D_MODEL].astype(F32) * _dot(branches[i], wb_ref[i])
        acc = t if acc is None else acc + t
    mixed = _dot(acc.astype(BF16), wo_ref[...])
    x1 = x_ref[...] + g1_ref[0] * mixed
    x1_o[...] = x1
    h2 = _rms(x1) * n2_ref[...]
    h2 = h2 * (1.0 + sc2_ref[0]) + sh2_ref[0]
    h2_o[...] = h2.astype(BF16)
    h2p_o[...] = _pack_rows(h2[:, 0:HALF_D], h2[:, HALF_D:D_MODEL])
    lg_o[...] = lax.dot_general(rw_ref[...], h2, (((1,), (1,)), ((), ())), preferred_element_type=F32,
                                precision=HIGHEST)


def _merge(u, y_bm, y_start, rows_per_seq, s5_d, glu_w, glu_b, yb, yc, yd, gate, wb, wo, x2d, g1, norm2_g, sc2, sh2,
           router_w, *, rows_per_mod):
    rows = x2d.shape[0]
    tm = min(512, rows_per_seq)
    assert rows % tm == 0 and rows_per_mod % tm == 0 and rows_per_seq % tm == 0 and y_start % tm == 0
    tiles_per_seq = rows_per_seq // tm

    def mod_map(i):
        return ((i * tm) // rows_per_mod, 0, 0)

    def y_spec(d):
        return pl.BlockSpec((1, 1, tm, 256), lambda i: (d, i // tiles_per_seq, y_start // tm + i % tiles_per_seq, 0))

    row = lambda w: pl.BlockSpec((tm, w), lambda i: (i, 0))
    full = lambda *shape: pl.BlockSpec(shape, lambda i: (0,) * len(shape))
    mod = pl.BlockSpec((1, 1, D_MODEL), mod_map)
    return pl.pallas_call(
        _merge_kernel,
        grid=(rows // tm,),
        in_specs=[row(256), y_spec(0), y_spec(1), full(1, 256), full(256, 256), full(1, 256),
                  row(256), row(256), row(256), row(GATE_WIDTH),
                  full(4, 256, D_MODEL), full(D_MODEL, D_MODEL), row(D_MODEL),
                  mod, full(1, D_MODEL), mod, mod, full(N_EXPERTS, D_MODEL)],
        out_specs=[row(D_MODEL), row(D_MODEL), row(HALF_D), pl.BlockSpec((N_EXPERTS, tm), lambda i: (0, i))],
        out_shape=[jax.ShapeDtypeStruct((rows, D_MODEL), F32),
                   jax.ShapeDtypeStruct((rows, D_MODEL), BF16),
                   jax.ShapeDtypeStruct((rows, HALF_D), jnp.int32),
                   jax.ShapeDtypeStruct((N_EXPERTS, rows), F32)],
        compiler_params=_params("arbitrary"),
        name="merge",
    )(u, y_bm, y_bm, s5_d.reshape(1, 256).astype(F32), glu_w.astype(BF16), glu_b.reshape(1, 256).astype(F32),
      yb, yc, yd, gate, wb.astype(BF16), wo.astype(BF16), x2d, g1, norm2_g.reshape(1, D_MODEL), sc2, sh2,
      router_w.astype(F32).T)


def _router_kernel(lg_ref, b_ref, tri_ref, idx_ref, rank_ref, w_ref, cnt_ref, base_ref):
    tr = lg_ref.shape[1]
    gsz = N_EXPERTS // N_EXPERT_GROUPS
    sc = jax.nn.sigmoid(lg_ref[...])
    bi = sc + b_ref[...]
    e_iota = lax.broadcasted_iota(jnp.int32, (gsz, tr), 0).astype(F32)
    groups = [bi[g * gsz:(g + 1) * gsz] for g in range(N_EXPERT_GROUPS)]
    gs = []
    for bg in groups:
        m1 = bg.max(axis=0, keepdims=True)
        i1 = jnp.where(bg == m1, e_iota, float(gsz)).min(axis=0, keepdims=True)
        m2 = jnp.where(e_iota == i1, -jnp.inf, bg).max(axis=0, keepdims=True)
        gs.append(m1 + m2)
    v = []
    for g in range(N_EXPERT_GROUPS):
        rank = jnp.zeros((1, tr), F32)
        for g2 in range(N_EXPERT_GROUPS):
            if g2 == g:
                continue
            beats = (gs[g2] >= gs[g]) if g2 < g else (gs[g2] > gs[g])
            rank = rank + jnp.where(beats, 1.0, 0.0)
        v.append(jnp.where(rank < TOPK_GROUPS, groups[g], NEG_INF))
    flat = [e_iota + float(g * gsz) for g in range(N_EXPERT_GROUPS)]
    sel = [jnp.zeros((gsz, tr), F32) for _ in range(N_EXPERT_GROUPS)]
    picks = []
    for _ in range(TOP_K):
        m = v[0].max(axis=0, keepdims=True)
        for g in range(1, N_EXPERT_GROUPS):
            m = jnp.maximum(m, v[g].max(axis=0, keepdims=True))
        am = jnp.where(v[0] == m, flat[0], float(N_EXPERTS)).min(axis=0, keepdims=True)
        for g in range(1, N_EXPERT_GROUPS):
            am = jnp.minimum(am, jnp.where(v[g] == m, flat[g], float(N_EXPERTS)).min(axis=0, keepdims=True))
        hits = []
        for g in range(N_EXPERT_GROUPS):
            hit = flat[g] == am
            hits.append(hit)
            sel[g] = jnp.where(hit, 1.0, sel[g])
            v[g] = jnp.where(hit, -jnp.inf, v[g])
        picks.append((am, hits))
    scg = [sc[g * gsz:(g + 1) * gsz] for g in range(N_EXPERT_GROUPS)]
    den = (sel[0] * scg[0]).sum(axis=0, keepdims=True)
    for g in range(1, N_EXPERT_GROUPS):
        den = den + (sel[g] * scg[g]).sum(axis=0, keepdims=True)

    @pl.when(pl.program_id(0) == 0)
    def _():
        base_ref[...] = jnp.zeros_like(base_ref)

    sel_all = jnp.concatenate(sel, axis=0)
    before = _dot(sel_all.astype(jnp.bfloat16), tri_ref[...]) + base_ref[...]
    for k, (am, hits) in enumerate(picks):
        wk = jnp.zeros((1, tr), F32)
        rk = jnp.zeros((1, tr), F32)
        for g in range(N_EXPERT_GROUPS):
            wk = wk + jnp.where(hits[g], scg[g], 0.0).sum(axis=0, keepdims=True)
            rk = rk + jnp.where(hits[g], before[g * gsz:(g + 1) * gsz], 0.0).sum(axis=0, keepdims=True)
        idx_ref[k:k + 1, :] = am.astype(jnp.int32)
        rank_ref[k:k + 1, :] = rk.astype(jnp.int32)
        w_ref[k:k + 1, :] = wk / den * ROUTED_SCALE
    idx_ref[TOP_K:8, :] = jnp.zeros((8 - TOP_K, tr), jnp.int32)
    rank_ref[TOP_K:8, :] = jnp.zeros((8 - TOP_K, tr), jnp.int32)
    w_ref[TOP_K:8, :] = jnp.zeros((8 - TOP_K, tr), F32)
    base_ref[...] += sel_all.sum(axis=1, keepdims=True)
    cnt_ref[...] = base_ref[...].astype(jnp.int32)


ROUTER_TILE = 512


def _router(logits_t, router_b):
    ne, rows = logits_t.shape
    tr = ROUTER_TILE
    assert rows % tr == 0
    tri = jnp.asarray(np.triu(np.ones((tr, tr), np.float32), k=1), jnp.bfloat16)
    pick = pl.BlockSpec((8, tr), lambda i: (0, i))
    return pl.pallas_call(
        _router_kernel,
        grid=(rows // tr,),
        in_specs=[pl.BlockSpec((ne, tr), lambda i: (0, i)),
                  pl.BlockSpec((ne, 1), lambda i: (0, 0)),
                  pl.BlockSpec((tr, tr), lambda i: (0, 0))],
        out_specs=[pick, pick, pick, pl.BlockSpec((ne, 1), lambda i: (0, 0))],
        out_shape=[jax.ShapeDtypeStruct((8, rows), jnp.int32),
                   jax.ShapeDtypeStruct((8, rows), jnp.int32),
                   jax.ShapeDtypeStruct((8, rows), F32),
                   jax.ShapeDtypeStruct((ne, 1), jnp.int32)],
        scratch_shapes=[pltpu.VMEM((ne, 1), F32)],
        compiler_params=_params("arbitrary"),
        name="router",
    )(logits_t, router_b.reshape(ne, 1).astype(F32), tri)


MOE_BLOCK = 512
MOE_TOKENS = 256
HALF_D = D_MODEL // 2


def _pack_rows(lo, hi):
    lo_b = pltpu.bitcast(lo.astype(jnp.bfloat16).astype(F32), jnp.uint32)
    hi_b = pltpu.bitcast(hi.astype(jnp.bfloat16).astype(F32), jnp.uint32)
    return pltpu.bitcast((hi_b & jnp.uint32(0xFFFF0000)) | (lo_b >> 16), jnp.int32)


def _unpack_rows(words):
    u = pltpu.bitcast(words, jnp.uint32)
    lo = pltpu.bitcast(u << 16, F32)
    hi = pltpu.bitcast(u & jnp.uint32(0xFFFF0000), F32)
    return lo, hi


def _swiglu(x_bf16, wgu, wd):
    hgu = _dot(x_bf16, wgu)
    g = hgu[:, 0:EXPERT_HIDDEN]
    a = g * jax.nn.sigmoid(g) * hgu[:, EXPERT_HIDDEN:2 * EXPERT_HIDDEN]
    return _dot(a.astype(BF16), wd)


SC_CORES = 2
SC_SUBCORES = 16
SC_STREAM_ROWS = 128


def _sc_for_each_chunk(total, body):
    chunk = SC_STREAM_ROWS
    assert total % chunk == 0
    nchunk = total // chunk
    per_worker = pl.cdiv(nchunk, SC_CORES * SC_SUBCORES)
    first = (lax.axis_index("s") * SC_CORES + lax.axis_index("c")) * per_worker

    @pl.loop(0, per_worker)
    def _(j):
        @pl.when(first + j < nchunk)
        def _():
            body((first + j) * chunk)


def _sc_scatter_rows(rows, slots, n_out):
    total, n = rows.shape
    chunk = SC_STREAM_ROWS
    mesh = plsc.VectorSubcoreMesh(core_axis_name="c", subcore_axis_name="s")

    @functools.partial(
        pl.kernel, mesh=mesh,
        out_type=jax.ShapeDtypeStruct((n_out, n), jnp.int32),
        scratch_types=[pltpu.VMEM((8, chunk), jnp.int32),
                       pltpu.VMEM((chunk, n), jnp.int32),
                       pltpu.SemaphoreType.DMA],
        name="moe_dispatch_sc",
    )
    def scatter(rows_hbm, slot_hbm, out_hbm, idx_v, rows_v, sem):
        def body(off):
            pltpu.sync_copy(rows_hbm.at[pl.ds(off, chunk)], rows_v)
            pltpu.sync_copy(slot_hbm.at[:, pl.ds(off, chunk)], idx_v)
            for k in range(TOP_K):
                pltpu.async_copy(rows_v, out_hbm.at[idx_v.at[k]], sem).wait()

        _sc_for_each_chunk(total, body)

    return scatter(rows, slots)


def _experts_kernel(be_ref, nv_ref, nb_ref, xs_ref, wg_ref, wu_ref, wd_ref, ys_ref, wgu_bf, wd_bf):
    b = pl.program_id(0)

    @pl.when(b < nb_ref[0])
    def _():
        @pl.when((b == 0) | (be_ref[b] != be_ref[jnp.maximum(b - 1, 0)]))
        def _():
            wgu_bf[:, 0:EXPERT_HIDDEN] = wg_ref[0, 0].astype(BF16)
            wgu_bf[:, EXPERT_HIDDEN:2 * EXPERT_HIDDEN] = wu_ref[0, 0].astype(BF16)
            wd_bf[...] = wd_ref[0, 0].astype(BF16)

        lo, hi = _unpack_rows(xs_ref[...])
        live = lax.broadcasted_iota(jnp.int32, lo.shape, 0) < nv_ref[b]
        lo = jnp.where(live, lo, 0.0)
        hi = jnp.where(live, hi, 0.0)
        x = jnp.concatenate([lo, hi], axis=1).astype(BF16)
        y = _swiglu(x, wgu_bf[...], wd_bf[...])
        ys_ref[...] = _pack_rows(y[:, 0:HALF_D], y[:, HALF_D:D_MODEL])


def _sc_gather_rows(table, indices):
    m, n = indices.shape[0], table.shape[1]
    chunk = SC_STREAM_ROWS
    mesh = plsc.VectorSubcoreMesh(core_axis_name="c", subcore_axis_name="s")

    @functools.partial(
        pl.kernel, mesh=mesh,
        out_type=jax.ShapeDtypeStruct((m, n), jnp.int32),
        scratch_types=[pltpu.VMEM((chunk,), jnp.int32),
                       pltpu.VMEM((chunk, n), jnp.int32),
                       pltpu.SemaphoreType.DMA],
        name="moe_gather_sc",
    )
    def gather(table_hbm, idx_hbm, out_hbm, idx_v, rows_v, sem):
        def body(off):
            pltpu.sync_copy(idx_hbm.at[pl.ds(off, chunk)], idx_v)
            pltpu.async_copy(table_hbm.at[idx_v], rows_v, sem).wait()
            pltpu.sync_copy(rows_v, out_hbm.at[pl.ds(off, chunk)])

        _sc_for_each_chunk(m, body)

    return gather(table, indices)


def _combine_kernel(w_ref, rows_ref, h_ref, wsgu_ref, wsd_ref, x1_ref, g2_ref, fg_ref, o_ref, *, final):
    shared = _swiglu(h_ref[...], wsgu_ref[...], wsd_ref[...])
    acc_lo = shared[:, 0:HALF_D]
    acc_hi = shared[:, HALF_D:D_MODEL]
    w = w_ref[...]
    for k in range(TOP_K):
        lo, hi = _unpack_rows(rows_ref[k])
        acc_lo = acc_lo + w[:, k:k + 1] * lo
        acc_hi = acc_hi + w[:, k:k + 1] * hi
    x2 = x1_ref[...] + g2_ref[0] * jnp.concatenate([acc_lo, acc_hi], axis=1)
    if final:
        x2 = _rms(x2) * fg_ref[...]
    o_ref[...] = x2


def _moe(h2, h2p, picks, lp, x1, g2, final_g, *, rows_per_mod, final):
    idx, rank, wsel, counts = picks
    rows = h2.shape[0]
    tt = MOE_TOKENS
    blk = MOE_BLOCK
    assert rows % tt == 0 and rows_per_mod % tt == 0 and (rows * TOP_K) % blk == 0
    ntile = rows // tt
    nblock = rows * TOP_K // blk + N_EXPERTS

    cnt = counts.reshape(N_EXPERTS)
    padded = (cnt + blk - 1) // blk * blk
    e_ids = jnp.arange(N_EXPERTS, dtype=jnp.int32)
    pends = jnp.sum(jnp.where(e_ids[None, :] <= e_ids[:, None], padded[None, :], 0), axis=1)
    pstart = (pends - padded).astype(jnp.int32)
    nb_used = (jnp.sum(padded) // blk).astype(jnp.int32).reshape(1)
    first_row = jnp.arange(nblock, dtype=jnp.int32) * blk
    block_e = jnp.minimum(jnp.sum(pends[None, :] <= first_row[:, None], axis=1), N_EXPERTS - 1).astype(jnp.int32)
    slot = rank + jnp.sum(jnp.where(idx[:, :, None] == jnp.arange(N_EXPERTS, dtype=jnp.int32), pstart, 0), axis=-1)
    seg_end = jnp.sum(jnp.where(block_e[:, None] == e_ids[None, :], (cnt + pstart)[None, :], 0), axis=1)
    n_valid = jnp.clip(seg_end - first_row, 0, blk).astype(jnp.int32)

    xs = _sc_scatter_rows(h2p, slot, nblock * blk)

    def blk_map(b, be, nv, nb):
        return (jnp.minimum(b, nb[0] - 1), 0)

    layer = lp['layer']

    def w_map(b, be, nv, nb):
        return (layer, be[jnp.minimum(b, nb[0] - 1)], 0, 0)

    ys = pl.pallas_call(
        _experts_kernel,
        grid_spec=pltpu.PrefetchScalarGridSpec(
            num_scalar_prefetch=3,
            grid=(nblock,),
            in_specs=[pl.BlockSpec((blk, HALF_D), blk_map),
                      pl.BlockSpec((1, 1, D_MODEL, EXPERT_HIDDEN), w_map),
                      pl.BlockSpec((1, 1, D_MODEL, EXPERT_HIDDEN), w_map),
                      pl.BlockSpec((1, 1, EXPERT_HIDDEN, D_MODEL), w_map)],
            out_specs=pl.BlockSpec((blk, HALF_D), blk_map),
            scratch_shapes=[pltpu.VMEM((D_MODEL, 2 * EXPERT_HIDDEN), BF16),
                            pltpu.VMEM((EXPERT_HIDDEN, D_MODEL), BF16)]),
        out_shape=jax.ShapeDtypeStruct((nblock * blk, HALF_D), jnp.int32),
        compiler_params=_params("arbitrary"),
        name="moe_experts",
    )(block_e, n_valid, nb_used, xs, lp['exp_w_gate'], lp['exp_w_up'], lp['exp_w_down'])

    wsgu = jnp.concatenate([lp['sh_w_gate'], lp['sh_w_up']], axis=1).astype(BF16)
    wsd = lp['sh_w_down'].astype(BF16)
    gathered = _sc_gather_rows(ys, slot[0:TOP_K].reshape(TOP_K * rows))
    gathered = gathered.reshape(TOP_K, rows, HALF_D)
    full = lambda *shape: pl.BlockSpec(shape, lambda i: (0,) * len(shape))
    row = lambda width: pl.BlockSpec((tt, width), lambda i: (i, 0))
    return pl.pallas_call(
        functools.partial(_combine_kernel, final=final),
        grid=(ntile,),
        in_specs=[row(8), pl.BlockSpec((TOP_K, tt, HALF_D), lambda i: (0, i, 0)), row(D_MODEL),
                  full(D_MODEL, 2 * EXPERT_HIDDEN), full(EXPERT_HIDDEN, D_MODEL), row(D_MODEL),
                  pl.BlockSpec((1, 1, D_MODEL), lambda i: ((i * tt) // rows_per_mod, 0, 0)),
                  full(1, D_MODEL)],
        out_specs=row(D_MODEL),
        out_shape=jax.ShapeDtypeStruct((rows, D_MODEL), F32),
        compiler_params=_params("arbitrary"),
        name="moe_combine",
    )(wsel.T, gathered, h2, wsgu, wsd, x1, g2, final_g.reshape(1, D_MODEL).astype(F32))


def _reorder_w_in(w_in):
    split = 256 + 768 + 512 + 768
    return jnp.concatenate([w_in[:, split:], w_in[:, :split]], axis=1).astype(BF16)


def _mods(mod_row_block):
    return [mod_row_block[:, None, k * D_MODEL:(k + 1) * D_MODEL] for k in range(6)]


def _moe_block(h2, h2p, logits, lp, x1, g2, final_g, *, rows_per_mod, final):
    picks = _router(logits, lp['router_b'])
    return _moe(h2, h2p, picks, lp, x1, g2, final_g, rows_per_mod=rows_per_mod, final=final)


def _layer(x2d, xc2d, c16, lp, layer_idx, tables, final_g, *, nbatch, seq, n_ctx, with_ctx_out, final):
    lambda_init = 0.8 - 0.6 * math.exp(-0.3 * layer_idx)
    mod = _ada_mod(c16, lp['ada_w'].astype(F32), lp['ada_b'].astype(F32))
    sh1, sc1, g1, sh2, sc2, g2 = _mods(mod[0:nbatch])
    csh1, csc1, cg1, csh2, csc2, cg2 = _mods(mod[nbatch:nbatch + 1])
    w_in = _reorder_w_in(lp['w_in'])
    rows_lat = nbatch * seq
    rows_ctx = nbatch * n_ctx

    gate, u, na, sw, df = _inproj(x2d, lp['norm1_g'], sc1, sh1, w_in, tables,
                                  rows_per_mod=seq, rope=True, seq=seq)
    gate_c, u_c, na_c, sw_c, df_c = _inproj(xc2d, lp['norm1_g'], csc1, csh1, w_in, tables,
                                            rows_per_mod=rows_ctx, rope=False, seq=seq)

    win, wout, a_re, a_im = _s5_params(lp['s5_lambda_re'], lp['s5_lambda_im'], lp['s5_log_step'],
                                       lp['s5_b_re'], lp['s5_b_im'], lp['s5_c_re'], lp['s5_c_im'])
    u_tm = jnp.concatenate([u_c.reshape(nbatch, n_ctx, 256).transpose(1, 0, 2),
                            u.reshape(nbatch, seq, 256).transpose(1, 0, 2)], axis=0)
    y_tm = _s5_scan(u_tm, win, wout, a_re, a_im, n_ctx)
    y_bm = y_tm.transpose(0, 2, 1, 3)

    lqk = [lp[k].reshape(1, DIFF_QK_DIM).astype(F32) for k in ('diff_lq1', 'diff_lk1', 'diff_lq2', 'diff_lk2')]
    bias = _na_bias_table(lp['na_rpb'], seq // GRID_W)
    yb = _na_attention(na, na_c, bias, nbatch, seq, n_ctx)
    yc = _swa_attention(sw, sw_c, lp['swa_sink'], nbatch, seq, n_ctx)
    yd = _diff_attention(df, df_c, lqk, lp['diff_subln_g'], lambda_init, nbatch, seq, n_ctx)

    merge_w = (lp['s5_d'], lp['s5_glu_w'], lp['s5_glu_b'])
    x1, h2, h2p, logits = _merge(u, y_bm, 0, seq, *merge_w, yb, yc, yd, gate, lp['w_branch'], lp['w_out'], x2d, g1,
                                 lp['norm2_g'], sc2, sh2, lp['router_w'], rows_per_mod=seq)
    x_out = _moe_block(h2, h2p, logits, lp, x1, g2, final_g, rows_per_mod=seq, final=final)

    xc_out = None
    if with_ctx_out:
        yb_c, yc_c, yd_c = _ctx_attention(na_c, sw_c, df_c, lp['swa_sink'], lqk, lp['diff_subln_g'],
                                          lambda_init, nbatch, n_ctx)
        x1c, h2c, h2pc, logits_c = _merge(u_c, y_bm, seq, n_ctx, *merge_w, yb_c, yc_c, yd_c, gate_c,
                                          lp['w_branch'], lp['w_out'], xc2d, cg1, lp['norm2_g'], csc2, csh2,
                                          lp['router_w'], rows_per_mod=rows_ctx)
        xc_out = _moe_block(h2c, h2pc, logits_c, lp, x1c, cg2, final_g, rows_per_mod=rows_ctx, final=False)
    return x_out, xc_out


def kernel(x, c, ctx, c_ctx, ada_w, ada_b, norm1_g, norm2_g, w_in, s5_lambda_re, s5_lambda_im, s5_log_step,
           s5_b_re, s5_b_im, s5_c_re, s5_c_im, s5_d, s5_glu_w, s5_glu_b, na_rpb, swa_sink, diff_lq1, diff_lk1,
           diff_lq2, diff_lk2, diff_subln_g, w_branch, w_out, router_w, router_b, exp_w_gate, exp_w_up,
           exp_w_down, sh_w_gate, sh_w_up, sh_w_down, final_g):
    nbatch, seq, d = x.shape
    n_ctx = ctx.shape[1]
    depth = ada_w.shape[0]
    assert d == D_MODEL and nbatch == 8
    stacked = dict(ada_w=ada_w, ada_b=ada_b, norm1_g=norm1_g, norm2_g=norm2_g, w_in=w_in,
                   s5_lambda_re=s5_lambda_re, s5_lambda_im=s5_lambda_im, s5_log_step=s5_log_step,
                   s5_b_re=s5_b_re, s5_b_im=s5_b_im, s5_c_re=s5_c_re, s5_c_im=s5_c_im, s5_d=s5_d,
                   s5_glu_w=s5_glu_w, s5_glu_b=s5_glu_b, na_rpb=na_rpb, swa_sink=swa_sink,
                   diff_lq1=diff_lq1, diff_lk1=diff_lk1, diff_lq2=diff_lq2, diff_lk2=diff_lk2,
                   diff_subln_g=diff_subln_g, w_branch=w_branch, w_out=w_out, router_w=router_w,
                   router_b=router_b, exp_w_gate=exp_w_gate, exp_w_up=exp_w_up, exp_w_down=exp_w_down,
                   sh_w_gate=sh_w_gate, sh_w_up=sh_w_up, sh_w_down=sh_w_down)
    tables = _rope_tables(seq)
    c16 = jnp.concatenate([c.astype(F32), c_ctx.reshape(1, d).astype(F32),
                           jnp.zeros((16 - nbatch - 1, d), F32)], axis=0)
    x2d = x.reshape(nbatch * seq, d).astype(F32)
    xc2d = ctx.reshape(nbatch * n_ctx, d).astype(F32)
    for l in range(depth):
        routed = ('exp_w_gate', 'exp_w_up', 'exp_w_down')
        lp = {k: (v.astype(F32) if k in routed else v[l]) for k, v in stacked.items()}
        lp['layer'] = l
        last = l == depth - 1
        x2d, xc2d = _layer(x2d, xc2d, c16, lp, l, tables, final_g, nbatch=nbatch, seq=seq, n_ctx=n_ctx,
                           with_ctx_out=not last, final=last)
    return x2d.reshape(nbatch, seq, d)
```

```python
import functools
import math

import numpy as np
import jax
import jax.numpy as jnp
from jax import lax
from jax.experimental import pallas as pl
from jax.experimental.pallas import tpu as pltpu
from jax.experimental.pallas import tpu_sc as plsc

F32 = jnp.float32
BF16 = jnp.bfloat16
HIGHEST = lax.Precision.HIGHEST

GRID_W = 64
EPS = 1e-6
NEG_INF = -1e30
ROPE_BASE = 10000.0
D_MODEL = 1024
BRANCH_WIDTH = 256
HEAD_DIM = 64
S5_GROUP = 16
S5_GROUPS = 16
S5_STATE = 64
S5_FLAT = S5_GROUPS * S5_STATE
NA_HEADS = 4
NA_WIN_ROWS = 8
NA_WIN_COLS = 16
SWA_KV_HEADS = 2
SWA_WINDOW = 128
DIFF_HEADS = 4
DIFF_QK_DIM = 32
N_EXPERTS = 64
N_EXPERT_GROUPS = 8
TOPK_GROUPS = 4
TOP_K = 6
EXPERT_HIDDEN = 256
ROUTED_SCALE = 2.5
GATE_WIDTH = 4 * D_MODEL

VMEM_LIMIT = 56 * 1024 * 1024


def _params(*sem):
    return pltpu.CompilerParams(dimension_semantics=sem, vmem_limit_bytes=VMEM_LIMIT)


def _nt_dot(a, b):
    return lax.dot_general(a, b, (((1,), (1,)), ((), ())), preferred_element_type=F32)


def _dot(a, b):
    return jnp.dot(a, b, preferred_element_type=F32)


def _rms(x):
    return x * lax.rsqrt(jnp.mean(x * x, axis=-1, keepdims=True) + EPS)


def _ada_kernel(c_ref, w_ref, b_ref, o_ref):
    c = c_ref[...]
    s = c * jax.nn.sigmoid(c)
    o_ref[...] = jnp.dot(s, w_ref[...], preferred_element_type=F32, precision=HIGHEST) + b_ref[...]


def _ada_mod(cc, w, b):
    rows, d = cc.shape
    width = w.shape[1]
    tn = 1536
    return pl.pallas_call(
        _ada_kernel,
        grid=(width // tn,),
        in_specs=[pl.BlockSpec((rows, d), lambda j: (0, 0)),
                  pl.BlockSpec((d, tn), lambda j: (0, j)),
                  pl.BlockSpec((1, tn), lambda j: (0, j))],
        out_specs=pl.BlockSpec((rows, tn), lambda j: (0, j)),
        out_shape=jax.ShapeDtypeStruct((rows, width), F32),
        compiler_params=_params("arbitrary"),
        name="ada_mod",
    )(cc, w, b.reshape(1, width))


_C_GATE = 0
_C_U = GATE_WIDTH
_C_NA = _C_U + 256
_C_SW = _C_NA + 768
_C_DF = _C_SW + 512
_C_END = _C_DF + 768


def _rope_apply(x, cos, sins, half):
    outs = []
    for j in range(x.shape[1] // 128):
        xs = x[:, j * 128:(j + 1) * 128]
        lane = lax.broadcasted_iota(jnp.int32, xs.shape, 1)
        lo = (lane % (2 * half)) < half
        partner = jnp.where(lo, pltpu.roll(xs, 128 - half, 1), pltpu.roll(xs, half, 1))
        outs.append(xs * cos + partner * sins)
    return outs[0] if len(outs) == 1 else jnp.concatenate(outs, axis=1)


def _inproj_kernel(x_ref, g_ref, sc_ref, sh_ref, w_ref, c64_ref, s64_ref, c32_ref, s32_ref,
                   gate_o, u_o, na_o, sw_o, df_o, *, rope):
    h = _rms(x_ref[...]) * g_ref[...]
    h = h * (1.0 + sc_ref[0]) + sh_ref[0]
    hb = h.astype(BF16)

    def mm(c0, c1):
        return _dot(hb, w_ref[:, c0:c1])

    for k in range(GATE_WIDTH // 512):
        gate_o[:, k * 512:(k + 1) * 512] = jax.nn.sigmoid(mm(k * 512, (k + 1) * 512)).astype(BF16)
    u_o[...] = mm(_C_U, _C_U + 256)

    na = mm(_C_NA, _C_NA + 768)
    na_o[:, 0:256] = (na[:, 0:256] * (HEAD_DIM ** -0.5 * (LOG2E if rope else 1.0))).astype(BF16)
    na_o[:, 256:768] = na[:, 256:768].astype(BF16)

    sw = mm(_C_SW, _C_SW + 512)
    swq, swk = sw[:, 0:256], sw[:, 256:384]
    if rope:
        swq = _rope_apply(swq, c64_ref[...], s64_ref[...], 16)
        swk = _rope_apply(swk, c64_ref[...], s64_ref[...], 16)
    sw_o[:, 0:256] = (swq * (HEAD_DIM ** -0.5 * (LOG2E if rope else 1.0))).astype(BF16)
    sw_o[:, 256:384] = swk.astype(BF16)
    sw_o[:, 384:512] = sw[:, 384:512].astype(BF16)

    df = mm(_C_DF, _C_DF + 768)
    dfq, dfk = df[:, 0:256], df[:, 256:512]
    if rope:
        dfq = _rope_apply(dfq, c32_ref[...], s32_ref[...], 8)
        dfk = _rope_apply(dfk, c32_ref[...], s32_ref[...], 8)
    df_o[:, 0:256] = (dfq * (DIFF_QK_DIM ** -0.5 * (LOG2E if rope else 1.0))).astype(BF16)
    df_o[:, 256:512] = dfk.astype(BF16)
    df_o[:, 512:768] = df[:, 512:768].astype(BF16)


def _inproj(x2d, norm_g, sc, sh, w_bf16, tables, *, rows_per_mod, rope, seq):
    rows = x2d.shape[0]
    tm = 512
    assert rows % tm == 0 and rows_per_mod % tm == 0 and seq % tm == 0
    tiles_per_seq = seq // tm

    def mod_map(i):
        return ((i * tm) // rows_per_mod, 0, 0)

    def tab_map(i):
        return (i % tiles_per_seq, 0)

    tab_spec = pl.BlockSpec((tm, 128), tab_map)
    row = lambda w: pl.BlockSpec((tm, w), lambda i: (i, 0))
    return pl.pallas_call(
        functools.partial(_inproj_kernel, rope=rope),
        grid=(rows // tm,),
        in_specs=[row(D_MODEL),
                  pl.BlockSpec((1, D_MODEL), lambda i: (0, 0)),
                  pl.BlockSpec((1, 1, D_MODEL), mod_map),
                  pl.BlockSpec((1, 1, D_MODEL), mod_map),
                  pl.BlockSpec((D_MODEL, _C_END), lambda i: (0, 0)),
                  tab_spec, tab_spec, tab_spec, tab_spec],
        out_specs=[row(GATE_WIDTH), row(256), row(768), row(512), row(768)],
        out_shape=[jax.ShapeDtypeStruct((rows, GATE_WIDTH), BF16),
                   jax.ShapeDtypeStruct((rows, 256), F32),
                   jax.ShapeDtypeStruct((rows, 768), BF16),
                   jax.ShapeDtypeStruct((rows, 512), BF16),
                   jax.ShapeDtypeStruct((rows, 768), BF16)],
        compiler_params=_params("arbitrary"),
        name="inproj",
    )(x2d, norm_g.reshape(1, D_MODEL), sc, sh, w_bf16, *tables)


def _rope_tables(seq):
    t = jnp.arange(seq)
    rows = (t // GRID_W).astype(F32)
    cols = (t % GRID_W).astype(F32)
    lane = np.arange(128)
    out = []
    for dim in (64, 32):
        quarter = dim // 4
        inv_freq = ROPE_BASE ** (-jnp.arange(quarter, dtype=F32) / quarter)
        l = lane % dim
        use_col = l >= dim // 2
        fidx = l % quarter
        hi = (l % (dim // 2)) >= quarter
        ang_r = rows[:, None] * inv_freq[None, :]
        ang_c = cols[:, None] * inv_freq[None, :]
        ang = jnp.where(use_col[None, :], ang_c[:, fidx], ang_r[:, fidx])
        out.append(jnp.cos(ang))
        out.append(jnp.where(hi[None, :], jnp.sin(ang), -jnp.sin(ang)))
    return tuple(out)


S5_CHUNK = 128


def _s5_kernel(u_ref, win_ref, wout_ref, are_ref, aim_ref, y_ref, bu_ref, st_ref, *, tc, nb):
    d = pl.program_id(0)
    i = pl.program_id(1)

    @pl.when(i == 0)
    def _():
        st_ref[...] = jnp.zeros_like(st_ref)

    u = u_ref[...].reshape(tc * nb, BRANCH_WIDTH).astype(BF16)
    bu_ref[...] = _dot(u, win_ref[0])
    ar = jnp.broadcast_to(are_ref[0], (nb, S5_FLAT))
    ai = jnp.broadcast_to(aim_ref[0], (nb, S5_FLAT))

    def body(j, carry):
        xr, xi = carry
        t = j + d * (tc - 1 - 2 * j)
        row = pl.multiple_of(t * nb, nb)
        br = bu_ref[pl.ds(row, nb), 0:S5_FLAT]
        bi = bu_ref[pl.ds(row, nb), S5_FLAT:2 * S5_FLAT]
        nr = ar * xr - ai * xi + br
        ni = ar * xi + ai * xr + bi
        bu_ref[pl.ds(row, nb), 0:S5_FLAT] = nr
        bu_ref[pl.ds(row, nb), S5_FLAT:2 * S5_FLAT] = ni
        return nr, ni

    xr, xi = lax.fori_loop(0, tc, body, (st_ref[:, 0:S5_FLAT], st_ref[:, S5_FLAT:2 * S5_FLAT]), unroll=4)
    st_ref[:, 0:S5_FLAT] = xr
    st_ref[:, S5_FLAT:2 * S5_FLAT] = xi
    y = _dot(bu_ref[...].astype(BF16), wout_ref[0])
    y_ref[0] = y.reshape(tc, nb, BRANCH_WIDTH)


def _s5_scan(u_tm, win, wout, a_re, a_im, n_ctx):
    s_len, nb, _ = u_tm.shape
    tc = S5_CHUNK
    assert nb == 8 and s_len % tc == 0 and n_ctx % tc == 0
    nct = n_ctx // tc
    nlt = (s_len - n_ctx) // tc

    def chunk(d, i):
        rev = jnp.where(i < nct, nct - 1 - i, 2 * nct + nlt - 1 - i)
        return jnp.where(d == 0, i, rev)

    def out_chunk(d, i):
        c = chunk(d, i)
        return jnp.where(c < nct, nlt + c, c - nct)

    return pl.pallas_call(
        functools.partial(_s5_kernel, tc=tc, nb=nb),
        grid=(2, nct + nlt),
        in_specs=[pl.BlockSpec((tc, nb, BRANCH_WIDTH), lambda d, i: (chunk(d, i), 0, 0)),
                  pl.BlockSpec((1, BRANCH_WIDTH, 2 * S5_FLAT), lambda d, i: (d, 0, 0)),
                  pl.BlockSpec((1, 2 * S5_FLAT, BRANCH_WIDTH), lambda d, i: (d, 0, 0)),
                  pl.BlockSpec((1, 1, S5_FLAT), lambda d, i: (d, 0, 0)),
                  pl.BlockSpec((1, 1, S5_FLAT), lambda d, i: (d, 0, 0))],
        out_specs=pl.BlockSpec((1, tc, nb, BRANCH_WIDTH), lambda d, i: (d, out_chunk(d, i), 0, 0)),
        out_shape=jax.ShapeDtypeStruct((2, s_len, nb, BRANCH_WIDTH), F32),
        scratch_shapes=[pltpu.VMEM((tc * nb, 2 * S5_FLAT), F32),
                        pltpu.VMEM((nb, 2 * S5_FLAT), F32)],
        compiler_params=_params("arbitrary", "arbitrary"),
        name="s5_scan",
    )(u_tm, win, wout, a_re, a_im)


def _s5_params(lam_re, lam_im, log_step, b_re, b_im, c_re, c_im):
    lr = lam_re.astype(F32)
    li = lam_im.astype(F32)
    dt = jnp.exp(log_step.astype(F32))[..., None]
    mag = jnp.exp(lr * dt)
    a_re = mag * jnp.cos(li * dt)
    a_im = mag * jnp.sin(li * dt)
    nr, ni, den = a_re - 1.0, a_im, lr * lr + li * li
    k_re = ((nr * lr + ni * li) / den)[..., None]
    k_im = ((ni * lr - nr * li) / den)[..., None]
    br = b_re.astype(F32)
    bi = b_im.astype(F32)
    bb_re = k_re * br - k_im * bi
    bb_im = k_re * bi + k_im * br
    eye = jnp.eye(S5_GROUPS, dtype=F32)

    def blockdiag_in(bb):
        m = jnp.einsum('dgpc,gh->dgchp', bb, eye)
        return m.reshape(2, S5_GROUPS * S5_GROUP, S5_GROUPS * S5_STATE)

    def blockdiag_out(cc):
        m = jnp.einsum('dgcp,gh->dgphc', cc, eye)
        return m.reshape(2, S5_GROUPS * S5_STATE, S5_GROUPS * S5_GROUP)

    win = jnp.concatenate([blockdiag_in(bb_re), blockdiag_in(bb_im)], axis=2).astype(BF16)
    wout = jnp.concatenate([blockdiag_out(c_re.astype(F32)), -blockdiag_out(c_im.astype(F32))], axis=1).astype(BF16)
    return win, wout, a_re.reshape(2, 1, S5_FLAT), a_im.reshape(2, 1, S5_FLAT)


def _softmax_parts(scores, extra=None):
    m = scores[0].max(axis=-1, keepdims=True)
    for s in scores[1:]:
        m = jnp.maximum(m, s.max(axis=-1, keepdims=True))
    if extra is not None:
        m = jnp.maximum(m, extra)
    ps = [jnp.exp(s - m) for s in scores]
    l = ps[0].sum(axis=-1, keepdims=True)
    for p in ps[1:]:
        l = l + p.sum(axis=-1, keepdims=True)
    if extra is not None:
        l = l + jnp.exp(extra - m)
    return ps, l


NA_QROWS = 2
NA_KROWS = 10
VROWS = HEAD_DIM + 16


def _na_window_start(r, grid_rows):
    start = jnp.clip(r - NA_WIN_ROWS // 2, 0, grid_rows - NA_WIN_ROWS)
    return (jnp.minimum(start, grid_rows - NA_KROWS) // 2) * 2


def _head_blockdiag(qt, qbd_ref, nheads, rows_per_head):
    n = qt.shape[1]
    row_h = lax.broadcasted_iota(jnp.int32, qt.shape, 0) // rows_per_head
    zero = jnp.zeros_like(qt)
    for h in range(nheads):
        qbd_ref[:, h * n:(h + 1) * n] = jnp.where(row_h == h, qt, zero)


def _na_kernel(qt_ref, k_ref, vt_ref, kc_ref, vct_ref, bias_ref, o_ref, qbd_ref, *, grid_rows):
    nq = NA_QROWS * GRID_W
    nk = NA_KROWS * GRID_W
    off = pl.multiple_of(_na_window_start(NA_QROWS * pl.program_id(1), grid_rows) * GRID_W, 128)
    _head_blockdiag(qt_ref[0], qbd_ref, NA_HEADS, HEAD_DIM)
    qbd = qbd_ref[...]
    s_loc = _dot(k_ref[0, pl.ds(off, nk), :], qbd) + bias_ref[0]
    s_ctx = _dot(kc_ref[0], qbd)
    m = jnp.maximum(s_loc.max(axis=0, keepdims=True), s_ctx.max(axis=0, keepdims=True))
    p_loc = jnp.exp2((s_loc - m).astype(BF16))
    p_ctx = jnp.exp2((s_ctx - m).astype(BF16))
    vw = vt_ref[0, :, pl.ds(off, nk)]
    outs = []
    for h in range(NA_HEADS):
        rows = slice(h * VROWS, (h + 1) * VROWS)
        cols = slice(h * nq, (h + 1) * nq)
        o = _dot(vw[rows], p_loc[:, cols]) + _dot(vct_ref[0, rows, :], p_ctx[:, cols])
        outs.append(o[0:HEAD_DIM] / o[HEAD_DIM:HEAD_DIM + 1])
    o_ref[...] = jnp.concatenate(outs, axis=0).T.astype(BF16)


def _na_classes(grid_rows):
    return [0, 2, 4, grid_rows - 4, grid_rows - 2]


def _na_bias_table(rpb, grid_rows):
    col = np.arange(GRID_W)
    cstart = np.clip(col - NA_WIN_COLS // 2, 0, GRID_W - NA_WIN_COLS)
    col_in = (col[None, :] >= cstart[:, None]) & (col[None, :] < cstart[:, None] + NA_WIN_COLS)
    cb = np.clip(col[None, :] - col[:, None] + (NA_WIN_COLS - 1), 0, 2 * NA_WIN_COLS - 2)
    classes = _na_classes(grid_rows)
    rbi = np.zeros((len(classes), NA_QROWS, NA_KROWS), np.int64)
    row_in = np.zeros((len(classes), NA_QROWS, NA_KROWS), bool)
    for c, r in enumerate(classes):
        a_row = (min(int(np.clip(r - NA_WIN_ROWS // 2, 0, grid_rows - NA_WIN_ROWS)), grid_rows - NA_KROWS) // 2) * 2
        for qi in range(NA_QROWS):
            start_q = int(np.clip(r + qi - NA_WIN_ROWS // 2, 0, grid_rows - NA_WIN_ROWS))
            for j in range(NA_KROWS):
                row_in[c, qi, j] = start_q <= a_row + j < start_q + NA_WIN_ROWS
                rbi[c, qi, j] = np.clip(a_row + j - (r + qi) + NA_WIN_ROWS - 1, 0, 2 * NA_WIN_ROWS - 2)
    oh_row = jnp.asarray(rbi[..., None] == np.arange(2 * NA_WIN_ROWS - 1), F32)
    oh_col = jnp.asarray(cb[:, :, None] == np.arange(2 * NA_WIN_COLS - 1), F32)
    t = jnp.einsum('hab,cija,qkb->cjkhiq', rpb.astype(F32), oh_row, oh_col, precision=HIGHEST) * LOG2E
    valid = row_in.transpose(0, 2, 1)[:, :, None, None, :, None] & col_in.T[None, None, :, None, None, :]
    t = jnp.where(valid, t, NEG_INF)
    return t.reshape(len(classes), NA_KROWS * GRID_W, NA_HEADS * NA_QROWS * GRID_W)


def _augment_vt(v, nheads):
    nb, s, _ = v.shape
    v = v.reshape(nb, s, nheads, HEAD_DIM)
    pad = jnp.zeros((nb, s, nheads, VROWS - HEAD_DIM), v.dtype).at[..., 0].set(1.0)
    return jnp.concatenate([v, pad], axis=-1).reshape(nb, s, nheads * VROWS).transpose(0, 2, 1)


def _na_attention(na, na_c, bias, nbatch, seq, n_ctx):
    grid_rows = seq // GRID_W
    assert grid_rows >= NA_KROWS and grid_rows % NA_QROWS == 0
    nq = NA_QROWS * GRID_W
    npair = grid_rows // NA_QROWS
    nal = na.reshape(nbatch, seq, 768)
    nac = na_c.reshape(nbatch, n_ctx, 768)
    qt = nal[:, :, 0:256].transpose(0, 2, 1)
    vt = _augment_vt(nal[:, :, 512:768], NA_HEADS)
    vct = _augment_vt(nac[:, :, 512:768], NA_HEADS)

    def cls(b, p):
        r = NA_QROWS * p
        c = jnp.where(r < 4, r // 2, jnp.where(r >= grid_rows - 4, (r - (grid_rows - 4)) // 2 + 3, 2))
        return (c, 0, 0)

    return pl.pallas_call(
        functools.partial(_na_kernel, grid_rows=grid_rows),
        grid=(nbatch, npair),
        in_specs=[pl.BlockSpec((1, 256, nq), lambda b, p: (b, 0, p)),
                  pl.BlockSpec((1, seq, 256), lambda b, p: (b, 0, 1)),
                  pl.BlockSpec((1, NA_HEADS * VROWS, seq), lambda b, p: (b, 0, 0)),
                  pl.BlockSpec((1, n_ctx, 256), lambda b, p: (b, 0, 1)),
                  pl.BlockSpec((1, NA_HEADS * VROWS, n_ctx), lambda b, p: (b, 0, 0)),
                  pl.BlockSpec((1, NA_KROWS * GRID_W, NA_HEADS * nq), cls)],
        out_specs=pl.BlockSpec((nq, 256), lambda b, p: (b * npair + p, 0)),
        out_shape=jax.ShapeDtypeStruct((nbatch * seq, 256), BF16),
        scratch_shapes=[pltpu.VMEM((256, NA_HEADS * nq), BF16)],
        compiler_params=_params("arbitrary", "arbitrary"),
        name="na_attention",
    )(qt, nal, vt, nac, vct, bias)


SWA_BLOCK = 128


def _swa_kernel(qt_ref, k_ref, vt_ref, kc_ref, vct_ref, sink_ref, o_ref, qbd_ref, *, seq):
    n = pl.program_id(1)
    band = 3 * SWA_BLOCK
    nq = SWA_BLOCK
    nqh = 2 * SWA_KV_HEADS
    bstart = pl.multiple_of(jnp.clip((n - 1) * SWA_BLOCK, 0, seq - band), SWA_BLOCK)
    qt = qt_ref[0]
    zero = jnp.zeros((HEAD_DIM, nq), qt.dtype)
    for hq in range(nqh):
        qh = qt[hq * HEAD_DIM:(hq + 1) * HEAD_DIM]
        qbd_ref[:, hq * nq:(hq + 1) * nq] = jnp.concatenate([qh, zero] if hq // 2 == 0 else [zero, qh], axis=0)
    qbd = qbd_ref[...]
    kpos = bstart + lax.broadcasted_iota(jnp.int32, (band, nqh * nq), 0)
    qpos = n * SWA_BLOCK + lax.broadcasted_iota(jnp.int32, (band, nqh * nq), 1) % nq
    s_loc = jnp.where(jnp.abs(qpos - kpos) <= SWA_WINDOW, _dot(k_ref[0, pl.ds(bstart, band), :], qbd), NEG_INF)
    s_ctx = _dot(kc_ref[0], qbd)
    sink = jnp.concatenate([jnp.broadcast_to(sink_ref[0:1, hq:hq + 1] * LOG2E, (1, nq)) for hq in range(nqh)], axis=1)
    m = jnp.maximum(jnp.maximum(s_loc.max(axis=0, keepdims=True), s_ctx.max(axis=0, keepdims=True)), sink)
    p_loc = jnp.exp2((s_loc - m).astype(BF16))
    p_ctx = jnp.exp2((s_ctx - m).astype(BF16))
    p_sink = jnp.exp2(sink - m)
    vw = vt_ref[0, :, pl.ds(bstart, band)]
    outs = []
    for hq in range(nqh):
        rows = slice((hq // 2) * VROWS, (hq // 2 + 1) * VROWS)
        cols = slice(hq * nq, (hq + 1) * nq)
        o = _dot(vw[rows], p_loc[:, cols]) + _dot(vct_ref[0, rows, :], p_ctx[:, cols])
        outs.append(o[0:HEAD_DIM] / (o[HEAD_DIM:HEAD_DIM + 1] + p_sink[:, cols]))
    o_ref[...] = jnp.concatenate(outs, axis=0).T.astype(BF16)


def _swa_attention(sw, sw_c, sink, nbatch, seq, n_ctx):
    nblk = seq // SWA_BLOCK
    nqh = 2 * SWA_KV_HEADS
    assert seq >= 3 * SWA_BLOCK
    sink_pad = jnp.zeros((1, 128), F32).at[0, 0:nqh].set(sink.astype(F32))
    swl = sw.reshape(nbatch, seq, 512)
    swc = sw_c.reshape(nbatch, n_ctx, 512)
    qt = swl[:, :, 0:256].transpose(0, 2, 1)
    vt = _augment_vt(swl[:, :, 384:512], SWA_KV_HEADS)
    vct = _augment_vt(swc[:, :, 384:512], SWA_KV_HEADS)
    return pl.pallas_call(
        functools.partial(_swa_kernel, seq=seq),
        grid=(nbatch, nblk),
        in_specs=[pl.BlockSpec((1, 256, SWA_BLOCK), lambda b, n: (b, 0, n)),
                  pl.BlockSpec((1, seq, 128), lambda b, n: (b, 0, 2)),
                  pl.BlockSpec((1, SWA_KV_HEADS * VROWS, seq), lambda b, n: (b, 0, 0)),
                  pl.BlockSpec((1, n_ctx, 128), lambda b, n: (b, 0, 2)),
                  pl.BlockSpec((1, SWA_KV_HEADS * VROWS, n_ctx), lambda b, n: (b, 0, 0)),
                  pl.BlockSpec((1, 128), lambda b, n: (0, 0))],
        out_specs=pl.BlockSpec((SWA_BLOCK, 256), lambda b, n: (b * nblk + n, 0)),
        out_shape=jax.ShapeDtypeStruct((nbatch * seq, 256), BF16),
        scratch_shapes=[pltpu.VMEM((SWA_KV_HEADS * HEAD_DIM, nqh * SWA_BLOCK), BF16)],
        compiler_params=_params("arbitrary", "arbitrary"),
        name="swa_attention",
    )(qt, swl, vt, swc, vct, sink_pad)


DIFF_TQ = 512
DIFF_CK = 256
LOG2E = math.log2(math.e)
DIFF_VROWS = HEAD_DIM + 16


def _diff_lambda(lq1_ref, lk1_ref, lq2_ref, lk2_ref, lambda_init):
    s1 = jnp.sum(lq1_ref[...] * lk1_ref[...], axis=-1, keepdims=True)
    s2 = jnp.sum(lq2_ref[...] * lk2_ref[...], axis=-1, keepdims=True)
    return jnp.exp(s1) - jnp.exp(s2) + lambda_init


def _stack_maps(qh):
    lane = lax.broadcasted_iota(jnp.int32, qh.shape, 1)
    zero = jnp.zeros_like(qh)
    return jnp.concatenate([jnp.where(lane < DIFF_QK_DIM, qh, zero),
                            jnp.where(lane >= DIFF_QK_DIM, qh, zero)], axis=0)


def _subln(o0, o1, lam, g, lambda_init):
    o = o0 - lam * o1
    return _rms(o) * g * (1.0 - lambda_init)


def _diff_kernel(qt_ref, k_ref, vt_ref, lq1_ref, lk1_ref, lq2_ref, lk2_ref, g_ref, o_ref, qbd_ref, acc_ref, s_ref,
                 *, nchunk, lambda_init):
    lam = _diff_lambda(lq1_ref, lk1_ref, lq2_ref, lk2_ref, lambda_init)
    qt = qt_ref[0]
    tq = qt.shape[1]
    w = 2 * tq
    row = lax.broadcasted_iota(jnp.int32, qt.shape, 0) // DIFF_QK_DIM
    zero = jnp.zeros_like(qt)
    for j in range(2 * DIFF_HEADS):
        qbd_ref[:, j * tq:(j + 1) * tq] = jnp.where(row == j, qt, zero)
    acc_ref[...] = jnp.zeros_like(acc_ref)

    def scores(slot, c, h):
        s = _dot(k_ref[0, c], qbd_ref[:, h * w:(h + 1) * w])
        s_ref[slot, h] = s
        return s.max(axis=0, keepdims=True)

    def softmax_pv(slot, c, h, m_run, m_chunk):
        m_new = jnp.maximum(m_run, m_chunk)
        alpha = jnp.exp2(m_run - m_new)
        p = jnp.exp2((s_ref[slot, h] - m_new).astype(BF16))
        acc_ref[h] = alpha * acc_ref[h] + _dot(vt_ref[0, c, h * DIFF_VROWS:(h + 1) * DIFF_VROWS, :], p)
        return m_new

    def step(slot, c, carry):
        m_run, m_chunk = carry
        new_run, new_chunk = [], []
        for h in range(DIFF_HEADS):
            new_chunk.append(scores(1 - slot, c + 1, h))
            new_run.append(softmax_pv(slot, c, h, m_run[h], m_chunk[h]))
        return tuple(new_run), tuple(new_chunk)

    def body(i, carry):
        c = 2 * i
        return step(1, c + 1, step(0, c, carry))

    m_run = tuple(jnp.full((1, w), NEG_INF, F32) for _ in range(DIFF_HEADS))
    m_chunk = tuple(scores(0, 0, h) for h in range(DIFF_HEADS))
    npair = (nchunk - 1) // 2
    m_run, m_chunk = lax.fori_loop(0, npair, body, (m_run, m_chunk))
    if nchunk % 2 == 0:
        m_run, m_chunk = step(0, nchunk - 2, (m_run, m_chunk))
        for h in range(DIFF_HEADS):
            softmax_pv(1, nchunk - 1, h, m_run[h], m_chunk[h])
    else:
        for h in range(DIFF_HEADS):
            softmax_pv(0, nchunk - 1, h, m_run[h], m_chunk[h])
    outs = []
    for h in range(DIFF_HEADS):
        o = acc_ref[h, 0:HEAD_DIM, :] / acc_ref[h, HEAD_DIM:HEAD_DIM + 1, :]
        d = o[:, 0:tq] - lam * o[:, tq:w]
        d = d * lax.rsqrt(jnp.mean(d * d, axis=0, keepdims=True) + EPS)
        outs.append(d * g_ref[...] * (1.0 - lambda_init))
    o_ref[...] = jnp.concatenate(outs, axis=0).T.astype(BF16)


def _diff_attention(df, df_c, lqk, subln_g, lambda_init, nbatch, seq, n_ctx):
    tq, ck = min(DIFF_TQ, seq), DIFF_CK
    s_all = seq + n_ctx
    assert s_all % ck == 0 and seq % tq == 0
    nchunk = s_all // ck
    nq = seq // tq
    dfl = df.reshape(nbatch, seq, 768)
    dfc = df_c.reshape(nbatch, n_ctx, 768)
    qt = dfl[:, :, 0:256].transpose(0, 2, 1)
    k_all = jnp.concatenate([dfl[:, :, 256:512], dfc[:, :, 256:512]], axis=1).reshape(nbatch, nchunk, ck, 256)
    v_all = jnp.concatenate([dfl[:, :, 512:768], dfc[:, :, 512:768]], axis=1)
    v_all = v_all.reshape(nbatch, nchunk, ck, DIFF_HEADS, HEAD_DIM)
    pad = jnp.zeros((nbatch, nchunk, ck, DIFF_HEADS, DIFF_VROWS - HEAD_DIM), BF16).at[..., 0].set(1.0)
    vt_all = jnp.concatenate([v_all, pad], axis=-1).reshape(nbatch, nchunk, ck, DIFF_HEADS * DIFF_VROWS)
    vt_all = vt_all.transpose(0, 1, 3, 2)
    vec = pl.BlockSpec((1, DIFF_QK_DIM), lambda b, n: (0, 0))
    return pl.pallas_call(
        functools.partial(_diff_kernel, nchunk=nchunk, lambda_init=lambda_init),
        grid=(nbatch, nq),
        in_specs=[pl.BlockSpec((1, 256, tq), lambda b, n: (b, 0, n)),
                  pl.BlockSpec((1, nchunk, ck, 256), lambda b, n: (b, 0, 0, 0)),
                  pl.BlockSpec((1, nchunk, DIFF_HEADS * DIFF_VROWS, ck), lambda b, n: (b, 0, 0, 0)),
                  vec, vec, vec, vec,
                  pl.BlockSpec((HEAD_DIM, 1), lambda b, n: (0, 0))],
        out_specs=pl.BlockSpec((tq, 256), lambda b, n: (b * nq + n, 0)),
        out_shape=jax.ShapeDtypeStruct((nbatch * seq, 256), BF16),
        scratch_shapes=[pltpu.VMEM((256, 2 * DIFF_HEADS * tq), BF16),
                        pltpu.VMEM((DIFF_HEADS, DIFF_VROWS, 2 * tq), F32),
                        pltpu.VMEM((2, DIFF_HEADS, ck, 2 * tq), F32)],
        compiler_params=_params("arbitrary", "arbitrary"),
        name="diff_attention",
    )(qt, k_all, vt_all, *lqk, subln_g.reshape(HEAD_DIM, 1).astype(F32))


def _ctx_attn_kernel(na_ref, sw_ref, df_ref, sink_ref, lq1_ref, lk1_ref, lq2_ref, lk2_ref, g_ref,
                     nb_o, sw_o, df_o, *, lambda_init):
    n = na_ref.shape[0]
    na = na_ref[...]
    outs = []
    for h in range(NA_HEADS):
        sl = slice(h * HEAD_DIM, (h + 1) * HEAD_DIM)
        (p,), l = _softmax_parts([_nt_dot(na[:, sl], na[:, 256 + h * HEAD_DIM:256 + (h + 1) * HEAD_DIM])])
        outs.append(_dot(p.astype(BF16), na[:, 512 + h * HEAD_DIM:512 + (h + 1) * HEAD_DIM]) / l)
    nb_o[...] = jnp.concatenate(outs, axis=1).astype(BF16)
    sw = sw_ref[...]
    outs = []
    for hq in range(4):
        kv = hq // 2
        k = sw[:, 256 + kv * HEAD_DIM:256 + (kv + 1) * HEAD_DIM]
        v = sw[:, 384 + kv * HEAD_DIM:384 + (kv + 1) * HEAD_DIM]
        sk = jnp.broadcast_to(sink_ref[0:1, hq:hq + 1], (n, 1))
        (p,), l = _softmax_parts([_nt_dot(sw[:, hq * HEAD_DIM:(hq + 1) * HEAD_DIM], k)], extra=sk)
        outs.append(_dot(p.astype(BF16), v) / l)
    sw_o[...] = jnp.concatenate(outs, axis=1).astype(BF16)
    lam = _diff_lambda(lq1_ref, lk1_ref, lq2_ref, lk2_ref, lambda_init)
    df = df_ref[...]
    outs = []
    for h in range(DIFF_HEADS):
        sl = slice(h * HEAD_DIM, (h + 1) * HEAD_DIM)
        q2 = _stack_maps(df[:, sl])
        (p,), l = _softmax_parts([_nt_dot(q2, df[:, 256 + h * HEAD_DIM:256 + (h + 1) * HEAD_DIM])])
        o = _dot(p.astype(BF16), df[:, 512 + h * HEAD_DIM:512 + (h + 1) * HEAD_DIM]) / l
        outs.append(_subln(o[0:n], o[n:2 * n], lam, g_ref[...], lambda_init))
    df_o[...] = jnp.concatenate(outs, axis=1).astype(BF16)


def _ctx_attention(na_c, sw_c, df_c, sink, lqk, subln_g, lambda_init, nbatch, n_ctx):
    sink_pad = jnp.zeros((1, 128), F32).at[0, 0:4].set(sink.astype(F32))
    vec = pl.BlockSpec((1, DIFF_QK_DIM), lambda b: (0, 0))
    out = jax.ShapeDtypeStruct((nbatch * n_ctx, 256), BF16)
    return pl.pallas_call(
        functools.partial(_ctx_attn_kernel, lambda_init=lambda_init),
        grid=(nbatch,),
        in_specs=[pl.BlockSpec((n_ctx, 768), lambda b: (b, 0)),
                  pl.BlockSpec((n_ctx, 512), lambda b: (b, 0)),
                  pl.BlockSpec((n_ctx, 768), lambda b: (b, 0)),
                  pl.BlockSpec((1, 128), lambda b: (0, 0)),
                  vec, vec, vec, vec,
                  pl.BlockSpec((1, HEAD_DIM), lambda b: (0, 0))],
        out_specs=[pl.BlockSpec((n_ctx, 256), lambda b: (b, 0))] * 3,
        out_shape=[out, out, out],
        compiler_params=_params("arbitrary"),
        name="ctx_attention",
    )(na_c, sw_c, df_c, sink_pad, *lqk, subln_g.reshape(1, HEAD_DIM).astype(F32))


def _merge_kernel(u_ref, yf_ref, yr_ref, d_ref, gw_ref, gb_ref, yb_ref, yc_ref, yd_ref, gate_ref,
                  wb_ref, wo_ref, x_ref, g1_ref, n2_ref, sc2_ref, sh2_ref, rw_ref,
                  x1_o, h2_o, h2p_o, lg_o):
    y = u_ref[...] * d_ref[...] + yf_ref[0, 0] + yr_ref[0, 0]
    a = jax.nn.gelu(y, approximate=True)
    ya = a * jax.nn.sigmoid(_dot(a.astype(BF16), gw_ref[...]) + gb_ref[...])
    branches = (ya.astype(BF16), yb_ref[...], yc_ref[...], yd_ref[...])
    acc = None
    for i in range(4):
        t = gate_ref[:, i * D_MODEL:(i + 1) * D_MODEL].astype(F32) * _dot(branches[i], wb_ref[i])
        acc = t if acc is None else acc + t
    mixed = _dot(acc.astype(BF16), wo_ref[...])
    x1 = x_ref[...] + g1_ref[0] * mixed
    x1_o[...] = x1
    h2 = _rms(x1) * n2_ref[...]
    h2 = h2 * (1.0 + sc2_ref[0]) + sh2_ref[0]
    h2_o[...] = h2.astype(BF16)
    h2p_o[...] = _pack_rows(h2[:, 0:HALF_D], h2[:, HALF_D:D_MODEL])
    lg_o[...] = lax.dot_general(rw_ref[...], h2, (((1,), (1,)), ((), ())), preferred_element_type=F32,
                                precision=HIGHEST)


def _merge(u, y_bm, y_start, rows_per_seq, s5_d, glu_w, glu_b, yb, yc, yd, gate, wb, wo, x2d, g1, norm2_g, sc2, sh2,
           router_w, *, rows_per_mod):
    rows = x2d.shape[0]
    tm = min(512, rows_per_seq)
    assert rows % tm == 0 and rows_per_mod % tm == 0 and rows_per_seq % tm == 0 and y_start % tm == 0
    tiles_per_seq = rows_per_seq // tm

    def mod_map(i):
        return ((i * tm) // rows_per_mod, 0, 0)

    def y_spec(d):
        return pl.BlockSpec((1, 1, tm, 256), lambda i: (d, i // tiles_per_seq, y_start // tm + i % tiles_per_seq, 0))

    row = lambda w: pl.BlockSpec((tm, w), lambda i: (i, 0))
    full = lambda *shape: pl.BlockSpec(shape, lambda i: (0,) * len(shape))
    mod = pl.BlockSpec((1, 1, D_MODEL), mod_map)
    return pl.pallas_call(
        _merge_kernel,
        grid=(rows // tm,),
        in_specs=[row(256), y_spec(0), y_spec(1), full(1, 256), full(256, 256), full(1, 256),
                  row(256), row(256), row(256), row(GATE_WIDTH),
                  full(4, 256, D_MODEL), full(D_MODEL, D_MODEL), row(D_MODEL),
                  mod, full(1, D_MODEL), mod, mod, full(N_EXPERTS, D_MODEL)],
        out_specs=[row(D_MODEL), row(D_MODEL), row(HALF_D), pl.BlockSpec((N_EXPERTS, tm), lambda i: (0, i))],
        out_shape=[jax.ShapeDtypeStruct((rows, D_MODEL), F32),
                   jax.ShapeDtypeStruct((rows, D_MODEL), BF16),
                   jax.ShapeDtypeStruct((rows, HALF_D), jnp.int32),
                   jax.ShapeDtypeStruct((N_EXPERTS, rows), F32)],
        compiler_params=_params("arbitrary"),
        name="merge",
    )(u, y_bm, y_bm, s5_d.reshape(1, 256).astype(F32), glu_w.astype(BF16), glu_b.reshape(1, 256).astype(F32),
      yb, yc, yd, gate, wb.astype(BF16), wo.astype(BF16), x2d, g1, norm2_g.reshape(1, D_MODEL), sc2, sh2,
      router_w.astype(F32).T)


def _router_kernel(lg_ref, b_ref, tri_ref, idx_ref, rank_ref, w_ref, cnt_ref, base_ref):
    tr = lg_ref.shape[1]
    gsz = N_EXPERTS // N_EXPERT_GROUPS
    sc = jax.nn.sigmoid(lg_ref[...])
    bi = sc + b_ref[...]
    e_iota = lax.broadcasted_iota(jnp.int32, (gsz, tr), 0).astype(F32)
    groups = [bi[g * gsz:(g + 1) * gsz] for g in range(N_EXPERT_GROUPS)]
    gs = []
    for bg in groups:
        m1 = bg.max(axis=0, keepdims=True)
        i1 = jnp.where(bg == m1, e_iota, float(gsz)).min(axis=0, keepdims=True)
        m2 = jnp.where(e_iota == i1, -jnp.inf, bg).max(axis=0, keepdims=True)
        gs.append(m1 + m2)
    v = []
    for g in range(N_EXPERT_GROUPS):
        rank = jnp.zeros((1, tr), F32)
        for g2 in range(N_EXPERT_GROUPS):
            if g2 == g:
                continue
            beats = (gs[g2] >= gs[g]) if g2 < g else (gs[g2] > gs[g])
            rank = rank + jnp.where(beats, 1.0, 0.0)
        v.append(jnp.where(rank < TOPK_GROUPS, groups[g], NEG_INF))
    flat = [e_iota + float(g * gsz) for g in range(N_EXPERT_GROUPS)]
    sel = [jnp.zeros((gsz, tr), F32) for _ in range(N_EXPERT_GROUPS)]
    picks = []
    for _ in range(TOP_K):
        m = v[0].max(axis=0, keepdims=True)
        for g in range(1, N_EXPERT_GROUPS):
            m = jnp.maximum(m, v[g].max(axis=0, keepdims=True))
        am = jnp.where(v[0] == m, flat[0], float(N_EXPERTS)).min(axis=0, keepdims=True)
        for g in range(1, N_EXPERT_GROUPS):
            am = jnp.minimum(am, jnp.where(v[g] == m, flat[g], float(N_EXPERTS)).min(axis=0, keepdims=True))
        hits = []
        for g in range(N_EXPERT_GROUPS):
            hit = flat[g] == am
            hits.append(hit)
            sel[g] = jnp.where(hit, 1.0, sel[g])
            v[g] = jnp.where(hit, -jnp.inf, v[g])
        picks.append((am, hits))
    scg = [sc[g * gsz:(g + 1) * gsz] for g in range(N_EXPERT_GROUPS)]
    den = (sel[0] * scg[0]).sum(axis=0, keepdims=True)
    for g in range(1, N_EXPERT_GROUPS):
        den = den + (sel[g] * scg[g]).sum(axis=0, keepdims=True)

    @pl.when(pl.program_id(0) == 0)
    def _():
        base_ref[...] = jnp.zeros_like(base_ref)

    sel_all = jnp.concatenate(sel, axis=0)
    before = _dot(sel_all.astype(jnp.bfloat16), tri_ref[...]) + base_ref[...]
    for k, (am, hits) in enumerate(picks):
        wk = jnp.zeros((1, tr), F32)
        rk = jnp.zeros((1, tr), F32)
        for g in range(N_EXPERT_GROUPS):
            wk = wk + jnp.where(hits[g], scg[g], 0.0).sum(axis=0, keepdims=True)
            rk = rk + jnp.where(hits[g], before[g * gsz:(g + 1) * gsz], 0.0).sum(axis=0, keepdims=True)
        idx_ref[k:k + 1, :] = am.astype(jnp.int32)
        rank_ref[k:k + 1, :] = rk.astype(jnp.int32)
        w_ref[k:k + 1, :] = wk / den * ROUTED_SCALE
    idx_ref[TOP_K:8, :] = jnp.zeros((8 - TOP_K, tr), jnp.int32)
    rank_ref[TOP_K:8, :] = jnp.zeros((8 - TOP_K, tr), jnp.int32)
    w_ref[TOP_K:8, :] = jnp.zeros((8 - TOP_K, tr), F32)
    base_ref[...] += sel_all.sum(axis=1, keepdims=True)
    cnt_ref[...] = base_ref[...].astype(jnp.int32)


ROUTER_TILE = 512


def _router(logits_t, router_b):
    ne, rows = logits_t.shape
    tr = ROUTER_TILE
    assert rows % tr == 0
    tri = jnp.asarray(np.triu(np.ones((tr, tr), np.float32), k=1), jnp.bfloat16)
    pick = pl.BlockSpec((8, tr), lambda i: (0, i))
    return pl.pallas_call(
        _router_kernel,
        grid=(rows // tr,),
        in_specs=[pl.BlockSpec((ne, tr), lambda i: (0, i)),
                  pl.BlockSpec((ne, 1), lambda i: (0, 0)),
                  pl.BlockSpec((tr, tr), lambda i: (0, 0))],
        out_specs=[pick, pick, pick, pl.BlockSpec((ne, 1), lambda i: (0, 0))],
        out_shape=[jax.ShapeDtypeStruct((8, rows), jnp.int32),
                   jax.ShapeDtypeStruct((8, rows), jnp.int32),
                   jax.ShapeDtypeStruct((8, rows), F32),
                   jax.ShapeDtypeStruct((ne, 1), jnp.int32)],
        scratch_shapes=[pltpu.VMEM((ne, 1), F32)],
        compiler_params=_params("arbitrary"),
        name="router",
    )(logits_t, router_b.reshape(ne, 1).astype(F32), tri)


MOE_BLOCK = 512
MOE_TOKENS = 256
MOE_COMBINE_GROUPS = 4
HALF_D = D_MODEL // 2


def _pack_rows(lo, hi):
    lo_b = pltpu.bitcast(lo.astype(jnp.bfloat16).astype(F32), jnp.uint32)
    hi_b = pltpu.bitcast(hi.astype(jnp.bfloat16).astype(F32), jnp.uint32)
    return pltpu.bitcast((hi_b & jnp.uint32(0xFFFF0000)) | (lo_b >> 16), jnp.int32)


def _unpack_rows(words):
    u = pltpu.bitcast(words, jnp.uint32)
    lo = pltpu.bitcast(u << 16, F32)
    hi = pltpu.bitcast(u & jnp.uint32(0xFFFF0000), F32)
    return lo, hi


def _swiglu(x_bf16, wgu, wd):
    hgu = _dot(x_bf16, wgu)
    g = hgu[:, 0:EXPERT_HIDDEN]
    a = g * jax.nn.sigmoid(g) * hgu[:, EXPERT_HIDDEN:2 * EXPERT_HIDDEN]
    return _dot(a.astype(BF16), wd)


SC_CORES = 2
SC_SUBCORES = 16
SC_STREAM_ROWS = 128


def _sc_for_each_chunk(total, body):
    chunk = SC_STREAM_ROWS
    assert total % chunk == 0
    nchunk = total // chunk
    per_worker = pl.cdiv(nchunk, SC_CORES * SC_SUBCORES)
    first = (lax.axis_index("s") * SC_CORES + lax.axis_index("c")) * per_worker

    @pl.loop(0, per_worker)
    def _(j):
        @pl.when(first + j < nchunk)
        def _():
            body((first + j) * chunk)


def _sc_scatter_rows(rows, slots, n_out):
    total, n = rows.shape
    chunk = SC_STREAM_ROWS
    mesh = plsc.VectorSubcoreMesh(core_axis_name="c", subcore_axis_name="s")

    @functools.partial(
        pl.kernel, mesh=mesh,
        out_type=jax.ShapeDtypeStruct((n_out, n), jnp.int32),
        scratch_types=[pltpu.VMEM((8, chunk), jnp.int32),
                       pltpu.VMEM((chunk, n), jnp.int32),
                       pltpu.SemaphoreType.DMA],
        name="moe_dispatch_sc",
    )
    def scatter(rows_hbm, slot_hbm, out_hbm, idx_v, rows_v, sem):
        def body(off):
            pltpu.sync_copy(rows_hbm.at[pl.ds(off, chunk)], rows_v)
            pltpu.sync_copy(slot_hbm.at[:, pl.ds(off, chunk)], idx_v)
            for k in range(TOP_K):
                pltpu.async_copy(rows_v, out_hbm.at[idx_v.at[k]], sem).wait()

        _sc_for_each_chunk(total, body)

    return scatter(rows, slots)


def _experts_kernel(be_ref, nv_ref, nb_ref, xs_ref, wg_ref, wu_ref, wd_ref, ys_ref, wgu_bf, wd_bf):
    b = pl.program_id(0)

    @pl.when(b < nb_ref[0])
    def _():
        @pl.when((b == 0) | (be_ref[b] != be_ref[jnp.maximum(b - 1, 0)]))
        def _():
            wgu_bf[:, 0:EXPERT_HIDDEN] = wg_ref[0, 0].astype(BF16)
            wgu_bf[:, EXPERT_HIDDEN:2 * EXPERT_HIDDEN] = wu_ref[0, 0].astype(BF16)
            wd_bf[...] = wd_ref[0, 0].astype(BF16)

        words = xs_ref[...]
        live = lax.broadcasted_iota(jnp.int32, words.shape, 0) < nv_ref[b]
        lo, hi = _unpack_rows(jnp.where(live, words, 0))
        x = jnp.concatenate([lo, hi], axis=1).astype(BF16)
        y = _swiglu(x, wgu_bf[...], wd_bf[...])
        ys_ref[...] = _pack_rows(y[:, 0:HALF_D], y[:, HALF_D:D_MODEL])


def _sc_gather_rows(table, indices):
    m, n = indices.shape[0], table.shape[1]
    chunk = SC_STREAM_ROWS
    mesh = plsc.VectorSubcoreMesh(core_axis_name="c", subcore_axis_name="s")

    @functools.partial(
        pl.kernel, mesh=mesh,
        out_type=jax.ShapeDtypeStruct((m, n), jnp.int32),
        scratch_types=[pltpu.VMEM((chunk,), jnp.int32),
                       pltpu.VMEM((chunk, n), jnp.int32),
                       pltpu.SemaphoreType.DMA],
        name="moe_gather_sc",
    )
    def gather(table_hbm, idx_hbm, out_hbm, idx_v, rows_v, sem):
        def body(off):
            pltpu.sync_copy(idx_hbm.at[pl.ds(off, chunk)], idx_v)
            pltpu.async_copy(table_hbm.at[idx_v], rows_v, sem).wait()
            pltpu.sync_copy(rows_v, out_hbm.at[pl.ds(off, chunk)])

        _sc_for_each_chunk(m, body)

    return gather(table, indices)


def _combine_into_kernel(w_ref, rows_ref, h_ref, wsgu_ref, wsd_ref, x1_ref, g2_ref, fg_ref, prev_ref, o_ref, *, final):
    del prev_ref
    _combine_kernel(w_ref, rows_ref, h_ref, wsgu_ref, wsd_ref, x1_ref, g2_ref, fg_ref, o_ref, final=final)


def _combine_kernel(w_ref, rows_ref, h_ref, wsgu_ref, wsd_ref, x1_ref, g2_ref, fg_ref, o_ref, *, final):
    shared = _swiglu(h_ref[...], wsgu_ref[...], wsd_ref[...])
    acc_lo = shared[:, 0:HALF_D]
    acc_hi = shared[:, HALF_D:D_MODEL]
    w = w_ref[...]
    for k in range(TOP_K):
        lo, hi = _unpack_rows(rows_ref[k])
        acc_lo = acc_lo + w[:, k:k + 1] * lo
        acc_hi = acc_hi + w[:, k:k + 1] * hi
    x2 = x1_ref[...] + g2_ref[0] * jnp.concatenate([acc_lo, acc_hi], axis=1)
    if final:
        x2 = _rms(x2) * fg_ref[...]
    o_ref[...] = x2


def _moe(h2, h2p, picks, lp, x1, g2, final_g, *, rows_per_mod, final):
    idx, rank, wsel, counts = picks
    rows = h2.shape[0]
    tt = MOE_TOKENS
    blk = MOE_BLOCK
    assert rows % tt == 0 and rows_per_mod % tt == 0 and (rows * TOP_K) % blk == 0
    ntile = rows // tt
    nblock = rows * TOP_K // blk + N_EXPERTS

    cnt = counts.reshape(N_EXPERTS)
    padded = (cnt + blk - 1) // blk * blk
    e_ids = jnp.arange(N_EXPERTS, dtype=jnp.int32)
    pends = jnp.sum(jnp.where(e_ids[None, :] <= e_ids[:, None], padded[None, :], 0), axis=1)
    pstart = (pends - padded).astype(jnp.int32)
    nb_used = (jnp.sum(padded) // blk).astype(jnp.int32).reshape(1)
    first_row = jnp.arange(nblock, dtype=jnp.int32) * blk
    block_e = jnp.minimum(jnp.sum(pends[None, :] <= first_row[:, None], axis=1), N_EXPERTS - 1).astype(jnp.int32)
    slot = rank + jnp.sum(jnp.where(idx[:, :, None] == jnp.arange(N_EXPERTS, dtype=jnp.int32), pstart, 0), axis=-1)
    seg_end = jnp.sum(jnp.where(block_e[:, None] == e_ids[None, :], (cnt + pstart)[None, :], 0), axis=1)
    n_valid = jnp.clip(seg_end - first_row, 0, blk).astype(jnp.int32)

    xs = _sc_scatter_rows(h2p, slot, nblock * blk)

    def blk_map(b, be, nv, nb):
        return (jnp.minimum(b, nb[0] - 1), 0)

    layer = lp['layer']

    def w_map(b, be, nv, nb):
        return (layer, be[jnp.minimum(b, nb[0] - 1)], 0, 0)

    ys = pl.pallas_call(
        _experts_kernel,
        grid_spec=pltpu.PrefetchScalarGridSpec(
            num_scalar_prefetch=3,
            grid=(nblock,),
            in_specs=[pl.BlockSpec((blk, HALF_D), blk_map),
                      pl.BlockSpec((1, 1, D_MODEL, EXPERT_HIDDEN), w_map),
                      pl.BlockSpec((1, 1, D_MODEL, EXPERT_HIDDEN), w_map),
                      pl.BlockSpec((1, 1, EXPERT_HIDDEN, D_MODEL), w_map)],
            out_specs=pl.BlockSpec((blk, HALF_D), blk_map),
            scratch_shapes=[pltpu.VMEM((D_MODEL, 2 * EXPERT_HIDDEN), BF16),
                            pltpu.VMEM((EXPERT_HIDDEN, D_MODEL), BF16)]),
        out_shape=jax.ShapeDtypeStruct((nblock * blk, HALF_D), jnp.int32),
        compiler_params=_params("arbitrary"),
        name="moe_experts",
    )(block_e, n_valid, nb_used, xs, lp['exp_w_gate'], lp['exp_w_up'], lp['exp_w_down'])

    wsgu = jnp.concatenate([lp['sh_w_gate'], lp['sh_w_up']], axis=1).astype(BF16)
    wsd = lp['sh_w_down'].astype(BF16)
    ngroup = MOE_COMBINE_GROUPS if ntile % MOE_COMBINE_GROUPS == 0 and ntile >= 4 * MOE_COMBINE_GROUPS else 1
    gtile = ntile // ngroup
    grows = gtile * tt
    full = lambda *shape: pl.BlockSpec(shape, lambda i: (0,) * len(shape))
    wsel_t = wsel.T
    fg = final_g.reshape(1, D_MODEL).astype(F32)
    out = None
    for p in range(ngroup):
        gathered = _sc_gather_rows(ys, slot[0:TOP_K, p * grows:(p + 1) * grows].reshape(TOP_K * grows))
        gathered = gathered.reshape(TOP_K, grows, HALF_D)
        row = lambda width, p=p: pl.BlockSpec((tt, width), lambda i: (p * gtile + i, 0))
        in_specs = [row(8), pl.BlockSpec((TOP_K, tt, HALF_D), lambda i: (0, i, 0)), row(D_MODEL),
                    full(D_MODEL, 2 * EXPERT_HIDDEN), full(EXPERT_HIDDEN, D_MODEL), row(D_MODEL),
                    pl.BlockSpec((1, 1, D_MODEL), lambda i, p=p: (((p * gtile + i) * tt) // rows_per_mod, 0, 0)),
                    full(1, D_MODEL)]
        args = [wsel_t, gathered, h2, wsgu, wsd, x1, g2, fg]
        kern = functools.partial(_combine_kernel, final=final)
        aliases = {}
        if out is not None:
            in_specs.append(pl.BlockSpec(memory_space=pl.ANY))
            args.append(out)
            aliases = {len(args) - 1: 0}
            kern = functools.partial(_combine_into_kernel, final=final)
        out = pl.pallas_call(
            kern,
            grid=(gtile,),
            in_specs=in_specs,
            out_specs=row(D_MODEL),
            out_shape=jax.ShapeDtypeStruct((rows, D_MODEL), F32),
            input_output_aliases=aliases,
            compiler_params=_params("arbitrary"),
            name="moe_combine",
        )(*args)
    return out


def _reorder_w_in(w_in):
    split = 256 + 768 + 512 + 768
    return jnp.concatenate([w_in[:, split:], w_in[:, :split]], axis=1).astype(BF16)


def _mods(mod_row_block):
    return [mod_row_block[:, None, k * D_MODEL:(k + 1) * D_MODEL] for k in range(6)]


def _moe_block(h2, h2p, logits, lp, x1, g2, final_g, *, rows_per_mod, final):
    picks = _router(logits, lp['router_b'])
    return _moe(h2, h2p, picks, lp, x1, g2, final_g, rows_per_mod=rows_per_mod, final=final)


def _layer(x2d, xc2d, c16, lp, layer_idx, tables, final_g, *, nbatch, seq, n_ctx, with_ctx_out, final):
    lambda_init = 0.8 - 0.6 * math.exp(-0.3 * layer_idx)
    mod = _ada_mod(c16, lp['ada_w'].astype(F32), lp['ada_b'].astype(F32))
    sh1, sc1, g1, sh2, sc2, g2 = _mods(mod[0:nbatch])
    csh1, csc1, cg1, csh2, csc2, cg2 = _mods(mod[nbatch:nbatch + 1])
    w_in = _reorder_w_in(lp['w_in'])
    rows_lat = nbatch * seq
    rows_ctx = nbatch * n_ctx

    gate, u, na, sw, df = _inproj(x2d, lp['norm1_g'], sc1, sh1, w_in, tables,
                                  rows_per_mod=seq, rope=True, seq=seq)
    gate_c, u_c, na_c, sw_c, df_c = _inproj(xc2d, lp['norm1_g'], csc1, csh1, w_in, tables,
                                            rows_per_mod=rows_ctx, rope=False, seq=seq)

    win, wout, a_re, a_im = _s5_params(lp['s5_lambda_re'], lp['s5_lambda_im'], lp['s5_log_step'],
                                       lp['s5_b_re'], lp['s5_b_im'], lp['s5_c_re'], lp['s5_c_im'])
    u_tm = jnp.concatenate([u_c.reshape(nbatch, n_ctx, 256).transpose(1, 0, 2),
                            u.reshape(nbatch, seq, 256).transpose(1, 0, 2)], axis=0)
    y_tm = _s5_scan(u_tm, win, wout, a_re, a_im, n_ctx)
    y_bm = y_tm.transpose(0, 2, 1, 3)

    lqk = [lp[k].reshape(1, DIFF_QK_DIM).astype(F32) for k in ('diff_lq1', 'diff_lk1', 'diff_lq2', 'diff_lk2')]
    bias = _na_bias_table(lp['na_rpb'], seq // GRID_W)
    yb = _na_attention(na, na_c, bias, nbatch, seq, n_ctx)
    yc = _swa_attention(sw, sw_c, lp['swa_sink'], nbatch, seq, n_ctx)
    yd = _diff_attention(df, df_c, lqk, lp['diff_subln_g'], lambda_init, nbatch, seq, n_ctx)

    merge_w = (lp['s5_d'], lp['s5_glu_w'], lp['s5_glu_b'])
    x1, h2, h2p, logits = _merge(u, y_bm, 0, seq, *merge_w, yb, yc, yd, gate, lp['w_branch'], lp['w_out'], x2d, g1,
                                 lp['norm2_g'], sc2, sh2, lp['router_w'], rows_per_mod=seq)
    x_out = _moe_block(h2, h2p, logits, lp, x1, g2, final_g, rows_per_mod=seq, final=final)

    xc_out = None
    if with_ctx_out:
        yb_c, yc_c, yd_c = _ctx_attention(na_c, sw_c, df_c, lp['swa_sink'], lqk, lp['diff_subln_g'],
                                          lambda_init, nbatch, n_ctx)
        x1c, h2c, h2pc, logits_c = _merge(u_c, y_bm, seq, n_ctx, *merge_w, yb_c, yc_c, yd_c, gate_c,
                                          lp['w_branch'], lp['w_out'], xc2d, cg1, lp['norm2_g'], csc2, csh2,
                                          lp['router_w'], rows_per_mod=rows_ctx)
        xc_out = _moe_block(h2c, h2pc, logits_c, lp, x1c, cg2, final_g, rows_per_mod=rows_ctx, final=False)
    return x_out, xc_out


def kernel(x, c, ctx, c_ctx, ada_w, ada_b, norm1_g, norm2_g, w_in, s5_lambda_re, s5_lambda_im, s5_log_step,
           s5_b_re, s5_b_im, s5_c_re, s5_c_im, s5_d, s5_glu_w, s5_glu_b, na_rpb, swa_sink, diff_lq1, diff_lk1,
           diff_lq2, diff_lk2, diff_subln_g, w_branch, w_out, router_w, router_b, exp_w_gate, exp_w_up,
           exp_w_down, sh_w_gate, sh_w_up, sh_w_down, final_g):
    nbatch, seq, d = x.shape
    n_ctx = ctx.shape[1]
    depth = ada_w.shape[0]
    assert d == D_MODEL and nbatch == 8
    stacked = dict(ada_w=ada_w, ada_b=ada_b, norm1_g=norm1_g, norm2_g=norm2_g, w_in=w_in,
                   s5_lambda_re=s5_lambda_re, s5_lambda_im=s5_lambda_im, s5_log_step=s5_log_step,
                   s5_b_re=s5_b_re, s5_b_im=s5_b_im, s5_c_re=s5_c_re, s5_c_im=s5_c_im, s5_d=s5_d,
                   s5_glu_w=s5_glu_w, s5_glu_b=s5_glu_b, na_rpb=na_rpb, swa_sink=swa_sink,
                   diff_lq1=diff_lq1, diff_lk1=diff_lk1, diff_lq2=diff_lq2, diff_lk2=diff_lk2,
                   diff_subln_g=diff_subln_g, w_branch=w_branch, w_out=w_out, router_w=router_w,
                   router_b=router_b, exp_w_gate=exp_w_gate, exp_w_up=exp_w_up, exp_w_down=exp_w_down,
                   sh_w_gate=sh_w_gate, sh_w_up=sh_w_up, sh_w_down=sh_w_down)
    tables = _rope_tables(seq)
    c16 = jnp.concatenate([c.astype(F32), c_ctx.reshape(1, d).astype(F32),
                           jnp.zeros((16 - nbatch - 1, d), F32)], axis=0)
    x2d = x.reshape(nbatch * seq, d).astype(F32)
    xc2d = ctx.reshape(nbatch * n_ctx, d).astype(F32)
    for l in range(depth):
        routed = ('exp_w_gate', 'exp_w_up', 'exp_w_down')
        lp = {k: (v.astype(F32) if k in routed else v[l]) for k, v in stacked.items()}
        lp['layer'] = l
        last = l == depth - 1
        x2d, xc2d = _layer(x2d, xc2d, c16, lp, l, tables, final_g, nbatch=nbatch, seq=seq, n_ctx=n_ctx,
                           with_ctx_out=not last, final=last)
    return x2d.reshape(nbatch, seq, d)
```

```python
import functools
import math

import numpy as np
import jax
import jax.numpy as jnp
from jax import lax
from jax.experimental import pallas as pl
from jax.experimental.pallas import tpu as pltpu
from jax.experimental.pallas import tpu_sc as plsc

F32 = jnp.float32
BF16 = jnp.bfloat16
HIGHEST = lax.Precision.HIGHEST

GRID_W = 64
EPS = 1e-6
NEG_INF = -1e30
ROPE_BASE = 10000.0
D_MODEL = 1024
BRANCH_WIDTH = 256
HEAD_DIM = 64
S5_GROUP = 16
S5_GROUPS = 16
S5_STATE = 64
S5_FLAT = S5_GROUPS * S5_STATE
NA_HEADS = 4
NA_WIN_ROWS = 8
NA_WIN_COLS = 16
SWA_KV_HEADS = 2
SWA_WINDOW = 128
DIFF_HEADS = 4
DIFF_QK_DIM = 32
N_EXPERTS = 64
N_EXPERT_GROUPS = 8
TOPK_GROUPS = 4
TOP_K = 6
EXPERT_HIDDEN = 256
ROUTED_SCALE = 2.5
GATE_WIDTH = 4 * D_MODEL

VMEM_LIMIT = 56 * 1024 * 1024


def _params(*sem):
    return pltpu.CompilerParams(dimension_semantics=sem, vmem_limit_bytes=VMEM_LIMIT)


def _nt_dot(a, b):
    return lax.dot_general(a, b, (((1,), (1,)), ((), ())), preferred_element_type=F32)


def _dot(a, b):
    return jnp.dot(a, b, preferred_element_type=F32)


def _rms(x):
    return x * lax.rsqrt(jnp.mean(x * x, axis=-1, keepdims=True) + EPS)


def _ada_kernel(c_ref, w_ref, b_ref, o_ref):
    c = c_ref[...]
    s = c * jax.nn.sigmoid(c)
    o_ref[...] = jnp.dot(s, w_ref[...], preferred_element_type=F32, precision=HIGHEST) + b_ref[...]


def _ada_mod(cc, w, b):
    rows, d = cc.shape
    width = w.shape[1]
    tn = 1536
    return pl.pallas_call(
        _ada_kernel,
        grid=(width // tn,),
        in_specs=[pl.BlockSpec((rows, d), lambda j: (0, 0)),
                  pl.BlockSpec((d, tn), lambda j: (0, j)),
                  pl.BlockSpec((1, tn), lambda j: (0, j))],
        out_specs=pl.BlockSpec((rows, tn), lambda j: (0, j)),
        out_shape=jax.ShapeDtypeStruct((rows, width), F32),
        compiler_params=_params("arbitrary"),
        name="ada_mod",
    )(cc, w, b.reshape(1, width))


_C_GATE = 0
_C_U = GATE_WIDTH
_C_NA = _C_U + 256
_C_SW = _C_NA + 768
_C_DF = _C_SW + 512
_C_END = _C_DF + 768


def _rope_apply(x, cos, sins, half):
    outs = []
    for j in range(x.shape[1] // 128):
        xs = x[:, j * 128:(j + 1) * 128]
        lane = lax.broadcasted_iota(jnp.int32, xs.shape, 1)
        lo = (lane % (2 * half)) < half
        partner = jnp.where(lo, pltpu.roll(xs, 128 - half, 1), pltpu.roll(xs, half, 1))
        outs.append(xs * cos + partner * sins)
    return outs[0] if len(outs) == 1 else jnp.concatenate(outs, axis=1)


def _inproj_kernel(x_ref, g_ref, sc_ref, sh_ref, w_ref, c64_ref, s64_ref, c32_ref, s32_ref,
                   gate_o, u_o, na_o, sw_o, df_o, *, rope):
    h = _rms(x_ref[...]) * g_ref[...]
    h = h * (1.0 + sc_ref[0]) + sh_ref[0]
    hb = h.astype(BF16)

    def mm(c0, c1):
        return _dot(hb, w_ref[:, c0:c1])

    for k in range(GATE_WIDTH // 512):
        gate_o[:, k * 512:(k + 1) * 512] = jax.nn.sigmoid(mm(k * 512, (k + 1) * 512)).astype(BF16)
    u_o[...] = mm(_C_U, _C_U + 256)

    na = mm(_C_NA, _C_NA + 768)
    na_o[:, 0:256] = (na[:, 0:256] * (HEAD_DIM ** -0.5 * (LOG2E if rope else 1.0))).astype(BF16)
    na_o[:, 256:768] = na[:, 256:768].astype(BF16)

    sw = mm(_C_SW, _C_SW + 512)
    swq, swk = sw[:, 0:256], sw[:, 256:384]
    if rope:
        swq = _rope_apply(swq, c64_ref[...], s64_ref[...], 16)
        swk = _rope_apply(swk, c64_ref[...], s64_ref[...], 16)
    sw_o[:, 0:256] = (swq * (HEAD_DIM ** -0.5 * (LOG2E if rope else 1.0))).astype(BF16)
    sw_o[:, 256:384] = swk.astype(BF16)
    sw_o[:, 384:512] = sw[:, 384:512].astype(BF16)

    df = mm(_C_DF, _C_DF + 768)
    dfq, dfk = df[:, 0:256], df[:, 256:512]
    if rope:
        dfq = _rope_apply(dfq, c32_ref[...], s32_ref[...], 8)
        dfk = _rope_apply(dfk, c32_ref[...], s32_ref[...], 8)
    df_o[:, 0:256] = (dfq * (DIFF_QK_DIM ** -0.5 * (LOG2E if rope else 1.0))).astype(BF16)
    df_o[:, 256:512] = dfk.astype(BF16)
    df_o[:, 512:768] = df[:, 512:768].astype(BF16)


def _inproj(x2d, norm_g, sc, sh, w_bf16, tables, *, rows_per_mod, rope, seq):
    rows = x2d.shape[0]
    tm = 512
    assert rows % tm == 0 and rows_per_mod % tm == 0 and seq % tm == 0
    tiles_per_seq = seq // tm

    def mod_map(i):
        return ((i * tm) // rows_per_mod, 0, 0)

    def tab_map(i):
        return (i % tiles_per_seq, 0)

    tab_spec = pl.BlockSpec((tm, 128), tab_map)
    row = lambda w: pl.BlockSpec((tm, w), lambda i: (i, 0))
    return pl.pallas_call(
        functools.partial(_inproj_kernel, rope=rope),
        grid=(rows // tm,),
        in_specs=[row(D_MODEL),
                  pl.BlockSpec((1, D_MODEL), lambda i: (0, 0)),
                  pl.BlockSpec((1, 1, D_MODEL), mod_map),
                  pl.BlockSpec((1, 1, D_MODEL), mod_map),
                  pl.BlockSpec((D_MODEL, _C_END), lambda i: (0, 0)),
                  tab_spec, tab_spec, tab_spec, tab_spec],
        out_specs=[row(GATE_WIDTH), row(256), row(768), row(512), row(768)],
        out_shape=[jax.ShapeDtypeStruct((rows, GATE_WIDTH), BF16),
                   jax.ShapeDtypeStruct((rows, 256), F32),
                   jax.ShapeDtypeStruct((rows, 768), BF16),
                   jax.ShapeDtypeStruct((rows, 512), BF16),
                   jax.ShapeDtypeStruct((rows, 768), BF16)],
        compiler_params=_params("arbitrary"),
        name="inproj",
    )(x2d, norm_g.reshape(1, D_MODEL), sc, sh, w_bf16, *tables)


def _rope_tables(seq):
    t = jnp.arange(seq)
    rows = (t // GRID_W).astype(F32)
    cols = (t % GRID_W).astype(F32)
    lane = np.arange(128)
    out = []
    for dim in (64, 32):
        quarter = dim // 4
        inv_freq = ROPE_BASE ** (-jnp.arange(quarter, dtype=F32) / quarter)
        l = lane % dim
        use_col = l >= dim // 2
        fidx = l % quarter
        hi = (l % (dim // 2)) >= quarter
        ang_r = rows[:, None] * inv_freq[None, :]
        ang_c = cols[:, None] * inv_freq[None, :]
        ang = jnp.where(use_col[None, :], ang_c[:, fidx], ang_r[:, fidx])
        out.append(jnp.cos(ang))
        out.append(jnp.where(hi[None, :], jnp.sin(ang), -jnp.sin(ang)))
    return tuple(out)


S5_CHUNK = 128


def _s5_kernel(u_ref, win_ref, wout_ref, are_ref, aim_ref, y_ref, bu_ref, st_ref, *, tc, nb):
    d = pl.program_id(0)
    i = pl.program_id(1)

    @pl.when(i == 0)
    def _():
        st_ref[...] = jnp.zeros_like(st_ref)

    u = u_ref[...].reshape(tc * nb, BRANCH_WIDTH).astype(BF16)
    bu_ref[...] = _dot(u, win_ref[0])
    ar = jnp.broadcast_to(are_ref[0], (nb, S5_FLAT))
    ai = jnp.broadcast_to(aim_ref[0], (nb, S5_FLAT))

    def body(j, carry):
        xr, xi = carry
        t = j + d * (tc - 1 - 2 * j)
        row = pl.multiple_of(t * nb, nb)
        br = bu_ref[pl.ds(row, nb), 0:S5_FLAT]
        bi = bu_ref[pl.ds(row, nb), S5_FLAT:2 * S5_FLAT]
        nr = ar * xr - ai * xi + br
        ni = ar * xi + ai * xr + bi
        bu_ref[pl.ds(row, nb), 0:S5_FLAT] = nr
        bu_ref[pl.ds(row, nb), S5_FLAT:2 * S5_FLAT] = ni
        return nr, ni

    xr, xi = lax.fori_loop(0, tc, body, (st_ref[:, 0:S5_FLAT], st_ref[:, S5_FLAT:2 * S5_FLAT]), unroll=4)
    st_ref[:, 0:S5_FLAT] = xr
    st_ref[:, S5_FLAT:2 * S5_FLAT] = xi
    y = _dot(bu_ref[...].astype(BF16), wout_ref[0])
    y_ref[0] = y.reshape(tc, nb, BRANCH_WIDTH)


def _s5_scan(u_tm, win, wout, a_re, a_im, n_ctx):
    s_len, nb, _ = u_tm.shape
    tc = S5_CHUNK
    assert nb == 8 and s_len % tc == 0 and n_ctx % tc == 0
    nct = n_ctx // tc
    nlt = (s_len - n_ctx) // tc

    def chunk(d, i):
        rev = jnp.where(i < nct, nct - 1 - i, 2 * nct + nlt - 1 - i)
        return jnp.where(d == 0, i, rev)

    def out_chunk(d, i):
        c = chunk(d, i)
        return jnp.where(c < nct, nlt + c, c - nct)

    return pl.pallas_call(
        functools.partial(_s5_kernel, tc=tc, nb=nb),
        grid=(2, nct + nlt),
        in_specs=[pl.BlockSpec((tc, nb, BRANCH_WIDTH), lambda d, i: (chunk(d, i), 0, 0)),
                  pl.BlockSpec((1, BRANCH_WIDTH, 2 * S5_FLAT), lambda d, i: (d, 0, 0)),
                  pl.BlockSpec((1, 2 * S5_FLAT, BRANCH_WIDTH), lambda d, i: (d, 0, 0)),
                  pl.BlockSpec((1, 1, S5_FLAT), lambda d, i: (d, 0, 0)),
                  pl.BlockSpec((1, 1, S5_FLAT), lambda d, i: (d, 0, 0))],
        out_specs=pl.BlockSpec((1, tc, nb, BRANCH_WIDTH), lambda d, i: (d, out_chunk(d, i), 0, 0)),
        out_shape=jax.ShapeDtypeStruct((2, s_len, nb, BRANCH_WIDTH), F32),
        scratch_shapes=[pltpu.VMEM((tc * nb, 2 * S5_FLAT), F32),
                        pltpu.VMEM((nb, 2 * S5_FLAT), F32)],
        compiler_params=_params("arbitrary", "arbitrary"),
        name="s5_scan",
    )(u_tm, win, wout, a_re, a_im)


def _s5_params(lam_re, lam_im, log_step, b_re, b_im, c_re, c_im):
    lr = lam_re.astype(F32)
    li = lam_im.astype(F32)
    dt = jnp.exp(log_step.astype(F32))[..., None]
    mag = jnp.exp(lr * dt)
    a_re = mag * jnp.cos(li * dt)
    a_im = mag * jnp.sin(li * dt)
    nr, ni, den = a_re - 1.0, a_im, lr * lr + li * li
    k_re = ((nr * lr + ni * li) / den)[..., None]
    k_im = ((ni * lr - nr * li) / den)[..., None]
    br = b_re.astype(F32)
    bi = b_im.astype(F32)
    bb_re = k_re * br - k_im * bi
    bb_im = k_re * bi + k_im * br
    eye = jnp.eye(S5_GROUPS, dtype=F32)

    def blockdiag_in(bb):
        m = jnp.einsum('dgpc,gh->dgchp', bb, eye)
        return m.reshape(2, S5_GROUPS * S5_GROUP, S5_GROUPS * S5_STATE)

    def blockdiag_out(cc):
        m = jnp.einsum('dgcp,gh->dgphc', cc, eye)
        return m.reshape(2, S5_GROUPS * S5_STATE, S5_GROUPS * S5_GROUP)

    win = jnp.concatenate([blockdiag_in(bb_re), blockdiag_in(bb_im)], axis=2).astype(BF16)
    wout = jnp.concatenate([blockdiag_out(c_re.astype(F32)), -blockdiag_out(c_im.astype(F32))], axis=1).astype(BF16)
    return win, wout, a_re.reshape(2, 1, S5_FLAT), a_im.reshape(2, 1, S5_FLAT)


def _softmax_parts(scores, extra=None):
    m = scores[0].max(axis=-1, keepdims=True)
    for s in scores[1:]:
        m = jnp.maximum(m, s.max(axis=-1, keepdims=True))
    if extra is not None:
        m = jnp.maximum(m, extra)
    ps = [jnp.exp(s - m) for s in scores]
    l = ps[0].sum(axis=-1, keepdims=True)
    for p in ps[1:]:
        l = l + p.sum(axis=-1, keepdims=True)
    if extra is not None:
        l = l + jnp.exp(extra - m)
    return ps, l


NA_QROWS = 2
NA_KROWS = 10
VROWS = HEAD_DIM + 16


def _na_window_start(r, grid_rows):
    start = jnp.clip(r - NA_WIN_ROWS // 2, 0, grid_rows - NA_WIN_ROWS)
    return (jnp.minimum(start, grid_rows - NA_KROWS) // 2) * 2


def _head_blockdiag(qt, qbd_ref, nheads, rows_per_head):
    n = qt.shape[1]
    row_h = lax.broadcasted_iota(jnp.int32, qt.shape, 0) // rows_per_head
    zero = jnp.zeros_like(qt)
    for h in range(nheads):
        qbd_ref[:, h * n:(h + 1) * n] = jnp.where(row_h == h, qt, zero)


def _na_scores(qt_ref, k_ref, kc_ref, bias_ref, qbd_ref, *, grid_rows):
    nk = NA_KROWS * GRID_W
    off = pl.multiple_of(_na_window_start(NA_QROWS * pl.program_id(1), grid_rows) * GRID_W, 128)
    _head_blockdiag(qt_ref[0], qbd_ref, NA_HEADS, HEAD_DIM)
    qbd = qbd_ref[...]
    s_loc = _dot(k_ref[0, pl.ds(off, nk), :], qbd) + bias_ref[0]
    s_ctx = _dot(kc_ref[0], qbd)
    return off, s_loc, s_ctx


def _na_values(off, s_loc, s_ctx, vt_ref, vct_ref):
    nq = NA_QROWS * GRID_W
    m = jnp.maximum(s_loc.max(axis=0, keepdims=True), s_ctx.max(axis=0, keepdims=True))
    p_loc = jnp.exp2((s_loc - m).astype(BF16))
    p_ctx = jnp.exp2((s_ctx - m).astype(BF16))
    vw = vt_ref[0, :, pl.ds(off, NA_KROWS * GRID_W)]
    outs = []
    for h in range(NA_HEADS):
        rows = slice(h * VROWS, (h + 1) * VROWS)
        cols = slice(h * nq, (h + 1) * nq)
        o = _dot(vw[rows], p_loc[:, cols]) + _dot(vct_ref[0, rows, :], p_ctx[:, cols])
        outs.append(o[0:HEAD_DIM] / o[HEAD_DIM:HEAD_DIM + 1])
    return jnp.concatenate(outs, axis=0).T.astype(BF16)


def _na_classes(grid_rows):
    return [0, 2, 4, grid_rows - 4, grid_rows - 2]


def _na_bias_table(rpb, grid_rows):
    col = np.arange(GRID_W)
    cstart = np.clip(col - NA_WIN_COLS // 2, 0, GRID_W - NA_WIN_COLS)
    col_in = (col[None, :] >= cstart[:, None]) & (col[None, :] < cstart[:, None] + NA_WIN_COLS)
    cb = np.clip(col[None, :] - col[:, None] + (NA_WIN_COLS - 1), 0, 2 * NA_WIN_COLS - 2)
    classes = _na_classes(grid_rows)
    rbi = np.zeros((len(classes), NA_QROWS, NA_KROWS), np.int64)
    row_in = np.zeros((len(classes), NA_QROWS, NA_KROWS), bool)
    for c, r in enumerate(classes):
        a_row = (min(int(np.clip(r - NA_WIN_ROWS // 2, 0, grid_rows - NA_WIN_ROWS)), grid_rows - NA_KROWS) // 2) * 2
        for qi in range(NA_QROWS):
            start_q = int(np.clip(r + qi - NA_WIN_ROWS // 2, 0, grid_rows - NA_WIN_ROWS))
            for j in range(NA_KROWS):
                row_in[c, qi, j] = start_q <= a_row + j < start_q + NA_WIN_ROWS
                rbi[c, qi, j] = np.clip(a_row + j - (r + qi) + NA_WIN_ROWS - 1, 0, 2 * NA_WIN_ROWS - 2)
    oh_row = jnp.asarray(rbi[..., None] == np.arange(2 * NA_WIN_ROWS - 1), F32)
    oh_col = jnp.asarray(cb[:, :, None] == np.arange(2 * NA_WIN_COLS - 1), F32)
    t = jnp.einsum('hab,cija,qkb->cjkhiq', rpb.astype(F32), oh_row, oh_col, precision=HIGHEST) * LOG2E
    valid = row_in.transpose(0, 2, 1)[:, :, None, None, :, None] & col_in.T[None, None, :, None, None, :]
    t = jnp.where(valid, t, NEG_INF)
    return t.reshape(len(classes), NA_KROWS * GRID_W, NA_HEADS * NA_QROWS * GRID_W)


def _augment_vt(v, nheads):
    nb, s, _ = v.shape
    v = v.reshape(nb, s, nheads, HEAD_DIM)
    pad = jnp.zeros((nb, s, nheads, VROWS - HEAD_DIM), v.dtype).at[..., 0].set(1.0)
    return jnp.concatenate([v, pad], axis=-1).reshape(nb, s, nheads * VROWS).transpose(0, 2, 1)


SWA_BLOCK = 128


def _swa_scores(qt_ref, k_ref, kc_ref, qbd_ref, *, seq):
    n = pl.program_id(1)
    band = 3 * SWA_BLOCK
    nq = SWA_BLOCK
    nqh = 2 * SWA_KV_HEADS
    bstart = pl.multiple_of(jnp.clip((n - 1) * SWA_BLOCK, 0, seq - band), SWA_BLOCK)
    qt = qt_ref[0]
    zero = jnp.zeros((HEAD_DIM, nq), qt.dtype)
    for hq in range(nqh):
        qh = qt[hq * HEAD_DIM:(hq + 1) * HEAD_DIM]
        qbd_ref[:, hq * nq:(hq + 1) * nq] = jnp.concatenate([qh, zero] if hq // 2 == 0 else [zero, qh], axis=0)
    qbd = qbd_ref[...]
    kpos = bstart + lax.broadcasted_iota(jnp.int32, (band, nqh * nq), 0)
    qpos = n * SWA_BLOCK + lax.broadcasted_iota(jnp.int32, (band, nqh * nq), 1) % nq
    s_loc = jnp.where(jnp.abs(qpos - kpos) <= SWA_WINDOW, _dot(k_ref[0, pl.ds(bstart, band), :], qbd), NEG_INF)
    s_ctx = _dot(kc_ref[0], qbd)
    return bstart, s_loc, s_ctx


def _swa_values(bstart, s_loc, s_ctx, vt_ref, vct_ref, sink_ref):
    nq = SWA_BLOCK
    nqh = 2 * SWA_KV_HEADS
    sink = jnp.concatenate([jnp.broadcast_to(sink_ref[0:1, hq:hq + 1] * LOG2E, (1, nq)) for hq in range(nqh)], axis=1)
    m = jnp.maximum(jnp.maximum(s_loc.max(axis=0, keepdims=True), s_ctx.max(axis=0, keepdims=True)), sink)
    p_loc = jnp.exp2((s_loc - m).astype(BF16))
    p_ctx = jnp.exp2((s_ctx - m).astype(BF16))
    p_sink = jnp.exp2(sink - m)
    vw = vt_ref[0, :, pl.ds(bstart, 3 * SWA_BLOCK)]
    outs = []
    for hq in range(nqh):
        rows = slice((hq // 2) * VROWS, (hq // 2 + 1) * VROWS)
        cols = slice(hq * nq, (hq + 1) * nq)
        o = _dot(vw[rows], p_loc[:, cols]) + _dot(vct_ref[0, rows, :], p_ctx[:, cols])
        outs.append(o[0:HEAD_DIM] / (o[HEAD_DIM:HEAD_DIM + 1] + p_sink[:, cols]))
    return jnp.concatenate(outs, axis=0).T.astype(BF16)


def _local_attn_kernel(na_qt, na_k, na_vt, na_kc, na_vct, bias_ref, sw_qt, sw_k, sw_vt, sw_kc, sw_vct, sink_ref,
                       na_o, sw_o, na_qbd, sw_qbd, *, grid_rows, seq):
    na_s = _na_scores(na_qt, na_k, na_kc, bias_ref, na_qbd, grid_rows=grid_rows)
    sw_s = _swa_scores(sw_qt, sw_k, sw_kc, sw_qbd, seq=seq)
    na_o[...] = _na_values(*na_s, na_vt, na_vct)
    sw_o[...] = _swa_values(*sw_s, sw_vt, sw_vct, sink_ref)


def _local_attention(na, na_c, bias, sw, sw_c, sink, nbatch, seq, n_ctx):
    grid_rows = seq // GRID_W
    nq = NA_QROWS * GRID_W
    nstep = seq // nq
    nqh = 2 * SWA_KV_HEADS
    assert grid_rows >= NA_KROWS and grid_rows % NA_QROWS == 0 and nq == SWA_BLOCK and seq >= 3 * SWA_BLOCK
    nal = na.reshape(nbatch, seq, 768)
    nac = na_c.reshape(nbatch, n_ctx, 768)
    na_qt = nal[:, :, 0:256].transpose(0, 2, 1)
    na_vt = _augment_vt(nal[:, :, 512:768], NA_HEADS)
    na_vct = _augment_vt(nac[:, :, 512:768], NA_HEADS)
    sink_pad = jnp.zeros((1, 128), F32).at[0, 0:nqh].set(sink.astype(F32))
    swl = sw.reshape(nbatch, seq, 512)
    swc = sw_c.reshape(nbatch, n_ctx, 512)
    sw_qt = swl[:, :, 0:256].transpose(0, 2, 1)
    sw_vt = _augment_vt(swl[:, :, 384:512], SWA_KV_HEADS)
    sw_vct = _augment_vt(swc[:, :, 384:512], SWA_KV_HEADS)

    def cls(b, p):
        r = NA_QROWS * p
        c = jnp.where(r < 4, r // 2, jnp.where(r >= grid_rows - 4, (r - (grid_rows - 4)) // 2 + 3, 2))
        return (c, 0, 0)

    out = jax.ShapeDtypeStruct((nbatch * seq, 256), BF16)
    out_spec = pl.BlockSpec((nq, 256), lambda b, p: (b * nstep + p, 0))
    return pl.pallas_call(
        functools.partial(_local_attn_kernel, grid_rows=grid_rows, seq=seq),
        grid=(nbatch, nstep),
        in_specs=[pl.BlockSpec((1, 256, nq), lambda b, p: (b, 0, p)),
                  pl.BlockSpec((1, seq, 256), lambda b, p: (b, 0, 1)),
                  pl.BlockSpec((1, NA_HEADS * VROWS, seq), lambda b, p: (b, 0, 0)),
                  pl.BlockSpec((1, n_ctx, 256), lambda b, p: (b, 0, 1)),
                  pl.BlockSpec((1, NA_HEADS * VROWS, n_ctx), lambda b, p: (b, 0, 0)),
                  pl.BlockSpec((1, NA_KROWS * GRID_W, NA_HEADS * nq), cls),
                  pl.BlockSpec((1, 256, nq), lambda b, p: (b, 0, p)),
                  pl.BlockSpec((1, seq, 128), lambda b, p: (b, 0, 2)),
                  pl.BlockSpec((1, SWA_KV_HEADS * VROWS, seq), lambda b, p: (b, 0, 0)),
                  pl.BlockSpec((1, n_ctx, 128), lambda b, p: (b, 0, 2)),
                  pl.BlockSpec((1, SWA_KV_HEADS * VROWS, n_ctx), lambda b, p: (b, 0, 0)),
                  pl.BlockSpec((1, 128), lambda b, p: (0, 0))],
        out_specs=[out_spec, out_spec],
        out_shape=[out, out],
        scratch_shapes=[pltpu.VMEM((256, NA_HEADS * nq), BF16),
                        pltpu.VMEM((SWA_KV_HEADS * HEAD_DIM, nqh * nq), BF16)],
        compiler_params=_params("arbitrary", "arbitrary"),
        name="local_attention",
    )(na_qt, nal, na_vt, nac, na_vct, bias, sw_qt, swl, sw_vt, swc, sw_vct, sink_pad)


DIFF_TQ = 512
DIFF_CK = 256
LOG2E = math.log2(math.e)
DIFF_VROWS = HEAD_DIM + 16


def _diff_lambda(lq1_ref, lk1_ref, lq2_ref, lk2_ref, lambda_init):
    s1 = jnp.sum(lq1_ref[...] * lk1_ref[...], axis=-1, keepdims=True)
    s2 = jnp.sum(lq2_ref[...] * lk2_ref[...], axis=-1, keepdims=True)
    return jnp.exp(s1) - jnp.exp(s2) + lambda_init


def _stack_maps(qh):
    lane = lax.broadcasted_iota(jnp.int32, qh.shape, 1)
    zero = jnp.zeros_like(qh)
    return jnp.concatenate([jnp.where(lane < DIFF_QK_DIM, qh, zero),
                            jnp.where(lane >= DIFF_QK_DIM, qh, zero)], axis=0)


def _subln(o0, o1, lam, g, lambda_init):
    o = o0 - lam * o1
    return _rms(o) * g * (1.0 - lambda_init)


def _diff_kernel(qt_ref, k_ref, vt_ref, lq1_ref, lk1_ref, lq2_ref, lk2_ref, g_ref, o_ref, qbd_ref, acc_ref, s_ref,
                 *, nchunk, lambda_init):
    lam = _diff_lambda(lq1_ref, lk1_ref, lq2_ref, lk2_ref, lambda_init)
    qt = qt_ref[0]
    tq = qt.shape[1]
    w = 2 * tq
    row = lax.broadcasted_iota(jnp.int32, qt.shape, 0) // DIFF_QK_DIM
    zero = jnp.zeros_like(qt)
    for j in range(2 * DIFF_HEADS):
        qbd_ref[:, j * tq:(j + 1) * tq] = jnp.where(row == j, qt, zero)
    acc_ref[...] = jnp.zeros_like(acc_ref)

    def scores(slot, c, h):
        s = _dot(k_ref[0, c], qbd_ref[:, h * w:(h + 1) * w])
        s_ref[slot, h] = s
        return s.max(axis=0, keepdims=True)

    def softmax_pv(slot, c, h, m_run, m_chunk):
        m_new = jnp.maximum(m_run, m_chunk)
        alpha = jnp.exp2(m_run - m_new)
        p = jnp.exp2((s_ref[slot, h] - m_new).astype(BF16))
        acc_ref[h] = alpha * acc_ref[h] + _dot(vt_ref[0, c, h * DIFF_VROWS:(h + 1) * DIFF_VROWS, :], p)
        return m_new

    def step(slot, c, carry):
        m_run, m_chunk = carry
        new_run, new_chunk = [], []
        for h in range(DIFF_HEADS):
            new_chunk.append(scores(1 - slot, c + 1, h))
            new_run.append(softmax_pv(slot, c, h, m_run[h], m_chunk[h]))
        return tuple(new_run), tuple(new_chunk)

    def body(i, carry):
        c = 2 * i
        return step(1, c + 1, step(0, c, carry))

    m_run = tuple(jnp.full((1, w), NEG_INF, F32) for _ in range(DIFF_HEADS))
    m_chunk = tuple(scores(0, 0, h) for h in range(DIFF_HEADS))
    npair = (nchunk - 1) // 2
    m_run, m_chunk = lax.fori_loop(0, npair, body, (m_run, m_chunk))
    if nchunk % 2 == 0:
        m_run, m_chunk = step(0, nchunk - 2, (m_run, m_chunk))
        for h in range(DIFF_HEADS):
            softmax_pv(1, nchunk - 1, h, m_run[h], m_chunk[h])
    else:
        for h in range(DIFF_HEADS):
            softmax_pv(0, nchunk - 1, h, m_run[h], m_chunk[h])
    outs = []
    for h in range(DIFF_HEADS):
        o = acc_ref[h, 0:HEAD_DIM, :] / acc_ref[h, HEAD_DIM:HEAD_DIM + 1, :]
        d = o[:, 0:tq] - lam * o[:, tq:w]
        d = d * lax.rsqrt(jnp.mean(d * d, axis=0, keepdims=True) + EPS)
        outs.append(d * g_ref[...] * (1.0 - lambda_init))
    o_ref[...] = jnp.concatenate(outs, axis=0).T.astype(BF16)


def _diff_attention(df, df_c, lqk, subln_g, lambda_init, nbatch, seq, n_ctx):
    tq, ck = min(DIFF_TQ, seq), DIFF_CK
    s_all = seq + n_ctx
    assert s_all % ck == 0 and seq % tq == 0
    nchunk = s_all // ck
    nq = seq // tq
    dfl = df.reshape(nbatch, seq, 768)
    dfc = df_c.reshape(nbatch, n_ctx, 768)
    qt = dfl[:, :, 0:256].transpose(0, 2, 1)
    k_all = jnp.concatenate([dfl[:, :, 256:512], dfc[:, :, 256:512]], axis=1).reshape(nbatch, nchunk, ck, 256)
    v_all = jnp.concatenate([dfl[:, :, 512:768], dfc[:, :, 512:768]], axis=1)
    v_all = v_all.reshape(nbatch, nchunk, ck, DIFF_HEADS, HEAD_DIM)
    pad = jnp.zeros((nbatch, nchunk, ck, DIFF_HEADS, DIFF_VROWS - HEAD_DIM), BF16).at[..., 0].set(1.0)
    vt_all = jnp.concatenate([v_all, pad], axis=-1).reshape(nbatch, nchunk, ck, DIFF_HEADS * DIFF_VROWS)
    vt_all = vt_all.transpose(0, 1, 3, 2)
    vec = pl.BlockSpec((1, DIFF_QK_DIM), lambda b, n: (0, 0))
    return pl.pallas_call(
        functools.partial(_diff_kernel, nchunk=nchunk, lambda_init=lambda_init),
        grid=(nbatch, nq),
        in_specs=[pl.BlockSpec((1, 256, tq), lambda b, n: (b, 0, n)),
                  pl.BlockSpec((1, nchunk, ck, 256), lambda b, n: (b, 0, 0, 0)),
                  pl.BlockSpec((1, nchunk, DIFF_HEADS * DIFF_VROWS, ck), lambda b, n: (b, 0, 0, 0)),
                  vec, vec, vec, vec,
                  pl.BlockSpec((HEAD_DIM, 1), lambda b, n: (0, 0))],
        out_specs=pl.BlockSpec((tq, 256), lambda b, n: (b * nq + n, 0)),
        out_shape=jax.ShapeDtypeStruct((nbatch * seq, 256), BF16),
        scratch_shapes=[pltpu.VMEM((256, 2 * DIFF_HEADS * tq), BF16),
                        pltpu.VMEM((DIFF_HEADS, DIFF_VROWS, 2 * tq), F32),
                        pltpu.VMEM((2, DIFF_HEADS, ck, 2 * tq), F32)],
        compiler_params=_params("arbitrary", "arbitrary"),
        name="diff_attention",
    )(qt, k_all, vt_all, *lqk, subln_g.reshape(HEAD_DIM, 1).astype(F32))


def _ctx_attn_kernel(na_ref, sw_ref, df_ref, sink_ref, lq1_ref, lk1_ref, lq2_ref, lk2_ref, g_ref,
                     nb_o, sw_o, df_o, *, lambda_init):
    n = na_ref.shape[0]
    na = na_ref[...]
    outs = []
    for h in range(NA_HEADS):
        sl = slice(h * HEAD_DIM, (h + 1) * HEAD_DIM)
        (p,), l = _softmax_parts([_nt_dot(na[:, sl], na[:, 256 + h * HEAD_DIM:256 + (h + 1) * HEAD_DIM])])
        outs.append(_dot(p.astype(BF16), na[:, 512 + h * HEAD_DIM:512 + (h + 1) * HEAD_DIM]) / l)
    nb_o[...] = jnp.concatenate(outs, axis=1).astype(BF16)
    sw = sw_ref[...]
    outs = []
    for hq in range(4):
        kv = hq // 2
        k = sw[:, 256 + kv * HEAD_DIM:256 + (kv + 1) * HEAD_DIM]
        v = sw[:, 384 + kv * HEAD_DIM:384 + (kv + 1) * HEAD_DIM]
        sk = jnp.broadcast_to(sink_ref[0:1, hq:hq + 1], (n, 1))
        (p,), l = _softmax_parts([_nt_dot(sw[:, hq * HEAD_DIM:(hq + 1) * HEAD_DIM], k)], extra=sk)
        outs.append(_dot(p.astype(BF16), v) / l)
    sw_o[...] = jnp.concatenate(outs, axis=1).astype(BF16)
    lam = _diff_lambda(lq1_ref, lk1_ref, lq2_ref, lk2_ref, lambda_init)
    df = df_ref[...]
    outs = []
    for h in range(DIFF_HEADS):
        sl = slice(h * HEAD_DIM, (h + 1) * HEAD_DIM)
        q2 = _stack_maps(df[:, sl])
        (p,), l = _softmax_parts([_nt_dot(q2, df[:, 256 + h * HEAD_DIM:256 + (h + 1) * HEAD_DIM])])
        o = _dot(p.astype(BF16), df[:, 512 + h * HEAD_DIM:512 + (h + 1) * HEAD_DIM]) / l
        outs.append(_subln(o[0:n], o[n:2 * n], lam, g_ref[...], lambda_init))
    df_o[...] = jnp.concatenate(outs, axis=1).astype(BF16)


def _ctx_attention(na_c, sw_c, df_c, sink, lqk, subln_g, lambda_init, nbatch, n_ctx):
    sink_pad = jnp.zeros((1, 128), F32).at[0, 0:4].set(sink.astype(F32))
    vec = pl.BlockSpec((1, DIFF_QK_DIM), lambda b: (0, 0))
    out = jax.ShapeDtypeStruct((nbatch * n_ctx, 256), BF16)
    return pl.pallas_call(
        functools.partial(_ctx_attn_kernel, lambda_init=lambda_init),
        grid=(nbatch,),
        in_specs=[pl.BlockSpec((n_ctx, 768), lambda b: (b, 0)),
                  pl.BlockSpec((n_ctx, 512), lambda b: (b, 0)),
                  pl.BlockSpec((n_ctx, 768), lambda b: (b, 0)),
                  pl.BlockSpec((1, 128), lambda b: (0, 0)),
                  vec, vec, vec, vec,
                  pl.BlockSpec((1, HEAD_DIM), lambda b: (0, 0))],
        out_specs=[pl.BlockSpec((n_ctx, 256), lambda b: (b, 0))] * 3,
        out_shape=[out, out, out],
        compiler_params=_params("arbitrary"),
        name="ctx_attention",
    )(na_c, sw_c, df_c, sink_pad, *lqk, subln_g.reshape(1, HEAD_DIM).astype(F32))


def _merge_kernel(u_ref, yf_ref, yr_ref, d_ref, gw_ref, gb_ref, yb_ref, yc_ref, yd_ref, gate_ref,
                  wb_ref, wo_ref, x_ref, g1_ref, n2_ref, sc2_ref, sh2_ref, rw_ref,
                  x1_o, h2_o, h2p_o, lg_o):
    y = u_ref[...] * d_ref[...] + yf_ref[0, 0] + yr_ref[0, 0]
    a = jax.nn.gelu(y, approximate=True)
    ya = a * jax.nn.sigmoid(_dot(a.astype(BF16), gw_ref[...]) + gb_ref[...])
    branches = (ya.astype(BF16), yb_ref[...], yc_ref[...], yd_ref[...])
    acc = None
    for i in range(4):
        t = gate_ref[:, i * D_MODEL:(i + 1) * D_MODEL].astype(F32) * _dot(branches[i], wb_ref[i])
        acc = t if acc is None else acc + t
    mixed = _dot(acc.astype(BF16), wo_ref[...])
    x1 = x_ref[...] + g1_ref[0] * mixed
    x1_o[...] = x1
    h2 = _rms(x1) * n2_ref[...]
    h2 = h2 * (1.0 + sc2_ref[0]) + sh2_ref[0]
    h2_o[...] = h2.astype(BF16)
    h2p_o[...] = _pack_rows(h2[:, 0:HALF_D], h2[:, HALF_D:D_MODEL])
    lg_o[...] = lax.dot_general(rw_ref[...], h2, (((1,), (1,)), ((), ())), preferred_element_type=F32,
                                precision=HIGHEST)


def _merge(u, y_bm, y_start, rows_per_seq, s5_d, glu_w, glu_b, yb, yc, yd, gate, wb, wo, x2d, g1, norm2_g, sc2, sh2,
           router_w, *, rows_per_mod):
    rows = x2d.shape[0]
    tm = min(512, rows_per_seq)
    assert rows % tm == 0 and rows_per_mod % tm == 0 and rows_per_seq % tm == 0 and y_start % tm == 0
    tiles_per_seq = rows_per_seq // tm

    def mod_map(i):
        return ((i * tm) // rows_per_mod, 0, 0)

    def y_spec(d):
        return pl.BlockSpec((1, 1, tm, 256), lambda i: (d, i // tiles_per_seq, y_start // tm + i % tiles_per_seq, 0))

    row = lambda w: pl.BlockSpec((tm, w), lambda i: (i, 0))
    full = lambda *shape: pl.BlockSpec(shape, lambda i: (0,) * len(shape))
    mod = pl.BlockSpec((1, 1, D_MODEL), mod_map)
    return pl.pallas_call(
        _merge_kernel,
        grid=(rows // tm,),
        in_specs=[row(256), y_spec(0), y_spec(1), full(1, 256), full(256, 256), full(1, 256),
                  row(256), row(256), row(256), row(GATE_WIDTH),
                  full(4, 256, D_MODEL), full(D_MODEL, D_MODEL), row(D_MODEL),
                  mod, full(1, D_MODEL), mod, mod, full(N_EXPERTS, D_MODEL)],
        out_specs=[row(D_MODEL), row(D_MODEL), row(HALF_D), pl.BlockSpec((N_EXPERTS, tm), lambda i: (0, i))],
        out_shape=[jax.ShapeDtypeStruct((rows, D_MODEL), F32),
                   jax.ShapeDtypeStruct((rows, D_MODEL), BF16),
                   jax.ShapeDtypeStruct((rows, HALF_D), jnp.int32),
                   jax.ShapeDtypeStruct((N_EXPERTS, rows), F32)],
        compiler_params=_params("arbitrary"),
        name="merge",
    )(u, y_bm, y_bm, s5_d.reshape(1, 256).astype(F32), glu_w.astype(BF16), glu_b.reshape(1, 256).astype(F32),
      yb, yc, yd, gate, wb.astype(BF16), wo.astype(BF16), x2d, g1, norm2_g.reshape(1, D_MODEL), sc2, sh2,
      router_w.astype(F32).T)


def _router_kernel(lg_ref, b_ref, tri_ref, idx_ref, rank_ref, w_ref, cnt_ref, base_ref):
    tr = lg_ref.shape[1]
    gsz = N_EXPERTS // N_EXPERT_GROUPS
    sc = jax.nn.sigmoid(lg_ref[...])
    bi = sc + b_ref[...]
    e_iota = lax.broadcasted_iota(jnp.int32, (gsz, tr), 0).astype(F32)
    groups = [bi[g * gsz:(g + 1) * gsz] for g in range(N_EXPERT_GROUPS)]
    gs = []
    for bg in groups:
        m1 = bg.max(axis=0, keepdims=True)
        i1 = jnp.where(bg == m1, e_iota, float(gsz)).min(axis=0, keepdims=True)
        m2 = jnp.where(e_iota == i1, -jnp.inf, bg).max(axis=0, keepdims=True)
        gs.append(m1 + m2)
    v = []
    for g in range(N_EXPERT_GROUPS):
        rank = jnp.zeros((1, tr), F32)
        for g2 in range(N_EXPERT_GROUPS):
            if g2 == g:
                continue
            beats = (gs[g2] >= gs[g]) if g2 < g else (gs[g2] > gs[g])
            rank = rank + jnp.where(beats, 1.0, 0.0)
        v.append(jnp.where(rank < TOPK_GROUPS, groups[g], NEG_INF))
    flat = [e_iota + float(g * gsz) for g in range(N_EXPERT_GROUPS)]
    sel = [jnp.zeros((gsz, tr), F32) for _ in range(N_EXPERT_GROUPS)]
    picks = []
    for _ in range(TOP_K):
        m = v[0].max(axis=0, keepdims=True)
        for g in range(1, N_EXPERT_GROUPS):
            m = jnp.maximum(m, v[g].max(axis=0, keepdims=True))
        am = jnp.where(v[0] == m, flat[0], float(N_EXPERTS)).min(axis=0, keepdims=True)
        for g in range(1, N_EXPERT_GROUPS):
            am = jnp.minimum(am, jnp.where(v[g] == m, flat[g], float(N_EXPERTS)).min(axis=0, keepdims=True))
        hits = []
        for g in range(N_EXPERT_GROUPS):
            hit = flat[g] == am
            hits.append(hit)
            sel[g] = jnp.where(hit, 1.0, sel[g])
            v[g] = jnp.where(hit, -jnp.inf, v[g])
        picks.append((am, hits))
    scg = [sc[g * gsz:(g + 1) * gsz] for g in range(N_EXPERT_GROUPS)]
    den = (sel[0] * scg[0]).sum(axis=0, keepdims=True)
    for g in range(1, N_EXPERT_GROUPS):
        den = den + (sel[g] * scg[g]).sum(axis=0, keepdims=True)

    @pl.when(pl.program_id(0) == 0)
    def _():
        base_ref[...] = jnp.zeros_like(base_ref)

    sel_all = jnp.concatenate(sel, axis=0)
    before = _dot(sel_all.astype(jnp.bfloat16), tri_ref[...]) + base_ref[...]
    for k, (am, hits) in enumerate(picks):
        wk = jnp.zeros((1, tr), F32)
        rk = jnp.zeros((1, tr), F32)
        for g in range(N_EXPERT_GROUPS):
            wk = wk + jnp.where(hits[g], scg[g], 0.0).sum(axis=0, keepdims=True)
            rk = rk + jnp.where(hits[g], before[g * gsz:(g + 1) * gsz], 0.0).sum(axis=0, keepdims=True)
        idx_ref[k:k + 1, :] = am.astype(jnp.int32)
        rank_ref[k:k + 1, :] = rk.astype(jnp.int32)
        w_ref[k:k + 1, :] = wk / den * ROUTED_SCALE
    idx_ref[TOP_K:8, :] = jnp.zeros((8 - TOP_K, tr), jnp.int32)
    rank_ref[TOP_K:8, :] = jnp.zeros((8 - TOP_K, tr), jnp.int32)
    w_ref[TOP_K:8, :] = jnp.zeros((8 - TOP_K, tr), F32)
    base_ref[...] += sel_all.sum(axis=1, keepdims=True)
    cnt_ref[...] = base_ref[...].astype(jnp.int32)


ROUTER_TILE = 512


def _router(logits_t, router_b):
    ne, rows = logits_t.shape
    tr = ROUTER_TILE
    assert rows % tr == 0
    tri = jnp.asarray(np.triu(np.ones((tr, tr), np.float32), k=1), jnp.bfloat16)
    pick = pl.BlockSpec((8, tr), lambda i: (0, i))
    return pl.pallas_call(
        _router_kernel,
        grid=(rows // tr,),
        in_specs=[pl.BlockSpec((ne, tr), lambda i: (0, i)),
                  pl.BlockSpec((ne, 1), lambda i: (0, 0)),
                  pl.BlockSpec((tr, tr), lambda i: (0, 0))],
        out_specs=[pick, pick, pick, pl.BlockSpec((ne, 1), lambda i: (0, 0))],
        out_shape=[jax.ShapeDtypeStruct((8, rows), jnp.int32),
                   jax.ShapeDtypeStruct((8, rows), jnp.int32),
                   jax.ShapeDtypeStruct((8, rows), F32),
                   jax.ShapeDtypeStruct((ne, 1), jnp.int32)],
        scratch_shapes=[pltpu.VMEM((ne, 1), F32)],
        compiler_params=_params("arbitrary"),
        name="router",
    )(logits_t, router_b.reshape(ne, 1).astype(F32), tri)


MOE_BLOCK = 512
MOE_TOKENS = 256
MOE_COMBINE_GROUPS = 4
HALF_D = D_MODEL // 2


def _pack_rows(lo, hi):
    lo_b = pltpu.bitcast(lo.astype(jnp.bfloat16).astype(F32), jnp.uint32)
    hi_b = pltpu.bitcast(hi.astype(jnp.bfloat16).astype(F32), jnp.uint32)
    return pltpu.bitcast((hi_b & jnp.uint32(0xFFFF0000)) | (lo_b >> 16), jnp.int32)


def _unpack_rows(words):
    u = pltpu.bitcast(words, jnp.uint32)
    lo = pltpu.bitcast(u << 16, F32)
    hi = pltpu.bitcast(u & jnp.uint32(0xFFFF0000), F32)
    return lo, hi


def _swiglu(x_bf16, wgu, wd):
    hgu = _dot(x_bf16, wgu)
    g = hgu[:, 0:EXPERT_HIDDEN]
    a = g * jax.nn.sigmoid(g) * hgu[:, EXPERT_HIDDEN:2 * EXPERT_HIDDEN]
    return _dot(a.astype(BF16), wd)


SC_CORES = 2
SC_SUBCORES = 16
SC_STREAM_ROWS = 128


def _sc_for_each_chunk(total, body):
    chunk = SC_STREAM_ROWS
    assert total % chunk == 0
    nchunk = total // chunk
    per_worker = pl.cdiv(nchunk, SC_CORES * SC_SUBCORES)
    first = (lax.axis_index("s") * SC_CORES + lax.axis_index("c")) * per_worker

    @pl.loop(0, per_worker)
    def _(j):
        @pl.when(first + j < nchunk)
        def _():
            body((first + j) * chunk)


def _sc_scatter_rows(rows, slots, n_out):
    total, n = rows.shape
    chunk = SC_STREAM_ROWS
    mesh = plsc.VectorSubcoreMesh(core_axis_name="c", subcore_axis_name="s")

    @functools.partial(
        pl.kernel, mesh=mesh,
        out_type=jax.ShapeDtypeStruct((n_out, n), jnp.int32),
        scratch_types=[pltpu.VMEM((8, chunk), jnp.int32),
                       pltpu.VMEM((chunk, n), jnp.int32),
                       pltpu.SemaphoreType.DMA],
        name="moe_dispatch_sc",
    )
    def scatter(rows_hbm, slot_hbm, out_hbm, idx_v, rows_v, sem):
        def body(off):
            pltpu.sync_copy(rows_hbm.at[pl.ds(off, chunk)], rows_v)
            pltpu.sync_copy(slot_hbm.at[:, pl.ds(off, chunk)], idx_v)
            for k in range(TOP_K):
                pltpu.async_copy(rows_v, out_hbm.at[idx_v.at[k]], sem).wait()

        _sc_for_each_chunk(total, body)

    return scatter(rows, slots)


def _experts_kernel(be_ref, nv_ref, nb_ref, xs_ref, wg_ref, wu_ref, wd_ref, ys_ref, wgu_bf, wd_bf):
    b = pl.program_id(0)

    @pl.when(b < nb_ref[0])
    def _():
        @pl.when((b == 0) | (be_ref[b] != be_ref[jnp.maximum(b - 1, 0)]))
        def _():
            wgu_bf[:, 0:EXPERT_HIDDEN] = wg_ref[0, 0].astype(BF16)
            wgu_bf[:, EXPERT_HIDDEN:2 * EXPERT_HIDDEN] = wu_ref[0, 0].astype(BF16)
            wd_bf[...] = wd_ref[0, 0].astype(BF16)

        words = xs_ref[...]
        live = lax.broadcasted_iota(jnp.int32, words.shape, 0) < nv_ref[b]
        lo, hi = _unpack_rows(jnp.where(live, words, 0))
        x = jnp.concatenate([lo, hi], axis=1).astype(BF16)
        y = _swiglu(x, wgu_bf[...], wd_bf[...])
        ys_ref[...] = _pack_rows(y[:, 0:HALF_D], y[:, HALF_D:D_MODEL])


def _sc_gather_rows(table, indices):
    m, n = indices.shape[0], table.shape[1]
    chunk = SC_STREAM_ROWS
    mesh = plsc.VectorSubcoreMesh(core_axis_name="c", subcore_axis_name="s")

    @functools.partial(
        pl.kernel, mesh=mesh,
        out_type=jax.ShapeDtypeStruct((m, n), jnp.int32),
        scratch_types=[pltpu.VMEM((chunk,), jnp.int32),
                       pltpu.VMEM((chunk, n), jnp.int32),
                       pltpu.SemaphoreType.DMA],
        name="moe_gather_sc",
    )
    def gather(table_hbm, idx_hbm, out_hbm, idx_v, rows_v, sem):
        def body(off):
            pltpu.sync_copy(idx_hbm.at[pl.ds(off, chunk)], idx_v)
            pltpu.async_copy(table_hbm.at[idx_v], rows_v, sem).wait()
            pltpu.sync_copy(rows_v, out_hbm.at[pl.ds(off, chunk)])

        _sc_for_each_chunk(m, body)

    return gather(table, indices)


def _combine_into_kernel(w_ref, rows_ref, h_ref, wsgu_ref, wsd_ref, x1_ref, g2_ref, fg_ref, prev_ref, o_ref, *, final):
    del prev_ref
    _combine_kernel(w_ref, rows_ref, h_ref, wsgu_ref, wsd_ref, x1_ref, g2_ref, fg_ref, o_ref, final=final)


def _combine_kernel(w_ref, rows_ref, h_ref, wsgu_ref, wsd_ref, x1_ref, g2_ref, fg_ref, o_ref, *, final):
    shared = _swiglu(h_ref[...], wsgu_ref[...], wsd_ref[...])
    acc_lo = shared[:, 0:HALF_D]
    acc_hi = shared[:, HALF_D:D_MODEL]
    w = w_ref[...]
    for k in range(TOP_K):
        lo, hi = _unpack_rows(rows_ref[k])
        acc_lo = acc_lo + w[:, k:k + 1] * lo
        acc_hi = acc_hi + w[:, k:k + 1] * hi
    x2 = x1_ref[...] + g2_ref[0] * jnp.concatenate([acc_lo, acc_hi], axis=1)
    if final:
        x2 = _rms(x2) * fg_ref[...]
    o_ref[...] = x2


def _moe(h2, h2p, picks, lp, x1, g2, final_g, *, rows_per_mod, final):
    idx, rank, wsel, counts = picks
    rows = h2.shape[0]
    tt = MOE_TOKENS
    blk = MOE_BLOCK
    assert rows % tt == 0 and rows_per_mod % tt == 0 and (rows * TOP_K) % blk == 0
    ntile = rows // tt
    nblock = rows * TOP_K // blk + N_EXPERTS

    cnt = counts.reshape(N_EXPERTS)
    padded = (cnt + blk - 1) // blk * blk
    e_ids = jnp.arange(N_EXPERTS, dtype=jnp.int32)
    pends = jnp.sum(jnp.where(e_ids[None, :] <= e_ids[:, None], padded[None, :], 0), axis=1)
    pstart = (pends - padded).astype(jnp.int32)
    nb_used = (jnp.sum(padded) // blk).astype(jnp.int32).reshape(1)
    first_row = jnp.arange(nblock, dtype=jnp.int32) * blk
    block_e = jnp.minimum(jnp.sum(pends[None, :] <= first_row[:, None], axis=1), N_EXPERTS - 1).astype(jnp.int32)
    slot = rank + jnp.sum(jnp.where(idx[:, :, None] == jnp.arange(N_EXPERTS, dtype=jnp.int32), pstart, 0), axis=-1)
    seg_end = jnp.sum(jnp.where(block_e[:, None] == e_ids[None, :], (cnt + pstart)[None, :], 0), axis=1)
    n_valid = jnp.clip(seg_end - first_row, 0, blk).astype(jnp.int32)

    xs = _sc_scatter_rows(h2p, slot, nblock * blk)

    def blk_map(b, be, nv, nb):
        return (jnp.minimum(b, nb[0] - 1), 0)

    layer = lp['layer']

    def w_map(b, be, nv, nb):
        return (layer, be[jnp.minimum(b, nb[0] - 1)], 0, 0)

    ys = pl.pallas_call(
        _experts_kernel,
        grid_spec=pltpu.PrefetchScalarGridSpec(
            num_scalar_prefetch=3,
            grid=(nblock,),
            in_specs=[pl.BlockSpec((blk, HALF_D), blk_map),
                      pl.BlockSpec((1, 1, D_MODEL, EXPERT_HIDDEN), w_map),
                      pl.BlockSpec((1, 1, D_MODEL, EXPERT_HIDDEN), w_map),
                      pl.BlockSpec((1, 1, EXPERT_HIDDEN, D_MODEL), w_map)],
            out_specs=pl.BlockSpec((blk, HALF_D), blk_map),
            scratch_shapes=[pltpu.VMEM((D_MODEL, 2 * EXPERT_HIDDEN), BF16),
                            pltpu.VMEM((EXPERT_HIDDEN, D_MODEL), BF16)]),
        out_shape=jax.ShapeDtypeStruct((nblock * blk, HALF_D), jnp.int32),
        compiler_params=_params("arbitrary"),
        name="moe_experts",
    )(block_e, n_valid, nb_used, xs, lp['exp_w_gate'], lp['exp_w_up'], lp['exp_w_down'])

    wsgu = jnp.concatenate([lp['sh_w_gate'], lp['sh_w_up']], axis=1).astype(BF16)
    wsd = lp['sh_w_down'].astype(BF16)
    ngroup = MOE_COMBINE_GROUPS if ntile % MOE_COMBINE_GROUPS == 0 and ntile >= 4 * MOE_COMBINE_GROUPS else 1
    gtile = ntile // ngroup
    grows = gtile * tt
    full = lambda *shape: pl.BlockSpec(shape, lambda i: (0,) * len(shape))
    wsel_t = wsel.T
    fg = final_g.reshape(1, D_MODEL).astype(F32)
    out = None
    for p in range(ngroup):
        gathered = _sc_gather_rows(ys, slot[0:TOP_K, p * grows:(p + 1) * grows].reshape(TOP_K * grows))
        gathered = gathered.reshape(TOP_K, grows, HALF_D)
        row = lambda width, p=p: pl.BlockSpec((tt, width), lambda i: (p * gtile + i, 0))
        in_specs = [row(8), pl.BlockSpec((TOP_K, tt, HALF_D), lambda i: (0, i, 0)), row(D_MODEL),
                    full(D_MODEL, 2 * EXPERT_HIDDEN), full(EXPERT_HIDDEN, D_MODEL), row(D_MODEL),
                    pl.BlockSpec((1, 1, D_MODEL), lambda i, p=p: (((p * gtile + i) * tt) // rows_per_mod, 0, 0)),
                    full(1, D_MODEL)]
        args = [wsel_t, gathered, h2, wsgu, wsd, x1, g2, fg]
        kern = functools.partial(_combine_kernel, final=final)
        aliases = {}
        if out is not None:
            in_specs.append(pl.BlockSpec(memory_space=pl.ANY))
            args.append(out)
            aliases = {len(args) - 1: 0}
            kern = functools.partial(_combine_into_kernel, final=final)
        out = pl.pallas_call(
            kern,
            grid=(gtile,),
            in_specs=in_specs,
            out_specs=row(D_MODEL),
            out_shape=jax.ShapeDtypeStruct((rows, D_MODEL), F32),
            input_output_aliases=aliases,
            compiler_params=_params("arbitrary"),
            name="moe_combine",
        )(*args)
    return out


def _reorder_w_in(w_in):
    split = 256 + 768 + 512 + 768
    return jnp.concatenate([w_in[:, split:], w_in[:, :split]], axis=1).astype(BF16)


def _mods(mod_row_block):
    return [mod_row_block[:, None, k * D_MODEL:(k + 1) * D_MODEL] for k in range(6)]


def _moe_block(h2, h2p, logits, lp, x1, g2, final_g, *, rows_per_mod, final):
    picks = _router(logits, lp['router_b'])
    return _moe(h2, h2p, picks, lp, x1, g2, final_g, rows_per_mod=rows_per_mod, final=final)


def _layer(x2d, xc2d, c16, lp, layer_idx, tables, final_g, *, nbatch, seq, n_ctx, with_ctx_out, final):
    lambda_init = 0.8 - 0.6 * math.exp(-0.3 * layer_idx)
    mod = _ada_mod(c16, lp['ada_w'].astype(F32), lp['ada_b'].astype(F32))
    sh1, sc1, g1, sh2, sc2, g2 = _mods(mod[0:nbatch])
    csh1, csc1, cg1, csh2, csc2, cg2 = _mods(mod[nbatch:nbatch + 1])
    w_in = _reorder_w_in(lp['w_in'])
    rows_lat = nbatch * seq
    rows_ctx = nbatch * n_ctx

    gate, u, na, sw, df = _inproj(x2d, lp['norm1_g'], sc1, sh1, w_in, tables,
                                  rows_per_mod=seq, rope=True, seq=seq)
    gate_c, u_c, na_c, sw_c, df_c = _inproj(xc2d, lp['norm1_g'], csc1, csh1, w_in, tables,
                                            rows_per_mod=rows_ctx, rope=False, seq=seq)

    win, wout, a_re, a_im = _s5_params(lp['s5_lambda_re'], lp['s5_lambda_im'], lp['s5_log_step'],
                                       lp['s5_b_re'], lp['s5_b_im'], lp['s5_c_re'], lp['s5_c_im'])
    u_tm = jnp.concatenate([u_c.reshape(nbatch, n_ctx, 256).transpose(1, 0, 2),
                            u.reshape(nbatch, seq, 256).transpose(1, 0, 2)], axis=0)
    y_tm = _s5_scan(u_tm, win, wout, a_re, a_im, n_ctx)
    y_bm = y_tm.transpose(0, 2, 1, 3)

    lqk = [lp[k].reshape(1, DIFF_QK_DIM).astype(F32) for k in ('diff_lq1', 'diff_lk1', 'diff_lq2', 'diff_lk2')]
    bias = _na_bias_table(lp['na_rpb'], seq // GRID_W)
    yb, yc = _local_attention(na, na_c, bias, sw, sw_c, lp['swa_sink'], nbatch, seq, n_ctx)
    yd = _diff_attention(df, df_c, lqk, lp['diff_subln_g'], lambda_init, nbatch, seq, n_ctx)

    merge_w = (lp['s5_d'], lp['s5_glu_w'], lp['s5_glu_b'])
    x1, h2, h2p, logits = _merge(u, y_bm, 0, seq, *merge_w, yb, yc, yd, gate, lp['w_branch'], lp['w_out'], x2d, g1,
                                 lp['norm2_g'], sc2, sh2, lp['router_w'], rows_per_mod=seq)
    x_out = _moe_block(h2, h2p, logits, lp, x1, g2, final_g, rows_per_mod=seq, final=final)

    xc_out = None
    if with_ctx_out:
        yb_c, yc_c, yd_c = _ctx_attention(na_c, sw_c, df_c, lp['swa_sink'], lqk, lp['diff_subln_g'],
                                          lambda_init, nbatch, n_ctx)
        x1c, h2c, h2pc, logits_c = _merge(u_c, y_bm, seq, n_ctx, *merge_w, yb_c, yc_c, yd_c, gate_c,
                                          lp['w_branch'], lp['w_out'], xc2d, cg1, lp['norm2_g'], csc2, csh2,
                                          lp['router_w'], rows_per_mod=rows_ctx)
        xc_out = _moe_block(h2c, h2pc, logits_c, lp, x1c, cg2, final_g, rows_per_mod=rows_ctx, final=False)
    return x_out, xc_out


def kernel(x, c, ctx, c_ctx, ada_w, ada_b, norm1_g, norm2_g, w_in, s5_lambda_re, s5_lambda_im, s5_log_step,
           s5_b_re, s5_b_im, s5_c_re, s5_c_im, s5_d, s5_glu_w, s5_glu_b, na_rpb, swa_sink, diff_lq1, diff_lk1,
           diff_lq2, diff_lk2, diff_subln_g, w_branch, w_out, router_w, router_b, exp_w_gate, exp_w_up,
           exp_w_down, sh_w_gate, sh_w_up, sh_w_down, final_g):
    nbatch, seq, d = x.shape
    n_ctx = ctx.shape[1]
    depth = ada_w.shape[0]
    assert d == D_MODEL and nbatch == 8
    stacked = dict(ada_w=ada_w, ada_b=ada_b, norm1_g=norm1_g, norm2_g=norm2_g, w_in=w_in,
                   s5_lambda_re=s5_lambda_re, s5_lambda_im=s5_lambda_im, s5_log_step=s5_log_step,
                   s5_b_re=s5_b_re, s5_b_im=s5_b_im, s5_c_re=s5_c_re, s5_c_im=s5_c_im, s5_d=s5_d,
                   s5_glu_w=s5_glu_w, s5_glu_b=s5_glu_b, na_rpb=na_rpb, swa_sink=swa_sink,
                   diff_lq1=diff_lq1, diff_lk1=diff_lk1, diff_lq2=diff_lq2, diff_lk2=diff_lk2,
                   diff_subln_g=diff_subln_g, w_branch=w_branch, w_out=w_out, router_w=router_w,
                   router_b=router_b, exp_w_gate=exp_w_gate, exp_w_up=exp_w_up, exp_w_down=exp_w_down,
                   sh_w_gate=sh_w_gate, sh_w_up=sh_w_up, sh_w_down=sh_w_down)
    tables = _rope_tables(seq)
    c16 = jnp.concatenate([c.astype(F32), c_ctx.reshape(1, d).astype(F32),
                           jnp.zeros((16 - nbatch - 1, d), F32)], axis=0)
    x2d = x.reshape(nbatch * seq, d).astype(F32)
    xc2d = ctx.reshape(nbatch * n_ctx, d).astype(F32)
    for l in range(depth):
        routed = ('exp_w_gate', 'exp_w_up', 'exp_w_down')
        lp = {k: (v.astype(F32) if k in routed else v[l]) for k, v in stacked.items()}
        lp['layer'] = l
        last = l == depth - 1
        x2d, xc2d = _layer(x2d, xc2d, c16, lp, l, tables, final_g, nbatch=nbatch, seq=seq, n_ctx=n_ctx,
                           with_ctx_out=not last, final=last)
    return x2d.reshape(nbatch, seq, d)
```

```python
import functools
import math

import numpy as np
import jax
import jax.numpy as jnp
from jax import lax
from jax.experimental import pallas as pl
from jax.experimental.pallas import tpu as pltpu
from jax.experimental.pallas import tpu_sc as plsc

F32 = jnp.float32
BF16 = jnp.bfloat16
HIGHEST = lax.Precision.HIGHEST

GRID_W = 64
EPS = 1e-6
NEG_INF = -1e30
ROPE_BASE = 10000.0
D_MODEL = 1024
BRANCH_WIDTH = 256
HEAD_DIM = 64
S5_GROUP = 16
S5_GROUPS = 16
S5_STATE = 64
S5_FLAT = S5_GROUPS * S5_STATE
NA_HEADS = 4
NA_WIN_ROWS = 8
NA_WIN_COLS = 16
SWA_KV_HEADS = 2
SWA_WINDOW = 128
DIFF_HEADS = 4
DIFF_QK_DIM = 32
N_EXPERTS = 64
N_EXPERT_GROUPS = 8
TOPK_GROUPS = 4
TOP_K = 6
EXPERT_HIDDEN = 256
ROUTED_SCALE = 2.5
GATE_WIDTH = 4 * D_MODEL

VMEM_LIMIT = 56 * 1024 * 1024


def _params(*sem):
    return pltpu.CompilerParams(dimension_semantics=sem, vmem_limit_bytes=VMEM_LIMIT)


def _nt_dot(a, b):
    return lax.dot_general(a, b, (((1,), (1,)), ((), ())), preferred_element_type=F32)


def _dot(a, b):
    return jnp.dot(a, b, preferred_element_type=F32)


def _rms(x):
    return x * lax.rsqrt(jnp.mean(x * x, axis=-1, keepdims=True) + EPS)


def _ada_kernel(c_ref, w_ref, b_ref, o_ref):
    c = c_ref[...]
    s = c * jax.nn.sigmoid(c)
    o_ref[...] = jnp.dot(s, w_ref[...], preferred_element_type=F32, precision=HIGHEST) + b_ref[...]


def _ada_mod(cc, w, b):
    rows, d = cc.shape
    width = w.shape[1]
    tn = 1536
    return pl.pallas_call(
        _ada_kernel,
        grid=(width // tn,),
        in_specs=[pl.BlockSpec((rows, d), lambda j: (0, 0)),
                  pl.BlockSpec((d, tn), lambda j: (0, j)),
                  pl.BlockSpec((1, tn), lambda j: (0, j))],
        out_specs=pl.BlockSpec((rows, tn), lambda j: (0, j)),
        out_shape=jax.ShapeDtypeStruct((rows, width), F32),
        compiler_params=_params("arbitrary"),
        name="ada_mod",
    )(cc, w, b.reshape(1, width))


_C_GATE = 0
_C_U = GATE_WIDTH
_C_NA = _C_U + 256
_C_SW = _C_NA + 768
_C_DF = _C_SW + 512
_C_END = _C_DF + 768


def _rope_apply(x, cos, sins, half):
    outs = []
    for j in range(x.shape[1] // 128):
        xs = x[:, j * 128:(j + 1) * 128]
        lane = lax.broadcasted_iota(jnp.int32, xs.shape, 1)
        lo = (lane % (2 * half)) < half
        partner = jnp.where(lo, pltpu.roll(xs, 128 - half, 1), pltpu.roll(xs, half, 1))
        outs.append(xs * cos + partner * sins)
    return outs[0] if len(outs) == 1 else jnp.concatenate(outs, axis=1)


def _inproj_kernel(x_ref, g_ref, sc_ref, sh_ref, w_ref, c64_ref, s64_ref, c32_ref, s32_ref,
                   gate_o, u_o, na_o, sw_o, df_o, *qt_os, rope):
    h = _rms(x_ref[...]) * g_ref[...]
    h = h * (1.0 + sc_ref[0]) + sh_ref[0]
    hb = h.astype(BF16)

    def mm(c0, c1):
        return _dot(hb, w_ref[:, c0:c1])

    for k in range(GATE_WIDTH // 512):
        gate_o[:, k * 512:(k + 1) * 512] = jax.nn.sigmoid(mm(k * 512, (k + 1) * 512)).astype(BF16)
    u_o[...] = mm(_C_U, _C_U + 256)

    na = mm(_C_NA, _C_NA + 768)
    naq = na[:, 0:256] * (HEAD_DIM ** -0.5 * (LOG2E if rope else 1.0))
    na_o[:, 0:256] = naq.astype(BF16)
    na_o[:, 256:768] = na[:, 256:768].astype(BF16)

    sw = mm(_C_SW, _C_SW + 512)
    swq, swk = sw[:, 0:256], sw[:, 256:384]
    if rope:
        swq = _rope_apply(swq, c64_ref[...], s64_ref[...], 16)
        swk = _rope_apply(swk, c64_ref[...], s64_ref[...], 16)
    swq = swq * (HEAD_DIM ** -0.5 * (LOG2E if rope else 1.0))
    sw_o[:, 0:256] = swq.astype(BF16)
    sw_o[:, 256:384] = swk.astype(BF16)
    sw_o[:, 384:512] = sw[:, 384:512].astype(BF16)

    df = mm(_C_DF, _C_DF + 768)
    dfq, dfk = df[:, 0:256], df[:, 256:512]
    if rope:
        dfq = _rope_apply(dfq, c32_ref[...], s32_ref[...], 8)
        dfk = _rope_apply(dfk, c32_ref[...], s32_ref[...], 8)
    dfq = dfq * (DIFF_QK_DIM ** -0.5 * (LOG2E if rope else 1.0))
    df_o[:, 0:256] = dfq.astype(BF16)
    df_o[:, 256:512] = dfk.astype(BF16)
    df_o[:, 512:768] = df[:, 512:768].astype(BF16)
    for qt_o, q in zip(qt_os, (naq, swq, dfq)):
        qt_o[0] = q.T.astype(BF16)


def _inproj(x2d, norm_g, sc, sh, w_bf16, tables, *, rows_per_mod, rope, seq):
    rows = x2d.shape[0]
    tm = 512
    assert rows % tm == 0 and rows_per_mod % tm == 0 and seq % tm == 0
    tiles_per_seq = seq // tm

    def mod_map(i):
        return ((i * tm) // rows_per_mod, 0, 0)

    def tab_map(i):
        return (i % tiles_per_seq, 0)

    tab_spec = pl.BlockSpec((tm, 128), tab_map)
    row = lambda w: pl.BlockSpec((tm, w), lambda i: (i, 0))
    out_specs = [row(GATE_WIDTH), row(256), row(768), row(512), row(768)]
    out_shape = [jax.ShapeDtypeStruct((rows, GATE_WIDTH), BF16),
                 jax.ShapeDtypeStruct((rows, 256), F32),
                 jax.ShapeDtypeStruct((rows, 768), BF16),
                 jax.ShapeDtypeStruct((rows, 512), BF16),
                 jax.ShapeDtypeStruct((rows, 768), BF16)]
    if rope:
        out_specs += [pl.BlockSpec((1, 256, tm), lambda i: (i // tiles_per_seq, 0, i % tiles_per_seq))] * 3
        out_shape += [jax.ShapeDtypeStruct((rows // seq, 256, seq), BF16)] * 3
    return pl.pallas_call(
        functools.partial(_inproj_kernel, rope=rope),
        grid=(rows // tm,),
        in_specs=[row(D_MODEL),
                  pl.BlockSpec((1, D_MODEL), lambda i: (0, 0)),
                  pl.BlockSpec((1, 1, D_MODEL), mod_map),
                  pl.BlockSpec((1, 1, D_MODEL), mod_map),
                  pl.BlockSpec((D_MODEL, _C_END), lambda i: (0, 0)),
                  tab_spec, tab_spec, tab_spec, tab_spec],
        out_specs=out_specs,
        out_shape=out_shape,
        compiler_params=_params("arbitrary"),
        name="inproj",
    )(x2d, norm_g.reshape(1, D_MODEL), sc, sh, w_bf16, *tables)


def _rope_tables(seq):
    t = jnp.arange(seq)
    rows = (t // GRID_W).astype(F32)
    cols = (t % GRID_W).astype(F32)
    lane = np.arange(128)
    out = []
    for dim in (64, 32):
        quarter = dim // 4
        inv_freq = ROPE_BASE ** (-jnp.arange(quarter, dtype=F32) / quarter)
        l = lane % dim
        use_col = l >= dim // 2
        fidx = l % quarter
        hi = (l % (dim // 2)) >= quarter
        ang_r = rows[:, None] * inv_freq[None, :]
        ang_c = cols[:, None] * inv_freq[None, :]
        ang = jnp.where(use_col[None, :], ang_c[:, fidx], ang_r[:, fidx])
        out.append(jnp.cos(ang))
        out.append(jnp.where(hi[None, :], jnp.sin(ang), -jnp.sin(ang)))
    return tuple(out)


S5_CHUNK = 128


def _s5_kernel(u_ref, win_ref, wout_ref, are_ref, aim_ref, y_ref, bu_ref, st_ref, *, tc, nb):
    d = pl.program_id(0)
    i = pl.program_id(1)

    @pl.when(i == 0)
    def _():
        st_ref[...] = jnp.zeros_like(st_ref)

    u = u_ref[...].reshape(tc * nb, BRANCH_WIDTH).astype(BF16)
    bu_ref[...] = _dot(u, win_ref[0])
    ar = jnp.broadcast_to(are_ref[0], (nb, S5_FLAT))
    ai = jnp.broadcast_to(aim_ref[0], (nb, S5_FLAT))

    def body(j, carry):
        xr, xi = carry
        t = j + d * (tc - 1 - 2 * j)
        row = pl.multiple_of(t * nb, nb)
        br = bu_ref[pl.ds(row, nb), 0:S5_FLAT]
        bi = bu_ref[pl.ds(row, nb), S5_FLAT:2 * S5_FLAT]
        nr = ar * xr - ai * xi + br
        ni = ar * xi + ai * xr + bi
        bu_ref[pl.ds(row, nb), 0:S5_FLAT] = nr
        bu_ref[pl.ds(row, nb), S5_FLAT:2 * S5_FLAT] = ni
        return nr, ni

    xr, xi = lax.fori_loop(0, tc, body, (st_ref[:, 0:S5_FLAT], st_ref[:, S5_FLAT:2 * S5_FLAT]), unroll=4)
    st_ref[:, 0:S5_FLAT] = xr
    st_ref[:, S5_FLAT:2 * S5_FLAT] = xi
    y = _dot(bu_ref[...].astype(BF16), wout_ref[0])
    y_ref[0] = y.reshape(tc, nb, BRANCH_WIDTH)


def _s5_scan(u_tm, win, wout, a_re, a_im, n_ctx):
    s_len, nb, _ = u_tm.shape
    tc = S5_CHUNK
    assert nb == 8 and s_len % tc == 0 and n_ctx % tc == 0
    nct = n_ctx // tc
    nlt = (s_len - n_ctx) // tc

    def chunk(d, i):
        rev = jnp.where(i < nct, nct - 1 - i, 2 * nct + nlt - 1 - i)
        return jnp.where(d == 0, i, rev)

    def out_chunk(d, i):
        c = chunk(d, i)
        return jnp.where(c < nct, nlt + c, c - nct)

    return pl.pallas_call(
        functools.partial(_s5_kernel, tc=tc, nb=nb),
        grid=(2, nct + nlt),
        in_specs=[pl.BlockSpec((tc, nb, BRANCH_WIDTH), lambda d, i: (chunk(d, i), 0, 0)),
                  pl.BlockSpec((1, BRANCH_WIDTH, 2 * S5_FLAT), lambda d, i: (d, 0, 0)),
                  pl.BlockSpec((1, 2 * S5_FLAT, BRANCH_WIDTH), lambda d, i: (d, 0, 0)),
                  pl.BlockSpec((1, 1, S5_FLAT), lambda d, i: (d, 0, 0)),
                  pl.BlockSpec((1, 1, S5_FLAT), lambda d, i: (d, 0, 0))],
        out_specs=pl.BlockSpec((1, tc, nb, BRANCH_WIDTH), lambda d, i: (d, out_chunk(d, i), 0, 0)),
        out_shape=jax.ShapeDtypeStruct((2, s_len, nb, BRANCH_WIDTH), F32),
        scratch_shapes=[pltpu.VMEM((tc * nb, 2 * S5_FLAT), F32),
                        pltpu.VMEM((nb, 2 * S5_FLAT), F32)],
        compiler_params=_params("arbitrary", "arbitrary"),
        name="s5_scan",
    )(u_tm, win, wout, a_re, a_im)


def _s5_params(lam_re, lam_im, log_step, b_re, b_im, c_re, c_im):
    lr = lam_re.astype(F32)
    li = lam_im.astype(F32)
    dt = jnp.exp(log_step.astype(F32))[..., None]
    mag = jnp.exp(lr * dt)
    a_re = mag * jnp.cos(li * dt)
    a_im = mag * jnp.sin(li * dt)
    nr, ni, den = a_re - 1.0, a_im, lr * lr + li * li
    k_re = ((nr * lr + ni * li) / den)[..., None]
    k_im = ((ni * lr - nr * li) / den)[..., None]
    br = b_re.astype(F32)
    bi = b_im.astype(F32)
    bb_re = k_re * br - k_im * bi
    bb_im = k_re * bi + k_im * br
    eye = jnp.eye(S5_GROUPS, dtype=F32)

    def blockdiag_in(bb):
        m = jnp.einsum('dgpc,gh->dgchp', bb, eye)
        return m.reshape(2, S5_GROUPS * S5_GROUP, S5_GROUPS * S5_STATE)

    def blockdiag_out(cc):
        m = jnp.einsum('dgcp,gh->dgphc', cc, eye)
        return m.reshape(2, S5_GROUPS * S5_STATE, S5_GROUPS * S5_GROUP)

    win = jnp.concatenate([blockdiag_in(bb_re), blockdiag_in(bb_im)], axis=2).astype(BF16)
    wout = jnp.concatenate([blockdiag_out(c_re.astype(F32)), -blockdiag_out(c_im.astype(F32))], axis=1).astype(BF16)
    return win, wout, a_re.reshape(2, 1, S5_FLAT), a_im.reshape(2, 1, S5_FLAT)


def _softmax_parts(scores, extra=None):
    m = scores[0].max(axis=-1, keepdims=True)
    for s in scores[1:]:
        m = jnp.maximum(m, s.max(axis=-1, keepdims=True))
    if extra is not None:
        m = jnp.maximum(m, extra)
    ps = [jnp.exp(s - m) for s in scores]
    l = ps[0].sum(axis=-1, keepdims=True)
    for p in ps[1:]:
        l = l + p.sum(axis=-1, keepdims=True)
    if extra is not None:
        l = l + jnp.exp(extra - m)
    return ps, l


NA_QROWS = 2
NA_KROWS = 10
VROWS = HEAD_DIM + 16


def _na_window_start(r, grid_rows):
    start = jnp.clip(r - NA_WIN_ROWS // 2, 0, grid_rows - NA_WIN_ROWS)
    return (jnp.minimum(start, grid_rows - NA_KROWS) // 2) * 2


def _head_blockdiag(qt, qbd_ref, nheads, rows_per_head):
    n = qt.shape[1]
    row_h = lax.broadcasted_iota(jnp.int32, qt.shape, 0) // rows_per_head
    zero = jnp.zeros_like(qt)
    for h in range(nheads):
        qbd_ref[:, h * n:(h + 1) * n] = jnp.where(row_h == h, qt, zero)


def _na_scores(qt_ref, k_ref, kc_ref, bias_ref, qbd_ref, *, grid_rows):
    nk = NA_KROWS * GRID_W
    off = pl.multiple_of(_na_window_start(NA_QROWS * pl.program_id(1), grid_rows) * GRID_W, 128)
    _head_blockdiag(qt_ref[0], qbd_ref, NA_HEADS, HEAD_DIM)
    qbd = qbd_ref[...]
    s_loc = _dot(k_ref[0, pl.ds(off, nk), :], qbd) + bias_ref[0]
    s_ctx = _dot(kc_ref[0], qbd)
    return off, s_loc, s_ctx


def _na_values(off, s_loc, s_ctx, vt_ref, vct_ref):
    nq = NA_QROWS * GRID_W
    m = jnp.maximum(s_loc.max(axis=0, keepdims=True), s_ctx.max(axis=0, keepdims=True))
    p_loc = jnp.exp2((s_loc - m).astype(BF16))
    p_ctx = jnp.exp2((s_ctx - m).astype(BF16))
    vw = vt_ref[0, :, pl.ds(off, NA_KROWS * GRID_W)]
    outs = []
    for h in range(NA_HEADS):
        rows = slice(h * VROWS, (h + 1) * VROWS)
        cols = slice(h * nq, (h + 1) * nq)
        o = _dot(vw[rows], p_loc[:, cols]) + _dot(vct_ref[0, rows, :], p_ctx[:, cols])
        outs.append(o[0:HEAD_DIM] / o[HEAD_DIM:HEAD_DIM + 1])
    return jnp.concatenate(outs, axis=0).T.astype(BF16)


def _na_classes(grid_rows):
    return [0, 2, 4, grid_rows - 4, grid_rows - 2]


def _na_bias_table(rpb, grid_rows):
    col = np.arange(GRID_W)
    cstart = np.clip(col - NA_WIN_COLS // 2, 0, GRID_W - NA_WIN_COLS)
    col_in = (col[None, :] >= cstart[:, None]) & (col[None, :] < cstart[:, None] + NA_WIN_COLS)
    cb = np.clip(col[None, :] - col[:, None] + (NA_WIN_COLS - 1), 0, 2 * NA_WIN_COLS - 2)
    classes = _na_classes(grid_rows)
    rbi = np.zeros((len(classes), NA_QROWS, NA_KROWS), np.int64)
    row_in = np.zeros((len(classes), NA_QROWS, NA_KROWS), bool)
    for c, r in enumerate(classes):
        a_row = (min(int(np.clip(r - NA_WIN_ROWS // 2, 0, grid_rows - NA_WIN_ROWS)), grid_rows - NA_KROWS) // 2) * 2
        for qi in range(NA_QROWS):
            start_q = int(np.clip(r + qi - NA_WIN_ROWS // 2, 0, grid_rows - NA_WIN_ROWS))
            for j in range(NA_KROWS):
                row_in[c, qi, j] = start_q <= a_row + j < start_q + NA_WIN_ROWS
                rbi[c, qi, j] = np.clip(a_row + j - (r + qi) + NA_WIN_ROWS - 1, 0, 2 * NA_WIN_ROWS - 2)
    oh_row = jnp.asarray(rbi[..., None] == np.arange(2 * NA_WIN_ROWS - 1), F32)
    oh_col = jnp.asarray(cb[:, :, None] == np.arange(2 * NA_WIN_COLS - 1), F32)
    t = jnp.einsum('hab,cija,qkb->cjkhiq', rpb.astype(F32), oh_row, oh_col, precision=HIGHEST) * LOG2E
    valid = row_in.transpose(0, 2, 1)[:, :, None, None, :, None] & col_in.T[None, None, :, None, None, :]
    t = jnp.where(valid, t, NEG_INF)
    return t.reshape(len(classes), NA_KROWS * GRID_W, NA_HEADS * NA_QROWS * GRID_W)


def _augment_vt(v, nheads):
    nb, s, _ = v.shape
    v = v.reshape(nb, s, nheads, HEAD_DIM)
    pad = jnp.zeros((nb, s, nheads, VROWS - HEAD_DIM), v.dtype).at[..., 0].set(1.0)
    return jnp.concatenate([v, pad], axis=-1).reshape(nb, s, nheads * VROWS).transpose(0, 2, 1)


SWA_BLOCK = 128


def _swa_scores(qt_ref, k_ref, kc_ref, qbd_ref, *, seq):
    n = pl.program_id(1)
    band = 3 * SWA_BLOCK
    nq = SWA_BLOCK
    nqh = 2 * SWA_KV_HEADS
    bstart = pl.multiple_of(jnp.clip((n - 1) * SWA_BLOCK, 0, seq - band), SWA_BLOCK)
    qt = qt_ref[0]
    zero = jnp.zeros((HEAD_DIM, nq), qt.dtype)
    for hq in range(nqh):
        qh = qt[hq * HEAD_DIM:(hq + 1) * HEAD_DIM]
        qbd_ref[:, hq * nq:(hq + 1) * nq] = jnp.concatenate([qh, zero] if hq // 2 == 0 else [zero, qh], axis=0)
    qbd = qbd_ref[...]
    kpos = bstart + lax.broadcasted_iota(jnp.int32, (band, nqh * nq), 0)
    qpos = n * SWA_BLOCK + lax.broadcasted_iota(jnp.int32, (band, nqh * nq), 1) % nq
    s_loc = jnp.where(jnp.abs(qpos - kpos) <= SWA_WINDOW, _dot(k_ref[0, pl.ds(bstart, band), :], qbd), NEG_INF)
    s_ctx = _dot(kc_ref[0], qbd)
    return bstart, s_loc, s_ctx


def _swa_values(bstart, s_loc, s_ctx, vt_ref, vct_ref, sink_ref):
    nq = SWA_BLOCK
    nqh = 2 * SWA_KV_HEADS
    sink = jnp.concatenate([jnp.broadcast_to(sink_ref[0:1, hq:hq + 1] * LOG2E, (1, nq)) for hq in range(nqh)], axis=1)
    m = jnp.maximum(jnp.maximum(s_loc.max(axis=0, keepdims=True), s_ctx.max(axis=0, keepdims=True)), sink)
    p_loc = jnp.exp2((s_loc - m).astype(BF16))
    p_ctx = jnp.exp2((s_ctx - m).astype(BF16))
    p_sink = jnp.exp2(sink - m)
    vw = vt_ref[0, :, pl.ds(bstart, 3 * SWA_BLOCK)]
    outs = []
    for hq in range(nqh):
        rows = slice((hq // 2) * VROWS, (hq // 2 + 1) * VROWS)
        cols = slice(hq * nq, (hq + 1) * nq)
        o = _dot(vw[rows], p_loc[:, cols]) + _dot(vct_ref[0, rows, :], p_ctx[:, cols])
        outs.append(o[0:HEAD_DIM] / (o[HEAD_DIM:HEAD_DIM + 1] + p_sink[:, cols]))
    return jnp.concatenate(outs, axis=0).T.astype(BF16)


def _local_attn_kernel(na_qt, na_k, na_vt, na_kc, na_vct, bias_ref, sw_qt, sw_k, sw_vt, sw_kc, sw_vct, sink_ref,
                       na_o, sw_o, na_qbd, sw_qbd, *, grid_rows, seq):
    na_s = _na_scores(na_qt, na_k, na_kc, bias_ref, na_qbd, grid_rows=grid_rows)
    sw_s = _swa_scores(sw_qt, sw_k, sw_kc, sw_qbd, seq=seq)
    na_o[...] = _na_values(*na_s, na_vt, na_vct)
    sw_o[...] = _swa_values(*sw_s, sw_vt, sw_vct, sink_ref)


def _local_attention(na_qt, na, na_c, bias, sw_qt, sw, sw_c, sink, nbatch, seq, n_ctx):
    grid_rows = seq // GRID_W
    nq = NA_QROWS * GRID_W
    nstep = seq // nq
    nqh = 2 * SWA_KV_HEADS
    assert grid_rows >= NA_KROWS and grid_rows % NA_QROWS == 0 and nq == SWA_BLOCK and seq >= 3 * SWA_BLOCK
    nal = na.reshape(nbatch, seq, 768)
    nac = na_c.reshape(nbatch, n_ctx, 768)
    na_vt = _augment_vt(nal[:, :, 512:768], NA_HEADS)
    na_vct = _augment_vt(nac[:, :, 512:768], NA_HEADS)
    sink_pad = jnp.zeros((1, 128), F32).at[0, 0:nqh].set(sink.astype(F32))
    swl = sw.reshape(nbatch, seq, 512)
    swc = sw_c.reshape(nbatch, n_ctx, 512)
    sw_vt = _augment_vt(swl[:, :, 384:512], SWA_KV_HEADS)
    sw_vct = _augment_vt(swc[:, :, 384:512], SWA_KV_HEADS)

    def cls(b, p):
        r = NA_QROWS * p
        c = jnp.where(r < 4, r // 2, jnp.where(r >= grid_rows - 4, (r - (grid_rows - 4)) // 2 + 3, 2))
        return (c, 0, 0)

    out = jax.ShapeDtypeStruct((nbatch * seq, 256), BF16)
    out_spec = pl.BlockSpec((nq, 256), lambda b, p: (b * nstep + p, 0))
    return pl.pallas_call(
        functools.partial(_local_attn_kernel, grid_rows=grid_rows, seq=seq),
        grid=(nbatch, nstep),
        in_specs=[pl.BlockSpec((1, 256, nq), lambda b, p: (b, 0, p)),
                  pl.BlockSpec((1, seq, 256), lambda b, p: (b, 0, 1)),
                  pl.BlockSpec((1, NA_HEADS * VROWS, seq), lambda b, p: (b, 0, 0)),
                  pl.BlockSpec((1, n_ctx, 256), lambda b, p: (b, 0, 1)),
                  pl.BlockSpec((1, NA_HEADS * VROWS, n_ctx), lambda b, p: (b, 0, 0)),
                  pl.BlockSpec((1, NA_KROWS * GRID_W, NA_HEADS * nq), cls),
                  pl.BlockSpec((1, 256, nq), lambda b, p: (b, 0, p)),
                  pl.BlockSpec((1, seq, 128), lambda b, p: (b, 0, 2)),
                  pl.BlockSpec((1, SWA_KV_HEADS * VROWS, seq), lambda b, p: (b, 0, 0)),
                  pl.BlockSpec((1, n_ctx, 128), lambda b, p: (b, 0, 2)),
                  pl.BlockSpec((1, SWA_KV_HEADS * VROWS, n_ctx), lambda b, p: (b, 0, 0)),
                  pl.BlockSpec((1, 128), lambda b, p: (0, 0))],
        out_specs=[out_spec, out_spec],
        out_shape=[out, out],
        scratch_shapes=[pltpu.VMEM((256, NA_HEADS * nq), BF16),
                        pltpu.VMEM((SWA_KV_HEADS * HEAD_DIM, nqh * nq), BF16)],
        compiler_params=_params("arbitrary", "arbitrary"),
        name="local_attention",
    )(na_qt, nal, na_vt, nac, na_vct, bias, sw_qt, swl, sw_vt, swc, sw_vct, sink_pad)


DIFF_TQ = 512
DIFF_CK = 256
LOG2E = math.log2(math.e)
DIFF_VROWS = HEAD_DIM + 16


def _diff_lambda(lq1_ref, lk1_ref, lq2_ref, lk2_ref, lambda_init):
    s1 = jnp.sum(lq1_ref[...] * lk1_ref[...], axis=-1, keepdims=True)
    s2 = jnp.sum(lq2_ref[...] * lk2_ref[...], axis=-1, keepdims=True)
    return jnp.exp(s1) - jnp.exp(s2) + lambda_init


def _stack_maps(qh):
    lane = lax.broadcasted_iota(jnp.int32, qh.shape, 1)
    zero = jnp.zeros_like(qh)
    return jnp.concatenate([jnp.where(lane < DIFF_QK_DIM, qh, zero),
                            jnp.where(lane >= DIFF_QK_DIM, qh, zero)], axis=0)


def _subln(o0, o1, lam, g, lambda_init):
    o = o0 - lam * o1
    return _rms(o) * g * (1.0 - lambda_init)


def _diff_kernel(qt_ref, k_ref, vt_ref, lq1_ref, lk1_ref, lq2_ref, lk2_ref, g_ref, o_ref, qbd_ref, acc_ref, s_ref,
                 *, nchunk, lambda_init):
    lam = _diff_lambda(lq1_ref, lk1_ref, lq2_ref, lk2_ref, lambda_init)
    qt = qt_ref[0]
    tq = qt.shape[1]
    w = 2 * tq
    row = lax.broadcasted_iota(jnp.int32, qt.shape, 0) // DIFF_QK_DIM
    zero = jnp.zeros_like(qt)
    for j in range(2 * DIFF_HEADS):
        qbd_ref[:, j * tq:(j + 1) * tq] = jnp.where(row == j, qt, zero)
    acc_ref[...] = jnp.zeros_like(acc_ref)

    def scores(slot, c, h):
        s = _dot(k_ref[0, c], qbd_ref[:, h * w:(h + 1) * w])
        s_ref[slot, h] = s
        return s.max(axis=0, keepdims=True)

    def softmax_pv(slot, c, h, m_run, m_chunk):
        m_new = jnp.maximum(m_run, m_chunk)
        alpha = jnp.exp2(m_run - m_new)
        p = jnp.exp2((s_ref[slot, h] - m_new).astype(BF16))
        acc_ref[h] = alpha * acc_ref[h] + _dot(vt_ref[0, c, h * DIFF_VROWS:(h + 1) * DIFF_VROWS, :], p)
        return m_new

    def step(slot, c, carry):
        m_run, m_chunk = carry
        new_run, new_chunk = [], []
        for h in range(DIFF_HEADS):
            new_chunk.append(scores(1 - slot, c + 1, h))
            new_run.append(softmax_pv(slot, c, h, m_run[h], m_chunk[h]))
        return tuple(new_run), tuple(new_chunk)

    def body(i, carry):
        c = 2 * i
        return step(1, c + 1, step(0, c, carry))

    m_run = tuple(jnp.full((1, w), NEG_INF, F32) for _ in range(DIFF_HEADS))
    m_chunk = tuple(scores(0, 0, h) for h in range(DIFF_HEADS))
    npair = (nchunk - 1) // 2
    m_run, m_chunk = lax.fori_loop(0, npair, body, (m_run, m_chunk))
    if nchunk % 2 == 0:
        m_run, m_chunk = step(0, nchunk - 2, (m_run, m_chunk))
        for h in range(DIFF_HEADS):
            softmax_pv(1, nchunk - 1, h, m_run[h], m_chunk[h])
    else:
        for h in range(DIFF_HEADS):
            softmax_pv(0, nchunk - 1, h, m_run[h], m_chunk[h])
    outs = []
    for h in range(DIFF_HEADS):
        o = acc_ref[h, 0:HEAD_DIM, :] / acc_ref[h, HEAD_DIM:HEAD_DIM + 1, :]
        d = o[:, 0:tq] - lam * o[:, tq:w]
        d = d * lax.rsqrt(jnp.mean(d * d, axis=0, keepdims=True) + EPS)
        outs.append(d * g_ref[...] * (1.0 - lambda_init))
    o_ref[...] = jnp.concatenate(outs, axis=0).T.astype(BF16)


def _diff_attention(qt, df, df_c, lqk, subln_g, lambda_init, nbatch, seq, n_ctx):
    tq, ck = min(DIFF_TQ, seq), DIFF_CK
    s_all = seq + n_ctx
    assert s_all % ck == 0 and seq % tq == 0
    nchunk = s_all // ck
    nq = seq // tq
    dfl = df.reshape(nbatch, seq, 768)
    dfc = df_c.reshape(nbatch, n_ctx, 768)
    k_all = jnp.concatenate([dfl[:, :, 256:512], dfc[:, :, 256:512]], axis=1).reshape(nbatch, nchunk, ck, 256)
    v_all = jnp.concatenate([dfl[:, :, 512:768], dfc[:, :, 512:768]], axis=1)
    v_all = v_all.reshape(nbatch, nchunk, ck, DIFF_HEADS, HEAD_DIM)
    pad = jnp.zeros((nbatch, nchunk, ck, DIFF_HEADS, DIFF_VROWS - HEAD_DIM), BF16).at[..., 0].set(1.0)
    vt_all = jnp.concatenate([v_all, pad], axis=-1).reshape(nbatch, nchunk, ck, DIFF_HEADS * DIFF_VROWS)
    vt_all = vt_all.transpose(0, 1, 3, 2)
    vec = pl.BlockSpec((1, DIFF_QK_DIM), lambda b, n: (0, 0))
    return pl.pallas_call(
        functools.partial(_diff_kernel, nchunk=nchunk, lambda_init=lambda_init),
        grid=(nbatch, nq),
        in_specs=[pl.BlockSpec((1, 256, tq), lambda b, n: (b, 0, n)),
                  pl.BlockSpec((1, nchunk, ck, 256), lambda b, n: (b, 0, 0, 0)),
                  pl.BlockSpec((1, nchunk, DIFF_HEADS * DIFF_VROWS, ck), lambda b, n: (b, 0, 0, 0)),
                  vec, vec, vec, vec,
                  pl.BlockSpec((HEAD_DIM, 1), lambda b, n: (0, 0))],
        out_specs=pl.BlockSpec((tq, 256), lambda b, n: (b * nq + n, 0)),
        out_shape=jax.ShapeDtypeStruct((nbatch * seq, 256), BF16),
        scratch_shapes=[pltpu.VMEM((256, 2 * DIFF_HEADS * tq), BF16),
                        pltpu.VMEM((DIFF_HEADS, DIFF_VROWS, 2 * tq), F32),
                        pltpu.VMEM((2, DIFF_HEADS, ck, 2 * tq), F32)],
        compiler_params=_params("arbitrary", "arbitrary"),
        name="diff_attention",
    )(qt, k_all, vt_all, *lqk, subln_g.reshape(HEAD_DIM, 1).astype(F32))


def _ctx_attn_kernel(na_ref, sw_ref, df_ref, sink_ref, lq1_ref, lk1_ref, lq2_ref, lk2_ref, g_ref,
                     nb_o, sw_o, df_o, *, lambda_init):
    n = na_ref.shape[0]
    na = na_ref[...]
    outs = []
    for h in range(NA_HEADS):
        sl = slice(h * HEAD_DIM, (h + 1) * HEAD_DIM)
        (p,), l = _softmax_parts([_nt_dot(na[:, sl], na[:, 256 + h * HEAD_DIM:256 + (h + 1) * HEAD_DIM])])
        outs.append(_dot(p.astype(BF16), na[:, 512 + h * HEAD_DIM:512 + (h + 1) * HEAD_DIM]) / l)
    nb_o[...] = jnp.concatenate(outs, axis=1).astype(BF16)
    sw = sw_ref[...]
    outs = []
    for hq in range(4):
        kv = hq // 2
        k = sw[:, 256 + kv * HEAD_DIM:256 + (kv + 1) * HEAD_DIM]
        v = sw[:, 384 + kv * HEAD_DIM:384 + (kv + 1) * HEAD_DIM]
        sk = jnp.broadcast_to(sink_ref[0:1, hq:hq + 1], (n, 1))
        (p,), l = _softmax_parts([_nt_dot(sw[:, hq * HEAD_DIM:(hq + 1) * HEAD_DIM], k)], extra=sk)
        outs.append(_dot(p.astype(BF16), v) / l)
    sw_o[...] = jnp.concatenate(outs, axis=1).astype(BF16)
    lam = _diff_lambda(lq1_ref, lk1_ref, lq2_ref, lk2_ref, lambda_init)
    df = df_ref[...]
    outs = []
    for h in range(DIFF_HEADS):
        sl = slice(h * HEAD_DIM, (h + 1) * HEAD_DIM)
        q2 = _stack_maps(df[:, sl])
        (p,), l = _softmax_parts([_nt_dot(q2, df[:, 256 + h * HEAD_DIM:256 + (h + 1) * HEAD_DIM])])
        o = _dot(p.astype(BF16), df[:, 512 + h * HEAD_DIM:512 + (h + 1) * HEAD_DIM]) / l
        outs.append(_subln(o[0:n], o[n:2 * n], lam, g_ref[...], lambda_init))
    df_o[...] = jnp.concatenate(outs, axis=1).astype(BF16)


def _ctx_attention(na_c, sw_c, df_c, sink, lqk, subln_g, lambda_init, nbatch, n_ctx):
    sink_pad = jnp.zeros((1, 128), F32).at[0, 0:4].set(sink.astype(F32))
    vec = pl.BlockSpec((1, DIFF_QK_DIM), lambda b: (0, 0))
    out = jax.ShapeDtypeStruct((nbatch * n_ctx, 256), BF16)
    return pl.pallas_call(
        functools.partial(_ctx_attn_kernel, lambda_init=lambda_init),
        grid=(nbatch,),
        in_specs=[pl.BlockSpec((n_ctx, 768), lambda b: (b, 0)),
                  pl.BlockSpec((n_ctx, 512), lambda b: (b, 0)),
                  pl.BlockSpec((n_ctx, 768), lambda b: (b, 0)),
                  pl.BlockSpec((1, 128), lambda b: (0, 0)),
                  vec, vec, vec, vec,
                  pl.BlockSpec((1, HEAD_DIM), lambda b: (0, 0))],
        out_specs=[pl.BlockSpec((n_ctx, 256), lambda b: (b, 0))] * 3,
        out_shape=[out, out, out],
        compiler_params=_params("arbitrary"),
        name="ctx_attention",
    )(na_c, sw_c, df_c, sink_pad, *lqk, subln_g.reshape(1, HEAD_DIM).astype(F32))


def _merge_kernel(u_ref, yf_ref, yr_ref, d_ref, gw_ref, gb_ref, yb_ref, yc_ref, yd_ref, gate_ref,
                  wb_ref, wo_ref, x_ref, g1_ref, n2_ref, sc2_ref, sh2_ref, rw_ref,
                  x1_o, h2_o, h2p_o, lg_o):
    y = u_ref[...] * d_ref[...] + yf_ref[0, 0] + yr_ref[0, 0]
    a = jax.nn.gelu(y, approximate=True)
    ya = a * jax.nn.sigmoid(_dot(a.astype(BF16), gw_ref[...]) + gb_ref[...])
    branches = (ya.astype(BF16), yb_ref[...], yc_ref[...], yd_ref[...])
    acc = None
    for i in range(4):
        t = gate_ref[:, i * D_MODEL:(i + 1) * D_MODEL].astype(F32) * _dot(branches[i], wb_ref[i])
        acc = t if acc is None else acc + t
    mixed = _dot(acc.astype(BF16), wo_ref[...])
    x1 = x_ref[...] + g1_ref[0] * mixed
    x1_o[...] = x1
    h2 = _rms(x1) * n2_ref[...]
    h2 = h2 * (1.0 + sc2_ref[0]) + sh2_ref[0]
    h2_o[...] = h2.astype(BF16)
    h2p_o[...] = _pack_rows(h2[:, 0:HALF_D], h2[:, HALF_D:D_MODEL])
    lg_o[...] = lax.dot_general(rw_ref[...], h2, (((1,), (1,)), ((), ())), preferred_element_type=F32,
                                precision=HIGHEST)


def _merge(u, y_bm, y_start, rows_per_seq, s5_d, glu_w, glu_b, yb, yc, yd, gate, wb, wo, x2d, g1, norm2_g, sc2, sh2,
           router_w, *, rows_per_mod):
    rows = x2d.shape[0]
    tm = min(512, rows_per_seq)
    assert rows % tm == 0 and rows_per_mod % tm == 0 and rows_per_seq % tm == 0 and y_start % tm == 0
    tiles_per_seq = rows_per_seq // tm

    def mod_map(i):
        return ((i * tm) // rows_per_mod, 0, 0)

    def y_spec(d):
        return pl.BlockSpec((1, 1, tm, 256), lambda i: (d, i // tiles_per_seq, y_start // tm + i % tiles_per_seq, 0))

    row = lambda w: pl.BlockSpec((tm, w), lambda i: (i, 0))
    full = lambda *shape: pl.BlockSpec(shape, lambda i: (0,) * len(shape))
    mod = pl.BlockSpec((1, 1, D_MODEL), mod_map)
    return pl.pallas_call(
        _merge_kernel,
        grid=(rows // tm,),
        in_specs=[row(256), y_spec(0), y_spec(1), full(1, 256), full(256, 256), full(1, 256),
                  row(256), row(256), row(256), row(GATE_WIDTH),
                  full(4, 256, D_MODEL), full(D_MODEL, D_MODEL), row(D_MODEL),
                  mod, full(1, D_MODEL), mod, mod, full(N_EXPERTS, D_MODEL)],
        out_specs=[row(D_MODEL), row(D_MODEL), row(HALF_D), pl.BlockSpec((N_EXPERTS, tm), lambda i: (0, i))],
        out_shape=[jax.ShapeDtypeStruct((rows, D_MODEL), F32),
                   jax.ShapeDtypeStruct((rows, D_MODEL), BF16),
                   jax.ShapeDtypeStruct((rows, HALF_D), jnp.int32),
                   jax.ShapeDtypeStruct((N_EXPERTS, rows), F32)],
        compiler_params=_params("arbitrary"),
        name="merge",
    )(u, y_bm, y_bm, s5_d.reshape(1, 256).astype(F32), glu_w.astype(BF16), glu_b.reshape(1, 256).astype(F32),
      yb, yc, yd, gate, wb.astype(BF16), wo.astype(BF16), x2d, g1, norm2_g.reshape(1, D_MODEL), sc2, sh2,
      router_w.astype(F32).T)


def _router_kernel(lg_ref, b_ref, tri_ref, idx_ref, rank_ref, w_ref, cnt_ref, base_ref):
    tr = lg_ref.shape[1]
    gsz = N_EXPERTS // N_EXPERT_GROUPS
    sc = jax.nn.sigmoid(lg_ref[...])
    bi = sc + b_ref[...]
    e_iota = lax.broadcasted_iota(jnp.int32, (gsz, tr), 0).astype(F32)
    groups = [bi[g * gsz:(g + 1) * gsz] for g in range(N_EXPERT_GROUPS)]
    gs = []
    for bg in groups:
        m1 = bg.max(axis=0, keepdims=True)
        i1 = jnp.where(bg == m1, e_iota, float(gsz)).min(axis=0, keepdims=True)
        m2 = jnp.where(e_iota == i1, -jnp.inf, bg).max(axis=0, keepdims=True)
        gs.append(m1 + m2)
    v = []
    for g in range(N_EXPERT_GROUPS):
        rank = jnp.zeros((1, tr), F32)
        for g2 in range(N_EXPERT_GROUPS):
            if g2 == g:
                continue
            beats = (gs[g2] >= gs[g]) if g2 < g else (gs[g2] > gs[g])
            rank = rank + jnp.where(beats, 1.0, 0.0)
        v.append(jnp.where(rank < TOPK_GROUPS, groups[g], NEG_INF))
    flat = [e_iota + float(g * gsz) for g in range(N_EXPERT_GROUPS)]
    sel = [jnp.zeros((gsz, tr), F32) for _ in range(N_EXPERT_GROUPS)]
    picks = []
    for _ in range(TOP_K):
        m = v[0].max(axis=0, keepdims=True)
        for g in range(1, N_EXPERT_GROUPS):
            m = jnp.maximum(m, v[g].max(axis=0, keepdims=True))
        am = jnp.where(v[0] == m, flat[0], float(N_EXPERTS)).min(axis=0, keepdims=True)
        for g in range(1, N_EXPERT_GROUPS):
            am = jnp.minimum(am, jnp.where(v[g] == m, flat[g], float(N_EXPERTS)).min(axis=0, keepdims=True))
        hits = []
        for g in range(N_EXPERT_GROUPS):
            hit = flat[g] == am
            hits.append(hit)
            sel[g] = jnp.where(hit, 1.0, sel[g])
            v[g] = jnp.where(hit, -jnp.inf, v[g])
        picks.append((am, hits))
    scg = [sc[g * gsz:(g + 1) * gsz] for g in range(N_EXPERT_GROUPS)]
    den = (sel[0] * scg[0]).sum(axis=0, keepdims=True)
    for g in range(1, N_EXPERT_GROUPS):
        den = den + (sel[g] * scg[g]).sum(axis=0, keepdims=True)

    @pl.when(pl.program_id(0) == 0)
    def _():
        base_ref[...] = jnp.zeros_like(base_ref)

    sel_all = jnp.concatenate(sel, axis=0)
    before = _dot(sel_all.astype(jnp.bfloat16), tri_ref[...]) + base_ref[...]
    for k, (am, hits) in enumerate(picks):
        wk = jnp.zeros((1, tr), F32)
        rk = jnp.zeros((1, tr), F32)
        for g in range(N_EXPERT_GROUPS):
            wk = wk + jnp.where(hits[g], scg[g], 0.0).sum(axis=0, keepdims=True)
            rk = rk + jnp.where(hits[g], before[g * gsz:(g + 1) * gsz], 0.0).sum(axis=0, keepdims=True)
        idx_ref[k:k + 1, :] = am.astype(jnp.int32)
        rank_ref[k:k + 1, :] = rk.astype(jnp.int32)
        w_ref[k:k + 1, :] = wk / den * ROUTED_SCALE
    idx_ref[TOP_K:8, :] = jnp.zeros((8 - TOP_K, tr), jnp.int32)
    rank_ref[TOP_K:8, :] = jnp.zeros((8 - TOP_K, tr), jnp.int32)
    w_ref[TOP_K:8, :] = jnp.zeros((8 - TOP_K, tr), F32)
    base_ref[...] += sel_all.sum(axis=1, keepdims=True)
    cnt_ref[...] = base_ref[...].astype(jnp.int32)


ROUTER_TILE = 512


def _router(logits_t, router_b):
    ne, rows = logits_t.shape
    tr = ROUTER_TILE
    assert rows % tr == 0
    tri = jnp.asarray(np.triu(np.ones((tr, tr), np.float32), k=1), jnp.bfloat16)
    pick = pl.BlockSpec((8, tr), lambda i: (0, i))
    return pl.pallas_call(
        _router_kernel,
        grid=(rows // tr,),
        in_specs=[pl.BlockSpec((ne, tr), lambda i: (0, i)),
                  pl.BlockSpec((ne, 1), lambda i: (0, 0)),
                  pl.BlockSpec((tr, tr), lambda i: (0, 0))],
        out_specs=[pick, pick, pick, pl.BlockSpec((ne, 1), lambda i: (0, 0))],
        out_shape=[jax.ShapeDtypeStruct((8, rows), jnp.int32),
                   jax.ShapeDtypeStruct((8, rows), jnp.int32),
                   jax.ShapeDtypeStruct((8, rows), F32),
                   jax.ShapeDtypeStruct((ne, 1), jnp.int32)],
        scratch_shapes=[pltpu.VMEM((ne, 1), F32)],
        compiler_params=_params("arbitrary"),
        name="router",
    )(logits_t, router_b.reshape(ne, 1).astype(F32), tri)


MOE_BLOCK = 512
MOE_TOKENS = 256
MOE_COMBINE_GROUPS = 4
HALF_D = D_MODEL // 2


def _pack_rows(lo, hi):
    lo_b = pltpu.bitcast(lo.astype(jnp.bfloat16).astype(F32), jnp.uint32)
    hi_b = pltpu.bitcast(hi.astype(jnp.bfloat16).astype(F32), jnp.uint32)
    return pltpu.bitcast((hi_b & jnp.uint32(0xFFFF0000)) | (lo_b >> 16), jnp.int32)


def _unpack_rows(words):
    u = pltpu.bitcast(words, jnp.uint32)
    lo = pltpu.bitcast(u << 16, F32)
    hi = pltpu.bitcast(u & jnp.uint32(0xFFFF0000), F32)
    return lo, hi


def _swiglu(x_bf16, wgu, wd):
    hgu = _dot(x_bf16, wgu)
    g = hgu[:, 0:EXPERT_HIDDEN]
    a = g * jax.nn.sigmoid(g) * hgu[:, EXPERT_HIDDEN:2 * EXPERT_HIDDEN]
    return _dot(a.astype(BF16), wd)


SC_CORES = 2
SC_SUBCORES = 16
SC_STREAM_ROWS = 128


def _sc_for_each_chunk(total, body):
    chunk = SC_STREAM_ROWS
    assert total % chunk == 0
    nchunk = total // chunk
    per_worker = pl.cdiv(nchunk, SC_CORES * SC_SUBCORES)
    first = (lax.axis_index("s") * SC_CORES + lax.axis_index("c")) * per_worker

    @pl.loop(0, per_worker)
    def _(j):
        @pl.when(first + j < nchunk)
        def _():
            body((first + j) * chunk)


def _sc_scatter_rows(rows, slots, n_out):
    total, n = rows.shape
    chunk = SC_STREAM_ROWS
    mesh = plsc.VectorSubcoreMesh(core_axis_name="c", subcore_axis_name="s")

    @functools.partial(
        pl.kernel, mesh=mesh,
        out_type=jax.ShapeDtypeStruct((n_out, n), jnp.int32),
        scratch_types=[pltpu.VMEM((8, chunk), jnp.int32),
                       pltpu.VMEM((chunk, n), jnp.int32),
                       pltpu.SemaphoreType.DMA],
        name="moe_dispatch_sc",
    )
    def scatter(rows_hbm, slot_hbm, out_hbm, idx_v, rows_v, sem):
        def body(off):
            pltpu.sync_copy(rows_hbm.at[pl.ds(off, chunk)], rows_v)
            pltpu.sync_copy(slot_hbm.at[:, pl.ds(off, chunk)], idx_v)
            for k in range(TOP_K):
                pltpu.async_copy(rows_v, out_hbm.at[idx_v.at[k]], sem).wait()

        _sc_for_each_chunk(total, body)

    return scatter(rows, slots)


def _experts_kernel(be_ref, nv_ref, nb_ref, xs_ref, wg_ref, wu_ref, wd_ref, ys_ref, wgu_bf, wd_bf):
    b = pl.program_id(0)

    @pl.when(b < nb_ref[0])
    def _():
        @pl.when((b == 0) | (be_ref[b] != be_ref[jnp.maximum(b - 1, 0)]))
        def _():
            wgu_bf[:, 0:EXPERT_HIDDEN] = wg_ref[0, 0].astype(BF16)
            wgu_bf[:, EXPERT_HIDDEN:2 * EXPERT_HIDDEN] = wu_ref[0, 0].astype(BF16)
            wd_bf[...] = wd_ref[0, 0].astype(BF16)

        words = xs_ref[...]
        live = lax.broadcasted_iota(jnp.int32, words.shape, 0) < nv_ref[b]
        lo, hi = _unpack_rows(jnp.where(live, words, 0))
        x = jnp.concatenate([lo, hi], axis=1).astype(BF16)
        y = _swiglu(x, wgu_bf[...], wd_bf[...])
        ys_ref[...] = _pack_rows(y[:, 0:HALF_D], y[:, HALF_D:D_MODEL])


def _sc_gather_rows(table, indices):
    m, n = indices.shape[0], table.shape[1]
    chunk = SC_STREAM_ROWS
    mesh = plsc.VectorSubcoreMesh(core_axis_name="c", subcore_axis_name="s")

    @functools.partial(
        pl.kernel, mesh=mesh,
        out_type=jax.ShapeDtypeStruct((m, n), jnp.int32),
        scratch_types=[pltpu.VMEM((chunk,), jnp.int32),
                       pltpu.VMEM((chunk, n), jnp.int32),
                       pltpu.SemaphoreType.DMA],
        name="moe_gather_sc",
    )
    def gather(table_hbm, idx_hbm, out_hbm, idx_v, rows_v, sem):
        def body(off):
            pltpu.sync_copy(idx_hbm.at[pl.ds(off, chunk)], idx_v)
            pltpu.async_copy(table_hbm.at[idx_v], rows_v, sem).wait()
            pltpu.sync_copy(rows_v, out_hbm.at[pl.ds(off, chunk)])

        _sc_for_each_chunk(m, body)

    return gather(table, indices)


def _combine_into_kernel(w_ref, rows_ref, h_ref, wsgu_ref, wsd_ref, x1_ref, g2_ref, fg_ref, prev_ref, o_ref, *, final):
    del prev_ref
    _combine_kernel(w_ref, rows_ref, h_ref, wsgu_ref, wsd_ref, x1_ref, g2_ref, fg_ref, o_ref, final=final)


def _combine_kernel(w_ref, rows_ref, h_ref, wsgu_ref, wsd_ref, x1_ref, g2_ref, fg_ref, o_ref, *, final):
    shared = _swiglu(h_ref[...], wsgu_ref[...], wsd_ref[...])
    acc_lo = shared[:, 0:HALF_D]
    acc_hi = shared[:, HALF_D:D_MODEL]
    w = w_ref[...]
    for k in range(TOP_K):
        lo, hi = _unpack_rows(rows_ref[k])
        acc_lo = acc_lo + w[:, k:k + 1] * lo
        acc_hi = acc_hi + w[:, k:k + 1] * hi
    x2 = x1_ref[...] + g2_ref[0] * jnp.concatenate([acc_lo, acc_hi], axis=1)
    if final:
        x2 = _rms(x2) * fg_ref[...]
    o_ref[...] = x2


def _moe(h2, h2p, picks, lp, x1, g2, final_g, *, rows_per_mod, final):
    idx, rank, wsel, counts = picks
    rows = h2.shape[0]
    tt = MOE_TOKENS
    blk = MOE_BLOCK
    assert rows % tt == 0 and rows_per_mod % tt == 0 and (rows * TOP_K) % blk == 0
    ntile = rows // tt
    nblock = rows * TOP_K // blk + N_EXPERTS

    cnt = counts.reshape(N_EXPERTS)
    padded = (cnt + blk - 1) // blk * blk
    e_ids = jnp.arange(N_EXPERTS, dtype=jnp.int32)
    pends = jnp.sum(jnp.where(e_ids[None, :] <= e_ids[:, None], padded[None, :], 0), axis=1)
    pstart = (pends - padded).astype(jnp.int32)
    nb_used = (jnp.sum(padded) // blk).astype(jnp.int32).reshape(1)
    first_row = jnp.arange(nblock, dtype=jnp.int32) * blk
    block_e = jnp.minimum(jnp.sum(pends[None, :] <= first_row[:, None], axis=1), N_EXPERTS - 1).astype(jnp.int32)
    slot = rank + jnp.sum(jnp.where(idx[:, :, None] == jnp.arange(N_EXPERTS, dtype=jnp.int32), pstart, 0), axis=-1)
    seg_end = jnp.sum(jnp.where(block_e[:, None] == e_ids[None, :], (cnt + pstart)[None, :], 0), axis=1)
    n_valid = jnp.clip(seg_end - first_row, 0, blk).astype(jnp.int32)

    xs = _sc_scatter_rows(h2p, slot, nblock * blk)

    def blk_map(b, be, nv, nb):
        return (jnp.minimum(b, nb[0] - 1), 0)

    layer = lp['layer']

    def w_map(b, be, nv, nb):
        return (layer, be[jnp.minimum(b, nb[0] - 1)], 0, 0)

    ys = pl.pallas_call(
        _experts_kernel,
        grid_spec=pltpu.PrefetchScalarGridSpec(
            num_scalar_prefetch=3,
            grid=(nblock,),
            in_specs=[pl.BlockSpec((blk, HALF_D), blk_map),
                      pl.BlockSpec((1, 1, D_MODEL, EXPERT_HIDDEN), w_map),
                      pl.BlockSpec((1, 1, D_MODEL, EXPERT_HIDDEN), w_map),
                      pl.BlockSpec((1, 1, EXPERT_HIDDEN, D_MODEL), w_map)],
            out_specs=pl.BlockSpec((blk, HALF_D), blk_map),
            scratch_shapes=[pltpu.VMEM((D_MODEL, 2 * EXPERT_HIDDEN), BF16),
                            pltpu.VMEM((EXPERT_HIDDEN, D_MODEL), BF16)]),
        out_shape=jax.ShapeDtypeStruct((nblock * blk, HALF_D), jnp.int32),
        compiler_params=_params("arbitrary"),
        name="moe_experts",
    )(block_e, n_valid, nb_used, xs, lp['exp_w_gate'], lp['exp_w_up'], lp['exp_w_down'])

    wsgu = jnp.concatenate([lp['sh_w_gate'], lp['sh_w_up']], axis=1).astype(BF16)
    wsd = lp['sh_w_down'].astype(BF16)
    ngroup = MOE_COMBINE_GROUPS if ntile % MOE_COMBINE_GROUPS == 0 and ntile >= 4 * MOE_COMBINE_GROUPS else 1
    gtile = ntile // ngroup
    grows = gtile * tt
    full = lambda *shape: pl.BlockSpec(shape, lambda i: (0,) * len(shape))
    wsel_t = wsel.T
    fg = final_g.reshape(1, D_MODEL).astype(F32)
    out = None
    for p in range(ngroup):
        gathered = _sc_gather_rows(ys, slot[0:TOP_K, p * grows:(p + 1) * grows].reshape(TOP_K * grows))
        gathered = gathered.reshape(TOP_K, grows, HALF_D)
        row = lambda width, p=p: pl.BlockSpec((tt, width), lambda i: (p * gtile + i, 0))
        in_specs = [row(8), pl.BlockSpec((TOP_K, tt, HALF_D), lambda i: (0, i, 0)), row(D_MODEL),
                    full(D_MODEL, 2 * EXPERT_HIDDEN), full(EXPERT_HIDDEN, D_MODEL), row(D_MODEL),
                    pl.BlockSpec((1, 1, D_MODEL), lambda i, p=p: (((p * gtile + i) * tt) // rows_per_mod, 0, 0)),
                    full(1, D_MODEL)]
        args = [wsel_t, gathered, h2, wsgu, wsd, x1, g2, fg]
        kern = functools.partial(_combine_kernel, final=final)
        aliases = {}
        if out is not None:
            in_specs.append(pl.BlockSpec(memory_space=pl.ANY))
            args.append(out)
            aliases = {len(args) - 1: 0}
            kern = functools.partial(_combine_into_kernel, final=final)
        out = pl.pallas_call(
            kern,
            grid=(gtile,),
            in_specs=in_specs,
            out_specs=row(D_MODEL),
            out_shape=jax.ShapeDtypeStruct((rows, D_MODEL), F32),
            input_output_aliases=aliases,
            compiler_params=_params("arbitrary"),
            name="moe_combine",
        )(*args)
    return out


def _reorder_w_in(w_in):
    split = 256 + 768 + 512 + 768
    return jnp.concatenate([w_in[:, split:], w_in[:, :split]], axis=1).astype(BF16)


def _mods(mod_row_block):
    return [mod_row_block[:, None, k * D_MODEL:(k + 1) * D_MODEL] for k in range(6)]


def _moe_block(h2, h2p, logits, lp, x1, g2, final_g, *, rows_per_mod, final):
    picks = _router(logits, lp['router_b'])
    return _moe(h2, h2p, picks, lp, x1, g2, final_g, rows_per_mod=rows_per_mod, final=final)


def _layer(x2d, xc2d, c16, lp, layer_idx, tables, final_g, *, nbatch, seq, n_ctx, with_ctx_out, final):
    lambda_init = 0.8 - 0.6 * math.exp(-0.3 * layer_idx)
    mod = _ada_mod(c16, lp['ada_w'].astype(F32), lp['ada_b'].astype(F32))
    sh1, sc1, g1, sh2, sc2, g2 = _mods(mod[0:nbatch])
    csh1, csc1, cg1, csh2, csc2, cg2 = _mods(mod[nbatch:nbatch + 1])
    w_in = _reorder_w_in(lp['w_in'])
    rows_lat = nbatch * seq
    rows_ctx = nbatch * n_ctx

    gate, u, na, sw, df, na_qt, sw_qt, df_qt = _inproj(x2d, lp['norm1_g'], sc1, sh1, w_in, tables,
                                                       rows_per_mod=seq, rope=True, seq=seq)
    gate_c, u_c, na_c, sw_c, df_c = _inproj(xc2d, lp['norm1_g'], csc1, csh1, w_in, tables,
                                            rows_per_mod=rows_ctx, rope=False, seq=seq)

    win, wout, a_re, a_im = _s5_params(lp['s5_lambda_re'], lp['s5_lambda_im'], lp['s5_log_step'],
                                       lp['s5_b_re'], lp['s5_b_im'], lp['s5_c_re'], lp['s5_c_im'])
    u_tm = jnp.concatenate([u_c.reshape(nbatch, n_ctx, 256).transpose(1, 0, 2),
                            u.reshape(nbatch, seq, 256).transpose(1, 0, 2)], axis=0)
    y_tm = _s5_scan(u_tm, win, wout, a_re, a_im, n_ctx)
    y_bm = y_tm.transpose(0, 2, 1, 3)

    lqk = [lp[k].reshape(1, DIFF_QK_DIM).astype(F32) for k in ('diff_lq1', 'diff_lk1', 'diff_lq2', 'diff_lk2')]
    bias = _na_bias_table(lp['na_rpb'], seq // GRID_W)
    yb, yc = _local_attention(na_qt, na, na_c, bias, sw_qt, sw, sw_c, lp['swa_sink'], nbatch, seq, n_ctx)
    yd = _diff_attention(df_qt, df, df_c, lqk, lp['diff_subln_g'], lambda_init, nbatch, seq, n_ctx)

    merge_w = (lp['s5_d'], lp['s5_glu_w'], lp['s5_glu_b'])
    x1, h2, h2p, logits = _merge(u, y_bm, 0, seq, *merge_w, yb, yc, yd, gate, lp['w_branch'], lp['w_out'], x2d, g1,
                                 lp['norm2_g'], sc2, sh2, lp['router_w'], rows_per_mod=seq)
    x_out = _moe_block(h2, h2p, logits, lp, x1, g2, final_g, rows_per_mod=seq, final=final)

    xc_out = None
    if with_ctx_out:
        yb_c, yc_c, yd_c = _ctx_attention(na_c, sw_c, df_c, lp['swa_sink'], lqk, lp['diff_subln_g'],
                                          lambda_init, nbatch, n_ctx)
        x1c, h2c, h2pc, logits_c = _merge(u_c, y_bm, seq, n_ctx, *merge_w, yb_c, yc_c, yd_c, gate_c,
                                          lp['w_branch'], lp['w_out'], xc2d, cg1, lp['norm2_g'], csc2, csh2,
                                          lp['router_w'], rows_per_mod=rows_ctx)
        xc_out = _moe_block(h2c, h2pc, logits_c, lp, x1c, cg2, final_g, rows_per_mod=rows_ctx, final=False)
    return x_out, xc_out


def kernel(x, c, ctx, c_ctx, ada_w, ada_b, norm1_g, norm2_g, w_in, s5_lambda_re, s5_lambda_im, s5_log_step,
           s5_b_re, s5_b_im, s5_c_re, s5_c_im, s5_d, s5_glu_w, s5_glu_b, na_rpb, swa_sink, diff_lq1, diff_lk1,
           diff_lq2, diff_lk2, diff_subln_g, w_branch, w_out, router_w, router_b, exp_w_gate, exp_w_up,
           exp_w_down, sh_w_gate, sh_w_up, sh_w_down, final_g):
    nbatch, seq, d = x.shape
    n_ctx = ctx.shape[1]
    depth = ada_w.shape[0]
    assert d == D_MODEL and nbatch == 8
    stacked = dict(ada_w=ada_w, ada_b=ada_b, norm1_g=norm1_g, norm2_g=norm2_g, w_in=w_in,
                   s5_lambda_re=s5_lambda_re, s5_lambda_im=s5_lambda_im, s5_log_step=s5_log_step,
                   s5_b_re=s5_b_re, s5_b_im=s5_b_im, s5_c_re=s5_c_re, s5_c_im=s5_c_im, s5_d=s5_d,
                   s5_glu_w=s5_glu_w, s5_glu_b=s5_glu_b, na_rpb=na_rpb, swa_sink=swa_sink,
                   diff_lq1=diff_lq1, diff_lk1=diff_lk1, diff_lq2=diff_lq2, diff_lk2=diff_lk2,
                   diff_subln_g=diff_subln_g, w_branch=w_branch, w_out=w_out, router_w=router_w,
                   router_b=router_b, exp_w_gate=exp_w_gate, exp_w_up=exp_w_up, exp_w_down=exp_w_down,
                   sh_w_gate=sh_w_gate, sh_w_up=sh_w_up, sh_w_down=sh_w_down)
    tables = _rope_tables(seq)
    c16 = jnp.concatenate([c.astype(F32), c_ctx.reshape(1, d).astype(F32),
                           jnp.zeros((16 - nbatch - 1, d), F32)], axis=0)
    x2d = x.reshape(nbatch * seq, d).astype(F32)
    xc2d = ctx.reshape(nbatch * n_ctx, d).astype(F32)
    for l in range(depth):
        routed = ('exp_w_gate', 'exp_w_up', 'exp_w_down')
        lp = {k: (v.astype(F32) if k in routed else v[l]) for k, v in stacked.items()}
        lp['layer'] = l
        last = l == depth - 1
        x2d, xc2d = _layer(x2d, xc2d, c16, lp, l, tables, final_g, nbatch=nbatch, seq=seq, n_ctx=n_ctx,
                           with_ctx_out=not last, final=last)
    return x2d.reshape(nbatch, seq, d)
```

```python
import functools
import math

import numpy as np
import jax
import jax.numpy as jnp
from jax import lax
from jax.experimental import pallas as pl
from jax.experimental.pallas import tpu as pltpu
from jax.experimental.pallas import tpu_sc as plsc

F32 = jnp.float32
BF16 = jnp.bfloat16
HIGHEST = lax.Precision.HIGHEST

GRID_W = 64
EPS = 1e-6
NEG_INF = -1e30
ROPE_BASE = 10000.0
D_MODEL = 1024
BRANCH_WIDTH = 256
HEAD_DIM = 64
S5_GROUP = 16
S5_GROUPS = 16
S5_STATE = 64
S5_FLAT = S5_GROUPS * S5_STATE
NA_HEADS = 4
NA_WIN_ROWS = 8
NA_WIN_COLS = 16
SWA_KV_HEADS = 2
SWA_WINDOW = 128
DIFF_HEADS = 4
DIFF_QK_DIM = 32
N_EXPERTS = 64
N_EXPERT_GROUPS = 8
TOPK_GROUPS = 4
TOP_K = 6
EXPERT_HIDDEN = 256
ROUTED_SCALE = 2.5
GATE_WIDTH = 4 * D_MODEL

VMEM_LIMIT = 56 * 1024 * 1024


def _params(*sem):
    return pltpu.CompilerParams(dimension_semantics=sem, vmem_limit_bytes=VMEM_LIMIT)


def _nt_dot(a, b):
    return lax.dot_general(a, b, (((1,), (1,)), ((), ())), preferred_element_type=F32)


def _dot(a, b):
    return jnp.dot(a, b, preferred_element_type=F32)


def _rms(x):
    return x * lax.rsqrt(jnp.mean(x * x, axis=-1, keepdims=True) + EPS)


def _ada_kernel(c_ref, w_ref, b_ref, o_ref):
    c = c_ref[...]
    s = c * jax.nn.sigmoid(c)
    o_ref[...] = jnp.dot(s, w_ref[...], preferred_element_type=F32, precision=HIGHEST) + b_ref[...]


def _ada_mod(cc, w, b):
    rows, d = cc.shape
    width = w.shape[1]
    tn = 1536
    return pl.pallas_call(
        _ada_kernel,
        grid=(width // tn,),
        in_specs=[pl.BlockSpec((rows, d), lambda j: (0, 0)),
                  pl.BlockSpec((d, tn), lambda j: (0, j)),
                  pl.BlockSpec((1, tn), lambda j: (0, j))],
        out_specs=pl.BlockSpec((rows, tn), lambda j: (0, j)),
        out_shape=jax.ShapeDtypeStruct((rows, width), F32),
        compiler_params=_params("arbitrary"),
        name="ada_mod",
    )(cc, w, b.reshape(1, width))


_C_GATE = 0
_C_U = GATE_WIDTH
_C_NA = _C_U + 256
_C_SW = _C_NA + 768
_C_DF = _C_SW + 512
_C_END = _C_DF + 768


def _rope_apply(x, cos, sins, half):
    outs = []
    for j in range(x.shape[1] // 128):
        xs = x[:, j * 128:(j + 1) * 128]
        lane = lax.broadcasted_iota(jnp.int32, xs.shape, 1)
        lo = (lane % (2 * half)) < half
        partner = jnp.where(lo, pltpu.roll(xs, 128 - half, 1), pltpu.roll(xs, half, 1))
        outs.append(xs * cos + partner * sins)
    return outs[0] if len(outs) == 1 else jnp.concatenate(outs, axis=1)


def _inproj_kernel(x_ref, g_ref, sc_ref, sh_ref, w_ref, c64_ref, s64_ref, c32_ref, s32_ref,
                   gate_o, u_o, na_o, sw_o, df_o, *qt_os, rope):
    h = _rms(x_ref[...]) * g_ref[...]
    h = h * (1.0 + sc_ref[0]) + sh_ref[0]
    hb = h.astype(BF16)

    def mm(c0, c1):
        return _dot(hb, w_ref[:, c0:c1])

    for k in range(GATE_WIDTH // 512):
        gate_o[:, k * 512:(k + 1) * 512] = jax.nn.sigmoid(mm(k * 512, (k + 1) * 512)).astype(BF16)
    u_o[...] = mm(_C_U, _C_U + 256)

    na = mm(_C_NA, _C_NA + 768)
    naq = na[:, 0:256] * (HEAD_DIM ** -0.5 * (LOG2E if rope else 1.0))
    na_o[:, 0:256] = naq.astype(BF16)
    na_o[:, 256:768] = na[:, 256:768].astype(BF16)

    sw = mm(_C_SW, _C_SW + 512)
    swq, swk = sw[:, 0:256], sw[:, 256:384]
    if rope:
        swq = _rope_apply(swq, c64_ref[...], s64_ref[...], 16)
        swk = _rope_apply(swk, c64_ref[...], s64_ref[...], 16)
    swq = swq * (HEAD_DIM ** -0.5 * (LOG2E if rope else 1.0))
    sw_o[:, 0:256] = swq.astype(BF16)
    sw_o[:, 256:384] = swk.astype(BF16)
    sw_o[:, 384:512] = sw[:, 384:512].astype(BF16)

    df = mm(_C_DF, _C_DF + 768)
    dfq, dfk = df[:, 0:256], df[:, 256:512]
    if rope:
        dfq = _rope_apply(dfq, c32_ref[...], s32_ref[...], 8)
        dfk = _rope_apply(dfk, c32_ref[...], s32_ref[...], 8)
    dfq = dfq * (DIFF_QK_DIM ** -0.5 * (LOG2E if rope else 1.0))
    df_o[:, 0:256] = dfq.astype(BF16)
    df_o[:, 256:512] = dfk.astype(BF16)
    df_o[:, 512:768] = df[:, 512:768].astype(BF16)
    if qt_os:
        na_qt_o, sw_qt_o, df_qt_o, na_vt_o, sw_vt_o = qt_os
        for qt_o, q in ((na_qt_o, naq), (sw_qt_o, swq), (df_qt_o, dfq)):
            qt_o[0] = q.T.astype(BF16)
        na_vt_o[0] = _ones_row_vt(na[:, 512:768].T)
        sw_vt_o[0] = _ones_row_vt(sw[:, 384:512].T)


def _ones_row_vt(vt):
    n = vt.shape[1]
    pad = jnp.where(lax.broadcasted_iota(jnp.int32, (VROWS - HEAD_DIM, n), 0) == 0, 1.0, 0.0)
    parts = []
    for h in range(vt.shape[0] // HEAD_DIM):
        parts += [vt[h * HEAD_DIM:(h + 1) * HEAD_DIM], pad]
    return jnp.concatenate(parts, axis=0).astype(BF16)


def _inproj(x2d, norm_g, sc, sh, w_bf16, tables, *, rows_per_mod, rope, seq):
    rows = x2d.shape[0]
    tm = 512
    assert rows % tm == 0 and rows_per_mod % tm == 0 and seq % tm == 0
    tiles_per_seq = seq // tm

    def mod_map(i):
        return ((i * tm) // rows_per_mod, 0, 0)

    def tab_map(i):
        return (i % tiles_per_seq, 0)

    tab_spec = pl.BlockSpec((tm, 128), tab_map)
    row = lambda w: pl.BlockSpec((tm, w), lambda i: (i, 0))
    out_specs = [row(GATE_WIDTH), row(256), row(768), row(512), row(768)]
    out_shape = [jax.ShapeDtypeStruct((rows, GATE_WIDTH), BF16),
                 jax.ShapeDtypeStruct((rows, 256), F32),
                 jax.ShapeDtypeStruct((rows, 768), BF16),
                 jax.ShapeDtypeStruct((rows, 512), BF16),
                 jax.ShapeDtypeStruct((rows, 768), BF16)]
    if rope:
        for nrow in (256, 256, 256, NA_HEADS * VROWS, SWA_KV_HEADS * VROWS):
            out_specs.append(pl.BlockSpec((1, nrow, tm), lambda i: (i // tiles_per_seq, 0, i % tiles_per_seq)))
            out_shape.append(jax.ShapeDtypeStruct((rows // seq, nrow, seq), BF16))
    return pl.pallas_call(
        functools.partial(_inproj_kernel, rope=rope),
        grid=(rows // tm,),
        in_specs=[row(D_MODEL),
                  pl.BlockSpec((1, D_MODEL), lambda i: (0, 0)),
                  pl.BlockSpec((1, 1, D_MODEL), mod_map),
                  pl.BlockSpec((1, 1, D_MODEL), mod_map),
                  pl.BlockSpec((D_MODEL, _C_END), lambda i: (0, 0)),
                  tab_spec, tab_spec, tab_spec, tab_spec],
        out_specs=out_specs,
        out_shape=out_shape,
        compiler_params=_params("arbitrary"),
        name="inproj",
    )(x2d, norm_g.reshape(1, D_MODEL), sc, sh, w_bf16, *tables)


def _rope_tables(seq):
    t = jnp.arange(seq)
    rows = (t // GRID_W).astype(F32)
    cols = (t % GRID_W).astype(F32)
    lane = np.arange(128)
    out = []
    for dim in (64, 32):
        quarter = dim // 4
        inv_freq = ROPE_BASE ** (-jnp.arange(quarter, dtype=F32) / quarter)
        l = lane % dim
        use_col = l >= dim // 2
        fidx = l % quarter
        hi = (l % (dim // 2)) >= quarter
        ang_r = rows[:, None] * inv_freq[None, :]
        ang_c = cols[:, None] * inv_freq[None, :]
        ang = jnp.where(use_col[None, :], ang_c[:, fidx], ang_r[:, fidx])
        out.append(jnp.cos(ang))
        out.append(jnp.where(hi[None, :], jnp.sin(ang), -jnp.sin(ang)))
    return tuple(out)


S5_CHUNK = 128


def _s5_kernel(u_ref, win_ref, wout_ref, are_ref, aim_ref, y_ref, bu_ref, st_ref, *, tc, nb):
    d = pl.program_id(0)
    i = pl.program_id(1)

    @pl.when(i == 0)
    def _():
        st_ref[...] = jnp.zeros_like(st_ref)

    u = u_ref[...].reshape(tc * nb, BRANCH_WIDTH).astype(BF16)
    bu_ref[...] = _dot(u, win_ref[0])
    ar = jnp.broadcast_to(are_ref[0], (nb, S5_FLAT))
    ai = jnp.broadcast_to(aim_ref[0], (nb, S5_FLAT))

    def body(j, carry):
        xr, xi = carry
        t = j + d * (tc - 1 - 2 * j)
        row = pl.multiple_of(t * nb, nb)
        br = bu_ref[pl.ds(row, nb), 0:S5_FLAT]
        bi = bu_ref[pl.ds(row, nb), S5_FLAT:2 * S5_FLAT]
        nr = ar * xr - ai * xi + br
        ni = ar * xi + ai * xr + bi
        bu_ref[pl.ds(row, nb), 0:S5_FLAT] = nr
        bu_ref[pl.ds(row, nb), S5_FLAT:2 * S5_FLAT] = ni
        return nr, ni

    xr, xi = lax.fori_loop(0, tc, body, (st_ref[:, 0:S5_FLAT], st_ref[:, S5_FLAT:2 * S5_FLAT]), unroll=4)
    st_ref[:, 0:S5_FLAT] = xr
    st_ref[:, S5_FLAT:2 * S5_FLAT] = xi
    y = _dot(bu_ref[...].astype(BF16), wout_ref[0])
    y_ref[0] = y.reshape(tc, nb, BRANCH_WIDTH)


def _s5_scan(u_tm, win, wout, a_re, a_im, n_ctx):
    s_len, nb, _ = u_tm.shape
    tc = S5_CHUNK
    assert nb == 8 and s_len % tc == 0 and n_ctx % tc == 0
    nct = n_ctx // tc
    nlt = (s_len - n_ctx) // tc

    def chunk(d, i):
        rev = jnp.where(i < nct, nct - 1 - i, 2 * nct + nlt - 1 - i)
        return jnp.where(d == 0, i, rev)

    def out_chunk(d, i):
        c = chunk(d, i)
        return jnp.where(c < nct, nlt + c, c - nct)

    return pl.pallas_call(
        functools.partial(_s5_kernel, tc=tc, nb=nb),
        grid=(2, nct + nlt),
        in_specs=[pl.BlockSpec((tc, nb, BRANCH_WIDTH), lambda d, i: (chunk(d, i), 0, 0)),
                  pl.BlockSpec((1, BRANCH_WIDTH, 2 * S5_FLAT), lambda d, i: (d, 0, 0)),
                  pl.BlockSpec((1, 2 * S5_FLAT, BRANCH_WIDTH), lambda d, i: (d, 0, 0)),
                  pl.BlockSpec((1, 1, S5_FLAT), lambda d, i: (d, 0, 0)),
                  pl.BlockSpec((1, 1, S5_FLAT), lambda d, i: (d, 0, 0))],
        out_specs=pl.BlockSpec((1, tc, nb, BRANCH_WIDTH), lambda d, i: (d, out_chunk(d, i), 0, 0)),
        out_shape=jax.ShapeDtypeStruct((2, s_len, nb, BRANCH_WIDTH), F32),
        scratch_shapes=[pltpu.VMEM((tc * nb, 2 * S5_FLAT), F32),
                        pltpu.VMEM((nb, 2 * S5_FLAT), F32)],
        compiler_params=_params("arbitrary", "arbitrary"),
        name="s5_scan",
    )(u_tm, win, wout, a_re, a_im)


def _s5_params(lam_re, lam_im, log_step, b_re, b_im, c_re, c_im):
    lr = lam_re.astype(F32)
    li = lam_im.astype(F32)
    dt = jnp.exp(log_step.astype(F32))[..., None]
    mag = jnp.exp(lr * dt)
    a_re = mag * jnp.cos(li * dt)
    a_im = mag * jnp.sin(li * dt)
    nr, ni, den = a_re - 1.0, a_im, lr * lr + li * li
    k_re = ((nr * lr + ni * li) / den)[..., None]
    k_im = ((ni * lr - nr * li) / den)[..., None]
    br = b_re.astype(F32)
    bi = b_im.astype(F32)
    bb_re = k_re * br - k_im * bi
    bb_im = k_re * bi + k_im * br
    eye = jnp.eye(S5_GROUPS, dtype=F32)

    def blockdiag_in(bb):
        m = jnp.einsum('dgpc,gh->dgchp', bb, eye)
        return m.reshape(2, S5_GROUPS * S5_GROUP, S5_GROUPS * S5_STATE)

    def blockdiag_out(cc):
        m = jnp.einsum('dgcp,gh->dgphc', cc, eye)
        return m.reshape(2, S5_GROUPS * S5_STATE, S5_GROUPS * S5_GROUP)

    win = jnp.concatenate([blockdiag_in(bb_re), blockdiag_in(bb_im)], axis=2).astype(BF16)
    wout = jnp.concatenate([blockdiag_out(c_re.astype(F32)), -blockdiag_out(c_im.astype(F32))], axis=1).astype(BF16)
    return win, wout, a_re.reshape(2, 1, S5_FLAT), a_im.reshape(2, 1, S5_FLAT)


def _softmax_parts(scores, extra=None):
    m = scores[0].max(axis=-1, keepdims=True)
    for s in scores[1:]:
        m = jnp.maximum(m, s.max(axis=-1, keepdims=True))
    if extra is not None:
        m = jnp.maximum(m, extra)
    ps = [jnp.exp(s - m) for s in scores]
    l = ps[0].sum(axis=-1, keepdims=True)
    for p in ps[1:]:
        l = l + p.sum(axis=-1, keepdims=True)
    if extra is not None:
        l = l + jnp.exp(extra - m)
    return ps, l


NA_QROWS = 2
NA_KROWS = 10
VROWS = HEAD_DIM + 16


def _na_window_start(r, grid_rows):
    start = jnp.clip(r - NA_WIN_ROWS // 2, 0, grid_rows - NA_WIN_ROWS)
    return (jnp.minimum(start, grid_rows - NA_KROWS) // 2) * 2


def _head_blockdiag(qt, qbd_ref, nheads, rows_per_head):
    n = qt.shape[1]
    row_h = lax.broadcasted_iota(jnp.int32, qt.shape, 0) // rows_per_head
    zero = jnp.zeros_like(qt)
    for h in range(nheads):
        qbd_ref[:, h * n:(h + 1) * n] = jnp.where(row_h == h, qt, zero)


def _na_scores(qt_ref, k_ref, kc_ref, bias_ref, qbd_ref, *, grid_rows):
    nk = NA_KROWS * GRID_W
    off = pl.multiple_of(_na_window_start(NA_QROWS * pl.program_id(1), grid_rows) * GRID_W, 128)
    _head_blockdiag(qt_ref[0], qbd_ref, NA_HEADS, HEAD_DIM)
    qbd = qbd_ref[...]
    s_loc = _dot(k_ref[0, pl.ds(off, nk), :], qbd) + bias_ref[0]
    s_ctx = _dot(kc_ref[0], qbd)
    return off, s_loc, s_ctx


def _na_values(off, s_loc, s_ctx, vt_ref, vct_ref):
    nq = NA_QROWS * GRID_W
    m = jnp.maximum(s_loc.max(axis=0, keepdims=True), s_ctx.max(axis=0, keepdims=True))
    p_loc = jnp.exp2((s_loc - m).astype(BF16))
    p_ctx = jnp.exp2((s_ctx - m).astype(BF16))
    vw = vt_ref[0, :, pl.ds(off, NA_KROWS * GRID_W)]
    outs = []
    for h in range(NA_HEADS):
        rows = slice(h * VROWS, (h + 1) * VROWS)
        cols = slice(h * nq, (h + 1) * nq)
        o = _dot(vw[rows], p_loc[:, cols]) + _dot(vct_ref[0, rows, :], p_ctx[:, cols])
        outs.append(o[0:HEAD_DIM] / o[HEAD_DIM:HEAD_DIM + 1])
    return jnp.concatenate(outs, axis=0).T.astype(BF16)


def _na_classes(grid_rows):
    return [0, 2, 4, grid_rows - 4, grid_rows - 2]


def _na_bias_table(rpb, grid_rows):
    col = np.arange(GRID_W)
    cstart = np.clip(col - NA_WIN_COLS // 2, 0, GRID_W - NA_WIN_COLS)
    col_in = (col[None, :] >= cstart[:, None]) & (col[None, :] < cstart[:, None] + NA_WIN_COLS)
    cb = np.clip(col[None, :] - col[:, None] + (NA_WIN_COLS - 1), 0, 2 * NA_WIN_COLS - 2)
    classes = _na_classes(grid_rows)
    rbi = np.zeros((len(classes), NA_QROWS, NA_KROWS), np.int64)
    row_in = np.zeros((len(classes), NA_QROWS, NA_KROWS), bool)
    for c, r in enumerate(classes):
        a_row = (min(int(np.clip(r - NA_WIN_ROWS // 2, 0, grid_rows - NA_WIN_ROWS)), grid_rows - NA_KROWS) // 2) * 2
        for qi in range(NA_QROWS):
            start_q = int(np.clip(r + qi - NA_WIN_ROWS // 2, 0, grid_rows - NA_WIN_ROWS))
            for j in range(NA_KROWS):
                row_in[c, qi, j] = start_q <= a_row + j < start_q + NA_WIN_ROWS
                rbi[c, qi, j] = np.clip(a_row + j - (r + qi) + NA_WIN_ROWS - 1, 0, 2 * NA_WIN_ROWS - 2)
    oh_row = jnp.asarray(rbi[..., None] == np.arange(2 * NA_WIN_ROWS - 1), F32)
    oh_col = jnp.asarray(cb[:, :, None] == np.arange(2 * NA_WIN_COLS - 1), F32)
    t = jnp.einsum('hab,cija,qkb->cjkhiq', rpb.astype(F32), oh_row, oh_col, precision=HIGHEST) * LOG2E
    valid = row_in.transpose(0, 2, 1)[:, :, None, None, :, None] & col_in.T[None, None, :, None, None, :]
    t = jnp.where(valid, t, NEG_INF)
    return t.reshape(len(classes), NA_KROWS * GRID_W, NA_HEADS * NA_QROWS * GRID_W)


def _augment_vt(v, nheads):
    nb, s, _ = v.shape
    v = v.reshape(nb, s, nheads, HEAD_DIM)
    pad = jnp.zeros((nb, s, nheads, VROWS - HEAD_DIM), v.dtype).at[..., 0].set(1.0)
    return jnp.concatenate([v, pad], axis=-1).reshape(nb, s, nheads * VROWS).transpose(0, 2, 1)


SWA_BLOCK = 128


def _swa_scores(qt_ref, k_ref, kc_ref, qbd_ref, *, seq):
    n = pl.program_id(1)
    band = 3 * SWA_BLOCK
    nq = SWA_BLOCK
    nqh = 2 * SWA_KV_HEADS
    bstart = pl.multiple_of(jnp.clip((n - 1) * SWA_BLOCK, 0, seq - band), SWA_BLOCK)
    qt = qt_ref[0]
    zero = jnp.zeros((HEAD_DIM, nq), qt.dtype)
    for hq in range(nqh):
        qh = qt[hq * HEAD_DIM:(hq + 1) * HEAD_DIM]
        qbd_ref[:, hq * nq:(hq + 1) * nq] = jnp.concatenate([qh, zero] if hq // 2 == 0 else [zero, qh], axis=0)
    qbd = qbd_ref[...]
    kpos = bstart + lax.broadcasted_iota(jnp.int32, (band, nqh * nq), 0)
    qpos = n * SWA_BLOCK + lax.broadcasted_iota(jnp.int32, (band, nqh * nq), 1) % nq
    s_loc = jnp.where(jnp.abs(qpos - kpos) <= SWA_WINDOW, _dot(k_ref[0, pl.ds(bstart, band), :], qbd), NEG_INF)
    s_ctx = _dot(kc_ref[0], qbd)
    return bstart, s_loc, s_ctx


def _swa_values(bstart, s_loc, s_ctx, vt_ref, vct_ref, sink_ref):
    nq = SWA_BLOCK
    nqh = 2 * SWA_KV_HEADS
    sink = jnp.concatenate([jnp.broadcast_to(sink_ref[0:1, hq:hq + 1] * LOG2E, (1, nq)) for hq in range(nqh)], axis=1)
    m = jnp.maximum(jnp.maximum(s_loc.max(axis=0, keepdims=True), s_ctx.max(axis=0, keepdims=True)), sink)
    p_loc = jnp.exp2((s_loc - m).astype(BF16))
    p_ctx = jnp.exp2((s_ctx - m).astype(BF16))
    p_sink = jnp.exp2(sink - m)
    vw = vt_ref[0, :, pl.ds(bstart, 3 * SWA_BLOCK)]
    outs = []
    for hq in range(nqh):
        rows = slice((hq // 2) * VROWS, (hq // 2 + 1) * VROWS)
        cols = slice(hq * nq, (hq + 1) * nq)
        o = _dot(vw[rows], p_loc[:, cols]) + _dot(vct_ref[0, rows, :], p_ctx[:, cols])
        outs.append(o[0:HEAD_DIM] / (o[HEAD_DIM:HEAD_DIM + 1] + p_sink[:, cols]))
    return jnp.concatenate(outs, axis=0).T.astype(BF16)


def _local_attn_kernel(na_qt, na_k, na_vt, na_kc, na_vct, bias_ref, sw_qt, sw_k, sw_vt, sw_kc, sw_vct, sink_ref,
                       na_o, sw_o, na_qbd, sw_qbd, *, grid_rows, seq):
    na_s = _na_scores(na_qt, na_k, na_kc, bias_ref, na_qbd, grid_rows=grid_rows)
    sw_s = _swa_scores(sw_qt, sw_k, sw_kc, sw_qbd, seq=seq)
    na_o[...] = _na_values(*na_s, na_vt, na_vct)
    sw_o[...] = _swa_values(*sw_s, sw_vt, sw_vct, sink_ref)


def _local_attention(na_qt, na_vt, na, na_c, bias, sw_qt, sw_vt, sw, sw_c, sink, nbatch, seq, n_ctx):
    grid_rows = seq // GRID_W
    nq = NA_QROWS * GRID_W
    nstep = seq // nq
    nqh = 2 * SWA_KV_HEADS
    assert grid_rows >= NA_KROWS and grid_rows % NA_QROWS == 0 and nq == SWA_BLOCK and seq >= 3 * SWA_BLOCK
    nal = na.reshape(nbatch, seq, 768)
    nac = na_c.reshape(nbatch, n_ctx, 768)
    na_vct = _augment_vt(nac[:, :, 512:768], NA_HEADS)
    sink_pad = jnp.zeros((1, 128), F32).at[0, 0:nqh].set(sink.astype(F32))
    swl = sw.reshape(nbatch, seq, 512)
    swc = sw_c.reshape(nbatch, n_ctx, 512)
    sw_vct = _augment_vt(swc[:, :, 384:512], SWA_KV_HEADS)

    def cls(b, p):
        r = NA_QROWS * p
        c = jnp.where(r < 4, r // 2, jnp.where(r >= grid_rows - 4, (r - (grid_rows - 4)) // 2 + 3, 2))
        return (c, 0, 0)

    out = jax.ShapeDtypeStruct((nbatch * seq, 256), BF16)
    out_spec = pl.BlockSpec((nq, 256), lambda b, p: (b * nstep + p, 0))
    return pl.pallas_call(
        functools.partial(_local_attn_kernel, grid_rows=grid_rows, seq=seq),
        grid=(nbatch, nstep),
        in_specs=[pl.BlockSpec((1, 256, nq), lambda b, p: (b, 0, p)),
                  pl.BlockSpec((1, seq, 256), lambda b, p: (b, 0, 1)),
                  pl.BlockSpec((1, NA_HEADS * VROWS, seq), lambda b, p: (b, 0, 0)),
                  pl.BlockSpec((1, n_ctx, 256), lambda b, p: (b, 0, 1)),
                  pl.BlockSpec((1, NA_HEADS * VROWS, n_ctx), lambda b, p: (b, 0, 0)),
                  pl.BlockSpec((1, NA_KROWS * GRID_W, NA_HEADS * nq), cls),
                  pl.BlockSpec((1, 256, nq), lambda b, p: (b, 0, p)),
                  pl.BlockSpec((1, seq, 128), lambda b, p: (b, 0, 2)),
                  pl.BlockSpec((1, SWA_KV_HEADS * VROWS, seq), lambda b, p: (b, 0, 0)),
                  pl.BlockSpec((1, n_ctx, 128), lambda b, p: (b, 0, 2)),
                  pl.BlockSpec((1, SWA_KV_HEADS * VROWS, n_ctx), lambda b, p: (b, 0, 0)),
                  pl.BlockSpec((1, 128), lambda b, p: (0, 0))],
        out_specs=[out_spec, out_spec],
        out_shape=[out, out],
        scratch_shapes=[pltpu.VMEM((256, NA_HEADS * nq), BF16),
                        pltpu.VMEM((SWA_KV_HEADS * HEAD_DIM, nqh * nq), BF16)],
        compiler_params=_params("arbitrary", "arbitrary"),
        name="local_attention",
    )(na_qt, nal, na_vt, nac, na_vct, bias, sw_qt, swl, sw_vt, swc, sw_vct, sink_pad)


DIFF_TQ = 512
DIFF_CK = 256
LOG2E = math.log2(math.e)
DIFF_VROWS = HEAD_DIM + 16


def _diff_lambda(lq1_ref, lk1_ref, lq2_ref, lk2_ref, lambda_init):
    s1 = jnp.sum(lq1_ref[...] * lk1_ref[...], axis=-1, keepdims=True)
    s2 = jnp.sum(lq2_ref[...] * lk2_ref[...], axis=-1, keepdims=True)
    return jnp.exp(s1) - jnp.exp(s2) + lambda_init


def _stack_maps(qh):
    lane = lax.broadcasted_iota(jnp.int32, qh.shape, 1)
    zero = jnp.zeros_like(qh)
    return jnp.concatenate([jnp.where(lane < DIFF_QK_DIM, qh, zero),
                            jnp.where(lane >= DIFF_QK_DIM, qh, zero)], axis=0)


def _subln(o0, o1, lam, g, lambda_init):
    o = o0 - lam * o1
    return _rms(o) * g * (1.0 - lambda_init)


def _diff_kernel(qt_ref, k_ref, vt_ref, lq1_ref, lk1_ref, lq2_ref, lk2_ref, g_ref, o_ref, qbd_ref, acc_ref, s_ref,
                 *, nchunk, lambda_init):
    lam = _diff_lambda(lq1_ref, lk1_ref, lq2_ref, lk2_ref, lambda_init)
    qt = qt_ref[0]
    tq = qt.shape[1]
    w = 2 * tq
    row = lax.broadcasted_iota(jnp.int32, qt.shape, 0) // DIFF_QK_DIM
    zero = jnp.zeros_like(qt)
    for j in range(2 * DIFF_HEADS):
        qbd_ref[:, j * tq:(j + 1) * tq] = jnp.where(row == j, qt, zero)
    acc_ref[...] = jnp.zeros_like(acc_ref)

    def scores(slot, c, h):
        s = _dot(k_ref[0, c], qbd_ref[:, h * w:(h + 1) * w])
        s_ref[slot, h] = s
        return s.max(axis=0, keepdims=True)

    def softmax_pv(slot, c, h, m_run, m_chunk):
        m_new = jnp.maximum(m_run, m_chunk)
        alpha = jnp.exp2(m_run - m_new)
        p = jnp.exp2((s_ref[slot, h] - m_new).astype(BF16))
        acc_ref[h] = alpha * acc_ref[h] + _dot(vt_ref[0, c, h * DIFF_VROWS:(h + 1) * DIFF_VROWS, :], p)
        return m_new

    def step(slot, c, carry):
        m_run, m_chunk = carry
        new_run, new_chunk = [], []
        for h in range(DIFF_HEADS):
            new_chunk.append(scores(1 - slot, c + 1, h))
            new_run.append(softmax_pv(slot, c, h, m_run[h], m_chunk[h]))
        return tuple(new_run), tuple(new_chunk)

    def body(i, carry):
        c = 2 * i
        return step(1, c + 1, step(0, c, carry))

    m_run = tuple(jnp.full((1, w), NEG_INF, F32) for _ in range(DIFF_HEADS))
    m_chunk = tuple(scores(0, 0, h) for h in range(DIFF_HEADS))
    npair = (nchunk - 1) // 2
    m_run, m_chunk = lax.fori_loop(0, npair, body, (m_run, m_chunk))
    if nchunk % 2 == 0:
        m_run, m_chunk = step(0, nchunk - 2, (m_run, m_chunk))
        for h in range(DIFF_HEADS):
            softmax_pv(1, nchunk - 1, h, m_run[h], m_chunk[h])
    else:
        for h in range(DIFF_HEADS):
            softmax_pv(0, nchunk - 1, h, m_run[h], m_chunk[h])
    outs = []
    for h in range(DIFF_HEADS):
        o = acc_ref[h, 0:HEAD_DIM, :] / acc_ref[h, HEAD_DIM:HEAD_DIM + 1, :]
        d = o[:, 0:tq] - lam * o[:, tq:w]
        d = d * lax.rsqrt(jnp.mean(d * d, axis=0, keepdims=True) + EPS)
        outs.append(d * g_ref[...] * (1.0 - lambda_init))
    o_ref[...] = jnp.concatenate(outs, axis=0).T.astype(BF16)


def _diff_attention(qt, df, df_c, lqk, subln_g, lambda_init, nbatch, seq, n_ctx):
    tq, ck = min(DIFF_TQ, seq), DIFF_CK
    s_all = seq + n_ctx
    assert s_all % ck == 0 and seq % tq == 0
    nchunk = s_all // ck
    nq = seq // tq
    dfl = df.reshape(nbatch, seq, 768)
    dfc = df_c.reshape(nbatch, n_ctx, 768)
    k_all = jnp.concatenate([dfl[:, :, 256:512], dfc[:, :, 256:512]], axis=1).reshape(nbatch, nchunk, ck, 256)
    v_all = jnp.concatenate([dfl[:, :, 512:768], dfc[:, :, 512:768]], axis=1)
    v_all = v_all.reshape(nbatch, nchunk, ck, DIFF_HEADS, HEAD_DIM)
    pad = jnp.zeros((nbatch, nchunk, ck, DIFF_HEADS, DIFF_VROWS - HEAD_DIM), BF16).at[..., 0].set(1.0)
    vt_all = jnp.concatenate([v_all, pad], axis=-1).reshape(nbatch, nchunk, ck, DIFF_HEADS * DIFF_VROWS)
    vt_all = vt_all.transpose(0, 1, 3, 2)
    vec = pl.BlockSpec((1, DIFF_QK_DIM), lambda b, n: (0, 0))
    return pl.pallas_call(
        functools.partial(_diff_kernel, nchunk=nchunk, lambda_init=lambda_init),
        grid=(nbatch, nq),
        in_specs=[pl.BlockSpec((1, 256, tq), lambda b, n: (b, 0, n)),
                  pl.BlockSpec((1, nchunk, ck, 256), lambda b, n: (b, 0, 0, 0)),
                  pl.BlockSpec((1, nchunk, DIFF_HEADS * DIFF_VROWS, ck), lambda b, n: (b, 0, 0, 0)),
                  vec, vec, vec, vec,
                  pl.BlockSpec((HEAD_DIM, 1), lambda b, n: (0, 0))],
        out_specs=pl.BlockSpec((tq, 256), lambda b, n: (b * nq + n, 0)),
        out_shape=jax.ShapeDtypeStruct((nbatch * seq, 256), BF16),
        scratch_shapes=[pltpu.VMEM((256, 2 * DIFF_HEADS * tq), BF16),
                        pltpu.VMEM((DIFF_HEADS, DIFF_VROWS, 2 * tq), F32),
                        pltpu.VMEM((2, DIFF_HEADS, ck, 2 * tq), F32)],
        compiler_params=_params("arbitrary", "arbitrary"),
        name="diff_attention",
    )(qt, k_all, vt_all, *lqk, subln_g.reshape(HEAD_DIM, 1).astype(F32))


def _ctx_attn_kernel(na_ref, sw_ref, df_ref, sink_ref, lq1_ref, lk1_ref, lq2_ref, lk2_ref, g_ref,
                     nb_o, sw_o, df_o, *, lambda_init):
    n = na_ref.shape[0]
    na = na_ref[...]
    outs = []
    for h in range(NA_HEADS):
        sl = slice(h * HEAD_DIM, (h + 1) * HEAD_DIM)
        (p,), l = _softmax_parts([_nt_dot(na[:, sl], na[:, 256 + h * HEAD_DIM:256 + (h + 1) * HEAD_DIM])])
        outs.append(_dot(p.astype(BF16), na[:, 512 + h * HEAD_DIM:512 + (h + 1) * HEAD_DIM]) / l)
    nb_o[...] = jnp.concatenate(outs, axis=1).astype(BF16)
    sw = sw_ref[...]
    outs = []
    for hq in range(4):
        kv = hq // 2
        k = sw[:, 256 + kv * HEAD_DIM:256 + (kv + 1) * HEAD_DIM]
        v = sw[:, 384 + kv * HEAD_DIM:384 + (kv + 1) * HEAD_DIM]
        sk = jnp.broadcast_to(sink_ref[0:1, hq:hq + 1], (n, 1))
        (p,), l = _softmax_parts([_nt_dot(sw[:, hq * HEAD_DIM:(hq + 1) * HEAD_DIM], k)], extra=sk)
        outs.append(_dot(p.astype(BF16), v) / l)
    sw_o[...] = jnp.concatenate(outs, axis=1).astype(BF16)
    lam = _diff_lambda(lq1_ref, lk1_ref, lq2_ref, lk2_ref, lambda_init)
    df = df_ref[...]
    outs = []
    for h in range(DIFF_HEADS):
        sl = slice(h * HEAD_DIM, (h + 1) * HEAD_DIM)
        q2 = _stack_maps(df[:, sl])
        (p,), l = _softmax_parts([_nt_dot(q2, df[:, 256 + h * HEAD_DIM:256 + (h + 1) * HEAD_DIM])])
        o = _dot(p.astype(BF16), df[:, 512 + h * HEAD_DIM:512 + (h + 1) * HEAD_DIM]) / l
        outs.append(_subln(o[0:n], o[n:2 * n], lam, g_ref[...], lambda_init))
    df_o[...] = jnp.concatenate(outs, axis=1).astype(BF16)


def _ctx_attention(na_c, sw_c, df_c, sink, lqk, subln_g, lambda_init, nbatch, n_ctx):
    sink_pad = jnp.zeros((1, 128), F32).at[0, 0:4].set(sink.astype(F32))
    vec = pl.BlockSpec((1, DIFF_QK_DIM), lambda b: (0, 0))
    out = jax.ShapeDtypeStruct((nbatch * n_ctx, 256), BF16)
    return pl.pallas_call(
        functools.partial(_ctx_attn_kernel, lambda_init=lambda_init),
        grid=(nbatch,),
        in_specs=[pl.BlockSpec((n_ctx, 768), lambda b: (b, 0)),
                  pl.BlockSpec((n_ctx, 512), lambda b: (b, 0)),
                  pl.BlockSpec((n_ctx, 768), lambda b: (b, 0)),
                  pl.BlockSpec((1, 128), lambda b: (0, 0)),
                  vec, vec, vec, vec,
                  pl.BlockSpec((1, HEAD_DIM), lambda b: (0, 0))],
        out_specs=[pl.BlockSpec((n_ctx, 256), lambda b: (b, 0))] * 3,
        out_shape=[out, out, out],
        compiler_params=_params("arbitrary"),
        name="ctx_attention",
    )(na_c, sw_c, df_c, sink_pad, *lqk, subln_g.reshape(1, HEAD_DIM).astype(F32))


def _merge_kernel(u_ref, yf_ref, yr_ref, d_ref, gw_ref, gb_ref, yb_ref, yc_ref, yd_ref, gate_ref,
                  wb_ref, wo_ref, x_ref, g1_ref, n2_ref, sc2_ref, sh2_ref, rw_ref,
                  x1_o, h2_o, h2p_o, lg_o):
    y = u_ref[...] * d_ref[...] + yf_ref[0, 0] + yr_ref[0, 0]
    a = jax.nn.gelu(y, approximate=True)
    ya = a * jax.nn.sigmoid(_dot(a.astype(BF16), gw_ref[...]) + gb_ref[...])
    branches = (ya.astype(BF16), yb_ref[...], yc_ref[...], yd_ref[...])
    acc = None
    for i in range(4):
        t = gate_ref[:, i * D_MODEL:(i + 1) * D_MODEL].astype(F32) * _dot(branches[i], wb_ref[i])
        acc = t if acc is None else acc + t
    mixed = _dot(acc.astype(BF16), wo_ref[...])
    x1 = x_ref[...] + g1_ref[0] * mixed
    x1_o[...] = x1
    h2 = _rms(x1) * n2_ref[...]
    h2 = h2 * (1.0 + sc2_ref[0]) + sh2_ref[0]
    h2_o[...] = h2.astype(BF16)
    h2p_o[...] = _pack_rows(h2[:, 0:HALF_D], h2[:, HALF_D:D_MODEL])
    lg_o[...] = lax.dot_general(rw_ref[...], h2, (((1,), (1,)), ((), ())), preferred_element_type=F32,
                                precision=HIGHEST)


def _merge(u, y_bm, y_start, rows_per_seq, s5_d, glu_w, glu_b, yb, yc, yd, gate, wb, wo, x2d, g1, norm2_g, sc2, sh2,
           router_w, *, rows_per_mod):
    rows = x2d.shape[0]
    tm = min(512, rows_per_seq)
    assert rows % tm == 0 and rows_per_mod % tm == 0 and rows_per_seq % tm == 0 and y_start % tm == 0
    tiles_per_seq = rows_per_seq // tm

    def mod_map(i):
        return ((i * tm) // rows_per_mod, 0, 0)

    def y_spec(d):
        return pl.BlockSpec((1, 1, tm, 256), lambda i: (d, i // tiles_per_seq, y_start // tm + i % tiles_per_seq, 0))

    row = lambda w: pl.BlockSpec((tm, w), lambda i: (i, 0))
    full = lambda *shape: pl.BlockSpec(shape, lambda i: (0,) * len(shape))
    mod = pl.BlockSpec((1, 1, D_MODEL), mod_map)
    return pl.pallas_call(
        _merge_kernel,
        grid=(rows // tm,),
        in_specs=[row(256), y_spec(0), y_spec(1), full(1, 256), full(256, 256), full(1, 256),
                  row(256), row(256), row(256), row(GATE_WIDTH),
                  full(4, 256, D_MODEL), full(D_MODEL, D_MODEL), row(D_MODEL),
                  mod, full(1, D_MODEL), mod, mod, full(N_EXPERTS, D_MODEL)],
        out_specs=[row(D_MODEL), row(D_MODEL), row(HALF_D), pl.BlockSpec((N_EXPERTS, tm), lambda i: (0, i))],
        out_shape=[jax.ShapeDtypeStruct((rows, D_MODEL), F32),
                   jax.ShapeDtypeStruct((rows, D_MODEL), BF16),
                   jax.ShapeDtypeStruct((rows, HALF_D), jnp.int32),
                   jax.ShapeDtypeStruct((N_EXPERTS, rows), F32)],
        compiler_params=_params("arbitrary"),
        name="merge",
    )(u, y_bm, y_bm, s5_d.reshape(1, 256).astype(F32), glu_w.astype(BF16), glu_b.reshape(1, 256).astype(F32),
      yb, yc, yd, gate, wb.astype(BF16), wo.astype(BF16), x2d, g1, norm2_g.reshape(1, D_MODEL), sc2, sh2,
      router_w.astype(F32).T)


def _router_kernel(lg_ref, b_ref, tri_ref, idx_ref, rank_ref, w_ref, cnt_ref, base_ref):
    tr = lg_ref.shape[1]
    gsz = N_EXPERTS // N_EXPERT_GROUPS
    sc = jax.nn.sigmoid(lg_ref[...])
    bi = sc + b_ref[...]
    e_iota = lax.broadcasted_iota(jnp.int32, (gsz, tr), 0).astype(F32)
    groups = [bi[g * gsz:(g + 1) * gsz] for g in range(N_EXPERT_GROUPS)]
    gs = []
    for bg in groups:
        m1 = bg.max(axis=0, keepdims=True)
        i1 = jnp.where(bg == m1, e_iota, float(gsz)).min(axis=0, keepdims=True)
        m2 = jnp.where(e_iota == i1, -jnp.inf, bg).max(axis=0, keepdims=True)
        gs.append(m1 + m2)
    v = []
    for g in range(N_EXPERT_GROUPS):
        rank = jnp.zeros((1, tr), F32)
        for g2 in range(N_EXPERT_GROUPS):
            if g2 == g:
                continue
            beats = (gs[g2] >= gs[g]) if g2 < g else (gs[g2] > gs[g])
            rank = rank + jnp.where(beats, 1.0, 0.0)
        v.append(jnp.where(rank < TOPK_GROUPS, groups[g], NEG_INF))
    flat = [e_iota + float(g * gsz) for g in range(N_EXPERT_GROUPS)]
    sel = [jnp.zeros((gsz, tr), F32) for _ in range(N_EXPERT_GROUPS)]
    picks = []
    for _ in range(TOP_K):
        m = v[0].max(axis=0, keepdims=True)
        for g in range(1, N_EXPERT_GROUPS):
            m = jnp.maximum(m, v[g].max(axis=0, keepdims=True))
        am = jnp.where(v[0] == m, flat[0], float(N_EXPERTS)).min(axis=0, keepdims=True)
        for g in range(1, N_EXPERT_GROUPS):
            am = jnp.minimum(am, jnp.where(v[g] == m, flat[g], float(N_EXPERTS)).min(axis=0, keepdims=True))
        hits = []
        for g in range(N_EXPERT_GROUPS):
            hit = flat[g] == am
            hits.append(hit)
            sel[g] = jnp.where(hit, 1.0, sel[g])
            v[g] = jnp.where(hit, -jnp.inf, v[g])
        picks.append((am, hits))
    scg = [sc[g * gsz:(g + 1) * gsz] for g in range(N_EXPERT_GROUPS)]
    den = (sel[0] * scg[0]).sum(axis=0, keepdims=True)
    for g in range(1, N_EXPERT_GROUPS):
        den = den + (sel[g] * scg[g]).sum(axis=0, keepdims=True)

    @pl.when(pl.program_id(0) == 0)
    def _():
        base_ref[...] = jnp.zeros_like(base_ref)

    sel_all = jnp.concatenate(sel, axis=0)
    before = _dot(sel_all.astype(jnp.bfloat16), tri_ref[...]) + base_ref[...]
    for k, (am, hits) in enumerate(picks):
        wk = jnp.zeros((1, tr), F32)
        rk = jnp.zeros((1, tr), F32)
        for g in range(N_EXPERT_GROUPS):
            wk = wk + jnp.where(hits[g], scg[g], 0.0).sum(axis=0, keepdims=True)
            rk = rk + jnp.where(hits[g], before[g * gsz:(g + 1) * gsz], 0.0).sum(axis=0, keepdims=True)
        idx_ref[k:k + 1, :] = am.astype(jnp.int32)
        rank_ref[k:k + 1, :] = rk.astype(jnp.int32)
        w_ref[k:k + 1, :] = wk / den * ROUTED_SCALE
    idx_ref[TOP_K:8, :] = jnp.zeros((8 - TOP_K, tr), jnp.int32)
    rank_ref[TOP_K:8, :] = jnp.zeros((8 - TOP_K, tr), jnp.int32)
    w_ref[TOP_K:8, :] = jnp.zeros((8 - TOP_K, tr), F32)
    base_ref[...] += sel_all.sum(axis=1, keepdims=True)
    cnt_ref[...] = base_ref[...].astype(jnp.int32)


ROUTER_TILE = 512


def _router(logits_t, router_b):
    ne, rows = logits_t.shape
    tr = ROUTER_TILE
    assert rows % tr == 0
    tri = jnp.asarray(np.triu(np.ones((tr, tr), np.float32), k=1), jnp.bfloat16)
    pick = pl.BlockSpec((8, tr), lambda i: (0, i))
    return pl.pallas_call(
        _router_kernel,
        grid=(rows // tr,),
        in_specs=[pl.BlockSpec((ne, tr), lambda i: (0, i)),
                  pl.BlockSpec((ne, 1), lambda i: (0, 0)),
                  pl.BlockSpec((tr, tr), lambda i: (0, 0))],
        out_specs=[pick, pick, pick, pl.BlockSpec((ne, 1), lambda i: (0, 0))],
        out_shape=[jax.ShapeDtypeStruct((8, rows), jnp.int32),
                   jax.ShapeDtypeStruct((8, rows), jnp.int32),
                   jax.ShapeDtypeStruct((8, rows), F32),
                   jax.ShapeDtypeStruct((ne, 1), jnp.int32)],
        scratch_shapes=[pltpu.VMEM((ne, 1), F32)],
        compiler_params=_params("arbitrary"),
        name="router",
    )(logits_t, router_b.reshape(ne, 1).astype(F32), tri)


MOE_BLOCK = 512
MOE_TOKENS = 256
MOE_COMBINE_GROUPS = 4
HALF_D = D_MODEL // 2


def _pack_rows(lo, hi):
    lo_b = pltpu.bitcast(lo.astype(jnp.bfloat16).astype(F32), jnp.uint32)
    hi_b = pltpu.bitcast(hi.astype(jnp.bfloat16).astype(F32), jnp.uint32)
    return pltpu.bitcast((hi_b & jnp.uint32(0xFFFF0000)) | (lo_b >> 16), jnp.int32)


def _unpack_rows(words):
    u = pltpu.bitcast(words, jnp.uint32)
    lo = pltpu.bitcast(u << 16, F32)
    hi = pltpu.bitcast(u & jnp.uint32(0xFFFF0000), F32)
    return lo, hi


def _swiglu(x_bf16, wgu, wd):
    hgu = _dot(x_bf16, wgu)
    g = hgu[:, 0:EXPERT_HIDDEN]
    a = g * jax.nn.sigmoid(g) * hgu[:, EXPERT_HIDDEN:2 * EXPERT_HIDDEN]
    return _dot(a.astype(BF16), wd)


SC_CORES = 2
SC_SUBCORES = 16
SC_STREAM_ROWS = 128


def _sc_for_each_chunk(total, body):
    chunk = SC_STREAM_ROWS
    assert total % chunk == 0
    nchunk = total // chunk
    per_worker = pl.cdiv(nchunk, SC_CORES * SC_SUBCORES)
    first = (lax.axis_index("s") * SC_CORES + lax.axis_index("c")) * per_worker

    @pl.loop(0, per_worker)
    def _(j):
        @pl.when(first + j < nchunk)
        def _():
            body((first + j) * chunk)


def _sc_scatter_rows(rows, slots, n_out):
    total, n = rows.shape
    chunk = SC_STREAM_ROWS
    mesh = plsc.VectorSubcoreMesh(core_axis_name="c", subcore_axis_name="s")

    @functools.partial(
        pl.kernel, mesh=mesh,
        out_type=jax.ShapeDtypeStruct((n_out, n), jnp.int32),
        scratch_types=[pltpu.VMEM((8, chunk), jnp.int32),
                       pltpu.VMEM((chunk, n), jnp.int32),
                       pltpu.SemaphoreType.DMA],
        name="moe_dispatch_sc",
    )
    def scatter(rows_hbm, slot_hbm, out_hbm, idx_v, rows_v, sem):
        def body(off):
            pltpu.sync_copy(rows_hbm.at[pl.ds(off, chunk)], rows_v)
            pltpu.sync_copy(slot_hbm.at[:, pl.ds(off, chunk)], idx_v)
            for k in range(TOP_K):
                pltpu.async_copy(rows_v, out_hbm.at[idx_v.at[k]], sem).wait()

        _sc_for_each_chunk(total, body)

    return scatter(rows, slots)


def _experts_kernel(be_ref, nv_ref, nb_ref, xs_ref, wg_ref, wu_ref, wd_ref, ys_ref, wgu_bf, wd_bf):
    b = pl.program_id(0)

    @pl.when(b < nb_ref[0])
    def _():
        @pl.when((b == 0) | (be_ref[b] != be_ref[jnp.maximum(b - 1, 0)]))
        def _():
            wgu_bf[:, 0:EXPERT_HIDDEN] = wg_ref[0, 0].astype(BF16)
            wgu_bf[:, EXPERT_HIDDEN:2 * EXPERT_HIDDEN] = wu_ref[0, 0].astype(BF16)
            wd_bf[...] = wd_ref[0, 0].astype(BF16)

        words = xs_ref[...]
        live = lax.broadcasted_iota(jnp.int32, words.shape, 0) < nv_ref[b]
        lo, hi = _unpack_rows(jnp.where(live, words, 0))
        x = jnp.concatenate([lo, hi], axis=1).astype(BF16)
        y = _swiglu(x, wgu_bf[...], wd_bf[...])
        ys_ref[...] = _pack_rows(y[:, 0:HALF_D], y[:, HALF_D:D_MODEL])


def _sc_gather_rows(table, indices):
    m, n = indices.shape[0], table.shape[1]
    chunk = SC_STREAM_ROWS
    mesh = plsc.VectorSubcoreMesh(core_axis_name="c", subcore_axis_name="s")

    @functools.partial(
        pl.kernel, mesh=mesh,
        out_type=jax.ShapeDtypeStruct((m, n), jnp.int32),
        scratch_types=[pltpu.VMEM((chunk,), jnp.int32),
                       pltpu.VMEM((chunk, n), jnp.int32),
                       pltpu.SemaphoreType.DMA],
        name="moe_gather_sc",
    )
    def gather(table_hbm, idx_hbm, out_hbm, idx_v, rows_v, sem):
        def body(off):
            pltpu.sync_copy(idx_hbm.at[pl.ds(off, chunk)], idx_v)
            pltpu.async_copy(table_hbm.at[idx_v], rows_v, sem).wait()
            pltpu.sync_copy(rows_v, out_hbm.at[pl.ds(off, chunk)])

        _sc_for_each_chunk(m, body)

    return gather(table, indices)


def _combine_into_kernel(w_ref, rows_ref, h_ref, wsgu_ref, wsd_ref, x1_ref, g2_ref, fg_ref, prev_ref, o_ref, *, final):
    del prev_ref
    _combine_kernel(w_ref, rows_ref, h_ref, wsgu_ref, wsd_ref, x1_ref, g2_ref, fg_ref, o_ref, final=final)


def _combine_kernel(w_ref, rows_ref, h_ref, wsgu_ref, wsd_ref, x1_ref, g2_ref, fg_ref, o_ref, *, final):
    shared = _swiglu(h_ref[...], wsgu_ref[...], wsd_ref[...])
    acc_lo = shared[:, 0:HALF_D]
    acc_hi = shared[:, HALF_D:D_MODEL]
    w = w_ref[...]
    for k in range(TOP_K):
        lo, hi = _unpack_rows(rows_ref[k])
        acc_lo = acc_lo + w[:, k:k + 1] * lo
        acc_hi = acc_hi + w[:, k:k + 1] * hi
    x2 = x1_ref[...] + g2_ref[0] * jnp.concatenate([acc_lo, acc_hi], axis=1)
    if final:
        x2 = _rms(x2) * fg_ref[...]
    o_ref[...] = x2


def _moe(h2, h2p, picks, lp, x1, g2, final_g, *, rows_per_mod, final):
    idx, rank, wsel, counts = picks
    rows = h2.shape[0]
    tt = MOE_TOKENS
    blk = MOE_BLOCK
    assert rows % tt == 0 and rows_per_mod % tt == 0 and (rows * TOP_K) % blk == 0
    ntile = rows // tt
    nblock = rows * TOP_K // blk + N_EXPERTS

    cnt = counts.reshape(N_EXPERTS)
    padded = (cnt + blk - 1) // blk * blk
    e_ids = jnp.arange(N_EXPERTS, dtype=jnp.int32)
    pends = jnp.sum(jnp.where(e_ids[None, :] <= e_ids[:, None], padded[None, :], 0), axis=1)
    pstart = (pends - padded).astype(jnp.int32)
    nb_used = (jnp.sum(padded) // blk).astype(jnp.int32).reshape(1)
    first_row = jnp.arange(nblock, dtype=jnp.int32) * blk
    block_e = jnp.minimum(jnp.sum(pends[None, :] <= first_row[:, None], axis=1), N_EXPERTS - 1).astype(jnp.int32)
    slot = rank + jnp.sum(jnp.where(idx[:, :, None] == jnp.arange(N_EXPERTS, dtype=jnp.int32), pstart, 0), axis=-1)
    seg_end = jnp.sum(jnp.where(block_e[:, None] == e_ids[None, :], (cnt + pstart)[None, :], 0), axis=1)
    n_valid = jnp.clip(seg_end - first_row, 0, blk).astype(jnp.int32)

    xs = _sc_scatter_rows(h2p, slot, nblock * blk)

    def blk_map(b, be, nv, nb):
        return (jnp.minimum(b, nb[0] - 1), 0)

    layer = lp['layer']

    def w_map(b, be, nv, nb):
        return (layer, be[jnp.minimum(b, nb[0] - 1)], 0, 0)

    ys = pl.pallas_call(
        _experts_kernel,
        grid_spec=pltpu.PrefetchScalarGridSpec(
            num_scalar_prefetch=3,
            grid=(nblock,),
            in_specs=[pl.BlockSpec((blk, HALF_D), blk_map),
                      pl.BlockSpec((1, 1, D_MODEL, EXPERT_HIDDEN), w_map),
                      pl.BlockSpec((1, 1, D_MODEL, EXPERT_HIDDEN), w_map),
                      pl.BlockSpec((1, 1, EXPERT_HIDDEN, D_MODEL), w_map)],
            out_specs=pl.BlockSpec((blk, HALF_D), blk_map),
            scratch_shapes=[pltpu.VMEM((D_MODEL, 2 * EXPERT_HIDDEN), BF16),
                            pltpu.VMEM((EXPERT_HIDDEN, D_MODEL), BF16)]),
        out_shape=jax.ShapeDtypeStruct((nblock * blk, HALF_D), jnp.int32),
        compiler_params=_params("arbitrary"),
        name="moe_experts",
    )(block_e, n_valid, nb_used, xs, lp['exp_w_gate'], lp['exp_w_up'], lp['exp_w_down'])

    wsgu = jnp.concatenate([lp['sh_w_gate'], lp['sh_w_up']], axis=1).astype(BF16)
    wsd = lp['sh_w_down'].astype(BF16)
    ngroup = MOE_COMBINE_GROUPS if ntile % MOE_COMBINE_GROUPS == 0 and ntile >= 4 * MOE_COMBINE_GROUPS else 1
    gtile = ntile // ngroup
    grows = gtile * tt
    full = lambda *shape: pl.BlockSpec(shape, lambda i: (0,) * len(shape))
    wsel_t = wsel.T
    fg = final_g.reshape(1, D_MODEL).astype(F32)
    out = None
    for p in range(ngroup):
        gathered = _sc_gather_rows(ys, slot[0:TOP_K, p * grows:(p + 1) * grows].reshape(TOP_K * grows))
        gathered = gathered.reshape(TOP_K, grows, HALF_D)
        row = lambda width, p=p: pl.BlockSpec((tt, width), lambda i: (p * gtile + i, 0))
        in_specs = [row(8), pl.BlockSpec((TOP_K, tt, HALF_D), lambda i: (0, i, 0)), row(D_MODEL),
                    full(D_MODEL, 2 * EXPERT_HIDDEN), full(EXPERT_HIDDEN, D_MODEL), row(D_MODEL),
                    pl.BlockSpec((1, 1, D_MODEL), lambda i, p=p: (((p * gtile + i) * tt) // rows_per_mod, 0, 0)),
                    full(1, D_MODEL)]
        args = [wsel_t, gathered, h2, wsgu, wsd, x1, g2, fg]
        kern = functools.partial(_combine_kernel, final=final)
        aliases = {}
        if out is not None:
            in_specs.append(pl.BlockSpec(memory_space=pl.ANY))
            args.append(out)
            aliases = {len(args) - 1: 0}
            kern = functools.partial(_combine_into_kernel, final=final)
        out = pl.pallas_call(
            kern,
            grid=(gtile,),
            in_specs=in_specs,
            out_specs=row(D_MODEL),
            out_shape=jax.ShapeDtypeStruct((rows, D_MODEL), F32),
            input_output_aliases=aliases,
            compiler_params=_params("arbitrary"),
            name="moe_combine",
        )(*args)
    return out


def _reorder_w_in(w_in):
    split = 256 + 768 + 512 + 768
    return jnp.concatenate([w_in[:, split:], w_in[:, :split]], axis=1).astype(BF16)


def _mods(mod_row_block):
    return [mod_row_block[:, None, k * D_MODEL:(k + 1) * D_MODEL] for k in range(6)]


def _moe_block(h2, h2p, logits, lp, x1, g2, final_g, *, rows_per_mod, final):
    picks = _router(logits, lp['router_b'])
    return _moe(h2, h2p, picks, lp, x1, g2, final_g, rows_per_mod=rows_per_mod, final=final)


def _layer(x2d, xc2d, c16, lp, layer_idx, tables, final_g, *, nbatch, seq, n_ctx, with_ctx_out, final):
    lambda_init = 0.8 - 0.6 * math.exp(-0.3 * layer_idx)
    mod = _ada_mod(c16, lp['ada_w'].astype(F32), lp['ada_b'].astype(F32))
    sh1, sc1, g1, sh2, sc2, g2 = _mods(mod[0:nbatch])
    csh1, csc1, cg1, csh2, csc2, cg2 = _mods(mod[nbatch:nbatch + 1])
    w_in = _reorder_w_in(lp['w_in'])
    rows_lat = nbatch * seq
    rows_ctx = nbatch * n_ctx

    gate, u, na, sw, df, na_qt, sw_qt, df_qt, na_vt, sw_vt = _inproj(x2d, lp['norm1_g'], sc1, sh1, w_in, tables,
                                                                     rows_per_mod=seq, rope=True, seq=seq)
    gate_c, u_c, na_c, sw_c, df_c = _inproj(xc2d, lp['norm1_g'], csc1, csh1, w_in, tables,
                                            rows_per_mod=rows_ctx, rope=False, seq=seq)

    win, wout, a_re, a_im = _s5_params(lp['s5_lambda_re'], lp['s5_lambda_im'], lp['s5_log_step'],
                                       lp['s5_b_re'], lp['s5_b_im'], lp['s5_c_re'], lp['s5_c_im'])
    u_tm = jnp.concatenate([u_c.reshape(nbatch, n_ctx, 256).transpose(1, 0, 2),
                            u.reshape(nbatch, seq, 256).transpose(1, 0, 2)], axis=0)
    y_tm = _s5_scan(u_tm, win, wout, a_re, a_im, n_ctx)
    y_bm = y_tm.transpose(0, 2, 1, 3)

    lqk = [lp[k].reshape(1, DIFF_QK_DIM).astype(F32) for k in ('diff_lq1', 'diff_lk1', 'diff_lq2', 'diff_lk2')]
    bias = _na_bias_table(lp['na_rpb'], seq // GRID_W)
    yb, yc = _local_attention(na_qt, na_vt, na, na_c, bias, sw_qt, sw_vt, sw, sw_c, lp['swa_sink'], nbatch, seq, n_ctx)
    yd = _diff_attention(df_qt, df, df_c, lqk, lp['diff_subln_g'], lambda_init, nbatch, seq, n_ctx)

    merge_w = (lp['s5_d'], lp['s5_glu_w'], lp['s5_glu_b'])
    x1, h2, h2p, logits = _merge(u, y_bm, 0, seq, *merge_w, yb, yc, yd, gate, lp['w_branch'], lp['w_out'], x2d, g1,
                                 lp['norm2_g'], sc2, sh2, lp['router_w'], rows_per_mod=seq)
    x_out = _moe_block(h2, h2p, logits, lp, x1, g2, final_g, rows_per_mod=seq, final=final)

    xc_out = None
    if with_ctx_out:
        yb_c, yc_c, yd_c = _ctx_attention(na_c, sw_c, df_c, lp['swa_sink'], lqk, lp['diff_subln_g'],
                                          lambda_init, nbatch, n_ctx)
        x1c, h2c, h2pc, logits_c = _merge(u_c, y_bm, seq, n_ctx, *merge_w, yb_c, yc_c, yd_c, gate_c,
                                          lp['w_branch'], lp['w_out'], xc2d, cg1, lp['norm2_g'], csc2, csh2,
                                          lp['router_w'], rows_per_mod=rows_ctx)
        xc_out = _moe_block(h2c, h2pc, logits_c, lp, x1c, cg2, final_g, rows_per_mod=rows_ctx, final=False)
    return x_out, xc_out


def kernel(x, c, ctx, c_ctx, ada_w, ada_b, norm1_g, norm2_g, w_in, s5_lambda_re, s5_lambda_im, s5_log_step,
           s5_b_re, s5_b_im, s5_c_re, s5_c_im, s5_d, s5_glu_w, s5_glu_b, na_rpb, swa_sink, diff_lq1, diff_lk1,
           diff_lq2, diff_lk2, diff_subln_g, w_branch, w_out, router_w, router_b, exp_w_gate, exp_w_up,
           exp_w_down, sh_w_gate, sh_w_up, sh_w_down, final_g):
    nbatch, seq, d = x.shape
    n_ctx = ctx.shape[1]
    depth = ada_w.shape[0]
    assert d == D_MODEL and nbatch == 8
    stacked = dict(ada_w=ada_w, ada_b=ada_b, norm1_g=norm1_g, norm2_g=norm2_g, w_in=w_in,
                   s5_lambda_re=s5_lambda_re, s5_lambda_im=s5_lambda_im, s5_log_step=s5_log_step,
                   s5_b_re=s5_b_re, s5_b_im=s5_b_im, s5_c_re=s5_c_re, s5_c_im=s5_c_im, s5_d=s5_d,
                   s5_glu_w=s5_glu_w, s5_glu_b=s5_glu_b, na_rpb=na_rpb, swa_sink=swa_sink,
                   diff_lq1=diff_lq1, diff_lk1=diff_lk1, diff_lq2=diff_lq2, diff_lk2=diff_lk2,
                   diff_subln_g=diff_subln_g, w_branch=w_branch, w_out=w_out, router_w=router_w,
                   router_b=router_b, exp_w_gate=exp_w_gate, exp_w_up=exp_w_up, exp_w_down=exp_w_down,
                   sh_w_gate=sh_w_gate, sh_w_up=sh_w_up, sh_w_down=sh_w_down)
    tables = _rope_tables(seq)
    c16 = jnp.concatenate([c.astype(F32), c_ctx.reshape(1, d).astype(F32),
                           jnp.zeros((16 - nbatch - 1, d), F32)], axis=0)
    x2d = x.reshape(nbatch * seq, d).astype(F32)
    xc2d = ctx.reshape(nbatch * n_ctx, d).astype(F32)
    for l in range(depth):
        routed = ('exp_w_gate', 'exp_w_up', 'exp_w_down')
        lp = {k: (v.astype(F32) if k in routed else v[l]) for k, v in stacked.items()}
        lp['layer'] = l
        last = l == depth - 1
        x2d, xc2d = _layer(x2d, xc2d, c16, lp, l, tables, final_g, nbatch=nbatch, seq=seq, n_ctx=n_ctx,
                           with_ctx_out=not last, final=last)
    return x2d.reshape(nbatch, seq, d)
```

```python
import functools
import math

import numpy as np
import jax
import jax.numpy as jnp
from jax import lax
from jax.experimental import pallas as pl
from jax.experimental.pallas import tpu as pltpu
from jax.experimental.pallas import tpu_sc as plsc

F32 = jnp.float32
BF16 = jnp.bfloat16
HIGHEST = lax.Precision.HIGHEST

GRID_W = 64
EPS = 1e-6
NEG_INF = -1e30
ROPE_BASE = 10000.0
D_MODEL = 1024
BRANCH_WIDTH = 256
HEAD_DIM = 64
S5_GROUP = 16
S5_GROUPS = 16
S5_STATE = 64
S5_FLAT = S5_GROUPS * S5_STATE
NA_HEADS = 4
NA_WIN_ROWS = 8
NA_WIN_COLS = 16
SWA_KV_HEADS = 2
SWA_WINDOW = 128
DIFF_HEADS = 4
DIFF_QK_DIM = 32
N_EXPERTS = 64
N_EXPERT_GROUPS = 8
TOPK_GROUPS = 4
TOP_K = 6
EXPERT_HIDDEN = 256
ROUTED_SCALE = 2.5
GATE_WIDTH = 4 * D_MODEL

VMEM_LIMIT = 56 * 1024 * 1024


def _params(*sem):
    return pltpu.CompilerParams(dimension_semantics=sem, vmem_limit_bytes=VMEM_LIMIT)


def _nt_dot(a, b):
    return lax.dot_general(a, b, (((1,), (1,)), ((), ())), preferred_element_type=F32)


def _dot(a, b):
    return jnp.dot(a, b, preferred_element_type=F32)


def _rms(x):
    return x * lax.rsqrt(jnp.mean(x * x, axis=-1, keepdims=True) + EPS)


def _ada_kernel(c_ref, w_ref, b_ref, o_ref):
    c = c_ref[...]
    s = c * jax.nn.sigmoid(c)
    o_ref[...] = jnp.dot(s, w_ref[...], preferred_element_type=F32, precision=HIGHEST) + b_ref[...]


def _ada_mod(cc, w, b):
    rows, d = cc.shape
    width = w.shape[1]
    tn = 1536
    return pl.pallas_call(
        _ada_kernel,
        grid=(width // tn,),
        in_specs=[pl.BlockSpec((rows, d), lambda j: (0, 0)),
                  pl.BlockSpec((d, tn), lambda j: (0, j)),
                  pl.BlockSpec((1, tn), lambda j: (0, j))],
        out_specs=pl.BlockSpec((rows, tn), lambda j: (0, j)),
        out_shape=jax.ShapeDtypeStruct((rows, width), F32),
        compiler_params=_params("arbitrary"),
        name="ada_mod",
    )(cc, w, b.reshape(1, width))


_C_GATE = 0
_C_U = GATE_WIDTH
_C_NA = _C_U + 256
_C_SW = _C_NA + 768
_C_DF = _C_SW + 512
_C_END = _C_DF + 768


def _rope_apply(x, cos, sins, half):
    outs = []
    for j in range(x.shape[1] // 128):
        xs = x[:, j * 128:(j + 1) * 128]
        lane = lax.broadcasted_iota(jnp.int32, xs.shape, 1)
        lo = (lane % (2 * half)) < half
        partner = jnp.where(lo, pltpu.roll(xs, 128 - half, 1), pltpu.roll(xs, half, 1))
        outs.append(xs * cos + partner * sins)
    return outs[0] if len(outs) == 1 else jnp.concatenate(outs, axis=1)


def _inproj_kernel(x_ref, g_ref, sc_ref, sh_ref, w_ref, c64_ref, s64_ref, c32_ref, s32_ref,
                   gate_o, u_o, na_o, sw_o, df_o, *qt_os, rope):
    h = _rms(x_ref[...]) * g_ref[...]
    h = h * (1.0 + sc_ref[0]) + sh_ref[0]
    hb = h.astype(BF16)

    def mm(c0, c1):
        return _dot(hb, w_ref[:, c0:c1])

    for k in range(GATE_WIDTH // 512):
        gate_o[:, k * 512:(k + 1) * 512] = jax.nn.sigmoid(mm(k * 512, (k + 1) * 512)).astype(BF16)
    u_o[...] = mm(_C_U, _C_U + 256)

    na = mm(_C_NA, _C_NA + 768)
    naq = na[:, 0:256] * (HEAD_DIM ** -0.5 * (LOG2E if rope else 1.0))
    na_o[:, 0:256] = naq.astype(BF16)
    na_o[:, 256:768] = na[:, 256:768].astype(BF16)

    sw = mm(_C_SW, _C_SW + 512)
    swq, swk = sw[:, 0:256], sw[:, 256:384]
    if rope:
        swq = _rope_apply(swq, c64_ref[...], s64_ref[...], 16)
        swk = _rope_apply(swk, c64_ref[...], s64_ref[...], 16)
    swq = swq * (HEAD_DIM ** -0.5 * (LOG2E if rope else 1.0))
    sw_o[:, 0:256] = swq.astype(BF16)
    sw_o[:, 256:384] = swk.astype(BF16)
    sw_o[:, 384:512] = sw[:, 384:512].astype(BF16)

    df = mm(_C_DF, _C_DF + 768)
    dfq, dfk = df[:, 0:256], df[:, 256:512]
    if rope:
        dfq = _rope_apply(dfq, c32_ref[...], s32_ref[...], 8)
        dfk = _rope_apply(dfk, c32_ref[...], s32_ref[...], 8)
    dfq = dfq * (DIFF_QK_DIM ** -0.5 * (LOG2E if rope else 1.0))
    df_o[:, 0:256] = dfq.astype(BF16)
    df_o[:, 256:512] = dfk.astype(BF16)
    df_o[:, 512:768] = df[:, 512:768].astype(BF16)
    if qt_os:
        na_qt_o, sw_qt_o, df_qt_o, na_vt_o, sw_vt_o, df_k_o, df_vt_o = qt_os
        for qt_o, q in ((na_qt_o, naq), (sw_qt_o, swq), (df_qt_o, dfq)):
            qt_o[0] = q.T.astype(BF16)
        na_vt_o[0] = _ones_row_vt(na[:, 512:768].T)
        sw_vt_o[0] = _ones_row_vt(sw[:, 384:512].T)
        df_vt = _ones_row_vt(df[:, 512:768].T)
        for j in range(df_k_o.shape[1]):
            df_k_o[0, j] = dfk[j * DIFF_CK:(j + 1) * DIFF_CK].astype(BF16)
            df_vt_o[0, j] = df_vt[:, j * DIFF_CK:(j + 1) * DIFF_CK]


def _ones_row_vt(vt):
    n = vt.shape[1]
    pad = jnp.where(lax.broadcasted_iota(jnp.int32, (VROWS - HEAD_DIM, n), 0) == 0, 1.0, 0.0)
    parts = []
    for h in range(vt.shape[0] // HEAD_DIM):
        parts += [vt[h * HEAD_DIM:(h + 1) * HEAD_DIM], pad]
    return jnp.concatenate(parts, axis=0).astype(BF16)


def _inproj(x2d, norm_g, sc, sh, w_bf16, tables, *, rows_per_mod, rope, seq):
    rows = x2d.shape[0]
    tm = 512
    assert rows % tm == 0 and rows_per_mod % tm == 0 and seq % tm == 0
    tiles_per_seq = seq // tm

    def mod_map(i):
        return ((i * tm) // rows_per_mod, 0, 0)

    def tab_map(i):
        return (i % tiles_per_seq, 0)

    tab_spec = pl.BlockSpec((tm, 128), tab_map)
    row = lambda w: pl.BlockSpec((tm, w), lambda i: (i, 0))
    out_specs = [row(GATE_WIDTH), row(256), row(768), row(512), row(768)]
    out_shape = [jax.ShapeDtypeStruct((rows, GATE_WIDTH), BF16),
                 jax.ShapeDtypeStruct((rows, 256), F32),
                 jax.ShapeDtypeStruct((rows, 768), BF16),
                 jax.ShapeDtypeStruct((rows, 512), BF16),
                 jax.ShapeDtypeStruct((rows, 768), BF16)]
    if rope:
        for nrow in (256, 256, 256, NA_HEADS * VROWS, SWA_KV_HEADS * VROWS):
            out_specs.append(pl.BlockSpec((1, nrow, tm), lambda i: (i // tiles_per_seq, 0, i % tiles_per_seq)))
            out_shape.append(jax.ShapeDtypeStruct((rows // seq, nrow, seq), BF16))
        assert tm % DIFF_CK == 0
        for shape in ((DIFF_CK, 256), (DIFF_HEADS * VROWS, DIFF_CK)):
            out_specs.append(pl.BlockSpec((1, tm // DIFF_CK) + shape,
                                          lambda i: (i // tiles_per_seq, i % tiles_per_seq, 0, 0)))
            out_shape.append(jax.ShapeDtypeStruct((rows // seq, seq // DIFF_CK) + shape, BF16))
    return pl.pallas_call(
        functools.partial(_inproj_kernel, rope=rope),
        grid=(rows // tm,),
        in_specs=[row(D_MODEL),
                  pl.BlockSpec((1, D_MODEL), lambda i: (0, 0)),
                  pl.BlockSpec((1, 1, D_MODEL), mod_map),
                  pl.BlockSpec((1, 1, D_MODEL), mod_map),
                  pl.BlockSpec((D_MODEL, _C_END), lambda i: (0, 0)),
                  tab_spec, tab_spec, tab_spec, tab_spec],
        out_specs=out_specs,
        out_shape=out_shape,
        compiler_params=_params("arbitrary"),
        name="inproj",
    )(x2d, norm_g.reshape(1, D_MODEL), sc, sh, w_bf16, *tables)


def _rope_tables(seq):
    t = jnp.arange(seq)
    rows = (t // GRID_W).astype(F32)
    cols = (t % GRID_W).astype(F32)
    lane = np.arange(128)
    out = []
    for dim in (64, 32):
        quarter = dim // 4
        inv_freq = ROPE_BASE ** (-jnp.arange(quarter, dtype=F32) / quarter)
        l = lane % dim
        use_col = l >= dim // 2
        fidx = l % quarter
        hi = (l % (dim // 2)) >= quarter
        ang_r = rows[:, None] * inv_freq[None, :]
        ang_c = cols[:, None] * inv_freq[None, :]
        ang = jnp.where(use_col[None, :], ang_c[:, fidx], ang_r[:, fidx])
        out.append(jnp.cos(ang))
        out.append(jnp.where(hi[None, :], jnp.sin(ang), -jnp.sin(ang)))
    return tuple(out)


S5_CHUNK = 128


def _s5_kernel(u_ref, win_ref, wout_ref, are_ref, aim_ref, y_ref, bu_ref, st_ref, *, tc, nb):
    d = pl.program_id(0)
    i = pl.program_id(1)

    @pl.when(i == 0)
    def _():
        st_ref[...] = jnp.zeros_like(st_ref)

    u = u_ref[...].reshape(tc * nb, BRANCH_WIDTH).astype(BF16)
    bu_ref[...] = _dot(u, win_ref[0])
    ar = jnp.broadcast_to(are_ref[0], (nb, S5_FLAT))
    ai = jnp.broadcast_to(aim_ref[0], (nb, S5_FLAT))

    def body(j, carry):
        xr, xi = carry
        t = j + d * (tc - 1 - 2 * j)
        row = pl.multiple_of(t * nb, nb)
        br = bu_ref[pl.ds(row, nb), 0:S5_FLAT]
        bi = bu_ref[pl.ds(row, nb), S5_FLAT:2 * S5_FLAT]
        nr = ar * xr - ai * xi + br
        ni = ar * xi + ai * xr + bi
        bu_ref[pl.ds(row, nb), 0:S5_FLAT] = nr
        bu_ref[pl.ds(row, nb), S5_FLAT:2 * S5_FLAT] = ni
        return nr, ni

    xr, xi = lax.fori_loop(0, tc, body, (st_ref[:, 0:S5_FLAT], st_ref[:, S5_FLAT:2 * S5_FLAT]), unroll=4)
    st_ref[:, 0:S5_FLAT] = xr
    st_ref[:, S5_FLAT:2 * S5_FLAT] = xi
    y = _dot(bu_ref[...].astype(BF16), wout_ref[0])
    y_ref[0] = y.reshape(tc, nb, BRANCH_WIDTH)


def _s5_scan(u_tm, win, wout, a_re, a_im, n_ctx):
    s_len, nb, _ = u_tm.shape
    tc = S5_CHUNK
    assert nb == 8 and s_len % tc == 0 and n_ctx % tc == 0
    nct = n_ctx // tc
    nlt = (s_len - n_ctx) // tc

    def chunk(d, i):
        rev = jnp.where(i < nct, nct - 1 - i, 2 * nct + nlt - 1 - i)
        return jnp.where(d == 0, i, rev)

    def out_chunk(d, i):
        c = chunk(d, i)
        return jnp.where(c < nct, nlt + c, c - nct)

    return pl.pallas_call(
        functools.partial(_s5_kernel, tc=tc, nb=nb),
        grid=(2, nct + nlt),
        in_specs=[pl.BlockSpec((tc, nb, BRANCH_WIDTH), lambda d, i: (chunk(d, i), 0, 0)),
                  pl.BlockSpec((1, BRANCH_WIDTH, 2 * S5_FLAT), lambda d, i: (d, 0, 0)),
                  pl.BlockSpec((1, 2 * S5_FLAT, BRANCH_WIDTH), lambda d, i: (d, 0, 0)),
                  pl.BlockSpec((1, 1, S5_FLAT), lambda d, i: (d, 0, 0)),
                  pl.BlockSpec((1, 1, S5_FLAT), lambda d, i: (d, 0, 0))],
        out_specs=pl.BlockSpec((1, tc, nb, BRANCH_WIDTH), lambda d, i: (d, out_chunk(d, i), 0, 0)),
        out_shape=jax.ShapeDtypeStruct((2, s_len, nb, BRANCH_WIDTH), F32),
        scratch_shapes=[pltpu.VMEM((tc * nb, 2 * S5_FLAT), F32),
                        pltpu.VMEM((nb, 2 * S5_FLAT), F32)],
        compiler_params=_params("arbitrary", "arbitrary"),
        name="s5_scan",
    )(u_tm, win, wout, a_re, a_im)


def _s5_params(lam_re, lam_im, log_step, b_re, b_im, c_re, c_im):
    lr = lam_re.astype(F32)
    li = lam_im.astype(F32)
    dt = jnp.exp(log_step.astype(F32))[..., None]
    mag = jnp.exp(lr * dt)
    a_re = mag * jnp.cos(li * dt)
    a_im = mag * jnp.sin(li * dt)
    nr, ni, den = a_re - 1.0, a_im, lr * lr + li * li
    k_re = ((nr * lr + ni * li) / den)[..., None]
    k_im = ((ni * lr - nr * li) / den)[..., None]
    br = b_re.astype(F32)
    bi = b_im.astype(F32)
    bb_re = k_re * br - k_im * bi
    bb_im = k_re * bi + k_im * br
    eye = jnp.eye(S5_GROUPS, dtype=F32)

    def blockdiag_in(bb):
        m = jnp.einsum('dgpc,gh->dgchp', bb, eye)
        return m.reshape(2, S5_GROUPS * S5_GROUP, S5_GROUPS * S5_STATE)

    def blockdiag_out(cc):
        m = jnp.einsum('dgcp,gh->dgphc', cc, eye)
        return m.reshape(2, S5_GROUPS * S5_STATE, S5_GROUPS * S5_GROUP)

    win = jnp.concatenate([blockdiag_in(bb_re), blockdiag_in(bb_im)], axis=2).astype(BF16)
    wout = jnp.concatenate([blockdiag_out(c_re.astype(F32)), -blockdiag_out(c_im.astype(F32))], axis=1).astype(BF16)
    return win, wout, a_re.reshape(2, 1, S5_FLAT), a_im.reshape(2, 1, S5_FLAT)


def _softmax_parts(scores, extra=None):
    m = scores[0].max(axis=-1, keepdims=True)
    for s in scores[1:]:
        m = jnp.maximum(m, s.max(axis=-1, keepdims=True))
    if extra is not None:
        m = jnp.maximum(m, extra)
    ps = [jnp.exp(s - m) for s in scores]
    l = ps[0].sum(axis=-1, keepdims=True)
    for p in ps[1:]:
        l = l + p.sum(axis=-1, keepdims=True)
    if extra is not None:
        l = l + jnp.exp(extra - m)
    return ps, l


NA_QROWS = 2
NA_KROWS = 10
VROWS = HEAD_DIM + 16


def _na_window_start(r, grid_rows):
    start = jnp.clip(r - NA_WIN_ROWS // 2, 0, grid_rows - NA_WIN_ROWS)
    return (jnp.minimum(start, grid_rows - NA_KROWS) // 2) * 2


def _head_blockdiag(qt, qbd_ref, nheads, rows_per_head):
    n = qt.shape[1]
    row_h = lax.broadcasted_iota(jnp.int32, qt.shape, 0) // rows_per_head
    zero = jnp.zeros_like(qt)
    for h in range(nheads):
        qbd_ref[:, h * n:(h + 1) * n] = jnp.where(row_h == h, qt, zero)


def _na_scores(qt_ref, k_ref, kc_ref, bias_ref, qbd_ref, *, grid_rows):
    nk = NA_KROWS * GRID_W
    off = pl.multiple_of(_na_window_start(NA_QROWS * pl.program_id(1), grid_rows) * GRID_W, 128)
    _head_blockdiag(qt_ref[0], qbd_ref, NA_HEADS, HEAD_DIM)
    qbd = qbd_ref[...]
    s_loc = _dot(k_ref[0, pl.ds(off, nk), :], qbd) + bias_ref[0]
    s_ctx = _dot(kc_ref[0], qbd)
    return off, s_loc, s_ctx


def _na_values(off, s_loc, s_ctx, vt_ref, vct_ref):
    nq = NA_QROWS * GRID_W
    m = jnp.maximum(s_loc.max(axis=0, keepdims=True), s_ctx.max(axis=0, keepdims=True))
    p_loc = jnp.exp2((s_loc - m).astype(BF16))
    p_ctx = jnp.exp2((s_ctx - m).astype(BF16))
    vw = vt_ref[0, :, pl.ds(off, NA_KROWS * GRID_W)]
    outs = []
    for h in range(NA_HEADS):
        rows = slice(h * VROWS, (h + 1) * VROWS)
        cols = slice(h * nq, (h + 1) * nq)
        o = _dot(vw[rows], p_loc[:, cols]) + _dot(vct_ref[0, rows, :], p_ctx[:, cols])
        outs.append(o[0:HEAD_DIM] / o[HEAD_DIM:HEAD_DIM + 1])
    return jnp.concatenate(outs, axis=0).T.astype(BF16)


def _na_classes(grid_rows):
    return [0, 2, 4, grid_rows - 4, grid_rows - 2]


def _na_bias_table(rpb, grid_rows):
    col = np.arange(GRID_W)
    cstart = np.clip(col - NA_WIN_COLS // 2, 0, GRID_W - NA_WIN_COLS)
    col_in = (col[None, :] >= cstart[:, None]) & (col[None, :] < cstart[:, None] + NA_WIN_COLS)
    cb = np.clip(col[None, :] - col[:, None] + (NA_WIN_COLS - 1), 0, 2 * NA_WIN_COLS - 2)
    classes = _na_classes(grid_rows)
    rbi = np.zeros((len(classes), NA_QROWS, NA_KROWS), np.int64)
    row_in = np.zeros((len(classes), NA_QROWS, NA_KROWS), bool)
    for c, r in enumerate(classes):
        a_row = (min(int(np.clip(r - NA_WIN_ROWS // 2, 0, grid_rows - NA_WIN_ROWS)), grid_rows - NA_KROWS) // 2) * 2
        for qi in range(NA_QROWS):
            start_q = int(np.clip(r + qi - NA_WIN_ROWS // 2, 0, grid_rows - NA_WIN_ROWS))
            for j in range(NA_KROWS):
                row_in[c, qi, j] = start_q <= a_row + j < start_q + NA_WIN_ROWS
                rbi[c, qi, j] = np.clip(a_row + j - (r + qi) + NA_WIN_ROWS - 1, 0, 2 * NA_WIN_ROWS - 2)
    oh_row = jnp.asarray(rbi[..., None] == np.arange(2 * NA_WIN_ROWS - 1), F32)
    oh_col = jnp.asarray(cb[:, :, None] == np.arange(2 * NA_WIN_COLS - 1), F32)
    t = jnp.einsum('hab,cija,qkb->cjkhiq', rpb.astype(F32), oh_row, oh_col, precision=HIGHEST) * LOG2E
    valid = row_in.transpose(0, 2, 1)[:, :, None, None, :, None] & col_in.T[None, None, :, None, None, :]
    t = jnp.where(valid, t, NEG_INF)
    return t.reshape(len(classes), NA_KROWS * GRID_W, NA_HEADS * NA_QROWS * GRID_W)


def _augment_vt(v, nheads):
    nb, s, _ = v.shape
    v = v.reshape(nb, s, nheads, HEAD_DIM)
    pad = jnp.zeros((nb, s, nheads, VROWS - HEAD_DIM), v.dtype).at[..., 0].set(1.0)
    return jnp.concatenate([v, pad], axis=-1).reshape(nb, s, nheads * VROWS).transpose(0, 2, 1)


SWA_BLOCK = 128


def _swa_scores(qt_ref, k_ref, kc_ref, qbd_ref, *, seq):
    n = pl.program_id(1)
    band = 3 * SWA_BLOCK
    nq = SWA_BLOCK
    nqh = 2 * SWA_KV_HEADS
    bstart = pl.multiple_of(jnp.clip((n - 1) * SWA_BLOCK, 0, seq - band), SWA_BLOCK)
    qt = qt_ref[0]
    zero = jnp.zeros((HEAD_DIM, nq), qt.dtype)
    for hq in range(nqh):
        qh = qt[hq * HEAD_DIM:(hq + 1) * HEAD_DIM]
        qbd_ref[:, hq * nq:(hq + 1) * nq] = jnp.concatenate([qh, zero] if hq // 2 == 0 else [zero, qh], axis=0)
    qbd = qbd_ref[...]
    kpos = bstart + lax.broadcasted_iota(jnp.int32, (band, nqh * nq), 0)
    qpos = n * SWA_BLOCK + lax.broadcasted_iota(jnp.int32, (band, nqh * nq), 1) % nq
    s_loc = jnp.where(jnp.abs(qpos - kpos) <= SWA_WINDOW, _dot(k_ref[0, pl.ds(bstart, band), :], qbd), NEG_INF)
    s_ctx = _dot(kc_ref[0], qbd)
    return bstart, s_loc, s_ctx


def _swa_values(bstart, s_loc, s_ctx, vt_ref, vct_ref, sink_ref):
    nq = SWA_BLOCK
    nqh = 2 * SWA_KV_HEADS
    sink = jnp.concatenate([jnp.broadcast_to(sink_ref[0:1, hq:hq + 1] * LOG2E, (1, nq)) for hq in range(nqh)], axis=1)
    m = jnp.maximum(jnp.maximum(s_loc.max(axis=0, keepdims=True), s_ctx.max(axis=0, keepdims=True)), sink)
    p_loc = jnp.exp2((s_loc - m).astype(BF16))
    p_ctx = jnp.exp2((s_ctx - m).astype(BF16))
    p_sink = jnp.exp2(sink - m)
    vw = vt_ref[0, :, pl.ds(bstart, 3 * SWA_BLOCK)]
    outs = []
    for hq in range(nqh):
        rows = slice((hq // 2) * VROWS, (hq // 2 + 1) * VROWS)
        cols = slice(hq * nq, (hq + 1) * nq)
        o = _dot(vw[rows], p_loc[:, cols]) + _dot(vct_ref[0, rows, :], p_ctx[:, cols])
        outs.append(o[0:HEAD_DIM] / (o[HEAD_DIM:HEAD_DIM + 1] + p_sink[:, cols]))
    return jnp.concatenate(outs, axis=0).T.astype(BF16)


def _local_attn_kernel(na_qt, na_k, na_vt, na_kc, na_vct, bias_ref, sw_qt, sw_k, sw_vt, sw_kc, sw_vct, sink_ref,
                       na_o, sw_o, na_qbd, sw_qbd, *, grid_rows, seq):
    na_s = _na_scores(na_qt, na_k, na_kc, bias_ref, na_qbd, grid_rows=grid_rows)
    sw_s = _swa_scores(sw_qt, sw_k, sw_kc, sw_qbd, seq=seq)
    na_o[...] = _na_values(*na_s, na_vt, na_vct)
    sw_o[...] = _swa_values(*sw_s, sw_vt, sw_vct, sink_ref)


def _local_attention(na_qt, na_vt, na, na_c, bias, sw_qt, sw_vt, sw, sw_c, sink, nbatch, seq, n_ctx):
    grid_rows = seq // GRID_W
    nq = NA_QROWS * GRID_W
    nstep = seq // nq
    nqh = 2 * SWA_KV_HEADS
    assert grid_rows >= NA_KROWS and grid_rows % NA_QROWS == 0 and nq == SWA_BLOCK and seq >= 3 * SWA_BLOCK
    nal = na.reshape(nbatch, seq, 768)
    nac = na_c.reshape(nbatch, n_ctx, 768)
    na_vct = _augment_vt(nac[:, :, 512:768], NA_HEADS)
    sink_pad = jnp.zeros((1, 128), F32).at[0, 0:nqh].set(sink.astype(F32))
    swl = sw.reshape(nbatch, seq, 512)
    swc = sw_c.reshape(nbatch, n_ctx, 512)
    sw_vct = _augment_vt(swc[:, :, 384:512], SWA_KV_HEADS)

    def cls(b, p):
        r = NA_QROWS * p
        c = jnp.where(r < 4, r // 2, jnp.where(r >= grid_rows - 4, (r - (grid_rows - 4)) // 2 + 3, 2))
        return (c, 0, 0)

    out = jax.ShapeDtypeStruct((nbatch * seq, 256), BF16)
    out_spec = pl.BlockSpec((nq, 256), lambda b, p: (b * nstep + p, 0))
    return pl.pallas_call(
        functools.partial(_local_attn_kernel, grid_rows=grid_rows, seq=seq),
        grid=(nbatch, nstep),
        in_specs=[pl.BlockSpec((1, 256, nq), lambda b, p: (b, 0, p)),
                  pl.BlockSpec((1, seq, 256), lambda b, p: (b, 0, 1)),
                  pl.BlockSpec((1, NA_HEADS * VROWS, seq), lambda b, p: (b, 0, 0)),
                  pl.BlockSpec((1, n_ctx, 256), lambda b, p: (b, 0, 1)),
                  pl.BlockSpec((1, NA_HEADS * VROWS, n_ctx), lambda b, p: (b, 0, 0)),
                  pl.BlockSpec((1, NA_KROWS * GRID_W, NA_HEADS * nq), cls),
                  pl.BlockSpec((1, 256, nq), lambda b, p: (b, 0, p)),
                  pl.BlockSpec((1, seq, 128), lambda b, p: (b, 0, 2)),
                  pl.BlockSpec((1, SWA_KV_HEADS * VROWS, seq), lambda b, p: (b, 0, 0)),
                  pl.BlockSpec((1, n_ctx, 128), lambda b, p: (b, 0, 2)),
                  pl.BlockSpec((1, SWA_KV_HEADS * VROWS, n_ctx), lambda b, p: (b, 0, 0)),
                  pl.BlockSpec((1, 128), lambda b, p: (0, 0))],
        out_specs=[out_spec, out_spec],
        out_shape=[out, out],
        scratch_shapes=[pltpu.VMEM((256, NA_HEADS * nq), BF16),
                        pltpu.VMEM((SWA_KV_HEADS * HEAD_DIM, nqh * nq), BF16)],
        compiler_params=_params("arbitrary", "arbitrary"),
        name="local_attention",
    )(na_qt, nal, na_vt, nac, na_vct, bias, sw_qt, swl, sw_vt, swc, sw_vct, sink_pad)


DIFF_TQ = 512
DIFF_CK = 256
LOG2E = math.log2(math.e)


def _diff_lambda(lq1_ref, lk1_ref, lq2_ref, lk2_ref, lambda_init):
    s1 = jnp.sum(lq1_ref[...] * lk1_ref[...], axis=-1, keepdims=True)
    s2 = jnp.sum(lq2_ref[...] * lk2_ref[...], axis=-1, keepdims=True)
    return jnp.exp(s1) - jnp.exp(s2) + lambda_init


def _stack_maps(qh):
    lane = lax.broadcasted_iota(jnp.int32, qh.shape, 1)
    zero = jnp.zeros_like(qh)
    return jnp.concatenate([jnp.where(lane < DIFF_QK_DIM, qh, zero),
                            jnp.where(lane >= DIFF_QK_DIM, qh, zero)], axis=0)


def _subln(o0, o1, lam, g, lambda_init):
    o = o0 - lam * o1
    return _rms(o) * g * (1.0 - lambda_init)


def _diff_kernel(qt_ref, k_ref, vt_ref, kc_ref, vct_ref, lq1_ref, lk1_ref, lq2_ref, lk2_ref, g_ref, o_ref,
                 qbd_ref, acc_ref, s_ref, *, nlat, lambda_init):
    lam = _diff_lambda(lq1_ref, lk1_ref, lq2_ref, lk2_ref, lambda_init)
    qt = qt_ref[0]
    tq = qt.shape[1]
    w = 2 * tq
    row = lax.broadcasted_iota(jnp.int32, qt.shape, 0) // DIFF_QK_DIM
    zero = jnp.zeros_like(qt)
    for j in range(2 * DIFF_HEADS):
        qbd_ref[:, j * tq:(j + 1) * tq] = jnp.where(row == j, qt, zero)
    acc_ref[...] = jnp.zeros_like(acc_ref)

    def is_ctx(c):
        return isinstance(c, int) and c == nlat

    def scores(slot, c, h):
        kblk = kc_ref[0] if is_ctx(c) else k_ref[0, c]
        s = _dot(kblk, qbd_ref[:, h * w:(h + 1) * w])
        s_ref[slot, h] = s
        return s.max(axis=0, keepdims=True)

    def softmax_pv(slot, c, h, m_run, m_chunk):
        m_new = jnp.maximum(m_run, m_chunk)
        alpha = jnp.exp2(m_run - m_new)
        p = jnp.exp2((s_ref[slot, h] - m_new).astype(BF16))
        rows = slice(h * VROWS, (h + 1) * VROWS)
        vblk = vct_ref[0, rows, :] if is_ctx(c) else vt_ref[0, c, rows, :]
        acc_ref[h] = alpha * acc_ref[h] + _dot(vblk, p)
        return m_new

    def step(slot, c, carry, last=False):
        m_run, m_chunk = carry
        new_run, new_chunk = [], []
        for h in range(DIFF_HEADS):
            if not last:
                new_chunk.append(scores(1 - slot, c + 1, h))
            new_run.append(softmax_pv(slot, c, h, m_run[h], m_chunk[h]))
        return tuple(new_run), tuple(new_chunk)

    def body(i, carry):
        c = 2 * i
        return step(1, c + 1, step(0, c, carry))

    carry = (tuple(jnp.full((1, w), NEG_INF, F32) for _ in range(DIFF_HEADS)),
             tuple(scores(0, 0, h) for h in range(DIFF_HEADS)))
    nloop = (nlat - 1) // 2
    carry = lax.fori_loop(0, nloop, body, carry)
    for c in range(2 * nloop, nlat + 1):
        carry = step(c % 2, c, carry, last=c == nlat)
    outs = []
    for h in range(DIFF_HEADS):
        o = acc_ref[h, 0:HEAD_DIM, :] / acc_ref[h, HEAD_DIM:HEAD_DIM + 1, :]
        d = o[:, 0:tq] - lam * o[:, tq:w]
        d = d * lax.rsqrt(jnp.mean(d * d, axis=0, keepdims=True) + EPS)
        outs.append(d * g_ref[...] * (1.0 - lambda_init))
    o_ref[...] = jnp.concatenate(outs, axis=0).T.astype(BF16)


def _diff_attention(qt, k4, vt4, df_c, lqk, subln_g, lambda_init, nbatch, seq, n_ctx):
    tq, ck = min(DIFF_TQ, seq), DIFF_CK
    assert seq % ck == 0 and seq % tq == 0 and n_ctx == ck
    nlat = seq // ck
    nq = seq // tq
    dfc = df_c.reshape(nbatch, n_ctx, 768)
    vct = _augment_vt(dfc[:, :, 512:768], DIFF_HEADS)
    vec = pl.BlockSpec((1, DIFF_QK_DIM), lambda b, n: (0, 0))
    return pl.pallas_call(
        functools.partial(_diff_kernel, nlat=nlat, lambda_init=lambda_init),
        grid=(nbatch, nq),
        in_specs=[pl.BlockSpec((1, 256, tq), lambda b, n: (b, 0, n)),
                  pl.BlockSpec((1, nlat, ck, 256), lambda b, n: (b, 0, 0, 0)),
                  pl.BlockSpec((1, nlat, DIFF_HEADS * VROWS, ck), lambda b, n: (b, 0, 0, 0)),
                  pl.BlockSpec((1, n_ctx, 256), lambda b, n: (b, 0, 1)),
                  pl.BlockSpec((1, DIFF_HEADS * VROWS, n_ctx), lambda b, n: (b, 0, 0)),
                  vec, vec, vec, vec,
                  pl.BlockSpec((HEAD_DIM, 1), lambda b, n: (0, 0))],
        out_specs=pl.BlockSpec((tq, 256), lambda b, n: (b * nq + n, 0)),
        out_shape=jax.ShapeDtypeStruct((nbatch * seq, 256), BF16),
        scratch_shapes=[pltpu.VMEM((256, 2 * DIFF_HEADS * tq), BF16),
                        pltpu.VMEM((DIFF_HEADS, VROWS, 2 * tq), F32),
                        pltpu.VMEM((2, DIFF_HEADS, ck, 2 * tq), F32)],
        compiler_params=_params("arbitrary", "arbitrary"),
        name="diff_attention",
    )(qt, k4, vt4, dfc, vct, *lqk, subln_g.reshape(HEAD_DIM, 1).astype(F32))


def _ctx_attn_kernel(na_ref, sw_ref, df_ref, sink_ref, lq1_ref, lk1_ref, lq2_ref, lk2_ref, g_ref,
                     nb_o, sw_o, df_o, *, lambda_init):
    n = na_ref.shape[0]
    na = na_ref[...]
    outs = []
    for h in range(NA_HEADS):
        sl = slice(h * HEAD_DIM, (h + 1) * HEAD_DIM)
        (p,), l = _softmax_parts([_nt_dot(na[:, sl], na[:, 256 + h * HEAD_DIM:256 + (h + 1) * HEAD_DIM])])
        outs.append(_dot(p.astype(BF16), na[:, 512 + h * HEAD_DIM:512 + (h + 1) * HEAD_DIM]) / l)
    nb_o[...] = jnp.concatenate(outs, axis=1).astype(BF16)
    sw = sw_ref[...]
    outs = []
    for hq in range(4):
        kv = hq // 2
        k = sw[:, 256 + kv * HEAD_DIM:256 + (kv + 1) * HEAD_DIM]
        v = sw[:, 384 + kv * HEAD_DIM:384 + (kv + 1) * HEAD_DIM]
        sk = jnp.broadcast_to(sink_ref[0:1, hq:hq + 1], (n, 1))
        (p,), l = _softmax_parts([_nt_dot(sw[:, hq * HEAD_DIM:(hq + 1) * HEAD_DIM], k)], extra=sk)
        outs.append(_dot(p.astype(BF16), v) / l)
    sw_o[...] = jnp.concatenate(outs, axis=1).astype(BF16)
    lam = _diff_lambda(lq1_ref, lk1_ref, lq2_ref, lk2_ref, lambda_init)
    df = df_ref[...]
    outs = []
    for h in range(DIFF_HEADS):
        sl = slice(h * HEAD_DIM, (h + 1) * HEAD_DIM)
        q2 = _stack_maps(df[:, sl])
        (p,), l = _softmax_parts([_nt_dot(q2, df[:, 256 + h * HEAD_DIM:256 + (h + 1) * HEAD_DIM])])
        o = _dot(p.astype(BF16), df[:, 512 + h * HEAD_DIM:512 + (h + 1) * HEAD_DIM]) / l
        outs.append(_subln(o[0:n], o[n:2 * n], lam, g_ref[...], lambda_init))
    df_o[...] = jnp.concatenate(outs, axis=1).astype(BF16)


def _ctx_attention(na_c, sw_c, df_c, sink, lqk, subln_g, lambda_init, nbatch, n_ctx):
    sink_pad = jnp.zeros((1, 128), F32).at[0, 0:4].set(sink.astype(F32))
    vec = pl.BlockSpec((1, DIFF_QK_DIM), lambda b: (0, 0))
    out = jax.ShapeDtypeStruct((nbatch * n_ctx, 256), BF16)
    return pl.pallas_call(
        functools.partial(_ctx_attn_kernel, lambda_init=lambda_init),
        grid=(nbatch,),
        in_specs=[pl.BlockSpec((n_ctx, 768), lambda b: (b, 0)),
                  pl.BlockSpec((n_ctx, 512), lambda b: (b, 0)),
                  pl.BlockSpec((n_ctx, 768), lambda b: (b, 0)),
                  pl.BlockSpec((1, 128), lambda b: (0, 0)),
                  vec, vec, vec, vec,
                  pl.BlockSpec((1, HEAD_DIM), lambda b: (0, 0))],
        out_specs=[pl.BlockSpec((n_ctx, 256), lambda b: (b, 0))] * 3,
        out_shape=[out, out, out],
        compiler_params=_params("arbitrary"),
        name="ctx_attention",
    )(na_c, sw_c, df_c, sink_pad, *lqk, subln_g.reshape(1, HEAD_DIM).astype(F32))


def _merge_kernel(u_ref, yf_ref, yr_ref, d_ref, gw_ref, gb_ref, yb_ref, yc_ref, yd_ref, gate_ref,
                  wb_ref, wo_ref, x_ref, g1_ref, n2_ref, sc2_ref, sh2_ref, rw_ref,
                  x1_o, h2_o, h2p_o, lg_o):
    y = u_ref[...] * d_ref[...] + yf_ref[0, 0] + yr_ref[0, 0]
    a = jax.nn.gelu(y, approximate=True)
    ya = a * jax.nn.sigmoid(_dot(a.astype(BF16), gw_ref[...]) + gb_ref[...])
    branches = (ya.astype(BF16), yb_ref[...], yc_ref[...], yd_ref[...])
    acc = None
    for i in range(4):
        t = gate_ref[:, i * D_MODEL:(i + 1) * D_MODEL].astype(F32) * _dot(branches[i], wb_ref[i])
        acc = t if acc is None else acc + t
    mixed = _dot(acc.astype(BF16), wo_ref[...])
    x1 = x_ref[...] + g1_ref[0] * mixed
    x1_o[...] = x1
    h2 = _rms(x1) * n2_ref[...]
    h2 = h2 * (1.0 + sc2_ref[0]) + sh2_ref[0]
    h2_o[...] = h2.astype(BF16)
    h2p_o[...] = _pack_rows(h2[:, 0:HALF_D], h2[:, HALF_D:D_MODEL])
    lg_o[...] = lax.dot_general(rw_ref[...], h2, (((1,), (1,)), ((), ())), preferred_element_type=F32,
                                precision=HIGHEST)


def _merge(u, y_bm, y_start, rows_per_seq, s5_d, glu_w, glu_b, yb, yc, yd, gate, wb, wo, x2d, g1, norm2_g, sc2, sh2,
           router_w, *, rows_per_mod):
    rows = x2d.shape[0]
    tm = min(512, rows_per_seq)
    assert rows % tm == 0 and rows_per_mod % tm == 0 and rows_per_seq % tm == 0 and y_start % tm == 0
    tiles_per_seq = rows_per_seq // tm

    def mod_map(i):
        return ((i * tm) // rows_per_mod, 0, 0)

    def y_spec(d):
        return pl.BlockSpec((1, 1, tm, 256), lambda i: (d, i // tiles_per_seq, y_start // tm + i % tiles_per_seq, 0))

    row = lambda w: pl.BlockSpec((tm, w), lambda i: (i, 0))
    full = lambda *shape: pl.BlockSpec(shape, lambda i: (0,) * len(shape))
    mod = pl.BlockSpec((1, 1, D_MODEL), mod_map)
    return pl.pallas_call(
        _merge_kernel,
        grid=(rows // tm,),
        in_specs=[row(256), y_spec(0), y_spec(1), full(1, 256), full(256, 256), full(1, 256),
                  row(256), row(256), row(256), row(GATE_WIDTH),
                  full(4, 256, D_MODEL), full(D_MODEL, D_MODEL), row(D_MODEL),
                  mod, full(1, D_MODEL), mod, mod, full(N_EXPERTS, D_MODEL)],
        out_specs=[row(D_MODEL), row(D_MODEL), row(HALF_D), pl.BlockSpec((N_EXPERTS, tm), lambda i: (0, i))],
        out_shape=[jax.ShapeDtypeStruct((rows, D_MODEL), F32),
                   jax.ShapeDtypeStruct((rows, D_MODEL), BF16),
                   jax.ShapeDtypeStruct((rows, HALF_D), jnp.int32),
                   jax.ShapeDtypeStruct((N_EXPERTS, rows), F32)],
        compiler_params=_params("arbitrary"),
        name="merge",
    )(u, y_bm, y_bm, s5_d.reshape(1, 256).astype(F32), glu_w.astype(BF16), glu_b.reshape(1, 256).astype(F32),
      yb, yc, yd, gate, wb.astype(BF16), wo.astype(BF16), x2d, g1, norm2_g.reshape(1, D_MODEL), sc2, sh2,
      router_w.astype(F32).T)


def _router_kernel(lg_ref, b_ref, tri_ref, idx_ref, rank_ref, w_ref, cnt_ref, base_ref):
    tr = lg_ref.shape[1]
    gsz = N_EXPERTS // N_EXPERT_GROUPS
    sc = jax.nn.sigmoid(lg_ref[...])
    bi = sc + b_ref[...]
    e_iota = lax.broadcasted_iota(jnp.int32, (gsz, tr), 0).astype(F32)
    groups = [bi[g * gsz:(g + 1) * gsz] for g in range(N_EXPERT_GROUPS)]
    gs = []
    for bg in groups:
        m1 = bg.max(axis=0, keepdims=True)
        i1 = jnp.where(bg == m1, e_iota, float(gsz)).min(axis=0, keepdims=True)
        m2 = jnp.where(e_iota == i1, -jnp.inf, bg).max(axis=0, keepdims=True)
        gs.append(m1 + m2)
    v = []
    for g in range(N_EXPERT_GROUPS):
        rank = jnp.zeros((1, tr), F32)
        for g2 in range(N_EXPERT_GROUPS):
            if g2 == g:
                continue
            beats = (gs[g2] >= gs[g]) if g2 < g else (gs[g2] > gs[g])
            rank = rank + jnp.where(beats, 1.0, 0.0)
        v.append(jnp.where(rank < TOPK_GROUPS, groups[g], NEG_INF))
    flat = [e_iota + float(g * gsz) for g in range(N_EXPERT_GROUPS)]
    sel = [jnp.zeros((gsz, tr), F32) for _ in range(N_EXPERT_GROUPS)]
    picks = []
    for _ in range(TOP_K):
        m = v[0].max(axis=0, keepdims=True)
        for g in range(1, N_EXPERT_GROUPS):
            m = jnp.maximum(m, v[g].max(axis=0, keepdims=True))
        am = jnp.where(v[0] == m, flat[0], float(N_EXPERTS)).min(axis=0, keepdims=True)
        for g in range(1, N_EXPERT_GROUPS):
            am = jnp.minimum(am, jnp.where(v[g] == m, flat[g], float(N_EXPERTS)).min(axis=0, keepdims=True))
        hits = []
        for g in range(N_EXPERT_GROUPS):
            hit = flat[g] == am
            hits.append(hit)
            sel[g] = jnp.where(hit, 1.0, sel[g])
            v[g] = jnp.where(hit, -jnp.inf, v[g])
        picks.append((am, hits))
    scg = [sc[g * gsz:(g + 1) * gsz] for g in range(N_EXPERT_GROUPS)]
    den = (sel[0] * scg[0]).sum(axis=0, keepdims=True)
    for g in range(1, N_EXPERT_GROUPS):
        den = den + (sel[g] * scg[g]).sum(axis=0, keepdims=True)

    @pl.when(pl.program_id(0) == 0)
    def _():
        base_ref[...] = jnp.zeros_like(base_ref)

    sel_all = jnp.concatenate(sel, axis=0)
    before = _dot(sel_all.astype(jnp.bfloat16), tri_ref[...]) + base_ref[...]
    for k, (am, hits) in enumerate(picks):
        wk = jnp.zeros((1, tr), F32)
        rk = jnp.zeros((1, tr), F32)
        for g in range(N_EXPERT_GROUPS):
            wk = wk + jnp.where(hits[g], scg[g], 0.0).sum(axis=0, keepdims=True)
            rk = rk + jnp.where(hits[g], before[g * gsz:(g + 1) * gsz], 0.0).sum(axis=0, keepdims=True)
        idx_ref[k:k + 1, :] = am.astype(jnp.int32)
        rank_ref[k:k + 1, :] = rk.astype(jnp.int32)
        w_ref[k:k + 1, :] = wk / den * ROUTED_SCALE
    idx_ref[TOP_K:8, :] = jnp.zeros((8 - TOP_K, tr), jnp.int32)
    rank_ref[TOP_K:8, :] = jnp.zeros((8 - TOP_K, tr), jnp.int32)
    w_ref[TOP_K:8, :] = jnp.zeros((8 - TOP_K, tr), F32)
    base_ref[...] += sel_all.sum(axis=1, keepdims=True)
    cnt_ref[...] = base_ref[...].astype(jnp.int32)


ROUTER_TILE = 512


def _router(logits_t, router_b):
    ne, rows = logits_t.shape
    tr = ROUTER_TILE
    assert rows % tr == 0
    tri = jnp.asarray(np.triu(np.ones((tr, tr), np.float32), k=1), jnp.bfloat16)
    pick = pl.BlockSpec((8, tr), lambda i: (0, i))
    return pl.pallas_call(
        _router_kernel,
        grid=(rows // tr,),
        in_specs=[pl.BlockSpec((ne, tr), lambda i: (0, i)),
                  pl.BlockSpec((ne, 1), lambda i: (0, 0)),
                  pl.BlockSpec((tr, tr), lambda i: (0, 0))],
        out_specs=[pick, pick, pick, pl.BlockSpec((ne, 1), lambda i: (0, 0))],
        out_shape=[jax.ShapeDtypeStruct((8, rows), jnp.int32),
                   jax.ShapeDtypeStruct((8, rows), jnp.int32),
                   jax.ShapeDtypeStruct((8, rows), F32),
                   jax.ShapeDtypeStruct((ne, 1), jnp.int32)],
        scratch_shapes=[pltpu.VMEM((ne, 1), F32)],
        compiler_params=_params("arbitrary"),
        name="router",
    )(logits_t, router_b.reshape(ne, 1).astype(F32), tri)


MOE_BLOCK = 512
MOE_TOKENS = 256
MOE_COMBINE_GROUPS = 4
HALF_D = D_MODEL // 2


def _pack_rows(lo, hi):
    lo_b = pltpu.bitcast(lo.astype(jnp.bfloat16).astype(F32), jnp.uint32)
    hi_b = pltpu.bitcast(hi.astype(jnp.bfloat16).astype(F32), jnp.uint32)
    return pltpu.bitcast((hi_b & jnp.uint32(0xFFFF0000)) | (lo_b >> 16), jnp.int32)


def _unpack_rows(words):
    u = pltpu.bitcast(words, jnp.uint32)
    lo = pltpu.bitcast(u << 16, F32)
    hi = pltpu.bitcast(u & jnp.uint32(0xFFFF0000), F32)
    return lo, hi


def _swiglu(x_bf16, wgu, wd):
    hgu = _dot(x_bf16, wgu)
    g = hgu[:, 0:EXPERT_HIDDEN]
    a = g * jax.nn.sigmoid(g) * hgu[:, EXPERT_HIDDEN:2 * EXPERT_HIDDEN]
    return _dot(a.astype(BF16), wd)


SC_CORES = 2
SC_SUBCORES = 16
SC_STREAM_ROWS = 128


def _sc_for_each_chunk(total, body):
    chunk = SC_STREAM_ROWS
    assert total % chunk == 0
    nchunk = total // chunk
    per_worker = pl.cdiv(nchunk, SC_CORES * SC_SUBCORES)
    first = (lax.axis_index("s") * SC_CORES + lax.axis_index("c")) * per_worker

    @pl.loop(0, per_worker)
    def _(j):
        @pl.when(first + j < nchunk)
        def _():
            body((first + j) * chunk)


def _sc_scatter_rows(rows, slots, n_out):
    total, n = rows.shape
    chunk = SC_STREAM_ROWS
    mesh = plsc.VectorSubcoreMesh(core_axis_name="c", subcore_axis_name="s")

    @functools.partial(
        pl.kernel, mesh=mesh,
        out_type=jax.ShapeDtypeStruct((n_out, n), jnp.int32),
        scratch_types=[pltpu.VMEM((8, chunk), jnp.int32),
                       pltpu.VMEM((chunk, n), jnp.int32),
                       pltpu.SemaphoreType.DMA],
        name="moe_dispatch_sc",
    )
    def scatter(rows_hbm, slot_hbm, out_hbm, idx_v, rows_v, sem):
        def body(off):
            pltpu.sync_copy(rows_hbm.at[pl.ds(off, chunk)], rows_v)
            pltpu.sync_copy(slot_hbm.at[:, pl.ds(off, chunk)], idx_v)
            for k in range(TOP_K):
                pltpu.async_copy(rows_v, out_hbm.at[idx_v.at[k]], sem).wait()

        _sc_for_each_chunk(total, body)

    return scatter(rows, slots)


def _experts_kernel(be_ref, nv_ref, nb_ref, xs_ref, wg_ref, wu_ref, wd_ref, ys_ref, wgu_bf, wd_bf):
    b = pl.program_id(0)

    @pl.when(b < nb_ref[0])
    def _():
        @pl.when((b == 0) | (be_ref[b] != be_ref[jnp.maximum(b - 1, 0)]))
        def _():
            wgu_bf[:, 0:EXPERT_HIDDEN] = wg_ref[0, 0].astype(BF16)
            wgu_bf[:, EXPERT_HIDDEN:2 * EXPERT_HIDDEN] = wu_ref[0, 0].astype(BF16)
            wd_bf[...] = wd_ref[0, 0].astype(BF16)

        words = xs_ref[...]
        live = lax.broadcasted_iota(jnp.int32, words.shape, 0) < nv_ref[b]
        lo, hi = _unpack_rows(jnp.where(live, words, 0))
        x = jnp.concatenate([lo, hi], axis=1).astype(BF16)
        y = _swiglu(x, wgu_bf[...], wd_bf[...])
        ys_ref[...] = _pack_rows(y[:, 0:HALF_D], y[:, HALF_D:D_MODEL])


def _sc_gather_rows(table, indices):
    m, n = indices.shape[0], table.shape[1]
    chunk = SC_STREAM_ROWS
    mesh = plsc.VectorSubcoreMesh(core_axis_name="c", subcore_axis_name="s")

    @functools.partial(
        pl.kernel, mesh=mesh,
        out_type=jax.ShapeDtypeStruct((m, n), jnp.int32),
        scratch_types=[pltpu.VMEM((chunk,), jnp.int32),
                       pltpu.VMEM((chunk, n), jnp.int32),
                       pltpu.SemaphoreType.DMA],
        name="moe_gather_sc",
    )
    def gather(table_hbm, idx_hbm, out_hbm, idx_v, rows_v, sem):
        def body(off):
            pltpu.sync_copy(idx_hbm.at[pl.ds(off, chunk)], idx_v)
            pltpu.async_copy(table_hbm.at[idx_v], rows_v, sem).wait()
            pltpu.sync_copy(rows_v, out_hbm.at[pl.ds(off, chunk)])

        _sc_for_each_chunk(m, body)

    return gather(table, indices)


def _combine_into_kernel(w_ref, rows_ref, h_ref, wsgu_ref, wsd_ref, x1_ref, g2_ref, fg_ref, prev_ref, o_ref, *, final):
    del prev_ref
    _combine_kernel(w_ref, rows_ref, h_ref, wsgu_ref, wsd_ref, x1_ref, g2_ref, fg_ref, o_ref, final=final)


def _combine_kernel(w_ref, rows_ref, h_ref, wsgu_ref, wsd_ref, x1_ref, g2_ref, fg_ref, o_ref, *, final):
    shared = _swiglu(h_ref[...], wsgu_ref[...], wsd_ref[...])
    acc_lo = shared[:, 0:HALF_D]
    acc_hi = shared[:, HALF_D:D_MODEL]
    w = w_ref[...]
    for k in range(TOP_K):
        lo, hi = _unpack_rows(rows_ref[k])
        acc_lo = acc_lo + w[:, k:k + 1] * lo
        acc_hi = acc_hi + w[:, k:k + 1] * hi
    x2 = x1_ref[...] + g2_ref[0] * jnp.concatenate([acc_lo, acc_hi], axis=1)
    if final:
        x2 = _rms(x2) * fg_ref[...]
    o_ref[...] = x2


def _moe(h2, h2p, picks, lp, x1, g2, final_g, *, rows_per_mod, final):
    idx, rank, wsel, counts = picks
    rows = h2.shape[0]
    tt = MOE_TOKENS
    blk = MOE_BLOCK
    assert rows % tt == 0 and rows_per_mod % tt == 0 and (rows * TOP_K) % blk == 0
    ntile = rows // tt
    nblock = rows * TOP_K // blk + N_EXPERTS

    cnt = counts.reshape(N_EXPERTS)
    padded = (cnt + blk - 1) // blk * blk
    e_ids = jnp.arange(N_EXPERTS, dtype=jnp.int32)
    pends = jnp.sum(jnp.where(e_ids[None, :] <= e_ids[:, None], padded[None, :], 0), axis=1)
    pstart = (pends - padded).astype(jnp.int32)
    nb_used = (jnp.sum(padded) // blk).astype(jnp.int32).reshape(1)
    first_row = jnp.arange(nblock, dtype=jnp.int32) * blk
    block_e = jnp.minimum(jnp.sum(pends[None, :] <= first_row[:, None], axis=1), N_EXPERTS - 1).astype(jnp.int32)
    slot = rank + jnp.sum(jnp.where(idx[:, :, None] == jnp.arange(N_EXPERTS, dtype=jnp.int32), pstart, 0), axis=-1)
    seg_end = jnp.sum(jnp.where(block_e[:, None] == e_ids[None, :], (cnt + pstart)[None, :], 0), axis=1)
    n_valid = jnp.clip(seg_end - first_row, 0, blk).astype(jnp.int32)

    xs = _sc_scatter_rows(h2p, slot, nblock * blk)

    def blk_map(b, be, nv, nb):
        return (jnp.minimum(b, nb[0] - 1), 0)

    layer = lp['layer']

    def w_map(b, be, nv, nb):
        return (layer, be[jnp.minimum(b, nb[0] - 1)], 0, 0)

    ys = pl.pallas_call(
        _experts_kernel,
        grid_spec=pltpu.PrefetchScalarGridSpec(
            num_scalar_prefetch=3,
            grid=(nblock,),
            in_specs=[pl.BlockSpec((blk, HALF_D), blk_map),
                      pl.BlockSpec((1, 1, D_MODEL, EXPERT_HIDDEN), w_map),
                      pl.BlockSpec((1, 1, D_MODEL, EXPERT_HIDDEN), w_map),
                      pl.BlockSpec((1, 1, EXPERT_HIDDEN, D_MODEL), w_map)],
            out_specs=pl.BlockSpec((blk, HALF_D), blk_map),
            scratch_shapes=[pltpu.VMEM((D_MODEL, 2 * EXPERT_HIDDEN), BF16),
                            pltpu.VMEM((EXPERT_HIDDEN, D_MODEL), BF16)]),
        out_shape=jax.ShapeDtypeStruct((nblock * blk, HALF_D), jnp.int32),
        compiler_params=_params("arbitrary"),
        name="moe_experts",
    )(block_e, n_valid, nb_used, xs, lp['exp_w_gate'], lp['exp_w_up'], lp['exp_w_down'])

    wsgu = jnp.concatenate([lp['sh_w_gate'], lp['sh_w_up']], axis=1).astype(BF16)
    wsd = lp['sh_w_down'].astype(BF16)
    ngroup = MOE_COMBINE_GROUPS if ntile % MOE_COMBINE_GROUPS == 0 and ntile >= 4 * MOE_COMBINE_GROUPS else 1
    gtile = ntile // ngroup
    grows = gtile * tt
    full = lambda *shape: pl.BlockSpec(shape, lambda i: (0,) * len(shape))
    wsel_t = wsel.T
    fg = final_g.reshape(1, D_MODEL).astype(F32)
    out = None
    for p in range(ngroup):
        gathered = _sc_gather_rows(ys, slot[0:TOP_K, p * grows:(p + 1) * grows].reshape(TOP_K * grows))
        gathered = gathered.reshape(TOP_K, grows, HALF_D)
        row = lambda width, p=p: pl.BlockSpec((tt, width), lambda i: (p * gtile + i, 0))
        in_specs = [row(8), pl.BlockSpec((TOP_K, tt, HALF_D), lambda i: (0, i, 0)), row(D_MODEL),
                    full(D_MODEL, 2 * EXPERT_HIDDEN), full(EXPERT_HIDDEN, D_MODEL), row(D_MODEL),
                    pl.BlockSpec((1, 1, D_MODEL), lambda i, p=p: (((p * gtile + i) * tt) // rows_per_mod, 0, 0)),
                    full(1, D_MODEL)]
        args = [wsel_t, gathered, h2, wsgu, wsd, x1, g2, fg]
        kern = functools.partial(_combine_kernel, final=final)
        aliases = {}
        if out is not None:
            in_specs.append(pl.BlockSpec(memory_space=pl.ANY))
            args.append(out)
            aliases = {len(args) - 1: 0}
            kern = functools.partial(_combine_into_kernel, final=final)
        out = pl.pallas_call(
            kern,
            grid=(gtile,),
            in_specs=in_specs,
            out_specs=row(D_MODEL),
            out_shape=jax.ShapeDtypeStruct((rows, D_MODEL), F32),
            input_output_aliases=aliases,
            compiler_params=_params("arbitrary"),
            name="moe_combine",
        )(*args)
    return out


def _reorder_w_in(w_in):
    split = 256 + 768 + 512 + 768
    return jnp.concatenate([w_in[:, split:], w_in[:, :split]], axis=1).astype(BF16)


def _mods(mod_row_block):
    return [mod_row_block[:, None, k * D_MODEL:(k + 1) * D_MODEL] for k in range(6)]


def _moe_block(h2, h2p, logits, lp, x1, g2, final_g, *, rows_per_mod, final):
    picks = _router(logits, lp['router_b'])
    return _moe(h2, h2p, picks, lp, x1, g2, final_g, rows_per_mod=rows_per_mod, final=final)


def _layer(x2d, xc2d, c16, lp, layer_idx, tables, final_g, *, nbatch, seq, n_ctx, with_ctx_out, final):
    lambda_init = 0.8 - 0.6 * math.exp(-0.3 * layer_idx)
    mod = _ada_mod(c16, lp['ada_w'].astype(F32), lp['ada_b'].astype(F32))
    sh1, sc1, g1, sh2, sc2, g2 = _mods(mod[0:nbatch])
    csh1, csc1, cg1, csh2, csc2, cg2 = _mods(mod[nbatch:nbatch + 1])
    w_in = _reorder_w_in(lp['w_in'])
    rows_lat = nbatch * seq
    rows_ctx = nbatch * n_ctx

    (gate, u, na, sw, df, na_qt, sw_qt, df_qt, na_vt, sw_vt, df_k4, df_vt4) = _inproj(
        x2d, lp['norm1_g'], sc1, sh1, w_in, tables, rows_per_mod=seq, rope=True, seq=seq)
    gate_c, u_c, na_c, sw_c, df_c = _inproj(xc2d, lp['norm1_g'], csc1, csh1, w_in, tables,
                                            rows_per_mod=rows_ctx, rope=False, seq=seq)

    win, wout, a_re, a_im = _s5_params(lp['s5_lambda_re'], lp['s5_lambda_im'], lp['s5_log_step'],
                                       lp['s5_b_re'], lp['s5_b_im'], lp['s5_c_re'], lp['s5_c_im'])
    u_tm = jnp.concatenate([u_c.reshape(nbatch, n_ctx, 256).transpose(1, 0, 2),
                            u.reshape(nbatch, seq, 256).transpose(1, 0, 2)], axis=0)
    y_tm = _s5_scan(u_tm, win, wout, a_re, a_im, n_ctx)
    y_bm = y_tm.transpose(0, 2, 1, 3)

    lqk = [lp[k].reshape(1, DIFF_QK_DIM).astype(F32) for k in ('diff_lq1', 'diff_lk1', 'diff_lq2', 'diff_lk2')]
    bias = _na_bias_table(lp['na_rpb'], seq // GRID_W)
    yb, yc = _local_attention(na_qt, na_vt, na, na_c, bias, sw_qt, sw_vt, sw, sw_c, lp['swa_sink'], nbatch, seq, n_ctx)
    yd = _diff_attention(df_qt, df_k4, df_vt4, df_c, lqk, lp['diff_subln_g'], lambda_init, nbatch, seq, n_ctx)

    merge_w = (lp['s5_d'], lp['s5_glu_w'], lp['s5_glu_b'])
    x1, h2, h2p, logits = _merge(u, y_bm, 0, seq, *merge_w, yb, yc, yd, gate, lp['w_branch'], lp['w_out'], x2d, g1,
                                 lp['norm2_g'], sc2, sh2, lp['router_w'], rows_per_mod=seq)
    x_out = _moe_block(h2, h2p, logits, lp, x1, g2, final_g, rows_per_mod=seq, final=final)

    xc_out = None
    if with_ctx_out:
        yb_c, yc_c, yd_c = _ctx_attention(na_c, sw_c, df_c, lp['swa_sink'], lqk, lp['diff_subln_g'],
                                          lambda_init, nbatch, n_ctx)
        x1c, h2c, h2pc, logits_c = _merge(u_c, y_bm, seq, n_ctx, *merge_w, yb_c, yc_c, yd_c, gate_c,
                                          lp['w_branch'], lp['w_out'], xc2d, cg1, lp['norm2_g'], csc2, csh2,
                                          lp['router_w'], rows_per_mod=rows_ctx)
        xc_out = _moe_block(h2c, h2pc, logits_c, lp, x1c, cg2, final_g, rows_per_mod=rows_ctx, final=False)
    return x_out, xc_out


def kernel(x, c, ctx, c_ctx, ada_w, ada_b, norm1_g, norm2_g, w_in, s5_lambda_re, s5_lambda_im, s5_log_step,
           s5_b_re, s5_b_im, s5_c_re, s5_c_im, s5_d, s5_glu_w, s5_glu_b, na_rpb, swa_sink, diff_lq1, diff_lk1,
           diff_lq2, diff_lk2, diff_subln_g, w_branch, w_out, router_w, router_b, exp_w_gate, exp_w_up,
           exp_w_down, sh_w_gate, sh_w_up, sh_w_down, final_g):
    nbatch, seq, d = x.shape
    n_ctx = ctx.shape[1]
    depth = ada_w.shape[0]
    assert d == D_MODEL and nbatch == 8
    stacked = dict(ada_w=ada_w, ada_b=ada_b, norm1_g=norm1_g, norm2_g=norm2_g, w_in=w_in,
                   s5_lambda_re=s5_lambda_re, s5_lambda_im=s5_lambda_im, s5_log_step=s5_log_step,
                   s5_b_re=s5_b_re, s5_b_im=s5_b_im, s5_c_re=s5_c_re, s5_c_im=s5_c_im, s5_d=s5_d,
                   s5_glu_w=s5_glu_w, s5_glu_b=s5_glu_b, na_rpb=na_rpb, swa_sink=swa_sink,
                   diff_lq1=diff_lq1, diff_lk1=diff_lk1, diff_lq2=diff_lq2, diff_lk2=diff_lk2,
                   diff_subln_g=diff_subln_g, w_branch=w_branch, w_out=w_out, router_w=router_w,
                   router_b=router_b, exp_w_gate=exp_w_gate, exp_w_up=exp_w_up, exp_w_down=exp_w_down,
                   sh_w_gate=sh_w_gate, sh_w_up=sh_w_up, sh_w_down=sh_w_down)
    tables = _rope_tables(seq)
    c16 = jnp.concatenate([c.astype(F32), c_ctx.reshape(1, d).astype(F32),
                           jnp.zeros((16 - nbatch - 1, d), F32)], axis=0)
    x2d = x.reshape(nbatch * seq, d).astype(F32)
    xc2d = ctx.reshape(nbatch * n_ctx, d).astype(F32)
    for l in range(depth):
        routed = ('exp_w_gate', 'exp_w_up', 'exp_w_down')
        lp = {k: (v.astype(F32) if k in routed else v[l]) for k, v in stacked.items()}
        lp['layer'] = l
        last = l == depth - 1
        x2d, xc2d = _layer(x2d, xc2d, c16, lp, l, tables, final_g, nbatch=nbatch, seq=seq, n_ctx=n_ctx,
                           with_ctx_out=not last, final=last)
    return x2d.reshape(nbatch, seq, d)
```

```python
import functools
import math

import numpy as np
import jax
import jax.numpy as jnp
from jax import lax
from jax.experimental import pallas as pl
from jax.experimental.pallas import tpu as pltpu
from jax.experimental.pallas import tpu_sc as plsc

F32 = jnp.float32
BF16 = jnp.bfloat16
HIGHEST = lax.Precision.HIGHEST

GRID_W = 64
EPS = 1e-6
NEG_INF = -1e30
ROPE_BASE = 10000.0
D_MODEL = 1024
BRANCH_WIDTH = 256
HEAD_DIM = 64
S5_GROUP = 16
S5_GROUPS = 16
S5_STATE = 64
S5_FLAT = S5_GROUPS * S5_STATE
NA_HEADS = 4
NA_WIN_ROWS = 8
NA_WIN_COLS = 16
SWA_KV_HEADS = 2
SWA_WINDOW = 128
DIFF_HEADS = 4
DIFF_QK_DIM = 32
N_EXPERTS = 64
N_EXPERT_GROUPS = 8
TOPK_GROUPS = 4
TOP_K = 6
EXPERT_HIDDEN = 256
ROUTED_SCALE = 2.5
GATE_WIDTH = 4 * D_MODEL

VMEM_LIMIT = 56 * 1024 * 1024


def _params(*sem):
    return pltpu.CompilerParams(dimension_semantics=sem, vmem_limit_bytes=VMEM_LIMIT)


def _nt_dot(a, b):
    return lax.dot_general(a, b, (((1,), (1,)), ((), ())), preferred_element_type=F32)


def _dot(a, b):
    return jnp.dot(a, b, preferred_element_type=F32)


def _rms(x):
    return x * lax.rsqrt(jnp.mean(x * x, axis=-1, keepdims=True) + EPS)


def _ada_kernel(c_ref, w_ref, b_ref, o_ref):
    c = c_ref[...]
    s = c * jax.nn.sigmoid(c)
    o_ref[...] = jnp.dot(s, w_ref[...], preferred_element_type=F32, precision=HIGHEST) + b_ref[...]


def _ada_mod(cc, w, b):
    rows, d = cc.shape
    width = w.shape[1]
    tn = 1536
    return pl.pallas_call(
        _ada_kernel,
        grid=(width // tn,),
        in_specs=[pl.BlockSpec((rows, d), lambda j: (0, 0)),
                  pl.BlockSpec((d, tn), lambda j: (0, j)),
                  pl.BlockSpec((1, tn), lambda j: (0, j))],
        out_specs=pl.BlockSpec((rows, tn), lambda j: (0, j)),
        out_shape=jax.ShapeDtypeStruct((rows, width), F32),
        compiler_params=_params("arbitrary"),
        name="ada_mod",
    )(cc, w, b.reshape(1, width))


_C_GATE = 0
_C_U = GATE_WIDTH
_C_NA = _C_U + 256
_C_SW = _C_NA + 768
_C_DF = _C_SW + 512
_C_END = _C_DF + 768


def _rope_apply(x, cos, sins, half):
    outs = []
    for j in range(x.shape[1] // 128):
        xs = x[:, j * 128:(j + 1) * 128]
        lane = lax.broadcasted_iota(jnp.int32, xs.shape, 1)
        lo = (lane % (2 * half)) < half
        partner = jnp.where(lo, pltpu.roll(xs, 128 - half, 1), pltpu.roll(xs, half, 1))
        outs.append(xs * cos + partner * sins)
    return outs[0] if len(outs) == 1 else jnp.concatenate(outs, axis=1)


def _inproj_kernel(x_ref, g_ref, sc_ref, sh_ref, w_ref, c64_ref, s64_ref, c32_ref, s32_ref,
                   gate_o, u_o, *outs, rope):
    h = _rms(x_ref[...]) * g_ref[...]
    h = h * (1.0 + sc_ref[0]) + sh_ref[0]
    hb = h.astype(BF16)

    def mm(c0, c1):
        return _dot(hb, w_ref[:, c0:c1])

    for k in range(GATE_WIDTH // 512):
        gate_o[:, k * 512:(k + 1) * 512] = jax.nn.sigmoid(mm(k * 512, (k + 1) * 512)).astype(BF16)
    u_o[...] = mm(_C_U, _C_U + 256)

    na = mm(_C_NA, _C_NA + 768)
    naq = na[:, 0:256] * (HEAD_DIM ** -0.5 * (LOG2E if rope else 1.0))

    sw = mm(_C_SW, _C_SW + 512)
    swq, swk = sw[:, 0:256], sw[:, 256:384]
    if rope:
        swq = _rope_apply(swq, c64_ref[...], s64_ref[...], 16)
        swk = _rope_apply(swk, c64_ref[...], s64_ref[...], 16)
    swq = swq * (HEAD_DIM ** -0.5 * (LOG2E if rope else 1.0))

    df = mm(_C_DF, _C_DF + 768)
    dfq, dfk = df[:, 0:256], df[:, 256:512]
    if rope:
        dfq = _rope_apply(dfq, c32_ref[...], s32_ref[...], 8)
        dfk = _rope_apply(dfk, c32_ref[...], s32_ref[...], 8)
    dfq = dfq * (DIFF_QK_DIM ** -0.5 * (LOG2E if rope else 1.0))
    if not rope:
        na_o, sw_o, df_o = outs
        na_o[:, 0:256] = naq.astype(BF16)
        na_o[:, 256:768] = na[:, 256:768].astype(BF16)
        sw_o[:, 0:256] = swq.astype(BF16)
        sw_o[:, 256:384] = swk.astype(BF16)
        sw_o[:, 384:512] = sw[:, 384:512].astype(BF16)
        df_o[:, 0:256] = dfq.astype(BF16)
        df_o[:, 256:512] = dfk.astype(BF16)
        df_o[:, 512:768] = df[:, 512:768].astype(BF16)
    else:
        na_k_o, sw_k_o, na_qt_o, sw_qt_o, df_qt_o, na_vt_o, sw_vt_o, df_k_o, df_vt_o = outs
        na_k_o[...] = na[:, 256:512].astype(BF16)
        sw_k_o[...] = swk.astype(BF16)
        for qt_o, q in ((na_qt_o, naq), (sw_qt_o, swq), (df_qt_o, dfq)):
            qt_o[0] = q.T.astype(BF16)
        na_vt_o[0] = _ones_row_vt(na[:, 512:768].T)
        sw_vt_o[0] = _ones_row_vt(sw[:, 384:512].T)
        df_vt = _ones_row_vt(df[:, 512:768].T)
        for j in range(df_k_o.shape[1]):
            df_k_o[0, j] = dfk[j * DIFF_CK:(j + 1) * DIFF_CK].astype(BF16)
            df_vt_o[0, j] = df_vt[:, j * DIFF_CK:(j + 1) * DIFF_CK]


def _ones_row_vt(vt):
    n = vt.shape[1]
    pad = jnp.where(lax.broadcasted_iota(jnp.int32, (VROWS - HEAD_DIM, n), 0) == 0, 1.0, 0.0)
    parts = []
    for h in range(vt.shape[0] // HEAD_DIM):
        parts += [vt[h * HEAD_DIM:(h + 1) * HEAD_DIM], pad]
    return jnp.concatenate(parts, axis=0).astype(BF16)


def _inproj(x2d, norm_g, sc, sh, w_bf16, tables, *, rows_per_mod, rope, seq):
    rows = x2d.shape[0]
    tm = 512
    assert rows % tm == 0 and rows_per_mod % tm == 0 and seq % tm == 0
    tiles_per_seq = seq // tm

    def mod_map(i):
        return ((i * tm) // rows_per_mod, 0, 0)

    def tab_map(i):
        return (i % tiles_per_seq, 0)

    tab_spec = pl.BlockSpec((tm, 128), tab_map)
    row = lambda w: pl.BlockSpec((tm, w), lambda i: (i, 0))
    out_specs = [row(GATE_WIDTH), row(256)]
    out_shape = [jax.ShapeDtypeStruct((rows, GATE_WIDTH), BF16),
                 jax.ShapeDtypeStruct((rows, 256), F32)]
    if not rope:
        for width in (768, 512, 768):
            out_specs.append(row(width))
            out_shape.append(jax.ShapeDtypeStruct((rows, width), BF16))
    else:
        for width in (256, 128):
            out_specs.append(row(width))
            out_shape.append(jax.ShapeDtypeStruct((rows, width), BF16))
        for nrow in (256, 256, 256, NA_HEADS * VROWS, SWA_KV_HEADS * VROWS):
            out_specs.append(pl.BlockSpec((1, nrow, tm), lambda i: (i // tiles_per_seq, 0, i % tiles_per_seq)))
            out_shape.append(jax.ShapeDtypeStruct((rows // seq, nrow, seq), BF16))
        assert tm % DIFF_CK == 0
        for shape in ((DIFF_CK, 256), (DIFF_HEADS * VROWS, DIFF_CK)):
            out_specs.append(pl.BlockSpec((1, tm // DIFF_CK) + shape,
                                          lambda i: (i // tiles_per_seq, i % tiles_per_seq, 0, 0)))
            out_shape.append(jax.ShapeDtypeStruct((rows // seq, seq // DIFF_CK) + shape, BF16))
    return pl.pallas_call(
        functools.partial(_inproj_kernel, rope=rope),
        grid=(rows // tm,),
        in_specs=[row(D_MODEL),
                  pl.BlockSpec((1, D_MODEL), lambda i: (0, 0)),
                  pl.BlockSpec((1, 1, D_MODEL), mod_map),
                  pl.BlockSpec((1, 1, D_MODEL), mod_map),
                  pl.BlockSpec((D_MODEL, _C_END), lambda i: (0, 0)),
                  tab_spec, tab_spec, tab_spec, tab_spec],
        out_specs=out_specs,
        out_shape=out_shape,
        compiler_params=_params("arbitrary"),
        name="inproj",
    )(x2d, norm_g.reshape(1, D_MODEL), sc, sh, w_bf16, *tables)


def _rope_tables(seq):
    t = jnp.arange(seq)
    rows = (t // GRID_W).astype(F32)
    cols = (t % GRID_W).astype(F32)
    lane = np.arange(128)
    out = []
    for dim in (64, 32):
        quarter = dim // 4
        inv_freq = ROPE_BASE ** (-jnp.arange(quarter, dtype=F32) / quarter)
        l = lane % dim
        use_col = l >= dim // 2
        fidx = l % quarter
        hi = (l % (dim // 2)) >= quarter
        ang_r = rows[:, None] * inv_freq[None, :]
        ang_c = cols[:, None] * inv_freq[None, :]
        ang = jnp.where(use_col[None, :], ang_c[:, fidx], ang_r[:, fidx])
        out.append(jnp.cos(ang))
        out.append(jnp.where(hi[None, :], jnp.sin(ang), -jnp.sin(ang)))
    return tuple(out)


S5_CHUNK = 128


def _s5_kernel(u_ref, win_ref, wout_ref, are_ref, aim_ref, y_ref, bu_ref, st_ref, *, tc, nb):
    d = pl.program_id(0)
    i = pl.program_id(1)

    @pl.when(i == 0)
    def _():
        st_ref[...] = jnp.zeros_like(st_ref)

    u = u_ref[...].reshape(tc * nb, BRANCH_WIDTH).astype(BF16)
    bu_ref[...] = _dot(u, win_ref[0])
    ar = jnp.broadcast_to(are_ref[0], (nb, S5_FLAT))
    ai = jnp.broadcast_to(aim_ref[0], (nb, S5_FLAT))

    def body(j, carry):
        xr, xi = carry
        t = j + d * (tc - 1 - 2 * j)
        row = pl.multiple_of(t * nb, nb)
        br = bu_ref[pl.ds(row, nb), 0:S5_FLAT]
        bi = bu_ref[pl.ds(row, nb), S5_FLAT:2 * S5_FLAT]
        nr = ar * xr - ai * xi + br
        ni = ar * xi + ai * xr + bi
        bu_ref[pl.ds(row, nb), 0:S5_FLAT] = nr
        bu_ref[pl.ds(row, nb), S5_FLAT:2 * S5_FLAT] = ni
        return nr, ni

    xr, xi = lax.fori_loop(0, tc, body, (st_ref[:, 0:S5_FLAT], st_ref[:, S5_FLAT:2 * S5_FLAT]), unroll=4)
    st_ref[:, 0:S5_FLAT] = xr
    st_ref[:, S5_FLAT:2 * S5_FLAT] = xi
    y = _dot(bu_ref[...].astype(BF16), wout_ref[0])
    y_ref[0] = y.reshape(tc, nb, BRANCH_WIDTH)


def _s5_scan(u_tm, win, wout, a_re, a_im, n_ctx):
    s_len, nb, _ = u_tm.shape
    tc = S5_CHUNK
    assert nb == 8 and s_len % tc == 0 and n_ctx % tc == 0
    nct = n_ctx // tc
    nlt = (s_len - n_ctx) // tc

    def chunk(d, i):
        rev = jnp.where(i < nct, nct - 1 - i, 2 * nct + nlt - 1 - i)
        return jnp.where(d == 0, i, rev)

    def out_chunk(d, i):
        c = chunk(d, i)
        return jnp.where(c < nct, nlt + c, c - nct)

    return pl.pallas_call(
        functools.partial(_s5_kernel, tc=tc, nb=nb),
        grid=(2, nct + nlt),
        in_specs=[pl.BlockSpec((tc, nb, BRANCH_WIDTH), lambda d, i: (chunk(d, i), 0, 0)),
                  pl.BlockSpec((1, BRANCH_WIDTH, 2 * S5_FLAT), lambda d, i: (d, 0, 0)),
                  pl.BlockSpec((1, 2 * S5_FLAT, BRANCH_WIDTH), lambda d, i: (d, 0, 0)),
                  pl.BlockSpec((1, 1, S5_FLAT), lambda d, i: (d, 0, 0)),
                  pl.BlockSpec((1, 1, S5_FLAT), lambda d, i: (d, 0, 0))],
        out_specs=pl.BlockSpec((1, tc, nb, BRANCH_WIDTH), lambda d, i: (d, out_chunk(d, i), 0, 0)),
        out_shape=jax.ShapeDtypeStruct((2, s_len, nb, BRANCH_WIDTH), F32),
        scratch_shapes=[pltpu.VMEM((tc * nb, 2 * S5_FLAT), F32),
                        pltpu.VMEM((nb, 2 * S5_FLAT), F32)],
        compiler_params=_params("arbitrary", "arbitrary"),
        name="s5_scan",
    )(u_tm, win, wout, a_re, a_im)


def _s5_params(lam_re, lam_im, log_step, b_re, b_im, c_re, c_im):
    lr = lam_re.astype(F32)
    li = lam_im.astype(F32)
    dt = jnp.exp(log_step.astype(F32))[..., None]
    mag = jnp.exp(lr * dt)
    a_re = mag * jnp.cos(li * dt)
    a_im = mag * jnp.sin(li * dt)
    nr, ni, den = a_re - 1.0, a_im, lr * lr + li * li
    k_re = ((nr * lr + ni * li) / den)[..., None]
    k_im = ((ni * lr - nr * li) / den)[..., None]
    br = b_re.astype(F32)
    bi = b_im.astype(F32)
    bb_re = k_re * br - k_im * bi
    bb_im = k_re * bi + k_im * br
    eye = jnp.eye(S5_GROUPS, dtype=F32)

    def blockdiag_in(bb):
        m = jnp.einsum('dgpc,gh->dgchp', bb, eye)
        return m.reshape(2, S5_GROUPS * S5_GROUP, S5_GROUPS * S5_STATE)

    def blockdiag_out(cc):
        m = jnp.einsum('dgcp,gh->dgphc', cc, eye)
        return m.reshape(2, S5_GROUPS * S5_STATE, S5_GROUPS * S5_GROUP)

    win = jnp.concatenate([blockdiag_in(bb_re), blockdiag_in(bb_im)], axis=2).astype(BF16)
    wout = jnp.concatenate([blockdiag_out(c_re.astype(F32)), -blockdiag_out(c_im.astype(F32))], axis=1).astype(BF16)
    return win, wout, a_re.reshape(2, 1, S5_FLAT), a_im.reshape(2, 1, S5_FLAT)


def _softmax_parts(scores, extra=None):
    m = scores[0].max(axis=-1, keepdims=True)
    for s in scores[1:]:
        m = jnp.maximum(m, s.max(axis=-1, keepdims=True))
    if extra is not None:
        m = jnp.maximum(m, extra)
    ps = [jnp.exp(s - m) for s in scores]
    l = ps[0].sum(axis=-1, keepdims=True)
    for p in ps[1:]:
        l = l + p.sum(axis=-1, keepdims=True)
    if extra is not None:
        l = l + jnp.exp(extra - m)
    return ps, l


NA_QROWS = 2
NA_KROWS = 10
VROWS = HEAD_DIM + 16


def _na_window_start(r, grid_rows):
    start = jnp.clip(r - NA_WIN_ROWS // 2, 0, grid_rows - NA_WIN_ROWS)
    return (jnp.minimum(start, grid_rows - NA_KROWS) // 2) * 2


def _head_blockdiag(qt, qbd_ref, nheads, rows_per_head):
    n = qt.shape[1]
    row_h = lax.broadcasted_iota(jnp.int32, qt.shape, 0) // rows_per_head
    zero = jnp.zeros_like(qt)
    for h in range(nheads):
        qbd_ref[:, h * n:(h + 1) * n] = jnp.where(row_h == h, qt, zero)


def _na_scores(qt, k_ref, kc_ref, bias, qbd_ref, p, *, grid_rows):
    nk = NA_KROWS * GRID_W
    off = pl.multiple_of(_na_window_start(NA_QROWS * p, grid_rows) * GRID_W, 128)
    _head_blockdiag(qt, qbd_ref, NA_HEADS, HEAD_DIM)
    qbd = qbd_ref[...]
    s_loc = _dot(k_ref[0, pl.ds(off, nk), :], qbd) + bias
    s_ctx = _dot(kc_ref[0], qbd)
    return off, s_loc, s_ctx


def _na_values(off, s_loc, s_ctx, vt_ref, vct_ref):
    nq = NA_QROWS * GRID_W
    m = jnp.maximum(s_loc.max(axis=0, keepdims=True), s_ctx.max(axis=0, keepdims=True))
    p_loc = jnp.exp2((s_loc - m).astype(BF16))
    p_ctx = jnp.exp2((s_ctx - m).astype(BF16))
    vw = vt_ref[0, :, pl.ds(off, NA_KROWS * GRID_W)]
    outs = []
    for h in range(NA_HEADS):
        rows = slice(h * VROWS, (h + 1) * VROWS)
        cols = slice(h * nq, (h + 1) * nq)
        o = _dot(vw[rows], p_loc[:, cols]) + _dot(vct_ref[0, rows, :], p_ctx[:, cols])
        outs.append(o[0:HEAD_DIM] / o[HEAD_DIM:HEAD_DIM + 1])
    return jnp.concatenate(outs, axis=0).T.astype(BF16)


def _na_classes(grid_rows):
    return [0, 2, 4, grid_rows - 4, grid_rows - 2]


def _na_bias_table(rpb, grid_rows):
    col = np.arange(GRID_W)
    cstart = np.clip(col - NA_WIN_COLS // 2, 0, GRID_W - NA_WIN_COLS)
    col_in = (col[None, :] >= cstart[:, None]) & (col[None, :] < cstart[:, None] + NA_WIN_COLS)
    cb = np.clip(col[None, :] - col[:, None] + (NA_WIN_COLS - 1), 0, 2 * NA_WIN_COLS - 2)
    classes = _na_classes(grid_rows)
    rbi = np.zeros((len(classes), NA_QROWS, NA_KROWS), np.int64)
    row_in = np.zeros((len(classes), NA_QROWS, NA_KROWS), bool)
    for c, r in enumerate(classes):
        a_row = (min(int(np.clip(r - NA_WIN_ROWS // 2, 0, grid_rows - NA_WIN_ROWS)), grid_rows - NA_KROWS) // 2) * 2
        for qi in range(NA_QROWS):
            start_q = int(np.clip(r + qi - NA_WIN_ROWS // 2, 0, grid_rows - NA_WIN_ROWS))
            for j in range(NA_KROWS):
                row_in[c, qi, j] = start_q <= a_row + j < start_q + NA_WIN_ROWS
                rbi[c, qi, j] = np.clip(a_row + j - (r + qi) + NA_WIN_ROWS - 1, 0, 2 * NA_WIN_ROWS - 2)
    oh_row = jnp.asarray(rbi[..., None] == np.arange(2 * NA_WIN_ROWS - 1), F32)
    oh_col = jnp.asarray(cb[:, :, None] == np.arange(2 * NA_WIN_COLS - 1), F32)
    t = jnp.einsum('hab,cija,qkb->cjkhiq', rpb.astype(F32), oh_row, oh_col, precision=HIGHEST) * LOG2E
    valid = row_in.transpose(0, 2, 1)[:, :, None, None, :, None] & col_in.T[None, None, :, None, None, :]
    t = jnp.where(valid, t, NEG_INF)
    return t.reshape(len(classes), NA_KROWS * GRID_W, NA_HEADS * NA_QROWS * GRID_W)


def _augment_vt(v, nheads):
    nb, s, _ = v.shape
    v = v.reshape(nb, s, nheads, HEAD_DIM)
    pad = jnp.zeros((nb, s, nheads, VROWS - HEAD_DIM), v.dtype).at[..., 0].set(1.0)
    return jnp.concatenate([v, pad], axis=-1).reshape(nb, s, nheads * VROWS).transpose(0, 2, 1)


SWA_BLOCK = 128


def _swa_scores(qt, k_ref, kc_ref, qbd_ref, n, *, seq):
    band = 3 * SWA_BLOCK
    nq = SWA_BLOCK
    nqh = 2 * SWA_KV_HEADS
    bstart = pl.multiple_of(jnp.clip((n - 1) * SWA_BLOCK, 0, seq - band), SWA_BLOCK)
    zero = jnp.zeros((HEAD_DIM, nq), qt.dtype)
    for hq in range(nqh):
        qh = qt[hq * HEAD_DIM:(hq + 1) * HEAD_DIM]
        qbd_ref[:, hq * nq:(hq + 1) * nq] = jnp.concatenate([qh, zero] if hq // 2 == 0 else [zero, qh], axis=0)
    qbd = qbd_ref[...]
    kpos = bstart + lax.broadcasted_iota(jnp.int32, (band, nqh * nq), 0)
    qpos = n * SWA_BLOCK + lax.broadcasted_iota(jnp.int32, (band, nqh * nq), 1) % nq
    s_loc = jnp.where(jnp.abs(qpos - kpos) <= SWA_WINDOW, _dot(k_ref[0, pl.ds(bstart, band), :], qbd), NEG_INF)
    s_ctx = _dot(kc_ref[0], qbd)
    return bstart, s_loc, s_ctx


def _swa_values(bstart, s_loc, s_ctx, vt_ref, vct_ref, sink_ref):
    nq = SWA_BLOCK
    nqh = 2 * SWA_KV_HEADS
    sink = jnp.concatenate([jnp.broadcast_to(sink_ref[0:1, hq:hq + 1] * LOG2E, (1, nq)) for hq in range(nqh)], axis=1)
    m = jnp.maximum(jnp.maximum(s_loc.max(axis=0, keepdims=True), s_ctx.max(axis=0, keepdims=True)), sink)
    p_loc = jnp.exp2((s_loc - m).astype(BF16))
    p_ctx = jnp.exp2((s_ctx - m).astype(BF16))
    p_sink = jnp.exp2(sink - m)
    vw = vt_ref[0, :, pl.ds(bstart, 3 * SWA_BLOCK)]
    outs = []
    for hq in range(nqh):
        rows = slice((hq // 2) * VROWS, (hq // 2 + 1) * VROWS)
        cols = slice(hq * nq, (hq + 1) * nq)
        o = _dot(vw[rows], p_loc[:, cols]) + _dot(vct_ref[0, rows, :], p_ctx[:, cols])
        outs.append(o[0:HEAD_DIM] / (o[HEAD_DIM:HEAD_DIM + 1] + p_sink[:, cols]))
    return jnp.concatenate(outs, axis=0).T.astype(BF16)


LOCAL_BLOCKS = 4


def _local_attn_kernel(na_qt, na_k, na_vt, na_kc, na_vct, sw_qt, sw_k, sw_vt, sw_kc, sw_vct, sink_ref, *rest,
                       grid_rows, seq):
    bias_refs = rest[0:LOCAL_BLOCKS]
    na_o, sw_o, na_qbd, sw_qbd = rest[LOCAL_BLOCKS:]
    nq = SWA_BLOCK
    scores = []
    for j in range(LOCAL_BLOCKS):
        p = pl.program_id(1) * LOCAL_BLOCKS + j
        cols = slice(j * nq, (j + 1) * nq)
        scores.append((_na_scores(na_qt[0, :, cols], na_k, na_kc, bias_refs[j][0], na_qbd.at[j], p, grid_rows=grid_rows),
                       _swa_scores(sw_qt[0, :, cols], sw_k, sw_kc, sw_qbd.at[j], p, seq=seq)))
    for j, (na_s, sw_s) in enumerate(scores):
        na_o[j * nq:(j + 1) * nq, :] = _na_values(*na_s, na_vt, na_vct)
        sw_o[j * nq:(j + 1) * nq, :] = _swa_values(*sw_s, sw_vt, sw_vct, sink_ref)


def _local_attention(na_qt, na_vt, na_k, na_c, bias, sw_qt, sw_vt, sw_k, sw_c, sink, nbatch, seq, n_ctx):
    grid_rows = seq // GRID_W
    nq = NA_QROWS * GRID_W
    nstep = seq // nq
    nqh = 2 * SWA_KV_HEADS
    assert grid_rows >= NA_KROWS and grid_rows % NA_QROWS == 0 and nq == SWA_BLOCK and seq >= 3 * SWA_BLOCK
    nal = na_k.reshape(nbatch, seq, 256)
    nac = na_c.reshape(nbatch, n_ctx, 768)
    na_vct = _augment_vt(nac[:, :, 512:768], NA_HEADS)
    sink_pad = jnp.zeros((1, 128), F32).at[0, 0:nqh].set(sink.astype(F32))
    swl = sw_k.reshape(nbatch, seq, 128)
    swc = sw_c.reshape(nbatch, n_ctx, 512)
    sw_vct = _augment_vt(swc[:, :, 384:512], SWA_KV_HEADS)

    nblk = LOCAL_BLOCKS
    assert nstep % nblk == 0

    def cls_map(j):
        def cls(b, s):
            r = NA_QROWS * (s * nblk + j)
            c = jnp.where(r < 4, r // 2, jnp.where(r >= grid_rows - 4, (r - (grid_rows - 4)) // 2 + 3, 2))
            return (c, 0, 0)
        return cls

    out = jax.ShapeDtypeStruct((nbatch * seq, 256), BF16)
    out_spec = pl.BlockSpec((nblk * nq, 256), lambda b, s: (b * (nstep // nblk) + s, 0))
    bias_specs = [pl.BlockSpec((1, NA_KROWS * GRID_W, NA_HEADS * nq), cls_map(j)) for j in range(nblk)]
    return pl.pallas_call(
        functools.partial(_local_attn_kernel, grid_rows=grid_rows, seq=seq),
        grid=(nbatch, nstep // nblk),
        in_specs=[pl.BlockSpec((1, 256, nblk * nq), lambda b, s: (b, 0, s)),
                  pl.BlockSpec((1, seq, 256), lambda b, s: (b, 0, 0)),
                  pl.BlockSpec((1, NA_HEADS * VROWS, seq), lambda b, s: (b, 0, 0)),
                  pl.BlockSpec((1, n_ctx, 256), lambda b, s: (b, 0, 1)),
                  pl.BlockSpec((1, NA_HEADS * VROWS, n_ctx), lambda b, s: (b, 0, 0)),
                  pl.BlockSpec((1, 256, nblk * nq), lambda b, s: (b, 0, s)),
                  pl.BlockSpec((1, seq, 128), lambda b, s: (b, 0, 0)),
                  pl.BlockSpec((1, SWA_KV_HEADS * VROWS, seq), lambda b, s: (b, 0, 0)),
                  pl.BlockSpec((1, n_ctx, 128), lambda b, s: (b, 0, 2)),
                  pl.BlockSpec((1, SWA_KV_HEADS * VROWS, n_ctx), lambda b, s: (b, 0, 0)),
                  pl.BlockSpec((1, 128), lambda b, s: (0, 0))] + bias_specs,
        out_specs=[out_spec, out_spec],
        out_shape=[out, out],
        scratch_shapes=[pltpu.VMEM((nblk, 256, NA_HEADS * nq), BF16),
                        pltpu.VMEM((nblk, SWA_KV_HEADS * HEAD_DIM, nqh * nq), BF16)],
        compiler_params=_params("arbitrary", "arbitrary"),
        name="local_attention",
    )(na_qt, nal, na_vt, nac, na_vct, sw_qt, swl, sw_vt, swc, sw_vct, sink_pad, *([bias] * nblk))


DIFF_TQ = 512
DIFF_CK = 256
LOG2E = math.log2(math.e)


def _diff_lambda(lq1_ref, lk1_ref, lq2_ref, lk2_ref, lambda_init):
    s1 = jnp.sum(lq1_ref[...] * lk1_ref[...], axis=-1, keepdims=True)
    s2 = jnp.sum(lq2_ref[...] * lk2_ref[...], axis=-1, keepdims=True)
    return jnp.exp(s1) - jnp.exp(s2) + lambda_init


def _stack_maps(qh):
    lane = lax.broadcasted_iota(jnp.int32, qh.shape, 1)
    zero = jnp.zeros_like(qh)
    return jnp.concatenate([jnp.where(lane < DIFF_QK_DIM, qh, zero),
                            jnp.where(lane >= DIFF_QK_DIM, qh, zero)], axis=0)


def _subln(o0, o1, lam, g, lambda_init):
    o = o0 - lam * o1
    return _rms(o) * g * (1.0 - lambda_init)


def _diff_kernel(qt_ref, k_ref, vt_ref, kc_ref, vct_ref, lq1_ref, lk1_ref, lq2_ref, lk2_ref, g_ref, o_ref,
                 qbd_ref, acc_ref, s_ref, *, nlat, lambda_init):
    lam = _diff_lambda(lq1_ref, lk1_ref, lq2_ref, lk2_ref, lambda_init)
    qt = qt_ref[0]
    tq = qt.shape[1]
    w = 2 * tq
    row = lax.broadcasted_iota(jnp.int32, qt.shape, 0) // DIFF_QK_DIM
    zero = jnp.zeros_like(qt)
    for j in range(2 * DIFF_HEADS):
        qbd_ref[:, j * tq:(j + 1) * tq] = jnp.where(row == j, qt, zero)
    acc_ref[...] = jnp.zeros_like(acc_ref)

    def is_ctx(c):
        return isinstance(c, int) and c == nlat

    def scores(slot, c, h):
        kblk = kc_ref[0] if is_ctx(c) else k_ref[0, c]
        s = _dot(kblk, qbd_ref[:, h * w:(h + 1) * w])
        s_ref[slot, h] = s
        return s.max(axis=0, keepdims=True)

    def softmax_pv(slot, c, h, m_run, m_chunk):
        m_new = jnp.maximum(m_run, m_chunk)
        alpha = jnp.exp2(m_run - m_new)
        p = jnp.exp2((s_ref[slot, h] - m_new).astype(BF16))
        rows = slice(h * VROWS, (h + 1) * VROWS)
        vblk = vct_ref[0, rows, :] if is_ctx(c) else vt_ref[0, c, rows, :]
        acc_ref[h] = alpha * acc_ref[h] + _dot(vblk, p)
        return m_new

    def step(slot, c, carry, last=False):
        m_run, m_chunk = carry
        new_run, new_chunk = [], []
        for h in range(DIFF_HEADS):
            if not last:
                new_chunk.append(scores(1 - slot, c + 1, h))
            new_run.append(softmax_pv(slot, c, h, m_run[h], m_chunk[h]))
        return tuple(new_run), tuple(new_chunk)

    def body(i, carry):
        c = 2 * i
        return step(1, c + 1, step(0, c, carry))

    carry = (tuple(jnp.full((1, w), NEG_INF, F32) for _ in range(DIFF_HEADS)),
             tuple(scores(0, 0, h) for h in range(DIFF_HEADS)))
    nloop = (nlat - 1) // 2
    carry = lax.fori_loop(0, nloop, body, carry)
    for c in range(2 * nloop, nlat + 1):
        carry = step(c % 2, c, carry, last=c == nlat)
    outs = []
    for h in range(DIFF_HEADS):
        o = acc_ref[h, 0:HEAD_DIM, :] / acc_ref[h, HEAD_DIM:HEAD_DIM + 1, :]
        d = o[:, 0:tq] - lam * o[:, tq:w]
        d = d * lax.rsqrt(jnp.mean(d * d, axis=0, keepdims=True) + EPS)
        outs.append(d * g_ref[...] * (1.0 - lambda_init))
    o_ref[...] = jnp.concatenate(outs, axis=0).T.astype(BF16)


def _diff_attention(qt, k4, vt4, df_c, lqk, subln_g, lambda_init, nbatch, seq, n_ctx):
    tq, ck = min(DIFF_TQ, seq), DIFF_CK
    assert seq % ck == 0 and seq % tq == 0 and n_ctx == ck
    nlat = seq // ck
    nq = seq // tq
    dfc = df_c.reshape(nbatch, n_ctx, 768)
    vct = _augment_vt(dfc[:, :, 512:768], DIFF_HEADS)
    vec = pl.BlockSpec((1, DIFF_QK_DIM), lambda b, n: (0, 0))
    return pl.pallas_call(
        functools.partial(_diff_kernel, nlat=nlat, lambda_init=lambda_init),
        grid=(nbatch, nq),
        in_specs=[pl.BlockSpec((1, 256, tq), lambda b, n: (b, 0, n)),
                  pl.BlockSpec((1, nlat, ck, 256), lambda b, n: (b, 0, 0, 0)),
                  pl.BlockSpec((1, nlat, DIFF_HEADS * VROWS, ck), lambda b, n: (b, 0, 0, 0)),
                  pl.BlockSpec((1, n_ctx, 256), lambda b, n: (b, 0, 1)),
                  pl.BlockSpec((1, DIFF_HEADS * VROWS, n_ctx), lambda b, n: (b, 0, 0)),
                  vec, vec, vec, vec,
                  pl.BlockSpec((HEAD_DIM, 1), lambda b, n: (0, 0))],
        out_specs=pl.BlockSpec((tq, 256), lambda b, n: (b * nq + n, 0)),
        out_shape=jax.ShapeDtypeStruct((nbatch * seq, 256), BF16),
        scratch_shapes=[pltpu.VMEM((256, 2 * DIFF_HEADS * tq), BF16),
                        pltpu.VMEM((DIFF_HEADS, VROWS, 2 * tq), F32),
                        pltpu.VMEM((2, DIFF_HEADS, ck, 2 * tq), F32)],
        compiler_params=_params("arbitrary", "arbitrary"),
        name="diff_attention",
    )(qt, k4, vt4, dfc, vct, *lqk, subln_g.reshape(HEAD_DIM, 1).astype(F32))


def _ctx_attn_kernel(na_ref, sw_ref, df_ref, sink_ref, lq1_ref, lk1_ref, lq2_ref, lk2_ref, g_ref,
                     nb_o, sw_o, df_o, *, lambda_init):
    n = na_ref.shape[0]
    na = na_ref[...]
    outs = []
    for h in range(NA_HEADS):
        sl = slice(h * HEAD_DIM, (h + 1) * HEAD_DIM)
        (p,), l = _softmax_parts([_nt_dot(na[:, sl], na[:, 256 + h * HEAD_DIM:256 + (h + 1) * HEAD_DIM])])
        outs.append(_dot(p.astype(BF16), na[:, 512 + h * HEAD_DIM:512 + (h + 1) * HEAD_DIM]) / l)
    nb_o[...] = jnp.concatenate(outs, axis=1).astype(BF16)
    sw = sw_ref[...]
    outs = []
    for hq in range(4):
        kv = hq // 2
        k = sw[:, 256 + kv * HEAD_DIM:256 + (kv + 1) * HEAD_DIM]
        v = sw[:, 384 + kv * HEAD_DIM:384 + (kv + 1) * HEAD_DIM]
        sk = jnp.broadcast_to(sink_ref[0:1, hq:hq + 1], (n, 1))
        (p,), l = _softmax_parts([_nt_dot(sw[:, hq * HEAD_DIM:(hq + 1) * HEAD_DIM], k)], extra=sk)
        outs.append(_dot(p.astype(BF16), v) / l)
    sw_o[...] = jnp.concatenate(outs, axis=1).astype(BF16)
    lam = _diff_lambda(lq1_ref, lk1_ref, lq2_ref, lk2_ref, lambda_init)
    df = df_ref[...]
    outs = []
    for h in range(DIFF_HEADS):
        sl = slice(h * HEAD_DIM, (h + 1) * HEAD_DIM)
        q2 = _stack_maps(df[:, sl])
        (p,), l = _softmax_parts([_nt_dot(q2, df[:, 256 + h * HEAD_DIM:256 + (h + 1) * HEAD_DIM])])
        o = _dot(p.astype(BF16), df[:, 512 + h * HEAD_DIM:512 + (h + 1) * HEAD_DIM]) / l
        outs.append(_subln(o[0:n], o[n:2 * n], lam, g_ref[...], lambda_init))
    df_o[...] = jnp.concatenate(outs, axis=1).astype(BF16)


def _ctx_attention(na_c, sw_c, df_c, sink, lqk, subln_g, lambda_init, nbatch, n_ctx):
    sink_pad = jnp.zeros((1, 128), F32).at[0, 0:4].set(sink.astype(F32))
    vec = pl.BlockSpec((1, DIFF_QK_DIM), lambda b: (0, 0))
    out = jax.ShapeDtypeStruct((nbatch * n_ctx, 256), BF16)
    return pl.pallas_call(
        functools.partial(_ctx_attn_kernel, lambda_init=lambda_init),
        grid=(nbatch,),
        in_specs=[pl.BlockSpec((n_ctx, 768), lambda b: (b, 0)),
                  pl.BlockSpec((n_ctx, 512), lambda b: (b, 0)),
                  pl.BlockSpec((n_ctx, 768), lambda b: (b, 0)),
                  pl.BlockSpec((1, 128), lambda b: (0, 0)),
                  vec, vec, vec, vec,
                  pl.BlockSpec((1, HEAD_DIM), lambda b: (0, 0))],
        out_specs=[pl.BlockSpec((n_ctx, 256), lambda b: (b, 0))] * 3,
        out_shape=[out, out, out],
        compiler_params=_params("arbitrary"),
        name="ctx_attention",
    )(na_c, sw_c, df_c, sink_pad, *lqk, subln_g.reshape(1, HEAD_DIM).astype(F32))


def _merge_kernel(u_ref, yf_ref, yr_ref, d_ref, gw_ref, gb_ref, yb_ref, yc_ref, yd_ref, gate_ref,
                  wb_ref, wo_ref, x_ref, g1_ref, n2_ref, sc2_ref, sh2_ref, rw_ref,
                  x1_o, h2_o, h2p_o, lg_o):
    y = u_ref[...] * d_ref[...] + yf_ref[0, 0] + yr_ref[0, 0]
    a = jax.nn.gelu(y, approximate=True)
    ya = a * jax.nn.sigmoid(_dot(a.astype(BF16), gw_ref[...]) + gb_ref[...])
    branches = (ya.astype(BF16), yb_ref[...], yc_ref[...], yd_ref[...])
    acc = None
    for i in range(4):
        t = gate_ref[:, i * D_MODEL:(i + 1) * D_MODEL].astype(F32) * _dot(branches[i], wb_ref[i])
        acc = t if acc is None else acc + t
    mixed = _dot(acc.astype(BF16), wo_ref[...])
    x1 = x_ref[...] + g1_ref[0] * mixed
    x1_o[...] = x1
    h2 = _rms(x1) * n2_ref[...]
    h2 = h2 * (1.0 + sc2_ref[0]) + sh2_ref[0]
    h2_o[...] = h2.astype(BF16)
    h2p_o[...] = _pack_rows(h2[:, 0:HALF_D], h2[:, HALF_D:D_MODEL])
    lg_o[...] = lax.dot_general(rw_ref[...], h2, (((1,), (1,)), ((), ())), preferred_element_type=F32,
                                precision=HIGHEST)


def _merge(u, y_bm, y_start, rows_per_seq, s5_d, glu_w, glu_b, yb, yc, yd, gate, wb, wo, x2d, g1, norm2_g, sc2, sh2,
           router_w, *, rows_per_mod):
    rows = x2d.shape[0]
    tm = min(512, rows_per_seq)
    assert rows % tm == 0 and rows_per_mod % tm == 0 and rows_per_seq % tm == 0 and y_start % tm == 0
    tiles_per_seq = rows_per_seq // tm

    def mod_map(i):
        return ((i * tm) // rows_per_mod, 0, 0)

    def y_spec(d):
        return pl.BlockSpec((1, 1, tm, 256), lambda i: (d, i // tiles_per_seq, y_start // tm + i % tiles_per_seq, 0))

    row = lambda w: pl.BlockSpec((tm, w), lambda i: (i, 0))
    full = lambda *shape: pl.BlockSpec(shape, lambda i: (0,) * len(shape))
    mod = pl.BlockSpec((1, 1, D_MODEL), mod_map)
    return pl.pallas_call(
        _merge_kernel,
        grid=(rows // tm,),
        in_specs=[row(256), y_spec(0), y_spec(1), full(1, 256), full(256, 256), full(1, 256),
                  row(256), row(256), row(256), row(GATE_WIDTH),
                  full(4, 256, D_MODEL), full(D_MODEL, D_MODEL), row(D_MODEL),
                  mod, full(1, D_MODEL), mod, mod, full(N_EXPERTS, D_MODEL)],
        out_specs=[row(D_MODEL), row(D_MODEL), row(HALF_D), pl.BlockSpec((N_EXPERTS, tm), lambda i: (0, i))],
        out_shape=[jax.ShapeDtypeStruct((rows, D_MODEL), F32),
                   jax.ShapeDtypeStruct((rows, D_MODEL), BF16),
                   jax.ShapeDtypeStruct((rows, HALF_D), jnp.int32),
                   jax.ShapeDtypeStruct((N_EXPERTS, rows), F32)],
        compiler_params=_params("arbitrary"),
        name="merge",
    )(u, y_bm, y_bm, s5_d.reshape(1, 256).astype(F32), glu_w.astype(BF16), glu_b.reshape(1, 256).astype(F32),
      yb, yc, yd, gate, wb.astype(BF16), wo.astype(BF16), x2d, g1, norm2_g.reshape(1, D_MODEL), sc2, sh2,
      router_w.astype(F32).T)


def _router_kernel(lg_ref, b_ref, tri_ref, idx_ref, rank_ref, w_ref, cnt_ref, base_ref):
    tr = lg_ref.shape[1]
    gsz = N_EXPERTS // N_EXPERT_GROUPS
    sc = jax.nn.sigmoid(lg_ref[...])
    bi = sc + b_ref[...]
    e_iota = lax.broadcasted_iota(jnp.int32, (gsz, tr), 0).astype(F32)
    groups = [bi[g * gsz:(g + 1) * gsz] for g in range(N_EXPERT_GROUPS)]
    gs = []
    for bg in groups:
        m1 = bg.max(axis=0, keepdims=True)
        i1 = jnp.where(bg == m1, e_iota, float(gsz)).min(axis=0, keepdims=True)
        m2 = jnp.where(e_iota == i1, -jnp.inf, bg).max(axis=0, keepdims=True)
        gs.append(m1 + m2)
    v = []
    for g in range(N_EXPERT_GROUPS):
        rank = jnp.zeros((1, tr), F32)
        for g2 in range(N_EXPERT_GROUPS):
            if g2 == g:
                continue
            beats = (gs[g2] >= gs[g]) if g2 < g else (gs[g2] > gs[g])
            rank = rank + jnp.where(beats, 1.0, 0.0)
        v.append(jnp.where(rank < TOPK_GROUPS, groups[g], NEG_INF))
    flat = [e_iota + float(g * gsz) for g in range(N_EXPERT_GROUPS)]
    sel = [jnp.zeros((gsz, tr), F32) for _ in range(N_EXPERT_GROUPS)]
    picks = []
    for _ in range(TOP_K):
        m = v[0].max(axis=0, keepdims=True)
        for g in range(1, N_EXPERT_GROUPS):
            m = jnp.maximum(m, v[g].max(axis=0, keepdims=True))
        am = jnp.where(v[0] == m, flat[0], float(N_EXPERTS)).min(axis=0, keepdims=True)
        for g in range(1, N_EXPERT_GROUPS):
            am = jnp.minimum(am, jnp.where(v[g] == m, flat[g], float(N_EXPERTS)).min(axis=0, keepdims=True))
        hits = []
        for g in range(N_EXPERT_GROUPS):
            hit = flat[g] == am
            hits.append(hit)
            sel[g] = jnp.where(hit, 1.0, sel[g])
            v[g] = jnp.where(hit, -jnp.inf, v[g])
        picks.append((am, hits))
    scg = [sc[g * gsz:(g + 1) * gsz] for g in range(N_EXPERT_GROUPS)]
    den = (sel[0] * scg[0]).sum(axis=0, keepdims=True)
    for g in range(1, N_EXPERT_GROUPS):
        den = den + (sel[g] * scg[g]).sum(axis=0, keepdims=True)

    @pl.when(pl.program_id(0) == 0)
    def _():
        base_ref[...] = jnp.zeros_like(base_ref)

    sel_all = jnp.concatenate(sel, axis=0)
    before = _dot(sel_all.astype(jnp.bfloat16), tri_ref[...]) + base_ref[...]
    for k, (am, hits) in enumerate(picks):
        wk = jnp.zeros((1, tr), F32)
        rk = jnp.zeros((1, tr), F32)
        for g in range(N_EXPERT_GROUPS):
            wk = wk + jnp.where(hits[g], scg[g], 0.0).sum(axis=0, keepdims=True)
            rk = rk + jnp.where(hits[g], before[g * gsz:(g + 1) * gsz], 0.0).sum(axis=0, keepdims=True)
        idx_ref[k:k + 1, :] = am.astype(jnp.int32)
        rank_ref[k:k + 1, :] = rk.astype(jnp.int32)
        w_ref[k:k + 1, :] = wk / den * ROUTED_SCALE
    idx_ref[TOP_K:8, :] = jnp.zeros((8 - TOP_K, tr), jnp.int32)
    rank_ref[TOP_K:8, :] = jnp.zeros((8 - TOP_K, tr), jnp.int32)
    w_ref[TOP_K:8, :] = jnp.zeros((8 - TOP_K, tr), F32)
    base_ref[...] += sel_all.sum(axis=1, keepdims=True)
    cnt_ref[...] = base_ref[...].astype(jnp.int32)


ROUTER_TILE = 512


def _router(logits_t, router_b):
    ne, rows = logits_t.shape
    tr = ROUTER_TILE
    assert rows % tr == 0
    tri = jnp.asarray(np.triu(np.ones((tr, tr), np.float32), k=1), jnp.bfloat16)
    pick = pl.BlockSpec((8, tr), lambda i: (0, i))
    return pl.pallas_call(
        _router_kernel,
        grid=(rows // tr,),
        in_specs=[pl.BlockSpec((ne, tr), lambda i: (0, i)),
                  pl.BlockSpec((ne, 1), lambda i: (0, 0)),
                  pl.BlockSpec((tr, tr), lambda i: (0, 0))],
        out_specs=[pick, pick, pick, pl.BlockSpec((ne, 1), lambda i: (0, 0))],
        out_shape=[jax.ShapeDtypeStruct((8, rows), jnp.int32),
                   jax.ShapeDtypeStruct((8, rows), jnp.int32),
                   jax.ShapeDtypeStruct((8, rows), F32),
                   jax.ShapeDtypeStruct((ne, 1), jnp.int32)],
        scratch_shapes=[pltpu.VMEM((ne, 1), F32)],
        compiler_params=_params("arbitrary"),
        name="router",
    )(logits_t, router_b.reshape(ne, 1).astype(F32), tri)


MOE_BLOCK = 512
MOE_TOKENS = 256
MOE_COMBINE_GROUPS = 4
HALF_D = D_MODEL // 2


def _pack_rows(lo, hi):
    lo_b = pltpu.bitcast(lo.astype(jnp.bfloat16).astype(F32), jnp.uint32)
    hi_b = pltpu.bitcast(hi.astype(jnp.bfloat16).astype(F32), jnp.uint32)
    return pltpu.bitcast((hi_b & jnp.uint32(0xFFFF0000)) | (lo_b >> 16), jnp.int32)


def _unpack_rows(words):
    u = pltpu.bitcast(words, jnp.uint32)
    lo = pltpu.bitcast(u << 16, F32)
    hi = pltpu.bitcast(u & jnp.uint32(0xFFFF0000), F32)
    return lo, hi


def _swiglu(x_bf16, wgu, wd):
    hgu = _dot(x_bf16, wgu)
    g = hgu[:, 0:EXPERT_HIDDEN]
    a = g * jax.nn.sigmoid(g) * hgu[:, EXPERT_HIDDEN:2 * EXPERT_HIDDEN]
    return _dot(a.astype(BF16), wd)


SC_CORES = 2
SC_SUBCORES = 16
SC_STREAM_ROWS = 128


def _sc_for_each_chunk(total, body):
    chunk = SC_STREAM_ROWS
    assert total % chunk == 0
    nchunk = total // chunk
    per_worker = pl.cdiv(nchunk, SC_CORES * SC_SUBCORES)
    first = (lax.axis_index("s") * SC_CORES + lax.axis_index("c")) * per_worker

    @pl.loop(0, per_worker)
    def _(j):
        @pl.when(first + j < nchunk)
        def _():
            body((first + j) * chunk)


def _sc_scatter_rows(rows, slots, n_out):
    total, n = rows.shape
    chunk = SC_STREAM_ROWS
    mesh = plsc.VectorSubcoreMesh(core_axis_name="c", subcore_axis_name="s")

    @functools.partial(
        pl.kernel, mesh=mesh,
        out_type=jax.ShapeDtypeStruct((n_out, n), jnp.int32),
        scratch_types=[pltpu.VMEM((8, chunk), jnp.int32),
                       pltpu.VMEM((chunk, n), jnp.int32),
                       pltpu.SemaphoreType.DMA],
        name="moe_dispatch_sc",
    )
    def scatter(rows_hbm, slot_hbm, out_hbm, idx_v, rows_v, sem):
        def body(off):
            pltpu.sync_copy(rows_hbm.at[pl.ds(off, chunk)], rows_v)
            pltpu.sync_copy(slot_hbm.at[:, pl.ds(off, chunk)], idx_v)
            for k in range(TOP_K):
                pltpu.async_copy(rows_v, out_hbm.at[idx_v.at[k]], sem).wait()

        _sc_for_each_chunk(total, body)

    return scatter(rows, slots)


def _experts_kernel(be_ref, nv_ref, nb_ref, xs_ref, wg_ref, wu_ref, wd_ref, ys_ref, wgu_bf, wd_bf):
    b = pl.program_id(0)

    @pl.when(b < nb_ref[0])
    def _():
        @pl.when((b == 0) | (be_ref[b] != be_ref[jnp.maximum(b - 1, 0)]))
        def _():
            wgu_bf[:, 0:EXPERT_HIDDEN] = wg_ref[0, 0].astype(BF16)
            wgu_bf[:, EXPERT_HIDDEN:2 * EXPERT_HIDDEN] = wu_ref[0, 0].astype(BF16)
            wd_bf[...] = wd_ref[0, 0].astype(BF16)

        words = xs_ref[...]
        live = lax.broadcasted_iota(jnp.int32, words.shape, 0) < nv_ref[b]
        lo, hi = _unpack_rows(jnp.where(live, words, 0))
        x = jnp.concatenate([lo, hi], axis=1).astype(BF16)
        y = _swiglu(x, wgu_bf[...], wd_bf[...])
        ys_ref[...] = _pack_rows(y[:, 0:HALF_D], y[:, HALF_D:D_MODEL])


def _sc_gather_rows(table, indices):
    m, n = indices.shape[0], table.shape[1]
    chunk = SC_STREAM_ROWS
    mesh = plsc.VectorSubcoreMesh(core_axis_name="c", subcore_axis_name="s")

    @functools.partial(
        pl.kernel, mesh=mesh,
        out_type=jax.ShapeDtypeStruct((m, n), jnp.int32),
        scratch_types=[pltpu.VMEM((chunk,), jnp.int32),
                       pltpu.VMEM((chunk, n), jnp.int32),
                       pltpu.SemaphoreType.DMA],
        name="moe_gather_sc",
    )
    def gather(table_hbm, idx_hbm, out_hbm, idx_v, rows_v, sem):
        def body(off):
            pltpu.sync_copy(idx_hbm.at[pl.ds(off, chunk)], idx_v)
            pltpu.async_copy(table_hbm.at[idx_v], rows_v, sem).wait()
            pltpu.sync_copy(rows_v, out_hbm.at[pl.ds(off, chunk)])

        _sc_for_each_chunk(m, body)

    return gather(table, indices)


def _combine_into_kernel(w_ref, rows_ref, h_ref, wsgu_ref, wsd_ref, x1_ref, g2_ref, fg_ref, prev_ref, o_ref, *, final):
    del prev_ref
    _combine_kernel(w_ref, rows_ref, h_ref, wsgu_ref, wsd_ref, x1_ref, g2_ref, fg_ref, o_ref, final=final)


def _combine_kernel(w_ref, rows_ref, h_ref, wsgu_ref, wsd_ref, x1_ref, g2_ref, fg_ref, o_ref, *, final):
    shared = _swiglu(h_ref[...], wsgu_ref[...], wsd_ref[...])
    acc_lo = shared[:, 0:HALF_D]
    acc_hi = shared[:, HALF_D:D_MODEL]
    w = w_ref[...]
    for k in range(TOP_K):
        lo, hi = _unpack_rows(rows_ref[k])
        acc_lo = acc_lo + w[:, k:k + 1] * lo
        acc_hi = acc_hi + w[:, k:k + 1] * hi
    x2 = x1_ref[...] + g2_ref[0] * jnp.concatenate([acc_lo, acc_hi], axis=1)
    if final:
        x2 = _rms(x2) * fg_ref[...]
    o_ref[...] = x2


def _moe(h2, h2p, picks, lp, x1, g2, final_g, *, rows_per_mod, final):
    idx, rank, wsel, counts = picks
    rows = h2.shape[0]
    tt = MOE_TOKENS
    blk = MOE_BLOCK
    assert rows % tt == 0 and rows_per_mod % tt == 0 and (rows * TOP_K) % blk == 0
    ntile = rows // tt
    nblock = rows * TOP_K // blk + N_EXPERTS

    cnt = counts.reshape(N_EXPERTS)
    padded = (cnt + blk - 1) // blk * blk
    e_ids = jnp.arange(N_EXPERTS, dtype=jnp.int32)
    pends = jnp.sum(jnp.where(e_ids[None, :] <= e_ids[:, None], padded[None, :], 0), axis=1)
    pstart = (pends - padded).astype(jnp.int32)
    nb_used = (jnp.sum(padded) // blk).astype(jnp.int32).reshape(1)
    first_row = jnp.arange(nblock, dtype=jnp.int32) * blk
    block_e = jnp.minimum(jnp.sum(pends[None, :] <= first_row[:, None], axis=1), N_EXPERTS - 1).astype(jnp.int32)
    slot = rank + jnp.sum(jnp.where(idx[:, :, None] == jnp.arange(N_EXPERTS, dtype=jnp.int32), pstart, 0), axis=-1)
    seg_end = jnp.sum(jnp.where(block_e[:, None] == e_ids[None, :], (cnt + pstart)[None, :], 0), axis=1)
    n_valid = jnp.clip(seg_end - first_row, 0, blk).astype(jnp.int32)

    xs = _sc_scatter_rows(h2p, slot, nblock * blk)

    def blk_map(b, be, nv, nb):
        return (jnp.minimum(b, nb[0] - 1), 0)

    layer = lp['layer']

    def w_map(b, be, nv, nb):
        return (layer, be[jnp.minimum(b, nb[0] - 1)], 0, 0)

    ys = pl.pallas_call(
        _experts_kernel,
        grid_spec=pltpu.PrefetchScalarGridSpec(
            num_scalar_prefetch=3,
            grid=(nblock,),
            in_specs=[pl.BlockSpec((blk, HALF_D), blk_map),
                      pl.BlockSpec((1, 1, D_MODEL, EXPERT_HIDDEN), w_map),
                      pl.BlockSpec((1, 1, D_MODEL, EXPERT_HIDDEN), w_map),
                      pl.BlockSpec((1, 1, EXPERT_HIDDEN, D_MODEL), w_map)],
            out_specs=pl.BlockSpec((blk, HALF_D), blk_map),
            scratch_shapes=[pltpu.VMEM((D_MODEL, 2 * EXPERT_HIDDEN), BF16),
                            pltpu.VMEM((EXPERT_HIDDEN, D_MODEL), BF16)]),
        out_shape=jax.ShapeDtypeStruct((nblock * blk, HALF_D), jnp.int32),
        compiler_params=_params("arbitrary"),
        name="moe_experts",
    )(block_e, n_valid, nb_used, xs, lp['exp_w_gate'], lp['exp_w_up'], lp['exp_w_down'])

    wsgu = jnp.concatenate([lp['sh_w_gate'], lp['sh_w_up']], axis=1).astype(BF16)
    wsd = lp['sh_w_down'].astype(BF16)
    ngroup = MOE_COMBINE_GROUPS if ntile % MOE_COMBINE_GROUPS == 0 and ntile >= 4 * MOE_COMBINE_GROUPS else 1
    gtile = ntile // ngroup
    grows = gtile * tt
    full = lambda *shape: pl.BlockSpec(shape, lambda i: (0,) * len(shape))
    wsel_t = wsel.T
    fg = final_g.reshape(1, D_MODEL).astype(F32)
    out = None
    for p in range(ngroup):
        gathered = _sc_gather_rows(ys, slot[0:TOP_K, p * grows:(p + 1) * grows].reshape(TOP_K * grows))
        gathered = gathered.reshape(TOP_K, grows, HALF_D)
        row = lambda width, p=p: pl.BlockSpec((tt, width), lambda i: (p * gtile + i, 0))
        in_specs = [row(8), pl.BlockSpec((TOP_K, tt, HALF_D), lambda i: (0, i, 0)), row(D_MODEL),
                    full(D_MODEL, 2 * EXPERT_HIDDEN), full(EXPERT_HIDDEN, D_MODEL), row(D_MODEL),
                    pl.BlockSpec((1, 1, D_MODEL), lambda i, p=p: (((p * gtile + i) * tt) // rows_per_mod, 0, 0)),
                    full(1, D_MODEL)]
        args = [wsel_t, gathered, h2, wsgu, wsd, x1, g2, fg]
        kern = functools.partial(_combine_kernel, final=final)
        aliases = {}
        if out is not None:
            in_specs.append(pl.BlockSpec(memory_space=pl.ANY))
            args.append(out)
            aliases = {len(args) - 1: 0}
            kern = functools.partial(_combine_into_kernel, final=final)
        out = pl.pallas_call(
            kern,
            grid=(gtile,),
            in_specs=in_specs,
            out_specs=row(D_MODEL),
            out_shape=jax.ShapeDtypeStruct((rows, D_MODEL), F32),
            input_output_aliases=aliases,
            compiler_params=_params("arbitrary"),
            name="moe_combine",
        )(*args)
    return out


def _reorder_w_in(w_in):
    split = 256 + 768 + 512 + 768
    return jnp.concatenate([w_in[:, split:], w_in[:, :split]], axis=1).astype(BF16)


def _mods(mod_row_block):
    return [mod_row_block[:, None, k * D_MODEL:(k + 1) * D_MODEL] for k in range(6)]


def _moe_block(h2, h2p, logits, lp, x1, g2, final_g, *, rows_per_mod, final):
    picks = _router(logits, lp['router_b'])
    return _moe(h2, h2p, picks, lp, x1, g2, final_g, rows_per_mod=rows_per_mod, final=final)


def _layer(x2d, xc2d, c16, lp, layer_idx, tables, final_g, *, nbatch, seq, n_ctx, with_ctx_out, final):
    lambda_init = 0.8 - 0.6 * math.exp(-0.3 * layer_idx)
    mod = _ada_mod(c16, lp['ada_w'].astype(F32), lp['ada_b'].astype(F32))
    sh1, sc1, g1, sh2, sc2, g2 = _mods(mod[0:nbatch])
    csh1, csc1, cg1, csh2, csc2, cg2 = _mods(mod[nbatch:nbatch + 1])
    w_in = _reorder_w_in(lp['w_in'])
    rows_lat = nbatch * seq
    rows_ctx = nbatch * n_ctx

    (gate, u, na_k, sw_k, na_qt, sw_qt, df_qt, na_vt, sw_vt, df_k4, df_vt4) = _inproj(
        x2d, lp['norm1_g'], sc1, sh1, w_in, tables, rows_per_mod=seq, rope=True, seq=seq)
    gate_c, u_c, na_c, sw_c, df_c = _inproj(xc2d, lp['norm1_g'], csc1, csh1, w_in, tables,
                                            rows_per_mod=rows_ctx, rope=False, seq=seq)

    win, wout, a_re, a_im = _s5_params(lp['s5_lambda_re'], lp['s5_lambda_im'], lp['s5_log_step'],
                                       lp['s5_b_re'], lp['s5_b_im'], lp['s5_c_re'], lp['s5_c_im'])
    u_tm = jnp.concatenate([u_c.reshape(nbatch, n_ctx, 256).transpose(1, 0, 2),
                            u.reshape(nbatch, seq, 256).transpose(1, 0, 2)], axis=0)
    y_tm = _s5_scan(u_tm, win, wout, a_re, a_im, n_ctx)
    y_bm = y_tm.transpose(0, 2, 1, 3)

    lqk = [lp[k].reshape(1, DIFF_QK_DIM).astype(F32) for k in ('diff_lq1', 'diff_lk1', 'diff_lq2', 'diff_lk2')]
    bias = _na_bias_table(lp['na_rpb'], seq // GRID_W)
    yb, yc = _local_attention(na_qt, na_vt, na_k, na_c, bias, sw_qt, sw_vt, sw_k, sw_c, lp['swa_sink'],
                              nbatch, seq, n_ctx)
    yd = _diff_attention(df_qt, df_k4, df_vt4, df_c, lqk, lp['diff_subln_g'], lambda_init, nbatch, seq, n_ctx)

    merge_w = (lp['s5_d'], lp['s5_glu_w'], lp['s5_glu_b'])
    x1, h2, h2p, logits = _merge(u, y_bm, 0, seq, *merge_w, yb, yc, yd, gate, lp['w_branch'], lp['w_out'], x2d, g1,
                                 lp['norm2_g'], sc2, sh2, lp['router_w'], rows_per_mod=seq)
    x_out = _moe_block(h2, h2p, logits, lp, x1, g2, final_g, rows_per_mod=seq, final=final)

    xc_out = None
    if with_ctx_out:
        yb_c, yc_c, yd_c = _ctx_attention(na_c, sw_c, df_c, lp['swa_sink'], lqk, lp['diff_subln_g'],
                                          lambda_init, nbatch, n_ctx)
        x1c, h2c, h2pc, logits_c = _merge(u_c, y_bm, seq, n_ctx, *merge_w, yb_c, yc_c, yd_c, gate_c,
                                          lp['w_branch'], lp['w_out'], xc2d, cg1, lp['norm2_g'], csc2, csh2,
                                          lp['router_w'], rows_per_mod=rows_ctx)
        xc_out = _moe_block(h2c, h2pc, logits_c, lp, x1c, cg2, final_g, rows_per_mod=rows_ctx, final=False)
    return x_out, xc_out


def kernel(x, c, ctx, c_ctx, ada_w, ada_b, norm1_g, norm2_g, w_in, s5_lambda_re, s5_lambda_im, s5_log_step,
           s5_b_re, s5_b_im, s5_c_re, s5_c_im, s5_d, s5_glu_w, s5_glu_b, na_rpb, swa_sink, diff_lq1, diff_lk1,
           diff_lq2, diff_lk2, diff_subln_g, w_branch, w_out, router_w, router_b, exp_w_gate, exp_w_up,
           exp_w_down, sh_w_gate, sh_w_up, sh_w_down, final_g):
    nbatch, seq, d = x.shape
    n_ctx = ctx.shape[1]
    depth = ada_w.shape[0]
    assert d == D_MODEL and nbatch == 8
    stacked = dict(ada_w=ada_w, ada_b=ada_b, norm1_g=norm1_g, norm2_g=norm2_g, w_in=w_in,
                   s5_lambda_re=s5_lambda_re, s5_lambda_im=s5_lambda_im, s5_log_step=s5_log_step,
                   s5_b_re=s5_b_re, s5_b_im=s5_b_im, s5_c_re=s5_c_re, s5_c_im=s5_c_im, s5_d=s5_d,
                   s5_glu_w=s5_glu_w, s5_glu_b=s5_glu_b, na_rpb=na_rpb, swa_sink=swa_sink,
                   diff_lq1=diff_lq1, diff_lk1=diff_lk1, diff_lq2=diff_lq2, diff_lk2=diff_lk2,
                   diff_subln_g=diff_subln_g, w_branch=w_branch, w_out=w_out, router_w=router_w,
                   router_b=router_b, exp_w_gate=exp_w_gate, exp_w_up=exp_w_up, exp_w_down=exp_w_down,
                   sh_w_gate=sh_w_gate, sh_w_up=sh_w_up, sh_w_down=sh_w_down)
    tables = _rope_tables(seq)
    c16 = jnp.concatenate([c.astype(F32), c_ctx.reshape(1, d).astype(F32),
                           jnp.zeros((16 - nbatch - 1, d), F32)], axis=0)
    x2d = x.reshape(nbatch * seq, d).astype(F32)
    xc2d = ctx.reshape(nbatch * n_ctx, d).astype(F32)
    for l in range(depth):
        routed = ('exp_w_gate', 'exp_w_up', 'exp_w_down')
        lp = {k: (v.astype(F32) if k in routed else v[l]) for k, v in stacked.items()}
        lp['layer'] = l
        last = l == depth - 1
        x2d, xc2d = _layer(x2d, xc2d, c16, lp, l, tables, final_g, nbatch=nbatch, seq=seq, n_ctx=n_ctx,
                           with_ctx_out=not last, final=last)
    return x2d.reshape(nbatch, seq, d)
```

```python
import functools
import math

import numpy as np
import jax
import jax.numpy as jnp
from jax import lax
from jax.experimental import pallas as pl
from jax.experimental.pallas import tpu as pltpu
from jax.experimental.pallas import tpu_sc as plsc

F32 = jnp.float32
BF16 = jnp.bfloat16
HIGHEST = lax.Precision.HIGHEST

GRID_W = 64
EPS = 1e-6
NEG_INF = -1e30
ROPE_BASE = 10000.0
D_MODEL = 1024
BRANCH_WIDTH = 256
HEAD_DIM = 64
S5_GROUP = 16
S5_GROUPS = 16
S5_STATE = 64
S5_FLAT = S5_GROUPS * S5_STATE
NA_HEADS = 4
NA_WIN_ROWS = 8
NA_WIN_COLS = 16
SWA_KV_HEADS = 2
SWA_WINDOW = 128
DIFF_HEADS = 4
DIFF_QK_DIM = 32
N_EXPERTS = 64
N_EXPERT_GROUPS = 8
TOPK_GROUPS = 4
TOP_K = 6
EXPERT_HIDDEN = 256
ROUTED_SCALE = 2.5
GATE_WIDTH = 4 * D_MODEL

VMEM_LIMIT = 56 * 1024 * 1024


def _params(*sem):
    return pltpu.CompilerParams(dimension_semantics=sem, vmem_limit_bytes=VMEM_LIMIT)


def _nt_dot(a, b):
    return lax.dot_general(a, b, (((1,), (1,)), ((), ())), preferred_element_type=F32)


def _dot(a, b):
    return jnp.dot(a, b, preferred_element_type=F32)


def _rms(x):
    return x * lax.rsqrt(jnp.mean(x * x, axis=-1, keepdims=True) + EPS)


def _ada_kernel(c_ref, w_ref, b_ref, o_ref):
    c = c_ref[...]
    s = c * jax.nn.sigmoid(c)
    o_ref[...] = jnp.dot(s, w_ref[...], preferred_element_type=F32, precision=HIGHEST) + b_ref[...]


def _ada_mod(cc, w, b):
    rows, d = cc.shape
    width = w.shape[1]
    tn = 1536
    return pl.pallas_call(
        _ada_kernel,
        grid=(width // tn,),
        in_specs=[pl.BlockSpec((rows, d), lambda j: (0, 0)),
                  pl.BlockSpec((d, tn), lambda j: (0, j)),
                  pl.BlockSpec((1, tn), lambda j: (0, j))],
        out_specs=pl.BlockSpec((rows, tn), lambda j: (0, j)),
        out_shape=jax.ShapeDtypeStruct((rows, width), F32),
        compiler_params=_params("arbitrary"),
        name="ada_mod",
    )(cc, w, b.reshape(1, width))


_C_GATE = 0
_C_U = GATE_WIDTH
_C_NA = _C_U + 256
_C_SW = _C_NA + 768
_C_DF = _C_SW + 512
_C_END = _C_DF + 768


def _rope_apply(x, cos, sins, half):
    outs = []
    for j in range(x.shape[1] // 128):
        xs = x[:, j * 128:(j + 1) * 128]
        lane = lax.broadcasted_iota(jnp.int32, xs.shape, 1)
        lo = (lane % (2 * half)) < half
        partner = jnp.where(lo, pltpu.roll(xs, 128 - half, 1), pltpu.roll(xs, half, 1))
        outs.append(xs * cos + partner * sins)
    return outs[0] if len(outs) == 1 else jnp.concatenate(outs, axis=1)


def _inproj_kernel(x_ref, g_ref, sc_ref, sh_ref, w_ref, c64_ref, s64_ref, c32_ref, s32_ref,
                   gate_o, u_o, na_o, sw_o, df_o, *qt_os, rope):
    h = _rms(x_ref[...]) * g_ref[...]
    h = h * (1.0 + sc_ref[0]) + sh_ref[0]
    hb = h.astype(BF16)

    def mm(c0, c1):
        return _dot(hb, w_ref[:, c0:c1])

    for k in range(GATE_WIDTH // 512):
        gate_o[:, k * 512:(k + 1) * 512] = jax.nn.sigmoid(mm(k * 512, (k + 1) * 512)).astype(BF16)
    u_o[...] = mm(_C_U, _C_U + 256)

    na = mm(_C_NA, _C_NA + 768)
    naq = na[:, 0:256] * (HEAD_DIM ** -0.5 * (LOG2E if rope else 1.0))
    na_o[:, 0:256] = naq.astype(BF16)
    na_o[:, 256:768] = na[:, 256:768].astype(BF16)

    sw = mm(_C_SW, _C_SW + 512)
    swq, swk = sw[:, 0:256], sw[:, 256:384]
    if rope:
        swq = _rope_apply(swq, c64_ref[...], s64_ref[...], 16)
        swk = _rope_apply(swk, c64_ref[...], s64_ref[...], 16)
    swq = swq * (HEAD_DIM ** -0.5 * (LOG2E if rope else 1.0))
    sw_o[:, 0:256] = swq.astype(BF16)
    sw_o[:, 256:384] = swk.astype(BF16)
    sw_o[:, 384:512] = sw[:, 384:512].astype(BF16)

    df = mm(_C_DF, _C_DF + 768)
    dfq, dfk = df[:, 0:256], df[:, 256:512]
    if rope:
        dfq = _rope_apply(dfq, c32_ref[...], s32_ref[...], 8)
        dfk = _rope_apply(dfk, c32_ref[...], s32_ref[...], 8)
    dfq = dfq * (DIFF_QK_DIM ** -0.5 * (LOG2E if rope else 1.0))
    df_o[:, 0:256] = dfq.astype(BF16)
    df_o[:, 256:512] = dfk.astype(BF16)
    df_o[:, 512:768] = df[:, 512:768].astype(BF16)
    if qt_os:
        na_qt_o, sw_qt_o, df_qt_o, na_vt_o, sw_vt_o, df_k_o, df_vt_o = qt_os
        for qt_o, q in ((na_qt_o, naq), (sw_qt_o, swq), (df_qt_o, dfq)):
            qt_o[0] = q.T.astype(BF16)
        na_vt_o[0] = _ones_row_vt(na[:, 512:768].T)
        sw_vt_o[0] = _ones_row_vt(sw[:, 384:512].T)
        df_vt = _ones_row_vt(df[:, 512:768].T)
        for j in range(df_k_o.shape[1]):
            df_k_o[0, j] = dfk[j * DIFF_CK:(j + 1) * DIFF_CK].astype(BF16)
            df_vt_o[0, j] = df_vt[:, j * DIFF_CK:(j + 1) * DIFF_CK]


def _ones_row_vt(vt):
    n = vt.shape[1]
    pad = jnp.where(lax.broadcasted_iota(jnp.int32, (VROWS - HEAD_DIM, n), 0) == 0, 1.0, 0.0)
    parts = []
    for h in range(vt.shape[0] // HEAD_DIM):
        parts += [vt[h * HEAD_DIM:(h + 1) * HEAD_DIM], pad]
    return jnp.concatenate(parts, axis=0).astype(BF16)


def _inproj(x2d, norm_g, sc, sh, w_bf16, tables, *, rows_per_mod, rope, seq):
    rows = x2d.shape[0]
    tm = 512
    assert rows % tm == 0 and rows_per_mod % tm == 0 and seq % tm == 0
    tiles_per_seq = seq // tm

    def mod_map(i):
        return ((i * tm) // rows_per_mod, 0, 0)

    def tab_map(i):
        return (i % tiles_per_seq, 0)

    tab_spec = pl.BlockSpec((tm, 128), tab_map)
    row = lambda w: pl.BlockSpec((tm, w), lambda i: (i, 0))
    out_specs = [row(GATE_WIDTH), row(256), row(768), row(512), row(768)]
    out_shape = [jax.ShapeDtypeStruct((rows, GATE_WIDTH), BF16),
                 jax.ShapeDtypeStruct((rows, 256), F32),
                 jax.ShapeDtypeStruct((rows, 768), BF16),
                 jax.ShapeDtypeStruct((rows, 512), BF16),
                 jax.ShapeDtypeStruct((rows, 768), BF16)]
    if rope:
        for nrow in (256, 256, 256, NA_HEADS * VROWS, SWA_KV_HEADS * VROWS):
            out_specs.append(pl.BlockSpec((1, nrow, tm), lambda i: (i // tiles_per_seq, 0, i % tiles_per_seq)))
            out_shape.append(jax.ShapeDtypeStruct((rows // seq, nrow, seq), BF16))
        assert tm % DIFF_CK == 0
        for shape in ((DIFF_CK, 256), (DIFF_HEADS * VROWS, DIFF_CK)):
            out_specs.append(pl.BlockSpec((1, tm // DIFF_CK) + shape,
                                          lambda i: (i // tiles_per_seq, i % tiles_per_seq, 0, 0)))
            out_shape.append(jax.ShapeDtypeStruct((rows // seq, seq // DIFF_CK) + shape, BF16))
    return pl.pallas_call(
        functools.partial(_inproj_kernel, rope=rope),
        grid=(rows // tm,),
        in_specs=[row(D_MODEL),
                  pl.BlockSpec((1, D_MODEL), lambda i: (0, 0)),
                  pl.BlockSpec((1, 1, D_MODEL), mod_map),
                  pl.BlockSpec((1, 1, D_MODEL), mod_map),
                  pl.BlockSpec((D_MODEL, _C_END), lambda i: (0, 0)),
                  tab_spec, tab_spec, tab_spec, tab_spec],
        out_specs=out_specs,
        out_shape=out_shape,
        compiler_params=_params("arbitrary"),
        name="inproj",
    )(x2d, norm_g.reshape(1, D_MODEL), sc, sh, w_bf16, *tables)


def _rope_tables(seq):
    t = jnp.arange(seq)
    rows = (t // GRID_W).astype(F32)
    cols = (t % GRID_W).astype(F32)
    lane = np.arange(128)
    out = []
    for dim in (64, 32):
        quarter = dim // 4
        inv_freq = ROPE_BASE ** (-jnp.arange(quarter, dtype=F32) / quarter)
        l = lane % dim
        use_col = l >= dim // 2
        fidx = l % quarter
        hi = (l % (dim // 2)) >= quarter
        ang_r = rows[:, None] * inv_freq[None, :]
        ang_c = cols[:, None] * inv_freq[None, :]
        ang = jnp.where(use_col[None, :], ang_c[:, fidx], ang_r[:, fidx])
        out.append(jnp.cos(ang))
        out.append(jnp.where(hi[None, :], jnp.sin(ang), -jnp.sin(ang)))
    return tuple(out)


S5_CHUNK = 256


def _s5_kernel(u_ref, win_ref, wout_ref, are_ref, aim_ref, y_ref, bu_ref, st_ref, *, tc, nb):
    d = pl.program_id(0)
    i = pl.program_id(1)

    @pl.when(i == 0)
    def _():
        st_ref[...] = jnp.zeros_like(st_ref)

    u = u_ref[...].reshape(tc * nb, BRANCH_WIDTH).astype(BF16)
    bu_ref[...] = _dot(u, win_ref[0])
    ar = jnp.broadcast_to(are_ref[0], (nb, S5_FLAT))
    ai = jnp.broadcast_to(aim_ref[0], (nb, S5_FLAT))

    def body(j, carry):
        xr, xi = carry
        t = j + d * (tc - 1 - 2 * j)
        row = pl.multiple_of(t * nb, nb)
        br = bu_ref[pl.ds(row, nb), 0:S5_FLAT]
        bi = bu_ref[pl.ds(row, nb), S5_FLAT:2 * S5_FLAT]
        nr = ar * xr - ai * xi + br
        ni = ar * xi + ai * xr + bi
        bu_ref[pl.ds(row, nb), 0:S5_FLAT] = nr
        bu_ref[pl.ds(row, nb), S5_FLAT:2 * S5_FLAT] = ni
        return nr, ni

    xr, xi = lax.fori_loop(0, tc, body, (st_ref[:, 0:S5_FLAT], st_ref[:, S5_FLAT:2 * S5_FLAT]), unroll=4)
    st_ref[:, 0:S5_FLAT] = xr
    st_ref[:, S5_FLAT:2 * S5_FLAT] = xi
    y = _dot(bu_ref[...].astype(BF16), wout_ref[0])
    y_ref[0] = y.reshape(tc, nb, BRANCH_WIDTH)


def _s5_scan(u_tm, win, wout, a_re, a_im, n_ctx):
    s_len, nb, _ = u_tm.shape
    tc = S5_CHUNK
    assert nb == 8 and s_len % tc == 0 and n_ctx % tc == 0
    nct = n_ctx // tc
    nlt = (s_len - n_ctx) // tc

    def chunk(d, i):
        rev = jnp.where(i < nct, nct - 1 - i, 2 * nct + nlt - 1 - i)
        return jnp.where(d == 0, i, rev)

    def out_chunk(d, i):
        c = chunk(d, i)
        return jnp.where(c < nct, nlt + c, c - nct)

    return pl.pallas_call(
        functools.partial(_s5_kernel, tc=tc, nb=nb),
        grid=(2, nct + nlt),
        in_specs=[pl.BlockSpec((tc, nb, BRANCH_WIDTH), lambda d, i: (chunk(d, i), 0, 0)),
                  pl.BlockSpec((1, BRANCH_WIDTH, 2 * S5_FLAT), lambda d, i: (d, 0, 0)),
                  pl.BlockSpec((1, 2 * S5_FLAT, BRANCH_WIDTH), lambda d, i: (d, 0, 0)),
                  pl.BlockSpec((1, 1, S5_FLAT), lambda d, i: (d, 0, 0)),
                  pl.BlockSpec((1, 1, S5_FLAT), lambda d, i: (d, 0, 0))],
        out_specs=pl.BlockSpec((1, tc, nb, BRANCH_WIDTH), lambda d, i: (d, out_chunk(d, i), 0, 0)),
        out_shape=jax.ShapeDtypeStruct((2, s_len, nb, BRANCH_WIDTH), F32),
        scratch_shapes=[pltpu.VMEM((tc * nb, 2 * S5_FLAT), F32),
                        pltpu.VMEM((nb, 2 * S5_FLAT), F32)],
        compiler_params=_params("arbitrary", "arbitrary"),
        name="s5_scan",
    )(u_tm, win, wout, a_re, a_im)


def _s5_params(lam_re, lam_im, log_step, b_re, b_im, c_re, c_im):
    lr = lam_re.astype(F32)
    li = lam_im.astype(F32)
    dt = jnp.exp(log_step.astype(F32))[..., None]
    mag = jnp.exp(lr * dt)
    a_re = mag * jnp.cos(li * dt)
    a_im = mag * jnp.sin(li * dt)
    nr, ni, den = a_re - 1.0, a_im, lr * lr + li * li
    k_re = ((nr * lr + ni * li) / den)[..., None]
    k_im = ((ni * lr - nr * li) / den)[..., None]
    br = b_re.astype(F32)
    bi = b_im.astype(F32)
    bb_re = k_re * br - k_im * bi
    bb_im = k_re * bi + k_im * br
    eye = jnp.eye(S5_GROUPS, dtype=F32)

    def blockdiag_in(bb):
        m = jnp.einsum('dgpc,gh->dgchp', bb, eye)
        return m.reshape(2, S5_GROUPS * S5_GROUP, S5_GROUPS * S5_STATE)

    def blockdiag_out(cc):
        m = jnp.einsum('dgcp,gh->dgphc', cc, eye)
        return m.reshape(2, S5_GROUPS * S5_STATE, S5_GROUPS * S5_GROUP)

    win = jnp.concatenate([blockdiag_in(bb_re), blockdiag_in(bb_im)], axis=2).astype(BF16)
    wout = jnp.concatenate([blockdiag_out(c_re.astype(F32)), -blockdiag_out(c_im.astype(F32))], axis=1).astype(BF16)
    return win, wout, a_re.reshape(2, 1, S5_FLAT), a_im.reshape(2, 1, S5_FLAT)


def _softmax_parts(scores, extra=None):
    m = scores[0].max(axis=-1, keepdims=True)
    for s in scores[1:]:
        m = jnp.maximum(m, s.max(axis=-1, keepdims=True))
    if extra is not None:
        m = jnp.maximum(m, extra)
    ps = [jnp.exp(s - m) for s in scores]
    l = ps[0].sum(axis=-1, keepdims=True)
    for p in ps[1:]:
        l = l + p.sum(axis=-1, keepdims=True)
    if extra is not None:
        l = l + jnp.exp(extra - m)
    return ps, l


NA_QROWS = 2
NA_KROWS = 10
VROWS = HEAD_DIM + 16


def _na_window_start(r, grid_rows):
    start = jnp.clip(r - NA_WIN_ROWS // 2, 0, grid_rows - NA_WIN_ROWS)
    return (jnp.minimum(start, grid_rows - NA_KROWS) // 2) * 2


def _head_blockdiag(qt, qbd_ref, nheads, rows_per_head):
    n = qt.shape[1]
    row_h = lax.broadcasted_iota(jnp.int32, qt.shape, 0) // rows_per_head
    zero = jnp.zeros_like(qt)
    for h in range(nheads):
        qbd_ref[:, h * n:(h + 1) * n] = jnp.where(row_h == h, qt, zero)


def _na_scores(qt, k_ref, kc_ref, bias, qbd_ref, p, *, grid_rows):
    nk = NA_KROWS * GRID_W
    off = pl.multiple_of(_na_window_start(NA_QROWS * p, grid_rows) * GRID_W, 128)
    _head_blockdiag(qt, qbd_ref, NA_HEADS, HEAD_DIM)
    qbd = qbd_ref[...]
    s_loc = _dot(k_ref[0, pl.ds(off, nk), :], qbd) + bias
    s_ctx = _dot(kc_ref[0], qbd)
    return off, s_loc, s_ctx


def _na_values(off, s_loc, s_ctx, vt_ref, vct_ref):
    nq = NA_QROWS * GRID_W
    m = jnp.maximum(s_loc.max(axis=0, keepdims=True), s_ctx.max(axis=0, keepdims=True))
    p_loc = jnp.exp2((s_loc - m).astype(BF16))
    p_ctx = jnp.exp2((s_ctx - m).astype(BF16))
    vw = vt_ref[0, :, pl.ds(off, NA_KROWS * GRID_W)]
    outs = []
    for h in range(NA_HEADS):
        rows = slice(h * VROWS, (h + 1) * VROWS)
        cols = slice(h * nq, (h + 1) * nq)
        o = _dot(vw[rows], p_loc[:, cols]) + _dot(vct_ref[0, rows, :], p_ctx[:, cols])
        outs.append(o[0:HEAD_DIM] / o[HEAD_DIM:HEAD_DIM + 1])
    return jnp.concatenate(outs, axis=0).T.astype(BF16)


def _na_classes(grid_rows):
    return [0, 2, 4, grid_rows - 4, grid_rows - 2]


def _na_bias_table(rpb, grid_rows):
    col = np.arange(GRID_W)
    cstart = np.clip(col - NA_WIN_COLS // 2, 0, GRID_W - NA_WIN_COLS)
    col_in = (col[None, :] >= cstart[:, None]) & (col[None, :] < cstart[:, None] + NA_WIN_COLS)
    cb = np.clip(col[None, :] - col[:, None] + (NA_WIN_COLS - 1), 0, 2 * NA_WIN_COLS - 2)
    classes = _na_classes(grid_rows)
    rbi = np.zeros((len(classes), NA_QROWS, NA_KROWS), np.int64)
    row_in = np.zeros((len(classes), NA_QROWS, NA_KROWS), bool)
    for c, r in enumerate(classes):
        a_row = (min(int(np.clip(r - NA_WIN_ROWS // 2, 0, grid_rows - NA_WIN_ROWS)), grid_rows - NA_KROWS) // 2) * 2
        for qi in range(NA_QROWS):
            start_q = int(np.clip(r + qi - NA_WIN_ROWS // 2, 0, grid_rows - NA_WIN_ROWS))
            for j in range(NA_KROWS):
                row_in[c, qi, j] = start_q <= a_row + j < start_q + NA_WIN_ROWS
                rbi[c, qi, j] = np.clip(a_row + j - (r + qi) + NA_WIN_ROWS - 1, 0, 2 * NA_WIN_ROWS - 2)
    oh_row = jnp.asarray(rbi[..., None] == np.arange(2 * NA_WIN_ROWS - 1), F32)
    oh_col = jnp.asarray(cb[:, :, None] == np.arange(2 * NA_WIN_COLS - 1), F32)
    t = jnp.einsum('hab,cija,qkb->cjkhiq', rpb.astype(F32), oh_row, oh_col, precision=HIGHEST) * LOG2E
    valid = row_in.transpose(0, 2, 1)[:, :, None, None, :, None] & col_in.T[None, None, :, None, None, :]
    t = jnp.where(valid, t, NEG_INF)
    return t.reshape(len(classes), NA_KROWS * GRID_W, NA_HEADS * NA_QROWS * GRID_W)


def _augment_vt(v, nheads):
    nb, s, _ = v.shape
    v = v.reshape(nb, s, nheads, HEAD_DIM)
    pad = jnp.zeros((nb, s, nheads, VROWS - HEAD_DIM), v.dtype).at[..., 0].set(1.0)
    return jnp.concatenate([v, pad], axis=-1).reshape(nb, s, nheads * VROWS).transpose(0, 2, 1)


SWA_BLOCK = 128


def _swa_scores(qt, k_ref, kc_ref, qbd_ref, n, *, seq):
    band = 3 * SWA_BLOCK
    nq = SWA_BLOCK
    nqh = 2 * SWA_KV_HEADS
    bstart = pl.multiple_of(jnp.clip((n - 1) * SWA_BLOCK, 0, seq - band), SWA_BLOCK)
    zero = jnp.zeros((HEAD_DIM, nq), qt.dtype)
    for hq in range(nqh):
        qh = qt[hq * HEAD_DIM:(hq + 1) * HEAD_DIM]
        qbd_ref[:, hq * nq:(hq + 1) * nq] = jnp.concatenate([qh, zero] if hq // 2 == 0 else [zero, qh], axis=0)
    qbd = qbd_ref[...]
    kpos = bstart + lax.broadcasted_iota(jnp.int32, (band, nqh * nq), 0)
    qpos = n * SWA_BLOCK + lax.broadcasted_iota(jnp.int32, (band, nqh * nq), 1) % nq
    s_loc = jnp.where(jnp.abs(qpos - kpos) <= SWA_WINDOW, _dot(k_ref[0, pl.ds(bstart, band), :], qbd), NEG_INF)
    s_ctx = _dot(kc_ref[0], qbd)
    return bstart, s_loc, s_ctx


def _swa_values(bstart, s_loc, s_ctx, vt_ref, vct_ref, sink_ref):
    nq = SWA_BLOCK
    nqh = 2 * SWA_KV_HEADS
    sink = jnp.concatenate([jnp.broadcast_to(sink_ref[0:1, hq:hq + 1] * LOG2E, (1, nq)) for hq in range(nqh)], axis=1)
    m = jnp.maximum(jnp.maximum(s_loc.max(axis=0, keepdims=True), s_ctx.max(axis=0, keepdims=True)), sink)
    p_loc = jnp.exp2((s_loc - m).astype(BF16))
    p_ctx = jnp.exp2((s_ctx - m).astype(BF16))
    p_sink = jnp.exp2(sink - m)
    vw = vt_ref[0, :, pl.ds(bstart, 3 * SWA_BLOCK)]
    outs = []
    for hq in range(nqh):
        rows = slice((hq // 2) * VROWS, (hq // 2 + 1) * VROWS)
        cols = slice(hq * nq, (hq + 1) * nq)
        o = _dot(vw[rows], p_loc[:, cols]) + _dot(vct_ref[0, rows, :], p_ctx[:, cols])
        outs.append(o[0:HEAD_DIM] / (o[HEAD_DIM:HEAD_DIM + 1] + p_sink[:, cols]))
    return jnp.concatenate(outs, axis=0).T.astype(BF16)


LOCAL_BLOCKS = 4


def _local_attn_kernel(na_qt, na_k, na_vt, na_kc, na_vct, sw_qt, sw_k, sw_vt, sw_kc, sw_vct, sink_ref, *rest,
                       grid_rows, seq):
    bias_refs = rest[0:LOCAL_BLOCKS]
    na_o, sw_o, na_qbd, sw_qbd = rest[LOCAL_BLOCKS:]
    nq = SWA_BLOCK
    scores = []
    for j in range(LOCAL_BLOCKS):
        p = pl.program_id(1) * LOCAL_BLOCKS + j
        cols = slice(j * nq, (j + 1) * nq)
        scores.append((_na_scores(na_qt[0, :, cols], na_k, na_kc, bias_refs[j][0], na_qbd.at[j], p, grid_rows=grid_rows),
                       _swa_scores(sw_qt[0, :, cols], sw_k, sw_kc, sw_qbd.at[j], p, seq=seq)))
    for j, (na_s, sw_s) in enumerate(scores):
        na_o[j * nq:(j + 1) * nq, :] = _na_values(*na_s, na_vt, na_vct)
        sw_o[j * nq:(j + 1) * nq, :] = _swa_values(*sw_s, sw_vt, sw_vct, sink_ref)


def _local_attention(na_qt, na_vt, na, na_c, bias, sw_qt, sw_vt, sw, sw_c, sink, nbatch, seq, n_ctx):
    grid_rows = seq // GRID_W
    nq = NA_QROWS * GRID_W
    nstep = seq // nq
    nqh = 2 * SWA_KV_HEADS
    assert grid_rows >= NA_KROWS and grid_rows % NA_QROWS == 0 and nq == SWA_BLOCK and seq >= 3 * SWA_BLOCK
    nal = na.reshape(nbatch, seq, 768)
    nac = na_c.reshape(nbatch, n_ctx, 768)
    na_vct = _augment_vt(nac[:, :, 512:768], NA_HEADS)
    sink_pad = jnp.zeros((1, 128), F32).at[0, 0:nqh].set(sink.astype(F32))
    swl = sw.reshape(nbatch, seq, 512)
    swc = sw_c.reshape(nbatch, n_ctx, 512)
    sw_vct = _augment_vt(swc[:, :, 384:512], SWA_KV_HEADS)

    nblk = LOCAL_BLOCKS
    assert nstep % nblk == 0

    def cls_map(j):
        def cls(b, s):
            r = NA_QROWS * (s * nblk + j)
            c = jnp.where(r < 4, r // 2, jnp.where(r >= grid_rows - 4, (r - (grid_rows - 4)) // 2 + 3, 2))
            return (c, 0, 0)
        return cls

    out = jax.ShapeDtypeStruct((nbatch * seq, 256), BF16)
    out_spec = pl.BlockSpec((nblk * nq, 256), lambda b, s: (b * (nstep // nblk) + s, 0))
    bias_specs = [pl.BlockSpec((1, NA_KROWS * GRID_W, NA_HEADS * nq), cls_map(j)) for j in range(nblk)]
    return pl.pallas_call(
        functools.partial(_local_attn_kernel, grid_rows=grid_rows, seq=seq),
        grid=(nbatch, nstep // nblk),
        in_specs=[pl.BlockSpec((1, 256, nblk * nq), lambda b, s: (b, 0, s)),
                  pl.BlockSpec((1, seq, 256), lambda b, s: (b, 0, 1)),
                  pl.BlockSpec((1, NA_HEADS * VROWS, seq), lambda b, s: (b, 0, 0)),
                  pl.BlockSpec((1, n_ctx, 256), lambda b, s: (b, 0, 1)),
                  pl.BlockSpec((1, NA_HEADS * VROWS, n_ctx), lambda b, s: (b, 0, 0)),
                  pl.BlockSpec((1, 256, nblk * nq), lambda b, s: (b, 0, s)),
                  pl.BlockSpec((1, seq, 128), lambda b, s: (b, 0, 2)),
                  pl.BlockSpec((1, SWA_KV_HEADS * VROWS, seq), lambda b, s: (b, 0, 0)),
                  pl.BlockSpec((1, n_ctx, 128), lambda b, s: (b, 0, 2)),
                  pl.BlockSpec((1, SWA_KV_HEADS * VROWS, n_ctx), lambda b, s: (b, 0, 0)),
                  pl.BlockSpec((1, 128), lambda b, s: (0, 0))] + bias_specs,
        out_specs=[out_spec, out_spec],
        out_shape=[out, out],
        scratch_shapes=[pltpu.VMEM((nblk, 256, NA_HEADS * nq), BF16),
                        pltpu.VMEM((nblk, SWA_KV_HEADS * HEAD_DIM, nqh * nq), BF16)],
        compiler_params=_params("arbitrary", "arbitrary"),
        name="local_attention",
    )(na_qt, nal, na_vt, nac, na_vct, sw_qt, swl, sw_vt, swc, sw_vct, sink_pad, *([bias] * nblk))


DIFF_TQ = 512
DIFF_CK = 256
LOG2E = math.log2(math.e)


def _diff_lambda(lq1_ref, lk1_ref, lq2_ref, lk2_ref, lambda_init):
    s1 = jnp.sum(lq1_ref[...] * lk1_ref[...], axis=-1, keepdims=True)
    s2 = jnp.sum(lq2_ref[...] * lk2_ref[...], axis=-1, keepdims=True)
    return jnp.exp(s1) - jnp.exp(s2) + lambda_init


def _stack_maps(qh):
    lane = lax.broadcasted_iota(jnp.int32, qh.shape, 1)
    zero = jnp.zeros_like(qh)
    return jnp.concatenate([jnp.where(lane < DIFF_QK_DIM, qh, zero),
                            jnp.where(lane >= DIFF_QK_DIM, qh, zero)], axis=0)


def _subln(o0, o1, lam, g, lambda_init):
    o = o0 - lam * o1
    return _rms(o) * g * (1.0 - lambda_init)


def _diff_kernel(qt_ref, k_ref, vt_ref, kc_ref, vct_ref, lq1_ref, lk1_ref, lq2_ref, lk2_ref, g_ref, o_ref,
                 qbd_ref, acc_ref, s_ref, *, nlat, lambda_init):
    lam = _diff_lambda(lq1_ref, lk1_ref, lq2_ref, lk2_ref, lambda_init)
    qt = qt_ref[0]
    tq = qt.shape[1]
    w = 2 * tq
    row = lax.broadcasted_iota(jnp.int32, qt.shape, 0) // DIFF_QK_DIM
    zero = jnp.zeros_like(qt)
    for j in range(2 * DIFF_HEADS):
        qbd_ref[:, j * tq:(j + 1) * tq] = jnp.where(row == j, qt, zero)
    acc_ref[...] = jnp.zeros_like(acc_ref)

    def is_ctx(c):
        return isinstance(c, int) and c == nlat

    def scores(slot, c, h):
        kblk = kc_ref[0] if is_ctx(c) else k_ref[0, c]
        s = _dot(kblk, qbd_ref[:, h * w:(h + 1) * w])
        s_ref[slot, h] = s
        return s.max(axis=0, keepdims=True)

    def softmax_pv(slot, c, h, m_run, m_chunk):
        m_new = jnp.maximum(m_run, m_chunk)
        alpha = jnp.exp2(m_run - m_new)
        p = jnp.exp2((s_ref[slot, h] - m_new).astype(BF16))
        rows = slice(h * VROWS, (h + 1) * VROWS)
        vblk = vct_ref[0, rows, :] if is_ctx(c) else vt_ref[0, c, rows, :]
        acc_ref[h] = alpha * acc_ref[h] + _dot(vblk, p)
        return m_new

    def step(slot, c, carry, last=False):
        m_run, m_chunk = carry
        new_run, new_chunk = [], []
        for h in range(DIFF_HEADS):
            if not last:
                new_chunk.append(scores(1 - slot, c + 1, h))
            new_run.append(softmax_pv(slot, c, h, m_run[h], m_chunk[h]))
        return tuple(new_run), tuple(new_chunk)

    def body(i, carry):
        c = 2 * i
        return step(1, c + 1, step(0, c, carry))

    carry = (tuple(jnp.full((1, w), NEG_INF, F32) for _ in range(DIFF_HEADS)),
             tuple(scores(0, 0, h) for h in range(DIFF_HEADS)))
    nloop = (nlat - 1) // 2
    carry = lax.fori_loop(0, nloop, body, carry)
    for c in range(2 * nloop, nlat + 1):
        carry = step(c % 2, c, carry, last=c == nlat)
    outs = []
    for h in range(DIFF_HEADS):
        o = acc_ref[h, 0:HEAD_DIM, :] / acc_ref[h, HEAD_DIM:HEAD_DIM + 1, :]
        d = o[:, 0:tq] - lam * o[:, tq:w]
        d = d * lax.rsqrt(jnp.mean(d * d, axis=0, keepdims=True) + EPS)
        outs.append(d * g_ref[...] * (1.0 - lambda_init))
    o_ref[...] = jnp.concatenate(outs, axis=0).T.astype(BF16)


def _diff_attention(qt, k4, vt4, df_c, lqk, subln_g, lambda_init, nbatch, seq, n_ctx):
    tq, ck = min(DIFF_TQ, seq), DIFF_CK
    assert seq % ck == 0 and seq % tq == 0 and n_ctx == ck
    nlat = seq // ck
    nq = seq // tq
    dfc = df_c.reshape(nbatch, n_ctx, 768)
    vct = _augment_vt(dfc[:, :, 512:768], DIFF_HEADS)
    vec = pl.BlockSpec((1, DIFF_QK_DIM), lambda b, n: (0, 0))
    return pl.pallas_call(
        functools.partial(_diff_kernel, nlat=nlat, lambda_init=lambda_init),
        grid=(nbatch, nq),
        in_specs=[pl.BlockSpec((1, 256, tq), lambda b, n: (b, 0, n)),
                  pl.BlockSpec((1, nlat, ck, 256), lambda b, n: (b, 0, 0, 0)),
                  pl.BlockSpec((1, nlat, DIFF_HEADS * VROWS, ck), lambda b, n: (b, 0, 0, 0)),
                  pl.BlockSpec((1, n_ctx, 256), lambda b, n: (b, 0, 1)),
                  pl.BlockSpec((1, DIFF_HEADS * VROWS, n_ctx), lambda b, n: (b, 0, 0)),
                  vec, vec, vec, vec,
                  pl.BlockSpec((HEAD_DIM, 1), lambda b, n: (0, 0))],
        out_specs=pl.BlockSpec((tq, 256), lambda b, n: (b * nq + n, 0)),
        out_shape=jax.ShapeDtypeStruct((nbatch * seq, 256), BF16),
        scratch_shapes=[pltpu.VMEM((256, 2 * DIFF_HEADS * tq), BF16),
                        pltpu.VMEM((DIFF_HEADS, VROWS, 2 * tq), F32),
                        pltpu.VMEM((2, DIFF_HEADS, ck, 2 * tq), F32)],
        compiler_params=_params("arbitrary", "arbitrary"),
        name="diff_attention",
    )(qt, k4, vt4, dfc, vct, *lqk, subln_g.reshape(HEAD_DIM, 1).astype(F32))


def _ctx_attn_kernel(na_ref, sw_ref, df_ref, sink_ref, lq1_ref, lk1_ref, lq2_ref, lk2_ref, g_ref,
                     nb_o, sw_o, df_o, *, lambda_init):
    n = na_ref.shape[0]
    na = na_ref[...]
    outs = []
    for h in range(NA_HEADS):
        sl = slice(h * HEAD_DIM, (h + 1) * HEAD_DIM)
        (p,), l = _softmax_parts([_nt_dot(na[:, sl], na[:, 256 + h * HEAD_DIM:256 + (h + 1) * HEAD_DIM])])
        outs.append(_dot(p.astype(BF16), na[:, 512 + h * HEAD_DIM:512 + (h + 1) * HEAD_DIM]) / l)
    nb_o[...] = jnp.concatenate(outs, axis=1).astype(BF16)
    sw = sw_ref[...]
    outs = []
    for hq in range(4):
        kv = hq // 2
        k = sw[:, 256 + kv * HEAD_DIM:256 + (kv + 1) * HEAD_DIM]
        v = sw[:, 384 + kv * HEAD_DIM:384 + (kv + 1) * HEAD_DIM]
        sk = jnp.broadcast_to(sink_ref[0:1, hq:hq + 1], (n, 1))
        (p,), l = _softmax_parts([_nt_dot(sw[:, hq * HEAD_DIM:(hq + 1) * HEAD_DIM], k)], extra=sk)
        outs.append(_dot(p.astype(BF16), v) / l)
    sw_o[...] = jnp.concatenate(outs, axis=1).astype(BF16)
    lam = _diff_lambda(lq1_ref, lk1_ref, lq2_ref, lk2_ref, lambda_init)
    df = df_ref[...]
    outs = []
    for h in range(DIFF_HEADS):
        sl = slice(h * HEAD_DIM, (h + 1) * HEAD_DIM)
        q2 = _stack_maps(df[:, sl])
        (p,), l = _softmax_parts([_nt_dot(q2, df[:, 256 + h * HEAD_DIM:256 + (h + 1) * HEAD_DIM])])
        o = _dot(p.astype(BF16), df[:, 512 + h * HEAD_DIM:512 + (h + 1) * HEAD_DIM]) / l
        outs.append(_subln(o[0:n], o[n:2 * n], lam, g_ref[...], lambda_init))
    df_o[...] = jnp.concatenate(outs, axis=1).astype(BF16)


def _ctx_attention(na_c, sw_c, df_c, sink, lqk, subln_g, lambda_init, nbatch, n_ctx):
    sink_pad = jnp.zeros((1, 128), F32).at[0, 0:4].set(sink.astype(F32))
    vec = pl.BlockSpec((1, DIFF_QK_DIM), lambda b: (0, 0))
    out = jax.ShapeDtypeStruct((nbatch * n_ctx, 256), BF16)
    return pl.pallas_call(
        functools.partial(_ctx_attn_kernel, lambda_init=lambda_init),
        grid=(nbatch,),
        in_specs=[pl.BlockSpec((n_ctx, 768), lambda b: (b, 0)),
                  pl.BlockSpec((n_ctx, 512), lambda b: (b, 0)),
                  pl.BlockSpec((n_ctx, 768), lambda b: (b, 0)),
                  pl.BlockSpec((1, 128), lambda b: (0, 0)),
                  vec, vec, vec, vec,
                  pl.BlockSpec((1, HEAD_DIM), lambda b: (0, 0))],
        out_specs=[pl.BlockSpec((n_ctx, 256), lambda b: (b, 0))] * 3,
        out_shape=[out, out, out],
        compiler_params=_params("arbitrary"),
        name="ctx_attention",
    )(na_c, sw_c, df_c, sink_pad, *lqk, subln_g.reshape(1, HEAD_DIM).astype(F32))


def _merge_kernel(u_ref, yf_ref, yr_ref, d_ref, gw_ref, gb_ref, yb_ref, yc_ref, yd_ref, gate_ref,
                  wb_ref, wo_ref, x_ref, g1_ref, n2_ref, sc2_ref, sh2_ref, rw_ref,
                  x1_o, h2_o, h2p_o, lg_o):
    y = u_ref[...] * d_ref[...] + yf_ref[0, 0] + yr_ref[0, 0]
    a = jax.nn.gelu(y, approximate=True)
    ya = a * jax.nn.sigmoid(_dot(a.astype(BF16), gw_ref[...]) + gb_ref[...])
    branches = (ya.astype(BF16), yb_ref[...], yc_ref[...], yd_ref[...])
    acc = None
    for i in range(4):
        t = gate_ref[:, i * D_MODEL:(i + 1) * D_MODEL].astype(F32) * _dot(branches[i], wb_ref[i])
        acc = t if acc is None else acc + t
    mixed = _dot(acc.astype(BF16), wo_ref[...])
    x1 = x_ref[...] + g1_ref[0] * mixed
    x1_o[...] = x1
    h2 = _rms(x1) * n2_ref[...]
    h2 = h2 * (1.0 + sc2_ref[0]) + sh2_ref[0]
    h2_o[...] = h2.astype(BF16)
    h2p_o[...] = _pack_rows(h2[:, 0:HALF_D], h2[:, HALF_D:D_MODEL])
    lg_o[...] = lax.dot_general(rw_ref[...], h2, (((1,), (1,)), ((), ())), preferred_element_type=F32,
                                precision=HIGHEST)


def _merge(u, y_bm, y_start, rows_per_seq, s5_d, glu_w, glu_b, yb, yc, yd, gate, wb, wo, x2d, g1, norm2_g, sc2, sh2,
           router_w, *, rows_per_mod):
    rows = x2d.shape[0]
    tm = min(512, rows_per_seq)
    assert rows % tm == 0 and rows_per_mod % tm == 0 and rows_per_seq % tm == 0 and y_start % tm == 0
    tiles_per_seq = rows_per_seq // tm

    def mod_map(i):
        return ((i * tm) // rows_per_mod, 0, 0)

    def y_spec(d):
        return pl.BlockSpec((1, 1, tm, 256), lambda i: (d, i // tiles_per_seq, y_start // tm + i % tiles_per_seq, 0))

    row = lambda w: pl.BlockSpec((tm, w), lambda i: (i, 0))
    full = lambda *shape: pl.BlockSpec(shape, lambda i: (0,) * len(shape))
    mod = pl.BlockSpec((1, 1, D_MODEL), mod_map)
    return pl.pallas_call(
        _merge_kernel,
        grid=(rows // tm,),
        in_specs=[row(256), y_spec(0), y_spec(1), full(1, 256), full(256, 256), full(1, 256),
                  row(256), row(256), row(256), row(GATE_WIDTH),
                  full(4, 256, D_MODEL), full(D_MODEL, D_MODEL), row(D_MODEL),
                  mod, full(1, D_MODEL), mod, mod, full(N_EXPERTS, D_MODEL)],
        out_specs=[row(D_MODEL), row(D_MODEL), row(HALF_D), pl.BlockSpec((N_EXPERTS, tm), lambda i: (0, i))],
        out_shape=[jax.ShapeDtypeStruct((rows, D_MODEL), F32),
                   jax.ShapeDtypeStruct((rows, D_MODEL), BF16),
                   jax.ShapeDtypeStruct((rows, HALF_D), jnp.int32),
                   jax.ShapeDtypeStruct((N_EXPERTS, rows), F32)],
        compiler_params=_params("arbitrary"),
        name="merge",
    )(u, y_bm, y_bm, s5_d.reshape(1, 256).astype(F32), glu_w.astype(BF16), glu_b.reshape(1, 256).astype(F32),
      yb, yc, yd, gate, wb.astype(BF16), wo.astype(BF16), x2d, g1, norm2_g.reshape(1, D_MODEL), sc2, sh2,
      router_w.astype(F32).T)


def _router_kernel(lg_ref, b_ref, tri_ref, idx_ref, rank_ref, w_ref, cnt_ref, base_ref):
    tr = lg_ref.shape[1]
    gsz = N_EXPERTS // N_EXPERT_GROUPS
    sc = jax.nn.sigmoid(lg_ref[...])
    bi = sc + b_ref[...]
    e_iota = lax.broadcasted_iota(jnp.int32, (gsz, tr), 0).astype(F32)
    groups = [bi[g * gsz:(g + 1) * gsz] for g in range(N_EXPERT_GROUPS)]
    gs = []
    for bg in groups:
        m1 = bg.max(axis=0, keepdims=True)
        i1 = jnp.where(bg == m1, e_iota, float(gsz)).min(axis=0, keepdims=True)
        m2 = jnp.where(e_iota == i1, -jnp.inf, bg).max(axis=0, keepdims=True)
        gs.append(m1 + m2)
    v = []
    for g in range(N_EXPERT_GROUPS):
        rank = jnp.zeros((1, tr), F32)
        for g2 in range(N_EXPERT_GROUPS):
            if g2 == g:
                continue
            beats = (gs[g2] >= gs[g]) if g2 < g else (gs[g2] > gs[g])
            rank = rank + jnp.where(beats, 1.0, 0.0)
        v.append(jnp.where(rank < TOPK_GROUPS, groups[g], NEG_INF))
    flat = [e_iota + float(g * gsz) for g in range(N_EXPERT_GROUPS)]
    sel = [jnp.zeros((gsz, tr), F32) for _ in range(N_EXPERT_GROUPS)]
    picks = []
    for _ in range(TOP_K):
        m = v[0].max(axis=0, keepdims=True)
        for g in range(1, N_EXPERT_GROUPS):
            m = jnp.maximum(m, v[g].max(axis=0, keepdims=True))
        am = jnp.where(v[0] == m, flat[0], float(N_EXPERTS)).min(axis=0, keepdims=True)
        for g in range(1, N_EXPERT_GROUPS):
            am = jnp.minimum(am, jnp.where(v[g] == m, flat[g], float(N_EXPERTS)).min(axis=0, keepdims=True))
        hits = []
        for g in range(N_EXPERT_GROUPS):
            hit = flat[g] == am
            hits.append(hit)
            sel[g] = jnp.where(hit, 1.0, sel[g])
            v[g] = jnp.where(hit, -jnp.inf, v[g])
        picks.append((am, hits))
    scg = [sc[g * gsz:(g + 1) * gsz] for g in range(N_EXPERT_GROUPS)]
    den = (sel[0] * scg[0]).sum(axis=0, keepdims=True)
    for g in range(1, N_EXPERT_GROUPS):
        den = den + (sel[g] * scg[g]).sum(axis=0, keepdims=True)

    @pl.when(pl.program_id(0) == 0)
    def _():
        base_ref[...] = jnp.zeros_like(base_ref)

    sel_all = jnp.concatenate(sel, axis=0)
    before = _dot(sel_all.astype(jnp.bfloat16), tri_ref[...]) + base_ref[...]
    for k, (am, hits) in enumerate(picks):
        wk = jnp.zeros((1, tr), F32)
        rk = jnp.zeros((1, tr), F32)
        for g in range(N_EXPERT_GROUPS):
            wk = wk + jnp.where(hits[g], scg[g], 0.0).sum(axis=0, keepdims=True)
            rk = rk + jnp.where(hits[g], before[g * gsz:(g + 1) * gsz], 0.0).sum(axis=0, keepdims=True)
        idx_ref[k:k + 1, :] = am.astype(jnp.int32)
        rank_ref[k:k + 1, :] = rk.astype(jnp.int32)
        w_ref[k:k + 1, :] = wk / den * ROUTED_SCALE
    idx_ref[TOP_K:8, :] = jnp.zeros((8 - TOP_K, tr), jnp.int32)
    rank_ref[TOP_K:8, :] = jnp.zeros((8 - TOP_K, tr), jnp.int32)
    w_ref[TOP_K:8, :] = jnp.zeros((8 - TOP_K, tr), F32)
    base_ref[...] += sel_all.sum(axis=1, keepdims=True)
    cnt_ref[...] = base_ref[...].astype(jnp.int32)


ROUTER_TILE = 512


def _router(logits_t, router_b):
    ne, rows = logits_t.shape
    tr = ROUTER_TILE
    assert rows % tr == 0
    tri = jnp.asarray(np.triu(np.ones((tr, tr), np.float32), k=1), jnp.bfloat16)
    pick = pl.BlockSpec((8, tr), lambda i: (0, i))
    return pl.pallas_call(
        _router_kernel,
        grid=(rows // tr,),
        in_specs=[pl.BlockSpec((ne, tr), lambda i: (0, i)),
                  pl.BlockSpec((ne, 1), lambda i: (0, 0)),
                  pl.BlockSpec((tr, tr), lambda i: (0, 0))],
        out_specs=[pick, pick, pick, pl.BlockSpec((ne, 1), lambda i: (0, 0))],
        out_shape=[jax.ShapeDtypeStruct((8, rows), jnp.int32),
                   jax.ShapeDtypeStruct((8, rows), jnp.int32),
                   jax.ShapeDtypeStruct((8, rows), F32),
                   jax.ShapeDtypeStruct((ne, 1), jnp.int32)],
        scratch_shapes=[pltpu.VMEM((ne, 1), F32)],
        compiler_params=_params("arbitrary"),
        name="router",
    )(logits_t, router_b.reshape(ne, 1).astype(F32), tri)


MOE_BLOCK = 512
MOE_TOKENS = 512
MOE_COMBINE_GROUPS = 4
HALF_D = D_MODEL // 2


def _pack_rows(lo, hi):
    lo_b = pltpu.bitcast(lo.astype(jnp.bfloat16).astype(F32), jnp.uint32)
    hi_b = pltpu.bitcast(hi.astype(jnp.bfloat16).astype(F32), jnp.uint32)
    return pltpu.bitcast((hi_b & jnp.uint32(0xFFFF0000)) | (lo_b >> 16), jnp.int32)


def _unpack_rows(words):
    u = pltpu.bitcast(words, jnp.uint32)
    lo = pltpu.bitcast(u << 16, F32)
    hi = pltpu.bitcast(u & jnp.uint32(0xFFFF0000), F32)
    return lo, hi


def _swiglu(x_bf16, wgu, wd):
    hgu = _dot(x_bf16, wgu)
    g = hgu[:, 0:EXPERT_HIDDEN]
    a = g * jax.nn.sigmoid(g) * hgu[:, EXPERT_HIDDEN:2 * EXPERT_HIDDEN]
    return _dot(a.astype(BF16), wd)


SC_CORES = 2
SC_SUBCORES = 16
SC_STREAM_ROWS = 128


def _sc_for_each_chunk(total, body):
    chunk = SC_STREAM_ROWS
    assert total % chunk == 0
    nchunk = total // chunk
    per_worker = pl.cdiv(nchunk, SC_CORES * SC_SUBCORES)
    first = (lax.axis_index("s") * SC_CORES + lax.axis_index("c")) * per_worker

    @pl.loop(0, per_worker)
    def _(j):
        @pl.when(first + j < nchunk)
        def _():
            body((first + j) * chunk)


def _sc_scatter_rows(rows, slots, n_out):
    total, n = rows.shape
    chunk = SC_STREAM_ROWS
    mesh = plsc.VectorSubcoreMesh(core_axis_name="c", subcore_axis_name="s")

    @functools.partial(
        pl.kernel, mesh=mesh,
        out_type=jax.ShapeDtypeStruct((n_out, n), jnp.int32),
        scratch_types=[pltpu.VMEM((8, chunk), jnp.int32),
                       pltpu.VMEM((chunk, n), jnp.int32),
                       pltpu.SemaphoreType.DMA],
        name="moe_dispatch_sc",
    )
    def scatter(rows_hbm, slot_hbm, out_hbm, idx_v, rows_v, sem):
        def body(off):
            pltpu.sync_copy(rows_hbm.at[pl.ds(off, chunk)], rows_v)
            pltpu.sync_copy(slot_hbm.at[:, pl.ds(off, chunk)], idx_v)
            for k in range(TOP_K):
                pltpu.async_copy(rows_v, out_hbm.at[idx_v.at[k]], sem).wait()

        _sc_for_each_chunk(total, body)

    return scatter(rows, slots)


def _experts_kernel(be_ref, nv_ref, nb_ref, xs_ref, wg_ref, wu_ref, wd_ref, ys_ref, wgu_bf, wd_bf):
    b = pl.program_id(0)

    @pl.when(b < nb_ref[0])
    def _():
        @pl.when((b == 0) | (be_ref[b] != be_ref[jnp.maximum(b - 1, 0)]))
        def _():
            wgu_bf[:, 0:EXPERT_HIDDEN] = wg_ref[0, 0].astype(BF16)
            wgu_bf[:, EXPERT_HIDDEN:2 * EXPERT_HIDDEN] = wu_ref[0, 0].astype(BF16)
            wd_bf[...] = wd_ref[0, 0].astype(BF16)

        words = xs_ref[...]
        live = lax.broadcasted_iota(jnp.int32, words.shape, 0) < nv_ref[b]
        lo, hi = _unpack_rows(jnp.where(live, words, 0))
        x = jnp.concatenate([lo, hi], axis=1).astype(BF16)
        y = _swiglu(x, wgu_bf[...], wd_bf[...])
        ys_ref[...] = _pack_rows(y[:, 0:HALF_D], y[:, HALF_D:D_MODEL])


def _sc_gather_rows(table, indices):
    m, n = indices.shape[0], table.shape[1]
    chunk = SC_STREAM_ROWS
    mesh = plsc.VectorSubcoreMesh(core_axis_name="c", subcore_axis_name="s")

    @functools.partial(
        pl.kernel, mesh=mesh,
        out_type=jax.ShapeDtypeStruct((m, n), jnp.int32),
        scratch_types=[pltpu.VMEM((chunk,), jnp.int32),
                       pltpu.VMEM((chunk, n), jnp.int32),
                       pltpu.SemaphoreType.DMA],
        name="moe_gather_sc",
    )
    def gather(table_hbm, idx_hbm, out_hbm, idx_v, rows_v, sem):
        def body(off):
            pltpu.sync_copy(idx_hbm.at[pl.ds(off, chunk)], idx_v)
            pltpu.async_copy(table_hbm.at[idx_v], rows_v, sem).wait()
            pltpu.sync_copy(rows_v, out_hbm.at[pl.ds(off, chunk)])

        _sc_for_each_chunk(m, body)

    return gather(table, indices)


def _combine_into_kernel(w_ref, rows_ref, h_ref, wsgu_ref, wsd_ref, x1_ref, g2_ref, fg_ref, prev_ref, o_ref, *, final):
    del prev_ref
    _combine_kernel(w_ref, rows_ref, h_ref, wsgu_ref, wsd_ref, x1_ref, g2_ref, fg_ref, o_ref, final=final)


def _combine_kernel(w_ref, rows_ref, h_ref, wsgu_ref, wsd_ref, x1_ref, g2_ref, fg_ref, o_ref, *, final):
    shared = _swiglu(h_ref[...], wsgu_ref[...], wsd_ref[...])
    acc_lo = shared[:, 0:HALF_D]
    acc_hi = shared[:, HALF_D:D_MODEL]
    w = w_ref[...]
    for k in range(TOP_K):
        lo, hi = _unpack_rows(rows_ref[k])
        acc_lo = acc_lo + w[:, k:k + 1] * lo
        acc_hi = acc_hi + w[:, k:k + 1] * hi
    x2 = x1_ref[...] + g2_ref[0] * jnp.concatenate([acc_lo, acc_hi], axis=1)
    if final:
        x2 = _rms(x2) * fg_ref[...]
    o_ref[...] = x2


def _moe(h2, h2p, picks, lp, x1, g2, final_g, *, rows_per_mod, final):
    idx, rank, wsel, counts = picks
    rows = h2.shape[0]
    tt = MOE_TOKENS
    blk = MOE_BLOCK
    assert rows % tt == 0 and rows_per_mod % tt == 0 and (rows * TOP_K) % blk == 0
    ntile = rows // tt
    nblock = rows * TOP_K // blk + N_EXPERTS

    cnt = counts.reshape(N_EXPERTS)
    padded = (cnt + blk - 1) // blk * blk
    e_ids = jnp.arange(N_EXPERTS, dtype=jnp.int32)
    pends = jnp.sum(jnp.where(e_ids[None, :] <= e_ids[:, None], padded[None, :], 0), axis=1)
    pstart = (pends - padded).astype(jnp.int32)
    nb_used = (jnp.sum(padded) // blk).astype(jnp.int32).reshape(1)
    first_row = jnp.arange(nblock, dtype=jnp.int32) * blk
    block_e = jnp.minimum(jnp.sum(pends[None, :] <= first_row[:, None], axis=1), N_EXPERTS - 1).astype(jnp.int32)
    slot = rank + jnp.sum(jnp.where(idx[:, :, None] == jnp.arange(N_EXPERTS, dtype=jnp.int32), pstart, 0), axis=-1)
    seg_end = jnp.sum(jnp.where(block_e[:, None] == e_ids[None, :], (cnt + pstart)[None, :], 0), axis=1)
    n_valid = jnp.clip(seg_end - first_row, 0, blk).astype(jnp.int32)

    xs = _sc_scatter_rows(h2p, slot, nblock * blk)

    def blk_map(b, be, nv, nb):
        return (jnp.minimum(b, nb[0] - 1), 0)

    layer = lp['layer']

    def w_map(b, be, nv, nb):
        return (layer, be[jnp.minimum(b, nb[0] - 1)], 0, 0)

    ys = pl.pallas_call(
        _experts_kernel,
        grid_spec=pltpu.PrefetchScalarGridSpec(
            num_scalar_prefetch=3,
            grid=(nblock,),
            in_specs=[pl.BlockSpec((blk, HALF_D), blk_map),
                      pl.BlockSpec((1, 1, D_MODEL, EXPERT_HIDDEN), w_map),
                      pl.BlockSpec((1, 1, D_MODEL, EXPERT_HIDDEN), w_map),
                      pl.BlockSpec((1, 1, EXPERT_HIDDEN, D_MODEL), w_map)],
            out_specs=pl.BlockSpec((blk, HALF_D), blk_map),
            scratch_shapes=[pltpu.VMEM((D_MODEL, 2 * EXPERT_HIDDEN), BF16),
                            pltpu.VMEM((EXPERT_HIDDEN, D_MODEL), BF16)]),
        out_shape=jax.ShapeDtypeStruct((nblock * blk, HALF_D), jnp.int32),
        compiler_params=_params("arbitrary"),
        name="moe_experts",
    )(block_e, n_valid, nb_used, xs, lp['exp_w_gate'], lp['exp_w_up'], lp['exp_w_down'])

    wsgu = jnp.concatenate([lp['sh_w_gate'], lp['sh_w_up']], axis=1).astype(BF16)
    wsd = lp['sh_w_down'].astype(BF16)
    ngroup = MOE_COMBINE_GROUPS if ntile % MOE_COMBINE_GROUPS == 0 and ntile >= 4 * MOE_COMBINE_GROUPS else 1
    gtile = ntile // ngroup
    grows = gtile * tt
    full = lambda *shape: pl.BlockSpec(shape, lambda i: (0,) * len(shape))
    wsel_t = wsel.T
    fg = final_g.reshape(1, D_MODEL).astype(F32)
    out = None
    for p in range(ngroup):
        gathered = _sc_gather_rows(ys, slot[0:TOP_K, p * grows:(p + 1) * grows].reshape(TOP_K * grows))
        gathered = gathered.reshape(TOP_K, grows, HALF_D)
        row = lambda width, p=p: pl.BlockSpec((tt, width), lambda i: (p * gtile + i, 0))
        in_specs = [row(8), pl.BlockSpec((TOP_K, tt, HALF_D), lambda i: (0, i, 0)), row(D_MODEL),
                    full(D_MODEL, 2 * EXPERT_HIDDEN), full(EXPERT_HIDDEN, D_MODEL), row(D_MODEL),
                    pl.BlockSpec((1, 1, D_MODEL), lambda i, p=p: (((p * gtile + i) * tt) // rows_per_mod, 0, 0)),
                    full(1, D_MODEL)]
        args = [wsel_t, gathered, h2, wsgu, wsd, x1, g2, fg]
        kern = functools.partial(_combine_kernel, final=final)
        aliases = {}
        if out is not None:
            in_specs.append(pl.BlockSpec(memory_space=pl.ANY))
            args.append(out)
            aliases = {len(args) - 1: 0}
            kern = functools.partial(_combine_into_kernel, final=final)
        out = pl.pallas_call(
            kern,
            grid=(gtile,),
            in_specs=in_specs,
            out_specs=row(D_MODEL),
            out_shape=jax.ShapeDtypeStruct((rows, D_MODEL), F32),
            input_output_aliases=aliases,
            compiler_params=_params("arbitrary"),
            name="moe_combine",
        )(*args)
    return out


def _reorder_w_in(w_in):
    split = 256 + 768 + 512 + 768
    return jnp.concatenate([w_in[:, split:], w_in[:, :split]], axis=1).astype(BF16)


def _mods(mod_row_block):
    return [mod_row_block[:, None, k * D_MODEL:(k + 1) * D_MODEL] for k in range(6)]


def _moe_block(h2, h2p, logits, lp, x1, g2, final_g, *, rows_per_mod, final):
    picks = _router(logits, lp['router_b'])
    return _moe(h2, h2p, picks, lp, x1, g2, final_g, rows_per_mod=rows_per_mod, final=final)


def _layer(x2d, xc2d, c16, lp, layer_idx, tables, final_g, *, nbatch, seq, n_ctx, with_ctx_out, final):
    lambda_init = 0.8 - 0.6 * math.exp(-0.3 * layer_idx)
    mod = _ada_mod(c16, lp['ada_w'].astype(F32), lp['ada_b'].astype(F32))
    sh1, sc1, g1, sh2, sc2, g2 = _mods(mod[0:nbatch])
    csh1, csc1, cg1, csh2, csc2, cg2 = _mods(mod[nbatch:nbatch + 1])
    w_in = _reorder_w_in(lp['w_in'])
    rows_lat = nbatch * seq
    rows_ctx = nbatch * n_ctx

    (gate, u, na, sw, df, na_qt, sw_qt, df_qt, na_vt, sw_vt, df_k4, df_vt4) = _inproj(
        x2d, lp['norm1_g'], sc1, sh1, w_in, tables, rows_per_mod=seq, rope=True, seq=seq)
    gate_c, u_c, na_c, sw_c, df_c = _inproj(xc2d, lp['norm1_g'], csc1, csh1, w_in, tables,
                                            rows_per_mod=rows_ctx, rope=False, seq=seq)

    win, wout, a_re, a_im = _s5_params(lp['s5_lambda_re'], lp['s5_lambda_im'], lp['s5_log_step'],
                                       lp['s5_b_re'], lp['s5_b_im'], lp['s5_c_re'], lp['s5_c_im'])
    u_tm = jnp.concatenate([u_c.reshape(nbatch, n_ctx, 256).transpose(1, 0, 2),
                            u.reshape(nbatch, seq, 256).transpose(1, 0, 2)], axis=0)
    y_tm = _s5_scan(u_tm, win, wout, a_re, a_im, n_ctx)
    y_bm = y_tm.transpose(0, 2, 1, 3)

    lqk = [lp[k].reshape(1, DIFF_QK_DIM).astype(F32) for k in ('diff_lq1', 'diff_lk1', 'diff_lq2', 'diff_lk2')]
    bias = _na_bias_table(lp['na_rpb'], seq // GRID_W)
    yb, yc = _local_attention(na_qt, na_vt, na, na_c, bias, sw_qt, sw_vt, sw, sw_c, lp['swa_sink'], nbatch, seq, n_ctx)
    yd = _diff_attention(df_qt, df_k4, df_vt4, df_c, lqk, lp['diff_subln_g'], lambda_init, nbatch, seq, n_ctx)

    merge_w = (lp['s5_d'], lp['s5_glu_w'], lp['s5_glu_b'])
    x1, h2, h2p, logits = _merge(u, y_bm, 0, seq, *merge_w, yb, yc, yd, gate, lp['w_branch'], lp['w_out'], x2d, g1,
                                 lp['norm2_g'], sc2, sh2, lp['router_w'], rows_per_mod=seq)
    x_out = _moe_block(h2, h2p, logits, lp, x1, g2, final_g, rows_per_mod=seq, final=final)

    xc_out = None
    if with_ctx_out:
        yb_c, yc_c, yd_c = _ctx_attention(na_c, sw_c, df_c, lp['swa_sink'], lqk, lp['diff_subln_g'],
                                          lambda_init, nbatch, n_ctx)
        x1c, h2c, h2pc, logits_c = _merge(u_c, y_bm, seq, n_ctx, *merge_w, yb_c, yc_c, yd_c, gate_c,
                                          lp['w_branch'], lp['w_out'], xc2d, cg1, lp['norm2_g'], csc2, csh2,
                                          lp['router_w'], rows_per_mod=rows_ctx)
        xc_out = _moe_block(h2c, h2pc, logits_c, lp, x1c, cg2, final_g, rows_per_mod=rows_ctx, final=False)
    return x_out, xc_out


def kernel(x, c, ctx, c_ctx, ada_w, ada_b, norm1_g, norm2_g, w_in, s5_lambda_re, s5_lambda_im, s5_log_step,
           s5_b_re, s5_b_im, s5_c_re, s5_c_im, s5_d, s5_glu_w, s5_glu_b, na_rpb, swa_sink, diff_lq1, diff_lk1,
           diff_lq2, diff_lk2, diff_subln_g, w_branch, w_out, router_w, router_b, exp_w_gate, exp_w_up,
           exp_w_down, sh_w_gate, sh_w_up, sh_w_down, final_g):
    nbatch, seq, d = x.shape
    n_ctx = ctx.shape[1]
    depth = ada_w.shape[0]
    assert d == D_MODEL and nbatch == 8
    stacked = dict(ada_w=ada_w, ada_b=ada_b, norm1_g=norm1_g, norm2_g=norm2_g, w_in=w_in,
                   s5_lambda_re=s5_lambda_re, s5_lambda_im=s5_lambda_im, s5_log_step=s5_log_step,
                   s5_b_re=s5_b_re, s5_b_im=s5_b_im, s5_c_re=s5_c_re, s5_c_im=s5_c_im, s5_d=s5_d,
                   s5_glu_w=s5_glu_w, s5_glu_b=s5_glu_b, na_rpb=na_rpb, swa_sink=swa_sink,
                   diff_lq1=diff_lq1, diff_lk1=diff_lk1, diff_lq2=diff_lq2, diff_lk2=diff_lk2,
                   diff_subln_g=diff_subln_g, w_branch=w_branch, w_out=w_out, router_w=router_w,
                   router_b=router_b, exp_w_gate=exp_w_gate, exp_w_up=exp_w_up, exp_w_down=exp_w_down,
                   sh_w_gate=sh_w_gate, sh_w_up=sh_w_up, sh_w_down=sh_w_down)
    tables = _rope_tables(seq)
    c16 = jnp.concatenate([c.astype(F32), c_ctx.reshape(1, d).astype(F32),
                           jnp.zeros((16 - nbatch - 1, d), F32)], axis=0)
    x2d = x.reshape(nbatch * seq, d).astype(F32)
    xc2d = ctx.reshape(nbatch * n_ctx, d).astype(F32)
    for l in range(depth):
        routed = ('exp_w_gate', 'exp_w_up', 'exp_w_down')
        lp = {k: (v.astype(F32) if k in routed else v[l]) for k, v in stacked.items()}
        lp['layer'] = l
        last = l == depth - 1
        x2d, xc2d = _layer(x2d, xc2d, c16, lp, l, tables, final_g, nbatch=nbatch, seq=seq, n_ctx=n_ctx,
                           with_ctx_out=not last, final=last)
    return x2d.reshape(nbatch, seq, d)
```
